```python
import math
import jax
import jax.numpy as jnp
from jax import lax
import numpy as np

D_MODEL = 1024
BATCH = 8
SEQ = 2048
DEPTH = 1

GRID_W = 64
CTX_LEN = 256

RET_HEADS = 4
RET_DV = D_MODEL // (2 * RET_HEADS)
RET_DK = RET_DV // 2
HG_HEADS = 4
HG_DV = D_MODEL // (2 * HG_HEADS)
HG_DK = HG_DV
RET_QK_W = RET_HEADS * RET_DK
RET_V_W = RET_HEADS * RET_DV
HG_K_W = HG_HEADS * HG_DK
HG_V_W = HG_HEADS * HG_DV
PROJ_WIDTHS = (RET_QK_W, RET_QK_W, RET_V_W, RET_V_W, HG_K_W, HG_K_W, HG_K_W, HG_V_W, HG_V_W)
PROJ_W = sum(PROJ_WIDTHS)

CHUNK = 64
ROPE_BASE = 10000.0
EPS = 1e-6

N_EXPERTS = 32
TOP_K = 4
D_FF = D_MODEL
SWIGLU_LIMIT = 7.0
SWIGLU_ALPHA = 1.702
MOE_BLOCK = 128

kernel_name = "hybrid_retention_hgrn2_moe_dit"


def _rms(x):
    xf = x.astype(jnp.float32)
    return xf * lax.rsqrt(jnp.mean(xf * xf, axis=-1, keepdims=True) + EPS)


def rms_norm(x, w):
    return _rms(x) * w.astype(jnp.float32)


def to_heads(t, n_heads):
    b, n, _ = t.shape
    return t.reshape(b, n, n_heads, -1).transpose(0, 2, 1, 3)


def from_heads(t):
    b, h, n, d = t.shape
    return t.transpose(0, 2, 1, 3).reshape(b, n, h * d)


def axial_rope_tables(row, col):
    n_freq = RET_DK // 4
    inv = ROPE_BASE ** (-jnp.arange(n_freq, dtype=jnp.float32) / n_freq)
    ang = jnp.concatenate([row[:, None] * inv, col[:, None] * inv], axis=-1)
    return jnp.cos(ang), jnp.sin(ang)


def apply_rope(t, cos, sin):
    half = t.shape[-1] // 2
    t1, t2 = t[..., :half], t[..., half:]
    return jnp.concatenate([t1 * cos - t2 * sin, t1 * sin + t2 * cos], axis=-1)


def chunk_recurrence(q, k, v, log_a, s0):
    B, H, N, dk = q.shape
    dv = v.shape[-1]
    nc = N // CHUNK

    def to_chunks(t):
        return t.astype(jnp.float32).reshape(B, H, nc, CHUNK, t.shape[-1]).transpose(2, 0, 1, 3, 4)

    causal = jnp.tril(jnp.ones((CHUNK, CHUNK), dtype=bool))[:, :, None]

    def step(state, blk):
        qc, kc, vc, la = blk
        b = jnp.cumsum(la, axis=2)
        rel = b[:, :, :, None, :] - b[:, :, None, :, :]
        decay = jnp.exp(jnp.where(causal, rel, -jnp.inf))
        if la.shape[-1] == 1:
            scores = jnp.einsum('bhtd,bhsd->bhts', qc, kc) * decay[..., 0]
        else:
            scores = jnp.einsum('bhtsd,bhsd->bhts', qc[:, :, :, None, :] * decay, kc)
        out = (jnp.einsum('bhts,bhsv->bhtv', scores, vc)
               + jnp.einsum('bhtd,bhdv->bhtv', qc * jnp.exp(b), state))
        b_end = b[:, :, -1:, :]
        new_state = (jnp.exp(b_end[:, :, 0, :])[..., None] * state
                     + jnp.einsum('bhsd,bhsv->bhdv', kc * jnp.exp(b_end - b), vc))
        return new_state, out

    state, out = lax.scan(step, s0, (to_chunks(q), to_chunks(k), to_chunks(v), to_chunks(log_a)))
    return out.transpose(1, 2, 0, 3, 4).reshape(B, H, N, dv), state


def _flip(t):
    return jnp.flip(t, axis=2)


def bidirectional_scan(c_fwd, c_bwd, x_fwd, x_bwd):
    B, H, _, dk = c_fwd[0].shape
    dv = c_fwd[2].shape[-1]
    s0 = jnp.zeros((B, H, dk, dv), jnp.float32)
    oc_f, sc_f = chunk_recurrence(*c_fwd, s0)
    oc_b, sc_b = chunk_recurrence(*tuple(_flip(t) for t in c_bwd), s0)
    ox_f, _ = chunk_recurrence(*x_fwd, sc_f)
    ox_b, _ = chunk_recurrence(*tuple(_flip(t) for t in x_bwd), sc_b)
    return ox_f + _flip(ox_b), oc_f + _flip(oc_b)


def hybrid_mixer(u_x, u_c, w_in_l, w_out_l, lb_l, hg_norm_w_l, cos, sin, ctx_out):
    B, N, _ = u_x.shape
    C = u_c.shape[1]
    split_pts = [int(p) for p in np.cumsum(PROJ_WIDTHS)[:-1]]
    px = jnp.split(u_x @ w_in_l, split_pts, axis=-1)
    pc = jnp.split(u_c @ w_in_l, split_pts, axis=-1)

    hidx = jnp.arange(RET_HEADS, dtype=jnp.float32)
    lg_f = jnp.log1p(-(2.0 ** (-5.0 - 2.0 * hidx)))
    lg_b = jnp.log1p(-(2.0 ** (-6.0 - 2.0 * hidx)))

    def ret_qkv(p, rope):
        q = to_heads(p[0], RET_HEADS)
        k = to_heads(p[1], RET_HEADS) * (RET_DK ** -0.5)
        if rope:
            q = apply_rope(q, cos, sin)
            k = apply_rope(k, cos, sin)
        return q, k, to_heads(p[2], RET_HEADS)

    def dec(lg, n):
        return jnp.broadcast_to(lg[None, :, None, None], (B, RET_HEADS, n, 1))

    rqx, rkx, rvx = ret_qkv(px, True)
    rqc, rkc, rvc = ret_qkv(pc, False)
    ret_x, ret_c = bidirectional_scan((rqc, rkc, rvc, dec(lg_f, C)), (rqc, rkc, rvc, dec(lg_b, C)),
                                      (rqx, rkx, rvx, dec(lg_f, N)), (rqx, rkx, rvx, dec(lg_b, N)))

    def hg_dir(p, z, lb):
        q = jax.nn.silu(to_heads(p[4], HG_HEADS))
        lbh = lb.reshape(1, HG_HEADS, 1, HG_DK)
        f = lbh + (1.0 - lbh) * jax.nn.sigmoid(to_heads(z, HG_HEADS).astype(jnp.float32))
        return q, 1.0 - f, to_heads(p[7], HG_HEADS), jnp.log(f)

    hg_x, hg_c = bidirectional_scan(hg_dir(pc, pc[5], lb_l[0]), hg_dir(pc, pc[6], lb_l[1]),
                                    hg_dir(px, px[5], lb_l[0]), hg_dir(px, px[6], lb_l[1]))

    def merge(o_ret, o_hg, p):
        a = from_heads(_rms(o_ret)) * jax.nn.silu(p[3])
        b = from_heads(rms_norm(o_hg, hg_norm_w_l)) * jax.nn.silu(p[8])
        return jnp.concatenate([a, b], axis=-1) @ w_out_l

    y_x = merge(ret_x, hg_x, px)
    if ctx_out:
        return y_x, merge(ret_c, hg_c, pc)
    return y_x, None


def moe_ffn(h, router_w, router_b, w1, b1, w2, b2):
    T, D = h.shape
    logits = (h @ router_w + router_b).astype(jnp.float32)
    top_logit, top_idx = lax.top_k(logits, TOP_K)
    gates = jax.nn.softmax(top_logit, axis=-1)
    A = T * TOP_K
    flat_e = top_idx.reshape(A)
    flat_tok = jnp.arange(A, dtype=jnp.int32) // TOP_K
    order = jnp.argsort(flat_e)
    se, stok, sg = flat_e[order], flat_tok[order], gates.reshape(A)[order]
    counts = jnp.bincount(flat_e, length=N_EXPERTS)
    padded = (counts + MOE_BLOCK - 1) // MOE_BLOCK * MOE_BLOCK
    start = jnp.cumsum(counts) - counts
    pend = jnp.cumsum(padded)
    pstart = pend - padded
    dest = pstart[se] + jnp.arange(A, dtype=jnp.int32) - start[se]
    n_blocks = -(-A // MOE_BLOCK) + N_EXPERTS
    R = n_blocks * MOE_BLOCK
    row_tok = jnp.full((R,), T, jnp.int32).at[dest].set(stok)
    row_gate = jnp.zeros((R,), jnp.float32).at[dest].set(sg)
    block_e = jnp.minimum(jnp.searchsorted(pend, jnp.arange(n_blocks, dtype=jnp.int32) * MOE_BLOCK,
                                           side='right'), N_EXPERTS - 1)
    h_pad = jnp.concatenate([h, jnp.zeros((1, D), h.dtype)], axis=0)
    xb = h_pad[row_tok].reshape(n_blocks, MOE_BLOCK, D)

    def run_block(args):
        xblk, e = args
        hu = xblk @ w1[e] + b1[e]
        x_glu = jnp.minimum(hu[:, :D_FF], SWIGLU_LIMIT)
        x_lin = jnp.clip(hu[:, D_FF:], -SWIGLU_LIMIT, SWIGLU_LIMIT)
        act = x_glu * jax.nn.sigmoid(SWIGLU_ALPHA * x_glu) * (x_lin + 1.0)
        return act @ w2[e] + b2[e]

    yb = lax.map(run_block, (xb, block_e))
    y = jnp.zeros((T + 1, D), jnp.float32).at[row_tok].add(yb.reshape(R, D) * row_gate[:, None])
    return y[:T]


def setup_inputs(seed: int = 0) -> dict:
    key = jax.random.key(seed)
    ks = jax.random.split(key, 20)
    f32 = jnp.float32
    sd = D_MODEL ** -0.5
    nrm = lambda k, s: jax.random.normal(k, s, f32)
    return {
        "x": nrm(ks[0], (BATCH, SEQ, D_MODEL)),
        "c": nrm(ks[1], (BATCH, D_MODEL)),
        "ctx": nrm(ks[2], (BATCH, CTX_LEN, D_MODEL)),
        "c_ctx": nrm(ks[3], (D_MODEL,)),
        "w_ada": nrm(ks[4], (DEPTH, D_MODEL, 6 * D_MODEL)) * sd,
        "b_ada": nrm(ks[5], (DEPTH, 6 * D_MODEL)) * 0.02,
        "norm_mix_w": 1.0 + 0.02 * nrm(ks[6], (DEPTH, D_MODEL)),
        "norm_ffn_w": 1.0 + 0.02 * nrm(ks[7], (DEPTH, D_MODEL)),
        "w_in": nrm(ks[8], (DEPTH, D_MODEL, PROJ_W)) * sd,
        "w_out": nrm(ks[9], (DEPTH, D_MODEL, D_MODEL)) * sd,
        "hg_lb": nrm(ks[10], (DEPTH + 1, 2, HG_K_W)) * 0.5,
        "hg_norm_w": 1.0 + 0.02 * nrm(ks[11], (DEPTH, HG_DV)),
        "router_w": nrm(ks[12], (DEPTH, D_MODEL, N_EXPERTS)) * sd,
        "router_b": nrm(ks[13], (DEPTH, N_EXPERTS)) * 0.01,
        "w1": nrm(ks[14], (DEPTH, N_EXPERTS, D_MODEL, 2 * D_FF)) * sd,
        "b1": nrm(ks[15], (DEPTH, N_EXPERTS, 2 * D_FF)) * 0.02,
        "w2": nrm(ks[16], (DEPTH, N_EXPERTS, D_FF, D_MODEL)) * (D_FF ** -0.5),
        "b2": nrm(ks[17], (DEPTH, N_EXPERTS, D_MODEL)) * 0.02,
        "norm_final_w": 1.0 + 0.02 * nrm(ks[18], (D_MODEL,)),
    }


def reference(x, c, ctx, c_ctx, w_ada, b_ada, norm_mix_w, norm_ffn_w, w_in, w_out, hg_lb,
              hg_norm_w, router_w, router_b, w1, b1, w2, b2, norm_final_w):
    B, N, D = x.shape
    C = ctx.shape[1]
    ROWS = N // GRID_W
    row = jnp.repeat(jnp.arange(ROWS, dtype=jnp.float32), GRID_W)
    col = jnp.tile(jnp.arange(GRID_W, dtype=jnp.float32), ROWS)
    cos, sin = axial_rope_tables(row, col)
    lb_all = jnp.cumsum(jax.nn.softmax(hg_lb.astype(jnp.float32), axis=0), axis=0)
    s_c = jax.nn.silu(c.astype(jnp.float32))
    s_cc = jax.nn.silu(c_ctx.astype(jnp.float32))
    h_lat, h_ctx = x, ctx
    for l in range(DEPTH):
        last = l == DEPTH - 1
        sh1, sc1, g1, sh2, sc2, g2 = jnp.split(s_c @ w_ada[l] + b_ada[l], 6, axis=-1)
        csh1, csc1, cg1, csh2, csc2, cg2 = jnp.split(s_cc @ w_ada[l] + b_ada[l], 6, axis=-1)
        u_x = rms_norm(h_lat, norm_mix_w[l]) * (1.0 + sc1[:, None]) + sh1[:, None]
        u_c = rms_norm(h_ctx, norm_mix_w[l]) * (1.0 + csc1) + csh1
        y_x, y_c = hybrid_mixer(u_x, u_c, w_in[l], w_out[l], lb_all[l], hg_norm_w[l], cos, sin,
                                not last)
        h_lat = h_lat + g1[:, None] * y_x
        v_x = (rms_norm(h_lat, norm_ffn_w[l]) * (1.0 + sc2[:, None]) + sh2[:, None]).reshape(B * N, D)
        if last:
            y = moe_ffn(v_x, router_w[l], router_b[l], w1[l], b1[l], w2[l], b2[l])
        else:
            h_ctx = h_ctx + cg1 * y_c
            v_c = (rms_norm(h_ctx, norm_ffn_w[l]) * (1.0 + csc2) + csh2).reshape(B * C, D)
            y_all = moe_ffn(jnp.concatenate([v_c, v_x], axis=0), router_w[l], router_b[l],
                            w1[l], b1[l], w2[l], b2[l])
            h_ctx = h_ctx + cg2 * y_all[:B * C].reshape(B, C, D)
            y = y_all[B * C:]
        h_lat = h_lat + g2[:, None] * y.reshape(B, N, D)
    return rms_norm(h_lat, norm_final_w)
```

```python
import functools
import math

import jax
import jax.numpy as jnp
import numpy as np
from jax import lax
from jax.experimental import pallas as pl
from jax.experimental.pallas import tpu as pltpu

F32 = jnp.float32
BF16 = jnp.bfloat16
I32 = jnp.int32

D_MODEL = 1024
GRID_W = 64
RET_HEADS = 4
RET_DK = 64
HG_HEADS = 4
PROJ_W = 4096
ROPE_BASE = 10000.0
EPS = 1e-6
N_EXPERTS = 32
TOP_K = 4
D_FF = 1024
SWIGLU_LIMIT = 7.0
SWIGLU_ALPHA = 1.702

LANES = 128
CHUNK = 128
ROW_BLOCK = 256
VMEM_LIMIT = 56 * 1024 * 1024

C_RQ, C_RK, C_RV, C_RG, C_HQ, C_FF, C_FB, C_HV, C_HG = (
    0, 256, 512, 1024, 1536, 2048, 2560, 3072, 3584)


def _cparams(sem):
    return pltpu.CompilerParams(dimension_semantics=sem, vmem_limit_bytes=VMEM_LIMIT)


def _split_bf16(x):
    hi = x.astype(BF16)
    lo = (x - hi.astype(F32)).astype(BF16)
    return hi, lo


def _dot(a, b):
    return jnp.dot(a, b, preferred_element_type=F32)


def _dot_nt(a, b):
    return lax.dot_general(a, b, (((1,), (1,)), ((), ())), preferred_element_type=F32)


def _dot3(a, b):
    ah, al = _split_bf16(a)
    bh, bl = _split_bf16(b)
    return _dot(ah, bh) + (_dot(ah, bl) + _dot(al, bh))


def _silu(x):
    return x * jax.nn.sigmoid(x)


def _ada_kernel(c_ref, w_ref, b_ref, o_ref):
    s = _silu(c_ref[...])
    o_ref[...] = _dot3(s, w_ref[...]) + b_ref[...]


def _ada(cc, w, b):
    nblk = w.shape[1] // D_MODEL
    return pl.pallas_call(
        _ada_kernel,
        out_shape=jax.ShapeDtypeStruct((cc.shape[0], w.shape[1]), F32),
        grid=(nblk,),
        in_specs=[pl.BlockSpec(cc.shape, lambda j: (0, 0)),
                  pl.BlockSpec((D_MODEL, D_MODEL), lambda j: (0, j)),
                  pl.BlockSpec((1, D_MODEL), lambda j: (0, j))],
        out_specs=pl.BlockSpec((cc.shape[0], D_MODEL), lambda j: (0, j)),
        compiler_params=_cparams(("arbitrary",)),
        name="ada",
    )(cc, w, b)


def _inproj_kernel(x_ref, sc_ref, sh_ref, nw_ref, w_ref, lb_ref, cos_ref, sin_ref,
                   pb_ref, lf_ref, u_scr, *, rope):
    x = x_ref[...]
    ms = jnp.mean(x * x, axis=-1, keepdims=True)
    u = x * lax.rsqrt(ms + EPS) * nw_ref[...] * (1.0 + sc_ref[...]) + sh_ref[...]
    u_scr[...] = u.astype(BF16)

    def proj(lo, width):
        return _dot(u_scr[...], w_ref[:, lo:lo + width])

    tm = x.shape[0]
    if rope:
        lane = lax.broadcasted_iota(I32, (tm, LANES), 1)
        first = (lane & 32) == 0

    def put_rot(col, scale):
        for j in range(2):
            lo = col + j * LANES
            t = proj(lo, LANES)
            if scale != 1.0:
                t = t * scale
            if rope:
                tb = j * LANES
                rot = jnp.where(first, pltpu.roll(t, 96, axis=1), pltpu.roll(t, 32, axis=1))
                t = t * cos_ref[:, tb:tb + LANES] + rot * sin_ref[:, tb:tb + LANES]
            pb_ref[:, lo:lo + LANES] = t.astype(BF16)

    put_rot(C_RQ, 1.0)
    put_rot(C_RK, RET_DK ** -0.5)
    pb_ref[:, C_RV:C_RV + 512] = proj(C_RV, 512).astype(BF16)
    pb_ref[:, C_RG:C_RG + 512] = _silu(proj(C_RG, 512)).astype(BF16)
    pb_ref[:, C_HQ:C_HQ + 512] = _silu(proj(C_HQ, 512)).astype(BF16)
    pb_ref[:, C_HV:C_HV + 512] = proj(C_HV, 512).astype(BF16)
    pb_ref[:, C_HG:C_HG + 512] = _silu(proj(C_HG, 512)).astype(BF16)

    la = lb_ref[0]
    lbb = lb_ref[1]
    mx = jnp.maximum(la, lbb)
    ea = jnp.exp(la - mx)
    eb = jnp.exp(lbb - mx)
    lb = ea / (ea + eb)
    for d, col in enumerate((C_FF, C_FB)):
        lbd = lb[d:d + 1, :]
        f = lbd + (1.0 - lbd) * jax.nn.sigmoid(proj(col, 512))
        pb_ref[:, col:col + 512] = (1.0 - f).astype(BF16)
        lf_ref[:, d * 512:(d + 1) * 512] = jnp.log(f)


def _inproj(x, sc, sh, mod_row, nw, w_bf, hg_lb, cos_t, sin_t, *, rope, tm):
    B, n, _ = x.shape
    nt = n // tm
    if mod_row is None:
        mrow = lambda b, j: (b, 0, 0)
    else:
        mrow = lambda b, j: (mod_row, 0, 0)
    return pl.pallas_call(
        functools.partial(_inproj_kernel, rope=rope),
        out_shape=(jax.ShapeDtypeStruct((B, n, PROJ_W), BF16),
                   jax.ShapeDtypeStruct((B, n, 1024), F32)),
        grid=(B, nt),
        in_specs=[pl.BlockSpec((None, tm, D_MODEL), lambda b, j: (b, j, 0)),
                  pl.BlockSpec((None, 1, D_MODEL), mrow),
                  pl.BlockSpec((None, 1, D_MODEL), mrow),
                  pl.BlockSpec((1, D_MODEL), lambda b, j: (0, 0)),
                  pl.BlockSpec((D_MODEL, PROJ_W), lambda b, j: (0, 0)),
                  pl.BlockSpec((2, 2, 512), lambda b, j: (0, 0, 0)),
                  pl.BlockSpec((tm, 256), lambda b, j: (j, 0)),
                  pl.BlockSpec((tm, 256), lambda b, j: (j, 0))],
        out_specs=(pl.BlockSpec((None, tm, PROJ_W), lambda b, j: (b, j, 0)),
                   pl.BlockSpec((None, tm, 1024), lambda b, j: (b, j, 0))),
        scratch_shapes=[pltpu.VMEM((tm, D_MODEL), BF16)],
        compiler_params=_cparams(("arbitrary", "arbitrary")),
        name="inproj_rope" if rope else "inproj_ctx",
    )(x, sc, sh, nw, w_bf, hg_lb, cos_t, sin_t)


_RET_LGF = [math.log1p(-(2.0 ** (-5.0 - 2.0 * h))) for h in range(RET_HEADS)]
_RET_LGB = [math.log1p(-(2.0 ** (-6.0 - 2.0 * h))) for h in range(RET_HEADS)]


def _ret_kernel(q_ref, k_ref, v_ref, g_ref, kc_ref, vc_ref, o_ref,
                u_scr, sin_scr, kt_scr, *, n_lat, n_ctx):
    L = CHUNK
    pair = pl.program_id(1)
    row = lax.broadcasted_iota(I32, (L, L), 0).astype(F32)
    col = lax.broadcasted_iota(I32, (L, L), 1).astype(F32)
    lane = lax.broadcasted_iota(I32, (L, LANES), 1)
    trow = lax.broadcasted_iota(I32, (L, 1), 0).astype(F32)
    tcol = lax.broadcasted_iota(I32, (1, L), 1).astype(F32)
    low_half = lane < RET_DK

    def u_chunk(k_blk, v_blk, ci, store_kt):
        kt = k_blk.astype(F32).T
        if store_kt is not None:
            kt_scr[store_kt] = kt.astype(BF16)
        for hh in range(2):
            lgf = jnp.where(pair == 0, _RET_LGF[hh], _RET_LGF[2 + hh])
            lgb = jnp.where(pair == 0, _RET_LGB[hh], _RET_LGB[2 + hh])
            kth = kt[hh * RET_DK:(hh + 1) * RET_DK, :]
            wkf = jnp.exp(lgf * (L - 1.0 - tcol))
            wkb = jnp.exp(lgb * tcol)
            lhs = jnp.concatenate([kth * wkf, kth * wkb], axis=0).astype(BF16)
            u_scr[hh, ci] = _dot(lhs, v_blk[:, hh * LANES:(hh + 1) * LANES])

    for c in range(n_ctx):
        u_chunk(kc_ref[c * L:(c + 1) * L, :], vc_ref[c * L:(c + 1) * L, :], c, None)

    def lat_u(c, carry):
        r0 = pl.multiple_of(c * L, L)
        u_chunk(k_ref[pl.ds(r0, L), :], v_ref[pl.ds(r0, L), :], n_ctx + c, c)
        return carry
    lax.fori_loop(0, n_lat, lat_u, 0)

    ones = jnp.ones((RET_DK, LANES), F32)
    for hh in range(2):
        lgf = jnp.where(pair == 0, _RET_LGF[hh], _RET_LGF[2 + hh])
        lgb = jnp.where(pair == 0, _RET_LGB[hh], _RET_LGB[2 + hh])
        af = jnp.exp(ones * (lgf * L))
        ab = jnp.exp(ones * (lgb * L))

        s = jnp.zeros((RET_DK, LANES), F32)
        for c in range(n_ctx):
            s = af * s + u_scr[hh, c, 0:RET_DK, :]
        sb = jnp.zeros((RET_DK, LANES), F32)
        for c in reversed(range(n_ctx)):
            sb = ab * sb + u_scr[hh, c, RET_DK:2 * RET_DK, :]

        def fwd(c, s, hh=hh, af=af):
            sin_scr[hh, c, 0:RET_DK, :] = s.astype(BF16)
            return af * s + u_scr[hh, n_ctx + c, 0:RET_DK, :]
        lax.fori_loop(0, n_lat, fwd, s)

        def bwd(i, sb, hh=hh, ab=ab):
            c = n_lat - 1 - i
            sin_scr[hh, c, RET_DK:2 * RET_DK, :] = sb.astype(BF16)
            return ab * sb + u_scr[hh, n_ctx + c, RET_DK:2 * RET_DK, :]
        lax.fori_loop(0, n_lat, bwd, sb)

    def out_chunk(c, carry):
        r0 = pl.multiple_of(c * L, L)
        q = q_ref[pl.ds(r0, L), :].astype(F32)
        qr = pltpu.roll(q, RET_DK, axis=1)
        kt = kt_scr[c]
        for hh in range(2):
            lgf = jnp.where(pair == 0, _RET_LGF[hh], _RET_LGF[2 + hh])
            lgb = jnp.where(pair == 0, _RET_LGB[hh], _RET_LGB[2 + hh])
            d = row - col
            dtot = jnp.where(d > 0, jnp.exp(lgf * jnp.maximum(d, 0.0)),
                             jnp.where(d < 0, jnp.exp(lgb * jnp.maximum(-d, 0.0)), 2.0))
            mine = low_half if hh == 0 else jnp.logical_not(low_half)
            qm = jnp.where(mine, q, 0.0).astype(BF16)
            p = (_dot(qm, kt) * dtot).astype(BF16)
            vh = v_ref[pl.ds(r0, L), hh * LANES:(hh + 1) * LANES]
            wqf = jnp.exp(lgf * (trow + 1.0))
            wqb = jnp.exp(lgb * (L - trow))
            qa, qb = (q, qr) if hh == 0 else (qr, q)
            qs = jnp.where(low_half, qa * wqf, qb * wqb).astype(BF16)
            o = _dot(p, vh) + _dot(qs, sin_scr[hh, c])
            ms = jnp.mean(o * o, axis=-1, keepdims=True)
            gh = g_ref[pl.ds(r0, L), hh * LANES:(hh + 1) * LANES].astype(F32)
            o_ref[pl.ds(r0, L), hh * LANES:(hh + 1) * LANES] = (
                o * lax.rsqrt(ms + EPS) * gh).astype(BF16)
        return carry
    lax.fori_loop(0, n_lat, out_chunk, 0)


def _retention(pb, pbc):
    B, n, _ = pb.shape
    nc = pbc.shape[1]
    n_lat, n_ctx = n // CHUNK, nc // CHUNK
    return pl.pallas_call(
        functools.partial(_ret_kernel, n_lat=n_lat, n_ctx=n_ctx),
        out_shape=jax.ShapeDtypeStruct((B, n, 512), BF16),
        grid=(B, 2),
        in_specs=[pl.BlockSpec((None, n, LANES), lambda b, p: (b, 0, C_RQ // LANES + p)),
                  pl.BlockSpec((None, n, LANES), lambda b, p: (b, 0, C_RK // LANES + p)),
                  pl.BlockSpec((None, n, 256), lambda b, p: (b, 0, C_RV // 256 + p)),
                  pl.BlockSpec((None, n, 256), lambda b, p: (b, 0, C_RG // 256 + p)),
                  pl.BlockSpec((None, nc, LANES), lambda b, p: (b, 0, C_RK // LANES + p)),
                  pl.BlockSpec((None, nc, 256), lambda b, p: (b, 0, C_RV // 256 + p))],
        out_specs=pl.BlockSpec((None, n, 256), lambda b, p: (b, 0, p)),
        scratch_shapes=[pltpu.VMEM((2, n_lat + n_ctx, CHUNK, LANES), F32),
                        pltpu.VMEM((2, n_lat, CHUNK, LANES), BF16),
                        pltpu.VMEM((n_lat, LANES, CHUNK), BF16)],
        compiler_params=_cparams(("arbitrary", "arbitrary")),
        name="retention",
    )(pb, pb, pb, pb, pbc, pbc)


_LEVELS = (64, 32, 16, 8, 4, 2, 1)


def _expand_rows(r, rep):
    n = r.shape[0]
    if n == 1:
        return jnp.broadcast_to(r, (rep, r.shape[1]))
    return jnp.concatenate(
        [jnp.broadcast_to(r[i:i + 1, :], (rep, r.shape[1])) for i in range(n)], axis=0)


def _hgrn_kernel(q_ref, kf_ref, kb_ref, v_ref, g_ref, lff_ref, lfb_ref,
                 kfc_ref, kbc_ref, vc_ref, lffc_ref, lfbc_ref, nw_ref, o_ref,
                 ut_scr, a_scr, qs_scr, oi_scr, sin_scr, bf_scr, bb_scr, *, n_lat, n_ctx):
    L = CHUNK
    row = lax.broadcasted_iota(I32, (L, L), 0)
    col = lax.broadcasted_iota(I32, (L, L), 1)
    xr = row ^ col
    row2 = lax.broadcasted_iota(I32, (L, 2 * L), 0)
    col2 = lax.broadcasted_iota(I32, (L, 2 * L), 1) & (L - 1)
    tril2 = jnp.where(col2 <= row2, 1.0, 0.0).astype(BF16)
    triu2 = jnp.where(col2 >= row2, 1.0, 0.0).astype(BF16)

    def cums(lff, lfb):
        hf, lof = _split_bf16(lff)
        hb, lob = _split_bf16(lfb)
        bf = _dot(tril2, jnp.concatenate([hf, lof], axis=0))
        bb = _dot(triu2, jnp.concatenate([hb, lob], axis=0))
        return bf, bb

    def state_part(ci, kf, kb, v_blk, bf, bb):
        endf = bf[L - 1:L, :]
        endb = bb[0:1, :]
        ksf = kf * jnp.exp(endf - bf)
        ksb = kb * jnp.exp(endb - bb)
        vt = v_blk.astype(F32).T.astype(BF16)
        ut_scr[ci] = _dot(vt, jnp.concatenate([ksf, ksb], axis=1).astype(BF16))
        a_scr[ci] = jnp.broadcast_to(
            jnp.concatenate([jnp.exp(endf), jnp.exp(endb)], axis=1), (8, 2 * LANES))

    for c in range(n_ctx):
        sl = slice(c * L, (c + 1) * L)
        bf, bb = cums(lffc_ref[sl, :], lfbc_ref[sl, :])
        state_part(c, kfc_ref[sl, :].astype(F32), kbc_ref[sl, :].astype(F32),
                   vc_ref[sl, :], bf, bb)

    def lat_chunk(c, carry):
        rows = pl.ds(pl.multiple_of(c * L, L), L)
        q = q_ref[rows, :].astype(F32)
        kf = kf_ref[rows, :].astype(F32)
        kb = kb_ref[rows, :].astype(F32)
        v_blk = v_ref[rows, :]
        lff = lff_ref[rows, :]
        lfb = lfb_ref[rows, :]
        bf, bb = cums(lff, lfb)
        state_part(n_ctx + c, kf, kb, v_blk, bf, bb)
        qs_scr[rows, :] = jnp.concatenate([q * jnp.exp(bf), q * jnp.exp(bb)],
                                          axis=1).astype(BF16)
        bf_scr[...] = bf
        bb_scr[...] = bb

        acc = jnp.zeros((L, L), F32)
        for lvl, h in enumerate(_LEVELS):
            bit = (row & h) != 0
            if h >= 4:
                n = (L // 2) // h
                if n == 1:
                    rf = bf_scr[h - 1:h, :]
                    rb = bb_scr[h:h + 1, :]
                else:
                    rf = bf_scr[pl.ds(h - 1, n, stride=2 * h), :]
                    rb = bb_scr[pl.ds(h, n, stride=2 * h), :]
                rf = _expand_rows(rf, 2 * h)
                rb = _expand_rows(rb, 2 * h)
                ef = jnp.where(bit, bf - rf, rf - bf)
                eb = jnp.where(bit, rb - bb, bb - rb)
            elif h == 2:
                m = row & 3
                ef = jnp.where(m == 2, lff,
                               jnp.where(m == 3, lff + pltpu.roll(lff, 1, axis=0),
                                         jnp.where(m == 0, pltpu.roll(lff, L - 1, axis=0), 0.0)))
                eb = jnp.where(m == 0, lfb + pltpu.roll(lfb, L - 1, axis=0),
                               jnp.where(m == 1, lfb,
                                         jnp.where(m == 3, pltpu.roll(lfb, 1, axis=0), 0.0)))
            else:
                ef = jnp.where(bit, lff, 0.0)
                eb = jnp.where(bit, 0.0, lfb)
            xf = jnp.exp(ef)
            xb = jnp.exp(eb)
            q2 = jnp.concatenate([jnp.where(bit, q * xf, 0.0),
                                  jnp.where(bit, 0.0, q * xb)], axis=1).astype(BF16)
            k2 = jnp.concatenate([jnp.where(bit, 0.0, kf * xf),
                                  jnp.where(bit, kb * xb, 0.0)], axis=1).astype(BF16)
            s = _dot_nt(q2, k2)
            shift = 6 - lvl
            acc = jnp.where((xr >> shift) == 1, s, acc)

        dsum = jnp.sum(q * (kf + kb), axis=-1, keepdims=True)
        oi_scr[rows, :] = _dot(acc.astype(BF16), v_blk) + dsum * v_blk.astype(F32)
        return carry
    lax.fori_loop(0, n_lat, lat_chunk, 0)

    st = jnp.zeros((LANES, LANES), F32)
    for c in range(n_ctx):
        st = st * a_scr[c, 0:1, 0:LANES] + ut_scr[c, :, 0:LANES]
    stb = jnp.zeros((LANES, LANES), F32)
    for c in reversed(range(n_ctx)):
        stb = stb * a_scr[c, 0:1, LANES:2 * LANES] + ut_scr[c, :, LANES:2 * LANES]

    def fwd(c, st):
        sin_scr[c, :, 0:LANES] = st.astype(BF16)
        ci = n_ctx + c
        return st * a_scr[ci, 0:1, 0:LANES] + ut_scr[ci, :, 0:LANES]
    lax.fori_loop(0, n_lat, fwd, st)

    def bwd(i, stb):
        c = n_lat - 1 - i
        sin_scr[c, :, LANES:2 * LANES] = stb.astype(BF16)
        ci = n_ctx + c
        return stb * a_scr[ci, 0:1, LANES:2 * LANES] + ut_scr[ci, :, LANES:2 * LANES]
    lax.fori_loop(0, n_lat, bwd, stb)

    def out_chunk(c, carry):
        rows = pl.ds(pl.multiple_of(c * L, L), L)
        o = oi_scr[rows, :] + _dot_nt(qs_scr[rows, :], sin_scr[c])
        ms = jnp.mean(o * o, axis=-1, keepdims=True)
        y = o * lax.rsqrt(ms + EPS) * nw_ref[...] * g_ref[rows, :].astype(F32)
        o_ref[rows, :] = y.astype(BF16)
        return carry
    lax.fori_loop(0, n_lat, out_chunk, 0)


def _hgrn(pb, lf, pbc, lfc, nw):
    B, n, _ = pb.shape
    nc = pbc.shape[1]
    n_lat, n_ctx = n // CHUNK, nc // CHUNK

    def colblk(rows, col0):
        return pl.BlockSpec((None, rows, LANES), lambda b, h: (b, 0, col0 // LANES + h))

    return pl.pallas_call(
        functools.partial(_hgrn_kernel, n_lat=n_lat, n_ctx=n_ctx),
        out_shape=jax.ShapeDtypeStruct((B, n, 512), BF16),
        grid=(B, HG_HEADS),
        in_specs=[colblk(n, C_HQ), colblk(n, C_FF), colblk(n, C_FB), colblk(n, C_HV),
                  colblk(n, C_HG), colblk(n, 0), colblk(n, 512),
                  colblk(nc, C_FF), colblk(nc, C_FB), colblk(nc, C_HV),
                  colblk(nc, 0), colblk(nc, 512),
                  pl.BlockSpec((1, LANES), lambda b, h: (0, 0))],
        out_specs=pl.BlockSpec((None, n, LANES), lambda b, h: (b, 0, h)),
        scratch_shapes=[pltpu.VMEM((n_lat + n_ctx, LANES, 2 * LANES), F32),
                        pltpu.VMEM((n_lat + n_ctx, 8, 2 * LANES), F32),
                        pltpu.VMEM((n, 2 * LANES), BF16),
                        pltpu.VMEM((n, LANES), F32),
                        pltpu.VMEM((n_lat, LANES, 2 * LANES), BF16),
                        pltpu.VMEM((CHUNK, LANES), F32),
                        pltpu.VMEM((CHUNK, LANES), F32)],
        compiler_params=_cparams(("arbitrary", "arbitrary")),
        name="hgrn2",
    )(pb, pb, pb, pb, pb, lf, lf, pbc, pbc, pbc, lfc, lfc, nw)


def _outproj_kernel(ar_ref, ah_ref, w_ref, x_ref, g1_ref, sc_ref, sh_ref, nw_ref,
                    rw_ref, rb_ref, h1_ref, v_ref, route_ref, cnt_ref, cnt_scr):
    first_step = jnp.logical_and(pl.program_id(0) == 0, pl.program_id(1) == 0)

    @pl.when(first_step)
    def _():
        cnt_scr[...] = jnp.zeros_like(cnt_scr)

    y = _dot(ar_ref[...], w_ref[0:512, :]) + _dot(ah_ref[...], w_ref[512:1024, :])
    h1 = x_ref[...] + g1_ref[...] * y
    h1_ref[...] = h1
    ms = jnp.mean(h1 * h1, axis=-1, keepdims=True)
    v = h1 * lax.rsqrt(ms + EPS) * nw_ref[...] * (1.0 + sc_ref[...]) + sh_ref[...]
    v_ref[...] = v

    tm = v.shape[0]
    logits = _dot3(v, rw_ref[...]) + rb_ref[...]
    lane = lax.broadcasted_iota(I32, (tm, LANES), 1)
    lane_f = lane.astype(F32)
    l = logits
    sels, tops, idxs = [], [], []
    for _ in range(TOP_K):
        m = jnp.max(l, axis=1, keepdims=True)
        i = jnp.min(jnp.where(l == m, lane_f, float(LANES)), axis=1, keepdims=True)
        sel = lane_f == i
        l = jnp.where(sel, -jnp.inf, l)
        sels.append(sel)
        tops.append(m)
        idxs.append(i)
    es = [jnp.exp(t - tops[0]) for t in tops]
    den = es[0] + es[1] + es[2] + es[3]
    gates = [e / den for e in es]

    oh = jnp.zeros((tm, LANES), F32)
    for sel in sels:
        oh = jnp.where(sel, 1.0, oh)
    r = lax.broadcasted_iota(I32, (tm, tm), 0)
    c = lax.broadcasted_iota(I32, (tm, tm), 1)
    tri = jnp.where(c < r, 1.0, 0.0).astype(BF16)
    before = _dot(tri, oh.astype(BF16)) + cnt_scr[0:1, :]
    ranks = [jnp.sum(jnp.where(sel, before, 0.0), axis=1, keepdims=True) for sel in sels]
    cnt_scr[...] = cnt_scr[...] + jnp.sum(oh, axis=0, keepdims=True)
    cnt_ref[...] = cnt_scr[...]

    out = jnp.zeros((tm, LANES), F32)
    for k in range(TOP_K):
        out = jnp.where(lane == k, idxs[k], out)
        out = jnp.where(lane == TOP_K + k, gates[k], out)
        out = jnp.where(lane == 2 * TOP_K + k, ranks[k], out)
    route_ref[...] = out


def _outproj(a_ret, a_hg, w_bf, x, g1, sc2, sh2, nw, rw, rb, *, tm):
    B, n, _ = x.shape
    nt = n // tm
    T = B * n
    mrow = lambda b, j: (b, 0, 0)
    tok = lambda b, j: (b * nt + j, 0)
    return pl.pallas_call(
        _outproj_kernel,
        out_shape=(jax.ShapeDtypeStruct((T, D_MODEL), F32),
                   jax.ShapeDtypeStruct((T, D_MODEL), F32),
                   jax.ShapeDtypeStruct((T, LANES), F32),
                   jax.ShapeDtypeStruct((8, LANES), F32)),
        grid=(B, nt),
        in_specs=[pl.BlockSpec((None, tm, 512), lambda b, j: (b, j, 0)),
                  pl.BlockSpec((None, tm, 512), lambda b, j: (b, j, 0)),
                  pl.BlockSpec((D_MODEL, D_MODEL), lambda b, j: (0, 0)),
                  pl.BlockSpec((None, tm, D_MODEL), lambda b, j: (b, j, 0)),
                  pl.BlockSpec((None, 1, D_MODEL), mrow),
                  pl.BlockSpec((None, 1, D_MODEL), mrow),
                  pl.BlockSpec((None, 1, D_MODEL), mrow),
                  pl.BlockSpec((1, D_MODEL), lambda b, j: (0, 0)),
                  pl.BlockSpec((D_MODEL, LANES), lambda b, j: (0, 0)),
                  pl.BlockSpec((1, LANES), lambda b, j: (0, 0))],
        out_specs=(pl.BlockSpec((tm, D_MODEL), tok),
                   pl.BlockSpec((tm, D_MODEL), tok),
                   pl.BlockSpec((tm, LANES), tok),
                   pl.BlockSpec((8, LANES), lambda b, j: (0, 0))),
        scratch_shapes=[pltpu.VMEM((8, LANES), F32)],
        compiler_params=_cparams(("arbitrary", "arbitrary")),
        name="outproj_router",
    )(a_ret, a_hg, w_bf, x, g1, sc2, sh2, nw, rw, rb)


def _dispatch_kernel(meta_ref, dest_hbm, v_hbm, xs_hbm, idx_smem, zbuf, sem_idx, sem_z, sem_rows,
                     *, tile, n_blocks):
    i = pl.program_id(0)
    n_asg = tile * TOP_K
    idx_cp = pltpu.make_async_copy(dest_hbm.at[i], idx_smem, sem_idx)
    idx_cp.start()

    def zero_copy(r0):
        return pltpu.make_async_copy(zbuf, xs_hbm.at[pl.ds(r0, ROW_BLOCK), :], sem_z)

    @pl.when(i == 0)
    def _():
        zbuf[...] = jnp.zeros_like(zbuf)
        n_used = meta_ref[2 * N_EXPERTS]

        def tail_start(e):
            return pl.multiple_of(meta_ref[e] - ROW_BLOCK, ROW_BLOCK)

        def has_tail(e):
            return (meta_ref[N_EXPERTS + e] & (ROW_BLOCK - 1)) != 0

        def z_issue(e, carry):
            @pl.when(has_tail(e))
            def _():
                zero_copy(tail_start(e)).start()
            return carry
        lax.fori_loop(0, N_EXPERTS, z_issue, 0)

        def u_issue(blk, carry):
            @pl.when(blk >= n_used)
            def _():
                zero_copy(pl.multiple_of(blk * ROW_BLOCK, ROW_BLOCK)).start()
            return carry
        lax.fori_loop(0, n_blocks, u_issue, 0)

        def z_wait(e, carry):
            @pl.when(has_tail(e))
            def _():
                zero_copy(0).wait()
            return carry
        lax.fori_loop(0, N_EXPERTS, z_wait, 0)

        def u_wait(blk, carry):
            @pl.when(blk >= n_used)
            def _():
                zero_copy(0).wait()
            return carry
        lax.fori_loop(0, n_blocks, u_wait, 0)

    idx_cp.wait()
    t0 = i * tile

    def issue(a, carry):
        t = t0 + lax.shift_right_logical(a, 2)
        d = idx_smem[a]
        pltpu.make_async_copy(v_hbm.at[pl.ds(t, 1), :], xs_hbm.at[pl.ds(d, 1), :], sem_rows).start()
        return carry
    lax.fori_loop(0, n_asg, issue, 0, unroll=8)
    pltpu.make_async_copy(v_hbm.at[pl.ds(0, n_asg), :], xs_hbm.at[pl.ds(0, n_asg), :],
                          sem_rows).wait()


def _dispatch(meta, dest_tiles, v, *, tile, n_blocks):
    T = v.shape[0]
    R = n_blocks * ROW_BLOCK
    return pl.pallas_call(
        functools.partial(_dispatch_kernel, tile=tile, n_blocks=n_blocks),
        out_shape=jax.ShapeDtypeStruct((R, D_MODEL), F32),
        grid_spec=pltpu.PrefetchScalarGridSpec(
            num_scalar_prefetch=1,
            grid=(T // tile,),
            in_specs=[pl.BlockSpec(memory_space=pl.ANY),
                      pl.BlockSpec(memory_space=pl.ANY)],
            out_specs=pl.BlockSpec(memory_space=pl.ANY),
            scratch_shapes=[pltpu.SMEM((tile * TOP_K,), I32),
                            pltpu.VMEM((ROW_BLOCK, D_MODEL), F32),
                            pltpu.SemaphoreType.DMA,
                            pltpu.SemaphoreType.DMA,
                            pltpu.SemaphoreType.DMA]),
        compiler_params=_cparams(("arbitrary",)),
        name="dispatch",
    )(meta, dest_tiles, v)


def _moe_kernel(be_ref, nu_ref, x_ref, w1_ref, b1_ref, w2_ref, b2_ref, y_ref, w1_scr, w2_scr):
    i = pl.program_id(0)
    prev = be_ref[jnp.maximum(i - 1, 0)]
    changed = jnp.logical_or(i == 0, be_ref[i] != prev)

    @pl.when(changed)
    def _():
        w1_scr[...] = w1_ref[...].astype(BF16)
        w2_scr[...] = w2_ref[...].astype(BF16)

    @pl.when(i < nu_ref[0])
    def _():
        xb = x_ref[...].astype(BF16)
        acc = jnp.zeros((ROW_BLOCK, D_MODEL), F32) + b2_ref[...]
        cw = 512
        for c in range(D_FF // cw):
            glu = _dot(xb, w1_scr[:, c * cw:(c + 1) * cw]) + b1_ref[:, c * cw:(c + 1) * cw]
            lin = (_dot(xb, w1_scr[:, D_FF + c * cw:D_FF + (c + 1) * cw])
                   + b1_ref[:, D_FF + c * cw:D_FF + (c + 1) * cw])
            glu = jnp.minimum(glu, SWIGLU_LIMIT)
            lin = jnp.clip(lin, -SWIGLU_LIMIT, SWIGLU_LIMIT)
            act = glu * jax.nn.sigmoid(SWIGLU_ALPHA * glu) * (lin + 1.0)
            acc = acc + _dot(act.astype(BF16), w2_scr[c * cw:(c + 1) * cw, :])
        y_ref[...] = acc

    @pl.when(i >= nu_ref[0])
    def _():
        y_ref[...] = jnp.zeros_like(y_ref)


def _moe(block_e, n_used, xs, w1, b1, w2, b2, *, n_blocks):
    R = xs.shape[0]
    return pl.pallas_call(
        _moe_kernel,
        out_shape=jax.ShapeDtypeStruct((R, D_MODEL), F32),
        grid_spec=pltpu.PrefetchScalarGridSpec(
            num_scalar_prefetch=2,
            grid=(n_blocks,),
            in_specs=[pl.BlockSpec((ROW_BLOCK, D_MODEL), lambda i, be, nu: (i, 0)),
                      pl.BlockSpec((None, D_MODEL, 2 * D_FF), lambda i, be, nu: (be[i], 0, 0)),
                      pl.BlockSpec((None, 1, 2 * D_FF), lambda i, be, nu: (be[i], 0, 0)),
                      pl.BlockSpec((None, D_FF, D_MODEL), lambda i, be, nu: (be[i], 0, 0)),
                      pl.BlockSpec((None, 1, D_MODEL), lambda i, be, nu: (be[i], 0, 0))],
            out_specs=pl.BlockSpec((ROW_BLOCK, D_MODEL), lambda i, be, nu: (i, 0)),
            scratch_shapes=[pltpu.VMEM((D_MODEL, 2 * D_FF), BF16),
                            pltpu.VMEM((D_FF, D_MODEL), BF16)]),
        compiler_params=_cparams(("arbitrary",)),
        name="moe_ffn",
    )(block_e, n_used, xs, w1, b1, w2, b2)


def _combine_kernel(dest_hbm, yb_hbm, route_ref, h1_ref, g2_ref, nw_ref, o_ref,
                    idx_smem, rows, sem_idx, sem_rows, *, tile):
    i = pl.program_id(0)
    n_asg = tile * TOP_K
    idx_cp = pltpu.make_async_copy(dest_hbm.at[i], idx_smem, sem_idx)
    idx_cp.start()
    idx_cp.wait()

    def issue(a, carry):
        t = lax.shift_right_logical(a, 2)
        k = a & (TOP_K - 1)
        d = idx_smem[a]
        pltpu.make_async_copy(yb_hbm.at[pl.ds(d, 1), :], rows.at[k, pl.ds(t, 1), :],
                              sem_rows).start()
        return carry
    lax.fori_loop(0, n_asg, issue, 0, unroll=8)
    for k in range(TOP_K):
        pltpu.make_async_copy(yb_hbm.at[pl.ds(0, tile), :], rows.at[k], sem_rows).wait()

    route = route_ref[...]
    y = jnp.zeros((tile, D_MODEL), F32)
    for k in range(TOP_K):
        y = y + route[:, TOP_K + k:TOP_K + k + 1] * rows[k]
    h = h1_ref[...] + g2_ref[...] * y
    ms = jnp.mean(h * h, axis=-1, keepdims=True)
    o_ref[...] = h * lax.rsqrt(ms + EPS) * nw_ref[...]


def _combine(dest_tiles, yb, route, h1, g2, nw, *, tile, tiles_per_batch):
    T = h1.shape[0]
    return pl.pallas_call(
        functools.partial(_combine_kernel, tile=tile),
        out_shape=jax.ShapeDtypeStruct((T, D_MODEL), F32),
        grid=(T // tile,),
        in_specs=[pl.BlockSpec(memory_space=pl.ANY),
                  pl.BlockSpec(memory_space=pl.ANY),
                  pl.BlockSpec((tile, LANES), lambda i: (i, 0)),
                  pl.BlockSpec((tile, D_MODEL), lambda i: (i, 0)),
                  pl.BlockSpec((None, 1, D_MODEL), lambda i: (i // tiles_per_batch, 0, 0)),
                  pl.BlockSpec((1, D_MODEL), lambda i: (0, 0))],
        out_specs=pl.BlockSpec((tile, D_MODEL), lambda i: (i, 0)),
        scratch_shapes=[pltpu.SMEM((tile * TOP_K,), I32),
                        pltpu.VMEM((TOP_K, tile, D_MODEL), F32),
                        pltpu.SemaphoreType.DMA,
                        pltpu.SemaphoreType.DMA],
        compiler_params=_cparams(("arbitrary",)),
        name="combine_norm",
    )(dest_tiles, yb, route, h1, g2, nw)


def _rope_tables(n):
    rows = n // GRID_W
    row = jnp.repeat(jnp.arange(rows, dtype=F32), GRID_W)
    col = jnp.tile(jnp.arange(GRID_W, dtype=F32), rows)
    n_freq = RET_DK // 4
    inv = ROPE_BASE ** (-jnp.arange(n_freq, dtype=F32) / n_freq)
    ang = jnp.concatenate([row[:, None] * inv, col[:, None] * inv], axis=-1)
    cos, sin = jnp.cos(ang), jnp.sin(ang)
    cos_h = jnp.concatenate([cos, cos], axis=-1)
    sin_h = jnp.concatenate([-sin, sin], axis=-1)
    return jnp.tile(cos_h, (1, RET_HEADS)), jnp.tile(sin_h, (1, RET_HEADS))


def kernel(x, c, ctx, c_ctx, w_ada, b_ada, norm_mix_w, norm_ffn_w, w_in, w_out, hg_lb,
           hg_norm_w, router_w, router_b, w1, b1, w2, b2, norm_final_w):
    B, N, D = x.shape
    C = ctx.shape[1]
    T = B * N
    assert D == D_MODEL and w_ada.shape[0] == 1

    cc = jnp.concatenate([c.astype(F32), c_ctx.astype(F32)[None, :],
                          jnp.zeros((16 - B - 1, D), F32)], axis=0)
    mod = _ada(cc, w_ada[0], b_ada[0][None, :])
    mod = mod.reshape(16, 6, 1, D).transpose(1, 0, 2, 3)
    sh1, sc1, g1, sh2, sc2, g2 = (mod[i] for i in range(6))

    w_in_bf = w_in[0].astype(BF16)
    w_out_bf = w_out[0].astype(BF16)
    nw_mix = norm_mix_w[0][None, :]
    cos_t, sin_t = _rope_tables(N)

    pb, lf = _inproj(x, sc1, sh1, None, nw_mix, w_in_bf, hg_lb[:2], cos_t, sin_t,
                     rope=True, tm=512)
    pbc, lfc = _inproj(ctx, sc1, sh1, B, nw_mix, w_in_bf, hg_lb[:2], cos_t[:C], sin_t[:C],
                       rope=False, tm=C)

    a_ret = _retention(pb, pbc)
    a_hg = _hgrn(pb, lf, pbc, lfc, hg_norm_w[0][None, :])

    rw = jnp.zeros((D, LANES), F32).at[:, :N_EXPERTS].set(router_w[0])
    rb = jnp.full((1, LANES), -1e30, F32).at[0, :N_EXPERTS].set(router_b[0])
    h1, v, route, cnt = _outproj(a_ret, a_hg, w_out_bf, x, g1, sc2, sh2,
                                 norm_ffn_w[0][None, :], rw, rb, tm=512)

    idx = route[:, 0:TOP_K].astype(I32)
    rank = route[:, 2 * TOP_K:3 * TOP_K].astype(I32)
    counts = cnt[0, :N_EXPERTS].astype(I32)
    padded = (counts + ROW_BLOCK - 1) // ROW_BLOCK * ROW_BLOCK
    pend = jnp.cumsum(padded)
    pstart = pend - padded
    dest = (pstart[idx] + rank).reshape(-1)
    n_blocks = (T * TOP_K) // ROW_BLOCK + N_EXPERTS
    n_used = (pend[-1] // ROW_BLOCK).astype(I32)
    block_e = jnp.minimum(
        jnp.searchsorted(pend, jnp.arange(n_blocks, dtype=I32) * ROW_BLOCK, side="right"),
        N_EXPERTS - 1).astype(I32)
    meta = jnp.concatenate([pend.astype(I32), counts, n_used[None]])

    tile_d = 512
    xs = _dispatch(meta, dest.reshape(T // tile_d, tile_d * TOP_K), v,
                   tile=tile_d, n_blocks=n_blocks)
    yb = _moe(block_e, n_used[None], xs, w1[0], b1[0][:, None, :], w2[0], b2[0][:, None, :],
              n_blocks=n_blocks)
    tile_c = 256
    out = _combine(dest.reshape(T // tile_c, tile_c * TOP_K), yb, route, h1, g2,
                   norm_final_w[None, :], tile=tile_c, tiles_per_batch=N // tile_c)
    return out.reshape(B, N, D)
```

```python
import functools
import math

import jax
import jax.numpy as jnp
import numpy as np
from jax import lax
from jax.experimental import pallas as pl
from jax.experimental.pallas import tpu as pltpu

F32 = jnp.float32
BF16 = jnp.bfloat16
I32 = jnp.int32

D_MODEL = 1024
GRID_W = 64
RET_HEADS = 4
RET_DK = 64
HG_HEADS = 4
PROJ_W = 4096
ROPE_BASE = 10000.0
EPS = 1e-6
N_EXPERTS = 32
TOP_K = 4
D_FF = 1024
SWIGLU_LIMIT = 7.0
SWIGLU_ALPHA = 1.702

LANES = 128
CHUNK = 128
ROW_BLOCK = 256
VMEM_LIMIT = 56 * 1024 * 1024

C_RQ, C_RK, C_RV, C_RG, C_HQ, C_FF, C_FB, C_HV, C_HG = (
    0, 256, 512, 1024, 1536, 2048, 2560, 3072, 3584)


def _cparams(sem):
    return pltpu.CompilerParams(dimension_semantics=sem, vmem_limit_bytes=VMEM_LIMIT)


def _split_bf16(x):
    hi = x.astype(BF16)
    lo = (x - hi.astype(F32)).astype(BF16)
    return hi, lo


def _dot(a, b):
    return jnp.dot(a, b, preferred_element_type=F32)


def _dot_nt(a, b):
    return lax.dot_general(a, b, (((1,), (1,)), ((), ())), preferred_element_type=F32)


def _dot3(a, b):
    ah, al = _split_bf16(a)
    bh, bl = _split_bf16(b)
    return _dot(ah, bh) + (_dot(ah, bl) + _dot(al, bh))


def _silu(x):
    return x * jax.nn.sigmoid(x)


def _ada_kernel(c_ref, w_ref, b_ref, o_ref):
    s = _silu(c_ref[...])
    o_ref[...] = _dot3(s, w_ref[...]) + b_ref[...]


def _ada(cc, w, b):
    nblk = w.shape[1] // D_MODEL
    return pl.pallas_call(
        _ada_kernel,
        out_shape=jax.ShapeDtypeStruct((cc.shape[0], w.shape[1]), F32),
        grid=(nblk,),
        in_specs=[pl.BlockSpec(cc.shape, lambda j: (0, 0)),
                  pl.BlockSpec((D_MODEL, D_MODEL), lambda j: (0, j)),
                  pl.BlockSpec((1, D_MODEL), lambda j: (0, j))],
        out_specs=pl.BlockSpec((cc.shape[0], D_MODEL), lambda j: (0, j)),
        compiler_params=_cparams(("arbitrary",)),
        name="ada",
    )(cc, w, b)


def _inproj_kernel(x_ref, sc_ref, sh_ref, nw_ref, w_ref, lb_ref, cos_ref, sin_ref,
                   pb_ref, lf_ref, u_scr, *, rope):
    x = x_ref[...]
    ms = jnp.mean(x * x, axis=-1, keepdims=True)
    u = x * lax.rsqrt(ms + EPS) * nw_ref[...] * (1.0 + sc_ref[...]) + sh_ref[...]
    u_scr[...] = u.astype(BF16)

    def proj(lo, width):
        return _dot(u_scr[...], w_ref[:, lo:lo + width])

    tm = x.shape[0]
    if rope:
        lane = lax.broadcasted_iota(I32, (tm, LANES), 1)
        first = (lane & 32) == 0

    def put_rot(col, scale):
        for j in range(2):
            lo = col + j * LANES
            t = proj(lo, LANES)
            if scale != 1.0:
                t = t * scale
            if rope:
                tb = j * LANES
                rot = jnp.where(first, pltpu.roll(t, 96, axis=1), pltpu.roll(t, 32, axis=1))
                t = t * cos_ref[:, tb:tb + LANES] + rot * sin_ref[:, tb:tb + LANES]
            pb_ref[:, lo:lo + LANES] = t.astype(BF16)

    put_rot(C_RQ, 1.0)
    put_rot(C_RK, RET_DK ** -0.5)
    pb_ref[:, C_RV:C_RV + 512] = proj(C_RV, 512).astype(BF16)
    pb_ref[:, C_RG:C_RG + 512] = _silu(proj(C_RG, 512)).astype(BF16)
    pb_ref[:, C_HQ:C_HQ + 512] = _silu(proj(C_HQ, 512)).astype(BF16)
    pb_ref[:, C_HV:C_HV + 512] = proj(C_HV, 512).astype(BF16)
    pb_ref[:, C_HG:C_HG + 512] = _silu(proj(C_HG, 512)).astype(BF16)

    la = lb_ref[0]
    lbb = lb_ref[1]
    mx = jnp.maximum(la, lbb)
    ea = jnp.exp(la - mx)
    eb = jnp.exp(lbb - mx)
    lb = ea / (ea + eb)
    for d, col in enumerate((C_FF, C_FB)):
        lbd = lb[d:d + 1, :]
        f = lbd + (1.0 - lbd) * jax.nn.sigmoid(proj(col, 512))
        pb_ref[:, col:col + 512] = (1.0 - f).astype(BF16)
        lf_ref[:, d * 512:(d + 1) * 512] = jnp.log(f)


def _inproj(x, sc, sh, mod_row, nw, w_bf, hg_lb, cos_t, sin_t, *, rope, tm):
    B, n, _ = x.shape
    nt = n // tm
    if mod_row is None:
        mrow = lambda b, j: (b, 0, 0)
    else:
        mrow = lambda b, j: (mod_row, 0, 0)
    return pl.pallas_call(
        functools.partial(_inproj_kernel, rope=rope),
        out_shape=(jax.ShapeDtypeStruct((B, n, PROJ_W), BF16),
                   jax.ShapeDtypeStruct((B, n, 1024), F32)),
        grid=(B, nt),
        in_specs=[pl.BlockSpec((None, tm, D_MODEL), lambda b, j: (b, j, 0)),
                  pl.BlockSpec((None, 1, D_MODEL), mrow),
                  pl.BlockSpec((None, 1, D_MODEL), mrow),
                  pl.BlockSpec((1, D_MODEL), lambda b, j: (0, 0)),
                  pl.BlockSpec((D_MODEL, PROJ_W), lambda b, j: (0, 0)),
                  pl.BlockSpec((2, 2, 512), lambda b, j: (0, 0, 0)),
                  pl.BlockSpec((tm, 256), lambda b, j: (j, 0)),
                  pl.BlockSpec((tm, 256), lambda b, j: (j, 0))],
        out_specs=(pl.BlockSpec((None, tm, PROJ_W), lambda b, j: (b, j, 0)),
                   pl.BlockSpec((None, tm, 1024), lambda b, j: (b, j, 0))),
        scratch_shapes=[pltpu.VMEM((tm, D_MODEL), BF16)],
        compiler_params=_cparams(("arbitrary", "arbitrary")),
        name="inproj_rope" if rope else "inproj_ctx",
    )(x, sc, sh, nw, w_bf, hg_lb, cos_t, sin_t)


_RET_LGF = [math.log1p(-(2.0 ** (-5.0 - 2.0 * h))) for h in range(RET_HEADS)]
_RET_LGB = [math.log1p(-(2.0 ** (-6.0 - 2.0 * h))) for h in range(RET_HEADS)]


def _ret_kernel(q_ref, k_ref, v_ref, g_ref, kc_ref, vc_ref, o_ref,
                u_scr, sin_scr, kt_scr, *, n_lat, n_ctx):
    L = CHUNK
    pair = pl.program_id(1)
    row = lax.broadcasted_iota(I32, (L, L), 0).astype(F32)
    col = lax.broadcasted_iota(I32, (L, L), 1).astype(F32)
    lane = lax.broadcasted_iota(I32, (L, LANES), 1)
    trow = lax.broadcasted_iota(I32, (L, 1), 0).astype(F32)
    tcol = lax.broadcasted_iota(I32, (1, L), 1).astype(F32)
    low_half = lane < RET_DK

    def u_chunk(k_blk, v_blk, ci, store_kt):
        kt = k_blk.astype(F32).T
        if store_kt is not None:
            kt_scr[store_kt] = kt.astype(BF16)
        for hh in range(2):
            lgf = jnp.where(pair == 0, _RET_LGF[hh], _RET_LGF[2 + hh])
            lgb = jnp.where(pair == 0, _RET_LGB[hh], _RET_LGB[2 + hh])
            kth = kt[hh * RET_DK:(hh + 1) * RET_DK, :]
            wkf = jnp.exp(lgf * (L - 1.0 - tcol))
            wkb = jnp.exp(lgb * tcol)
            lhs = jnp.concatenate([kth * wkf, kth * wkb], axis=0).astype(BF16)
            u_scr[hh, ci] = _dot(lhs, v_blk[:, hh * LANES:(hh + 1) * LANES])

    for c in range(n_ctx):
        u_chunk(kc_ref[c * L:(c + 1) * L, :], vc_ref[c * L:(c + 1) * L, :], c, None)

    def lat_u(c, carry):
        r0 = pl.multiple_of(c * L, L)
        u_chunk(k_ref[pl.ds(r0, L), :], v_ref[pl.ds(r0, L), :], n_ctx + c, c)
        return carry
    lax.fori_loop(0, n_lat, lat_u, 0)

    ones = jnp.ones((RET_DK, LANES), F32)
    for hh in range(2):
        lgf = jnp.where(pair == 0, _RET_LGF[hh], _RET_LGF[2 + hh])
        lgb = jnp.where(pair == 0, _RET_LGB[hh], _RET_LGB[2 + hh])
        af = jnp.exp(ones * (lgf * L))
        ab = jnp.exp(ones * (lgb * L))

        s = jnp.zeros((RET_DK, LANES), F32)
        for c in range(n_ctx):
            s = af * s + u_scr[hh, c, 0:RET_DK, :]
        sb = jnp.zeros((RET_DK, LANES), F32)
        for c in reversed(range(n_ctx)):
            sb = ab * sb + u_scr[hh, c, RET_DK:2 * RET_DK, :]

        def fwd(c, s, hh=hh, af=af):
            sin_scr[hh, c, 0:RET_DK, :] = s.astype(BF16)
            return af * s + u_scr[hh, n_ctx + c, 0:RET_DK, :]
        lax.fori_loop(0, n_lat, fwd, s)

        def bwd(i, sb, hh=hh, ab=ab):
            c = n_lat - 1 - i
            sin_scr[hh, c, RET_DK:2 * RET_DK, :] = sb.astype(BF16)
            return ab * sb + u_scr[hh, n_ctx + c, RET_DK:2 * RET_DK, :]
        lax.fori_loop(0, n_lat, bwd, sb)

    def out_chunk(c, carry):
        r0 = pl.multiple_of(c * L, L)
        q = q_ref[pl.ds(r0, L), :].astype(F32)
        qr = pltpu.roll(q, RET_DK, axis=1)
        kt = kt_scr[c]
        for hh in range(2):
            lgf = jnp.where(pair == 0, _RET_LGF[hh], _RET_LGF[2 + hh])
            lgb = jnp.where(pair == 0, _RET_LGB[hh], _RET_LGB[2 + hh])
            d = row - col
            dtot = jnp.where(d > 0, jnp.exp(lgf * jnp.maximum(d, 0.0)),
                             jnp.where(d < 0, jnp.exp(lgb * jnp.maximum(-d, 0.0)), 2.0))
            mine = low_half if hh == 0 else jnp.logical_not(low_half)
            qm = jnp.where(mine, q, 0.0).astype(BF16)
            p = (_dot(qm, kt) * dtot).astype(BF16)
            vh = v_ref[pl.ds(r0, L), hh * LANES:(hh + 1) * LANES]
            wqf = jnp.exp(lgf * (trow + 1.0))
            wqb = jnp.exp(lgb * (L - trow))
            qa, qb = (q, qr) if hh == 0 else (qr, q)
            qs = jnp.where(low_half, qa * wqf, qb * wqb).astype(BF16)
            o = _dot(p, vh) + _dot(qs, sin_scr[hh, c])
            ms = jnp.mean(o * o, axis=-1, keepdims=True)
            gh = g_ref[pl.ds(r0, L), hh * LANES:(hh + 1) * LANES].astype(F32)
            o_ref[pl.ds(r0, L), hh * LANES:(hh + 1) * LANES] = (
                o * lax.rsqrt(ms + EPS) * gh).astype(BF16)
        return carry
    lax.fori_loop(0, n_lat, out_chunk, 0)


def _retention(pb, pbc):
    B, n, _ = pb.shape
    nc = pbc.shape[1]
    n_lat, n_ctx = n // CHUNK, nc // CHUNK
    return pl.pallas_call(
        functools.partial(_ret_kernel, n_lat=n_lat, n_ctx=n_ctx),
        out_shape=jax.ShapeDtypeStruct((B, n, 512), BF16),
        grid=(B, 2),
        in_specs=[pl.BlockSpec((None, n, LANES), lambda b, p: (b, 0, C_RQ // LANES + p)),
                  pl.BlockSpec((None, n, LANES), lambda b, p: (b, 0, C_RK // LANES + p)),
                  pl.BlockSpec((None, n, 256), lambda b, p: (b, 0, C_RV // 256 + p)),
                  pl.BlockSpec((None, n, 256), lambda b, p: (b, 0, C_RG // 256 + p)),
                  pl.BlockSpec((None, nc, LANES), lambda b, p: (b, 0, C_RK // LANES + p)),
                  pl.BlockSpec((None, nc, 256), lambda b, p: (b, 0, C_RV // 256 + p))],
        out_specs=pl.BlockSpec((None, n, 256), lambda b, p: (b, 0, p)),
        scratch_shapes=[pltpu.VMEM((2, n_lat + n_ctx, CHUNK, LANES), F32),
                        pltpu.VMEM((2, n_lat, CHUNK, LANES), BF16),
                        pltpu.VMEM((n_lat, LANES, CHUNK), BF16)],
        compiler_params=_cparams(("arbitrary", "arbitrary")),
        name="retention",
    )(pb, pb, pb, pb, pbc, pbc)


_LEVELS = (64, 32, 16, 8, 4, 2, 1)


def _expand_rows(r, rep):
    n = r.shape[0]
    if n == 1:
        return jnp.broadcast_to(r, (rep, r.shape[1]))
    return jnp.concatenate(
        [jnp.broadcast_to(r[i:i + 1, :], (rep, r.shape[1])) for i in range(n)], axis=0)


def _hgrn_kernel(q_ref, kf_ref, kb_ref, v_ref, g_ref, lff_ref, lfb_ref,
                 kfc_ref, kbc_ref, vc_ref, lffc_ref, lfbc_ref, nw_ref, o_ref,
                 ut_scr, a_scr, qs_scr, oi_scr, sin_scr, bf_scr, bb_scr, *, n_lat, n_ctx):
    L = CHUNK
    row = lax.broadcasted_iota(I32, (L, L), 0)
    col = lax.broadcasted_iota(I32, (L, L), 1)
    xr = row ^ col
    row2 = lax.broadcasted_iota(I32, (L, 2 * L), 0)
    col2 = lax.broadcasted_iota(I32, (L, 2 * L), 1) & (L - 1)
    tril2 = jnp.where(col2 <= row2, 1.0, 0.0).astype(BF16)
    triu2 = jnp.where(col2 >= row2, 1.0, 0.0).astype(BF16)

    def cums(lff, lfb):
        hf, lof = _split_bf16(lff)
        hb, lob = _split_bf16(lfb)
        bf = _dot(tril2, jnp.concatenate([hf, lof], axis=0))
        bb = _dot(triu2, jnp.concatenate([hb, lob], axis=0))
        return bf, bb

    def state_part(ci, kf, kb, v_blk, bf, bb):
        endf = bf[L - 1:L, :]
        endb = bb[0:1, :]
        ksf = kf * jnp.exp(endf - bf)
        ksb = kb * jnp.exp(endb - bb)
        vt = v_blk.astype(F32).T.astype(BF16)
        ut_scr[ci] = _dot(vt, jnp.concatenate([ksf, ksb], axis=1).astype(BF16))
        a_scr[ci] = jnp.broadcast_to(
            jnp.concatenate([jnp.exp(endf), jnp.exp(endb)], axis=1), (8, 2 * LANES))

    for c in range(n_ctx):
        sl = slice(c * L, (c + 1) * L)
        bf, bb = cums(lffc_ref[sl, :], lfbc_ref[sl, :])
        state_part(c, kfc_ref[sl, :].astype(F32), kbc_ref[sl, :].astype(F32),
                   vc_ref[sl, :], bf, bb)

    def lat_chunk(c, carry):
        rows = pl.ds(pl.multiple_of(c * L, L), L)
        q = q_ref[rows, :].astype(F32)
        kf = kf_ref[rows, :].astype(F32)
        kb = kb_ref[rows, :].astype(F32)
        v_blk = v_ref[rows, :]
        lff = lff_ref[rows, :]
        lfb = lfb_ref[rows, :]
        bf, bb = cums(lff, lfb)
        state_part(n_ctx + c, kf, kb, v_blk, bf, bb)
        qs_scr[rows, :] = jnp.concatenate([q * jnp.exp(bf), q * jnp.exp(bb)],
                                          axis=1).astype(BF16)
        bf_scr[...] = bf
        bb_scr[...] = bb

        acc = jnp.zeros((L, L), F32)
        for lvl, h in enumerate(_LEVELS):
            bit = (row & h) != 0
            if h >= 4:
                n = (L // 2) // h
                if n == 1:
                    rf = bf_scr[h - 1:h, :]
                    rb = bb_scr[h:h + 1, :]
                else:
                    rf = bf_scr[pl.ds(h - 1, n, stride=2 * h), :]
                    rb = bb_scr[pl.ds(h, n, stride=2 * h), :]
                rf = _expand_rows(rf, 2 * h)
                rb = _expand_rows(rb, 2 * h)
                ef = jnp.where(bit, bf - rf, rf - bf)
                eb = jnp.where(bit, rb - bb, bb - rb)
            elif h == 2:
                m = row & 3
                ef = jnp.where(m == 2, lff,
                               jnp.where(m == 3, lff + pltpu.roll(lff, 1, axis=0),
                                         jnp.where(m == 0, pltpu.roll(lff, L - 1, axis=0), 0.0)))
                eb = jnp.where(m == 0, lfb + pltpu.roll(lfb, L - 1, axis=0),
                               jnp.where(m == 1, lfb,
                                         jnp.where(m == 3, pltpu.roll(lfb, 1, axis=0), 0.0)))
            else:
                ef = jnp.where(bit, lff, 0.0)
                eb = jnp.where(bit, 0.0, lfb)
            xf = jnp.exp(ef)
            xb = jnp.exp(eb)
            q2 = jnp.concatenate([jnp.where(bit, q * xf, 0.0),
                                  jnp.where(bit, 0.0, q * xb)], axis=1).astype(BF16)
            k2 = jnp.concatenate([jnp.where(bit, 0.0, kf * xf),
                                  jnp.where(bit, kb * xb, 0.0)], axis=1).astype(BF16)
            s = _dot_nt(q2, k2)
            shift = 6 - lvl
            acc = jnp.where((xr >> shift) == 1, s, acc)

        dsum = jnp.sum(q * (kf + kb), axis=-1, keepdims=True)
        oi_scr[rows, :] = _dot(acc.astype(BF16), v_blk) + dsum * v_blk.astype(F32)
        return carry
    lax.fori_loop(0, n_lat, lat_chunk, 0)

    st = jnp.zeros((LANES, LANES), F32)
    for c in range(n_ctx):
        st = st * a_scr[c, 0:1, 0:LANES] + ut_scr[c, :, 0:LANES]
    stb = jnp.zeros((LANES, LANES), F32)
    for c in reversed(range(n_ctx)):
        stb = stb * a_scr[c, 0:1, LANES:2 * LANES] + ut_scr[c, :, LANES:2 * LANES]

    def fwd(c, st):
        sin_scr[c, :, 0:LANES] = st.astype(BF16)
        ci = n_ctx + c
        return st * a_scr[ci, 0:1, 0:LANES] + ut_scr[ci, :, 0:LANES]
    lax.fori_loop(0, n_lat, fwd, st)

    def bwd(i, stb):
        c = n_lat - 1 - i
        sin_scr[c, :, LANES:2 * LANES] = stb.astype(BF16)
        ci = n_ctx + c
        return stb * a_scr[ci, 0:1, LANES:2 * LANES] + ut_scr[ci, :, LANES:2 * LANES]
    lax.fori_loop(0, n_lat, bwd, stb)

    def out_chunk(c, carry):
        rows = pl.ds(pl.multiple_of(c * L, L), L)
        o = oi_scr[rows, :] + _dot_nt(qs_scr[rows, :], sin_scr[c])
        ms = jnp.mean(o * o, axis=-1, keepdims=True)
        y = o * lax.rsqrt(ms + EPS) * nw_ref[...] * g_ref[rows, :].astype(F32)
        o_ref[rows, :] = y.astype(BF16)
        return carry
    lax.fori_loop(0, n_lat, out_chunk, 0)


def _hgrn(pb, lf, pbc, lfc, nw):
    B, n, _ = pb.shape
    nc = pbc.shape[1]
    n_lat, n_ctx = n // CHUNK, nc // CHUNK

    def colblk(rows, col0):
        return pl.BlockSpec((None, rows, LANES), lambda b, h: (b, 0, col0 // LANES + h))

    return pl.pallas_call(
        functools.partial(_hgrn_kernel, n_lat=n_lat, n_ctx=n_ctx),
        out_shape=jax.ShapeDtypeStruct((B, n, 512), BF16),
        grid=(B, HG_HEADS),
        in_specs=[colblk(n, C_HQ), colblk(n, C_FF), colblk(n, C_FB), colblk(n, C_HV),
                  colblk(n, C_HG), colblk(n, 0), colblk(n, 512),
                  colblk(nc, C_FF), colblk(nc, C_FB), colblk(nc, C_HV),
                  colblk(nc, 0), colblk(nc, 512),
                  pl.BlockSpec((1, LANES), lambda b, h: (0, 0))],
        out_specs=pl.BlockSpec((None, n, LANES), lambda b, h: (b, 0, h)),
        scratch_shapes=[pltpu.VMEM((n_lat + n_ctx, LANES, 2 * LANES), F32),
                        pltpu.VMEM((n_lat + n_ctx, 8, 2 * LANES), F32),
                        pltpu.VMEM((n, 2 * LANES), BF16),
                        pltpu.VMEM((n, LANES), F32),
                        pltpu.VMEM((n_lat, LANES, 2 * LANES), BF16),
                        pltpu.VMEM((CHUNK, LANES), F32),
                        pltpu.VMEM((CHUNK, LANES), F32)],
        compiler_params=_cparams(("arbitrary", "arbitrary")),
        name="hgrn2",
    )(pb, pb, pb, pb, pb, lf, lf, pbc, pbc, pbc, lfc, lfc, nw)


def _outproj_kernel(ar_ref, ah_ref, w_ref, x_ref, g1_ref, sc_ref, sh_ref, nw_ref,
                    rw_ref, rb_ref, h1_ref, v_ref, route_ref, cnt_ref, cnt_scr):
    first_step = jnp.logical_and(pl.program_id(0) == 0, pl.program_id(1) == 0)

    @pl.when(first_step)
    def _():
        cnt_scr[...] = jnp.zeros_like(cnt_scr)

    y = _dot(ar_ref[...], w_ref[0:512, :]) + _dot(ah_ref[...], w_ref[512:1024, :])
    h1 = x_ref[...] + g1_ref[...] * y
    h1_ref[...] = h1
    ms = jnp.mean(h1 * h1, axis=-1, keepdims=True)
    v = h1 * lax.rsqrt(ms + EPS) * nw_ref[...] * (1.0 + sc_ref[...]) + sh_ref[...]
    v_ref[...] = v

    tm = v.shape[0]
    logits = _dot3(v, rw_ref[...]) + rb_ref[...]
    lane = lax.broadcasted_iota(I32, (tm, LANES), 1)
    lane_f = lane.astype(F32)
    l = logits
    sels, tops, idxs = [], [], []
    for _ in range(TOP_K):
        m = jnp.max(l, axis=1, keepdims=True)
        i = jnp.min(jnp.where(l == m, lane_f, float(LANES)), axis=1, keepdims=True)
        sel = lane_f == i
        l = jnp.where(sel, -jnp.inf, l)
        sels.append(sel)
        tops.append(m)
        idxs.append(i)
    es = [jnp.exp(t - tops[0]) for t in tops]
    den = es[0] + es[1] + es[2] + es[3]
    gates = [e / den for e in es]

    oh = jnp.zeros((tm, LANES), F32)
    for sel in sels:
        oh = jnp.where(sel, 1.0, oh)
    r = lax.broadcasted_iota(I32, (tm, tm), 0)
    c = lax.broadcasted_iota(I32, (tm, tm), 1)
    tri = jnp.where(c < r, 1.0, 0.0).astype(BF16)
    before = _dot(tri, oh.astype(BF16)) + cnt_scr[0:1, :]
    ranks = [jnp.sum(jnp.where(sel, before, 0.0), axis=1, keepdims=True) for sel in sels]
    cnt_scr[...] = cnt_scr[...] + jnp.sum(oh, axis=0, keepdims=True)
    cnt_ref[...] = cnt_scr[...]

    out = jnp.zeros((tm, LANES), F32)
    for k in range(TOP_K):
        out = jnp.where(lane == k, idxs[k], out)
        out = jnp.where(lane == TOP_K + k, gates[k], out)
        out = jnp.where(lane == 2 * TOP_K + k, ranks[k], out)
    route_ref[...] = out


def _outproj(a_ret, a_hg, w_bf, x, g1, sc2, sh2, nw, rw, rb, *, tm):
    B, n, _ = x.shape
    nt = n // tm
    T = B * n
    mrow = lambda b, j: (b, 0, 0)
    tok = lambda b, j: (b * nt + j, 0)
    return pl.pallas_call(
        _outproj_kernel,
        out_shape=(jax.ShapeDtypeStruct((T, D_MODEL), F32),
                   jax.ShapeDtypeStruct((T, D_MODEL), F32),
                   jax.ShapeDtypeStruct((T, LANES), F32),
                   jax.ShapeDtypeStruct((8, LANES), F32)),
        grid=(B, nt),
        in_specs=[pl.BlockSpec((None, tm, 512), lambda b, j: (b, j, 0)),
                  pl.BlockSpec((None, tm, 512), lambda b, j: (b, j, 0)),
                  pl.BlockSpec((D_MODEL, D_MODEL), lambda b, j: (0, 0)),
                  pl.BlockSpec((None, tm, D_MODEL), lambda b, j: (b, j, 0)),
                  pl.BlockSpec((None, 1, D_MODEL), mrow),
                  pl.BlockSpec((None, 1, D_MODEL), mrow),
                  pl.BlockSpec((None, 1, D_MODEL), mrow),
                  pl.BlockSpec((1, D_MODEL), lambda b, j: (0, 0)),
                  pl.BlockSpec((D_MODEL, LANES), lambda b, j: (0, 0)),
                  pl.BlockSpec((1, LANES), lambda b, j: (0, 0))],
        out_specs=(pl.BlockSpec((tm, D_MODEL), tok),
                   pl.BlockSpec((tm, D_MODEL), tok),
                   pl.BlockSpec((tm, LANES), tok),
                   pl.BlockSpec((8, LANES), lambda b, j: (0, 0))),
        scratch_shapes=[pltpu.VMEM((8, LANES), F32)],
        compiler_params=_cparams(("arbitrary", "arbitrary")),
        name="outproj_router",
    )(a_ret, a_hg, w_bf, x, g1, sc2, sh2, nw, rw, rb)


def _dispatch_kernel(meta_ref, dest_hbm, v_ref, xs_hbm, idx_smem, zbuf, sem_idx, sem_z, sem_rows,
                     *, tile, n_blocks):
    i = pl.program_id(0)
    n_asg = tile * TOP_K
    idx_cp = pltpu.make_async_copy(dest_hbm.at[i], idx_smem, sem_idx)
    idx_cp.start()

    def zero_copy(r0):
        return pltpu.make_async_copy(zbuf, xs_hbm.at[pl.ds(r0, ROW_BLOCK), :], sem_z)

    @pl.when(i == 0)
    def _():
        zbuf[...] = jnp.zeros_like(zbuf)
        n_used = meta_ref[2 * N_EXPERTS]

        def tail_start(e):
            return pl.multiple_of(meta_ref[e] - ROW_BLOCK, ROW_BLOCK)

        def has_tail(e):
            return (meta_ref[N_EXPERTS + e] & (ROW_BLOCK - 1)) != 0

        def z_issue(e, carry):
            @pl.when(has_tail(e))
            def _():
                zero_copy(tail_start(e)).start()
            return carry
        lax.fori_loop(0, N_EXPERTS, z_issue, 0)

        def u_issue(blk, carry):
            @pl.when(blk >= n_used)
            def _():
                zero_copy(pl.multiple_of(blk * ROW_BLOCK, ROW_BLOCK)).start()
            return carry
        lax.fori_loop(0, n_blocks, u_issue, 0)

        def z_wait(e, carry):
            @pl.when(has_tail(e))
            def _():
                zero_copy(0).wait()
            return carry
        lax.fori_loop(0, N_EXPERTS, z_wait, 0)

        def u_wait(blk, carry):
            @pl.when(blk >= n_used)
            def _():
                zero_copy(0).wait()
            return carry
        lax.fori_loop(0, n_blocks, u_wait, 0)

    idx_cp.wait()

    def issue(a, carry):
        t = lax.shift_right_logical(a, 2)
        d = idx_smem[a]
        pltpu.make_async_copy(v_ref.at[pl.ds(t, 1), :], xs_hbm.at[pl.ds(d, 1), :], sem_rows).start()
        return carry
    lax.fori_loop(0, n_asg, issue, 0, unroll=8)
    for _ in range(TOP_K):
        pltpu.make_async_copy(v_ref, xs_hbm.at[pl.ds(0, tile), :], sem_rows).wait()


def _dispatch(meta, dest_tiles, v, *, tile, n_blocks):
    T = v.shape[0]
    R = n_blocks * ROW_BLOCK
    return pl.pallas_call(
        functools.partial(_dispatch_kernel, tile=tile, n_blocks=n_blocks),
        out_shape=jax.ShapeDtypeStruct((R, D_MODEL), F32),
        grid_spec=pltpu.PrefetchScalarGridSpec(
            num_scalar_prefetch=1,
            grid=(T // tile,),
            in_specs=[pl.BlockSpec(memory_space=pl.ANY),
                      pl.BlockSpec((tile, D_MODEL), lambda i, meta: (i, 0))],
            out_specs=pl.BlockSpec(memory_space=pl.ANY),
            scratch_shapes=[pltpu.SMEM((tile * TOP_K,), I32),
                            pltpu.VMEM((ROW_BLOCK, D_MODEL), F32),
                            pltpu.SemaphoreType.DMA,
                            pltpu.SemaphoreType.DMA,
                            pltpu.SemaphoreType.DMA]),
        compiler_params=_cparams(("arbitrary",)),
        name="dispatch",
    )(meta, dest_tiles, v)


def _moe_kernel(be_ref, nu_ref, x_ref, w1_ref, b1_ref, w2_ref, b2_ref, y_ref, w1_scr, w2_scr):
    i = pl.program_id(0)
    prev = be_ref[jnp.maximum(i - 1, 0)]
    changed = jnp.logical_or(i == 0, be_ref[i] != prev)

    @pl.when(changed)
    def _():
        w1_scr[...] = w1_ref[...].astype(BF16)
        w2_scr[...] = w2_ref[...].astype(BF16)

    @pl.when(i < nu_ref[0])
    def _():
        xb = x_ref[...].astype(BF16)
        acc = jnp.zeros((ROW_BLOCK, D_MODEL), F32) + b2_ref[...]
        cw = 512
        for c in range(D_FF // cw):
            glu = _dot(xb, w1_scr[:, c * cw:(c + 1) * cw]) + b1_ref[:, c * cw:(c + 1) * cw]
            lin = (_dot(xb, w1_scr[:, D_FF + c * cw:D_FF + (c + 1) * cw])
                   + b1_ref[:, D_FF + c * cw:D_FF + (c + 1) * cw])
            glu = jnp.minimum(glu, SWIGLU_LIMIT)
            lin = jnp.clip(lin, -SWIGLU_LIMIT, SWIGLU_LIMIT)
            act = glu * jax.nn.sigmoid(SWIGLU_ALPHA * glu) * (lin + 1.0)
            acc = acc + _dot(act.astype(BF16), w2_scr[c * cw:(c + 1) * cw, :])
        y_ref[...] = acc

    @pl.when(i >= nu_ref[0])
    def _():
        y_ref[...] = jnp.zeros_like(y_ref)


def _moe(block_e, n_used, xs, w1, b1, w2, b2, *, n_blocks):
    R = xs.shape[0]
    return pl.pallas_call(
        _moe_kernel,
        out_shape=jax.ShapeDtypeStruct((R, D_MODEL), F32),
        grid_spec=pltpu.PrefetchScalarGridSpec(
            num_scalar_prefetch=2,
            grid=(n_blocks,),
            in_specs=[pl.BlockSpec((ROW_BLOCK, D_MODEL), lambda i, be, nu: (i, 0)),
                      pl.BlockSpec((None, D_MODEL, 2 * D_FF), lambda i, be, nu: (be[i], 0, 0)),
                      pl.BlockSpec((None, 1, 2 * D_FF), lambda i, be, nu: (be[i], 0, 0)),
                      pl.BlockSpec((None, D_FF, D_MODEL), lambda i, be, nu: (be[i], 0, 0)),
                      pl.BlockSpec((None, 1, D_MODEL), lambda i, be, nu: (be[i], 0, 0))],
            out_specs=pl.BlockSpec((ROW_BLOCK, D_MODEL), lambda i, be, nu: (i, 0)),
            scratch_shapes=[pltpu.VMEM((D_MODEL, 2 * D_FF), BF16),
                            pltpu.VMEM((D_FF, D_MODEL), BF16)]),
        compiler_params=_cparams(("arbitrary",)),
        name="moe_ffn",
    )(block_e, n_used, xs, w1, b1, w2, b2)


def _combine_kernel(dest_hbm, yb_hbm, route_ref, h1_ref, g2_ref, nw_ref, o_ref,
                    idx_smem, rows, sem_idx, sem_rows, *, tile):
    i = pl.program_id(0)
    n_asg = tile * TOP_K
    idx_cp = pltpu.make_async_copy(dest_hbm.at[i], idx_smem, sem_idx)
    idx_cp.start()
    idx_cp.wait()

    def issue(a, carry):
        t = lax.shift_right_logical(a, 2)
        k = a & (TOP_K - 1)
        d = idx_smem[a]
        pltpu.make_async_copy(yb_hbm.at[pl.ds(d, 1), :], rows.at[k, pl.ds(t, 1), :],
                              sem_rows).start()
        return carry
    lax.fori_loop(0, n_asg, issue, 0, unroll=8)
    for k in range(TOP_K):
        pltpu.make_async_copy(yb_hbm.at[pl.ds(0, tile), :], rows.at[k], sem_rows).wait()

    route = route_ref[...]
    y = jnp.zeros((tile, D_MODEL), F32)
    for k in range(TOP_K):
        y = y + route[:, TOP_K + k:TOP_K + k + 1] * rows[k]
    h = h1_ref[...] + g2_ref[...] * y
    ms = jnp.mean(h * h, axis=-1, keepdims=True)
    o_ref[...] = h * lax.rsqrt(ms + EPS) * nw_ref[...]


def _combine(dest_tiles, yb, route, h1, g2, nw, *, tile, tiles_per_batch):
    T = h1.shape[0]
    return pl.pallas_call(
        functools.partial(_combine_kernel, tile=tile),
        out_shape=jax.ShapeDtypeStruct((T, D_MODEL), F32),
        grid=(T // tile,),
        in_specs=[pl.BlockSpec(memory_space=pl.ANY),
                  pl.BlockSpec(memory_space=pl.ANY),
                  pl.BlockSpec((tile, LANES), lambda i: (i, 0)),
                  pl.BlockSpec((tile, D_MODEL), lambda i: (i, 0)),
                  pl.BlockSpec((None, 1, D_MODEL), lambda i: (i // tiles_per_batch, 0, 0)),
                  pl.BlockSpec((1, D_MODEL), lambda i: (0, 0))],
        out_specs=pl.BlockSpec((tile, D_MODEL), lambda i: (i, 0)),
        scratch_shapes=[pltpu.SMEM((tile * TOP_K,), I32),
                        pltpu.VMEM((TOP_K, tile, D_MODEL), F32),
                        pltpu.SemaphoreType.DMA,
                        pltpu.SemaphoreType.DMA],
        compiler_params=_cparams(("arbitrary",)),
        name="combine_norm",
    )(dest_tiles, yb, route, h1, g2, nw)


def _rope_tables(n):
    rows = n // GRID_W
    row = jnp.repeat(jnp.arange(rows, dtype=F32), GRID_W)
    col = jnp.tile(jnp.arange(GRID_W, dtype=F32), rows)
    n_freq = RET_DK // 4
    inv = ROPE_BASE ** (-jnp.arange(n_freq, dtype=F32) / n_freq)
    ang = jnp.concatenate([row[:, None] * inv, col[:, None] * inv], axis=-1)
    cos, sin = jnp.cos(ang), jnp.sin(ang)
    cos_h = jnp.concatenate([cos, cos], axis=-1)
    sin_h = jnp.concatenate([-sin, sin], axis=-1)
    return jnp.tile(cos_h, (1, RET_HEADS)), jnp.tile(sin_h, (1, RET_HEADS))


def kernel(x, c, ctx, c_ctx, w_ada, b_ada, norm_mix_w, norm_ffn_w, w_in, w_out, hg_lb,
           hg_norm_w, router_w, router_b, w1, b1, w2, b2, norm_final_w):
    B, N, D = x.shape
    C = ctx.shape[1]
    T = B * N
    assert D == D_MODEL and w_ada.shape[0] == 1

    cc = jnp.concatenate([c.astype(F32), c_ctx.astype(F32)[None, :],
                          jnp.zeros((16 - B - 1, D), F32)], axis=0)
    mod = _ada(cc, w_ada[0], b_ada[0][None, :])
    mod = mod.reshape(16, 6, 1, D).transpose(1, 0, 2, 3)
    sh1, sc1, g1, sh2, sc2, g2 = (mod[i] for i in range(6))

    w_in_bf = w_in[0].astype(BF16)
    w_out_bf = w_out[0].astype(BF16)
    nw_mix = norm_mix_w[0][None, :]
    cos_t, sin_t = _rope_tables(N)

    pb, lf = _inproj(x, sc1, sh1, None, nw_mix, w_in_bf, hg_lb[:2], cos_t, sin_t,
                     rope=True, tm=512)
    pbc, lfc = _inproj(ctx, sc1, sh1, B, nw_mix, w_in_bf, hg_lb[:2], cos_t[:C], sin_t[:C],
                       rope=False, tm=C)

    a_ret = _retention(pb, pbc)
    a_hg = _hgrn(pb, lf, pbc, lfc, hg_norm_w[0][None, :])

    rw = jnp.zeros((D, LANES), F32).at[:, :N_EXPERTS].set(router_w[0])
    rb = jnp.full((1, LANES), -1e30, F32).at[0, :N_EXPERTS].set(router_b[0])
    h1, v, route, cnt = _outproj(a_ret, a_hg, w_out_bf, x, g1, sc2, sh2,
                                 norm_ffn_w[0][None, :], rw, rb, tm=512)

    idx = route[:, 0:TOP_K].astype(I32)
    rank = route[:, 2 * TOP_K:3 * TOP_K].astype(I32)
    counts = cnt[0, :N_EXPERTS].astype(I32)
    padded = (counts + ROW_BLOCK - 1) // ROW_BLOCK * ROW_BLOCK
    pend = jnp.cumsum(padded)
    pstart = pend - padded
    dest = (pstart[idx] + rank).reshape(-1)
    n_blocks = (T * TOP_K) // ROW_BLOCK + N_EXPERTS
    n_used = (pend[-1] // ROW_BLOCK).astype(I32)
    starts = jnp.arange(n_blocks, dtype=I32) * ROW_BLOCK
    block_e = jnp.minimum(jnp.sum((pend[None, :] <= starts[:, None]).astype(I32), axis=1),
                          N_EXPERTS - 1)
    meta = jnp.concatenate([pend.astype(I32), counts, n_used[None]])

    tile_d = 512
    xs = _dispatch(meta, dest.reshape(T // tile_d, tile_d * TOP_K), v,
                   tile=tile_d, n_blocks=n_blocks)
    yb = _moe(block_e, n_used[None], xs, w1[0], b1[0][:, None, :], w2[0], b2[0][:, None, :],
              n_blocks=n_blocks)
    tile_c = 256
    out = _combine(dest.reshape(T // tile_c, tile_c * TOP_K), yb, route, h1, g2,
                   norm_final_w[None, :], tile=tile_c, tiles_per_batch=N // tile_c)
    return out.reshape(B, N, D)
```

```python
import functools
import math

import jax
import jax.numpy as jnp
import numpy as np
from jax import lax
from jax.experimental import pallas as pl
from jax.experimental.pallas import tpu as pltpu

F32 = jnp.float32
BF16 = jnp.bfloat16
I32 = jnp.int32

D_MODEL = 1024
GRID_W = 64
RET_HEADS = 4
RET_DK = 64
HG_HEADS = 4
PROJ_W = 4096
ROPE_BASE = 10000.0
EPS = 1e-6
N_EXPERTS = 32
TOP_K = 4
D_FF = 1024
SWIGLU_LIMIT = 7.0
SWIGLU_ALPHA = 1.702

LANES = 128
CHUNK = 128
ROW_BLOCK = 512
VMEM_LIMIT = 56 * 1024 * 1024

C_RQ, C_RK, C_RV, C_RG, C_HQ, C_FF, C_FB, C_HV, C_HG = (
    0, 256, 512, 1024, 1536, 2048, 2560, 3072, 3584)


def _cparams(sem):
    return pltpu.CompilerParams(dimension_semantics=sem, vmem_limit_bytes=VMEM_LIMIT)


def _split_bf16(x):
    hi = x.astype(BF16)
    lo = (x - hi.astype(F32)).astype(BF16)
    return hi, lo


def _dot(a, b):
    return jnp.dot(a, b, preferred_element_type=F32)


def _dot_nt(a, b):
    return lax.dot_general(a, b, (((1,), (1,)), ((), ())), preferred_element_type=F32)


def _dot3(a, b):
    ah, al = _split_bf16(a)
    bh, bl = _split_bf16(b)
    return _dot(ah, bh) + (_dot(ah, bl) + _dot(al, bh))


def _silu(x):
    return x * jax.nn.sigmoid(x)


def _ada_kernel(c_ref, w_ref, b_ref, o_ref):
    s = _silu(c_ref[...])
    o_ref[...] = _dot3(s, w_ref[...]) + b_ref[...]


def _ada(cc, w, b):
    nblk = w.shape[1] // D_MODEL
    return pl.pallas_call(
        _ada_kernel,
        out_shape=jax.ShapeDtypeStruct((cc.shape[0], w.shape[1]), F32),
        grid=(nblk,),
        in_specs=[pl.BlockSpec(cc.shape, lambda j: (0, 0)),
                  pl.BlockSpec((D_MODEL, D_MODEL), lambda j: (0, j)),
                  pl.BlockSpec((1, D_MODEL), lambda j: (0, j))],
        out_specs=pl.BlockSpec((cc.shape[0], D_MODEL), lambda j: (0, j)),
        compiler_params=_cparams(("arbitrary",)),
        name="ada",
    )(cc, w, b)


def _inproj_kernel(x_ref, sc_ref, sh_ref, nw_ref, w_ref, lb_ref, cos_ref, sin_ref,
                   pb_ref, lf_ref, u_scr, *, rope):
    x = x_ref[...]
    ms = jnp.mean(x * x, axis=-1, keepdims=True)
    u = x * lax.rsqrt(ms + EPS) * nw_ref[...] * (1.0 + sc_ref[...]) + sh_ref[...]
    u_scr[...] = u.astype(BF16)

    def proj(lo, width):
        return _dot(u_scr[...], w_ref[:, lo:lo + width])

    tm = x.shape[0]
    if rope:
        lane = lax.broadcasted_iota(I32, (tm, LANES), 1)
        first = (lane & 32) == 0

    def put_rot(col, scale):
        for j in range(2):
            lo = col + j * LANES
            t = proj(lo, LANES)
            if scale != 1.0:
                t = t * scale
            if rope:
                tb = j * LANES
                rot = jnp.where(first, pltpu.roll(t, 96, axis=1), pltpu.roll(t, 32, axis=1))
                t = t * cos_ref[:, tb:tb + LANES] + rot * sin_ref[:, tb:tb + LANES]
            pb_ref[:, lo:lo + LANES] = t.astype(BF16)

    put_rot(C_RQ, 1.0)
    put_rot(C_RK, RET_DK ** -0.5)
    pb_ref[:, C_RV:C_RV + 512] = proj(C_RV, 512).astype(BF16)
    pb_ref[:, C_RG:C_RG + 512] = _silu(proj(C_RG, 512)).astype(BF16)
    pb_ref[:, C_HQ:C_HQ + 512] = _silu(proj(C_HQ, 512)).astype(BF16)
    pb_ref[:, C_HV:C_HV + 512] = proj(C_HV, 512).astype(BF16)
    pb_ref[:, C_HG:C_HG + 512] = _silu(proj(C_HG, 512)).astype(BF16)

    la = lb_ref[0]
    lbb = lb_ref[1]
    mx = jnp.maximum(la, lbb)
    ea = jnp.exp(la - mx)
    eb = jnp.exp(lbb - mx)
    lb = ea / (ea + eb)
    for d, col in enumerate((C_FF, C_FB)):
        lbd = lb[d:d + 1, :]
        f = lbd + (1.0 - lbd) * jax.nn.sigmoid(proj(col, 512))
        pb_ref[:, col:col + 512] = (1.0 - f).astype(BF16)
        lf_ref[:, d * 512:(d + 1) * 512] = jnp.log(f)


def _inproj(x, sc, sh, mod_row, nw, w_bf, hg_lb, cos_t, sin_t, *, rope, tm):
    B, n, _ = x.shape
    nt = n // tm
    if mod_row is None:
        mrow = lambda b, j: (b, 0, 0)
    else:
        mrow = lambda b, j: (mod_row, 0, 0)
    return pl.pallas_call(
        functools.partial(_inproj_kernel, rope=rope),
        out_shape=(jax.ShapeDtypeStruct((B, n, PROJ_W), BF16),
                   jax.ShapeDtypeStruct((B, n, 1024), F32)),
        grid=(B, nt),
        in_specs=[pl.BlockSpec((None, tm, D_MODEL), lambda b, j: (b, j, 0)),
                  pl.BlockSpec((None, 1, D_MODEL), mrow),
                  pl.BlockSpec((None, 1, D_MODEL), mrow),
                  pl.BlockSpec((1, D_MODEL), lambda b, j: (0, 0)),
                  pl.BlockSpec((D_MODEL, PROJ_W), lambda b, j: (0, 0)),
                  pl.BlockSpec((2, 2, 512), lambda b, j: (0, 0, 0)),
                  pl.BlockSpec((tm, 256), lambda b, j: (j, 0)),
                  pl.BlockSpec((tm, 256), lambda b, j: (j, 0))],
        out_specs=(pl.BlockSpec((None, tm, PROJ_W), lambda b, j: (b, j, 0)),
                   pl.BlockSpec((None, tm, 1024), lambda b, j: (b, j, 0))),
        scratch_shapes=[pltpu.VMEM((tm, D_MODEL), BF16)],
        compiler_params=_cparams(("arbitrary", "arbitrary")),
        name="inproj_rope" if rope else "inproj_ctx",
    )(x, sc, sh, nw, w_bf, hg_lb, cos_t, sin_t)


_RET_LGF = [math.log1p(-(2.0 ** (-5.0 - 2.0 * h))) for h in range(RET_HEADS)]
_RET_LGB = [math.log1p(-(2.0 ** (-6.0 - 2.0 * h))) for h in range(RET_HEADS)]


def _ret_kernel(q_ref, k_ref, v_ref, g_ref, kc_ref, vc_ref, o_ref,
                u_scr, sin_scr, kt_scr, dtot_scr, *, n_lat, n_ctx):
    L = CHUNK
    pair = pl.program_id(1)
    row = lax.broadcasted_iota(I32, (L, L), 0).astype(F32)
    col = lax.broadcasted_iota(I32, (L, L), 1).astype(F32)
    lane = lax.broadcasted_iota(I32, (L, LANES), 1)
    trow = lax.broadcasted_iota(I32, (L, 1), 0).astype(F32)
    tcol = lax.broadcasted_iota(I32, (1, L), 1).astype(F32)
    low_half = lane < RET_DK

    def u_chunk(k_blk, v_blk, ci, store_kt):
        kt = k_blk.astype(F32).T
        if store_kt is not None:
            kt_scr[store_kt] = kt.astype(BF16)
        for hh in range(2):
            lgf = jnp.where(pair == 0, _RET_LGF[hh], _RET_LGF[2 + hh])
            lgb = jnp.where(pair == 0, _RET_LGB[hh], _RET_LGB[2 + hh])
            kth = kt[hh * RET_DK:(hh + 1) * RET_DK, :]
            wkf = jnp.exp(lgf * (L - 1.0 - tcol))
            wkb = jnp.exp(lgb * tcol)
            lhs = jnp.concatenate([kth * wkf, kth * wkb], axis=0).astype(BF16)
            u_scr[hh, ci] = _dot(lhs, v_blk[:, hh * LANES:(hh + 1) * LANES])

    for c in range(n_ctx):
        u_chunk(kc_ref[c * L:(c + 1) * L, :], vc_ref[c * L:(c + 1) * L, :], c, None)

    def lat_u(c, carry):
        r0 = pl.multiple_of(c * L, L)
        u_chunk(k_ref[pl.ds(r0, L), :], v_ref[pl.ds(r0, L), :], n_ctx + c, c)
        return carry
    lax.fori_loop(0, n_lat, lat_u, 0, unroll=2)

    ones = jnp.ones((RET_DK, LANES), F32)
    for hh in range(2):
        lgf = jnp.where(pair == 0, _RET_LGF[hh], _RET_LGF[2 + hh])
        lgb = jnp.where(pair == 0, _RET_LGB[hh], _RET_LGB[2 + hh])
        d = row - col
        dtot_scr[hh] = jnp.where(d > 0, jnp.exp(lgf * jnp.maximum(d, 0.0)),
                                 jnp.where(d < 0, jnp.exp(lgb * jnp.maximum(-d, 0.0)), 2.0))
        af = jnp.exp(ones * (lgf * L))
        ab = jnp.exp(ones * (lgb * L))

        s = jnp.zeros((RET_DK, LANES), F32)
        for c in range(n_ctx):
            s = af * s + u_scr[hh, c, 0:RET_DK, :]
        sb = jnp.zeros((RET_DK, LANES), F32)
        for c in reversed(range(n_ctx)):
            sb = ab * sb + u_scr[hh, c, RET_DK:2 * RET_DK, :]

        def fwd(c, s, hh=hh, af=af):
            sin_scr[hh, c, 0:RET_DK, :] = s.astype(BF16)
            return af * s + u_scr[hh, n_ctx + c, 0:RET_DK, :]
        lax.fori_loop(0, n_lat, fwd, s)

        def bwd(i, sb, hh=hh, ab=ab):
            c = n_lat - 1 - i
            sin_scr[hh, c, RET_DK:2 * RET_DK, :] = sb.astype(BF16)
            return ab * sb + u_scr[hh, n_ctx + c, RET_DK:2 * RET_DK, :]
        lax.fori_loop(0, n_lat, bwd, sb)

    def out_chunk(c, carry):
        r0 = pl.multiple_of(c * L, L)
        q = q_ref[pl.ds(r0, L), :].astype(F32)
        qr = pltpu.roll(q, RET_DK, axis=1)
        kt = kt_scr[c]
        for hh in range(2):
            lgf = jnp.where(pair == 0, _RET_LGF[hh], _RET_LGF[2 + hh])
            lgb = jnp.where(pair == 0, _RET_LGB[hh], _RET_LGB[2 + hh])
            mine = low_half if hh == 0 else jnp.logical_not(low_half)
            qm = jnp.where(mine, q, 0.0).astype(BF16)
            p = (_dot(qm, kt) * dtot_scr[hh]).astype(BF16)
            vh = v_ref[pl.ds(r0, L), hh * LANES:(hh + 1) * LANES]
            wqf = jnp.exp(lgf * (trow + 1.0))
            wqb = jnp.exp(lgb * (L - trow))
            qa, qb = (q, qr) if hh == 0 else (qr, q)
            qs = jnp.where(low_half, qa * wqf, qb * wqb).astype(BF16)
            o = _dot(p, vh) + _dot(qs, sin_scr[hh, c])
            ms = jnp.mean(o * o, axis=-1, keepdims=True)
            gh = g_ref[pl.ds(r0, L), hh * LANES:(hh + 1) * LANES].astype(F32)
            o_ref[pl.ds(r0, L), hh * LANES:(hh + 1) * LANES] = (
                o * lax.rsqrt(ms + EPS) * gh).astype(BF16)
        return carry
    lax.fori_loop(0, n_lat, out_chunk, 0, unroll=2)


def _retention(pb, pbc):
    B, n, _ = pb.shape
    nc = pbc.shape[1]
    n_lat, n_ctx = n // CHUNK, nc // CHUNK
    return pl.pallas_call(
        functools.partial(_ret_kernel, n_lat=n_lat, n_ctx=n_ctx),
        out_shape=jax.ShapeDtypeStruct((B, n, 512), BF16),
        grid=(B, 2),
        in_specs=[pl.BlockSpec((None, n, LANES), lambda b, p: (b, 0, C_RQ // LANES + p)),
                  pl.BlockSpec((None, n, LANES), lambda b, p: (b, 0, C_RK // LANES + p)),
                  pl.BlockSpec((None, n, 256), lambda b, p: (b, 0, C_RV // 256 + p)),
                  pl.BlockSpec((None, n, 256), lambda b, p: (b, 0, C_RG // 256 + p)),
                  pl.BlockSpec((None, nc, LANES), lambda b, p: (b, 0, C_RK // LANES + p)),
                  pl.BlockSpec((None, nc, 256), lambda b, p: (b, 0, C_RV // 256 + p))],
        out_specs=pl.BlockSpec((None, n, 256), lambda b, p: (b, 0, p)),
        scratch_shapes=[pltpu.VMEM((2, n_lat + n_ctx, CHUNK, LANES), F32),
                        pltpu.VMEM((2, n_lat, CHUNK, LANES), BF16),
                        pltpu.VMEM((n_lat, LANES, CHUNK), BF16),
                        pltpu.VMEM((2, CHUNK, CHUNK), F32)],
        compiler_params=_cparams(("arbitrary", "arbitrary")),
        name="retention",
    )(pb, pb, pb, pb, pbc, pbc)


_LEVELS = (64, 32, 16, 8, 4, 2, 1)


def _expand_rows(r, rep):
    n = r.shape[0]
    if n == 1:
        return jnp.broadcast_to(r, (rep, r.shape[1]))
    return jnp.concatenate(
        [jnp.broadcast_to(r[i:i + 1, :], (rep, r.shape[1])) for i in range(n)], axis=0)


def _hgrn_kernel(q_ref, kf_ref, kb_ref, v_ref, g_ref, lff_ref, lfb_ref,
                 kfc_ref, kbc_ref, vc_ref, lffc_ref, lfbc_ref, nw_ref, o_ref,
                 ut_scr, a_scr, qs_scr, oi_scr, sin_scr, bfb_scr, *, n_lat, n_ctx):
    L = CHUNK
    row = lax.broadcasted_iota(I32, (L, L), 0)
    col = lax.broadcasted_iota(I32, (L, L), 1)
    xr_bits = lax.bitcast_convert_type((row ^ col).astype(F32), I32)
    lv = lax.shift_right_logical(xr_bits, 23) - 127
    row2 = lax.broadcasted_iota(I32, (L, 2 * L), 0)
    col2 = lax.broadcasted_iota(I32, (L, 2 * L), 1) & (L - 1)
    tril2 = jnp.where(col2 <= row2, 1.0, 0.0).astype(BF16)
    triu2 = jnp.where(col2 >= row2, 1.0, 0.0).astype(BF16)

    def cums(lff, lfb):
        hf, lof = _split_bf16(lff)
        hb, lob = _split_bf16(lfb)
        bf = _dot(tril2, jnp.concatenate([hf, lof], axis=0))
        bb = _dot(triu2, jnp.concatenate([hb, lob], axis=0))
        return bf, bb

    def state_part(ci, kf, kb, v_blk, bf, bb):
        endf = bf[L - 1:L, :]
        endb = bb[0:1, :]
        ksf = kf * jnp.exp(endf - bf)
        ksb = kb * jnp.exp(endb - bb)
        vt = v_blk.astype(F32).T.astype(BF16)
        ut_scr[ci] = _dot(vt, jnp.concatenate([ksf, ksb], axis=1).astype(BF16))
        a_scr[ci] = jnp.broadcast_to(
            jnp.concatenate([jnp.exp(endf), jnp.exp(endb)], axis=1), (8, 2 * LANES))

    for c in range(n_ctx):
        sl = slice(c * L, (c + 1) * L)
        bf, bb = cums(lffc_ref[sl, :], lfbc_ref[sl, :])
        state_part(c, kfc_ref[sl, :].astype(F32), kbc_ref[sl, :].astype(F32),
                   vc_ref[sl, :], bf, bb)

    def lat_chunk(c, u):
        rows = pl.ds(pl.multiple_of(c * L, L), L)
        bf_scr = bfb_scr.at[u, 0]
        bb_scr = bfb_scr.at[u, 1]
        q = q_ref[rows, :].astype(F32)
        kf = kf_ref[rows, :].astype(F32)
        kb = kb_ref[rows, :].astype(F32)
        v_blk = v_ref[rows, :]
        lff = lff_ref[rows, :]
        lfb = lfb_ref[rows, :]
        bf, bb = cums(lff, lfb)
        state_part(n_ctx + c, kf, kb, v_blk, bf, bb)
        qs_scr[rows, :] = jnp.concatenate([q * jnp.exp(bf), q * jnp.exp(bb)],
                                          axis=1).astype(BF16)
        bf_scr[...] = bf
        bb_scr[...] = bb

        acc = jnp.zeros((L, L), F32)
        for lvl, h in enumerate(_LEVELS):
            bit = (row & h) != 0
            ksel = jnp.where(bit, kb, kf)
            if h >= 4:
                n = (L // 2) // h
                if n == 1:
                    rf = bf_scr[h - 1:h, :]
                    rb = bb_scr[h:h + 1, :]
                else:
                    rf = bf_scr[pl.ds(h - 1, n, stride=2 * h), :]
                    rb = bb_scr[pl.ds(h, n, stride=2 * h), :]
                df = bf - _expand_rows(rf, 2 * h)
                db = bb - _expand_rows(rb, 2 * h)
                eq = jnp.where(bit, df, db)
                ek = -jnp.where(bit, db, df)
            elif h == 2:
                m = row & 3
                lff_n = pltpu.roll(lff, L - 1, axis=0)
                lfb_n = pltpu.roll(lfb, L - 1, axis=0)
                eq = jnp.where(m == 2, lff,
                               jnp.where(m == 3, lff + pltpu.roll(lff, 1, axis=0),
                                         jnp.where(m == 0, lfb + lfb_n, lfb)))
                ek = jnp.where(m == 3, pltpu.roll(lfb, 1, axis=0),
                               jnp.where(m == 0, lff_n, 0.0))
            else:
                eq = jnp.where(bit, lff, lfb)
                ek = None
            lhs = (q * jnp.exp(eq)).astype(BF16)
            rhs = (ksel if ek is None else ksel * jnp.exp(ek)).astype(BF16)
            acc = jnp.where(lv == 6 - lvl, _dot_nt(lhs, rhs), acc)

        dsum = jnp.sum(q * (kf + kb), axis=-1, keepdims=True)
        oi_scr[rows, :] = _dot(acc.astype(BF16), v_blk) + dsum * v_blk.astype(F32)

    def lat_pair(i, carry):
        for u in range(2):
            lat_chunk(2 * i + u, u)
        return carry
    lax.fori_loop(0, n_lat // 2, lat_pair, 0)

    st = jnp.zeros((LANES, LANES), F32)
    for c in range(n_ctx):
        st = st * a_scr[c, 0:1, 0:LANES] + ut_scr[c, :, 0:LANES]
    stb = jnp.zeros((LANES, LANES), F32)
    for c in reversed(range(n_ctx)):
        stb = stb * a_scr[c, 0:1, LANES:2 * LANES] + ut_scr[c, :, LANES:2 * LANES]

    def fwd(c, st):
        sin_scr[c, :, 0:LANES] = st.astype(BF16)
        ci = n_ctx + c
        return st * a_scr[ci, 0:1, 0:LANES] + ut_scr[ci, :, 0:LANES]
    lax.fori_loop(0, n_lat, fwd, st)

    def bwd(i, stb):
        c = n_lat - 1 - i
        sin_scr[c, :, LANES:2 * LANES] = stb.astype(BF16)
        ci = n_ctx + c
        return stb * a_scr[ci, 0:1, LANES:2 * LANES] + ut_scr[ci, :, LANES:2 * LANES]
    lax.fori_loop(0, n_lat, bwd, stb)

    def out_chunk(c, carry):
        rows = pl.ds(pl.multiple_of(c * L, L), L)
        o = oi_scr[rows, :] + _dot_nt(qs_scr[rows, :], sin_scr[c])
        ms = jnp.mean(o * o, axis=-1, keepdims=True)
        y = o * lax.rsqrt(ms + EPS) * nw_ref[...] * g_ref[rows, :].astype(F32)
        o_ref[rows, :] = y.astype(BF16)
        return carry
    lax.fori_loop(0, n_lat, out_chunk, 0, unroll=4)


def _hgrn(pb, lf, pbc, lfc, nw):
    B, n, _ = pb.shape
    nc = pbc.shape[1]
    n_lat, n_ctx = n // CHUNK, nc // CHUNK

    def colblk(rows, col0):
        return pl.BlockSpec((None, rows, LANES), lambda b, h: (b, 0, col0 // LANES + h))

    return pl.pallas_call(
        functools.partial(_hgrn_kernel, n_lat=n_lat, n_ctx=n_ctx),
        out_shape=jax.ShapeDtypeStruct((B, n, 512), BF16),
        grid=(B, HG_HEADS),
        in_specs=[colblk(n, C_HQ), colblk(n, C_FF), colblk(n, C_FB), colblk(n, C_HV),
                  colblk(n, C_HG), colblk(n, 0), colblk(n, 512),
                  colblk(nc, C_FF), colblk(nc, C_FB), colblk(nc, C_HV),
                  colblk(nc, 0), colblk(nc, 512),
                  pl.BlockSpec((1, LANES), lambda b, h: (0, 0))],
        out_specs=pl.BlockSpec((None, n, LANES), lambda b, h: (b, 0, h)),
        scratch_shapes=[pltpu.VMEM((n_lat + n_ctx, LANES, 2 * LANES), F32),
                        pltpu.VMEM((n_lat + n_ctx, 8, 2 * LANES), F32),
                        pltpu.VMEM((n, 2 * LANES), BF16),
                        pltpu.VMEM((n, LANES), F32),
                        pltpu.VMEM((n_lat, LANES, 2 * LANES), BF16),
                        pltpu.VMEM((2, 2, CHUNK, LANES), F32)],
        compiler_params=_cparams(("arbitrary", "arbitrary")),
        name="hgrn2",
    )(pb, pb, pb, pb, pb, lf, lf, pbc, pbc, pbc, lfc, lfc, nw)


def _outproj_kernel(ar_ref, ah_ref, w_ref, x_ref, g1_ref, sc_ref, sh_ref, nw_ref,
                    rw_ref, rb_ref, h1_ref, v_ref, route_ref, route_t_ref, cnt_ref, cnt_scr):
    first_step = jnp.logical_and(pl.program_id(0) == 0, pl.program_id(1) == 0)

    @pl.when(first_step)
    def _():
        cnt_scr[...] = jnp.zeros_like(cnt_scr)

    y = _dot(ar_ref[...], w_ref[0:512, :]) + _dot(ah_ref[...], w_ref[512:1024, :])
    h1 = x_ref[...] + g1_ref[...] * y
    h1_ref[...] = h1
    ms = jnp.mean(h1 * h1, axis=-1, keepdims=True)
    v = h1 * lax.rsqrt(ms + EPS) * nw_ref[...] * (1.0 + sc_ref[...]) + sh_ref[...]
    v_ref[...] = v

    tm = v.shape[0]
    logits = _dot3(v, rw_ref[...]) + rb_ref[...]
    lane = lax.broadcasted_iota(I32, (tm, LANES), 1)
    lane_f = lane.astype(F32)
    l = logits
    sels, tops, idxs = [], [], []
    for _ in range(TOP_K):
        m = jnp.max(l, axis=1, keepdims=True)
        i = jnp.min(jnp.where(l == m, lane_f, float(LANES)), axis=1, keepdims=True)
        sel = lane_f == i
        l = jnp.where(sel, -jnp.inf, l)
        sels.append(sel)
        tops.append(m)
        idxs.append(i)
    es = [jnp.exp(t - tops[0]) for t in tops]
    den = es[0] + es[1] + es[2] + es[3]
    gates = [e / den for e in es]

    oh = jnp.zeros((tm, LANES), F32)
    for sel in sels:
        oh = jnp.where(sel, 1.0, oh)
    r = lax.broadcasted_iota(I32, (tm, tm), 0)
    c = lax.broadcasted_iota(I32, (tm, tm), 1)
    tri = jnp.where(c < r, 1.0, 0.0).astype(BF16)
    before = _dot(tri, oh.astype(BF16)) + cnt_scr[0:1, :]
    ranks = [jnp.sum(jnp.where(sel, before, 0.0), axis=1, keepdims=True) for sel in sels]
    cnt_scr[...] = cnt_scr[...] + jnp.sum(oh, axis=0, keepdims=True)
    cnt_ref[...] = cnt_scr[...]

    out = jnp.zeros((tm, LANES), F32)
    for k in range(TOP_K):
        out = jnp.where(lane == k, idxs[k], out)
        out = jnp.where(lane == TOP_K + k, gates[k], out)
        out = jnp.where(lane == 2 * TOP_K + k, ranks[k], out)
    route_ref[...] = out
    route_t_ref[...] = out.T[0:16, :]


def _outproj(a_ret, a_hg, w_bf, x, g1, sc2, sh2, nw, rw, rb, *, tm):
    B, n, _ = x.shape
    nt = n // tm
    T = B * n
    mrow = lambda b, j: (b, 0, 0)
    tok = lambda b, j: (b * nt + j, 0)
    return pl.pallas_call(
        _outproj_kernel,
        out_shape=(jax.ShapeDtypeStruct((T, D_MODEL), F32),
                   jax.ShapeDtypeStruct((T, D_MODEL), F32),
                   jax.ShapeDtypeStruct((T, LANES), F32),
                   jax.ShapeDtypeStruct((16, T), F32),
                   jax.ShapeDtypeStruct((8, LANES), F32)),
        grid=(B, nt),
        in_specs=[pl.BlockSpec((None, tm, 512), lambda b, j: (b, j, 0)),
                  pl.BlockSpec((None, tm, 512), lambda b, j: (b, j, 0)),
                  pl.BlockSpec((D_MODEL, D_MODEL), lambda b, j: (0, 0)),
                  pl.BlockSpec((None, tm, D_MODEL), lambda b, j: (b, j, 0)),
                  pl.BlockSpec((None, 1, D_MODEL), mrow),
                  pl.BlockSpec((None, 1, D_MODEL), mrow),
                  pl.BlockSpec((None, 1, D_MODEL), mrow),
                  pl.BlockSpec((1, D_MODEL), lambda b, j: (0, 0)),
                  pl.BlockSpec((D_MODEL, LANES), lambda b, j: (0, 0)),
                  pl.BlockSpec((1, LANES), lambda b, j: (0, 0))],
        out_specs=(pl.BlockSpec((tm, D_MODEL), tok),
                   pl.BlockSpec((tm, D_MODEL), tok),
                   pl.BlockSpec((tm, LANES), tok),
                   pl.BlockSpec((16, tm), lambda b, j: (0, b * nt + j)),
                   pl.BlockSpec((8, LANES), lambda b, j: (0, 0))),
        scratch_shapes=[pltpu.VMEM((8, LANES), F32)],
        compiler_params=_cparams(("arbitrary", "arbitrary")),
        name="outproj_router",
    )(a_ret, a_hg, w_bf, x, g1, sc2, sh2, nw, rw, rb)


def _dispatch_kernel(meta_ref, dest_hbm, v_ref, xs_hbm, idx_smem, zbuf, sem_idx, sem_z, sem_rows,
                     *, tile, n_blocks, n_tiles):
    i = pl.program_id(0)
    slot = lax.rem(i, 2)

    def idx_copy(j, s):
        return pltpu.make_async_copy(dest_hbm.at[j], idx_smem.at[s], sem_idx.at[s])

    @pl.when(i == 0)
    def _():
        idx_copy(0, 0).start()

    def zero_copy(r0):
        return pltpu.make_async_copy(zbuf, xs_hbm.at[pl.ds(r0, ROW_BLOCK), :], sem_z)

    @pl.when(i == 0)
    def _():
        zbuf[...] = jnp.zeros_like(zbuf)
        n_used = meta_ref[2 * N_EXPERTS]

        def tail_start(e):
            return pl.multiple_of(meta_ref[e] - ROW_BLOCK, ROW_BLOCK)

        def has_tail(e):
            return (meta_ref[N_EXPERTS + e] & (ROW_BLOCK - 1)) != 0

        def z_issue(e, carry):
            @pl.when(has_tail(e))
            def _():
                zero_copy(tail_start(e)).start()
            return carry
        lax.fori_loop(0, N_EXPERTS, z_issue, 0)

        def u_issue(blk, carry):
            @pl.when(blk >= n_used)
            def _():
                zero_copy(pl.multiple_of(blk * ROW_BLOCK, ROW_BLOCK)).start()
            return carry
        lax.fori_loop(0, n_blocks, u_issue, 0)

        def z_wait(e, carry):
            @pl.when(has_tail(e))
            def _():
                zero_copy(0).wait()
            return carry
        lax.fori_loop(0, N_EXPERTS, z_wait, 0)

        def u_wait(blk, carry):
            @pl.when(blk >= n_used)
            def _():
                zero_copy(0).wait()
            return carry
        lax.fori_loop(0, n_blocks, u_wait, 0)

    idx_copy(i, slot).wait()

    @pl.when(i + 1 < n_tiles)
    def _():
        idx_copy(i + 1, 1 - slot).start()

    def issue(t, carry):
        for k in range(TOP_K):
            d = idx_smem[slot, k * tile + t]
            pltpu.make_async_copy(v_ref.at[pl.ds(t, 1), :], xs_hbm.at[pl.ds(d, 1), :],
                                  sem_rows).start(priority=k % 2)
        return carry
    lax.fori_loop(0, tile, issue, 0, unroll=2)
    for _ in range(TOP_K):
        pltpu.make_async_copy(v_ref, xs_hbm.at[pl.ds(0, tile), :], sem_rows).wait()


def _dispatch(meta, dest_tiles, v, *, tile, n_blocks):
    T = v.shape[0]
    R = n_blocks * ROW_BLOCK
    return pl.pallas_call(
        functools.partial(_dispatch_kernel, tile=tile, n_blocks=n_blocks, n_tiles=T // tile),
        out_shape=jax.ShapeDtypeStruct((R, D_MODEL), F32),
        grid_spec=pltpu.PrefetchScalarGridSpec(
            num_scalar_prefetch=1,
            grid=(T // tile,),
            in_specs=[pl.BlockSpec(memory_space=pl.ANY),
                      pl.BlockSpec((tile, D_MODEL), lambda i, meta: (i, 0))],
            out_specs=pl.BlockSpec(memory_space=pl.ANY),
            scratch_shapes=[pltpu.SMEM((2, tile * TOP_K), I32),
                            pltpu.VMEM((ROW_BLOCK, D_MODEL), F32),
                            pltpu.SemaphoreType.DMA((2,)),
                            pltpu.SemaphoreType.DMA,
                            pltpu.SemaphoreType.DMA]),
        compiler_params=_cparams(("arbitrary",)),
        name="dispatch",
    )(meta, dest_tiles, v)


def _moe_kernel(be_ref, nu_ref, x_ref, w1_ref, b1_ref, w2_ref, b2_ref, y_ref, w1_scr, w2_scr):
    i = pl.program_id(0)
    prev = be_ref[jnp.maximum(i - 1, 0)]
    changed = jnp.logical_or(i == 0, be_ref[i] != prev)

    @pl.when(changed)
    def _():
        w1_scr[...] = w1_ref[...].astype(BF16)
        w2_scr[...] = w2_ref[...].astype(BF16)

    @pl.when(i < nu_ref[0])
    def _():
        xb = x_ref[...].astype(BF16)
        acc = jnp.zeros((ROW_BLOCK, D_MODEL), F32) + b2_ref[...]
        cw = 512
        for c in range(D_FF // cw):
            glu = _dot(xb, w1_scr[:, c * cw:(c + 1) * cw]) + b1_ref[:, c * cw:(c + 1) * cw]
            lin = (_dot(xb, w1_scr[:, D_FF + c * cw:D_FF + (c + 1) * cw])
                   + b1_ref[:, D_FF + c * cw:D_FF + (c + 1) * cw])
            glu = jnp.minimum(glu, SWIGLU_LIMIT)
            lin = jnp.clip(lin, -SWIGLU_LIMIT, SWIGLU_LIMIT)
            act = glu * jax.nn.sigmoid(SWIGLU_ALPHA * glu) * (lin + 1.0)
            acc = acc + _dot(act.astype(BF16), w2_scr[c * cw:(c + 1) * cw, :])
        y_ref[...] = acc

    @pl.when(i >= nu_ref[0])
    def _():
        y_ref[...] = jnp.zeros_like(y_ref)


def _moe(block_e, n_used, xs, w1, b1, w2, b2, *, n_blocks):
    R = xs.shape[0]
    return pl.pallas_call(
        _moe_kernel,
        out_shape=jax.ShapeDtypeStruct((R, D_MODEL), F32),
        grid_spec=pltpu.PrefetchScalarGridSpec(
            num_scalar_prefetch=2,
            grid=(n_blocks,),
            in_specs=[pl.BlockSpec((ROW_BLOCK, D_MODEL), lambda i, be, nu: (i, 0)),
                      pl.BlockSpec((None, D_MODEL, 2 * D_FF), lambda i, be, nu: (be[i], 0, 0)),
                      pl.BlockSpec((None, 1, 2 * D_FF), lambda i, be, nu: (be[i], 0, 0)),
                      pl.BlockSpec((None, D_FF, D_MODEL), lambda i, be, nu: (be[i], 0, 0)),
                      pl.BlockSpec((None, 1, D_MODEL), lambda i, be, nu: (be[i], 0, 0))],
            out_specs=pl.BlockSpec((ROW_BLOCK, D_MODEL), lambda i, be, nu: (i, 0)),
            scratch_shapes=[pltpu.VMEM((D_MODEL, 2 * D_FF), BF16),
                            pltpu.VMEM((D_FF, D_MODEL), BF16)]),
        compiler_params=_cparams(("arbitrary",)),
        name="moe_ffn",
    )(block_e, n_used, xs, w1, b1, w2, b2)


def _combine_kernel(dest_hbm, yb_hbm, route_ref, h1_ref, g2_ref, nw_ref, o_ref,
                    idx_smem, rows, sem_idx, sem_rows, *, tile, n_tiles):
    i = pl.program_id(0)
    slot = lax.rem(i, 2)

    def idx_copy(j, s):
        return pltpu.make_async_copy(dest_hbm.at[j], idx_smem.at[s], sem_idx.at[s])

    def issue_rows(s):
        def issue(t, carry):
            for k in range(TOP_K):
                d = idx_smem[s, k * tile + t]
                pltpu.make_async_copy(yb_hbm.at[pl.ds(d, 1), :], rows.at[s, k, pl.ds(t, 1), :],
                                      sem_rows.at[s]).start(priority=k % 2)
            return carry
        lax.fori_loop(0, tile, issue, 0, unroll=2)

    @pl.when(i == 0)
    def _():
        idx_copy(0, 0).start()
        idx_copy(0, 0).wait()
        issue_rows(0)

        if n_tiles > 1:
            idx_copy(1, 1).start()

    @pl.when(i + 1 < n_tiles)
    def _():
        idx_copy(i + 1, 1 - slot).wait()
        issue_rows(1 - slot)

        @pl.when(i + 2 < n_tiles)
        def _():
            idx_copy(i + 2, slot).start()

    for k in range(TOP_K):
        pltpu.make_async_copy(yb_hbm.at[pl.ds(0, tile), :], rows.at[slot, k],
                              sem_rows.at[slot]).wait()

    route = route_ref[...]
    y = jnp.zeros((tile, D_MODEL), F32)
    for k in range(TOP_K):
        y = y + route[:, TOP_K + k:TOP_K + k + 1] * rows[slot, k]
    h = h1_ref[...] + g2_ref[...] * y
    ms = jnp.mean(h * h, axis=-1, keepdims=True)
    o_ref[...] = h * lax.rsqrt(ms + EPS) * nw_ref[...]


def _combine(dest_tiles, yb, route, h1, g2, nw, *, tile, tiles_per_batch):
    T = h1.shape[0]
    return pl.pallas_call(
        functools.partial(_combine_kernel, tile=tile, n_tiles=T // tile),
        out_shape=jax.ShapeDtypeStruct((T, D_MODEL), F32),
        grid=(T // tile,),
        in_specs=[pl.BlockSpec(memory_space=pl.ANY),
                  pl.BlockSpec(memory_space=pl.ANY),
                  pl.BlockSpec((tile, LANES), lambda i: (i, 0)),
                  pl.BlockSpec((tile, D_MODEL), lambda i: (i, 0)),
                  pl.BlockSpec((None, 1, D_MODEL), lambda i: (i // tiles_per_batch, 0, 0)),
                  pl.BlockSpec((1, D_MODEL), lambda i: (0, 0))],
        out_specs=pl.BlockSpec((tile, D_MODEL), lambda i: (i, 0)),
        scratch_shapes=[pltpu.SMEM((2, tile * TOP_K), I32),
                        pltpu.VMEM((2, TOP_K, tile, D_MODEL), F32),
                        pltpu.SemaphoreType.DMA((2,)),
                        pltpu.SemaphoreType.DMA((2,))],
        compiler_params=_cparams(("arbitrary",)),
        name="combine_norm",
    )(dest_tiles, yb, route, h1, g2, nw)


def _rope_tables(n):
    rows = n // GRID_W
    row = jnp.repeat(jnp.arange(rows, dtype=F32), GRID_W)
    col = jnp.tile(jnp.arange(GRID_W, dtype=F32), rows)
    n_freq = RET_DK // 4
    inv = ROPE_BASE ** (-jnp.arange(n_freq, dtype=F32) / n_freq)
    ang = jnp.concatenate([row[:, None] * inv, col[:, None] * inv], axis=-1)
    cos, sin = jnp.cos(ang), jnp.sin(ang)
    cos_h = jnp.concatenate([cos, cos], axis=-1)
    sin_h = jnp.concatenate([-sin, sin], axis=-1)
    return jnp.tile(cos_h, (1, RET_HEADS)), jnp.tile(sin_h, (1, RET_HEADS))


def kernel(x, c, ctx, c_ctx, w_ada, b_ada, norm_mix_w, norm_ffn_w, w_in, w_out, hg_lb,
           hg_norm_w, router_w, router_b, w1, b1, w2, b2, norm_final_w):
    B, N, D = x.shape
    C = ctx.shape[1]
    T = B * N
    assert D == D_MODEL and w_ada.shape[0] == 1

    cc = jnp.concatenate([c.astype(F32), c_ctx.astype(F32)[None, :],
                          jnp.zeros((16 - B - 1, D), F32)], axis=0)
    mod = _ada(cc, w_ada[0], b_ada[0][None, :])
    mod = mod.reshape(16, 6, 1, D).transpose(1, 0, 2, 3)
    sh1, sc1, g1, sh2, sc2, g2 = (mod[i] for i in range(6))

    w_in_bf = w_in[0].astype(BF16)
    w_out_bf = w_out[0].astype(BF16)
    nw_mix = norm_mix_w[0][None, :]
    cos_t, sin_t = _rope_tables(N)

    pb, lf = _inproj(x, sc1, sh1, None, nw_mix, w_in_bf, hg_lb[:2], cos_t, sin_t,
                     rope=True, tm=512)
    pbc, lfc = _inproj(ctx, sc1, sh1, B, nw_mix, w_in_bf, hg_lb[:2], cos_t[:C], sin_t[:C],
                       rope=False, tm=C)

    a_ret = _retention(pb, pbc)
    a_hg = _hgrn(pb, lf, pbc, lfc, hg_norm_w[0][None, :])

    rw = jnp.zeros((D, LANES), F32).at[:, :N_EXPERTS].set(router_w[0])
    rb = jnp.full((1, LANES), -1e30, F32).at[0, :N_EXPERTS].set(router_b[0])
    h1, v, route, route_t, cnt = _outproj(a_ret, a_hg, w_out_bf, x, g1, sc2, sh2,
                                          norm_ffn_w[0][None, :], rw, rb, tm=512)

    idx = route_t[0:TOP_K].astype(I32)
    rank = route_t[2 * TOP_K:3 * TOP_K].astype(I32)
    counts = cnt[0, :N_EXPERTS].astype(I32)
    padded = (counts + ROW_BLOCK - 1) // ROW_BLOCK * ROW_BLOCK
    pend = jnp.cumsum(padded)
    pstart = pend - padded
    dest = pstart[idx] + rank

    def dest_tiles(tile):
        return dest.reshape(TOP_K, T // tile, tile).transpose(1, 0, 2).reshape(T // tile,
                                                                               TOP_K * tile)
    n_blocks = (T * TOP_K) // ROW_BLOCK + N_EXPERTS
    n_used = (pend[-1] // ROW_BLOCK).astype(I32)
    starts = jnp.arange(n_blocks, dtype=I32) * ROW_BLOCK
    block_e = jnp.minimum(jnp.sum((pend[None, :] <= starts[:, None]).astype(I32), axis=1),
                          N_EXPERTS - 1)
    meta = jnp.concatenate([pend.astype(I32), counts, n_used[None]])

    tile_d = 512
    xs = _dispatch(meta, dest_tiles(tile_d), v, tile=tile_d, n_blocks=n_blocks)
    yb = _moe(block_e, n_used[None], xs, w1[0], b1[0][:, None, :], w2[0], b2[0][:, None, :],
              n_blocks=n_blocks)
    tile_c = 256
    out = _combine(dest_tiles(tile_c), yb, route, h1, g2,
                   norm_final_w[None, :], tile=tile_c, tiles_per_batch=N // tile_c)
    return out.reshape(B, N, D)
```

```python
import functools
import math

import jax
import jax.numpy as jnp
import numpy as np
from jax import lax
from jax.experimental import pallas as pl
from jax.experimental.pallas import tpu as pltpu

F32 = jnp.float32
BF16 = jnp.bfloat16
I32 = jnp.int32

D_MODEL = 1024
GRID_W = 64
RET_HEADS = 4
RET_DK = 64
HG_HEADS = 4
PROJ_W = 4096
ROPE_BASE = 10000.0
EPS = 1e-6
N_EXPERTS = 32
TOP_K = 4
D_FF = 1024
SWIGLU_LIMIT = 7.0
SWIGLU_ALPHA = 1.702

LANES = 128
CHUNK = 128
ROW_BLOCK = 512
VMEM_LIMIT = 56 * 1024 * 1024
ROW_TILES = D_MODEL // LANES

C_RQ, C_RK, C_RV, C_RG, C_HQ, C_FF, C_FB, C_HV, C_HG = (
    0, 256, 512, 1024, 1536, 2048, 2560, 3072, 3584)


def _cparams(sem):
    return pltpu.CompilerParams(dimension_semantics=sem, vmem_limit_bytes=VMEM_LIMIT)


def _split_bf16(x):
    hi = x.astype(BF16)
    lo = (x - hi.astype(F32)).astype(BF16)
    return hi, lo


def _dot(a, b):
    return jnp.dot(a, b, preferred_element_type=F32)


def _dot_nt(a, b):
    return lax.dot_general(a, b, (((1,), (1,)), ((), ())), preferred_element_type=F32)


def _dot3(a, b):
    ah, al = _split_bf16(a)
    bh, bl = _split_bf16(b)
    return _dot(ah, bh) + (_dot(ah, bl) + _dot(al, bh))


def _silu(x):
    return x * jax.nn.sigmoid(x)


def _load_tile_rows(ref, n):
    return jnp.concatenate([ref[pl.ds(s, n, stride=ROW_TILES), :] for s in range(ROW_TILES)],
                           axis=1)


def _store_tile_rows(ref, x):
    n = x.shape[0]
    for s in range(ROW_TILES):
        ref[pl.ds(s, n, stride=ROW_TILES), :] = x[:, s * LANES:(s + 1) * LANES]


def _ada_kernel(c_ref, w_ref, b_ref, o_ref):
    s = _silu(c_ref[...])
    o_ref[...] = _dot3(s, w_ref[...]) + b_ref[...]


def _ada(cc, w, b):
    nblk = w.shape[1] // D_MODEL
    return pl.pallas_call(
        _ada_kernel,
        out_shape=jax.ShapeDtypeStruct((cc.shape[0], w.shape[1]), F32),
        grid=(nblk,),
        in_specs=[pl.BlockSpec(cc.shape, lambda j: (0, 0)),
                  pl.BlockSpec((D_MODEL, D_MODEL), lambda j: (0, j)),
                  pl.BlockSpec((1, D_MODEL), lambda j: (0, j))],
        out_specs=pl.BlockSpec((cc.shape[0], D_MODEL), lambda j: (0, j)),
        compiler_params=_cparams(("arbitrary",)),
        name="ada",
    )(cc, w, b)


def _inproj_kernel(x_ref, sc_ref, sh_ref, nw_ref, w_ref, lb_ref, cos_ref, sin_ref,
                   pb_ref, lf_ref, u_scr, *, rope):
    x = x_ref[...]
    ms = jnp.mean(x * x, axis=-1, keepdims=True)
    u = x * lax.rsqrt(ms + EPS) * nw_ref[...] * (1.0 + sc_ref[...]) + sh_ref[...]
    u_scr[...] = u.astype(BF16)

    def proj(lo, width):
        return _dot(u_scr[...], w_ref[:, lo:lo + width])

    tm = x.shape[0]
    if rope:
        lane = lax.broadcasted_iota(I32, (tm, LANES), 1)
        first = (lane & 32) == 0

    def put_rot(col, scale):
        for j in range(2):
            lo = col + j * LANES
            t = proj(lo, LANES)
            if scale != 1.0:
                t = t * scale
            if rope:
                tb = j * LANES
                rot = jnp.where(first, pltpu.roll(t, 96, axis=1), pltpu.roll(t, 32, axis=1))
                t = t * cos_ref[:, tb:tb + LANES] + rot * sin_ref[:, tb:tb + LANES]
            pb_ref[:, lo:lo + LANES] = t.astype(BF16)

    put_rot(C_RQ, 1.0)
    put_rot(C_RK, RET_DK ** -0.5)
    pb_ref[:, C_RV:C_RV + 512] = proj(C_RV, 512).astype(BF16)
    pb_ref[:, C_RG:C_RG + 512] = _silu(proj(C_RG, 512)).astype(BF16)
    pb_ref[:, C_HQ:C_HQ + 512] = _silu(proj(C_HQ, 512)).astype(BF16)
    pb_ref[:, C_HV:C_HV + 512] = proj(C_HV, 512).astype(BF16)
    pb_ref[:, C_HG:C_HG + 512] = _silu(proj(C_HG, 512)).astype(BF16)

    la = lb_ref[0]
    lbb = lb_ref[1]
    mx = jnp.maximum(la, lbb)
    ea = jnp.exp(la - mx)
    eb = jnp.exp(lbb - mx)
    lb = ea / (ea + eb)
    for d, col in enumerate((C_FF, C_FB)):
        lbd = lb[d:d + 1, :]
        f = lbd + (1.0 - lbd) * jax.nn.sigmoid(proj(col, 512))
        pb_ref[:, col:col + 512] = (1.0 - f).astype(BF16)
        lf_ref[:, d * 512:(d + 1) * 512] = jnp.log(f)


def _inproj(x, sc, sh, mod_row, nw, w_bf, hg_lb, cos_t, sin_t, *, rope, tm):
    B, n, _ = x.shape
    nt = n // tm
    if mod_row is None:
        mrow = lambda b, j: (b, 0, 0)
    else:
        mrow = lambda b, j: (mod_row, 0, 0)
    return pl.pallas_call(
        functools.partial(_inproj_kernel, rope=rope),
        out_shape=(jax.ShapeDtypeStruct((B, n, PROJ_W), BF16),
                   jax.ShapeDtypeStruct((B, n, 1024), F32)),
        grid=(B, nt),
        in_specs=[pl.BlockSpec((None, tm, D_MODEL), lambda b, j: (b, j, 0)),
                  pl.BlockSpec((None, 1, D_MODEL), mrow),
                  pl.BlockSpec((None, 1, D_MODEL), mrow),
                  pl.BlockSpec((1, D_MODEL), lambda b, j: (0, 0)),
                  pl.BlockSpec((D_MODEL, PROJ_W), lambda b, j: (0, 0)),
                  pl.BlockSpec((2, 2, 512), lambda b, j: (0, 0, 0)),
                  pl.BlockSpec((tm, 256), lambda b, j: (j, 0)),
                  pl.BlockSpec((tm, 256), lambda b, j: (j, 0))],
        out_specs=(pl.BlockSpec((None, tm, PROJ_W), lambda b, j: (b, j, 0)),
                   pl.BlockSpec((None, tm, 1024), lambda b, j: (b, j, 0))),
        scratch_shapes=[pltpu.VMEM((tm, D_MODEL), BF16)],
        compiler_params=_cparams(("arbitrary", "arbitrary")),
        name="inproj_rope" if rope else "inproj_ctx",
    )(x, sc, sh, nw, w_bf, hg_lb, cos_t, sin_t)


_RET_LGF = [math.log1p(-(2.0 ** (-5.0 - 2.0 * h))) for h in range(RET_HEADS)]
_RET_LGB = [math.log1p(-(2.0 ** (-6.0 - 2.0 * h))) for h in range(RET_HEADS)]


def _ret_kernel(q_ref, k_ref, v_ref, g_ref, kc_ref, vc_ref, o_ref,
                u_scr, sin_scr, kt_scr, dtot_scr, *, n_lat, n_ctx):
    L = CHUNK
    pair = pl.program_id(1)
    row = lax.broadcasted_iota(I32, (L, L), 0).astype(F32)
    col = lax.broadcasted_iota(I32, (L, L), 1).astype(F32)
    lane = lax.broadcasted_iota(I32, (L, LANES), 1)
    trow = lax.broadcasted_iota(I32, (L, 1), 0).astype(F32)
    tcol = lax.broadcasted_iota(I32, (1, L), 1).astype(F32)
    low_half = lane < RET_DK

    def u_chunk(k_blk, v_blk, ci, store_kt):
        kt = k_blk.astype(F32).T
        if store_kt is not None:
            kt_scr[store_kt] = kt.astype(BF16)
        for hh in range(2):
            lgf = jnp.where(pair == 0, _RET_LGF[hh], _RET_LGF[2 + hh])
            lgb = jnp.where(pair == 0, _RET_LGB[hh], _RET_LGB[2 + hh])
            kth = kt[hh * RET_DK:(hh + 1) * RET_DK, :]
            wkf = jnp.exp(lgf * (L - 1.0 - tcol))
            wkb = jnp.exp(lgb * tcol)
            lhs = jnp.concatenate([kth * wkf, kth * wkb], axis=0).astype(BF16)
            u_scr[hh, ci] = _dot(lhs, v_blk[:, hh * LANES:(hh + 1) * LANES])

    for c in range(n_ctx):
        u_chunk(kc_ref[c * L:(c + 1) * L, :], vc_ref[c * L:(c + 1) * L, :], c, None)

    def lat_u(c, carry):
        r0 = pl.multiple_of(c * L, L)
        u_chunk(k_ref[pl.ds(r0, L), :], v_ref[pl.ds(r0, L), :], n_ctx + c, c)
        return carry
    lax.fori_loop(0, n_lat, lat_u, 0, unroll=2)

    ones = jnp.ones((RET_DK, LANES), F32)
    for hh in range(2):
        lgf = jnp.where(pair == 0, _RET_LGF[hh], _RET_LGF[2 + hh])
        lgb = jnp.where(pair == 0, _RET_LGB[hh], _RET_LGB[2 + hh])
        d = row - col
        dtot_scr[hh] = jnp.where(d > 0, jnp.exp(lgf * jnp.maximum(d, 0.0)),
                                 jnp.where(d < 0, jnp.exp(lgb * jnp.maximum(-d, 0.0)), 2.0))
        af = jnp.exp(ones * (lgf * L))
        ab = jnp.exp(ones * (lgb * L))

        s = jnp.zeros((RET_DK, LANES), F32)
        for c in range(n_ctx):
            s = af * s + u_scr[hh, c, 0:RET_DK, :]
        sb = jnp.zeros((RET_DK, LANES), F32)
        for c in reversed(range(n_ctx)):
            sb = ab * sb + u_scr[hh, c, RET_DK:2 * RET_DK, :]

        def fwd(c, s, hh=hh, af=af):
            sin_scr[hh, c, 0:RET_DK, :] = s.astype(BF16)
            return af * s + u_scr[hh, n_ctx + c, 0:RET_DK, :]
        lax.fori_loop(0, n_lat, fwd, s)

        def bwd(i, sb, hh=hh, ab=ab):
            c = n_lat - 1 - i
            sin_scr[hh, c, RET_DK:2 * RET_DK, :] = sb.astype(BF16)
            return ab * sb + u_scr[hh, n_ctx + c, RET_DK:2 * RET_DK, :]
        lax.fori_loop(0, n_lat, bwd, sb)

    def out_chunk(c, carry):
        r0 = pl.multiple_of(c * L, L)
        q = q_ref[pl.ds(r0, L), :].astype(F32)
        qr = pltpu.roll(q, RET_DK, axis=1)
        kt = kt_scr[c]
        for hh in range(2):
            lgf = jnp.where(pair == 0, _RET_LGF[hh], _RET_LGF[2 + hh])
            lgb = jnp.where(pair == 0, _RET_LGB[hh], _RET_LGB[2 + hh])
            mine = low_half if hh == 0 else jnp.logical_not(low_half)
            qm = jnp.where(mine, q, 0.0).astype(BF16)
            p = (_dot(qm, kt) * dtot_scr[hh]).astype(BF16)
            vh = v_ref[pl.ds(r0, L), hh * LANES:(hh + 1) * LANES]
            wqf = jnp.exp(lgf * (trow + 1.0))
            wqb = jnp.exp(lgb * (L - trow))
            qa, qb = (q, qr) if hh == 0 else (qr, q)
            qs = jnp.where(low_half, qa * wqf, qb * wqb).astype(BF16)
            o = _dot(p, vh) + _dot(qs, sin_scr[hh, c])
            ms = jnp.mean(o * o, axis=-1, keepdims=True)
            gh = g_ref[pl.ds(r0, L), hh * LANES:(hh + 1) * LANES].astype(F32)
            o_ref[pl.ds(r0, L), hh * LANES:(hh + 1) * LANES] = (
                o * lax.rsqrt(ms + EPS) * gh).astype(BF16)
        return carry
    lax.fori_loop(0, n_lat, out_chunk, 0, unroll=2)


def _retention(pb, pbc):
    B, n, _ = pb.shape
    nc = pbc.shape[1]
    n_lat, n_ctx = n // CHUNK, nc // CHUNK
    return pl.pallas_call(
        functools.partial(_ret_kernel, n_lat=n_lat, n_ctx=n_ctx),
        out_shape=jax.ShapeDtypeStruct((B, n, 512), BF16),
        grid=(B, 2),
        in_specs=[pl.BlockSpec((None, n, LANES), lambda b, p: (b, 0, C_RQ // LANES + p)),
                  pl.BlockSpec((None, n, LANES), lambda b, p: (b, 0, C_RK // LANES + p)),
                  pl.BlockSpec((None, n, 256), lambda b, p: (b, 0, C_RV // 256 + p)),
                  pl.BlockSpec((None, n, 256), lambda b, p: (b, 0, C_RG // 256 + p)),
                  pl.BlockSpec((None, nc, LANES), lambda b, p: (b, 0, C_RK // LANES + p)),
                  pl.BlockSpec((None, nc, 256), lambda b, p: (b, 0, C_RV // 256 + p))],
        out_specs=pl.BlockSpec((None, n, 256), lambda b, p: (b, 0, p)),
        scratch_shapes=[pltpu.VMEM((2, n_lat + n_ctx, CHUNK, LANES), F32),
                        pltpu.VMEM((2, n_lat, CHUNK, LANES), BF16),
                        pltpu.VMEM((n_lat, LANES, CHUNK), BF16),
                        pltpu.VMEM((2, CHUNK, CHUNK), F32)],
        compiler_params=_cparams(("arbitrary", "arbitrary")),
        name="retention",
    )(pb, pb, pb, pb, pbc, pbc)


_LEVELS = (64, 32, 16, 8, 4, 2, 1)


def _expand_rows(r, rep):
    n = r.shape[0]
    if n == 1:
        return jnp.broadcast_to(r, (rep, r.shape[1]))
    return jnp.concatenate(
        [jnp.broadcast_to(r[i:i + 1, :], (rep, r.shape[1])) for i in range(n)], axis=0)


def _hgrn_kernel(q_ref, kf_ref, kb_ref, v_ref, g_ref, lff_ref, lfb_ref,
                 kfc_ref, kbc_ref, vc_ref, lffc_ref, lfbc_ref, nw_ref, o_ref,
                 ut_scr, a_scr, qs_scr, oi_scr, sin_scr, bfb_scr, *, n_lat, n_ctx):
    L = CHUNK
    row = lax.broadcasted_iota(I32, (L, L), 0)
    col = lax.broadcasted_iota(I32, (L, L), 1)
    xr_bits = lax.bitcast_convert_type((row ^ col).astype(F32), I32)
    lv = lax.shift_right_logical(xr_bits, 23) - 127
    row2 = lax.broadcasted_iota(I32, (L, 2 * L), 0)
    col2 = lax.broadcasted_iota(I32, (L, 2 * L), 1) & (L - 1)
    tril2 = jnp.where(col2 <= row2, 1.0, 0.0).astype(BF16)
    triu2 = jnp.where(col2 >= row2, 1.0, 0.0).astype(BF16)

    def cums(lff, lfb):
        hf, lof = _split_bf16(lff)
        hb, lob = _split_bf16(lfb)
        bf = _dot(tril2, jnp.concatenate([hf, lof], axis=0))
        bb = _dot(triu2, jnp.concatenate([hb, lob], axis=0))
        return bf, bb

    def state_part(ci, kf, kb, v_blk, bf, bb):
        endf = bf[L - 1:L, :]
        endb = bb[0:1, :]
        ksf = kf * jnp.exp(endf - bf)
        ksb = kb * jnp.exp(endb - bb)
        vt = v_blk.astype(F32).T.astype(BF16)
        ut_scr[ci] = _dot(vt, jnp.concatenate([ksf, ksb], axis=1).astype(BF16))
        a_scr[ci] = jnp.broadcast_to(
            jnp.concatenate([jnp.exp(endf), jnp.exp(endb)], axis=1), (8, 2 * LANES))

    for c in range(n_ctx):
        sl = slice(c * L, (c + 1) * L)
        bf, bb = cums(lffc_ref[sl, :], lfbc_ref[sl, :])
        state_part(c, kfc_ref[sl, :].astype(F32), kbc_ref[sl, :].astype(F32),
                   vc_ref[sl, :], bf, bb)

    def lat_chunk(c, u):
        rows = pl.ds(pl.multiple_of(c * L, L), L)
        bf_scr = bfb_scr.at[u, 0]
        bb_scr = bfb_scr.at[u, 1]
        q = q_ref[rows, :].astype(F32)
        kf = kf_ref[rows, :].astype(F32)
        kb = kb_ref[rows, :].astype(F32)
        v_blk = v_ref[rows, :]
        lff = lff_ref[rows, :]
        lfb = lfb_ref[rows, :]
        bf, bb = cums(lff, lfb)
        state_part(n_ctx + c, kf, kb, v_blk, bf, bb)
        qs_scr[rows, :] = jnp.concatenate([q * jnp.exp(bf), q * jnp.exp(bb)],
                                          axis=1).astype(BF16)
        bf_scr[...] = bf
        bb_scr[...] = bb

        acc = jnp.zeros((L, L), F32)
        for lvl, h in enumerate(_LEVELS):
            bit = (row & h) != 0
            ksel = jnp.where(bit, kb, kf)
            if h >= 4:
                n = (L // 2) // h
                if n == 1:
                    rf = bf_scr[h - 1:h, :]
                    rb = bb_scr[h:h + 1, :]
                else:
                    rf = bf_scr[pl.ds(h - 1, n, stride=2 * h), :]
                    rb = bb_scr[pl.ds(h, n, stride=2 * h), :]
                df = bf - _expand_rows(rf, 2 * h)
                db = bb - _expand_rows(rb, 2 * h)
                eq = jnp.where(bit, df, db)
                ek = -jnp.where(bit, db, df)
            elif h == 2:
                m = row & 3
                lff_n = pltpu.roll(lff, L - 1, axis=0)
                lfb_n = pltpu.roll(lfb, L - 1, axis=0)
                eq = jnp.where(m == 2, lff,
                               jnp.where(m == 3, lff + pltpu.roll(lff, 1, axis=0),
                                         jnp.where(m == 0, lfb + lfb_n, lfb)))
                ek = jnp.where(m == 3, pltpu.roll(lfb, 1, axis=0),
                               jnp.where(m == 0, lff_n, 0.0))
            else:
                eq = jnp.where(bit, lff, lfb)
                ek = None
            lhs = (q * jnp.exp(eq)).astype(BF16)
            rhs = (ksel if ek is None else ksel * jnp.exp(ek)).astype(BF16)
            acc = jnp.where(lv == 6 - lvl, _dot_nt(lhs, rhs), acc)

        dsum = jnp.sum(q * (kf + kb), axis=-1, keepdims=True)
        oi_scr[rows, :] = _dot(acc.astype(BF16), v_blk) + dsum * v_blk.astype(F32)

    def lat_pair(i, carry):
        for u in range(2):
            lat_chunk(2 * i + u, u)
        return carry
    lax.fori_loop(0, n_lat // 2, lat_pair, 0)

    st = jnp.zeros((LANES, LANES), F32)
    for c in range(n_ctx):
        st = st * a_scr[c, 0:1, 0:LANES] + ut_scr[c, :, 0:LANES]
    stb = jnp.zeros((LANES, LANES), F32)
    for c in reversed(range(n_ctx)):
        stb = stb * a_scr[c, 0:1, LANES:2 * LANES] + ut_scr[c, :, LANES:2 * LANES]

    def fwd(c, st):
        sin_scr[c, :, 0:LANES] = st.astype(BF16)
        ci = n_ctx + c
        return st * a_scr[ci, 0:1, 0:LANES] + ut_scr[ci, :, 0:LANES]
    lax.fori_loop(0, n_lat, fwd, st)

    def bwd(i, stb):
        c = n_lat - 1 - i
        sin_scr[c, :, LANES:2 * LANES] = stb.astype(BF16)
        ci = n_ctx + c
        return stb * a_scr[ci, 0:1, LANES:2 * LANES] + ut_scr[ci, :, LANES:2 * LANES]
    lax.fori_loop(0, n_lat, bwd, stb)

    def out_chunk(c, carry):
        rows = pl.ds(pl.multiple_of(c * L, L), L)
        o = oi_scr[rows, :] + _dot_nt(qs_scr[rows, :], sin_scr[c])
        ms = jnp.mean(o * o, axis=-1, keepdims=True)
        y = o * lax.rsqrt(ms + EPS) * nw_ref[...] * g_ref[rows, :].astype(F32)
        o_ref[rows, :] = y.astype(BF16)
        return carry
    lax.fori_loop(0, n_lat, out_chunk, 0, unroll=4)


def _hgrn(pb, lf, pbc, lfc, nw):
    B, n, _ = pb.shape
    nc = pbc.shape[1]
    n_lat, n_ctx = n // CHUNK, nc // CHUNK

    def colblk(rows, col0):
        return pl.BlockSpec((None, rows, LANES), lambda b, h: (b, 0, col0 // LANES + h))

    return pl.pallas_call(
        functools.partial(_hgrn_kernel, n_lat=n_lat, n_ctx=n_ctx),
        out_shape=jax.ShapeDtypeStruct((B, n, 512), BF16),
        grid=(B, HG_HEADS),
        in_specs=[colblk(n, C_HQ), colblk(n, C_FF), colblk(n, C_FB), colblk(n, C_HV),
                  colblk(n, C_HG), colblk(n, 0), colblk(n, 512),
                  colblk(nc, C_FF), colblk(nc, C_FB), colblk(nc, C_HV),
                  colblk(nc, 0), colblk(nc, 512),
                  pl.BlockSpec((1, LANES), lambda b, h: (0, 0))],
        out_specs=pl.BlockSpec((None, n, LANES), lambda b, h: (b, 0, h)),
        scratch_shapes=[pltpu.VMEM((n_lat + n_ctx, LANES, 2 * LANES), F32),
                        pltpu.VMEM((n_lat + n_ctx, 8, 2 * LANES), F32),
                        pltpu.VMEM((n, 2 * LANES), BF16),
                        pltpu.VMEM((n, LANES), F32),
                        pltpu.VMEM((n_lat, LANES, 2 * LANES), BF16),
                        pltpu.VMEM((2, 2, CHUNK, LANES), F32)],
        compiler_params=_cparams(("arbitrary", "arbitrary")),
        name="hgrn2",
    )(pb, pb, pb, pb, pb, lf, lf, pbc, pbc, pbc, lfc, lfc, nw)


def _outproj_kernel(ar_ref, ah_ref, w_ref, x_ref, g1_ref, sc_ref, sh_ref, nw_ref,
                    rw_ref, rb_ref, h1_ref, v_ref, route_ref, route_t_ref, cnt_ref, cnt_scr):
    first_step = jnp.logical_and(pl.program_id(0) == 0, pl.program_id(1) == 0)

    @pl.when(first_step)
    def _():
        cnt_scr[...] = jnp.zeros_like(cnt_scr)

    y = _dot(ar_ref[...], w_ref[0:512, :]) + _dot(ah_ref[...], w_ref[512:1024, :])
    h1 = x_ref[...] + g1_ref[...] * y
    h1_ref[...] = h1
    ms = jnp.mean(h1 * h1, axis=-1, keepdims=True)
    v = h1 * lax.rsqrt(ms + EPS) * nw_ref[...] * (1.0 + sc_ref[...]) + sh_ref[...]
    _store_tile_rows(v_ref, v)

    tm = v.shape[0]
    logits = _dot3(v, rw_ref[...]) + rb_ref[...]
    lane = lax.broadcasted_iota(I32, (tm, LANES), 1)
    lane_f = lane.astype(F32)
    l = logits
    sels, tops, idxs = [], [], []
    for _ in range(TOP_K):
        m = jnp.max(l, axis=1, keepdims=True)
        i = jnp.min(jnp.where(l == m, lane_f, float(LANES)), axis=1, keepdims=True)
        sel = lane_f == i
        l = jnp.where(sel, -jnp.inf, l)
        sels.append(sel)
        tops.append(m)
        idxs.append(i)
    es = [jnp.exp(t - tops[0]) for t in tops]
    den = es[0] + es[1] + es[2] + es[3]
    gates = [e / den for e in es]

    oh = jnp.zeros((tm, LANES), F32)
    for sel in sels:
        oh = jnp.where(sel, 1.0, oh)
    r = lax.broadcasted_iota(I32, (tm, tm), 0)
    c = lax.broadcasted_iota(I32, (tm, tm), 1)
    tri = jnp.where(c < r, 1.0, 0.0).astype(BF16)
    before = _dot(tri, oh.astype(BF16)) + cnt_scr[0:1, :]
    ranks = [jnp.sum(jnp.where(sel, before, 0.0), axis=1, keepdims=True) for sel in sels]
    cnt_scr[...] = cnt_scr[...] + jnp.sum(oh, axis=0, keepdims=True)
    cnt_ref[...] = cnt_scr[...]

    out = jnp.zeros((tm, LANES), F32)
    for k in range(TOP_K):
        out = jnp.where(lane == k, idxs[k], out)
        out = jnp.where(lane == TOP_K + k, gates[k], out)
        out = jnp.where(lane == 2 * TOP_K + k, ranks[k], out)
    route_ref[...] = out
    route_t_ref[...] = out.T[0:16, :]


def _outproj(a_ret, a_hg, w_bf, x, g1, sc2, sh2, nw, rw, rb, *, tm):
    B, n, _ = x.shape
    nt = n // tm
    T = B * n
    mrow = lambda b, j: (b, 0, 0)
    tok = lambda b, j: (b * nt + j, 0)
    return pl.pallas_call(
        _outproj_kernel,
        out_shape=(jax.ShapeDtypeStruct((T, D_MODEL), F32),
                   jax.ShapeDtypeStruct((T * ROW_TILES, LANES), F32),
                   jax.ShapeDtypeStruct((T, LANES), F32),
                   jax.ShapeDtypeStruct((16, T), F32),
                   jax.ShapeDtypeStruct((8, LANES), F32)),
        grid=(B, nt),
        in_specs=[pl.BlockSpec((None, tm, 512), lambda b, j: (b, j, 0)),
                  pl.BlockSpec((None, tm, 512), lambda b, j: (b, j, 0)),
                  pl.BlockSpec((D_MODEL, D_MODEL), lambda b, j: (0, 0)),
                  pl.BlockSpec((None, tm, D_MODEL), lambda b, j: (b, j, 0)),
                  pl.BlockSpec((None, 1, D_MODEL), mrow),
                  pl.BlockSpec((None, 1, D_MODEL), mrow),
                  pl.BlockSpec((None, 1, D_MODEL), mrow),
                  pl.BlockSpec((1, D_MODEL), lambda b, j: (0, 0)),
                  pl.BlockSpec((D_MODEL, LANES), lambda b, j: (0, 0)),
                  pl.BlockSpec((1, LANES), lambda b, j: (0, 0))],
        out_specs=(pl.BlockSpec((tm, D_MODEL), tok),
                   pl.BlockSpec((tm * ROW_TILES, LANES), tok),
                   pl.BlockSpec((tm, LANES), tok),
                   pl.BlockSpec((16, tm), lambda b, j: (0, b * nt + j)),
                   pl.BlockSpec((8, LANES), lambda b, j: (0, 0))),
        scratch_shapes=[pltpu.VMEM((8, LANES), F32)],
        compiler_params=_cparams(("arbitrary", "arbitrary")),
        name="outproj_router",
    )(a_ret, a_hg, w_bf, x, g1, sc2, sh2, nw, rw, rb)


def _dispatch_kernel(meta_ref, dest_hbm, v_ref, xs_hbm, idx_smem, zbuf, sem_idx, sem_z, sem_rows,
                     *, tile, n_blocks, n_tiles):
    i = pl.program_id(0)
    slot = lax.rem(i, 2)

    def idx_copy(j, s):
        n_asg = tile * TOP_K
        return pltpu.make_async_copy(dest_hbm.at[j], idx_smem.at[pl.ds(s * n_asg, n_asg)],
                                     sem_idx.at[s])

    @pl.when(i == 0)
    def _():
        idx_copy(0, 0).start()

    def zero_copy(r0):
        return pltpu.make_async_copy(
            zbuf, xs_hbm.at[pl.ds(r0 * ROW_TILES, ROW_BLOCK * ROW_TILES), :], sem_z)

    @pl.when(i == 0)
    def _():
        zbuf[...] = jnp.zeros_like(zbuf)
        n_used = meta_ref[2 * N_EXPERTS]

        def tail_start(e):
            return pl.multiple_of(meta_ref[e] - ROW_BLOCK, ROW_BLOCK)

        def has_tail(e):
            return (meta_ref[N_EXPERTS + e] & (ROW_BLOCK - 1)) != 0

        def z_issue(e, carry):
            @pl.when(has_tail(e))
            def _():
                zero_copy(tail_start(e)).start()
            return carry
        lax.fori_loop(0, N_EXPERTS, z_issue, 0)

        def u_issue(blk, carry):
            @pl.when(blk >= n_used)
            def _():
                zero_copy(pl.multiple_of(blk * ROW_BLOCK, ROW_BLOCK)).start()
            return carry
        lax.fori_loop(0, n_blocks, u_issue, 0)

        def z_wait(e, carry):
            @pl.when(has_tail(e))
            def _():
                zero_copy(0).wait()
            return carry
        lax.fori_loop(0, N_EXPERTS, z_wait, 0)

        def u_wait(blk, carry):
            @pl.when(blk >= n_used)
            def _():
                zero_copy(0).wait()
            return carry
        lax.fori_loop(0, n_blocks, u_wait, 0)

    idx_copy(i, slot).wait()

    @pl.when(i + 1 < n_tiles)
    def _():
        idx_copy(i + 1, 1 - slot).start()

    def issue(t, carry):
        for k in range(TOP_K):
            d = pl.multiple_of(idx_smem[slot * (tile * TOP_K) + k * tile + t], ROW_TILES)
            src = v_ref.at[pl.ds(pl.multiple_of(t * ROW_TILES, ROW_TILES), ROW_TILES), :]
            pltpu.make_async_copy(src, xs_hbm.at[pl.ds(d, ROW_TILES), :],
                                  sem_rows).start(priority=k % 2)
        return carry
    lax.fori_loop(0, tile, issue, 0, unroll=2)
    for _ in range(TOP_K):
        pltpu.make_async_copy(v_ref, xs_hbm.at[pl.ds(0, tile * ROW_TILES), :], sem_rows).wait()


def _dispatch(meta, dest_tiles, v, *, tile, n_blocks):
    T = v.shape[0] // ROW_TILES
    R = n_blocks * ROW_BLOCK
    return pl.pallas_call(
        functools.partial(_dispatch_kernel, tile=tile, n_blocks=n_blocks, n_tiles=T // tile),
        out_shape=jax.ShapeDtypeStruct((R * ROW_TILES, LANES), F32),
        grid_spec=pltpu.PrefetchScalarGridSpec(
            num_scalar_prefetch=1,
            grid=(T // tile,),
            in_specs=[pl.BlockSpec(memory_space=pl.ANY),
                      pl.BlockSpec((tile * ROW_TILES, LANES), lambda i, meta: (i, 0))],
            out_specs=pl.BlockSpec(memory_space=pl.ANY),
            scratch_shapes=[pltpu.SMEM((2 * tile * TOP_K,), I32),
                            pltpu.VMEM((ROW_BLOCK * ROW_TILES, LANES), F32),
                            pltpu.SemaphoreType.DMA((2,)),
                            pltpu.SemaphoreType.DMA,
                            pltpu.SemaphoreType.DMA]),
        compiler_params=_cparams(("arbitrary",)),
        name="dispatch",
    )(meta, dest_tiles, v)


def _moe_kernel(be_ref, nu_ref, x_ref, w1_ref, b1_ref, w2_ref, b2_ref, y_ref, w1_scr, w2_scr):
    i = pl.program_id(0)
    prev = be_ref[jnp.maximum(i - 1, 0)]
    changed = jnp.logical_or(i == 0, be_ref[i] != prev)

    @pl.when(changed)
    def _():
        w1_scr[...] = w1_ref[...].astype(BF16)
        w2_scr[...] = w2_ref[...].astype(BF16)

    @pl.when(i < nu_ref[0])
    def _():
        xb = _load_tile_rows(x_ref, ROW_BLOCK).astype(BF16)
        acc = jnp.zeros((ROW_BLOCK, D_MODEL), F32) + b2_ref[...]
        cw = 512
        for c in range(D_FF // cw):
            glu = _dot(xb, w1_scr[:, c * cw:(c + 1) * cw]) + b1_ref[:, c * cw:(c + 1) * cw]
            lin = (_dot(xb, w1_scr[:, D_FF + c * cw:D_FF + (c + 1) * cw])
                   + b1_ref[:, D_FF + c * cw:D_FF + (c + 1) * cw])
            glu = jnp.minimum(glu, SWIGLU_LIMIT)
            lin = jnp.clip(lin, -SWIGLU_LIMIT, SWIGLU_LIMIT)
            act = glu * jax.nn.sigmoid(SWIGLU_ALPHA * glu) * (lin + 1.0)
            acc = acc + _dot(act.astype(BF16), w2_scr[c * cw:(c + 1) * cw, :])
        _store_tile_rows(y_ref, acc)

    @pl.when(i >= nu_ref[0])
    def _():
        y_ref[...] = jnp.zeros_like(y_ref)


def _moe(block_e, n_used, xs, w1, b1, w2, b2, *, n_blocks):
    return pl.pallas_call(
        _moe_kernel,
        out_shape=jax.ShapeDtypeStruct(xs.shape, F32),
        grid_spec=pltpu.PrefetchScalarGridSpec(
            num_scalar_prefetch=2,
            grid=(n_blocks,),
            in_specs=[pl.BlockSpec((ROW_BLOCK * ROW_TILES, LANES), lambda i, be, nu: (i, 0)),
                      pl.BlockSpec((None, D_MODEL, 2 * D_FF), lambda i, be, nu: (be[i], 0, 0)),
                      pl.BlockSpec((None, 1, 2 * D_FF), lambda i, be, nu: (be[i], 0, 0)),
                      pl.BlockSpec((None, D_FF, D_MODEL), lambda i, be, nu: (be[i], 0, 0)),
                      pl.BlockSpec((None, 1, D_MODEL), lambda i, be, nu: (be[i], 0, 0))],
            out_specs=pl.BlockSpec((ROW_BLOCK * ROW_TILES, LANES), lambda i, be, nu: (i, 0)),
            scratch_shapes=[pltpu.VMEM((D_MODEL, 2 * D_FF), BF16),
                            pltpu.VMEM((D_FF, D_MODEL), BF16)]),
        compiler_params=_cparams(("arbitrary",)),
        name="moe_ffn",
    )(block_e, n_used, xs, w1, b1, w2, b2)


def _combine_kernel(dest_hbm, yb_hbm, route_ref, h1_ref, g2_ref, nw_ref, o_ref,
                    idx_smem, rows, sem_idx, sem_rows, *, tile, n_tiles):
    i = pl.program_id(0)
    slot = lax.rem(i, 2)

    def idx_copy(j, s):
        n_asg = tile * TOP_K
        return pltpu.make_async_copy(dest_hbm.at[j], idx_smem.at[pl.ds(s * n_asg, n_asg)],
                                     sem_idx.at[s])

    def issue_rows(s):
        def issue(t, carry):
            for k in range(TOP_K):
                d = pl.multiple_of(idx_smem[s * (tile * TOP_K) + k * tile + t], ROW_TILES)
                dst = rows.at[s, k, pl.ds(pl.multiple_of(t * ROW_TILES, ROW_TILES), ROW_TILES), :]
                pltpu.make_async_copy(yb_hbm.at[pl.ds(d, ROW_TILES), :], dst,
                                      sem_rows.at[s]).start(priority=k % 2)
            return carry
        lax.fori_loop(0, tile, issue, 0, unroll=2)

    @pl.when(i == 0)
    def _():
        idx_copy(0, 0).start()
        idx_copy(0, 0).wait()
        issue_rows(0)

        if n_tiles > 1:
            idx_copy(1, 1).start()

    @pl.when(i + 1 < n_tiles)
    def _():
        idx_copy(i + 1, 1 - slot).wait()
        issue_rows(1 - slot)

        @pl.when(i + 2 < n_tiles)
        def _():
            idx_copy(i + 2, slot).start()

    for k in range(TOP_K):
        pltpu.make_async_copy(yb_hbm.at[pl.ds(0, tile * ROW_TILES), :], rows.at[slot, k],
                              sem_rows.at[slot]).wait()

    route = route_ref[...]
    y = jnp.zeros((tile, D_MODEL), F32)
    for k in range(TOP_K):
        y = y + route[:, TOP_K + k:TOP_K + k + 1] * _load_tile_rows(rows.at[slot, k], tile)
    h = h1_ref[...] + g2_ref[...] * y
    ms = jnp.mean(h * h, axis=-1, keepdims=True)
    o_ref[...] = h * lax.rsqrt(ms + EPS) * nw_ref[...]


def _combine(dest_tiles, yb, route, h1, g2, nw, *, tile, tiles_per_batch):
    T = h1.shape[0]
    return pl.pallas_call(
        functools.partial(_combine_kernel, tile=tile, n_tiles=T // tile),
        out_shape=jax.ShapeDtypeStruct((T, D_MODEL), F32),
        grid=(T // tile,),
        in_specs=[pl.BlockSpec(memory_space=pl.ANY),
                  pl.BlockSpec(memory_space=pl.ANY),
                  pl.BlockSpec((tile, LANES), lambda i: (i, 0)),
                  pl.BlockSpec((tile, D_MODEL), lambda i: (i, 0)),
                  pl.BlockSpec((None, 1, D_MODEL), lambda i: (i // tiles_per_batch, 0, 0)),
                  pl.BlockSpec((1, D_MODEL), lambda i: (0, 0))],
        out_specs=pl.BlockSpec((tile, D_MODEL), lambda i: (i, 0)),
        scratch_shapes=[pltpu.SMEM((2 * tile * TOP_K,), I32),
                        pltpu.VMEM((2, TOP_K, tile * ROW_TILES, LANES), F32),
                        pltpu.SemaphoreType.DMA((2,)),
                        pltpu.SemaphoreType.DMA((2,))],
        compiler_params=_cparams(("arbitrary",)),
        name="combine_norm",
    )(dest_tiles, yb, route, h1, g2, nw)


def _rope_tables(n):
    rows = n // GRID_W
    row = jnp.repeat(jnp.arange(rows, dtype=F32), GRID_W)
    col = jnp.tile(jnp.arange(GRID_W, dtype=F32), rows)
    n_freq = RET_DK // 4
    inv = ROPE_BASE ** (-jnp.arange(n_freq, dtype=F32) / n_freq)
    ang = jnp.concatenate([row[:, None] * inv, col[:, None] * inv], axis=-1)
    cos, sin = jnp.cos(ang), jnp.sin(ang)
    cos_h = jnp.concatenate([cos, cos], axis=-1)
    sin_h = jnp.concatenate([-sin, sin], axis=-1)
    return jnp.tile(cos_h, (1, RET_HEADS)), jnp.tile(sin_h, (1, RET_HEADS))


def kernel(x, c, ctx, c_ctx, w_ada, b_ada, norm_mix_w, norm_ffn_w, w_in, w_out, hg_lb,
           hg_norm_w, router_w, router_b, w1, b1, w2, b2, norm_final_w):
    B, N, D = x.shape
    C = ctx.shape[1]
    T = B * N
    assert D == D_MODEL and w_ada.shape[0] == 1

    cc = jnp.concatenate([c.astype(F32), c_ctx.astype(F32)[None, :],
                          jnp.zeros((16 - B - 1, D), F32)], axis=0)
    mod = _ada(cc, w_ada[0], b_ada[0][None, :])
    mod = mod.reshape(16, 6, 1, D).transpose(1, 0, 2, 3)
    sh1, sc1, g1, sh2, sc2, g2 = (mod[i] for i in range(6))

    w_in_bf = w_in[0].astype(BF16)
    w_out_bf = w_out[0].astype(BF16)
    nw_mix = norm_mix_w[0][None, :]
    cos_t, sin_t = _rope_tables(N)

    pb, lf = _inproj(x, sc1, sh1, None, nw_mix, w_in_bf, hg_lb[:2], cos_t, sin_t,
                     rope=True, tm=512)
    pbc, lfc = _inproj(ctx, sc1, sh1, B, nw_mix, w_in_bf, hg_lb[:2], cos_t[:C], sin_t[:C],
                       rope=False, tm=C)

    a_ret = _retention(pb, pbc)
    a_hg = _hgrn(pb, lf, pbc, lfc, hg_norm_w[0][None, :])

    rw = jnp.zeros((D, LANES), F32).at[:, :N_EXPERTS].set(router_w[0])
    rb = jnp.full((1, LANES), -1e30, F32).at[0, :N_EXPERTS].set(router_b[0])
    h1, v, route, route_t, cnt = _outproj(a_ret, a_hg, w_out_bf, x, g1, sc2, sh2,
                                          norm_ffn_w[0][None, :], rw, rb, tm=512)

    idx = route_t[0:TOP_K].astype(I32)
    rank = route_t[2 * TOP_K:3 * TOP_K].astype(I32)
    counts = cnt[0, :N_EXPERTS].astype(I32)
    padded = (counts + ROW_BLOCK - 1) // ROW_BLOCK * ROW_BLOCK
    pend = jnp.cumsum(padded)
    pstart = pend - padded
    onehot = idx[None] == jnp.arange(N_EXPERTS, dtype=I32)[:, None, None]
    row_start = jnp.sum(jnp.where(onehot, pstart[:, None, None], 0), axis=0)
    dest = (row_start + rank) * ROW_TILES

    def dest_tiles(tile):
        return dest.reshape(TOP_K, T // tile, tile).transpose(1, 0, 2).reshape(T // tile,
                                                                               TOP_K * tile)
    n_blocks = (T * TOP_K) // ROW_BLOCK + N_EXPERTS
    n_used = (pend[-1] // ROW_BLOCK).astype(I32)
    starts = jnp.arange(n_blocks, dtype=I32) * ROW_BLOCK
    block_e = jnp.minimum(jnp.sum((pend[None, :] <= starts[:, None]).astype(I32), axis=1),
                          N_EXPERTS - 1)
    meta = jnp.concatenate([pend.astype(I32), counts, n_used[None]])

    tile_d = 512
    xs = _dispatch(meta, dest_tiles(tile_d), v, tile=tile_d, n_blocks=n_blocks)
    yb = _moe(block_e, n_used[None], xs, w1[0], b1[0][:, None, :], w2[0], b2[0][:, None, :],
              n_blocks=n_blocks)
    tile_c = 256
    out = _combine(dest_tiles(tile_c), yb, route, h1, g2,
                   norm_final_w[None, :], tile=tile_c, tiles_per_batch=N // tile_c)
    return out.reshape(B, N, D)
```

```python
import functools
import math

import jax
import jax.numpy as jnp
import numpy as np
from jax import lax
from jax.experimental import pallas as pl
from jax.experimental.pallas import tpu as pltpu

F32 = jnp.float32
BF16 = jnp.bfloat16
I32 = jnp.int32

D_MODEL = 1024
GRID_W = 64
RET_HEADS = 4
RET_DK = 64
HG_HEADS = 4
PROJ_W = 4096
ROPE_BASE = 10000.0
EPS = 1e-6
N_EXPERTS = 32
TOP_K = 4
D_FF = 1024
SWIGLU_LIMIT = 7.0
SWIGLU_ALPHA = 1.702

LANES = 128
CHUNK = 128
ROW_BLOCK = 512
VMEM_LIMIT = 56 * 1024 * 1024
ROW_TILES = D_MODEL // LANES

C_RQ, C_RK, C_RV, C_RG, C_HQ, C_FF, C_FB, C_HV, C_HG = (
    0, 256, 512, 1024, 1536, 2048, 2560, 3072, 3584)


def _cparams(sem):
    return pltpu.CompilerParams(dimension_semantics=sem, vmem_limit_bytes=VMEM_LIMIT)


def _split_bf16(x):
    hi = x.astype(BF16)
    lo = (x - hi.astype(F32)).astype(BF16)
    return hi, lo


def _dot(a, b):
    return jnp.dot(a, b, preferred_element_type=F32)


def _dot_nt(a, b):
    return lax.dot_general(a, b, (((1,), (1,)), ((), ())), preferred_element_type=F32)


def _dot3(a, b):
    ah, al = _split_bf16(a)
    bh, bl = _split_bf16(b)
    return _dot(ah, bh) + (_dot(ah, bl) + _dot(al, bh))


def _silu(x):
    return x * jax.nn.sigmoid(x)


def _load_tile_rows(ref, n):
    return jnp.concatenate([ref[pl.ds(s, n, stride=ROW_TILES), :] for s in range(ROW_TILES)],
                           axis=1)


def _store_tile_rows(ref, x):
    n = x.shape[0]
    for s in range(ROW_TILES):
        ref[pl.ds(s, n, stride=ROW_TILES), :] = x[:, s * LANES:(s + 1) * LANES]


def _ada_kernel(c_ref, w_ref, b_ref, o_ref):
    s = _silu(c_ref[...])
    o_ref[...] = _dot3(s, w_ref[...]) + b_ref[...]


def _ada(cc, w, b):
    nblk = w.shape[1] // D_MODEL
    return pl.pallas_call(
        _ada_kernel,
        out_shape=jax.ShapeDtypeStruct((cc.shape[0], w.shape[1]), F32),
        grid=(nblk,),
        in_specs=[pl.BlockSpec(cc.shape, lambda j: (0, 0)),
                  pl.BlockSpec((D_MODEL, D_MODEL), lambda j: (0, j)),
                  pl.BlockSpec((1, D_MODEL), lambda j: (0, j))],
        out_specs=pl.BlockSpec((cc.shape[0], D_MODEL), lambda j: (0, j)),
        compiler_params=_cparams(("arbitrary",)),
        name="ada",
    )(cc, w, b)


def _inproj_kernel(x_ref, sc_ref, sh_ref, nw_ref, w_ref, lb_ref, cos_ref, sin_ref,
                   pb_ref, lf_ref, u_scr, *, rope):
    x = x_ref[...]
    ms = jnp.mean(x * x, axis=-1, keepdims=True)
    u = x * lax.rsqrt(ms + EPS) * nw_ref[...] * (1.0 + sc_ref[...]) + sh_ref[...]
    u_scr[...] = u.astype(BF16)

    def proj(lo, width):
        return _dot(u_scr[...], w_ref[:, lo:lo + width])

    tm = x.shape[0]
    if rope:
        lane = lax.broadcasted_iota(I32, (tm, LANES), 1)
        first = (lane & 32) == 0

    def put_rot(col, scale):
        for j in range(2):
            lo = col + j * LANES
            t = proj(lo, LANES)
            if scale != 1.0:
                t = t * scale
            if rope:
                tb = j * LANES
                rot = jnp.where(first, pltpu.roll(t, 96, axis=1), pltpu.roll(t, 32, axis=1))
                t = t * cos_ref[:, tb:tb + LANES] + rot * sin_ref[:, tb:tb + LANES]
            pb_ref[:, lo:lo + LANES] = t.astype(BF16)

    put_rot(C_RQ, 1.0)
    put_rot(C_RK, RET_DK ** -0.5)
    pb_ref[:, C_RV:C_RV + 512] = proj(C_RV, 512).astype(BF16)
    pb_ref[:, C_RG:C_RG + 512] = _silu(proj(C_RG, 512)).astype(BF16)
    pb_ref[:, C_HQ:C_HQ + 512] = _silu(proj(C_HQ, 512)).astype(BF16)
    pb_ref[:, C_HV:C_HV + 512] = proj(C_HV, 512).astype(BF16)
    pb_ref[:, C_HG:C_HG + 512] = _silu(proj(C_HG, 512)).astype(BF16)

    la = lb_ref[0]
    lbb = lb_ref[1]
    mx = jnp.maximum(la, lbb)
    ea = jnp.exp(la - mx)
    eb = jnp.exp(lbb - mx)
    lb = ea / (ea + eb)
    for d, col in enumerate((C_FF, C_FB)):
        lbd = lb[d:d + 1, :]
        f = lbd + (1.0 - lbd) * jax.nn.sigmoid(proj(col, 512))
        pb_ref[:, col:col + 512] = (1.0 - f).astype(BF16)
        lf_ref[:, d * 512:(d + 1) * 512] = jnp.log(f)


def _inproj(x, sc, sh, mod_row, nw, w_bf, hg_lb, cos_t, sin_t, *, rope, tm):
    B, n, _ = x.shape
    nt = n // tm
    if mod_row is None:
        mrow = lambda b, j: (b, 0, 0)
    else:
        mrow = lambda b, j: (mod_row, 0, 0)
    return pl.pallas_call(
        functools.partial(_inproj_kernel, rope=rope),
        out_shape=(jax.ShapeDtypeStruct((B, n, PROJ_W), BF16),
                   jax.ShapeDtypeStruct((B, n, 1024), F32)),
        grid=(B, nt),
        in_specs=[pl.BlockSpec((None, tm, D_MODEL), lambda b, j: (b, j, 0)),
                  pl.BlockSpec((None, 1, D_MODEL), mrow),
                  pl.BlockSpec((None, 1, D_MODEL), mrow),
                  pl.BlockSpec((1, D_MODEL), lambda b, j: (0, 0)),
                  pl.BlockSpec((D_MODEL, PROJ_W), lambda b, j: (0, 0)),
                  pl.BlockSpec((2, 2, 512), lambda b, j: (0, 0, 0)),
                  pl.BlockSpec((tm, 256), lambda b, j: (j, 0)),
                  pl.BlockSpec((tm, 256), lambda b, j: (j, 0))],
        out_specs=(pl.BlockSpec((None, tm, PROJ_W), lambda b, j: (b, j, 0)),
                   pl.BlockSpec((None, tm, 1024), lambda b, j: (b, j, 0))),
        scratch_shapes=[pltpu.VMEM((tm, D_MODEL), BF16)],
        compiler_params=_cparams(("arbitrary", "arbitrary")),
        name="inproj_rope" if rope else "inproj_ctx",
    )(x, sc, sh, nw, w_bf, hg_lb, cos_t, sin_t)


_RET_LGF = [math.log1p(-(2.0 ** (-5.0 - 2.0 * h))) for h in range(RET_HEADS)]
_RET_LGB = [math.log1p(-(2.0 ** (-6.0 - 2.0 * h))) for h in range(RET_HEADS)]


def _ret_kernel(q_ref, k_ref, v_ref, g_ref, kc_ref, vc_ref, o_ref,
                u_scr, sin_scr, kt_scr, dtot_scr, *, n_lat, n_ctx):
    L = CHUNK
    pair = pl.program_id(1)
    row = lax.broadcasted_iota(I32, (L, L), 0).astype(F32)
    col = lax.broadcasted_iota(I32, (L, L), 1).astype(F32)
    lane = lax.broadcasted_iota(I32, (L, LANES), 1)
    trow = lax.broadcasted_iota(I32, (L, 1), 0).astype(F32)
    tcol = lax.broadcasted_iota(I32, (1, L), 1).astype(F32)
    low_half = lane < RET_DK

    def u_chunk(k_blk, v_blk, ci, store_kt):
        kt = k_blk.astype(F32).T
        if store_kt is not None:
            kt_scr[store_kt] = kt.astype(BF16)
        for hh in range(2):
            lgf = jnp.where(pair == 0, _RET_LGF[hh], _RET_LGF[2 + hh])
            lgb = jnp.where(pair == 0, _RET_LGB[hh], _RET_LGB[2 + hh])
            kth = kt[hh * RET_DK:(hh + 1) * RET_DK, :]
            wkf = jnp.exp(lgf * (L - 1.0 - tcol))
            wkb = jnp.exp(lgb * tcol)
            lhs = jnp.concatenate([kth * wkf, kth * wkb], axis=0).astype(BF16)
            u_scr[hh, ci] = _dot(lhs, v_blk[:, hh * LANES:(hh + 1) * LANES])

    for c in range(n_ctx):
        u_chunk(kc_ref[c * L:(c + 1) * L, :], vc_ref[c * L:(c + 1) * L, :], c, None)

    def lat_u(c, carry):
        r0 = pl.multiple_of(c * L, L)
        u_chunk(k_ref[pl.ds(r0, L), :], v_ref[pl.ds(r0, L), :], n_ctx + c, c)
        return carry
    lax.fori_loop(0, n_lat, lat_u, 0, unroll=2)

    ones = jnp.ones((RET_DK, LANES), F32)
    for hh in range(2):
        lgf = jnp.where(pair == 0, _RET_LGF[hh], _RET_LGF[2 + hh])
        lgb = jnp.where(pair == 0, _RET_LGB[hh], _RET_LGB[2 + hh])
        d = row - col
        dtot_scr[hh] = jnp.where(d > 0, jnp.exp(lgf * jnp.maximum(d, 0.0)),
                                 jnp.where(d < 0, jnp.exp(lgb * jnp.maximum(-d, 0.0)), 2.0))
        af = jnp.exp(ones * (lgf * L))
        ab = jnp.exp(ones * (lgb * L))

        s = jnp.zeros((RET_DK, LANES), F32)
        for c in range(n_ctx):
            s = af * s + u_scr[hh, c, 0:RET_DK, :]
        sb = jnp.zeros((RET_DK, LANES), F32)
        for c in reversed(range(n_ctx)):
            sb = ab * sb + u_scr[hh, c, RET_DK:2 * RET_DK, :]

        def fwd(c, s, hh=hh, af=af):
            sin_scr[hh, c, 0:RET_DK, :] = s.astype(BF16)
            return af * s + u_scr[hh, n_ctx + c, 0:RET_DK, :]
        lax.fori_loop(0, n_lat, fwd, s)

        def bwd(i, sb, hh=hh, ab=ab):
            c = n_lat - 1 - i
            sin_scr[hh, c, RET_DK:2 * RET_DK, :] = sb.astype(BF16)
            return ab * sb + u_scr[hh, n_ctx + c, RET_DK:2 * RET_DK, :]
        lax.fori_loop(0, n_lat, bwd, sb)

    def out_chunk(c, carry):
        r0 = pl.multiple_of(c * L, L)
        q = q_ref[pl.ds(r0, L), :].astype(F32)
        qr = pltpu.roll(q, RET_DK, axis=1)
        kt = kt_scr[c]
        for hh in range(2):
            lgf = jnp.where(pair == 0, _RET_LGF[hh], _RET_LGF[2 + hh])
            lgb = jnp.where(pair == 0, _RET_LGB[hh], _RET_LGB[2 + hh])
            mine = low_half if hh == 0 else jnp.logical_not(low_half)
            qm = jnp.where(mine, q, 0.0).astype(BF16)
            p = (_dot(qm, kt) * dtot_scr[hh]).astype(BF16)
            vh = v_ref[pl.ds(r0, L), hh * LANES:(hh + 1) * LANES]
            wqf = jnp.exp(lgf * (trow + 1.0))
            wqb = jnp.exp(lgb * (L - trow))
            qa, qb = (q, qr) if hh == 0 else (qr, q)
            qs = jnp.where(low_half, qa * wqf, qb * wqb).astype(BF16)
            o = _dot(p, vh) + _dot(qs, sin_scr[hh, c])
            ms = jnp.mean(o * o, axis=-1, keepdims=True)
            gh = g_ref[pl.ds(r0, L), hh * LANES:(hh + 1) * LANES].astype(F32)
            o_ref[pl.ds(r0, L), hh * LANES:(hh + 1) * LANES] = (
                o * lax.rsqrt(ms + EPS) * gh).astype(BF16)
        return carry
    lax.fori_loop(0, n_lat, out_chunk, 0, unroll=2)


def _retention(pb, pbc):
    B, n, _ = pb.shape
    nc = pbc.shape[1]
    n_lat, n_ctx = n // CHUNK, nc // CHUNK
    return pl.pallas_call(
        functools.partial(_ret_kernel, n_lat=n_lat, n_ctx=n_ctx),
        out_shape=jax.ShapeDtypeStruct((B, n, 512), BF16),
        grid=(B, 2),
        in_specs=[pl.BlockSpec((None, n, LANES), lambda b, p: (b, 0, C_RQ // LANES + p)),
                  pl.BlockSpec((None, n, LANES), lambda b, p: (b, 0, C_RK // LANES + p)),
                  pl.BlockSpec((None, n, 256), lambda b, p: (b, 0, C_RV // 256 + p)),
                  pl.BlockSpec((None, n, 256), lambda b, p: (b, 0, C_RG // 256 + p)),
                  pl.BlockSpec((None, nc, LANES), lambda b, p: (b, 0, C_RK // LANES + p)),
                  pl.BlockSpec((None, nc, 256), lambda b, p: (b, 0, C_RV // 256 + p))],
        out_specs=pl.BlockSpec((None, n, 256), lambda b, p: (b, 0, p)),
        scratch_shapes=[pltpu.VMEM((2, n_lat + n_ctx, CHUNK, LANES), F32),
                        pltpu.VMEM((2, n_lat, CHUNK, LANES), BF16),
                        pltpu.VMEM((n_lat, LANES, CHUNK), BF16),
                        pltpu.VMEM((2, CHUNK, CHUNK), F32)],
        compiler_params=_cparams(("arbitrary", "arbitrary")),
        name="retention",
    )(pb, pb, pb, pb, pbc, pbc)


_LEVELS = (64, 32, 16, 8, 4, 2, 1)


def _expand_rows(r, rep):
    n = r.shape[0]
    if n == 1:
        return jnp.broadcast_to(r, (rep, r.shape[1]))
    return jnp.concatenate(
        [jnp.broadcast_to(r[i:i + 1, :], (rep, r.shape[1])) for i in range(n)], axis=0)


def _hgrn_kernel(q_ref, kf_ref, kb_ref, v_ref, g_ref, lff_ref, lfb_ref,
                 kfc_ref, kbc_ref, vc_ref, lffc_ref, lfbc_ref, nw_ref, o_ref,
                 ut_scr, a_scr, qs_scr, oi_scr, sin_scr, bfb_scr, *, n_lat, n_ctx):
    L = CHUNK
    row = lax.broadcasted_iota(I32, (L, L), 0)
    col = lax.broadcasted_iota(I32, (L, L), 1)
    xr_bits = lax.bitcast_convert_type((row ^ col).astype(F32), I32)
    lv = lax.shift_right_logical(xr_bits, 23) - 127
    row2 = lax.broadcasted_iota(I32, (L, 2 * L), 0)
    col2 = lax.broadcasted_iota(I32, (L, 2 * L), 1) & (L - 1)
    tril2 = jnp.where(col2 <= row2, 1.0, 0.0).astype(BF16)
    triu2 = jnp.where(col2 >= row2, 1.0, 0.0).astype(BF16)

    def cums(lff, lfb):
        hf, lof = _split_bf16(lff)
        hb, lob = _split_bf16(lfb)
        bf = _dot(tril2, jnp.concatenate([hf, lof], axis=0))
        bb = _dot(triu2, jnp.concatenate([hb, lob], axis=0))
        return bf, bb

    def state_part(ci, kf, kb, v_blk, bf, bb):
        endf = bf[L - 1:L, :]
        endb = bb[0:1, :]
        ksf = kf * jnp.exp(endf - bf)
        ksb = kb * jnp.exp(endb - bb)
        vt = v_blk.astype(F32).T.astype(BF16)
        ut_scr[ci] = _dot(vt, jnp.concatenate([ksf, ksb], axis=1).astype(BF16))
        a_scr[ci] = jnp.broadcast_to(
            jnp.concatenate([jnp.exp(endf), jnp.exp(endb)], axis=1), (8, 2 * LANES))

    for c in range(n_ctx):
        sl = slice(c * L, (c + 1) * L)
        bf, bb = cums(lffc_ref[sl, :], lfbc_ref[sl, :])
        state_part(c, kfc_ref[sl, :].astype(F32), kbc_ref[sl, :].astype(F32),
                   vc_ref[sl, :], bf, bb)

    def lat_chunk(c, u):
        rows = pl.ds(pl.multiple_of(c * L, L), L)
        bf_scr = bfb_scr.at[u, 0]
        bb_scr = bfb_scr.at[u, 1]
        q = q_ref[rows, :].astype(F32)
        kf = kf_ref[rows, :].astype(F32)
        kb = kb_ref[rows, :].astype(F32)
        v_blk = v_ref[rows, :]
        lff = lff_ref[rows, :]
        lfb = lfb_ref[rows, :]
        bf, bb = cums(lff, lfb)
        state_part(n_ctx + c, kf, kb, v_blk, bf, bb)
        qs_scr[rows, :] = jnp.concatenate([q * jnp.exp(bf), q * jnp.exp(bb)],
                                          axis=1).astype(BF16)
        bf_scr[...] = bf
        bb_scr[...] = bb

        acc = jnp.zeros((L, L), F32)
        for lvl, h in enumerate(_LEVELS):
            bit = (row & h) != 0
            ksel = jnp.where(bit, kb, kf)
            if h >= 4:
                n = (L // 2) // h
                if n == 1:
                    rf = bf_scr[h - 1:h, :]
                    rb = bb_scr[h:h + 1, :]
                else:
                    rf = bf_scr[pl.ds(h - 1, n, stride=2 * h), :]
                    rb = bb_scr[pl.ds(h, n, stride=2 * h), :]
                df = bf - _expand_rows(rf, 2 * h)
                db = bb - _expand_rows(rb, 2 * h)
                eq = jnp.where(bit, df, db)
                ek = -jnp.where(bit, db, df)
            elif h == 2:
                m = row & 3
                lff_n = pltpu.roll(lff, L - 1, axis=0)
                lfb_n = pltpu.roll(lfb, L - 1, axis=0)
                eq = jnp.where(m == 2, lff,
                               jnp.where(m == 3, lff + pltpu.roll(lff, 1, axis=0),
                                         jnp.where(m == 0, lfb + lfb_n, lfb)))
                ek = jnp.where(m == 3, pltpu.roll(lfb, 1, axis=0),
                               jnp.where(m == 0, lff_n, 0.0))
            else:
                eq = jnp.where(bit, lff, lfb)
                ek = None
            lhs = (q * jnp.exp(eq)).astype(BF16)
            rhs = (ksel if ek is None else ksel * jnp.exp(ek)).astype(BF16)
            acc = jnp.where(lv == 6 - lvl, _dot_nt(lhs, rhs), acc)

        dsum = jnp.sum(q * (kf + kb), axis=-1, keepdims=True)
        oi_scr[rows, :] = _dot(acc.astype(BF16), v_blk) + dsum * v_blk.astype(F32)

    def lat_pair(i, carry):
        for u in range(2):
            lat_chunk(2 * i + u, u)
        return carry
    lax.fori_loop(0, n_lat // 2, lat_pair, 0)

    st = jnp.zeros((LANES, LANES), F32)
    for c in range(n_ctx):
        st = st * a_scr[c, 0:1, 0:LANES] + ut_scr[c, :, 0:LANES]
    stb = jnp.zeros((LANES, LANES), F32)
    for c in reversed(range(n_ctx)):
        stb = stb * a_scr[c, 0:1, LANES:2 * LANES] + ut_scr[c, :, LANES:2 * LANES]

    def fwd(c, st):
        sin_scr[c, :, 0:LANES] = st.astype(BF16)
        ci = n_ctx + c
        return st * a_scr[ci, 0:1, 0:LANES] + ut_scr[ci, :, 0:LANES]
    lax.fori_loop(0, n_lat, fwd, st)

    def bwd(i, stb):
        c = n_lat - 1 - i
        sin_scr[c, :, LANES:2 * LANES] = stb.astype(BF16)
        ci = n_ctx + c
        return stb * a_scr[ci, 0:1, LANES:2 * LANES] + ut_scr[ci, :, LANES:2 * LANES]
    lax.fori_loop(0, n_lat, bwd, stb)

    def out_chunk(c, carry):
        rows = pl.ds(pl.multiple_of(c * L, L), L)
        o = oi_scr[rows, :] + _dot_nt(qs_scr[rows, :], sin_scr[c])
        ms = jnp.mean(o * o, axis=-1, keepdims=True)
        y = o * lax.rsqrt(ms + EPS) * nw_ref[...] * g_ref[rows, :].astype(F32)
        o_ref[rows, :] = y.astype(BF16)
        return carry
    lax.fori_loop(0, n_lat, out_chunk, 0, unroll=4)


def _hgrn(pb, lf, pbc, lfc, nw):
    B, n, _ = pb.shape
    nc = pbc.shape[1]
    n_lat, n_ctx = n // CHUNK, nc // CHUNK

    def colblk(rows, col0):
        return pl.BlockSpec((None, rows, LANES), lambda b, h: (b, 0, col0 // LANES + h))

    return pl.pallas_call(
        functools.partial(_hgrn_kernel, n_lat=n_lat, n_ctx=n_ctx),
        out_shape=jax.ShapeDtypeStruct((B, n, 512), BF16),
        grid=(B, HG_HEADS),
        in_specs=[colblk(n, C_HQ), colblk(n, C_FF), colblk(n, C_FB), colblk(n, C_HV),
                  colblk(n, C_HG), colblk(n, 0), colblk(n, 512),
                  colblk(nc, C_FF), colblk(nc, C_FB), colblk(nc, C_HV),
                  colblk(nc, 0), colblk(nc, 512),
                  pl.BlockSpec((1, LANES), lambda b, h: (0, 0))],
        out_specs=pl.BlockSpec((None, n, LANES), lambda b, h: (b, 0, h)),
        scratch_shapes=[pltpu.VMEM((n_lat + n_ctx, LANES, 2 * LANES), F32),
                        pltpu.VMEM((n_lat + n_ctx, 8, 2 * LANES), F32),
                        pltpu.VMEM((n, 2 * LANES), BF16),
                        pltpu.VMEM((n, LANES), F32),
                        pltpu.VMEM((n_lat, LANES, 2 * LANES), BF16),
                        pltpu.VMEM((2, 2, CHUNK, LANES), F32)],
        compiler_params=_cparams(("arbitrary", "arbitrary")),
        name="hgrn2",
    )(pb, pb, pb, pb, pb, lf, lf, pbc, pbc, pbc, lfc, lfc, nw)


def _outproj_kernel(ar_ref, ah_ref, w_ref, x_ref, g1_ref, sc_ref, sh_ref, nw_ref,
                    rw_ref, rb_ref, h1_ref, v_ref, route_ref, route_t_ref, cnt_ref, cnt_scr):
    first_step = jnp.logical_and(pl.program_id(0) == 0, pl.program_id(1) == 0)

    @pl.when(first_step)
    def _():
        cnt_scr[...] = jnp.zeros_like(cnt_scr)

    y = _dot(ar_ref[...], w_ref[0:512, :]) + _dot(ah_ref[...], w_ref[512:1024, :])
    h1 = x_ref[...] + g1_ref[...] * y
    h1_ref[...] = h1
    ms = jnp.mean(h1 * h1, axis=-1, keepdims=True)
    v = h1 * lax.rsqrt(ms + EPS) * nw_ref[...] * (1.0 + sc_ref[...]) + sh_ref[...]
    _store_tile_rows(v_ref, v)

    tm = v.shape[0]
    logits = _dot3(v, rw_ref[...]) + rb_ref[...]
    lane = lax.broadcasted_iota(I32, (tm, LANES), 1)
    lane_f = lane.astype(F32)
    l = logits
    sels, tops, idxs = [], [], []
    for _ in range(TOP_K):
        m = jnp.max(l, axis=1, keepdims=True)
        i = jnp.min(jnp.where(l == m, lane_f, float(LANES)), axis=1, keepdims=True)
        sel = lane_f == i
        l = jnp.where(sel, -jnp.inf, l)
        sels.append(sel)
        tops.append(m)
        idxs.append(i)
    es = [jnp.exp(t - tops[0]) for t in tops]
    den = es[0] + es[1] + es[2] + es[3]
    gates = [e / den for e in es]

    oh = jnp.zeros((tm, LANES), F32)
    for sel in sels:
        oh = jnp.where(sel, 1.0, oh)
    r = lax.broadcasted_iota(I32, (tm, tm), 0)
    c = lax.broadcasted_iota(I32, (tm, tm), 1)
    tri = jnp.where(c < r, 1.0, 0.0).astype(BF16)
    before = _dot(tri, oh.astype(BF16)) + cnt_scr[0:1, :]
    ranks = [jnp.sum(jnp.where(sel, before, 0.0), axis=1, keepdims=True) for sel in sels]
    cnt_scr[...] = cnt_scr[...] + jnp.sum(oh, axis=0, keepdims=True)
    cnt_ref[...] = cnt_scr[...]

    out = jnp.zeros((tm, LANES), F32)
    for k in range(TOP_K):
        out = jnp.where(lane == k, idxs[k], out)
        out = jnp.where(lane == TOP_K + k, gates[k], out)
        out = jnp.where(lane == 2 * TOP_K + k, ranks[k], out)
    route_ref[...] = out
    route_t_ref[...] = out.T[0:16, :]


def _outproj(a_ret, a_hg, w_bf, x, g1, sc2, sh2, nw, rw, rb, *, tm):
    B, n, _ = x.shape
    nt = n // tm
    T = B * n
    mrow = lambda b, j: (b, 0, 0)
    tok = lambda b, j: (b * nt + j, 0)
    return pl.pallas_call(
        _outproj_kernel,
        out_shape=(jax.ShapeDtypeStruct((T, D_MODEL), F32),
                   jax.ShapeDtypeStruct((T * ROW_TILES, LANES), F32),
                   jax.ShapeDtypeStruct((T, LANES), F32),
                   jax.ShapeDtypeStruct((16, T), F32),
                   jax.ShapeDtypeStruct((8, LANES), F32)),
        grid=(B, nt),
        in_specs=[pl.BlockSpec((None, tm, 512), lambda b, j: (b, j, 0)),
                  pl.BlockSpec((None, tm, 512), lambda b, j: (b, j, 0)),
                  pl.BlockSpec((D_MODEL, D_MODEL), lambda b, j: (0, 0)),
                  pl.BlockSpec((None, tm, D_MODEL), lambda b, j: (b, j, 0)),
                  pl.BlockSpec((None, 1, D_MODEL), mrow),
                  pl.BlockSpec((None, 1, D_MODEL), mrow),
                  pl.BlockSpec((None, 1, D_MODEL), mrow),
                  pl.BlockSpec((1, D_MODEL), lambda b, j: (0, 0)),
                  pl.BlockSpec((D_MODEL, LANES), lambda b, j: (0, 0)),
                  pl.BlockSpec((1, LANES), lambda b, j: (0, 0))],
        out_specs=(pl.BlockSpec((tm, D_MODEL), tok),
                   pl.BlockSpec((tm * ROW_TILES, LANES), tok),
                   pl.BlockSpec((tm, LANES), tok),
                   pl.BlockSpec((16, tm), lambda b, j: (0, b * nt + j)),
                   pl.BlockSpec((8, LANES), lambda b, j: (0, 0))),
        scratch_shapes=[pltpu.VMEM((8, LANES), F32)],
        compiler_params=_cparams(("arbitrary", "arbitrary")),
        name="outproj_router",
    )(a_ret, a_hg, w_bf, x, g1, sc2, sh2, nw, rw, rb)


def _dispatch_kernel(meta_ref, dest_hbm, v_ref, xs_hbm, idx_smem, zbuf, sem_idx, sem_z, sem_rows,
                     *, tile, n_blocks, n_tiles):
    i = pl.program_id(0)
    slot = lax.rem(i, 2)

    def idx_copy(j, s):
        n_asg = tile * TOP_K
        return pltpu.make_async_copy(dest_hbm.at[j], idx_smem.at[pl.ds(s * n_asg, n_asg)],
                                     sem_idx.at[s])

    @pl.when(i == 0)
    def _():
        idx_copy(0, 0).start()

    def zero_copy(r0):
        return pltpu.make_async_copy(
            zbuf, xs_hbm.at[pl.ds(r0 * ROW_TILES, ROW_BLOCK * ROW_TILES), :], sem_z)

    @pl.when(i == 0)
    def _():
        zbuf[...] = jnp.zeros_like(zbuf)
        n_used = meta_ref[2 * N_EXPERTS]

        def tail_start(e):
            return pl.multiple_of(meta_ref[e] - ROW_BLOCK, ROW_BLOCK)

        def has_tail(e):
            return (meta_ref[N_EXPERTS + e] & (ROW_BLOCK - 1)) != 0

        def z_issue(e, carry):
            @pl.when(has_tail(e))
            def _():
                zero_copy(tail_start(e)).start()
            return carry
        lax.fori_loop(0, N_EXPERTS, z_issue, 0)

        def u_issue(blk, carry):
            @pl.when(blk >= n_used)
            def _():
                zero_copy(pl.multiple_of(blk * ROW_BLOCK, ROW_BLOCK)).start()
            return carry
        lax.fori_loop(0, n_blocks, u_issue, 0)

        def z_wait(e, carry):
            @pl.when(has_tail(e))
            def _():
                zero_copy(0).wait()
            return carry
        lax.fori_loop(0, N_EXPERTS, z_wait, 0)

        def u_wait(blk, carry):
            @pl.when(blk >= n_used)
            def _():
                zero_copy(0).wait()
            return carry
        lax.fori_loop(0, n_blocks, u_wait, 0)

    idx_copy(i, slot).wait()

    @pl.when(i + 1 < n_tiles)
    def _():
        idx_copy(i + 1, 1 - slot).start()

    def issue(t, carry):
        for k in range(TOP_K):
            d = pl.multiple_of(idx_smem[slot * (tile * TOP_K) + k * tile + t], ROW_TILES)
            src = v_ref.at[pl.ds(pl.multiple_of(t * ROW_TILES, ROW_TILES), ROW_TILES), :]
            pltpu.make_async_copy(src, xs_hbm.at[pl.ds(d, ROW_TILES), :],
                                  sem_rows).start(priority=k % 2)
        return carry
    lax.fori_loop(0, tile, issue, 0, unroll=2)
    for _ in range(TOP_K):
        pltpu.make_async_copy(v_ref, xs_hbm.at[pl.ds(0, tile * ROW_TILES), :], sem_rows).wait()


def _dispatch(meta, dest_tiles, v, *, tile, n_blocks):
    T = v.shape[0] // ROW_TILES
    R = n_blocks * ROW_BLOCK
    return pl.pallas_call(
        functools.partial(_dispatch_kernel, tile=tile, n_blocks=n_blocks, n_tiles=T // tile),
        out_shape=jax.ShapeDtypeStruct((R * ROW_TILES, LANES), F32),
        grid_spec=pltpu.PrefetchScalarGridSpec(
            num_scalar_prefetch=1,
            grid=(T // tile,),
            in_specs=[pl.BlockSpec(memory_space=pl.ANY),
                      pl.BlockSpec((tile * ROW_TILES, LANES), lambda i, meta: (i, 0))],
            out_specs=pl.BlockSpec(memory_space=pl.ANY),
            scratch_shapes=[pltpu.SMEM((2 * tile * TOP_K,), I32),
                            pltpu.VMEM((ROW_BLOCK * ROW_TILES, LANES), F32),
                            pltpu.SemaphoreType.DMA((2,)),
                            pltpu.SemaphoreType.DMA,
                            pltpu.SemaphoreType.DMA]),
        compiler_params=_cparams(("arbitrary",)),
        name="dispatch",
    )(meta, dest_tiles, v)


def _moe_kernel(be_ref, nu_ref, x_ref, w1_ref, b1_ref, w2_ref, b2_ref, y_ref,
                w1_scr, w2_scr, x_scr, act_scr):
    i = pl.program_id(0)
    prev = be_ref[jnp.maximum(i - 1, 0)]
    changed = jnp.logical_or(i == 0, be_ref[i] != prev)

    @pl.when(changed)
    def _():
        w1_scr[...] = w1_ref[...].astype(BF16)
        w2_scr[...] = w2_ref[...].astype(BF16)

    @pl.when(i < nu_ref[0])
    def _():
        x_scr[...] = _load_tile_rows(x_ref, ROW_BLOCK).astype(BF16)
        cw = 256
        for c in range(D_FF // cw):
            glu = _dot(x_scr[...], w1_scr[:, c * cw:(c + 1) * cw]) + b1_ref[:, c * cw:(c + 1) * cw]
            lin = (_dot(x_scr[...], w1_scr[:, D_FF + c * cw:D_FF + (c + 1) * cw])
                   + b1_ref[:, D_FF + c * cw:D_FF + (c + 1) * cw])
            glu = jnp.minimum(glu, SWIGLU_LIMIT)
            lin = jnp.clip(lin, -SWIGLU_LIMIT, SWIGLU_LIMIT)
            act = glu * jax.nn.sigmoid(SWIGLU_ALPHA * glu) * (lin + 1.0)
            act_scr[:, c * cw:(c + 1) * cw] = act.astype(BF16)
        for c in range(D_MODEL // cw):
            y = _dot(act_scr[...], w2_scr[:, c * cw:(c + 1) * cw]) + b2_ref[:, c * cw:(c + 1) * cw]
            for s in range(cw // LANES):
                t = c * (cw // LANES) + s
                y_ref[pl.ds(t, ROW_BLOCK, stride=ROW_TILES), :] = y[:, s * LANES:(s + 1) * LANES]

    @pl.when(i >= nu_ref[0])
    def _():
        y_ref[...] = jnp.zeros_like(y_ref)


def _moe(block_e, n_used, xs, w1, b1, w2, b2, *, n_blocks):
    return pl.pallas_call(
        _moe_kernel,
        out_shape=jax.ShapeDtypeStruct(xs.shape, F32),
        grid_spec=pltpu.PrefetchScalarGridSpec(
            num_scalar_prefetch=2,
            grid=(n_blocks,),
            in_specs=[pl.BlockSpec((ROW_BLOCK * ROW_TILES, LANES),
                                   lambda i, be, nu: (jnp.minimum(i, nu[0] - 1), 0)),
                      pl.BlockSpec((None, D_MODEL, 2 * D_FF), lambda i, be, nu: (be[i], 0, 0)),
                      pl.BlockSpec((None, 1, 2 * D_FF), lambda i, be, nu: (be[i], 0, 0)),
                      pl.BlockSpec((None, D_FF, D_MODEL), lambda i, be, nu: (be[i], 0, 0)),
                      pl.BlockSpec((None, 1, D_MODEL), lambda i, be, nu: (be[i], 0, 0))],
            out_specs=pl.BlockSpec((ROW_BLOCK * ROW_TILES, LANES), lambda i, be, nu: (i, 0)),
            scratch_shapes=[pltpu.VMEM((D_MODEL, 2 * D_FF), BF16),
                            pltpu.VMEM((D_FF, D_MODEL), BF16),
                            pltpu.VMEM((ROW_BLOCK, D_MODEL), BF16),
                            pltpu.VMEM((ROW_BLOCK, D_FF), BF16)]),
        compiler_params=_cparams(("arbitrary",)),
        name="moe_ffn",
    )(block_e, n_used, xs, w1, b1, w2, b2)


def _combine_kernel(dest_hbm, yb_hbm, route_ref, h1_ref, g2_ref, nw_ref, o_ref,
                    idx_smem, rows, sem_idx, sem_rows, *, tile, n_tiles):
    i = pl.program_id(0)
    slot = lax.rem(i, 2)

    def idx_copy(j, s):
        n_asg = tile * TOP_K
        return pltpu.make_async_copy(dest_hbm.at[j], idx_smem.at[pl.ds(s * n_asg, n_asg)],
                                     sem_idx.at[s])

    def issue_rows(s):
        def issue(t, carry):
            for k in range(TOP_K):
                d = pl.multiple_of(idx_smem[s * (tile * TOP_K) + k * tile + t], ROW_TILES)
                dst = rows.at[s, k, pl.ds(pl.multiple_of(t * ROW_TILES, ROW_TILES), ROW_TILES), :]
                pltpu.make_async_copy(yb_hbm.at[pl.ds(d, ROW_TILES), :], dst,
                                      sem_rows.at[s]).start(priority=k % 2)
            return carry
        lax.fori_loop(0, tile, issue, 0, unroll=2)

    @pl.when(i == 0)
    def _():
        idx_copy(0, 0).start()
        idx_copy(0, 0).wait()
        issue_rows(0)

        if n_tiles > 1:
            idx_copy(1, 1).start()

    @pl.when(i + 1 < n_tiles)
    def _():
        idx_copy(i + 1, 1 - slot).wait()
        issue_rows(1 - slot)

        @pl.when(i + 2 < n_tiles)
        def _():
            idx_copy(i + 2, slot).start()

    for k in range(TOP_K):
        pltpu.make_async_copy(yb_hbm.at[pl.ds(0, tile * ROW_TILES), :], rows.at[slot, k],
                              sem_rows.at[slot]).wait()

    route = route_ref[...]
    y = jnp.zeros((tile, D_MODEL), F32)
    for k in range(TOP_K):
        y = y + route[:, TOP_K + k:TOP_K + k + 1] * _load_tile_rows(rows.at[slot, k], tile)
    h = h1_ref[...] + g2_ref[...] * y
    ms = jnp.mean(h * h, axis=-1, keepdims=True)
    o_ref[...] = h * lax.rsqrt(ms + EPS) * nw_ref[...]


def _combine(dest_tiles, yb, route, h1, g2, nw, *, tile, tiles_per_batch):
    T = h1.shape[0]
    return pl.pallas_call(
        functools.partial(_combine_kernel, tile=tile, n_tiles=T // tile),
        out_shape=jax.ShapeDtypeStruct((T, D_MODEL), F32),
        grid=(T // tile,),
        in_specs=[pl.BlockSpec(memory_space=pl.ANY),
                  pl.BlockSpec(memory_space=pl.ANY),
                  pl.BlockSpec((tile, LANES), lambda i: (i, 0)),
                  pl.BlockSpec((tile, D_MODEL), lambda i: (i, 0)),
                  pl.BlockSpec((None, 1, D_MODEL), lambda i: (i // tiles_per_batch, 0, 0)),
                  pl.BlockSpec((1, D_MODEL), lambda i: (0, 0))],
        out_specs=pl.BlockSpec((tile, D_MODEL), lambda i: (i, 0)),
        scratch_shapes=[pltpu.SMEM((2 * tile * TOP_K,), I32),
                        pltpu.VMEM((2, TOP_K, tile * ROW_TILES, LANES), F32),
                        pltpu.SemaphoreType.DMA((2,)),
                        pltpu.SemaphoreType.DMA((2,))],
        compiler_params=_cparams(("arbitrary",)),
        name="combine_norm",
    )(dest_tiles, yb, route, h1, g2, nw)


def _rope_tables(n):
    rows = n // GRID_W
    row = jnp.repeat(jnp.arange(rows, dtype=F32), GRID_W)
    col = jnp.tile(jnp.arange(GRID_W, dtype=F32), rows)
    n_freq = RET_DK // 4
    inv = ROPE_BASE ** (-jnp.arange(n_freq, dtype=F32) / n_freq)
    ang = jnp.concatenate([row[:, None] * inv, col[:, None] * inv], axis=-1)
    cos, sin = jnp.cos(ang), jnp.sin(ang)
    cos_h = jnp.concatenate([cos, cos], axis=-1)
    sin_h = jnp.concatenate([-sin, sin], axis=-1)
    return jnp.tile(cos_h, (1, RET_HEADS)), jnp.tile(sin_h, (1, RET_HEADS))


def kernel(x, c, ctx, c_ctx, w_ada, b_ada, norm_mix_w, norm_ffn_w, w_in, w_out, hg_lb,
           hg_norm_w, router_w, router_b, w1, b1, w2, b2, norm_final_w):
    B, N, D = x.shape
    C = ctx.shape[1]
    T = B * N
    assert D == D_MODEL and w_ada.shape[0] == 1

    cc = jnp.concatenate([c.astype(F32), c_ctx.astype(F32)[None, :],
                          jnp.zeros((16 - B - 1, D), F32)], axis=0)
    mod = _ada(cc, w_ada[0], b_ada[0][None, :])
    mod = mod.reshape(16, 6, 1, D).transpose(1, 0, 2, 3)
    sh1, sc1, g1, sh2, sc2, g2 = (mod[i] for i in range(6))

    w_in_bf = w_in[0].astype(BF16)
    w_out_bf = w_out[0].astype(BF16)
    nw_mix = norm_mix_w[0][None, :]
    cos_t, sin_t = _rope_tables(N)

    pb, lf = _inproj(x, sc1, sh1, None, nw_mix, w_in_bf, hg_lb[:2], cos_t, sin_t,
                     rope=True, tm=512)
    pbc, lfc = _inproj(ctx, sc1, sh1, B, nw_mix, w_in_bf, hg_lb[:2], cos_t[:C], sin_t[:C],
                       rope=False, tm=C)

    a_ret = _retention(pb, pbc)
    a_hg = _hgrn(pb, lf, pbc, lfc, hg_norm_w[0][None, :])

    rw = jnp.zeros((D, LANES), F32).at[:, :N_EXPERTS].set(router_w[0])
    rb = jnp.full((1, LANES), -1e30, F32).at[0, :N_EXPERTS].set(router_b[0])
    h1, v, route, route_t, cnt = _outproj(a_ret, a_hg, w_out_bf, x, g1, sc2, sh2,
                                          norm_ffn_w[0][None, :], rw, rb, tm=512)

    idx = route_t[0:TOP_K].astype(I32)
    rank = route_t[2 * TOP_K:3 * TOP_K].astype(I32)
    counts = cnt[0, :N_EXPERTS].astype(I32)
    padded = (counts + ROW_BLOCK - 1) // ROW_BLOCK * ROW_BLOCK
    pend = jnp.cumsum(padded)
    pstart = pend - padded
    onehot = idx[None] == jnp.arange(N_EXPERTS, dtype=I32)[:, None, None]
    row_start = jnp.sum(jnp.where(onehot, pstart[:, None, None], 0), axis=0)
    dest = (row_start + rank) * ROW_TILES

    def dest_tiles(tile):
        return dest.reshape(TOP_K, T // tile, tile).transpose(1, 0, 2).reshape(T // tile,
                                                                               TOP_K * tile)
    n_blocks = (T * TOP_K) // ROW_BLOCK + N_EXPERTS
    n_used = (pend[-1] // ROW_BLOCK).astype(I32)
    starts = jnp.arange(n_blocks, dtype=I32) * ROW_BLOCK
    block_e = jnp.minimum(jnp.sum((pend[None, :] <= starts[:, None]).astype(I32), axis=1),
                          N_EXPERTS - 1)
    meta = jnp.concatenate([pend.astype(I32), counts, n_used[None]])

    tile_d = 512
    xs = _dispatch(meta, dest_tiles(tile_d), v, tile=tile_d, n_blocks=n_blocks)
    yb = _moe(block_e, n_used[None], xs, w1[0], b1[0][:, None, :], w2[0], b2[0][:, None, :],
              n_blocks=n_blocks)
    tile_c = 256
    out = _combine(dest_tiles(tile_c), yb, route, h1, g2,
                   norm_final_w[None, :], tile=tile_c, tiles_per_batch=N // tile_c)
    return out.reshape(B, N, D)
```

```python
import functools
import math

import jax
import jax.numpy as jnp
import numpy as np
from jax import lax
from jax.experimental import pallas as pl
from jax.experimental.pallas import tpu as pltpu

F32 = jnp.float32
BF16 = jnp.bfloat16
I32 = jnp.int32

D_MODEL = 1024
GRID_W = 64
RET_HEADS = 4
RET_DK = 64
HG_HEADS = 4
PROJ_W = 4096
ROPE_BASE = 10000.0
EPS = 1e-6
N_EXPERTS = 32
TOP_K = 4
D_FF = 1024
SWIGLU_LIMIT = 7.0
SWIGLU_ALPHA = 1.702

LANES = 128
CHUNK = 128
ROW_BLOCK = 1024
SUB_ROWS = 256
VMEM_LIMIT = 56 * 1024 * 1024
ROW_TILES = D_MODEL // LANES

C_RQ, C_RK, C_RV, C_RG, C_HQ, C_FF, C_FB, C_HV, C_HG = (
    0, 256, 512, 1024, 1536, 2048, 2560, 3072, 3584)


def _cparams(sem):
    return pltpu.CompilerParams(dimension_semantics=sem, vmem_limit_bytes=VMEM_LIMIT)


def _split_bf16(x):
    hi = x.astype(BF16)
    lo = (x - hi.astype(F32)).astype(BF16)
    return hi, lo


def _dot(a, b):
    return jnp.dot(a, b, preferred_element_type=F32)


def _dot_nt(a, b):
    return lax.dot_general(a, b, (((1,), (1,)), ((), ())), preferred_element_type=F32)


def _dot3(a, b):
    ah, al = _split_bf16(a)
    bh, bl = _split_bf16(b)
    return _dot(ah, bh) + (_dot(ah, bl) + _dot(al, bh))


def _silu(x):
    return x * jax.nn.sigmoid(x)


def _load_tile_rows(ref, n):
    return jnp.concatenate([ref[pl.ds(s, n, stride=ROW_TILES), :] for s in range(ROW_TILES)],
                           axis=1)


def _store_tile_rows(ref, x):
    n = x.shape[0]
    for s in range(ROW_TILES):
        ref[pl.ds(s, n, stride=ROW_TILES), :] = x[:, s * LANES:(s + 1) * LANES]


def _ada_kernel(c_ref, w_ref, b_ref, o_ref):
    s = _silu(c_ref[...])
    o_ref[...] = _dot3(s, w_ref[...]) + b_ref[...]


def _ada(cc, w, b):
    nblk = w.shape[1] // D_MODEL
    return pl.pallas_call(
        _ada_kernel,
        out_shape=jax.ShapeDtypeStruct((cc.shape[0], w.shape[1]), F32),
        grid=(nblk,),
        in_specs=[pl.BlockSpec(cc.shape, lambda j: (0, 0)),
                  pl.BlockSpec((D_MODEL, D_MODEL), lambda j: (0, j)),
                  pl.BlockSpec((1, D_MODEL), lambda j: (0, j))],
        out_specs=pl.BlockSpec((cc.shape[0], D_MODEL), lambda j: (0, j)),
        compiler_params=_cparams(("arbitrary",)),
        name="ada",
    )(cc, w, b)


def _inproj_kernel(x_ref, sc_ref, sh_ref, nw_ref, w_ref, lb_ref, cos_ref, sin_ref,
                   pb_ref, lf_ref, u_scr, *, rope):
    x = x_ref[...]
    ms = jnp.mean(x * x, axis=-1, keepdims=True)
    u = x * lax.rsqrt(ms + EPS) * nw_ref[...] * (1.0 + sc_ref[...]) + sh_ref[...]
    u_scr[...] = u.astype(BF16)

    def proj(lo, width):
        return _dot(u_scr[...], w_ref[:, lo:lo + width])

    tm = x.shape[0]
    if rope:
        lane = lax.broadcasted_iota(I32, (tm, LANES), 1)
        first = (lane & 32) == 0

    def put_rot(col, scale):
        for j in range(2):
            lo = col + j * LANES
            t = proj(lo, LANES)
            if scale != 1.0:
                t = t * scale
            if rope:
                tb = j * LANES
                rot = jnp.where(first, pltpu.roll(t, 96, axis=1), pltpu.roll(t, 32, axis=1))
                t = t * cos_ref[:, tb:tb + LANES] + rot * sin_ref[:, tb:tb + LANES]
            pb_ref[:, lo:lo + LANES] = t.astype(BF16)

    put_rot(C_RQ, 1.0)
    put_rot(C_RK, RET_DK ** -0.5)
    pb_ref[:, C_RV:C_RV + 512] = proj(C_RV, 512).astype(BF16)
    pb_ref[:, C_RG:C_RG + 512] = _silu(proj(C_RG, 512)).astype(BF16)
    pb_ref[:, C_HQ:C_HQ + 512] = _silu(proj(C_HQ, 512)).astype(BF16)
    pb_ref[:, C_HV:C_HV + 512] = proj(C_HV, 512).astype(BF16)
    pb_ref[:, C_HG:C_HG + 512] = _silu(proj(C_HG, 512)).astype(BF16)

    la = lb_ref[0]
    lbb = lb_ref[1]
    mx = jnp.maximum(la, lbb)
    ea = jnp.exp(la - mx)
    eb = jnp.exp(lbb - mx)
    lb = ea / (ea + eb)
    for d, col in enumerate((C_FF, C_FB)):
        lbd = lb[d:d + 1, :]
        f = lbd + (1.0 - lbd) * jax.nn.sigmoid(proj(col, 512))
        pb_ref[:, col:col + 512] = (1.0 - f).astype(BF16)
        lf_ref[:, d * 512:(d + 1) * 512] = jnp.log(f)


def _inproj(x, sc, sh, mod_row, nw, w_bf, hg_lb, cos_t, sin_t, *, rope, tm):
    B, n, _ = x.shape
    nt = n // tm
    if mod_row is None:
        mrow = lambda b, j: (b, 0, 0)
    else:
        mrow = lambda b, j: (mod_row, 0, 0)
    return pl.pallas_call(
        functools.partial(_inproj_kernel, rope=rope),
        out_shape=(jax.ShapeDtypeStruct((B, n, PROJ_W), BF16),
                   jax.ShapeDtypeStruct((B, n, 1024), F32)),
        grid=(B, nt),
        in_specs=[pl.BlockSpec((None, tm, D_MODEL), lambda b, j: (b, j, 0)),
                  pl.BlockSpec((None, 1, D_MODEL), mrow),
                  pl.BlockSpec((None, 1, D_MODEL), mrow),
                  pl.BlockSpec((1, D_MODEL), lambda b, j: (0, 0)),
                  pl.BlockSpec((D_MODEL, PROJ_W), lambda b, j: (0, 0)),
                  pl.BlockSpec((2, 2, 512), lambda b, j: (0, 0, 0)),
                  pl.BlockSpec((tm, 256), lambda b, j: (j, 0)),
                  pl.BlockSpec((tm, 256), lambda b, j: (j, 0))],
        out_specs=(pl.BlockSpec((None, tm, PROJ_W), lambda b, j: (b, j, 0)),
                   pl.BlockSpec((None, tm, 1024), lambda b, j: (b, j, 0))),
        scratch_shapes=[pltpu.VMEM((tm, D_MODEL), BF16)],
        compiler_params=_cparams(("arbitrary", "arbitrary")),
        name="inproj_rope" if rope else "inproj_ctx",
    )(x, sc, sh, nw, w_bf, hg_lb, cos_t, sin_t)


_RET_LGF = [math.log1p(-(2.0 ** (-5.0 - 2.0 * h))) for h in range(RET_HEADS)]
_RET_LGB = [math.log1p(-(2.0 ** (-6.0 - 2.0 * h))) for h in range(RET_HEADS)]


def _ret_kernel(q_ref, k_ref, v_ref, g_ref, kc_ref, vc_ref, o_ref,
                u_scr, sin_scr, kt_scr, dtot_scr, *, n_lat, n_ctx):
    L = CHUNK
    pair = pl.program_id(1)
    row = lax.broadcasted_iota(I32, (L, L), 0).astype(F32)
    col = lax.broadcasted_iota(I32, (L, L), 1).astype(F32)
    lane = lax.broadcasted_iota(I32, (L, LANES), 1)
    trow = lax.broadcasted_iota(I32, (L, 1), 0).astype(F32)
    tcol = lax.broadcasted_iota(I32, (1, L), 1).astype(F32)
    low_half = lane < RET_DK

    def u_chunk(k_blk, v_blk, ci, store_kt):
        kt = k_blk.astype(F32).T
        if store_kt is not None:
            kt_scr[store_kt] = kt.astype(BF16)
        for hh in range(2):
            lgf = jnp.where(pair == 0, _RET_LGF[hh], _RET_LGF[2 + hh])
            lgb = jnp.where(pair == 0, _RET_LGB[hh], _RET_LGB[2 + hh])
            kth = kt[hh * RET_DK:(hh + 1) * RET_DK, :]
            wkf = jnp.exp(lgf * (L - 1.0 - tcol))
            wkb = jnp.exp(lgb * tcol)
            lhs = jnp.concatenate([kth * wkf, kth * wkb], axis=0).astype(BF16)
            u_scr[hh, ci] = _dot(lhs, v_blk[:, hh * LANES:(hh + 1) * LANES])

    for c in range(n_ctx):
        u_chunk(kc_ref[c * L:(c + 1) * L, :], vc_ref[c * L:(c + 1) * L, :], c, None)

    def lat_u(c, carry):
        r0 = pl.multiple_of(c * L, L)
        u_chunk(k_ref[pl.ds(r0, L), :], v_ref[pl.ds(r0, L), :], n_ctx + c, c)
        return carry
    lax.fori_loop(0, n_lat, lat_u, 0, unroll=2)

    ones = jnp.ones((RET_DK, LANES), F32)
    for hh in range(2):
        lgf = jnp.where(pair == 0, _RET_LGF[hh], _RET_LGF[2 + hh])
        lgb = jnp.where(pair == 0, _RET_LGB[hh], _RET_LGB[2 + hh])
        d = row - col
        dtot_scr[hh] = jnp.where(d > 0, jnp.exp(lgf * jnp.maximum(d, 0.0)),
                                 jnp.where(d < 0, jnp.exp(lgb * jnp.maximum(-d, 0.0)), 2.0))
        af = jnp.exp(ones * (lgf * L))
        ab = jnp.exp(ones * (lgb * L))

        s = jnp.zeros((RET_DK, LANES), F32)
        for c in range(n_ctx):
            s = af * s + u_scr[hh, c, 0:RET_DK, :]
        sb = jnp.zeros((RET_DK, LANES), F32)
        for c in reversed(range(n_ctx)):
            sb = ab * sb + u_scr[hh, c, RET_DK:2 * RET_DK, :]

        def fwd(c, s, hh=hh, af=af):
            sin_scr[hh, c, 0:RET_DK, :] = s.astype(BF16)
            return af * s + u_scr[hh, n_ctx + c, 0:RET_DK, :]
        lax.fori_loop(0, n_lat, fwd, s)

        def bwd(i, sb, hh=hh, ab=ab):
            c = n_lat - 1 - i
            sin_scr[hh, c, RET_DK:2 * RET_DK, :] = sb.astype(BF16)
            return ab * sb + u_scr[hh, n_ctx + c, RET_DK:2 * RET_DK, :]
        lax.fori_loop(0, n_lat, bwd, sb)

    def out_chunk(c, carry):
        r0 = pl.multiple_of(c * L, L)
        q = q_ref[pl.ds(r0, L), :].astype(F32)
        qr = pltpu.roll(q, RET_DK, axis=1)
        kt = kt_scr[c]
        for hh in range(2):
            lgf = jnp.where(pair == 0, _RET_LGF[hh], _RET_LGF[2 + hh])
            lgb = jnp.where(pair == 0, _RET_LGB[hh], _RET_LGB[2 + hh])
            mine = low_half if hh == 0 else jnp.logical_not(low_half)
            qm = jnp.where(mine, q, 0.0).astype(BF16)
            p = (_dot(qm, kt) * dtot_scr[hh]).astype(BF16)
            vh = v_ref[pl.ds(r0, L), hh * LANES:(hh + 1) * LANES]
            wqf = jnp.exp(lgf * (trow + 1.0))
            wqb = jnp.exp(lgb * (L - trow))
            qa, qb = (q, qr) if hh == 0 else (qr, q)
            qs = jnp.where(low_half, qa * wqf, qb * wqb).astype(BF16)
            o = _dot(p, vh) + _dot(qs, sin_scr[hh, c])
            ms = jnp.mean(o * o, axis=-1, keepdims=True)
            gh = g_ref[pl.ds(r0, L), hh * LANES:(hh + 1) * LANES].astype(F32)
            o_ref[pl.ds(r0, L), hh * LANES:(hh + 1) * LANES] = (
                o * lax.rsqrt(ms + EPS) * gh).astype(BF16)
        return carry
    lax.fori_loop(0, n_lat, out_chunk, 0, unroll=2)


def _retention(pb, pbc):
    B, n, _ = pb.shape
    nc = pbc.shape[1]
    n_lat, n_ctx = n // CHUNK, nc // CHUNK
    return pl.pallas_call(
        functools.partial(_ret_kernel, n_lat=n_lat, n_ctx=n_ctx),
        out_shape=jax.ShapeDtypeStruct((B, n, 512), BF16),
        grid=(B, 2),
        in_specs=[pl.BlockSpec((None, n, LANES), lambda b, p: (b, 0, C_RQ // LANES + p)),
                  pl.BlockSpec((None, n, LANES), lambda b, p: (b, 0, C_RK // LANES + p)),
                  pl.BlockSpec((None, n, 256), lambda b, p: (b, 0, C_RV // 256 + p)),
                  pl.BlockSpec((None, n, 256), lambda b, p: (b, 0, C_RG // 256 + p)),
                  pl.BlockSpec((None, nc, LANES), lambda b, p: (b, 0, C_RK // LANES + p)),
                  pl.BlockSpec((None, nc, 256), lambda b, p: (b, 0, C_RV // 256 + p))],
        out_specs=pl.BlockSpec((None, n, 256), lambda b, p: (b, 0, p)),
        scratch_shapes=[pltpu.VMEM((2, n_lat + n_ctx, CHUNK, LANES), F32),
                        pltpu.VMEM((2, n_lat, CHUNK, LANES), BF16),
                        pltpu.VMEM((n_lat, LANES, CHUNK), BF16),
                        pltpu.VMEM((2, CHUNK, CHUNK), F32)],
        compiler_params=_cparams(("arbitrary", "arbitrary")),
        name="retention",
    )(pb, pb, pb, pb, pbc, pbc)


_LEVELS = (64, 32, 16, 8, 4, 2, 1)


def _expand_rows(r, rep):
    n = r.shape[0]
    if n == 1:
        return jnp.broadcast_to(r, (rep, r.shape[1]))
    return jnp.concatenate(
        [jnp.broadcast_to(r[i:i + 1, :], (rep, r.shape[1])) for i in range(n)], axis=0)


def _hgrn_kernel(q_ref, kf_ref, kb_ref, v_ref, g_ref, lff_ref, lfb_ref,
                 kfc_ref, kbc_ref, vc_ref, lffc_ref, lfbc_ref, nw_ref, o_ref,
                 ut_scr, a_scr, qs_scr, oi_scr, sin_scr, bfb_scr, *, n_lat, n_ctx):
    L = CHUNK
    row = lax.broadcasted_iota(I32, (L, L), 0)
    col = lax.broadcasted_iota(I32, (L, L), 1)
    xr_bits = lax.bitcast_convert_type((row ^ col).astype(F32), I32)
    lv = lax.shift_right_logical(xr_bits, 23) - 127
    row2 = lax.broadcasted_iota(I32, (L, 2 * L), 0)
    col2 = lax.broadcasted_iota(I32, (L, 2 * L), 1) & (L - 1)
    tril2 = jnp.where(col2 <= row2, 1.0, 0.0).astype(BF16)
    triu2 = jnp.where(col2 >= row2, 1.0, 0.0).astype(BF16)

    def cums(lff, lfb):
        hf, lof = _split_bf16(lff)
        hb, lob = _split_bf16(lfb)
        bf = _dot(tril2, jnp.concatenate([hf, lof], axis=0))
        bb = _dot(triu2, jnp.concatenate([hb, lob], axis=0))
        return bf, bb

    def state_part(ci, kf, kb, v_blk, bf, bb):
        endf = bf[L - 1:L, :]
        endb = bb[0:1, :]
        ksf = kf * jnp.exp(endf - bf)
        ksb = kb * jnp.exp(endb - bb)
        vt = v_blk.astype(F32).T.astype(BF16)
        ut_scr[ci] = _dot(vt, jnp.concatenate([ksf, ksb], axis=1).astype(BF16))
        a_scr[ci] = jnp.broadcast_to(
            jnp.concatenate([jnp.exp(endf), jnp.exp(endb)], axis=1), (8, 2 * LANES))

    for c in range(n_ctx):
        sl = slice(c * L, (c + 1) * L)
        bf, bb = cums(lffc_ref[sl, :], lfbc_ref[sl, :])
        state_part(c, kfc_ref[sl, :].astype(F32), kbc_ref[sl, :].astype(F32),
                   vc_ref[sl, :], bf, bb)

    def lat_chunk(c, u):
        rows = pl.ds(pl.multiple_of(c * L, L), L)
        bf_scr = bfb_scr.at[u, 0]
        bb_scr = bfb_scr.at[u, 1]
        q = q_ref[rows, :].astype(F32)
        kf = kf_ref[rows, :].astype(F32)
        kb = kb_ref[rows, :].astype(F32)
        v_blk = v_ref[rows, :]
        lff = lff_ref[rows, :]
        lfb = lfb_ref[rows, :]
        bf, bb = cums(lff, lfb)
        state_part(n_ctx + c, kf, kb, v_blk, bf, bb)
        qs_scr[rows, :] = jnp.concatenate([q * jnp.exp(bf), q * jnp.exp(bb)],
                                          axis=1).astype(BF16)
        bf_scr[...] = bf
        bb_scr[...] = bb

        acc = jnp.zeros((L, L), F32)
        for lvl, h in enumerate(_LEVELS):
            bit = (row & h) != 0
            ksel = jnp.where(bit, kb, kf)
            if h >= 4:
                n = (L // 2) // h
                if n == 1:
                    rf = bf_scr[h - 1:h, :]
                    rb = bb_scr[h:h + 1, :]
                else:
                    rf = bf_scr[pl.ds(h - 1, n, stride=2 * h), :]
                    rb = bb_scr[pl.ds(h, n, stride=2 * h), :]
                df = bf - _expand_rows(rf, 2 * h)
                db = bb - _expand_rows(rb, 2 * h)
                eq = jnp.where(bit, df, db)
                ek = -jnp.where(bit, db, df)
            elif h == 2:
                m = row & 3
                lff_n = pltpu.roll(lff, L - 1, axis=0)
                lfb_n = pltpu.roll(lfb, L - 1, axis=0)
                eq = jnp.where(m == 2, lff,
                               jnp.where(m == 3, lff + pltpu.roll(lff, 1, axis=0),
                                         jnp.where(m == 0, lfb + lfb_n, lfb)))
                ek = jnp.where(m == 3, pltpu.roll(lfb, 1, axis=0),
                               jnp.where(m == 0, lff_n, 0.0))
            else:
                eq = jnp.where(bit, lff, lfb)
                ek = None
            lhs = (q * jnp.exp(eq)).astype(BF16)
            rhs = (ksel if ek is None else ksel * jnp.exp(ek)).astype(BF16)
            acc = jnp.where(lv == 6 - lvl, _dot_nt(lhs, rhs), acc)

        dsum = jnp.sum(q * (kf + kb), axis=-1, keepdims=True)
        oi_scr[rows, :] = _dot(acc.astype(BF16), v_blk) + dsum * v_blk.astype(F32)

    def lat_pair(i, carry):
        for u in range(2):
            lat_chunk(2 * i + u, u)
        return carry
    lax.fori_loop(0, n_lat // 2, lat_pair, 0)

    st = jnp.zeros((LANES, LANES), F32)
    for c in range(n_ctx):
        st = st * a_scr[c, 0:1, 0:LANES] + ut_scr[c, :, 0:LANES]
    stb = jnp.zeros((LANES, LANES), F32)
    for c in reversed(range(n_ctx)):
        stb = stb * a_scr[c, 0:1, LANES:2 * LANES] + ut_scr[c, :, LANES:2 * LANES]

    def fwd(c, st):
        sin_scr[c, :, 0:LANES] = st.astype(BF16)
        ci = n_ctx + c
        return st * a_scr[ci, 0:1, 0:LANES] + ut_scr[ci, :, 0:LANES]
    lax.fori_loop(0, n_lat, fwd, st)

    def bwd(i, stb):
        c = n_lat - 1 - i
        sin_scr[c, :, LANES:2 * LANES] = stb.astype(BF16)
        ci = n_ctx + c
        return stb * a_scr[ci, 0:1, LANES:2 * LANES] + ut_scr[ci, :, LANES:2 * LANES]
    lax.fori_loop(0, n_lat, bwd, stb)

    def out_chunk(c, carry):
        rows = pl.ds(pl.multiple_of(c * L, L), L)
        o = oi_scr[rows, :] + _dot_nt(qs_scr[rows, :], sin_scr[c])
        ms = jnp.mean(o * o, axis=-1, keepdims=True)
        y = o * lax.rsqrt(ms + EPS) * nw_ref[...] * g_ref[rows, :].astype(F32)
        o_ref[rows, :] = y.astype(BF16)
        return carry
    lax.fori_loop(0, n_lat, out_chunk, 0, unroll=4)


def _hgrn(pb, lf, pbc, lfc, nw):
    B, n, _ = pb.shape
    nc = pbc.shape[1]
    n_lat, n_ctx = n // CHUNK, nc // CHUNK

    def colblk(rows, col0):
        return pl.BlockSpec((None, rows, LANES), lambda b, h: (b, 0, col0 // LANES + h))

    return pl.pallas_call(
        functools.partial(_hgrn_kernel, n_lat=n_lat, n_ctx=n_ctx),
        out_shape=jax.ShapeDtypeStruct((B, n, 512), BF16),
        grid=(B, HG_HEADS),
        in_specs=[colblk(n, C_HQ), colblk(n, C_FF), colblk(n, C_FB), colblk(n, C_HV),
                  colblk(n, C_HG), colblk(n, 0), colblk(n, 512),
                  colblk(nc, C_FF), colblk(nc, C_FB), colblk(nc, C_HV),
                  colblk(nc, 0), colblk(nc, 512),
                  pl.BlockSpec((1, LANES), lambda b, h: (0, 0))],
        out_specs=pl.BlockSpec((None, n, LANES), lambda b, h: (b, 0, h)),
        scratch_shapes=[pltpu.VMEM((n_lat + n_ctx, LANES, 2 * LANES), F32),
                        pltpu.VMEM((n_lat + n_ctx, 8, 2 * LANES), F32),
                        pltpu.VMEM((n, 2 * LANES), BF16),
                        pltpu.VMEM((n, LANES), F32),
                        pltpu.VMEM((n_lat, LANES, 2 * LANES), BF16),
                        pltpu.VMEM((2, 2, CHUNK, LANES), F32)],
        compiler_params=_cparams(("arbitrary", "arbitrary")),
        name="hgrn2",
    )(pb, pb, pb, pb, pb, lf, lf, pbc, pbc, pbc, lfc, lfc, nw)


def _outproj_kernel(ar_ref, ah_ref, w_ref, x_ref, g1_ref, sc_ref, sh_ref, nw_ref,
                    rw_ref, rb_ref, h1_ref, v_ref, route_ref, route_t_ref, cnt_ref, cnt_scr):
    first_step = jnp.logical_and(pl.program_id(0) == 0, pl.program_id(1) == 0)

    @pl.when(first_step)
    def _():
        cnt_scr[...] = jnp.zeros_like(cnt_scr)

    y = _dot(ar_ref[...], w_ref[0:512, :]) + _dot(ah_ref[...], w_ref[512:1024, :])
    h1 = x_ref[...] + g1_ref[...] * y
    h1_ref[...] = h1
    ms = jnp.mean(h1 * h1, axis=-1, keepdims=True)
    v = h1 * lax.rsqrt(ms + EPS) * nw_ref[...] * (1.0 + sc_ref[...]) + sh_ref[...]
    _store_tile_rows(v_ref, v)

    tm = v.shape[0]
    logits = _dot3(v, rw_ref[...]) + rb_ref[...]
    lane = lax.broadcasted_iota(I32, (tm, LANES), 1)
    lane_f = lane.astype(F32)
    l = logits
    sels, tops, idxs = [], [], []
    for _ in range(TOP_K):
        m = jnp.max(l, axis=1, keepdims=True)
        i = jnp.min(jnp.where(l == m, lane_f, float(LANES)), axis=1, keepdims=True)
        sel = lane_f == i
        l = jnp.where(sel, -jnp.inf, l)
        sels.append(sel)
        tops.append(m)
        idxs.append(i)
    es = [jnp.exp(t - tops[0]) for t in tops]
    den = es[0] + es[1] + es[2] + es[3]
    gates = [e / den for e in es]

    oh = jnp.zeros((tm, LANES), F32)
    for sel in sels:
        oh = jnp.where(sel, 1.0, oh)
    r = lax.broadcasted_iota(I32, (tm, tm), 0)
    c = lax.broadcasted_iota(I32, (tm, tm), 1)
    tri = jnp.where(c < r, 1.0, 0.0).astype(BF16)
    before = _dot(tri, oh.astype(BF16)) + cnt_scr[0:1, :]
    ranks = [jnp.sum(jnp.where(sel, before, 0.0), axis=1, keepdims=True) for sel in sels]
    cnt_scr[...] = cnt_scr[...] + jnp.sum(oh, axis=0, keepdims=True)
    cnt_ref[...] = cnt_scr[...]

    out = jnp.zeros((tm, LANES), F32)
    for k in range(TOP_K):
        out = jnp.where(lane == k, idxs[k], out)
        out = jnp.where(lane == TOP_K + k, gates[k], out)
        out = jnp.where(lane == 2 * TOP_K + k, ranks[k], out)
    route_ref[...] = out
    route_t_ref[...] = out.T[0:16, :]


def _outproj(a_ret, a_hg, w_bf, x, g1, sc2, sh2, nw, rw, rb, *, tm):
    B, n, _ = x.shape
    nt = n // tm
    T = B * n
    mrow = lambda b, j: (b, 0, 0)
    tok = lambda b, j: (b * nt + j, 0)
    return pl.pallas_call(
        _outproj_kernel,
        out_shape=(jax.ShapeDtypeStruct((T, D_MODEL), F32),
                   jax.ShapeDtypeStruct((T * ROW_TILES, LANES), F32),
                   jax.ShapeDtypeStruct((T, LANES), F32),
                   jax.ShapeDtypeStruct((16, T), F32),
                   jax.ShapeDtypeStruct((8, LANES), F32)),
        grid=(B, nt),
        in_specs=[pl.BlockSpec((None, tm, 512), lambda b, j: (b, j, 0)),
                  pl.BlockSpec((None, tm, 512), lambda b, j: (b, j, 0)),
                  pl.BlockSpec((D_MODEL, D_MODEL), lambda b, j: (0, 0)),
                  pl.BlockSpec((None, tm, D_MODEL), lambda b, j: (b, j, 0)),
                  pl.BlockSpec((None, 1, D_MODEL), mrow),
                  pl.BlockSpec((None, 1, D_MODEL), mrow),
                  pl.BlockSpec((None, 1, D_MODEL), mrow),
                  pl.BlockSpec((1, D_MODEL), lambda b, j: (0, 0)),
                  pl.BlockSpec((D_MODEL, LANES), lambda b, j: (0, 0)),
                  pl.BlockSpec((1, LANES), lambda b, j: (0, 0))],
        out_specs=(pl.BlockSpec((tm, D_MODEL), tok),
                   pl.BlockSpec((tm * ROW_TILES, LANES), tok),
                   pl.BlockSpec((tm, LANES), tok),
                   pl.BlockSpec((16, tm), lambda b, j: (0, b * nt + j)),
                   pl.BlockSpec((8, LANES), lambda b, j: (0, 0))),
        scratch_shapes=[pltpu.VMEM((8, LANES), F32)],
        compiler_params=_cparams(("arbitrary", "arbitrary")),
        name="outproj_router",
    )(a_ret, a_hg, w_bf, x, g1, sc2, sh2, nw, rw, rb)


def _dispatch_kernel(meta_ref, dest_hbm, v_ref, xs_hbm, idx_smem, zbuf, sem_idx, sem_z, sem_rows,
                     *, tile, n_blocks, n_tiles):
    i = pl.program_id(0)
    slot = lax.rem(i, 2)

    def idx_copy(j, s):
        n_asg = tile * TOP_K
        return pltpu.make_async_copy(dest_hbm.at[j], idx_smem.at[pl.ds(s * n_asg, n_asg)],
                                     sem_idx.at[s])

    @pl.when(i == 0)
    def _():
        idx_copy(0, 0).start()

    def zero_copy(r0):
        return pltpu.make_async_copy(
            zbuf, xs_hbm.at[pl.ds(r0 * ROW_TILES, ROW_BLOCK * ROW_TILES), :], sem_z)

    @pl.when(i == 0)
    def _():
        zbuf[...] = jnp.zeros_like(zbuf)
        n_used = meta_ref[2 * N_EXPERTS]

        def tail_start(e):
            return pl.multiple_of(meta_ref[e] - ROW_BLOCK, ROW_BLOCK)

        def has_tail(e):
            return (meta_ref[N_EXPERTS + e] & (ROW_BLOCK - 1)) != 0

        def z_issue(e, carry):
            @pl.when(has_tail(e))
            def _():
                zero_copy(tail_start(e)).start()
            return carry
        lax.fori_loop(0, N_EXPERTS, z_issue, 0)

        def u_issue(blk, carry):
            @pl.when(blk >= n_used)
            def _():
                zero_copy(pl.multiple_of(blk * ROW_BLOCK, ROW_BLOCK)).start()
            return carry
        lax.fori_loop(0, n_blocks, u_issue, 0)

        def z_wait(e, carry):
            @pl.when(has_tail(e))
            def _():
                zero_copy(0).wait()
            return carry
        lax.fori_loop(0, N_EXPERTS, z_wait, 0)

        def u_wait(blk, carry):
            @pl.when(blk >= n_used)
            def _():
                zero_copy(0).wait()
            return carry
        lax.fori_loop(0, n_blocks, u_wait, 0)

    idx_copy(i, slot).wait()

    @pl.when(i + 1 < n_tiles)
    def _():
        idx_copy(i + 1, 1 - slot).start()

    def issue(t, carry):
        for k in range(TOP_K):
            d = pl.multiple_of(idx_smem[slot * (tile * TOP_K) + k * tile + t], ROW_TILES)
            src = v_ref.at[pl.ds(pl.multiple_of(t * ROW_TILES, ROW_TILES), ROW_TILES), :]
            pltpu.make_async_copy(src, xs_hbm.at[pl.ds(d, ROW_TILES), :],
                                  sem_rows).start(priority=k % 2)
        return carry
    lax.fori_loop(0, tile, issue, 0, unroll=2)
    for _ in range(TOP_K):
        pltpu.make_async_copy(v_ref, xs_hbm.at[pl.ds(0, tile * ROW_TILES), :], sem_rows).wait()


def _dispatch(meta, dest_tiles, v, *, tile, n_blocks):
    T = v.shape[0] // ROW_TILES
    R = n_blocks * ROW_BLOCK
    return pl.pallas_call(
        functools.partial(_dispatch_kernel, tile=tile, n_blocks=n_blocks, n_tiles=T // tile),
        out_shape=jax.ShapeDtypeStruct((R * ROW_TILES, LANES), F32),
        grid_spec=pltpu.PrefetchScalarGridSpec(
            num_scalar_prefetch=1,
            grid=(T // tile,),
            in_specs=[pl.BlockSpec(memory_space=pl.ANY),
                      pl.BlockSpec((tile * ROW_TILES, LANES), lambda i, meta: (i, 0))],
            out_specs=pl.BlockSpec(memory_space=pl.ANY),
            scratch_shapes=[pltpu.SMEM((2 * tile * TOP_K,), I32),
                            pltpu.VMEM((ROW_BLOCK * ROW_TILES, LANES), F32),
                            pltpu.SemaphoreType.DMA((2,)),
                            pltpu.SemaphoreType.DMA,
                            pltpu.SemaphoreType.DMA]),
        compiler_params=_cparams(("arbitrary",)),
        name="dispatch",
    )(meta, dest_tiles, v)


def _moe_kernel(be_ref, ns_ref, nx_ref, nu_ref, x_ref, w1_hbm, b1_ref, w2_hbm, b2_ref, y_ref,
                w1_stage, w2_stage, w1_scr, w2_scr, x_scr, act_scr, sem_w):
    i = pl.program_id(0)
    e = be_ref[i]
    nsub = ns_ref[i]
    changed = jnp.logical_or(i == 0, e != be_ref[jnp.maximum(i - 1, 0)])

    def weight_copies(ex):
        return (pltpu.make_async_copy(w1_hbm.at[ex], w1_stage, sem_w.at[0]),
                pltpu.make_async_copy(w2_hbm.at[ex], w2_stage, sem_w.at[1]))

    @pl.when(i == 0)
    def _():
        for cp in weight_copies(e):
            cp.start()

    @pl.when(jnp.logical_and(changed, nsub > 0))
    def _():
        for cp in weight_copies(e):
            cp.wait()
        w1_scr[...] = w1_stage[...].astype(BF16)
        w2_scr[...] = w2_stage[...].astype(BF16)
        nxt = nx_ref[e]

        @pl.when(nxt >= 0)
        def _():
            for cp in weight_copies(nxt):
                cp.start()

    def compute(rows):
        x_scr[0:rows, :] = _load_tile_rows(x_ref, rows).astype(BF16)
        cw = 256
        for c in range(D_FF // cw):
            glu = (_dot(x_scr[0:rows, :], w1_scr[:, c * cw:(c + 1) * cw])
                   + b1_ref[:, c * cw:(c + 1) * cw])
            lin = (_dot(x_scr[0:rows, :], w1_scr[:, D_FF + c * cw:D_FF + (c + 1) * cw])
                   + b1_ref[:, D_FF + c * cw:D_FF + (c + 1) * cw])
            glu = jnp.minimum(glu, SWIGLU_LIMIT)
            lin = jnp.clip(lin, -SWIGLU_LIMIT, SWIGLU_LIMIT)
            act = glu * jax.nn.sigmoid(SWIGLU_ALPHA * glu) * (lin + 1.0)
            act_scr[0:rows, c * cw:(c + 1) * cw] = act.astype(BF16)
        for c in range(D_MODEL // cw):
            y = (_dot(act_scr[0:rows, :], w2_scr[:, c * cw:(c + 1) * cw])
                 + b2_ref[:, c * cw:(c + 1) * cw])
            for s in range(cw // LANES):
                t = c * (cw // LANES) + s
                y_ref[pl.ds(t, rows, stride=ROW_TILES), :] = y[:, s * LANES:(s + 1) * LANES]
        if rows < ROW_BLOCK:
            y_ref[rows * ROW_TILES:, :] = jnp.zeros(((ROW_BLOCK - rows) * ROW_TILES, LANES), F32)

    for m in range(1, ROW_BLOCK // SUB_ROWS + 1):
        @pl.when(nsub == m)
        def _(m=m):
            compute(m * SUB_ROWS)

    @pl.when(nsub == 0)
    def _():
        y_ref[...] = jnp.zeros_like(y_ref)


def _moe(block_e, nsub, next_e, n_used, xs, w1, b1, w2, b2, *, n_blocks):
    return pl.pallas_call(
        _moe_kernel,
        out_shape=jax.ShapeDtypeStruct(xs.shape, F32),
        grid_spec=pltpu.PrefetchScalarGridSpec(
            num_scalar_prefetch=4,
            grid=(n_blocks,),
            in_specs=[pl.BlockSpec((ROW_BLOCK * ROW_TILES, LANES),
                                   lambda i, be, ns, nx, nu: (jnp.minimum(i, nu[0] - 1), 0)),
                      pl.BlockSpec(memory_space=pl.ANY),
                      pl.BlockSpec((None, 1, 2 * D_FF), lambda i, be, ns, nx, nu: (be[i], 0, 0)),
                      pl.BlockSpec(memory_space=pl.ANY),
                      pl.BlockSpec((None, 1, D_MODEL), lambda i, be, ns, nx, nu: (be[i], 0, 0))],
            out_specs=pl.BlockSpec((ROW_BLOCK * ROW_TILES, LANES),
                                   lambda i, be, ns, nx, nu: (i, 0)),
            scratch_shapes=[pltpu.VMEM((D_MODEL, 2 * D_FF), F32),
                            pltpu.VMEM((D_FF, D_MODEL), F32),
                            pltpu.VMEM((D_MODEL, 2 * D_FF), BF16),
                            pltpu.VMEM((D_FF, D_MODEL), BF16),
                            pltpu.VMEM((ROW_BLOCK, D_MODEL), BF16),
                            pltpu.VMEM((ROW_BLOCK, D_FF), BF16),
                            pltpu.SemaphoreType.DMA((2,))]),
        compiler_params=_cparams(("arbitrary",)),
        name="moe_ffn",
    )(block_e, nsub, next_e, n_used, xs, w1, b1, w2, b2)


def _combine_kernel(dest_hbm, yb_hbm, route_ref, h1_ref, g2_ref, nw_ref, o_ref,
                    idx_smem, rows, sem_idx, sem_rows, *, tile, n_tiles):
    i = pl.program_id(0)
    slot = lax.rem(i, 2)

    def idx_copy(j, s):
        n_asg = tile * TOP_K
        return pltpu.make_async_copy(dest_hbm.at[j], idx_smem.at[pl.ds(s * n_asg, n_asg)],
                                     sem_idx.at[s])

    def issue_rows(s):
        def issue(t, carry):
            for k in range(TOP_K):
                d = pl.multiple_of(idx_smem[s * (tile * TOP_K) + k * tile + t], ROW_TILES)
                dst = rows.at[s, k, pl.ds(pl.multiple_of(t * ROW_TILES, ROW_TILES), ROW_TILES), :]
                pltpu.make_async_copy(yb_hbm.at[pl.ds(d, ROW_TILES), :], dst,
                                      sem_rows.at[s]).start(priority=k % 2)
            return carry
        lax.fori_loop(0, tile, issue, 0, unroll=2)

    @pl.when(i == 0)
    def _():
        idx_copy(0, 0).start()
        idx_copy(0, 0).wait()
        issue_rows(0)

        if n_tiles > 1:
            idx_copy(1, 1).start()

    @pl.when(i + 1 < n_tiles)
    def _():
        idx_copy(i + 1, 1 - slot).wait()
        issue_rows(1 - slot)

        @pl.when(i + 2 < n_tiles)
        def _():
            idx_copy(i + 2, slot).start()

    for k in range(TOP_K):
        pltpu.make_async_copy(yb_hbm.at[pl.ds(0, tile * ROW_TILES), :], rows.at[slot, k],
                              sem_rows.at[slot]).wait()

    route = route_ref[...]
    y = jnp.zeros((tile, D_MODEL), F32)
    for k in range(TOP_K):
        y = y + route[:, TOP_K + k:TOP_K + k + 1] * _load_tile_rows(rows.at[slot, k], tile)
    h = h1_ref[...] + g2_ref[...] * y
    ms = jnp.mean(h * h, axis=-1, keepdims=True)
    o_ref[...] = h * lax.rsqrt(ms + EPS) * nw_ref[...]


def _combine(dest_tiles, yb, route, h1, g2, nw, *, tile, tiles_per_batch):
    T = h1.shape[0]
    return pl.pallas_call(
        functools.partial(_combine_kernel, tile=tile, n_tiles=T // tile),
        out_shape=jax.ShapeDtypeStruct((T, D_MODEL), F32),
        grid=(T // tile,),
        in_specs=[pl.BlockSpec(memory_space=pl.ANY),
                  pl.BlockSpec(memory_space=pl.ANY),
                  pl.BlockSpec((tile, LANES), lambda i: (i, 0)),
                  pl.BlockSpec((tile, D_MODEL), lambda i: (i, 0)),
                  pl.BlockSpec((None, 1, D_MODEL), lambda i: (i // tiles_per_batch, 0, 0)),
                  pl.BlockSpec((1, D_MODEL), lambda i: (0, 0))],
        out_specs=pl.BlockSpec((tile, D_MODEL), lambda i: (i, 0)),
        scratch_shapes=[pltpu.SMEM((2 * tile * TOP_K,), I32),
                        pltpu.VMEM((2, TOP_K, tile * ROW_TILES, LANES), F32),
                        pltpu.SemaphoreType.DMA((2,)),
                        pltpu.SemaphoreType.DMA((2,))],
        compiler_params=_cparams(("arbitrary",)),
        name="combine_norm",
    )(dest_tiles, yb, route, h1, g2, nw)


def _rope_tables(n):
    rows = n // GRID_W
    row = jnp.repeat(jnp.arange(rows, dtype=F32), GRID_W)
    col = jnp.tile(jnp.arange(GRID_W, dtype=F32), rows)
    n_freq = RET_DK // 4
    inv = ROPE_BASE ** (-jnp.arange(n_freq, dtype=F32) / n_freq)
    ang = jnp.concatenate([row[:, None] * inv, col[:, None] * inv], axis=-1)
    cos, sin = jnp.cos(ang), jnp.sin(ang)
    cos_h = jnp.concatenate([cos, cos], axis=-1)
    sin_h = jnp.concatenate([-sin, sin], axis=-1)
    return jnp.tile(cos_h, (1, RET_HEADS)), jnp.tile(sin_h, (1, RET_HEADS))


def kernel(x, c, ctx, c_ctx, w_ada, b_ada, norm_mix_w, norm_ffn_w, w_in, w_out, hg_lb,
           hg_norm_w, router_w, router_b, w1, b1, w2, b2, norm_final_w):
    B, N, D = x.shape
    C = ctx.shape[1]
    T = B * N
    assert D == D_MODEL and w_ada.shape[0] == 1

    cc = jnp.concatenate([c.astype(F32), c_ctx.astype(F32)[None, :],
                          jnp.zeros((16 - B - 1, D), F32)], axis=0)
    mod = _ada(cc, w_ada[0], b_ada[0][None, :])
    mod = mod.reshape(16, 6, 1, D).transpose(1, 0, 2, 3)
    sh1, sc1, g1, sh2, sc2, g2 = (mod[i] for i in range(6))

    w_in_bf = w_in[0].astype(BF16)
    w_out_bf = w_out[0].astype(BF16)
    nw_mix = norm_mix_w[0][None, :]
    cos_t, sin_t = _rope_tables(N)

    pb, lf = _inproj(x, sc1, sh1, None, nw_mix, w_in_bf, hg_lb[:2], cos_t, sin_t,
                     rope=True, tm=512)
    pbc, lfc = _inproj(ctx, sc1, sh1, B, nw_mix, w_in_bf, hg_lb[:2], cos_t[:C], sin_t[:C],
                       rope=False, tm=C)

    a_ret = _retention(pb, pbc)
    a_hg = _hgrn(pb, lf, pbc, lfc, hg_norm_w[0][None, :])

    rw = jnp.zeros((D, LANES), F32).at[:, :N_EXPERTS].set(router_w[0])
    rb = jnp.full((1, LANES), -1e30, F32).at[0, :N_EXPERTS].set(router_b[0])
    h1, v, route, route_t, cnt = _outproj(a_ret, a_hg, w_out_bf, x, g1, sc2, sh2,
                                          norm_ffn_w[0][None, :], rw, rb, tm=512)

    idx = route_t[0:TOP_K].astype(I32)
    rank = route_t[2 * TOP_K:3 * TOP_K].astype(I32)
    counts = cnt[0, :N_EXPERTS].astype(I32)
    padded = (counts + ROW_BLOCK - 1) // ROW_BLOCK * ROW_BLOCK
    pend = jnp.cumsum(padded)
    pstart = pend - padded
    onehot = idx[None] == jnp.arange(N_EXPERTS, dtype=I32)[:, None, None]
    row_start = jnp.sum(jnp.where(onehot, pstart[:, None, None], 0), axis=0)
    dest = (row_start + rank) * ROW_TILES

    def dest_tiles(tile):
        return dest.reshape(TOP_K, T // tile, tile).transpose(1, 0, 2).reshape(T // tile,
                                                                               TOP_K * tile)
    n_blocks = (T * TOP_K) // ROW_BLOCK + N_EXPERTS
    n_used = (pend[-1] // ROW_BLOCK).astype(I32)
    starts = jnp.arange(n_blocks, dtype=I32) * ROW_BLOCK
    block_e = jnp.minimum(jnp.sum((pend[None, :] <= starts[:, None]).astype(I32), axis=1),
                          N_EXPERTS - 1)
    valid = jnp.clip(counts[block_e] - (starts - pstart[block_e]), 0, ROW_BLOCK)
    nsub = (valid + SUB_ROWS - 1) // SUB_ROWS
    first_from = lax.cummin(jnp.where(counts > 0, jnp.arange(N_EXPERTS, dtype=I32), N_EXPERTS),
                            reverse=True)
    next_e = jnp.concatenate([first_from[1:], jnp.full((1,), N_EXPERTS, I32)])
    next_e = jnp.where(next_e == N_EXPERTS, -1, next_e)
    meta = jnp.concatenate([pend.astype(I32), counts, n_used[None]])

    tile_d = 512
    xs = _dispatch(meta, dest_tiles(tile_d), v, tile=tile_d, n_blocks=n_blocks)
    yb = _moe(block_e, nsub, next_e, n_used[None], xs, w1[0], b1[0][:, None, :], w2[0],
              b2[0][:, None, :], n_blocks=n_blocks)
    tile_c = 256
    out = _combine(dest_tiles(tile_c), yb, route, h1, g2,
                   norm_final_w[None, :], tile=tile_c, tiles_per_batch=N // tile_c)
    return out.reshape(B, N, D)
```

```python
import functools
import math

import jax
import jax.numpy as jnp
import numpy as np
from jax import lax
from jax.experimental import pallas as pl
from jax.experimental.pallas import tpu as pltpu

F32 = jnp.float32
BF16 = jnp.bfloat16
I32 = jnp.int32

D_MODEL = 1024
GRID_W = 64
RET_HEADS = 4
RET_DK = 64
HG_HEADS = 4
PROJ_W = 4096
ROPE_BASE = 10000.0
EPS = 1e-6
N_EXPERTS = 32
TOP_K = 4
D_FF = 1024
SWIGLU_LIMIT = 7.0
SWIGLU_ALPHA = 1.702

LANES = 128
CHUNK = 128
ROW_BLOCK = 1024
SUB_ROWS = 256
VMEM_LIMIT = 56 * 1024 * 1024
ROW_TILES = D_MODEL // LANES

C_RQ, C_RK, C_RV, C_RG, C_HQ, C_FF, C_FB, C_HV, C_HG = (
    0, 256, 512, 1024, 1536, 2048, 2560, 3072, 3584)


def _cparams(sem):
    return pltpu.CompilerParams(dimension_semantics=sem, vmem_limit_bytes=VMEM_LIMIT)


def _split_bf16(x):
    hi = x.astype(BF16)
    lo = (x - hi.astype(F32)).astype(BF16)
    return hi, lo


def _dot(a, b):
    return jnp.dot(a, b, preferred_element_type=F32)


def _dot_nt(a, b):
    return lax.dot_general(a, b, (((1,), (1,)), ((), ())), preferred_element_type=F32)


def _dot3(a, b):
    ah, al = _split_bf16(a)
    bh, bl = _split_bf16(b)
    return _dot(ah, bh) + (_dot(ah, bl) + _dot(al, bh))


def _silu(x):
    return x * jax.nn.sigmoid(x)


def _load_tile_rows(ref, n):
    return jnp.concatenate([ref[pl.ds(s, n, stride=ROW_TILES), :] for s in range(ROW_TILES)],
                           axis=1)


def _store_tile_rows(ref, x):
    n = x.shape[0]
    for s in range(ROW_TILES):
        ref[pl.ds(s, n, stride=ROW_TILES), :] = x[:, s * LANES:(s + 1) * LANES]


def _ada_kernel(c_ref, w_ref, b_ref, o_ref):
    s = _silu(c_ref[...])
    o_ref[...] = _dot3(s, w_ref[...]) + b_ref[...]


def _ada(cc, w, b):
    nblk = w.shape[1] // D_MODEL
    return pl.pallas_call(
        _ada_kernel,
        out_shape=jax.ShapeDtypeStruct((cc.shape[0], w.shape[1]), F32),
        grid=(nblk,),
        in_specs=[pl.BlockSpec(cc.shape, lambda j: (0, 0)),
                  pl.BlockSpec((D_MODEL, D_MODEL), lambda j: (0, j)),
                  pl.BlockSpec((1, D_MODEL), lambda j: (0, j))],
        out_specs=pl.BlockSpec((cc.shape[0], D_MODEL), lambda j: (0, j)),
        compiler_params=_cparams(("arbitrary",)),
        name="ada",
    )(cc, w, b)


def _inproj_kernel(x_ref, sc_ref, sh_ref, nw_ref, w_ref, lb_ref, cos_ref, sin_ref,
                   pb_ref, lf_ref, u_scr, *, rope):
    x = x_ref[...]
    ms = jnp.mean(x * x, axis=-1, keepdims=True)
    u = x * lax.rsqrt(ms + EPS) * nw_ref[...] * (1.0 + sc_ref[...]) + sh_ref[...]
    u_scr[...] = u.astype(BF16)

    def proj(lo, width):
        return _dot(u_scr[...], w_ref[:, lo:lo + width])

    tm = x.shape[0]
    if rope:
        lane = lax.broadcasted_iota(I32, (tm, LANES), 1)
        first = (lane & 32) == 0

    def put_rot(col, scale):
        for j in range(2):
            lo = col + j * LANES
            t = proj(lo, LANES)
            if scale != 1.0:
                t = t * scale
            if rope:
                tb = j * LANES
                rot = jnp.where(first, pltpu.roll(t, 96, axis=1), pltpu.roll(t, 32, axis=1))
                t = t * cos_ref[:, tb:tb + LANES] + rot * sin_ref[:, tb:tb + LANES]
            pb_ref[:, lo:lo + LANES] = t.astype(BF16)

    put_rot(C_RQ, 1.0)
    put_rot(C_RK, RET_DK ** -0.5)
    pb_ref[:, C_RV:C_RV + 512] = proj(C_RV, 512).astype(BF16)
    pb_ref[:, C_RG:C_RG + 512] = _silu(proj(C_RG, 512)).astype(BF16)
    pb_ref[:, C_HQ:C_HQ + 512] = _silu(proj(C_HQ, 512)).astype(BF16)
    pb_ref[:, C_HV:C_HV + 512] = proj(C_HV, 512).astype(BF16)
    pb_ref[:, C_HG:C_HG + 512] = _silu(proj(C_HG, 512)).astype(BF16)

    la = lb_ref[0]
    lbb = lb_ref[1]
    mx = jnp.maximum(la, lbb)
    ea = jnp.exp(la - mx)
    eb = jnp.exp(lbb - mx)
    lb = ea / (ea + eb)
    for d, col in enumerate((C_FF, C_FB)):
        lbd = lb[d:d + 1, :]
        f = lbd + (1.0 - lbd) * jax.nn.sigmoid(proj(col, 512))
        pb_ref[:, col:col + 512] = (1.0 - f).astype(BF16)
        lf_ref[:, d * 512:(d + 1) * 512] = jnp.log(f)


def _inproj(x, sc, sh, mod_row, nw, w_bf, hg_lb, cos_t, sin_t, *, rope, tm):
    B, n, _ = x.shape
    nt = n // tm
    if mod_row is None:
        mrow = lambda b, j: (b, 0, 0)
    else:
        mrow = lambda b, j: (mod_row, 0, 0)
    return pl.pallas_call(
        functools.partial(_inproj_kernel, rope=rope),
        out_shape=(jax.ShapeDtypeStruct((B, n, PROJ_W), BF16),
                   jax.ShapeDtypeStruct((B, n, 1024), F32)),
        grid=(B, nt),
        in_specs=[pl.BlockSpec((None, tm, D_MODEL), lambda b, j: (b, j, 0)),
                  pl.BlockSpec((None, 1, D_MODEL), mrow),
                  pl.BlockSpec((None, 1, D_MODEL), mrow),
                  pl.BlockSpec((1, D_MODEL), lambda b, j: (0, 0)),
                  pl.BlockSpec((D_MODEL, PROJ_W), lambda b, j: (0, 0),
                               pipeline_mode=pl.Buffered(1)),
                  pl.BlockSpec((2, 2, 512), lambda b, j: (0, 0, 0)),
                  pl.BlockSpec((tm, 256), lambda b, j: (j, 0)),
                  pl.BlockSpec((tm, 256), lambda b, j: (j, 0))],
        out_specs=(pl.BlockSpec((None, tm, PROJ_W), lambda b, j: (b, j, 0)),
                   pl.BlockSpec((None, tm, 1024), lambda b, j: (b, j, 0))),
        scratch_shapes=[pltpu.VMEM((tm, D_MODEL), BF16)],
        compiler_params=_cparams(("arbitrary", "arbitrary")),
        name="inproj_rope" if rope else "inproj_ctx",
    )(x, sc, sh, nw, w_bf, hg_lb, cos_t, sin_t)


_RET_LGF = [math.log1p(-(2.0 ** (-5.0 - 2.0 * h))) for h in range(RET_HEADS)]
_RET_LGB = [math.log1p(-(2.0 ** (-6.0 - 2.0 * h))) for h in range(RET_HEADS)]


def _ret_kernel(q_ref, k_ref, v_ref, g_ref, kc_ref, vc_ref, o_ref,
                u_scr, sin_scr, kt_scr, dtot_scr, *, n_lat, n_ctx):
    L = CHUNK
    pair = pl.program_id(1)
    row = lax.broadcasted_iota(I32, (L, L), 0).astype(F32)
    col = lax.broadcasted_iota(I32, (L, L), 1).astype(F32)
    lane = lax.broadcasted_iota(I32, (L, LANES), 1)
    trow = lax.broadcasted_iota(I32, (L, 1), 0).astype(F32)
    tcol = lax.broadcasted_iota(I32, (1, L), 1).astype(F32)
    low_half = lane < RET_DK

    def u_chunk(k_blk, v_blk, ci, store_kt):
        kt = k_blk.astype(F32).T
        if store_kt is not None:
            kt_scr[store_kt] = kt.astype(BF16)
        for hh in range(2):
            lgf = jnp.where(pair == 0, _RET_LGF[hh], _RET_LGF[2 + hh])
            lgb = jnp.where(pair == 0, _RET_LGB[hh], _RET_LGB[2 + hh])
            kth = kt[hh * RET_DK:(hh + 1) * RET_DK, :]
            wkf = jnp.exp(lgf * (L - 1.0 - tcol))
            wkb = jnp.exp(lgb * tcol)
            lhs = jnp.concatenate([kth * wkf, kth * wkb], axis=0).astype(BF16)
            u_scr[hh, ci] = _dot(lhs, v_blk[:, hh * LANES:(hh + 1) * LANES])

    for c in range(n_ctx):
        u_chunk(kc_ref[c * L:(c + 1) * L, :], vc_ref[c * L:(c + 1) * L, :], c, None)

    def lat_u(c, carry):
        r0 = pl.multiple_of(c * L, L)
        u_chunk(k_ref[pl.ds(r0, L), :], v_ref[pl.ds(r0, L), :], n_ctx + c, c)
        return carry
    lax.fori_loop(0, n_lat, lat_u, 0, unroll=2)

    ones = jnp.ones((RET_DK, LANES), F32)
    for hh in range(2):
        lgf = jnp.where(pair == 0, _RET_LGF[hh], _RET_LGF[2 + hh])
        lgb = jnp.where(pair == 0, _RET_LGB[hh], _RET_LGB[2 + hh])
        d = row - col
        dtot_scr[hh] = jnp.where(d > 0, jnp.exp(lgf * jnp.maximum(d, 0.0)),
                                 jnp.where(d < 0, jnp.exp(lgb * jnp.maximum(-d, 0.0)), 2.0))
        af = jnp.exp(ones * (lgf * L))
        ab = jnp.exp(ones * (lgb * L))

        s = jnp.zeros((RET_DK, LANES), F32)
        for c in range(n_ctx):
            s = af * s + u_scr[hh, c, 0:RET_DK, :]
        sb = jnp.zeros((RET_DK, LANES), F32)
        for c in reversed(range(n_ctx)):
            sb = ab * sb + u_scr[hh, c, RET_DK:2 * RET_DK, :]

        def fwd(c, s, hh=hh, af=af):
            sin_scr[hh, c, 0:RET_DK, :] = s.astype(BF16)
            return af * s + u_scr[hh, n_ctx + c, 0:RET_DK, :]
        lax.fori_loop(0, n_lat, fwd, s)

        def bwd(i, sb, hh=hh, ab=ab):
            c = n_lat - 1 - i
            sin_scr[hh, c, RET_DK:2 * RET_DK, :] = sb.astype(BF16)
            return ab * sb + u_scr[hh, n_ctx + c, RET_DK:2 * RET_DK, :]
        lax.fori_loop(0, n_lat, bwd, sb)

    def out_chunk(c, carry):
        r0 = pl.multiple_of(c * L, L)
        q = q_ref[pl.ds(r0, L), :].astype(F32)
        qr = pltpu.roll(q, RET_DK, axis=1)
        kt = kt_scr[c]
        for hh in range(2):
            lgf = jnp.where(pair == 0, _RET_LGF[hh], _RET_LGF[2 + hh])
            lgb = jnp.where(pair == 0, _RET_LGB[hh], _RET_LGB[2 + hh])
            mine = low_half if hh == 0 else jnp.logical_not(low_half)
            qm = jnp.where(mine, q, 0.0).astype(BF16)
            p = (_dot(qm, kt) * dtot_scr[hh]).astype(BF16)
            vh = v_ref[pl.ds(r0, L), hh * LANES:(hh + 1) * LANES]
            wqf = jnp.exp(lgf * (trow + 1.0))
            wqb = jnp.exp(lgb * (L - trow))
            qa, qb = (q, qr) if hh == 0 else (qr, q)
            qs = jnp.where(low_half, qa * wqf, qb * wqb).astype(BF16)
            o = _dot(p, vh) + _dot(qs, sin_scr[hh, c])
            ms = jnp.mean(o * o, axis=-1, keepdims=True)
            gh = g_ref[pl.ds(r0, L), hh * LANES:(hh + 1) * LANES].astype(F32)
            o_ref[pl.ds(r0, L), hh * LANES:(hh + 1) * LANES] = (
                o * lax.rsqrt(ms + EPS) * gh).astype(BF16)
        return carry
    lax.fori_loop(0, n_lat, out_chunk, 0, unroll=2)


def _retention(pb, pbc):
    B, n, _ = pb.shape
    nc = pbc.shape[1]
    n_lat, n_ctx = n // CHUNK, nc // CHUNK
    return pl.pallas_call(
        functools.partial(_ret_kernel, n_lat=n_lat, n_ctx=n_ctx),
        out_shape=jax.ShapeDtypeStruct((B, n, 512), BF16),
        grid=(B, 2),
        in_specs=[pl.BlockSpec((None, n, LANES), lambda b, p: (b, 0, C_RQ // LANES + p)),
                  pl.BlockSpec((None, n, LANES), lambda b, p: (b, 0, C_RK // LANES + p)),
                  pl.BlockSpec((None, n, 256), lambda b, p: (b, 0, C_RV // 256 + p)),
                  pl.BlockSpec((None, n, 256), lambda b, p: (b, 0, C_RG // 256 + p)),
                  pl.BlockSpec((None, nc, LANES), lambda b, p: (b, 0, C_RK // LANES + p)),
                  pl.BlockSpec((None, nc, 256), lambda b, p: (b, 0, C_RV // 256 + p))],
        out_specs=pl.BlockSpec((None, n, 256), lambda b, p: (b, 0, p)),
        scratch_shapes=[pltpu.VMEM((2, n_lat + n_ctx, CHUNK, LANES), F32),
                        pltpu.VMEM((2, n_lat, CHUNK, LANES), BF16),
                        pltpu.VMEM((n_lat, LANES, CHUNK), BF16),
                        pltpu.VMEM((2, CHUNK, CHUNK), F32)],
        compiler_params=_cparams(("arbitrary", "arbitrary")),
        name="retention",
    )(pb, pb, pb, pb, pbc, pbc)


_LEVELS = (64, 32, 16, 8, 4, 2, 1)


def _expand_rows(r, rep):
    n = r.shape[0]
    if n == 1:
        return jnp.broadcast_to(r, (rep, r.shape[1]))
    return jnp.concatenate(
        [jnp.broadcast_to(r[i:i + 1, :], (rep, r.shape[1])) for i in range(n)], axis=0)


def _hgrn_kernel(q_ref, kf_ref, kb_ref, v_ref, g_ref, lff_ref, lfb_ref,
                 kfc_ref, kbc_ref, vc_ref, lffc_ref, lfbc_ref, nw_ref, o_ref,
                 ut_scr, a_scr, qs_scr, oi_scr, sin_scr, bfb_scr, *, n_lat, n_ctx):
    L = CHUNK
    row = lax.broadcasted_iota(I32, (L, L), 0)
    col = lax.broadcasted_iota(I32, (L, L), 1)
    xr_bits = lax.bitcast_convert_type((row ^ col).astype(F32), I32)
    lv = lax.shift_right_logical(xr_bits, 23) - 127
    row2 = lax.broadcasted_iota(I32, (L, 2 * L), 0)
    col2 = lax.broadcasted_iota(I32, (L, 2 * L), 1) & (L - 1)
    tril2 = jnp.where(col2 <= row2, 1.0, 0.0).astype(BF16)
    triu2 = jnp.where(col2 >= row2, 1.0, 0.0).astype(BF16)

    def cums(lff, lfb):
        hf, lof = _split_bf16(lff)
        hb, lob = _split_bf16(lfb)
        bf = _dot(tril2, jnp.concatenate([hf, lof], axis=0))
        bb = _dot(triu2, jnp.concatenate([hb, lob], axis=0))
        return bf, bb

    def state_part(ci, kf, kb, v_blk, bf, bb):
        endf = bf[L - 1:L, :]
        endb = bb[0:1, :]
        ksf = kf * jnp.exp(endf - bf)
        ksb = kb * jnp.exp(endb - bb)
        vt = v_blk.astype(F32).T.astype(BF16)
        ut_scr[ci] = _dot(vt, jnp.concatenate([ksf, ksb], axis=1).astype(BF16))
        a_scr[ci] = jnp.broadcast_to(
            jnp.concatenate([jnp.exp(endf), jnp.exp(endb)], axis=1), (8, 2 * LANES))

    for c in range(n_ctx):
        sl = slice(c * L, (c + 1) * L)
        bf, bb = cums(lffc_ref[sl, :], lfbc_ref[sl, :])
        state_part(c, kfc_ref[sl, :].astype(F32), kbc_ref[sl, :].astype(F32),
                   vc_ref[sl, :], bf, bb)

    def lat_chunk(c, u):
        rows = pl.ds(pl.multiple_of(c * L, L), L)
        bf_scr = bfb_scr.at[u, 0]
        bb_scr = bfb_scr.at[u, 1]
        q = q_ref[rows, :].astype(F32)
        kf = kf_ref[rows, :].astype(F32)
        kb = kb_ref[rows, :].astype(F32)
        v_blk = v_ref[rows, :]
        lff = lff_ref[rows, :]
        lfb = lfb_ref[rows, :]
        bf, bb = cums(lff, lfb)
        state_part(n_ctx + c, kf, kb, v_blk, bf, bb)
        qs_scr[rows, :] = jnp.concatenate([q * jnp.exp(bf), q * jnp.exp(bb)],
                                          axis=1).astype(BF16)
        bf_scr[...] = bf
        bb_scr[...] = bb

        acc = jnp.zeros((L, L), F32)
        for lvl, h in enumerate(_LEVELS):
            bit = (row & h) != 0
            ksel = jnp.where(bit, kb, kf)
            if h >= 4:
                n = (L // 2) // h
                if n == 1:
                    rf = bf_scr[h - 1:h, :]
                    rb = bb_scr[h:h + 1, :]
                else:
                    rf = bf_scr[pl.ds(h - 1, n, stride=2 * h), :]
                    rb = bb_scr[pl.ds(h, n, stride=2 * h), :]
                df = bf - _expand_rows(rf, 2 * h)
                db = bb - _expand_rows(rb, 2 * h)
                eq = jnp.where(bit, df, db)
                ek = -jnp.where(bit, db, df)
            elif h == 2:
                m = row & 3
                lff_n = pltpu.roll(lff, L - 1, axis=0)
                lfb_n = pltpu.roll(lfb, L - 1, axis=0)
                eq = jnp.where(m == 2, lff,
                               jnp.where(m == 3, lff + pltpu.roll(lff, 1, axis=0),
                                         jnp.where(m == 0, lfb + lfb_n, lfb)))
                ek = jnp.where(m == 3, pltpu.roll(lfb, 1, axis=0),
                               jnp.where(m == 0, lff_n, 0.0))
            else:
                eq = jnp.where(bit, lff, lfb)
                ek = None
            lhs = (q * jnp.exp(eq)).astype(BF16)
            rhs = (ksel if ek is None else ksel * jnp.exp(ek)).astype(BF16)
            acc = jnp.where(lv == 6 - lvl, _dot_nt(lhs, rhs), acc)

        dsum = jnp.sum(q * (kf + kb), axis=-1, keepdims=True)
        oi_scr[rows, :] = _dot(acc.astype(BF16), v_blk) + dsum * v_blk.astype(F32)

    def lat_pair(i, carry):
        for u in range(4):
            lat_chunk(4 * i + u, u)
        return carry
    lax.fori_loop(0, n_lat // 4, lat_pair, 0)

    st = jnp.zeros((LANES, LANES), F32)
    for c in range(n_ctx):
        st = st * a_scr[c, 0:1, 0:LANES] + ut_scr[c, :, 0:LANES]
    stb = jnp.zeros((LANES, LANES), F32)
    for c in reversed(range(n_ctx)):
        stb = stb * a_scr[c, 0:1, LANES:2 * LANES] + ut_scr[c, :, LANES:2 * LANES]

    def fwd(c, st):
        sin_scr[c, :, 0:LANES] = st.astype(BF16)
        ci = n_ctx + c
        return st * a_scr[ci, 0:1, 0:LANES] + ut_scr[ci, :, 0:LANES]
    lax.fori_loop(0, n_lat, fwd, st)

    def bwd(i, stb):
        c = n_lat - 1 - i
        sin_scr[c, :, LANES:2 * LANES] = stb.astype(BF16)
        ci = n_ctx + c
        return stb * a_scr[ci, 0:1, LANES:2 * LANES] + ut_scr[ci, :, LANES:2 * LANES]
    lax.fori_loop(0, n_lat, bwd, stb)

    def out_chunk(c, carry):
        rows = pl.ds(pl.multiple_of(c * L, L), L)
        o = oi_scr[rows, :] + _dot_nt(qs_scr[rows, :], sin_scr[c])
        ms = jnp.mean(o * o, axis=-1, keepdims=True)
        y = o * lax.rsqrt(ms + EPS) * nw_ref[...] * g_ref[rows, :].astype(F32)
        o_ref[rows, :] = y.astype(BF16)
        return carry
    lax.fori_loop(0, n_lat, out_chunk, 0, unroll=4)


def _hgrn(pb, lf, pbc, lfc, nw):
    B, n, _ = pb.shape
    nc = pbc.shape[1]
    n_lat, n_ctx = n // CHUNK, nc // CHUNK

    def colblk(rows, col0):
        return pl.BlockSpec((None, rows, LANES), lambda b, h: (b, 0, col0 // LANES + h))

    return pl.pallas_call(
        functools.partial(_hgrn_kernel, n_lat=n_lat, n_ctx=n_ctx),
        out_shape=jax.ShapeDtypeStruct((B, n, 512), BF16),
        grid=(B, HG_HEADS),
        in_specs=[colblk(n, C_HQ), colblk(n, C_FF), colblk(n, C_FB), colblk(n, C_HV),
                  colblk(n, C_HG), colblk(n, 0), colblk(n, 512),
                  colblk(nc, C_FF), colblk(nc, C_FB), colblk(nc, C_HV),
                  colblk(nc, 0), colblk(nc, 512),
                  pl.BlockSpec((1, LANES), lambda b, h: (0, 0))],
        out_specs=pl.BlockSpec((None, n, LANES), lambda b, h: (b, 0, h)),
        scratch_shapes=[pltpu.VMEM((n_lat + n_ctx, LANES, 2 * LANES), F32),
                        pltpu.VMEM((n_lat + n_ctx, 8, 2 * LANES), F32),
                        pltpu.VMEM((n, 2 * LANES), BF16),
                        pltpu.VMEM((n, LANES), F32),
                        pltpu.VMEM((n_lat, LANES, 2 * LANES), BF16),
                        pltpu.VMEM((4, 2, CHUNK, LANES), F32)],
        compiler_params=_cparams(("arbitrary", "arbitrary")),
        name="hgrn2",
    )(pb, pb, pb, pb, pb, lf, lf, pbc, pbc, pbc, lfc, lfc, nw)


def _outproj_kernel(ar_ref, ah_ref, w_ref, x_ref, g1_ref, sc_ref, sh_ref, nw_ref,
                    rw_ref, rb_ref, h1_ref, v_ref, route_ref, route_t_ref, cnt_ref, cnt_scr):
    first_step = jnp.logical_and(pl.program_id(0) == 0, pl.program_id(1) == 0)

    @pl.when(first_step)
    def _():
        cnt_scr[...] = jnp.zeros_like(cnt_scr)

    y = _dot(ar_ref[...], w_ref[0:512, :]) + _dot(ah_ref[...], w_ref[512:1024, :])
    h1 = x_ref[...] + g1_ref[...] * y
    h1_ref[...] = h1
    ms = jnp.mean(h1 * h1, axis=-1, keepdims=True)
    v = h1 * lax.rsqrt(ms + EPS) * nw_ref[...] * (1.0 + sc_ref[...]) + sh_ref[...]
    _store_tile_rows(v_ref, v)

    tm = v.shape[0]
    logits = _dot3(v, rw_ref[...]) + rb_ref[...]
    lane = lax.broadcasted_iota(I32, (tm, LANES), 1)
    lane_f = lane.astype(F32)
    l = logits
    sels, tops, idxs = [], [], []
    for _ in range(TOP_K):
        m = jnp.max(l, axis=1, keepdims=True)
        i = jnp.min(jnp.where(l == m, lane_f, float(LANES)), axis=1, keepdims=True)
        sel = lane_f == i
        l = jnp.where(sel, -jnp.inf, l)
        sels.append(sel)
        tops.append(m)
        idxs.append(i)
    es = [jnp.exp(t - tops[0]) for t in tops]
    den = es[0] + es[1] + es[2] + es[3]
    gates = [e / den for e in es]

    oh = jnp.zeros((tm, LANES), F32)
    for sel in sels:
        oh = jnp.where(sel, 1.0, oh)
    r = lax.broadcasted_iota(I32, (tm, tm), 0)
    c = lax.broadcasted_iota(I32, (tm, tm), 1)
    tri = jnp.where(c < r, 1.0, 0.0).astype(BF16)
    before = _dot(tri, oh.astype(BF16)) + cnt_scr[0:1, :]
    ranks = [jnp.sum(jnp.where(sel, before, 0.0), axis=1, keepdims=True) for sel in sels]
    cnt_scr[...] = cnt_scr[...] + jnp.sum(oh, axis=0, keepdims=True)
    cnt_ref[...] = cnt_scr[...]

    out = jnp.zeros((tm, LANES), F32)
    for k in range(TOP_K):
        out = jnp.where(lane == k, idxs[k], out)
        out = jnp.where(lane == TOP_K + k, gates[k], out)
        out = jnp.where(lane == 2 * TOP_K + k, ranks[k], out)
    route_ref[...] = out
    route_t_ref[...] = out.T[0:16, :]


def _outproj(a_ret, a_hg, w_bf, x, g1, sc2, sh2, nw, rw, rb, *, tm):
    B, n, _ = x.shape
    nt = n // tm
    T = B * n
    mrow = lambda b, j: (b, 0, 0)
    tok = lambda b, j: (b * nt + j, 0)
    return pl.pallas_call(
        _outproj_kernel,
        out_shape=(jax.ShapeDtypeStruct((T, D_MODEL), F32),
                   jax.ShapeDtypeStruct((T * ROW_TILES, LANES), F32),
                   jax.ShapeDtypeStruct((T, LANES), F32),
                   jax.ShapeDtypeStruct((16, T), F32),
                   jax.ShapeDtypeStruct((8, LANES), F32)),
        grid=(B, nt),
        in_specs=[pl.BlockSpec((None, tm, 512), lambda b, j: (b, j, 0)),
                  pl.BlockSpec((None, tm, 512), lambda b, j: (b, j, 0)),
                  pl.BlockSpec((D_MODEL, D_MODEL), lambda b, j: (0, 0)),
                  pl.BlockSpec((None, tm, D_MODEL), lambda b, j: (b, j, 0)),
                  pl.BlockSpec((None, 1, D_MODEL), mrow),
                  pl.BlockSpec((None, 1, D_MODEL), mrow),
                  pl.BlockSpec((None, 1, D_MODEL), mrow),
                  pl.BlockSpec((1, D_MODEL), lambda b, j: (0, 0)),
                  pl.BlockSpec((D_MODEL, LANES), lambda b, j: (0, 0)),
                  pl.BlockSpec((1, LANES), lambda b, j: (0, 0))],
        out_specs=(pl.BlockSpec((tm, D_MODEL), tok),
                   pl.BlockSpec((tm * ROW_TILES, LANES), tok),
                   pl.BlockSpec((tm, LANES), tok),
                   pl.BlockSpec((16, tm), lambda b, j: (0, b * nt + j)),
                   pl.BlockSpec((8, LANES), lambda b, j: (0, 0))),
        scratch_shapes=[pltpu.VMEM((8, LANES), F32)],
        compiler_params=_cparams(("arbitrary", "arbitrary")),
        name="outproj_router",
    )(a_ret, a_hg, w_bf, x, g1, sc2, sh2, nw, rw, rb)


def _dispatch_kernel(meta_ref, dest_hbm, v_ref, xs_hbm, idx_smem, zbuf, sem_idx, sem_z, sem_rows,
                     *, tile, n_blocks, n_tiles):
    i = pl.program_id(0)
    slot = lax.rem(i, 2)

    def idx_copy(j, s):
        n_asg = tile * TOP_K
        return pltpu.make_async_copy(dest_hbm.at[j], idx_smem.at[pl.ds(s * n_asg, n_asg)],
                                     sem_idx.at[s])

    @pl.when(i == 0)
    def _():
        idx_copy(0, 0).start()

    def zero_copy(r0):
        return pltpu.make_async_copy(
            zbuf, xs_hbm.at[pl.ds(r0 * ROW_TILES, SUB_ROWS * ROW_TILES), :], sem_z)

    @pl.when(i == 0)
    def _():
        zbuf[...] = jnp.zeros_like(zbuf)

        def has_tail(e):
            return (meta_ref[N_EXPERTS + e] & (SUB_ROWS - 1)) != 0

        def tail_start(e):
            full = lax.shift_right_logical(meta_ref[N_EXPERTS + e], SUB_ROWS.bit_length() - 1)
            return pl.multiple_of(meta_ref[e] + full * SUB_ROWS, SUB_ROWS)

        def z_issue(e, carry):
            @pl.when(has_tail(e))
            def _():
                zero_copy(tail_start(e)).start()
            return carry
        lax.fori_loop(0, N_EXPERTS, z_issue, 0)

        def z_wait(e, carry):
            @pl.when(has_tail(e))
            def _():
                zero_copy(0).wait()
            return carry
        lax.fori_loop(0, N_EXPERTS, z_wait, 0)

    idx_copy(i, slot).wait()

    @pl.when(i + 1 < n_tiles)
    def _():
        idx_copy(i + 1, 1 - slot).start()

    def issue(t, carry):
        for k in range(TOP_K):
            d = pl.multiple_of(idx_smem[slot * (tile * TOP_K) + k * tile + t], ROW_TILES)
            src = v_ref.at[pl.ds(pl.multiple_of(t * ROW_TILES, ROW_TILES), ROW_TILES), :]
            pltpu.make_async_copy(src, xs_hbm.at[pl.ds(d, ROW_TILES), :],
                                  sem_rows).start(priority=k % 2)
        return carry
    lax.fori_loop(0, tile, issue, 0, unroll=4)
    for _ in range(TOP_K):
        pltpu.make_async_copy(v_ref, xs_hbm.at[pl.ds(0, tile * ROW_TILES), :], sem_rows).wait()


def _dispatch(meta, dest_tiles, v, *, tile, n_blocks):
    T = v.shape[0] // ROW_TILES
    R = n_blocks * ROW_BLOCK
    return pl.pallas_call(
        functools.partial(_dispatch_kernel, tile=tile, n_blocks=n_blocks, n_tiles=T // tile),
        out_shape=jax.ShapeDtypeStruct((R * ROW_TILES, LANES), F32),
        grid_spec=pltpu.PrefetchScalarGridSpec(
            num_scalar_prefetch=1,
            grid=(T // tile,),
            in_specs=[pl.BlockSpec(memory_space=pl.ANY),
                      pl.BlockSpec((tile * ROW_TILES, LANES), lambda i, meta: (i, 0))],
            out_specs=pl.BlockSpec(memory_space=pl.ANY),
            scratch_shapes=[pltpu.SMEM((2 * tile * TOP_K,), I32),
                            pltpu.VMEM((SUB_ROWS * ROW_TILES, LANES), F32),
                            pltpu.SemaphoreType.DMA((2,)),
                            pltpu.SemaphoreType.DMA,
                            pltpu.SemaphoreType.DMA]),
        compiler_params=_cparams(("arbitrary",)),
        name="dispatch",
    )(meta, dest_tiles, v)


def _moe_kernel(be_ref, ns_ref, nx_ref, nu_ref, x_ref, w1_hbm, b1_ref, w2_hbm, b2_ref, y_ref,
                w1_stage, w2_stage, w1_scr, w2_scr, x_scr, act_scr, sem_w):
    i = pl.program_id(0)
    e = be_ref[i]
    nsub = ns_ref[i]
    changed = jnp.logical_or(i == 0, e != be_ref[jnp.maximum(i - 1, 0)])

    def weight_copies(ex):
        return (pltpu.make_async_copy(w1_hbm.at[ex], w1_stage, sem_w.at[0]),
                pltpu.make_async_copy(w2_hbm.at[ex], w2_stage, sem_w.at[1]))

    @pl.when(i == 0)
    def _():
        for cp in weight_copies(e):
            cp.start()

    @pl.when(jnp.logical_and(changed, nsub > 0))
    def _():
        for cp in weight_copies(e):
            cp.wait()
        w1_scr[...] = w1_stage[...].astype(BF16)
        w2_scr[...] = w2_stage[...].astype(BF16)
        nxt = nx_ref[e]

        @pl.when(nxt >= 0)
        def _():
            for cp in weight_copies(nxt):
                cp.start()

    def compute(rows):
        x_scr[0:rows, :] = _load_tile_rows(x_ref, rows).astype(BF16)
        cw = 256
        for c in range(D_FF // cw):
            glu = (_dot(x_scr[0:rows, :], w1_scr[:, c * cw:(c + 1) * cw])
                   + b1_ref[:, c * cw:(c + 1) * cw])
            lin = (_dot(x_scr[0:rows, :], w1_scr[:, D_FF + c * cw:D_FF + (c + 1) * cw])
                   + b1_ref[:, D_FF + c * cw:D_FF + (c + 1) * cw])
            glu = jnp.minimum(glu, SWIGLU_LIMIT)
            lin = jnp.clip(lin, -SWIGLU_LIMIT, SWIGLU_LIMIT)
            act = glu * jax.nn.sigmoid(SWIGLU_ALPHA * glu) * (lin + 1.0)
            act_scr[0:rows, c * cw:(c + 1) * cw] = act.astype(BF16)
        for c in range(D_MODEL // cw):
            y = (_dot(act_scr[0:rows, :], w2_scr[:, c * cw:(c + 1) * cw])
                 + b2_ref[:, c * cw:(c + 1) * cw])
            for s in range(cw // LANES):
                t = c * (cw // LANES) + s
                y_ref[pl.ds(t, rows, stride=ROW_TILES), :] = y[:, s * LANES:(s + 1) * LANES]
        if rows < ROW_BLOCK:
            y_ref[rows * ROW_TILES:, :] = jnp.zeros(((ROW_BLOCK - rows) * ROW_TILES, LANES), F32)

    for m in range(1, ROW_BLOCK // SUB_ROWS + 1):
        @pl.when(nsub == m)
        def _(m=m):
            compute(m * SUB_ROWS)

    @pl.when(nsub == 0)
    def _():
        y_ref[...] = jnp.zeros_like(y_ref)


def _moe(block_e, nsub, next_e, n_used, xs, w1, b1, w2, b2, *, n_blocks):
    return pl.pallas_call(
        _moe_kernel,
        out_shape=jax.ShapeDtypeStruct(xs.shape, F32),
        grid_spec=pltpu.PrefetchScalarGridSpec(
            num_scalar_prefetch=4,
            grid=(n_blocks,),
            in_specs=[pl.BlockSpec((ROW_BLOCK * ROW_TILES, LANES),
                                   lambda i, be, ns, nx, nu: (jnp.minimum(i, nu[0] - 1), 0)),
                      pl.BlockSpec(memory_space=pl.ANY),
                      pl.BlockSpec((None, 1, 2 * D_FF), lambda i, be, ns, nx, nu: (be[i], 0, 0)),
                      pl.BlockSpec(memory_space=pl.ANY),
                      pl.BlockSpec((None, 1, D_MODEL), lambda i, be, ns, nx, nu: (be[i], 0, 0))],
            out_specs=pl.BlockSpec((ROW_BLOCK * ROW_TILES, LANES),
                                   lambda i, be, ns, nx, nu: (i, 0)),
            scratch_shapes=[pltpu.VMEM((D_MODEL, 2 * D_FF), F32),
                            pltpu.VMEM((D_FF, D_MODEL), F32),
                            pltpu.VMEM((D_MODEL, 2 * D_FF), BF16),
                            pltpu.VMEM((D_FF, D_MODEL), BF16),
                            pltpu.VMEM((ROW_BLOCK, D_MODEL), BF16),
                            pltpu.VMEM((ROW_BLOCK, D_FF), BF16),
                            pltpu.SemaphoreType.DMA((2,))]),
        compiler_params=_cparams(("arbitrary",)),
        name="moe_ffn",
    )(block_e, nsub, next_e, n_used, xs, w1, b1, w2, b2)


def _combine_kernel(dest_hbm, yb_hbm, route_ref, h1_ref, g2_ref, nw_ref, o_ref,
                    idx_smem, rows, sem_idx, sem_rows, *, tile, n_tiles):
    i = pl.program_id(0)
    slot = lax.rem(i, 2)

    def idx_copy(j, s):
        n_asg = tile * TOP_K
        return pltpu.make_async_copy(dest_hbm.at[j], idx_smem.at[pl.ds(s * n_asg, n_asg)],
                                     sem_idx.at[s])

    def issue_rows(s):
        def issue(t, carry):
            for k in range(TOP_K):
                d = pl.multiple_of(idx_smem[s * (tile * TOP_K) + k * tile + t], ROW_TILES)
                dst = rows.at[s, k, pl.ds(pl.multiple_of(t * ROW_TILES, ROW_TILES), ROW_TILES), :]
                pltpu.make_async_copy(yb_hbm.at[pl.ds(d, ROW_TILES), :], dst,
                                      sem_rows.at[s]).start(priority=k % 2)
            return carry
        lax.fori_loop(0, tile, issue, 0, unroll=4)

    @pl.when(i == 0)
    def _():
        idx_copy(0, 0).start()
        idx_copy(0, 0).wait()
        issue_rows(0)

        if n_tiles > 1:
            idx_copy(1, 1).start()

    @pl.when(i + 1 < n_tiles)
    def _():
        idx_copy(i + 1, 1 - slot).wait()
        issue_rows(1 - slot)

        @pl.when(i + 2 < n_tiles)
        def _():
            idx_copy(i + 2, slot).start()

    for k in range(TOP_K):
        pltpu.make_async_copy(yb_hbm.at[pl.ds(0, tile * ROW_TILES), :], rows.at[slot, k],
                              sem_rows.at[slot]).wait()

    route = route_ref[...]
    y = jnp.zeros((tile, D_MODEL), F32)
    for k in range(TOP_K):
        y = y + route[:, TOP_K + k:TOP_K + k + 1] * _load_tile_rows(rows.at[slot, k], tile)
    h = h1_ref[...] + g2_ref[...] * y
    ms = jnp.mean(h * h, axis=-1, keepdims=True)
    o_ref[...] = h * lax.rsqrt(ms + EPS) * nw_ref[...]


def _combine(dest_tiles, yb, route, h1, g2, nw, *, tile, tiles_per_batch):
    T = h1.shape[0]
    return pl.pallas_call(
        functools.partial(_combine_kernel, tile=tile, n_tiles=T // tile),
        out_shape=jax.ShapeDtypeStruct((T, D_MODEL), F32),
        grid=(T // tile,),
        in_specs=[pl.BlockSpec(memory_space=pl.ANY),
                  pl.BlockSpec(memory_space=pl.ANY),
                  pl.BlockSpec((tile, LANES), lambda i: (i, 0)),
                  pl.BlockSpec((tile, D_MODEL), lambda i: (i, 0)),
                  pl.BlockSpec((None, 1, D_MODEL), lambda i: (i // tiles_per_batch, 0, 0)),
                  pl.BlockSpec((1, D_MODEL), lambda i: (0, 0))],
        out_specs=pl.BlockSpec((tile, D_MODEL), lambda i: (i, 0)),
        scratch_shapes=[pltpu.SMEM((2 * tile * TOP_K,), I32),
                        pltpu.VMEM((2, TOP_K, tile * ROW_TILES, LANES), F32),
                        pltpu.SemaphoreType.DMA((2,)),
                        pltpu.SemaphoreType.DMA((2,))],
        compiler_params=_cparams(("arbitrary",)),
        name="combine_norm",
    )(dest_tiles, yb, route, h1, g2, nw)


def _rope_tables(n):
    rows = n // GRID_W
    row = jnp.repeat(jnp.arange(rows, dtype=F32), GRID_W)
    col = jnp.tile(jnp.arange(GRID_W, dtype=F32), rows)
    n_freq = RET_DK // 4
    inv = ROPE_BASE ** (-jnp.arange(n_freq, dtype=F32) / n_freq)
    ang = jnp.concatenate([row[:, None] * inv, col[:, None] * inv], axis=-1)
    cos, sin = jnp.cos(ang), jnp.sin(ang)
    cos_h = jnp.concatenate([cos, cos], axis=-1)
    sin_h = jnp.concatenate([-sin, sin], axis=-1)
    return jnp.tile(cos_h, (1, RET_HEADS)), jnp.tile(sin_h, (1, RET_HEADS))


def kernel(x, c, ctx, c_ctx, w_ada, b_ada, norm_mix_w, norm_ffn_w, w_in, w_out, hg_lb,
           hg_norm_w, router_w, router_b, w1, b1, w2, b2, norm_final_w):
    B, N, D = x.shape
    C = ctx.shape[1]
    T = B * N
    assert D == D_MODEL and w_ada.shape[0] == 1

    cc = jnp.concatenate([c.astype(F32), c_ctx.astype(F32)[None, :],
                          jnp.zeros((16 - B - 1, D), F32)], axis=0)
    mod = _ada(cc, w_ada[0], b_ada[0][None, :])
    mod = mod.reshape(16, 6, 1, D).transpose(1, 0, 2, 3)
    sh1, sc1, g1, sh2, sc2, g2 = (mod[i] for i in range(6))

    w_in_bf = w_in[0].astype(BF16)
    w_out_bf = w_out[0].astype(BF16)
    nw_mix = norm_mix_w[0][None, :]
    cos_t, sin_t = _rope_tables(N)

    pb, lf = _inproj(x, sc1, sh1, None, nw_mix, w_in_bf, hg_lb[:2], cos_t, sin_t,
                     rope=True, tm=1024)
    pbc, lfc = _inproj(ctx, sc1, sh1, B, nw_mix, w_in_bf, hg_lb[:2], cos_t[:C], sin_t[:C],
                       rope=False, tm=C)

    a_ret = _retention(pb, pbc)
    a_hg = _hgrn(pb, lf, pbc, lfc, hg_norm_w[0][None, :])

    rw = jnp.zeros((D, LANES), F32).at[:, :N_EXPERTS].set(router_w[0])
    rb = jnp.full((1, LANES), -1e30, F32).at[0, :N_EXPERTS].set(router_b[0])
    h1, v, route, route_t, cnt = _outproj(a_ret, a_hg, w_out_bf, x, g1, sc2, sh2,
                                          norm_ffn_w[0][None, :], rw, rb, tm=512)

    idx = route_t[0:TOP_K].astype(I32)
    rank = route_t[2 * TOP_K:3 * TOP_K].astype(I32)
    counts = cnt[0, :N_EXPERTS].astype(I32)
    padded = (counts + ROW_BLOCK - 1) // ROW_BLOCK * ROW_BLOCK
    pend = jnp.cumsum(padded)
    pstart = pend - padded
    onehot = idx[None] == jnp.arange(N_EXPERTS, dtype=I32)[:, None, None]
    row_start = jnp.sum(jnp.where(onehot, pstart[:, None, None], 0), axis=0)
    dest = (row_start + rank) * ROW_TILES

    def dest_tiles(tile):
        return dest.reshape(TOP_K, T // tile, tile).transpose(1, 0, 2).reshape(T // tile,
                                                                               TOP_K * tile)
    n_blocks = (T * TOP_K) // ROW_BLOCK + N_EXPERTS
    n_used = (pend[-1] // ROW_BLOCK).astype(I32)
    starts = jnp.arange(n_blocks, dtype=I32) * ROW_BLOCK
    block_e = jnp.minimum(jnp.sum((pend[None, :] <= starts[:, None]).astype(I32), axis=1),
                          N_EXPERTS - 1)
    valid = jnp.clip(counts[block_e] - (starts - pstart[block_e]), 0, ROW_BLOCK)
    nsub = (valid + SUB_ROWS - 1) // SUB_ROWS
    first_from = lax.cummin(jnp.where(counts > 0, jnp.arange(N_EXPERTS, dtype=I32), N_EXPERTS),
                            reverse=True)
    next_e = jnp.concatenate([first_from[1:], jnp.full((1,), N_EXPERTS, I32)])
    next_e = jnp.where(next_e == N_EXPERTS, -1, next_e)
    meta = jnp.concatenate([pstart.astype(I32), counts])

    tile_d = 512
    xs = _dispatch(meta, dest_tiles(tile_d), v, tile=tile_d, n_blocks=n_blocks)
    yb = _moe(block_e, nsub, next_e, n_used[None], xs, w1[0], b1[0][:, None, :], w2[0],
              b2[0][:, None, :], n_blocks=n_blocks)
    tile_c = 256
    out = _combine(dest_tiles(tile_c), yb, route, h1, g2,
                   norm_final_w[None, :], tile=tile_c, tiles_per_batch=N // tile_c)
    return out.reshape(B, N, D)
```

```python
import functools
import math

import jax
import jax.numpy as jnp
import numpy as np
from jax import lax
from jax.experimental import pallas as pl
from jax.experimental.pallas import tpu as pltpu

F32 = jnp.float32
BF16 = jnp.bfloat16
I32 = jnp.int32

D_MODEL = 1024
GRID_W = 64
RET_HEADS = 4
RET_DK = 64
HG_HEADS = 4
PROJ_W = 4096
ROPE_BASE = 10000.0
EPS = 1e-6
N_EXPERTS = 32
TOP_K = 4
D_FF = 1024
SWIGLU_LIMIT = 7.0
SWIGLU_ALPHA = 1.702

LANES = 128
CHUNK = 128
ROW_BLOCK = 1024
SUB_ROWS = 256
VMEM_LIMIT = 56 * 1024 * 1024
ROW_TILES = D_MODEL // LANES

C_RQ, C_RK, C_RV, C_RG, C_HQ, C_FF, C_FB, C_HV, C_HG = (
    0, 256, 512, 1024, 1536, 2048, 2560, 3072, 3584)


def _cparams(sem):
    return pltpu.CompilerParams(dimension_semantics=sem, vmem_limit_bytes=VMEM_LIMIT)


def _split_bf16(x):
    hi = x.astype(BF16)
    lo = (x - hi.astype(F32)).astype(BF16)
    return hi, lo


def _dot(a, b):
    return jnp.dot(a, b, preferred_element_type=F32)


def _dot_nt(a, b):
    return lax.dot_general(a, b, (((1,), (1,)), ((), ())), preferred_element_type=F32)


def _dot3(a, b):
    ah, al = _split_bf16(a)
    bh, bl = _split_bf16(b)
    return _dot(ah, bh) + (_dot(ah, bl) + _dot(al, bh))


def _silu(x):
    return x * jax.nn.sigmoid(x)


def _load_tile_rows(ref, n):
    return jnp.concatenate([ref[pl.ds(s, n, stride=ROW_TILES), :] for s in range(ROW_TILES)],
                           axis=1)


def _store_tile_rows(ref, x):
    n = x.shape[0]
    for s in range(ROW_TILES):
        ref[pl.ds(s, n, stride=ROW_TILES), :] = x[:, s * LANES:(s + 1) * LANES]


def _ada_kernel(c_ref, w_ref, b_ref, o_ref):
    s = _silu(c_ref[...])
    o_ref[...] = _dot3(s, w_ref[...]) + b_ref[...]


def _ada(cc, w, b):
    nblk = w.shape[1] // D_MODEL
    return pl.pallas_call(
        _ada_kernel,
        out_shape=jax.ShapeDtypeStruct((cc.shape[0], w.shape[1]), F32),
        grid=(nblk,),
        in_specs=[pl.BlockSpec(cc.shape, lambda j: (0, 0)),
                  pl.BlockSpec((D_MODEL, D_MODEL), lambda j: (0, j)),
                  pl.BlockSpec((1, D_MODEL), lambda j: (0, j))],
        out_specs=pl.BlockSpec((cc.shape[0], D_MODEL), lambda j: (0, j)),
        compiler_params=_cparams(("arbitrary",)),
        name="ada",
    )(cc, w, b)


def _inproj_kernel(x_ref, sc_ref, sh_ref, nw_ref, w_ref, lb_ref, cos_ref, sin_ref,
                   pb_ref, lf_ref, u_scr, *, rope):
    x = x_ref[...]
    ms = jnp.mean(x * x, axis=-1, keepdims=True)
    u = x * lax.rsqrt(ms + EPS) * nw_ref[...] * (1.0 + sc_ref[...]) + sh_ref[...]
    u_scr[...] = u.astype(BF16)

    def proj(lo, width):
        return _dot(u_scr[...], w_ref[:, lo:lo + width])

    tm = x.shape[0]
    if rope:
        lane = lax.broadcasted_iota(I32, (tm, LANES), 1)
        first = (lane & 32) == 0

    def put_rot(col, scale):
        for j in range(2):
            lo = col + j * LANES
            t = proj(lo, LANES)
            if scale != 1.0:
                t = t * scale
            if rope:
                tb = j * LANES
                rot = jnp.where(first, pltpu.roll(t, 96, axis=1), pltpu.roll(t, 32, axis=1))
                t = t * cos_ref[:, tb:tb + LANES] + rot * sin_ref[:, tb:tb + LANES]
            pb_ref[:, lo:lo + LANES] = t.astype(BF16)

    put_rot(C_RQ, 1.0)
    put_rot(C_RK, RET_DK ** -0.5)
    pb_ref[:, C_RV:C_RV + 512] = proj(C_RV, 512).astype(BF16)
    pb_ref[:, C_RG:C_RG + 512] = _silu(proj(C_RG, 512)).astype(BF16)
    pb_ref[:, C_HQ:C_HQ + 512] = _silu(proj(C_HQ, 512)).astype(BF16)
    pb_ref[:, C_HV:C_HV + 512] = proj(C_HV, 512).astype(BF16)
    pb_ref[:, C_HG:C_HG + 512] = _silu(proj(C_HG, 512)).astype(BF16)

    la = lb_ref[0]
    lbb = lb_ref[1]
    mx = jnp.maximum(la, lbb)
    ea = jnp.exp(la - mx)
    eb = jnp.exp(lbb - mx)
    lb = ea / (ea + eb)
    for d, col in enumerate((C_FF, C_FB)):
        lbd = lb[d:d + 1, :]
        f = lbd + (1.0 - lbd) * jax.nn.sigmoid(proj(col, 512))
        pb_ref[:, col:col + 512] = (1.0 - f).astype(BF16)
        lf_ref[:, d * 512:(d + 1) * 512] = jnp.log(f)


def _inproj(x, sc, sh, mod_row, nw, w_bf, hg_lb, cos_t, sin_t, *, rope, tm):
    B, n, _ = x.shape
    nt = n // tm
    if mod_row is None:
        mrow = lambda b, j: (b, 0, 0)
    else:
        mrow = lambda b, j: (mod_row, 0, 0)
    return pl.pallas_call(
        functools.partial(_inproj_kernel, rope=rope),
        out_shape=(jax.ShapeDtypeStruct((B, n, PROJ_W), BF16),
                   jax.ShapeDtypeStruct((B, n, 1024), F32)),
        grid=(B, nt),
        in_specs=[pl.BlockSpec((None, tm, D_MODEL), lambda b, j: (b, j, 0)),
                  pl.BlockSpec((None, 1, D_MODEL), mrow),
                  pl.BlockSpec((None, 1, D_MODEL), mrow),
                  pl.BlockSpec((1, D_MODEL), lambda b, j: (0, 0)),
                  pl.BlockSpec((D_MODEL, PROJ_W), lambda b, j: (0, 0),
                               pipeline_mode=pl.Buffered(1)),
                  pl.BlockSpec((2, 2, 512), lambda b, j: (0, 0, 0)),
                  pl.BlockSpec((tm, 256), lambda b, j: (j, 0)),
                  pl.BlockSpec((tm, 256), lambda b, j: (j, 0))],
        out_specs=(pl.BlockSpec((None, tm, PROJ_W), lambda b, j: (b, j, 0)),
                   pl.BlockSpec((None, tm, 1024), lambda b, j: (b, j, 0))),
        scratch_shapes=[pltpu.VMEM((tm, D_MODEL), BF16)],
        compiler_params=_cparams(("arbitrary", "arbitrary")),
        name="inproj_rope" if rope else "inproj_ctx",
    )(x, sc, sh, nw, w_bf, hg_lb, cos_t, sin_t)


_RET_LGF = [math.log1p(-(2.0 ** (-5.0 - 2.0 * h))) for h in range(RET_HEADS)]
_RET_LGB = [math.log1p(-(2.0 ** (-6.0 - 2.0 * h))) for h in range(RET_HEADS)]


def _ret_kernel(q_ref, k_ref, v_ref, g_ref, kc_ref, vc_ref, o_ref,
                u_scr, sin_scr, kt_scr, dtot_scr, *, n_lat, n_ctx):
    L = CHUNK
    pair = pl.program_id(1)
    row = lax.broadcasted_iota(I32, (L, L), 0).astype(F32)
    col = lax.broadcasted_iota(I32, (L, L), 1).astype(F32)
    lane = lax.broadcasted_iota(I32, (L, LANES), 1)
    trow = lax.broadcasted_iota(I32, (L, 1), 0).astype(F32)
    tcol = lax.broadcasted_iota(I32, (1, L), 1).astype(F32)
    low_half = lane < RET_DK

    def u_chunk(k_blk, v_blk, ci, store_kt):
        kt = k_blk.astype(F32).T
        if store_kt is not None:
            kt_scr[store_kt] = kt.astype(BF16)
        for hh in range(2):
            lgf = jnp.where(pair == 0, _RET_LGF[hh], _RET_LGF[2 + hh])
            lgb = jnp.where(pair == 0, _RET_LGB[hh], _RET_LGB[2 + hh])
            kth = kt[hh * RET_DK:(hh + 1) * RET_DK, :]
            wkf = jnp.exp(lgf * (L - 1.0 - tcol))
            wkb = jnp.exp(lgb * tcol)
            lhs = jnp.concatenate([kth * wkf, kth * wkb], axis=0).astype(BF16)
            u_scr[hh, ci] = _dot(lhs, v_blk[:, hh * LANES:(hh + 1) * LANES])

    for c in range(n_ctx):
        u_chunk(kc_ref[c * L:(c + 1) * L, :], vc_ref[c * L:(c + 1) * L, :], c, None)

    def lat_u(c, carry):
        r0 = pl.multiple_of(c * L, L)
        u_chunk(k_ref[pl.ds(r0, L), :], v_ref[pl.ds(r0, L), :], n_ctx + c, c)
        return carry
    lax.fori_loop(0, n_lat, lat_u, 0, unroll=2)

    ones = jnp.ones((RET_DK, LANES), F32)
    for hh in range(2):
        lgf = jnp.where(pair == 0, _RET_LGF[hh], _RET_LGF[2 + hh])
        lgb = jnp.where(pair == 0, _RET_LGB[hh], _RET_LGB[2 + hh])
        d = row - col
        dtot_scr[hh] = jnp.where(d > 0, jnp.exp(lgf * jnp.maximum(d, 0.0)),
                                 jnp.where(d < 0, jnp.exp(lgb * jnp.maximum(-d, 0.0)), 2.0))
        af = jnp.exp(ones * (lgf * L))
        ab = jnp.exp(ones * (lgb * L))

        s = jnp.zeros((RET_DK, LANES), F32)
        for c in range(n_ctx):
            s = af * s + u_scr[hh, c, 0:RET_DK, :]
        sb = jnp.zeros((RET_DK, LANES), F32)
        for c in reversed(range(n_ctx)):
            sb = ab * sb + u_scr[hh, c, RET_DK:2 * RET_DK, :]

        def fwd(c, s, hh=hh, af=af):
            sin_scr[hh, c, 0:RET_DK, :] = s.astype(BF16)
            return af * s + u_scr[hh, n_ctx + c, 0:RET_DK, :]
        lax.fori_loop(0, n_lat, fwd, s)

        def bwd(i, sb, hh=hh, ab=ab):
            c = n_lat - 1 - i
            sin_scr[hh, c, RET_DK:2 * RET_DK, :] = sb.astype(BF16)
            return ab * sb + u_scr[hh, n_ctx + c, RET_DK:2 * RET_DK, :]
        lax.fori_loop(0, n_lat, bwd, sb)

    def out_chunk(c, carry):
        r0 = pl.multiple_of(c * L, L)
        q = q_ref[pl.ds(r0, L), :].astype(F32)
        qr = pltpu.roll(q, RET_DK, axis=1)
        kt = kt_scr[c]
        for hh in range(2):
            lgf = jnp.where(pair == 0, _RET_LGF[hh], _RET_LGF[2 + hh])
            lgb = jnp.where(pair == 0, _RET_LGB[hh], _RET_LGB[2 + hh])
            mine = low_half if hh == 0 else jnp.logical_not(low_half)
            qm = jnp.where(mine, q, 0.0).astype(BF16)
            p = (_dot(qm, kt) * dtot_scr[hh]).astype(BF16)
            vh = v_ref[pl.ds(r0, L), hh * LANES:(hh + 1) * LANES]
            wqf = jnp.exp(lgf * (trow + 1.0))
            wqb = jnp.exp(lgb * (L - trow))
            qa, qb = (q, qr) if hh == 0 else (qr, q)
            qs = jnp.where(low_half, qa * wqf, qb * wqb).astype(BF16)
            o = _dot(p, vh) + _dot(qs, sin_scr[hh, c])
            ms = jnp.mean(o * o, axis=-1, keepdims=True)
            gh = g_ref[pl.ds(r0, L), hh * LANES:(hh + 1) * LANES].astype(F32)
            o_ref[pl.ds(r0, L), hh * LANES:(hh + 1) * LANES] = (
                o * lax.rsqrt(ms + EPS) * gh).astype(BF16)
        return carry
    lax.fori_loop(0, n_lat, out_chunk, 0, unroll=2)


def _retention(pb, pbc):
    B, n, _ = pb.shape
    nc = pbc.shape[1]
    n_lat, n_ctx = n // CHUNK, nc // CHUNK
    return pl.pallas_call(
        functools.partial(_ret_kernel, n_lat=n_lat, n_ctx=n_ctx),
        out_shape=jax.ShapeDtypeStruct((B, n, 512), BF16),
        grid=(B, 2),
        in_specs=[pl.BlockSpec((None, n, LANES), lambda b, p: (b, 0, C_RQ // LANES + p)),
                  pl.BlockSpec((None, n, LANES), lambda b, p: (b, 0, C_RK // LANES + p)),
                  pl.BlockSpec((None, n, 256), lambda b, p: (b, 0, C_RV // 256 + p)),
                  pl.BlockSpec((None, n, 256), lambda b, p: (b, 0, C_RG // 256 + p)),
                  pl.BlockSpec((None, nc, LANES), lambda b, p: (b, 0, C_RK // LANES + p)),
                  pl.BlockSpec((None, nc, 256), lambda b, p: (b, 0, C_RV // 256 + p))],
        out_specs=pl.BlockSpec((None, n, 256), lambda b, p: (b, 0, p)),
        scratch_shapes=[pltpu.VMEM((2, n_lat + n_ctx, CHUNK, LANES), F32),
                        pltpu.VMEM((2, n_lat, CHUNK, LANES), BF16),
                        pltpu.VMEM((n_lat, LANES, CHUNK), BF16),
                        pltpu.VMEM((2, CHUNK, CHUNK), F32)],
        compiler_params=_cparams(("arbitrary", "arbitrary")),
        name="retention",
    )(pb, pb, pb, pb, pbc, pbc)


_LEVELS = (64, 32, 16, 8, 4, 2, 1)


def _expand_rows(r, rep):
    n = r.shape[0]
    if n == 1:
        return jnp.broadcast_to(r, (rep, r.shape[1]))
    return jnp.concatenate(
        [jnp.broadcast_to(r[i:i + 1, :], (rep, r.shape[1])) for i in range(n)], axis=0)


def _hgrn_kernel(q_ref, kf_ref, kb_ref, v_ref, g_ref, lff_ref, lfb_ref,
                 kfc_ref, kbc_ref, vc_ref, lffc_ref, lfbc_ref, nw_ref, o_ref,
                 ut_scr, a_scr, qs_scr, oi_scr, sin_scr, bfb_scr, *, n_lat, n_ctx):
    L = CHUNK
    row = lax.broadcasted_iota(I32, (L, L), 0)
    col = lax.broadcasted_iota(I32, (L, L), 1)
    xr_bits = lax.bitcast_convert_type((row ^ col).astype(F32), I32)
    lv = lax.shift_right_logical(xr_bits, 23) - 127
    row2 = lax.broadcasted_iota(I32, (L, 2 * L), 0)
    col2 = lax.broadcasted_iota(I32, (L, 2 * L), 1) & (L - 1)
    tril2 = jnp.where(col2 <= row2, 1.0, 0.0).astype(BF16)
    triu2 = jnp.where(col2 >= row2, 1.0, 0.0).astype(BF16)

    def cums(lff, lfb):
        hf, lof = _split_bf16(lff)
        hb, lob = _split_bf16(lfb)
        bf = _dot(tril2, jnp.concatenate([hf, lof], axis=0))
        bb = _dot(triu2, jnp.concatenate([hb, lob], axis=0))
        return bf, bb

    def state_part(ci, kf, kb, v_blk, bf, bb):
        endf = bf[L - 1:L, :]
        endb = bb[0:1, :]
        ksf = kf * jnp.exp(endf - bf)
        ksb = kb * jnp.exp(endb - bb)
        vt = v_blk.astype(F32).T.astype(BF16)
        ut_scr[ci] = _dot(vt, jnp.concatenate([ksf, ksb], axis=1).astype(BF16))
        a_scr[ci] = jnp.broadcast_to(
            jnp.concatenate([jnp.exp(endf), jnp.exp(endb)], axis=1), (8, 2 * LANES))

    for c in range(n_ctx):
        sl = slice(c * L, (c + 1) * L)
        bf, bb = cums(lffc_ref[sl, :], lfbc_ref[sl, :])
        state_part(c, kfc_ref[sl, :].astype(F32), kbc_ref[sl, :].astype(F32),
                   vc_ref[sl, :], bf, bb)

    def lat_chunk(c, u):
        rows = pl.ds(pl.multiple_of(c * L, L), L)
        bf_scr = bfb_scr.at[u, 0]
        bb_scr = bfb_scr.at[u, 1]
        q = q_ref[rows, :].astype(F32)
        kf = kf_ref[rows, :].astype(F32)
        kb = kb_ref[rows, :].astype(F32)
        v_blk = v_ref[rows, :]
        lff = lff_ref[rows, :]
        lfb = lfb_ref[rows, :]
        bf, bb = cums(lff, lfb)
        state_part(n_ctx + c, kf, kb, v_blk, bf, bb)
        qs_scr[rows, :] = jnp.concatenate([q * jnp.exp(bf), q * jnp.exp(bb)],
                                          axis=1).astype(BF16)
        bf_scr[...] = bf
        bb_scr[...] = bb

        acc = jnp.zeros((L, L), F32)
        for lvl, h in enumerate(_LEVELS):
            bit = (row & h) != 0
            ksel = jnp.where(bit, kb, kf)
            if h >= 4:
                n = (L // 2) // h
                if n == 1:
                    rf = bf_scr[h - 1:h, :]
                    rb = bb_scr[h:h + 1, :]
                else:
                    rf = bf_scr[pl.ds(h - 1, n, stride=2 * h), :]
                    rb = bb_scr[pl.ds(h, n, stride=2 * h), :]
                df = bf - _expand_rows(rf, 2 * h)
                db = bb - _expand_rows(rb, 2 * h)
                eq = jnp.where(bit, df, db)
                ek = -jnp.where(bit, db, df)
            elif h == 2:
                m = row & 3
                lff_n = pltpu.roll(lff, L - 1, axis=0)
                lfb_n = pltpu.roll(lfb, L - 1, axis=0)
                eq = jnp.where(m == 2, lff,
                               jnp.where(m == 3, lff + pltpu.roll(lff, 1, axis=0),
                                         jnp.where(m == 0, lfb + lfb_n, lfb)))
                ek = jnp.where(m == 3, pltpu.roll(lfb, 1, axis=0),
                               jnp.where(m == 0, lff_n, 0.0))
            else:
                eq = jnp.where(bit, lff, lfb)
                ek = None
            lhs = (q * jnp.exp(eq)).astype(BF16)
            rhs = (ksel if ek is None else ksel * jnp.exp(ek)).astype(BF16)
            acc = jnp.where(lv == 6 - lvl, _dot_nt(lhs, rhs), acc)

        dsum = jnp.sum(q * (kf + kb), axis=-1, keepdims=True)
        oi_scr[rows, :] = _dot(acc.astype(BF16), v_blk) + dsum * v_blk.astype(F32)

    def lat_pair(i, carry):
        for u in range(4):
            lat_chunk(4 * i + u, u)
        return carry
    lax.fori_loop(0, n_lat // 4, lat_pair, 0)

    st = jnp.zeros((LANES, LANES), F32)
    for c in range(n_ctx):
        st = st * a_scr[c, 0:1, 0:LANES] + ut_scr[c, :, 0:LANES]
    stb = jnp.zeros((LANES, LANES), F32)
    for c in reversed(range(n_ctx)):
        stb = stb * a_scr[c, 0:1, LANES:2 * LANES] + ut_scr[c, :, LANES:2 * LANES]

    def fwd(c, st):
        sin_scr[c, :, 0:LANES] = st.astype(BF16)
        ci = n_ctx + c
        return st * a_scr[ci, 0:1, 0:LANES] + ut_scr[ci, :, 0:LANES]
    lax.fori_loop(0, n_lat, fwd, st)

    def bwd(i, stb):
        c = n_lat - 1 - i
        sin_scr[c, :, LANES:2 * LANES] = stb.astype(BF16)
        ci = n_ctx + c
        return stb * a_scr[ci, 0:1, LANES:2 * LANES] + ut_scr[ci, :, LANES:2 * LANES]
    lax.fori_loop(0, n_lat, bwd, stb)

    def out_chunk(c, carry):
        rows = pl.ds(pl.multiple_of(c * L, L), L)
        o = oi_scr[rows, :] + _dot_nt(qs_scr[rows, :], sin_scr[c])
        ms = jnp.mean(o * o, axis=-1, keepdims=True)
        y = o * lax.rsqrt(ms + EPS) * nw_ref[...] * g_ref[rows, :].astype(F32)
        o_ref[rows, :] = y.astype(BF16)
        return carry
    lax.fori_loop(0, n_lat, out_chunk, 0, unroll=4)


def _hgrn(pb, lf, pbc, lfc, nw):
    B, n, _ = pb.shape
    nc = pbc.shape[1]
    n_lat, n_ctx = n // CHUNK, nc // CHUNK

    def colblk(rows, col0):
        return pl.BlockSpec((None, rows, LANES), lambda b, h: (b, 0, col0 // LANES + h))

    return pl.pallas_call(
        functools.partial(_hgrn_kernel, n_lat=n_lat, n_ctx=n_ctx),
        out_shape=jax.ShapeDtypeStruct((B, n, 512), BF16),
        grid=(B, HG_HEADS),
        in_specs=[colblk(n, C_HQ), colblk(n, C_FF), colblk(n, C_FB), colblk(n, C_HV),
                  colblk(n, C_HG), colblk(n, 0), colblk(n, 512),
                  colblk(nc, C_FF), colblk(nc, C_FB), colblk(nc, C_HV),
                  colblk(nc, 0), colblk(nc, 512),
                  pl.BlockSpec((1, LANES), lambda b, h: (0, 0))],
        out_specs=pl.BlockSpec((None, n, LANES), lambda b, h: (b, 0, h)),
        scratch_shapes=[pltpu.VMEM((n_lat + n_ctx, LANES, 2 * LANES), F32),
                        pltpu.VMEM((n_lat + n_ctx, 8, 2 * LANES), F32),
                        pltpu.VMEM((n, 2 * LANES), BF16),
                        pltpu.VMEM((n, LANES), F32),
                        pltpu.VMEM((n_lat, LANES, 2 * LANES), BF16),
                        pltpu.VMEM((4, 2, CHUNK, LANES), F32)],
        compiler_params=_cparams(("arbitrary", "arbitrary")),
        name="hgrn2",
    )(pb, pb, pb, pb, pb, lf, lf, pbc, pbc, pbc, lfc, lfc, nw)


def _outproj_kernel(ar_ref, ah_ref, w_ref, x_ref, g1_ref, sc_ref, sh_ref, nw_ref,
                    rw_ref, rb_ref, h1_ref, v_ref, route_ref, route_t_ref, cnt_ref, cnt_scr):
    first_step = jnp.logical_and(pl.program_id(0) == 0, pl.program_id(1) == 0)

    @pl.when(first_step)
    def _():
        cnt_scr[...] = jnp.zeros_like(cnt_scr)

    y = _dot(ar_ref[...], w_ref[0:512, :]) + _dot(ah_ref[...], w_ref[512:1024, :])
    h1 = x_ref[...] + g1_ref[...] * y
    h1_ref[...] = h1
    ms = jnp.mean(h1 * h1, axis=-1, keepdims=True)
    v = h1 * lax.rsqrt(ms + EPS) * nw_ref[...] * (1.0 + sc_ref[...]) + sh_ref[...]
    _store_tile_rows(v_ref, v)

    tm = v.shape[0]
    logits = _dot3(v, rw_ref[...]) + rb_ref[...]
    lane = lax.broadcasted_iota(I32, (tm, LANES), 1)
    lane_f = lane.astype(F32)
    l = logits
    sels, tops, idxs = [], [], []
    for _ in range(TOP_K):
        m = jnp.max(l, axis=1, keepdims=True)
        i = jnp.min(jnp.where(l == m, lane_f, float(LANES)), axis=1, keepdims=True)
        sel = lane_f == i
        l = jnp.where(sel, -jnp.inf, l)
        sels.append(sel)
        tops.append(m)
        idxs.append(i)
    es = [jnp.exp(t - tops[0]) for t in tops]
    den = es[0] + es[1] + es[2] + es[3]
    gates = [e / den for e in es]

    oh = jnp.zeros((tm, LANES), F32)
    for sel in sels:
        oh = jnp.where(sel, 1.0, oh)
    r = lax.broadcasted_iota(I32, (tm, tm), 0)
    c = lax.broadcasted_iota(I32, (tm, tm), 1)
    tri = jnp.where(c < r, 1.0, 0.0).astype(BF16)
    before = _dot(tri, oh.astype(BF16)) + cnt_scr[0:1, :]
    ranks = [jnp.sum(jnp.where(sel, before, 0.0), axis=1, keepdims=True) for sel in sels]
    cnt_scr[...] = cnt_scr[...] + jnp.sum(oh, axis=0, keepdims=True)
    cnt_ref[...] = cnt_scr[...]

    out = jnp.zeros((tm, LANES), F32)
    for k in range(TOP_K):
        out = jnp.where(lane == k, idxs[k], out)
        out = jnp.where(lane == TOP_K + k, gates[k], out)
        out = jnp.where(lane == 2 * TOP_K + k, ranks[k], out)
    route_ref[...] = out
    route_t_ref[...] = out.T[0:16, :]


def _outproj(a_ret, a_hg, w_bf, x, g1, sc2, sh2, nw, rw, rb, *, tm):
    B, n, _ = x.shape
    nt = n // tm
    T = B * n
    mrow = lambda b, j: (b, 0, 0)
    tok = lambda b, j: (b * nt + j, 0)
    return pl.pallas_call(
        _outproj_kernel,
        out_shape=(jax.ShapeDtypeStruct((T, D_MODEL), F32),
                   jax.ShapeDtypeStruct((T * ROW_TILES, LANES), F32),
                   jax.ShapeDtypeStruct((T, LANES), F32),
                   jax.ShapeDtypeStruct((16, T), F32),
                   jax.ShapeDtypeStruct((8, LANES), F32)),
        grid=(B, nt),
        in_specs=[pl.BlockSpec((None, tm, 512), lambda b, j: (b, j, 0)),
                  pl.BlockSpec((None, tm, 512), lambda b, j: (b, j, 0)),
                  pl.BlockSpec((D_MODEL, D_MODEL), lambda b, j: (0, 0)),
                  pl.BlockSpec((None, tm, D_MODEL), lambda b, j: (b, j, 0)),
                  pl.BlockSpec((None, 1, D_MODEL), mrow),
                  pl.BlockSpec((None, 1, D_MODEL), mrow),
                  pl.BlockSpec((None, 1, D_MODEL), mrow),
                  pl.BlockSpec((1, D_MODEL), lambda b, j: (0, 0)),
                  pl.BlockSpec((D_MODEL, LANES), lambda b, j: (0, 0)),
                  pl.BlockSpec((1, LANES), lambda b, j: (0, 0))],
        out_specs=(pl.BlockSpec((tm, D_MODEL), tok),
                   pl.BlockSpec((tm * ROW_TILES, LANES), tok),
                   pl.BlockSpec((tm, LANES), tok),
                   pl.BlockSpec((16, tm), lambda b, j: (0, b * nt + j)),
                   pl.BlockSpec((8, LANES), lambda b, j: (0, 0))),
        scratch_shapes=[pltpu.VMEM((8, LANES), F32)],
        compiler_params=_cparams(("arbitrary", "arbitrary")),
        name="outproj_router",
    )(a_ret, a_hg, w_bf, x, g1, sc2, sh2, nw, rw, rb)


def _dispatch_kernel(dest_hbm, v_ref, xs_hbm, idx_smem, zbuf, sem_idx, sem_z, sem_rows,
                     *, tile, n_tiles, n_rows):
    i = pl.program_id(0)
    slot = lax.rem(i, 2)

    def idx_copy(j, s):
        n_asg = tile * TOP_K
        return pltpu.make_async_copy(dest_hbm.at[j], idx_smem.at[pl.ds(s * n_asg, n_asg)],
                                     sem_idx.at[s])

    def zero_copy(j):
        r0 = (n_rows + j * SUB_ROWS) * ROW_TILES
        return pltpu.make_async_copy(zbuf, xs_hbm.at[pl.ds(r0, SUB_ROWS * ROW_TILES), :], sem_z)

    @pl.when(i == 0)
    def _():
        idx_copy(0, 0).start()
        zbuf[...] = jnp.zeros_like(zbuf)
        for j in range(ROW_BLOCK // SUB_ROWS):
            zero_copy(j).start()
        for j in range(ROW_BLOCK // SUB_ROWS):
            zero_copy(j).wait()

    idx_copy(i, slot).wait()

    @pl.when(i + 1 < n_tiles)
    def _():
        idx_copy(i + 1, 1 - slot).start()

    def issue(t, carry):
        for k in range(TOP_K):
            d = pl.multiple_of(idx_smem[slot * (tile * TOP_K) + k * tile + t], ROW_TILES)
            src = v_ref.at[pl.ds(pl.multiple_of(t * ROW_TILES, ROW_TILES), ROW_TILES), :]
            pltpu.make_async_copy(src, xs_hbm.at[pl.ds(d, ROW_TILES), :],
                                  sem_rows).start(priority=k % 2)
        return carry
    lax.fori_loop(0, tile, issue, 0, unroll=4)
    for _ in range(TOP_K):
        pltpu.make_async_copy(v_ref, xs_hbm.at[pl.ds(0, tile * ROW_TILES), :], sem_rows).wait()


def _dispatch(dest_tiles, v, *, tile):
    T = v.shape[0] // ROW_TILES
    n_rows = T * TOP_K
    return pl.pallas_call(
        functools.partial(_dispatch_kernel, tile=tile, n_tiles=T // tile, n_rows=n_rows),
        out_shape=jax.ShapeDtypeStruct(((n_rows + ROW_BLOCK) * ROW_TILES, LANES), F32),
        grid=(T // tile,),
        in_specs=[pl.BlockSpec(memory_space=pl.ANY),
                  pl.BlockSpec((tile * ROW_TILES, LANES), lambda i: (i, 0))],
        out_specs=pl.BlockSpec(memory_space=pl.ANY),
        scratch_shapes=[pltpu.SMEM((2 * tile * TOP_K,), I32),
                        pltpu.VMEM((SUB_ROWS * ROW_TILES, LANES), F32),
                        pltpu.SemaphoreType.DMA((2,)),
                        pltpu.SemaphoreType.DMA,
                        pltpu.SemaphoreType.DMA],
        compiler_params=_cparams(("arbitrary",)),
        name="dispatch",
    )(dest_tiles, v)


def _moe_kernel(be_ref, ns_ref, nx_ref, xr_ref, x_hbm, w1_hbm, b1_ref, w2_hbm, b2_ref, y_ref,
                w1_stage, w2_stage, w1_scr, w2_scr, xbuf, x_scr, act_scr, sem_w, sem_x,
                *, n_blocks):
    i = pl.program_id(0)
    slot = lax.rem(i, 2)
    e = be_ref[i]
    nsub = ns_ref[i]
    changed = jnp.logical_or(i == 0, e != be_ref[jnp.maximum(i - 1, 0)])

    def x_copy(j, s):
        r0 = pl.multiple_of(xr_ref[j], ROW_TILES)
        return pltpu.make_async_copy(x_hbm.at[pl.ds(r0, ROW_BLOCK * ROW_TILES), :], xbuf.at[s],
                                     sem_x.at[s])

    @pl.when(i == 0)
    def _():
        x_copy(0, 0).start()

    @pl.when(nsub > 0)
    def _():
        x_copy(i, slot).wait()

    nxt_blk = jnp.minimum(i + 1, n_blocks - 1)

    @pl.when(jnp.logical_and(i + 1 < n_blocks, ns_ref[nxt_blk] > 0))
    def _():
        x_copy(nxt_blk, 1 - slot).start()

    def weight_copies(ex):
        return (pltpu.make_async_copy(w1_hbm.at[ex], w1_stage, sem_w.at[0]),
                pltpu.make_async_copy(w2_hbm.at[ex], w2_stage, sem_w.at[1]))

    @pl.when(i == 0)
    def _():
        for cp in weight_copies(e):
            cp.start()

    @pl.when(jnp.logical_and(changed, nsub > 0))
    def _():
        for cp in weight_copies(e):
            cp.wait()
        w1_scr[...] = w1_stage[...].astype(BF16)
        w2_scr[...] = w2_stage[...].astype(BF16)
        nxt = nx_ref[e]

        @pl.when(nxt >= 0)
        def _():
            for cp in weight_copies(nxt):
                cp.start()

    def compute(rows):
        x_scr[0:rows, :] = _load_tile_rows(xbuf.at[slot], rows).astype(BF16)
        cw = 256
        for c in range(D_FF // cw):
            glu = (_dot(x_scr[0:rows, :], w1_scr[:, c * cw:(c + 1) * cw])
                   + b1_ref[:, c * cw:(c + 1) * cw])
            lin = (_dot(x_scr[0:rows, :], w1_scr[:, D_FF + c * cw:D_FF + (c + 1) * cw])
                   + b1_ref[:, D_FF + c * cw:D_FF + (c + 1) * cw])
            glu = jnp.minimum(glu, SWIGLU_LIMIT)
            lin = jnp.clip(lin, -SWIGLU_LIMIT, SWIGLU_LIMIT)
            act = glu * jax.nn.sigmoid(SWIGLU_ALPHA * glu) * (lin + 1.0)
            act_scr[0:rows, c * cw:(c + 1) * cw] = act.astype(BF16)
        for c in range(D_MODEL // cw):
            y = (_dot(act_scr[0:rows, :], w2_scr[:, c * cw:(c + 1) * cw])
                 + b2_ref[:, c * cw:(c + 1) * cw])
            for s in range(cw // LANES):
                t = c * (cw // LANES) + s
                y_ref[pl.ds(t, rows, stride=ROW_TILES), :] = y[:, s * LANES:(s + 1) * LANES]
        if rows < ROW_BLOCK:
            y_ref[rows * ROW_TILES:, :] = jnp.zeros(((ROW_BLOCK - rows) * ROW_TILES, LANES), F32)

    for m in range(1, ROW_BLOCK // SUB_ROWS + 1):
        @pl.when(nsub == m)
        def _(m=m):
            compute(m * SUB_ROWS)

    @pl.when(nsub == 0)
    def _():
        y_ref[...] = jnp.zeros_like(y_ref)


def _moe(block_e, nsub, next_e, xrow, xs, w1, b1, w2, b2, *, n_blocks):
    return pl.pallas_call(
        functools.partial(_moe_kernel, n_blocks=n_blocks),
        out_shape=jax.ShapeDtypeStruct((n_blocks * ROW_BLOCK * ROW_TILES, LANES), F32),
        grid_spec=pltpu.PrefetchScalarGridSpec(
            num_scalar_prefetch=4,
            grid=(n_blocks,),
            in_specs=[pl.BlockSpec(memory_space=pl.ANY),
                      pl.BlockSpec(memory_space=pl.ANY),
                      pl.BlockSpec((None, 1, 2 * D_FF), lambda i, be, ns, nx, xr: (be[i], 0, 0)),
                      pl.BlockSpec(memory_space=pl.ANY),
                      pl.BlockSpec((None, 1, D_MODEL), lambda i, be, ns, nx, xr: (be[i], 0, 0))],
            out_specs=pl.BlockSpec((ROW_BLOCK * ROW_TILES, LANES),
                                   lambda i, be, ns, nx, xr: (i, 0)),
            scratch_shapes=[pltpu.VMEM((D_MODEL, 2 * D_FF), F32),
                            pltpu.VMEM((D_FF, D_MODEL), F32),
                            pltpu.VMEM((D_MODEL, 2 * D_FF), BF16),
                            pltpu.VMEM((D_FF, D_MODEL), BF16),
                            pltpu.VMEM((2, ROW_BLOCK * ROW_TILES, LANES), F32),
                            pltpu.VMEM((ROW_BLOCK, D_MODEL), BF16),
                            pltpu.VMEM((ROW_BLOCK, D_FF), BF16),
                            pltpu.SemaphoreType.DMA((2,)),
                            pltpu.SemaphoreType.DMA((2,))]),
        compiler_params=_cparams(("arbitrary",)),
        name="moe_ffn",
    )(block_e, nsub, next_e, xrow, xs, w1, b1, w2, b2)


def _combine_kernel(dest_hbm, yb_hbm, route_ref, h1_ref, g2_ref, nw_ref, o_ref,
                    idx_smem, rows, sem_idx, sem_rows, *, tile, n_tiles):
    i = pl.program_id(0)
    slot = lax.rem(i, 2)

    def idx_copy(j, s):
        n_asg = tile * TOP_K
        return pltpu.make_async_copy(dest_hbm.at[j], idx_smem.at[pl.ds(s * n_asg, n_asg)],
                                     sem_idx.at[s])

    def issue_rows(s):
        def issue(t, carry):
            for k in range(TOP_K):
                d = pl.multiple_of(idx_smem[s * (tile * TOP_K) + k * tile + t], ROW_TILES)
                dst = rows.at[s, k, pl.ds(pl.multiple_of(t * ROW_TILES, ROW_TILES), ROW_TILES), :]
                pltpu.make_async_copy(yb_hbm.at[pl.ds(d, ROW_TILES), :], dst,
                                      sem_rows.at[s]).start(priority=k % 2)
            return carry
        lax.fori_loop(0, tile, issue, 0, unroll=4)

    @pl.when(i == 0)
    def _():
        idx_copy(0, 0).start()
        idx_copy(0, 0).wait()
        issue_rows(0)

        if n_tiles > 1:
            idx_copy(1, 1).start()

    @pl.when(i + 1 < n_tiles)
    def _():
        idx_copy(i + 1, 1 - slot).wait()
        issue_rows(1 - slot)

        @pl.when(i + 2 < n_tiles)
        def _():
            idx_copy(i + 2, slot).start()

    for k in range(TOP_K):
        pltpu.make_async_copy(yb_hbm.at[pl.ds(0, tile * ROW_TILES), :], rows.at[slot, k],
                              sem_rows.at[slot]).wait()

    route = route_ref[...]
    y = jnp.zeros((tile, D_MODEL), F32)
    for k in range(TOP_K):
        y = y + route[:, TOP_K + k:TOP_K + k + 1] * _load_tile_rows(rows.at[slot, k], tile)
    h = h1_ref[...] + g2_ref[...] * y
    ms = jnp.mean(h * h, axis=-1, keepdims=True)
    o_ref[...] = h * lax.rsqrt(ms + EPS) * nw_ref[...]


def _combine(dest_tiles, yb, route, h1, g2, nw, *, tile, tiles_per_batch):
    T = h1.shape[0]
    return pl.pallas_call(
        functools.partial(_combine_kernel, tile=tile, n_tiles=T // tile),
        out_shape=jax.ShapeDtypeStruct((T, D_MODEL), F32),
        grid=(T // tile,),
        in_specs=[pl.BlockSpec(memory_space=pl.ANY),
                  pl.BlockSpec(memory_space=pl.ANY),
                  pl.BlockSpec((tile, LANES), lambda i: (i, 0)),
                  pl.BlockSpec((tile, D_MODEL), lambda i: (i, 0)),
                  pl.BlockSpec((None, 1, D_MODEL), lambda i: (i // tiles_per_batch, 0, 0)),
                  pl.BlockSpec((1, D_MODEL), lambda i: (0, 0))],
        out_specs=pl.BlockSpec((tile, D_MODEL), lambda i: (i, 0)),
        scratch_shapes=[pltpu.SMEM((2 * tile * TOP_K,), I32),
                        pltpu.VMEM((2, TOP_K, tile * ROW_TILES, LANES), F32),
                        pltpu.SemaphoreType.DMA((2,)),
                        pltpu.SemaphoreType.DMA((2,))],
        compiler_params=_cparams(("arbitrary",)),
        name="combine_norm",
    )(dest_tiles, yb, route, h1, g2, nw)


def _rope_tables(n):
    rows = n // GRID_W
    row = jnp.repeat(jnp.arange(rows, dtype=F32), GRID_W)
    col = jnp.tile(jnp.arange(GRID_W, dtype=F32), rows)
    n_freq = RET_DK // 4
    inv = ROPE_BASE ** (-jnp.arange(n_freq, dtype=F32) / n_freq)
    ang = jnp.concatenate([row[:, None] * inv, col[:, None] * inv], axis=-1)
    cos, sin = jnp.cos(ang), jnp.sin(ang)
    cos_h = jnp.concatenate([cos, cos], axis=-1)
    sin_h = jnp.concatenate([-sin, sin], axis=-1)
    return jnp.tile(cos_h, (1, RET_HEADS)), jnp.tile(sin_h, (1, RET_HEADS))


def kernel(x, c, ctx, c_ctx, w_ada, b_ada, norm_mix_w, norm_ffn_w, w_in, w_out, hg_lb,
           hg_norm_w, router_w, router_b, w1, b1, w2, b2, norm_final_w):
    B, N, D = x.shape
    C = ctx.shape[1]
    T = B * N
    assert D == D_MODEL and w_ada.shape[0] == 1

    cc = jnp.concatenate([c.astype(F32), c_ctx.astype(F32)[None, :],
                          jnp.zeros((16 - B - 1, D), F32)], axis=0)
    mod = _ada(cc, w_ada[0], b_ada[0][None, :])
    mod = mod.reshape(16, 6, 1, D).transpose(1, 0, 2, 3)
    sh1, sc1, g1, sh2, sc2, g2 = (mod[i] for i in range(6))

    w_in_bf = w_in[0].astype(BF16)
    w_out_bf = w_out[0].astype(BF16)
    nw_mix = norm_mix_w[0][None, :]
    cos_t, sin_t = _rope_tables(N)

    pb, lf = _inproj(x, sc1, sh1, None, nw_mix, w_in_bf, hg_lb[:2], cos_t, sin_t,
                     rope=True, tm=1024)
    pbc, lfc = _inproj(ctx, sc1, sh1, B, nw_mix, w_in_bf, hg_lb[:2], cos_t[:C], sin_t[:C],
                       rope=False, tm=C)

    a_ret = _retention(pb, pbc)
    a_hg = _hgrn(pb, lf, pbc, lfc, hg_norm_w[0][None, :])

    rw = jnp.zeros((D, LANES), F32).at[:, :N_EXPERTS].set(router_w[0])
    rb = jnp.full((1, LANES), -1e30, F32).at[0, :N_EXPERTS].set(router_b[0])
    h1, v, route, route_t, cnt = _outproj(a_ret, a_hg, w_out_bf, x, g1, sc2, sh2,
                                          norm_ffn_w[0][None, :], rw, rb, tm=512)

    idx = route_t[0:TOP_K].astype(I32)
    rank = route_t[2 * TOP_K:3 * TOP_K].astype(I32)
    counts = cnt[0, :N_EXPERTS].astype(I32)
    padded = (counts + ROW_BLOCK - 1) // ROW_BLOCK * ROW_BLOCK
    pend = jnp.cumsum(padded)
    pstart = pend - padded
    cstart = jnp.cumsum(counts) - counts
    onehot = idx[None] == jnp.arange(N_EXPERTS, dtype=I32)[:, None, None]

    def dest_tiles(first_row, tile):
        dest = (jnp.sum(jnp.where(onehot, first_row[:, None, None], 0), axis=0) + rank) * ROW_TILES
        return dest.reshape(TOP_K, T // tile, tile).transpose(1, 0, 2).reshape(T // tile,
                                                                               TOP_K * tile)
    n_blocks = (T * TOP_K) // ROW_BLOCK + N_EXPERTS
    starts = jnp.arange(n_blocks, dtype=I32) * ROW_BLOCK
    block_e = jnp.minimum(jnp.sum((pend[None, :] <= starts[:, None]).astype(I32), axis=1),
                          N_EXPERTS - 1)
    valid = jnp.clip(counts[block_e] - (starts - pstart[block_e]), 0, ROW_BLOCK)
    nsub = (valid + SUB_ROWS - 1) // SUB_ROWS
    first_from = lax.cummin(jnp.where(counts > 0, jnp.arange(N_EXPERTS, dtype=I32), N_EXPERTS),
                            reverse=True)
    next_e = jnp.concatenate([first_from[1:], jnp.full((1,), N_EXPERTS, I32)])
    next_e = jnp.where(next_e == N_EXPERTS, -1, next_e)
    xrow = jnp.where(nsub > 0, cstart[block_e] + starts - pstart[block_e], 0) * ROW_TILES

    tile_d = 512
    xs = _dispatch(dest_tiles(cstart, tile_d), v, tile=tile_d)
    yb = _moe(block_e, nsub, next_e, xrow.astype(I32), xs, w1[0], b1[0][:, None, :], w2[0],
              b2[0][:, None, :], n_blocks=n_blocks)
    tile_c = 256
    out = _combine(dest_tiles(pstart, tile_c), yb, route, h1, g2,
                   norm_final_w[None, :], tile=tile_c, tiles_per_batch=N // tile_c)
    return out.reshape(B, N, D)
```

```python
import functools
import math

import jax
import jax.numpy as jnp
import numpy as np
from jax import lax
from jax.experimental import pallas as pl
from jax.experimental.pallas import tpu as pltpu

F32 = jnp.float32
BF16 = jnp.bfloat16
I32 = jnp.int32

D_MODEL = 1024
GRID_W = 64
RET_HEADS = 4
RET_DK = 64
HG_HEADS = 4
PROJ_W = 4096
ROPE_BASE = 10000.0
EPS = 1e-6
N_EXPERTS = 32
TOP_K = 4
D_FF = 1024
SWIGLU_LIMIT = 7.0
SWIGLU_ALPHA = 1.702
LOG2_E = 1.0 / math.log(2.0)

LANES = 128
CHUNK = 128
ROW_BLOCK = 1024
SUB_ROWS = 256
VMEM_LIMIT = 56 * 1024 * 1024
ROW_TILES = D_MODEL // LANES

C_RQ, C_RK, C_RV, C_RG, C_HQ, C_FF, C_FB, C_HV, C_HG = (
    0, 256, 512, 1024, 1536, 2048, 2560, 3072, 3584)


def _cparams(sem):
    return pltpu.CompilerParams(dimension_semantics=sem, vmem_limit_bytes=VMEM_LIMIT)


def _split_bf16(x):
    hi = x.astype(BF16)
    lo = (x - hi.astype(F32)).astype(BF16)
    return hi, lo


def _dot(a, b):
    return jnp.dot(a, b, preferred_element_type=F32)


def _dot_nt(a, b):
    return lax.dot_general(a, b, (((1,), (1,)), ((), ())), preferred_element_type=F32)


def _dot3(a, b):
    ah, al = _split_bf16(a)
    bh, bl = _split_bf16(b)
    return _dot(ah, bh) + (_dot(ah, bl) + _dot(al, bh))


def _silu(x):
    return x * jax.nn.sigmoid(x)


def _load_tile_rows(ref, n):
    return jnp.concatenate([ref[pl.ds(s, n, stride=ROW_TILES), :] for s in range(ROW_TILES)],
                           axis=1)


def _store_tile_rows(ref, x):
    n = x.shape[0]
    for s in range(ROW_TILES):
        ref[pl.ds(s, n, stride=ROW_TILES), :] = x[:, s * LANES:(s + 1) * LANES]


def _ada_kernel(c_ref, w_ref, b_ref, o_ref):
    s = _silu(c_ref[...])
    o_ref[...] = _dot3(s, w_ref[...]) + b_ref[...]


def _ada(cc, w, b):
    nblk = w.shape[1] // D_MODEL
    return pl.pallas_call(
        _ada_kernel,
        out_shape=jax.ShapeDtypeStruct((cc.shape[0], w.shape[1]), F32),
        grid=(nblk,),
        in_specs=[pl.BlockSpec(cc.shape, lambda j: (0, 0)),
                  pl.BlockSpec((D_MODEL, D_MODEL), lambda j: (0, j)),
                  pl.BlockSpec((1, D_MODEL), lambda j: (0, j))],
        out_specs=pl.BlockSpec((cc.shape[0], D_MODEL), lambda j: (0, j)),
        compiler_params=_cparams(("arbitrary",)),
        name="ada",
    )(cc, w, b)


def _inproj_kernel(x_ref, sc_ref, sh_ref, nw_ref, w_ref, lb_ref, cos_ref, sin_ref,
                   pb_ref, lf_ref, u_scr, *, rope):
    x = x_ref[...]
    ms = jnp.mean(x * x, axis=-1, keepdims=True)
    u = x * lax.rsqrt(ms + EPS) * nw_ref[...] * (1.0 + sc_ref[...]) + sh_ref[...]
    u_scr[...] = u.astype(BF16)

    def proj(lo, width):
        return _dot(u_scr[...], w_ref[:, lo:lo + width])

    tm = x.shape[0]
    if rope:
        lane = lax.broadcasted_iota(I32, (tm, LANES), 1)
        first = (lane & 32) == 0

    def put_rot(col, scale):
        for j in range(2):
            lo = col + j * LANES
            t = proj(lo, LANES)
            if scale != 1.0:
                t = t * scale
            if rope:
                tb = j * LANES
                rot = jnp.where(first, pltpu.roll(t, 96, axis=1), pltpu.roll(t, 32, axis=1))
                t = t * cos_ref[:, tb:tb + LANES] + rot * sin_ref[:, tb:tb + LANES]
            pb_ref[:, lo:lo + LANES] = t.astype(BF16)

    put_rot(C_RQ, 1.0)
    put_rot(C_RK, RET_DK ** -0.5)
    pb_ref[:, C_RV:C_RV + 512] = proj(C_RV, 512).astype(BF16)
    pb_ref[:, C_RG:C_RG + 512] = _silu(proj(C_RG, 512)).astype(BF16)
    pb_ref[:, C_HQ:C_HQ + 512] = _silu(proj(C_HQ, 512)).astype(BF16)
    pb_ref[:, C_HV:C_HV + 512] = proj(C_HV, 512).astype(BF16)
    pb_ref[:, C_HG:C_HG + 512] = _silu(proj(C_HG, 512)).astype(BF16)

    la = lb_ref[0]
    lbb = lb_ref[1]
    mx = jnp.maximum(la, lbb)
    ea = jnp.exp(la - mx)
    eb = jnp.exp(lbb - mx)
    lb = ea / (ea + eb)
    for d, col in enumerate((C_FF, C_FB)):
        lbd = lb[d:d + 1, :]
        f = lbd + (1.0 - lbd) * jax.nn.sigmoid(proj(col, 512))
        pb_ref[:, col:col + 512] = (1.0 - f).astype(BF16)
        lf_ref[:, d * 512:(d + 1) * 512] = jnp.log(f) * LOG2_E


def _inproj(x, sc, sh, mod_row, nw, w_bf, hg_lb, cos_t, sin_t, *, rope, tm):
    B, n, _ = x.shape
    nt = n // tm
    if mod_row is None:
        mrow = lambda b, j: (b, 0, 0)
    else:
        mrow = lambda b, j: (mod_row, 0, 0)
    return pl.pallas_call(
        functools.partial(_inproj_kernel, rope=rope),
        out_shape=(jax.ShapeDtypeStruct((B, n, PROJ_W), BF16),
                   jax.ShapeDtypeStruct((B, n, 1024), F32)),
        grid=(B, nt),
        in_specs=[pl.BlockSpec((None, tm, D_MODEL), lambda b, j: (b, j, 0)),
                  pl.BlockSpec((None, 1, D_MODEL), mrow),
                  pl.BlockSpec((None, 1, D_MODEL), mrow),
                  pl.BlockSpec((1, D_MODEL), lambda b, j: (0, 0)),
                  pl.BlockSpec((D_MODEL, PROJ_W), lambda b, j: (0, 0),
                               pipeline_mode=pl.Buffered(1)),
                  pl.BlockSpec((2, 2, 512), lambda b, j: (0, 0, 0)),
                  pl.BlockSpec((tm, 256), lambda b, j: (j, 0)),
                  pl.BlockSpec((tm, 256), lambda b, j: (j, 0))],
        out_specs=(pl.BlockSpec((None, tm, PROJ_W), lambda b, j: (b, j, 0)),
                   pl.BlockSpec((None, tm, 1024), lambda b, j: (b, j, 0))),
        scratch_shapes=[pltpu.VMEM((tm, D_MODEL), BF16)],
        compiler_params=_cparams(("arbitrary", "arbitrary")),
        name="inproj_rope" if rope else "inproj_ctx",
    )(x, sc, sh, nw, w_bf, hg_lb, cos_t, sin_t)


_RET_LGF = [math.log1p(-(2.0 ** (-5.0 - 2.0 * h))) for h in range(RET_HEADS)]
_RET_LGB = [math.log1p(-(2.0 ** (-6.0 - 2.0 * h))) for h in range(RET_HEADS)]


def _ret_kernel(q_ref, k_ref, v_ref, g_ref, kc_ref, vc_ref, o_ref,
                u_scr, sin_scr, kt_scr, dtot_scr, *, n_lat, n_ctx):
    L = CHUNK
    pair = pl.program_id(1)
    row = lax.broadcasted_iota(I32, (L, L), 0).astype(F32)
    col = lax.broadcasted_iota(I32, (L, L), 1).astype(F32)
    lane = lax.broadcasted_iota(I32, (L, LANES), 1)
    trow = lax.broadcasted_iota(I32, (L, 1), 0).astype(F32)
    tcol = lax.broadcasted_iota(I32, (1, L), 1).astype(F32)
    low_half = lane < RET_DK

    def u_chunk(k_blk, v_blk, ci, store_kt):
        kt = k_blk.astype(F32).T
        if store_kt is not None:
            kt_scr[store_kt] = kt.astype(BF16)
        for hh in range(2):
            lgf = jnp.where(pair == 0, _RET_LGF[hh], _RET_LGF[2 + hh])
            lgb = jnp.where(pair == 0, _RET_LGB[hh], _RET_LGB[2 + hh])
            kth = kt[hh * RET_DK:(hh + 1) * RET_DK, :]
            wkf = jnp.exp(lgf * (L - 1.0 - tcol))
            wkb = jnp.exp(lgb * tcol)
            lhs = jnp.concatenate([kth * wkf, kth * wkb], axis=0).astype(BF16)
            u_scr[hh, ci] = _dot(lhs, v_blk[:, hh * LANES:(hh + 1) * LANES])

    for c in range(n_ctx):
        u_chunk(kc_ref[c * L:(c + 1) * L, :], vc_ref[c * L:(c + 1) * L, :], c, None)

    def lat_u(c, carry):
        r0 = pl.multiple_of(c * L, L)
        u_chunk(k_ref[pl.ds(r0, L), :], v_ref[pl.ds(r0, L), :], n_ctx + c, c)
        return carry
    lax.fori_loop(0, n_lat, lat_u, 0, unroll=2)

    ones = jnp.ones((RET_DK, LANES), F32)
    for hh in range(2):
        lgf = jnp.where(pair == 0, _RET_LGF[hh], _RET_LGF[2 + hh])
        lgb = jnp.where(pair == 0, _RET_LGB[hh], _RET_LGB[2 + hh])
        d = row - col
        dtot_scr[hh] = jnp.where(d > 0, jnp.exp(lgf * jnp.maximum(d, 0.0)),
                                 jnp.where(d < 0, jnp.exp(lgb * jnp.maximum(-d, 0.0)), 2.0))
        af = jnp.exp(ones * (lgf * L))
        ab = jnp.exp(ones * (lgb * L))

        s = jnp.zeros((RET_DK, LANES), F32)
        for c in range(n_ctx):
            s = af * s + u_scr[hh, c, 0:RET_DK, :]
        sb = jnp.zeros((RET_DK, LANES), F32)
        for c in reversed(range(n_ctx)):
            sb = ab * sb + u_scr[hh, c, RET_DK:2 * RET_DK, :]

        def fwd(c, s, hh=hh, af=af):
            sin_scr[hh, c, 0:RET_DK, :] = s.astype(BF16)
            return af * s + u_scr[hh, n_ctx + c, 0:RET_DK, :]
        lax.fori_loop(0, n_lat, fwd, s)

        def bwd(i, sb, hh=hh, ab=ab):
            c = n_lat - 1 - i
            sin_scr[hh, c, RET_DK:2 * RET_DK, :] = sb.astype(BF16)
            return ab * sb + u_scr[hh, n_ctx + c, RET_DK:2 * RET_DK, :]
        lax.fori_loop(0, n_lat, bwd, sb)

    def out_chunk(c, carry):
        r0 = pl.multiple_of(c * L, L)
        q = q_ref[pl.ds(r0, L), :].astype(F32)
        qr = pltpu.roll(q, RET_DK, axis=1)
        kt = kt_scr[c]
        for hh in range(2):
            lgf = jnp.where(pair == 0, _RET_LGF[hh], _RET_LGF[2 + hh])
            lgb = jnp.where(pair == 0, _RET_LGB[hh], _RET_LGB[2 + hh])
            mine = low_half if hh == 0 else jnp.logical_not(low_half)
            qm = jnp.where(mine, q, 0.0).astype(BF16)
            p = (_dot(qm, kt) * dtot_scr[hh]).astype(BF16)
            vh = v_ref[pl.ds(r0, L), hh * LANES:(hh + 1) * LANES]
            wqf = jnp.exp(lgf * (trow + 1.0))
            wqb = jnp.exp(lgb * (L - trow))
            qa, qb = (q, qr) if hh == 0 else (qr, q)
            qs = jnp.where(low_half, qa * wqf, qb * wqb).astype(BF16)
            o = _dot(p, vh) + _dot(qs, sin_scr[hh, c])
            ms = jnp.mean(o * o, axis=-1, keepdims=True)
            gh = g_ref[pl.ds(r0, L), hh * LANES:(hh + 1) * LANES].astype(F32)
            o_ref[pl.ds(r0, L), hh * LANES:(hh + 1) * LANES] = (
                o * lax.rsqrt(ms + EPS) * gh).astype(BF16)
        return carry
    lax.fori_loop(0, n_lat, out_chunk, 0, unroll=2)


def _retention(pb, pbc):
    B, n, _ = pb.shape
    nc = pbc.shape[1]
    n_lat, n_ctx = n // CHUNK, nc // CHUNK
    return pl.pallas_call(
        functools.partial(_ret_kernel, n_lat=n_lat, n_ctx=n_ctx),
        out_shape=jax.ShapeDtypeStruct((B, n, 512), BF16),
        grid=(B, 2),
        in_specs=[pl.BlockSpec((None, n, LANES), lambda b, p: (b, 0, C_RQ // LANES + p)),
                  pl.BlockSpec((None, n, LANES), lambda b, p: (b, 0, C_RK // LANES + p)),
                  pl.BlockSpec((None, n, 256), lambda b, p: (b, 0, C_RV // 256 + p)),
                  pl.BlockSpec((None, n, 256), lambda b, p: (b, 0, C_RG // 256 + p)),
                  pl.BlockSpec((None, nc, LANES), lambda b, p: (b, 0, C_RK // LANES + p)),
                  pl.BlockSpec((None, nc, 256), lambda b, p: (b, 0, C_RV // 256 + p))],
        out_specs=pl.BlockSpec((None, n, 256), lambda b, p: (b, 0, p)),
        scratch_shapes=[pltpu.VMEM((2, n_lat + n_ctx, CHUNK, LANES), F32),
                        pltpu.VMEM((2, n_lat, CHUNK, LANES), BF16),
                        pltpu.VMEM((n_lat, LANES, CHUNK), BF16),
                        pltpu.VMEM((2, CHUNK, CHUNK), F32)],
        compiler_params=_cparams(("arbitrary", "arbitrary")),
        name="retention",
    )(pb, pb, pb, pb, pbc, pbc)


_LEVELS = (64, 32, 16, 8, 4, 2, 1)


def _expand_rows(r, rep):
    n = r.shape[0]
    if n == 1:
        return jnp.broadcast_to(r, (rep, r.shape[1]))
    return jnp.concatenate(
        [jnp.broadcast_to(r[i:i + 1, :], (rep, r.shape[1])) for i in range(n)], axis=0)


def _hgrn_kernel(q_ref, kf_ref, kb_ref, v_ref, g_ref, lff_ref, lfb_ref,
                 kfc_ref, kbc_ref, vc_ref, lffc_ref, lfbc_ref, nw_ref, o_ref,
                 ut_scr, a_scr, qs_scr, oi_scr, sin_scr, bfb_scr, *, n_lat, n_ctx):
    L = CHUNK
    row = lax.broadcasted_iota(I32, (L, L), 0)
    col = lax.broadcasted_iota(I32, (L, L), 1)
    xr_bits = lax.bitcast_convert_type((row ^ col).astype(F32), I32)
    lv = lax.shift_right_logical(xr_bits, 23) - 127
    row2 = lax.broadcasted_iota(I32, (L, 2 * L), 0)
    col2 = lax.broadcasted_iota(I32, (L, 2 * L), 1) & (L - 1)
    tril2 = jnp.where(col2 <= row2, 1.0, 0.0).astype(BF16)
    triu2 = jnp.where(col2 >= row2, 1.0, 0.0).astype(BF16)

    def cums(lff, lfb):
        hf, lof = _split_bf16(lff)
        hb, lob = _split_bf16(lfb)
        bf = _dot(tril2, jnp.concatenate([hf, lof], axis=0))
        bb = _dot(triu2, jnp.concatenate([hb, lob], axis=0))
        return bf, bb

    def state_part(ci, kf, kb, v_blk, bf, bb):
        endf = bf[L - 1:L, :]
        endb = bb[0:1, :]
        ksf = kf * jnp.exp2(endf - bf)
        ksb = kb * jnp.exp2(endb - bb)
        vt = v_blk.astype(F32).T.astype(BF16)
        ut_scr[ci] = _dot(vt, jnp.concatenate([ksf, ksb], axis=1).astype(BF16))
        a_scr[ci] = jnp.broadcast_to(
            jnp.concatenate([jnp.exp2(endf), jnp.exp2(endb)], axis=1), (8, 2 * LANES))

    for c in range(n_ctx):
        sl = slice(c * L, (c + 1) * L)
        bf, bb = cums(lffc_ref[sl, :], lfbc_ref[sl, :])
        state_part(c, kfc_ref[sl, :].astype(F32), kbc_ref[sl, :].astype(F32),
                   vc_ref[sl, :], bf, bb)

    def lat_chunk(c, u):
        rows = pl.ds(pl.multiple_of(c * L, L), L)
        bf_scr = bfb_scr.at[u, 0]
        bb_scr = bfb_scr.at[u, 1]
        q = q_ref[rows, :].astype(F32)
        kf = kf_ref[rows, :].astype(F32)
        kb = kb_ref[rows, :].astype(F32)
        v_blk = v_ref[rows, :]
        lff = lff_ref[rows, :]
        lfb = lfb_ref[rows, :]
        bf, bb = cums(lff, lfb)
        state_part(n_ctx + c, kf, kb, v_blk, bf, bb)
        qs_scr[rows, :] = jnp.concatenate([q * jnp.exp2(bf), q * jnp.exp2(bb)],
                                          axis=1).astype(BF16)
        bf_scr[...] = bf
        bb_scr[...] = bb

        acc = jnp.zeros((L, L), F32)
        for lvl, h in enumerate(_LEVELS):
            bit = (row & h) != 0
            ksel = jnp.where(bit, kb, kf)
            if h >= 4:
                n = (L // 2) // h
                if n == 1:
                    rf = bf_scr[h - 1:h, :]
                    rb = bb_scr[h:h + 1, :]
                else:
                    rf = bf_scr[pl.ds(h - 1, n, stride=2 * h), :]
                    rb = bb_scr[pl.ds(h, n, stride=2 * h), :]
                df = bf - _expand_rows(rf, 2 * h)
                db = bb - _expand_rows(rb, 2 * h)
                eq = jnp.where(bit, df, db)
                ek = -jnp.where(bit, db, df)
            elif h == 2:
                m = row & 3
                lff_n = pltpu.roll(lff, L - 1, axis=0)
                lfb_n = pltpu.roll(lfb, L - 1, axis=0)
                eq = jnp.where(m == 2, lff,
                               jnp.where(m == 3, lff + pltpu.roll(lff, 1, axis=0),
                                         jnp.where(m == 0, lfb + lfb_n, lfb)))
                ek = jnp.where(m == 3, pltpu.roll(lfb, 1, axis=0),
                               jnp.where(m == 0, lff_n, 0.0))
            else:
                eq = jnp.where(bit, lff, lfb)
                ek = None
            lhs = (q * jnp.exp2(eq)).astype(BF16)
            rhs = (ksel if ek is None else ksel * jnp.exp2(ek)).astype(BF16)
            acc = jnp.where(lv == 6 - lvl, _dot_nt(lhs, rhs), acc)

        dsum = jnp.sum(q * (kf + kb), axis=-1, keepdims=True)
        oi_scr[rows, :] = _dot(acc.astype(BF16), v_blk) + dsum * v_blk.astype(F32)

    def lat_pair(i, carry):
        for u in range(4):
            lat_chunk(4 * i + u, u)
        return carry
    lax.fori_loop(0, n_lat // 4, lat_pair, 0)

    st = jnp.zeros((LANES, LANES), F32)
    for c in range(n_ctx):
        st = st * a_scr[c, 0:1, 0:LANES] + ut_scr[c, :, 0:LANES]
    stb = jnp.zeros((LANES, LANES), F32)
    for c in reversed(range(n_ctx)):
        stb = stb * a_scr[c, 0:1, LANES:2 * LANES] + ut_scr[c, :, LANES:2 * LANES]

    def fwd(c, st):
        sin_scr[c, :, 0:LANES] = st.astype(BF16)
        ci = n_ctx + c
        return st * a_scr[ci, 0:1, 0:LANES] + ut_scr[ci, :, 0:LANES]
    lax.fori_loop(0, n_lat, fwd, st)

    def bwd(i, stb):
        c = n_lat - 1 - i
        sin_scr[c, :, LANES:2 * LANES] = stb.astype(BF16)
        ci = n_ctx + c
        return stb * a_scr[ci, 0:1, LANES:2 * LANES] + ut_scr[ci, :, LANES:2 * LANES]
    lax.fori_loop(0, n_lat, bwd, stb)

    def out_chunk(c, carry):
        rows = pl.ds(pl.multiple_of(c * L, L), L)
        o = oi_scr[rows, :] + _dot_nt(qs_scr[rows, :], sin_scr[c])
        ms = jnp.mean(o * o, axis=-1, keepdims=True)
        y = o * lax.rsqrt(ms + EPS) * nw_ref[...] * g_ref[rows, :].astype(F32)
        o_ref[rows, :] = y.astype(BF16)
        return carry
    lax.fori_loop(0, n_lat, out_chunk, 0, unroll=4)


def _hgrn(pb, lf, pbc, lfc, nw):
    B, n, _ = pb.shape
    nc = pbc.shape[1]
    n_lat, n_ctx = n // CHUNK, nc // CHUNK

    def colblk(rows, col0):
        return pl.BlockSpec((None, rows, LANES), lambda b, h: (b, 0, col0 // LANES + h))

    return pl.pallas_call(
        functools.partial(_hgrn_kernel, n_lat=n_lat, n_ctx=n_ctx),
        out_shape=jax.ShapeDtypeStruct((B, n, 512), BF16),
        grid=(B, HG_HEADS),
        in_specs=[colblk(n, C_HQ), colblk(n, C_FF), colblk(n, C_FB), colblk(n, C_HV),
                  colblk(n, C_HG), colblk(n, 0), colblk(n, 512),
                  colblk(nc, C_FF), colblk(nc, C_FB), colblk(nc, C_HV),
                  colblk(nc, 0), colblk(nc, 512),
                  pl.BlockSpec((1, LANES), lambda b, h: (0, 0))],
        out_specs=pl.BlockSpec((None, n, LANES), lambda b, h: (b, 0, h)),
        scratch_shapes=[pltpu.VMEM((n_lat + n_ctx, LANES, 2 * LANES), F32),
                        pltpu.VMEM((n_lat + n_ctx, 8, 2 * LANES), F32),
                        pltpu.VMEM((n, 2 * LANES), BF16),
                        pltpu.VMEM((n, LANES), F32),
                        pltpu.VMEM((n_lat, LANES, 2 * LANES), BF16),
                        pltpu.VMEM((4, 2, CHUNK, LANES), F32)],
        compiler_params=_cparams(("arbitrary", "arbitrary")),
        name="hgrn2",
    )(pb, pb, pb, pb, pb, lf, lf, pbc, pbc, pbc, lfc, lfc, nw)


def _outproj_kernel(ar_ref, ah_ref, w_ref, x_ref, g1_ref, sc_ref, sh_ref, nw_ref,
                    rw_ref, rb_ref, h1_ref, v_ref, route_ref, route_t_ref, cnt_ref, cnt_scr):
    first_step = jnp.logical_and(pl.program_id(0) == 0, pl.program_id(1) == 0)

    @pl.when(first_step)
    def _():
        cnt_scr[...] = jnp.zeros_like(cnt_scr)

    y = _dot(ar_ref[...], w_ref[0:512, :]) + _dot(ah_ref[...], w_ref[512:1024, :])
    h1 = x_ref[...] + g1_ref[...] * y
    h1_ref[...] = h1
    ms = jnp.mean(h1 * h1, axis=-1, keepdims=True)
    v = h1 * lax.rsqrt(ms + EPS) * nw_ref[...] * (1.0 + sc_ref[...]) + sh_ref[...]
    _store_tile_rows(v_ref, v)

    tm = v.shape[0]
    logits = _dot3(v, rw_ref[...]) + rb_ref[...]
    lane = lax.broadcasted_iota(I32, (tm, LANES), 1)
    lane_f = lane.astype(F32)
    l = logits
    sels, tops, idxs = [], [], []
    for _ in range(TOP_K):
        m = jnp.max(l, axis=1, keepdims=True)
        i = jnp.min(jnp.where(l == m, lane_f, float(LANES)), axis=1, keepdims=True)
        sel = lane_f == i
        l = jnp.where(sel, -jnp.inf, l)
        sels.append(sel)
        tops.append(m)
        idxs.append(i)
    es = [jnp.exp(t - tops[0]) for t in tops]
    den = es[0] + es[1] + es[2] + es[3]
    gates = [e / den for e in es]

    oh = jnp.zeros((tm, LANES), F32)
    for sel in sels:
        oh = jnp.where(sel, 1.0, oh)
    r = lax.broadcasted_iota(I32, (tm, tm), 0)
    c = lax.broadcasted_iota(I32, (tm, tm), 1)
    tri = jnp.where(c < r, 1.0, 0.0).astype(BF16)
    before = _dot(tri, oh.astype(BF16)) + cnt_scr[0:1, :]
    ranks = [jnp.sum(jnp.where(sel, before, 0.0), axis=1, keepdims=True) for sel in sels]
    cnt_scr[...] = cnt_scr[...] + jnp.sum(oh, axis=0, keepdims=True)
    cnt_ref[...] = cnt_scr[...]

    out = jnp.zeros((tm, LANES), F32)
    for k in range(TOP_K):
        out = jnp.where(lane == k, idxs[k], out)
        out = jnp.where(lane == TOP_K + k, gates[k], out)
        out = jnp.where(lane == 2 * TOP_K + k, ranks[k], out)
    route_ref[...] = out
    route_t_ref[...] = out.T[0:16, :]


def _outproj(a_ret, a_hg, w_bf, x, g1, sc2, sh2, nw, rw, rb, *, tm):
    B, n, _ = x.shape
    nt = n // tm
    T = B * n
    mrow = lambda b, j: (b, 0, 0)
    tok = lambda b, j: (b * nt + j, 0)
    return pl.pallas_call(
        _outproj_kernel,
        out_shape=(jax.ShapeDtypeStruct((T, D_MODEL), F32),
                   jax.ShapeDtypeStruct((T * ROW_TILES, LANES), F32),
                   jax.ShapeDtypeStruct((T, LANES), F32),
                   jax.ShapeDtypeStruct((16, T), F32),
                   jax.ShapeDtypeStruct((8, LANES), F32)),
        grid=(B, nt),
        in_specs=[pl.BlockSpec((None, tm, 512), lambda b, j: (b, j, 0)),
                  pl.BlockSpec((None, tm, 512), lambda b, j: (b, j, 0)),
                  pl.BlockSpec((D_MODEL, D_MODEL), lambda b, j: (0, 0)),
                  pl.BlockSpec((None, tm, D_MODEL), lambda b, j: (b, j, 0)),
                  pl.BlockSpec((None, 1, D_MODEL), mrow),
                  pl.BlockSpec((None, 1, D_MODEL), mrow),
                  pl.BlockSpec((None, 1, D_MODEL), mrow),
                  pl.BlockSpec((1, D_MODEL), lambda b, j: (0, 0)),
                  pl.BlockSpec((D_MODEL, LANES), lambda b, j: (0, 0)),
                  pl.BlockSpec((1, LANES), lambda b, j: (0, 0))],
        out_specs=(pl.BlockSpec((tm, D_MODEL), tok),
                   pl.BlockSpec((tm * ROW_TILES, LANES), tok),
                   pl.BlockSpec((tm, LANES), tok),
                   pl.BlockSpec((16, tm), lambda b, j: (0, b * nt + j)),
                   pl.BlockSpec((8, LANES), lambda b, j: (0, 0))),
        scratch_shapes=[pltpu.VMEM((8, LANES), F32)],
        compiler_params=_cparams(("arbitrary", "arbitrary")),
        name="outproj_router",
    )(a_ret, a_hg, w_bf, x, g1, sc2, sh2, nw, rw, rb)


def _dispatch_kernel(dest_hbm, v_ref, xs_hbm, idx_smem, zbuf, sem_idx, sem_z, sem_rows,
                     *, tile, n_tiles, n_rows):
    i = pl.program_id(0)
    slot = lax.rem(i, 2)

    def idx_copy(j, s):
        n_asg = tile * TOP_K
        return pltpu.make_async_copy(dest_hbm.at[j], idx_smem.at[pl.ds(s * n_asg, n_asg)],
                                     sem_idx.at[s])

    def zero_copy(j):
        r0 = (n_rows + j * SUB_ROWS) * ROW_TILES
        return pltpu.make_async_copy(zbuf, xs_hbm.at[pl.ds(r0, SUB_ROWS * ROW_TILES), :], sem_z)

    @pl.when(i == 0)
    def _():
        idx_copy(0, 0).start()
        zbuf[...] = jnp.zeros_like(zbuf)
        for j in range(ROW_BLOCK // SUB_ROWS):
            zero_copy(j).start()
        for j in range(ROW_BLOCK // SUB_ROWS):
            zero_copy(j).wait()

    idx_copy(i, slot).wait()

    @pl.when(i + 1 < n_tiles)
    def _():
        idx_copy(i + 1, 1 - slot).start()

    def issue(t, carry):
        for k in range(TOP_K):
            d = pl.multiple_of(idx_smem[slot * (tile * TOP_K) + k * tile + t], ROW_TILES)
            src = v_ref.at[pl.ds(pl.multiple_of(t * ROW_TILES, ROW_TILES), ROW_TILES), :]
            pltpu.make_async_copy(src, xs_hbm.at[pl.ds(d, ROW_TILES), :],
                                  sem_rows).start(priority=k % 2)
        return carry
    lax.fori_loop(0, tile, issue, 0, unroll=4)
    for _ in range(TOP_K):
        pltpu.make_async_copy(v_ref, xs_hbm.at[pl.ds(0, tile * ROW_TILES), :], sem_rows).wait()


def _dispatch(dest_tiles, v, *, tile):
    T = v.shape[0] // ROW_TILES
    n_rows = T * TOP_K
    return pl.pallas_call(
        functools.partial(_dispatch_kernel, tile=tile, n_tiles=T // tile, n_rows=n_rows),
        out_shape=jax.ShapeDtypeStruct(((n_rows + ROW_BLOCK) * ROW_TILES, LANES), F32),
        grid=(T // tile,),
        in_specs=[pl.BlockSpec(memory_space=pl.ANY),
                  pl.BlockSpec((tile * ROW_TILES, LANES), lambda i: (i, 0))],
        out_specs=pl.BlockSpec(memory_space=pl.ANY),
        scratch_shapes=[pltpu.SMEM((2 * tile * TOP_K,), I32),
                        pltpu.VMEM((SUB_ROWS * ROW_TILES, LANES), F32),
                        pltpu.SemaphoreType.DMA((2,)),
                        pltpu.SemaphoreType.DMA,
                        pltpu.SemaphoreType.DMA],
        compiler_params=_cparams(("arbitrary",)),
        name="dispatch",
    )(dest_tiles, v)


def _moe_kernel(be_ref, ns_ref, nx_ref, xr_ref, x_hbm, w1_hbm, b1_ref, w2_hbm, b2_ref, y_ref,
                w1_stage, w2_stage, w1_scr, w2_scr, xbuf, x_scr, act_scr, sem_w, sem_x,
                *, n_blocks):
    i = pl.program_id(0)
    slot = lax.rem(i, 2)
    e = be_ref[i]
    nsub = ns_ref[i]
    changed = jnp.logical_or(i == 0, e != be_ref[jnp.maximum(i - 1, 0)])

    def x_copy(j, s):
        r0 = pl.multiple_of(xr_ref[j], ROW_TILES)
        return pltpu.make_async_copy(x_hbm.at[pl.ds(r0, ROW_BLOCK * ROW_TILES), :], xbuf.at[s],
                                     sem_x.at[s])

    @pl.when(i == 0)
    def _():
        x_copy(0, 0).start()

    @pl.when(nsub > 0)
    def _():
        x_copy(i, slot).wait()

    nxt_blk = jnp.minimum(i + 1, n_blocks - 1)

    @pl.when(jnp.logical_and(i + 1 < n_blocks, ns_ref[nxt_blk] > 0))
    def _():
        x_copy(nxt_blk, 1 - slot).start()

    def weight_copies(ex):
        return (pltpu.make_async_copy(w1_hbm.at[ex], w1_stage, sem_w.at[0]),
                pltpu.make_async_copy(w2_hbm.at[ex], w2_stage, sem_w.at[1]))

    @pl.when(i == 0)
    def _():
        for cp in weight_copies(e):
            cp.start()

    @pl.when(jnp.logical_and(changed, nsub > 0))
    def _():
        for cp in weight_copies(e):
            cp.wait()
        w1_scr[...] = w1_stage[...].astype(BF16)
        w2_scr[...] = w2_stage[...].astype(BF16)
        nxt = nx_ref[e]

        @pl.when(nxt >= 0)
        def _():
            for cp in weight_copies(nxt):
                cp.start()

    def compute(rows):
        x_scr[0:rows, :] = _load_tile_rows(xbuf.at[slot], rows).astype(BF16)
        cw = 256
        for c in range(D_FF // cw):
            glu = (_dot(x_scr[0:rows, :], w1_scr[:, c * cw:(c + 1) * cw])
                   + b1_ref[:, c * cw:(c + 1) * cw])
            lin = (_dot(x_scr[0:rows, :], w1_scr[:, D_FF + c * cw:D_FF + (c + 1) * cw])
                   + b1_ref[:, D_FF + c * cw:D_FF + (c + 1) * cw])
            glu = jnp.minimum(glu, SWIGLU_LIMIT)
            lin = jnp.clip(lin, -SWIGLU_LIMIT, SWIGLU_LIMIT)
            act = glu * jax.nn.sigmoid(SWIGLU_ALPHA * glu) * (lin + 1.0)
            act_scr[0:rows, c * cw:(c + 1) * cw] = act.astype(BF16)
        for c in range(D_MODEL // cw):
            y = (_dot(act_scr[0:rows, :], w2_scr[:, c * cw:(c + 1) * cw])
                 + b2_ref[:, c * cw:(c + 1) * cw])
            for s in range(cw // LANES):
                t = c * (cw // LANES) + s
                y_ref[pl.ds(t, rows, stride=ROW_TILES), :] = y[:, s * LANES:(s + 1) * LANES]
        if rows < ROW_BLOCK:
            y_ref[rows * ROW_TILES:, :] = jnp.zeros(((ROW_BLOCK - rows) * ROW_TILES, LANES), F32)

    for m in range(1, ROW_BLOCK // SUB_ROWS + 1):
        @pl.when(nsub == m)
        def _(m=m):
            compute(m * SUB_ROWS)

    @pl.when(nsub == 0)
    def _():
        y_ref[...] = jnp.zeros_like(y_ref)


def _moe(block_e, nsub, next_e, xrow, xs, w1, b1, w2, b2, *, n_blocks):
    return pl.pallas_call(
        functools.partial(_moe_kernel, n_blocks=n_blocks),
        out_shape=jax.ShapeDtypeStruct((n_blocks * ROW_BLOCK * ROW_TILES, LANES), F32),
        grid_spec=pltpu.PrefetchScalarGridSpec(
            num_scalar_prefetch=4,
            grid=(n_blocks,),
            in_specs=[pl.BlockSpec(memory_space=pl.ANY),
                      pl.BlockSpec(memory_space=pl.ANY),
                      pl.BlockSpec((None, 1, 2 * D_FF), lambda i, be, ns, nx, xr: (be[i], 0, 0)),
                      pl.BlockSpec(memory_space=pl.ANY),
                      pl.BlockSpec((None, 1, D_MODEL), lambda i, be, ns, nx, xr: (be[i], 0, 0))],
            out_specs=pl.BlockSpec((ROW_BLOCK * ROW_TILES, LANES),
                                   lambda i, be, ns, nx, xr: (i, 0)),
            scratch_shapes=[pltpu.VMEM((D_MODEL, 2 * D_FF), F32),
                            pltpu.VMEM((D_FF, D_MODEL), F32),
                            pltpu.VMEM((D_MODEL, 2 * D_FF), BF16),
                            pltpu.VMEM((D_FF, D_MODEL), BF16),
                            pltpu.VMEM((2, ROW_BLOCK * ROW_TILES, LANES), F32),
                            pltpu.VMEM((ROW_BLOCK, D_MODEL), BF16),
                            pltpu.VMEM((ROW_BLOCK, D_FF), BF16),
                            pltpu.SemaphoreType.DMA((2,)),
                            pltpu.SemaphoreType.DMA((2,))]),
        compiler_params=_cparams(("arbitrary",)),
        name="moe_ffn",
    )(block_e, nsub, next_e, xrow, xs, w1, b1, w2, b2)


def _combine_kernel(dest_hbm, yb_hbm, route_ref, h1_ref, g2_ref, nw_ref, o_ref,
                    idx_smem, rows, sem_idx, sem_rows, *, tile, n_tiles):
    i = pl.program_id(0)
    slot = lax.rem(i, 2)

    def idx_copy(j, s):
        n_asg = tile * TOP_K
        return pltpu.make_async_copy(dest_hbm.at[j], idx_smem.at[pl.ds(s * n_asg, n_asg)],
                                     sem_idx.at[s])

    def issue_rows(s):
        def issue(t, carry):
            for k in range(TOP_K):
                d = pl.multiple_of(idx_smem[s * (tile * TOP_K) + k * tile + t], ROW_TILES)
                dst = rows.at[s, k, pl.ds(pl.multiple_of(t * ROW_TILES, ROW_TILES), ROW_TILES), :]
                pltpu.make_async_copy(yb_hbm.at[pl.ds(d, ROW_TILES), :], dst,
                                      sem_rows.at[s]).start(priority=k % 2)
            return carry
        lax.fori_loop(0, tile, issue, 0, unroll=4)

    @pl.when(i == 0)
    def _():
        idx_copy(0, 0).start()
        idx_copy(0, 0).wait()
        issue_rows(0)

        if n_tiles > 1:
            idx_copy(1, 1).start()

    @pl.when(i + 1 < n_tiles)
    def _():
        idx_copy(i + 1, 1 - slot).wait()
        issue_rows(1 - slot)

        @pl.when(i + 2 < n_tiles)
        def _():
            idx_copy(i + 2, slot).start()

    for k in range(TOP_K):
        pltpu.make_async_copy(yb_hbm.at[pl.ds(0, tile * ROW_TILES), :], rows.at[slot, k],
                              sem_rows.at[slot]).wait()

    route = route_ref[...]
    y = jnp.zeros((tile, D_MODEL), F32)
    for k in range(TOP_K):
        y = y + route[:, TOP_K + k:TOP_K + k + 1] * _load_tile_rows(rows.at[slot, k], tile)
    h = h1_ref[...] + g2_ref[...] * y
    ms = jnp.mean(h * h, axis=-1, keepdims=True)
    o_ref[...] = h * lax.rsqrt(ms + EPS) * nw_ref[...]


def _combine(dest_tiles, yb, route, h1, g2, nw, *, tile, tiles_per_batch):
    T = h1.shape[0]
    return pl.pallas_call(
        functools.partial(_combine_kernel, tile=tile, n_tiles=T // tile),
        out_shape=jax.ShapeDtypeStruct((T, D_MODEL), F32),
        grid=(T // tile,),
        in_specs=[pl.BlockSpec(memory_space=pl.ANY),
                  pl.BlockSpec(memory_space=pl.ANY),
                  pl.BlockSpec((tile, LANES), lambda i: (i, 0)),
                  pl.BlockSpec((tile, D_MODEL), lambda i: (i, 0)),
                  pl.BlockSpec((None, 1, D_MODEL), lambda i: (i // tiles_per_batch, 0, 0)),
                  pl.BlockSpec((1, D_MODEL), lambda i: (0, 0))],
        out_specs=pl.BlockSpec((tile, D_MODEL), lambda i: (i, 0)),
        scratch_shapes=[pltpu.SMEM((2 * tile * TOP_K,), I32),
                        pltpu.VMEM((2, TOP_K, tile * ROW_TILES, LANES), F32),
                        pltpu.SemaphoreType.DMA((2,)),
                        pltpu.SemaphoreType.DMA((2,))],
        compiler_params=_cparams(("arbitrary",)),
        name="combine_norm",
    )(dest_tiles, yb, route, h1, g2, nw)


def _rope_tables(n):
    rows = n // GRID_W
    row = jnp.repeat(jnp.arange(rows, dtype=F32), GRID_W)
    col = jnp.tile(jnp.arange(GRID_W, dtype=F32), rows)
    n_freq = RET_DK // 4
    inv = ROPE_BASE ** (-jnp.arange(n_freq, dtype=F32) / n_freq)
    ang = jnp.concatenate([row[:, None] * inv, col[:, None] * inv], axis=-1)
    cos, sin = jnp.cos(ang), jnp.sin(ang)
    cos_h = jnp.concatenate([cos, cos], axis=-1)
    sin_h = jnp.concatenate([-sin, sin], axis=-1)
    return jnp.tile(cos_h, (1, RET_HEADS)), jnp.tile(sin_h, (1, RET_HEADS))


def kernel(x, c, ctx, c_ctx, w_ada, b_ada, norm_mix_w, norm_ffn_w, w_in, w_out, hg_lb,
           hg_norm_w, router_w, router_b, w1, b1, w2, b2, norm_final_w):
    B, N, D = x.shape
    C = ctx.shape[1]
    T = B * N
    assert D == D_MODEL and w_ada.shape[0] == 1

    cc = jnp.concatenate([c.astype(F32), c_ctx.astype(F32)[None, :],
                          jnp.zeros((16 - B - 1, D), F32)], axis=0)
    mod = _ada(cc, w_ada[0], b_ada[0][None, :])
    mod = mod.reshape(16, 6, 1, D).transpose(1, 0, 2, 3)
    sh1, sc1, g1, sh2, sc2, g2 = (mod[i] for i in range(6))

    w_in_bf = w_in[0].astype(BF16)
    w_out_bf = w_out[0].astype(BF16)
    nw_mix = norm_mix_w[0][None, :]
    cos_t, sin_t = _rope_tables(N)

    pb, lf = _inproj(x, sc1, sh1, None, nw_mix, w_in_bf, hg_lb[:2], cos_t, sin_t,
                     rope=True, tm=1024)
    pbc, lfc = _inproj(ctx, sc1, sh1, B, nw_mix, w_in_bf, hg_lb[:2], cos_t[:C], sin_t[:C],
                       rope=False, tm=C)

    a_ret = _retention(pb, pbc)
    a_hg = _hgrn(pb, lf, pbc, lfc, hg_norm_w[0][None, :])

    rw = jnp.zeros((D, LANES), F32).at[:, :N_EXPERTS].set(router_w[0])
    rb = jnp.full((1, LANES), -1e30, F32).at[0, :N_EXPERTS].set(router_b[0])
    h1, v, route, route_t, cnt = _outproj(a_ret, a_hg, w_out_bf, x, g1, sc2, sh2,
                                          norm_ffn_w[0][None, :], rw, rb, tm=512)

    idx = route_t[0:TOP_K].astype(I32)
    rank = route_t[2 * TOP_K:3 * TOP_K].astype(I32)
    counts = cnt[0, :N_EXPERTS].astype(I32)
    padded = (counts + ROW_BLOCK - 1) // ROW_BLOCK * ROW_BLOCK
    pend = jnp.cumsum(padded)
    pstart = pend - padded
    cstart = jnp.cumsum(counts) - counts
    onehot = idx[None] == jnp.arange(N_EXPERTS, dtype=I32)[:, None, None]

    def dest_tiles(first_row, tile):
        dest = (jnp.sum(jnp.where(onehot, first_row[:, None, None], 0), axis=0) + rank) * ROW_TILES
        return dest.reshape(TOP_K, T // tile, tile).transpose(1, 0, 2).reshape(T // tile,
                                                                               TOP_K * tile)
    n_blocks = (T * TOP_K) // ROW_BLOCK + N_EXPERTS
    starts = jnp.arange(n_blocks, dtype=I32) * ROW_BLOCK
    block_e = jnp.minimum(jnp.sum((pend[None, :] <= starts[:, None]).astype(I32), axis=1),
                          N_EXPERTS - 1)
    valid = jnp.clip(counts[block_e] - (starts - pstart[block_e]), 0, ROW_BLOCK)
    nsub = (valid + SUB_ROWS - 1) // SUB_ROWS
    first_from = lax.cummin(jnp.where(counts > 0, jnp.arange(N_EXPERTS, dtype=I32), N_EXPERTS),
                            reverse=True)
    next_e = jnp.concatenate([first_from[1:], jnp.full((1,), N_EXPERTS, I32)])
    next_e = jnp.where(next_e == N_EXPERTS, -1, next_e)
    xrow = jnp.where(nsub > 0, cstart[block_e] + starts - pstart[block_e], 0) * ROW_TILES

    tile_d = 512
    xs = _dispatch(dest_tiles(cstart, tile_d), v, tile=tile_d)
    yb = _moe(block_e, nsub, next_e, xrow.astype(I32), xs, w1[0], b1[0][:, None, :], w2[0],
              b2[0][:, None, :], n_blocks=n_blocks)
    tile_c = 256
    out = _combine(dest_tiles(pstart, tile_c), yb, route, h1, g2,
                   norm_final_w[None, :], tile=tile_c, tiles_per_batch=N // tile_c)
    return out.reshape(B, N, D)
```

```python
import functools
import math

import jax
import jax.numpy as jnp
import numpy as np
from jax import lax
from jax.experimental import pallas as pl
from jax.experimental.pallas import tpu as pltpu

F32 = jnp.float32
BF16 = jnp.bfloat16
I32 = jnp.int32

D_MODEL = 1024
GRID_W = 64
RET_HEADS = 4
RET_DK = 64
HG_HEADS = 4
PROJ_W = 4096
ROPE_BASE = 10000.0
EPS = 1e-6
N_EXPERTS = 32
TOP_K = 4
D_FF = 1024
SWIGLU_LIMIT = 7.0
SWIGLU_ALPHA = 1.702
LOG2_E = 1.0 / math.log(2.0)

LANES = 128
CHUNK = 128
ROW_BLOCK = 1024
SUB_ROWS = 256
COMBINE_TILE = 256
WIN_ROWS = 16
WIN_CHUNKS = (COMBINE_TILE * TOP_K + N_EXPERTS * (WIN_ROWS - 1) + WIN_ROWS - 1) // WIN_ROWS
VMEM_LIMIT = 56 * 1024 * 1024
ROW_TILES = D_MODEL // LANES

C_RQ, C_RK, C_RV, C_RG, C_HQ, C_FF, C_FB, C_HV, C_HG = (
    0, 256, 512, 1024, 1536, 2048, 2560, 3072, 3584)


def _cparams(sem):
    return pltpu.CompilerParams(dimension_semantics=sem, vmem_limit_bytes=VMEM_LIMIT)


def _split_bf16(x):
    hi = x.astype(BF16)
    lo = (x - hi.astype(F32)).astype(BF16)
    return hi, lo


def _dot(a, b):
    return jnp.dot(a, b, preferred_element_type=F32)


def _dot_nt(a, b):
    return lax.dot_general(a, b, (((1,), (1,)), ((), ())), preferred_element_type=F32)


def _dot3(a, b):
    ah, al = _split_bf16(a)
    bh, bl = _split_bf16(b)
    return _dot(ah, bh) + (_dot(ah, bl) + _dot(al, bh))


def _silu(x):
    return x * jax.nn.sigmoid(x)


def _load_tile_rows(ref, n):
    return jnp.concatenate([ref[pl.ds(s, n, stride=ROW_TILES), :] for s in range(ROW_TILES)],
                           axis=1)


def _store_tile_rows(ref, x):
    n = x.shape[0]
    for s in range(ROW_TILES):
        ref[pl.ds(s, n, stride=ROW_TILES), :] = x[:, s * LANES:(s + 1) * LANES]


def _ada_kernel(c_ref, w_ref, b_ref, o_ref):
    s = _silu(c_ref[...])
    o_ref[...] = _dot3(s, w_ref[...]) + b_ref[...]


def _ada(cc, w, b):
    nblk = w.shape[1] // D_MODEL
    return pl.pallas_call(
        _ada_kernel,
        out_shape=jax.ShapeDtypeStruct((cc.shape[0], w.shape[1]), F32),
        grid=(nblk,),
        in_specs=[pl.BlockSpec(cc.shape, lambda j: (0, 0)),
                  pl.BlockSpec((D_MODEL, D_MODEL), lambda j: (0, j)),
                  pl.BlockSpec((1, D_MODEL), lambda j: (0, j))],
        out_specs=pl.BlockSpec((cc.shape[0], D_MODEL), lambda j: (0, j)),
        compiler_params=_cparams(("arbitrary",)),
        name="ada",
    )(cc, w, b)


def _inproj_kernel(x_ref, sc_ref, sh_ref, nw_ref, w_ref, lb_ref, cos_ref, sin_ref,
                   pb_ref, lf_ref, u_scr, *, rope):
    x = x_ref[...]
    ms = jnp.mean(x * x, axis=-1, keepdims=True)
    u = x * lax.rsqrt(ms + EPS) * nw_ref[...] * (1.0 + sc_ref[...]) + sh_ref[...]
    u_scr[...] = u.astype(BF16)

    def proj(lo, width):
        return _dot(u_scr[...], w_ref[:, lo:lo + width])

    tm = x.shape[0]
    if rope:
        lane = lax.broadcasted_iota(I32, (tm, LANES), 1)
        first = (lane & 32) == 0

    def put_rot(col, scale):
        for j in range(2):
            lo = col + j * LANES
            t = proj(lo, LANES)
            if scale != 1.0:
                t = t * scale
            if rope:
                tb = j * LANES
                rot = jnp.where(first, pltpu.roll(t, 96, axis=1), pltpu.roll(t, 32, axis=1))
                t = t * cos_ref[:, tb:tb + LANES] + rot * sin_ref[:, tb:tb + LANES]
            pb_ref[:, lo:lo + LANES] = t.astype(BF16)

    put_rot(C_RQ, 1.0)
    put_rot(C_RK, RET_DK ** -0.5)
    pb_ref[:, C_RV:C_RV + 512] = proj(C_RV, 512).astype(BF16)
    pb_ref[:, C_RG:C_RG + 512] = _silu(proj(C_RG, 512)).astype(BF16)
    pb_ref[:, C_HQ:C_HQ + 512] = _silu(proj(C_HQ, 512)).astype(BF16)
    pb_ref[:, C_HV:C_HV + 512] = proj(C_HV, 512).astype(BF16)
    pb_ref[:, C_HG:C_HG + 512] = _silu(proj(C_HG, 512)).astype(BF16)

    la = lb_ref[0]
    lbb = lb_ref[1]
    mx = jnp.maximum(la, lbb)
    ea = jnp.exp(la - mx)
    eb = jnp.exp(lbb - mx)
    lb = ea / (ea + eb)
    for d, col in enumerate((C_FF, C_FB)):
        lbd = lb[d:d + 1, :]
        f = lbd + (1.0 - lbd) * jax.nn.sigmoid(proj(col, 512))
        pb_ref[:, col:col + 512] = (1.0 - f).astype(BF16)
        lf_ref[:, d * 512:(d + 1) * 512] = jnp.log(f) * LOG2_E


def _inproj(x, sc, sh, mod_row, nw, w_bf, hg_lb, cos_t, sin_t, *, rope, tm):
    B, n, _ = x.shape
    nt = n // tm
    if mod_row is None:
        mrow = lambda b, j: (b, 0, 0)
    else:
        mrow = lambda b, j: (mod_row, 0, 0)
    return pl.pallas_call(
        functools.partial(_inproj_kernel, rope=rope),
        out_shape=(jax.ShapeDtypeStruct((B, n, PROJ_W), BF16),
                   jax.ShapeDtypeStruct((B, n, 1024), F32)),
        grid=(B, nt),
        in_specs=[pl.BlockSpec((None, tm, D_MODEL), lambda b, j: (b, j, 0)),
                  pl.BlockSpec((None, 1, D_MODEL), mrow),
                  pl.BlockSpec((None, 1, D_MODEL), mrow),
                  pl.BlockSpec((1, D_MODEL), lambda b, j: (0, 0)),
                  pl.BlockSpec((D_MODEL, PROJ_W), lambda b, j: (0, 0),
                               pipeline_mode=pl.Buffered(1)),
                  pl.BlockSpec((2, 2, 512), lambda b, j: (0, 0, 0)),
                  pl.BlockSpec((tm, 256), lambda b, j: (j, 0)),
                  pl.BlockSpec((tm, 256), lambda b, j: (j, 0))],
        out_specs=(pl.BlockSpec((None, tm, PROJ_W), lambda b, j: (b, j, 0)),
                   pl.BlockSpec((None, tm, 1024), lambda b, j: (b, j, 0))),
        scratch_shapes=[pltpu.VMEM((tm, D_MODEL), BF16)],
        compiler_params=_cparams(("arbitrary", "arbitrary")),
        name="inproj_rope" if rope else "inproj_ctx",
    )(x, sc, sh, nw, w_bf, hg_lb, cos_t, sin_t)


_RET_LGF = [math.log1p(-(2.0 ** (-5.0 - 2.0 * h))) for h in range(RET_HEADS)]
_RET_LGB = [math.log1p(-(2.0 ** (-6.0 - 2.0 * h))) for h in range(RET_HEADS)]


def _ret_kernel(q_ref, k_ref, v_ref, g_ref, kc_ref, vc_ref, o_ref,
                u_scr, sin_scr, kt_scr, dtot_scr, *, n_lat, n_ctx):
    L = CHUNK
    pair = pl.program_id(1)
    row = lax.broadcasted_iota(I32, (L, L), 0).astype(F32)
    col = lax.broadcasted_iota(I32, (L, L), 1).astype(F32)
    lane = lax.broadcasted_iota(I32, (L, LANES), 1)
    trow = lax.broadcasted_iota(I32, (L, 1), 0).astype(F32)
    tcol = lax.broadcasted_iota(I32, (1, L), 1).astype(F32)
    low_half = lane < RET_DK

    def u_chunk(k_blk, v_blk, ci, store_kt):
        kt = k_blk.astype(F32).T
        if store_kt is not None:
            kt_scr[store_kt] = kt.astype(BF16)
        for hh in range(2):
            lgf = jnp.where(pair == 0, _RET_LGF[hh], _RET_LGF[2 + hh])
            lgb = jnp.where(pair == 0, _RET_LGB[hh], _RET_LGB[2 + hh])
            kth = kt[hh * RET_DK:(hh + 1) * RET_DK, :]
            wkf = jnp.exp(lgf * (L - 1.0 - tcol))
            wkb = jnp.exp(lgb * tcol)
            lhs = jnp.concatenate([kth * wkf, kth * wkb], axis=0).astype(BF16)
            u_scr[hh, ci] = _dot(lhs, v_blk[:, hh * LANES:(hh + 1) * LANES])

    for c in range(n_ctx):
        u_chunk(kc_ref[c * L:(c + 1) * L, :], vc_ref[c * L:(c + 1) * L, :], c, None)

    def lat_u(c, carry):
        r0 = pl.multiple_of(c * L, L)
        u_chunk(k_ref[pl.ds(r0, L), :], v_ref[pl.ds(r0, L), :], n_ctx + c, c)
        return carry
    lax.fori_loop(0, n_lat, lat_u, 0, unroll=2)

    ones = jnp.ones((RET_DK, LANES), F32)
    for hh in range(2):
        lgf = jnp.where(pair == 0, _RET_LGF[hh], _RET_LGF[2 + hh])
        lgb = jnp.where(pair == 0, _RET_LGB[hh], _RET_LGB[2 + hh])
        d = row - col
        dtot_scr[hh] = jnp.where(d > 0, jnp.exp(lgf * jnp.maximum(d, 0.0)),
                                 jnp.where(d < 0, jnp.exp(lgb * jnp.maximum(-d, 0.0)), 2.0))
        af = jnp.exp(ones * (lgf * L))
        ab = jnp.exp(ones * (lgb * L))

        s = jnp.zeros((RET_DK, LANES), F32)
        for c in range(n_ctx):
            s = af * s + u_scr[hh, c, 0:RET_DK, :]
        sb = jnp.zeros((RET_DK, LANES), F32)
        for c in reversed(range(n_ctx)):
            sb = ab * sb + u_scr[hh, c, RET_DK:2 * RET_DK, :]

        def fwd(c, s, hh=hh, af=af):
            sin_scr[hh, c, 0:RET_DK, :] = s.astype(BF16)
            return af * s + u_scr[hh, n_ctx + c, 0:RET_DK, :]
        lax.fori_loop(0, n_lat, fwd, s)

        def bwd(i, sb, hh=hh, ab=ab):
            c = n_lat - 1 - i
            sin_scr[hh, c, RET_DK:2 * RET_DK, :] = sb.astype(BF16)
            return ab * sb + u_scr[hh, n_ctx + c, RET_DK:2 * RET_DK, :]
        lax.fori_loop(0, n_lat, bwd, sb)

    def out_chunk(c, carry):
        r0 = pl.multiple_of(c * L, L)
        q = q_ref[pl.ds(r0, L), :].astype(F32)
        qr = pltpu.roll(q, RET_DK, axis=1)
        kt = kt_scr[c]
        for hh in range(2):
            lgf = jnp.where(pair == 0, _RET_LGF[hh], _RET_LGF[2 + hh])
            lgb = jnp.where(pair == 0, _RET_LGB[hh], _RET_LGB[2 + hh])
            mine = low_half if hh == 0 else jnp.logical_not(low_half)
            qm = jnp.where(mine, q, 0.0).astype(BF16)
            p = (_dot(qm, kt) * dtot_scr[hh]).astype(BF16)
            vh = v_ref[pl.ds(r0, L), hh * LANES:(hh + 1) * LANES]
            wqf = jnp.exp(lgf * (trow + 1.0))
            wqb = jnp.exp(lgb * (L - trow))
            qa, qb = (q, qr) if hh == 0 else (qr, q)
            qs = jnp.where(low_half, qa * wqf, qb * wqb).astype(BF16)
            o = _dot(p, vh) + _dot(qs, sin_scr[hh, c])
            ms = jnp.mean(o * o, axis=-1, keepdims=True)
            gh = g_ref[pl.ds(r0, L), hh * LANES:(hh + 1) * LANES].astype(F32)
            o_ref[pl.ds(r0, L), hh * LANES:(hh + 1) * LANES] = (
                o * lax.rsqrt(ms + EPS) * gh).astype(BF16)
        return carry
    lax.fori_loop(0, n_lat, out_chunk, 0, unroll=2)


def _retention(pb, pbc):
    B, n, _ = pb.shape
    nc = pbc.shape[1]
    n_lat, n_ctx = n // CHUNK, nc // CHUNK
    return pl.pallas_call(
        functools.partial(_ret_kernel, n_lat=n_lat, n_ctx=n_ctx),
        out_shape=jax.ShapeDtypeStruct((B, n, 512), BF16),
        grid=(B, 2),
        in_specs=[pl.BlockSpec((None, n, LANES), lambda b, p: (b, 0, C_RQ // LANES + p)),
                  pl.BlockSpec((None, n, LANES), lambda b, p: (b, 0, C_RK // LANES + p)),
                  pl.BlockSpec((None, n, 256), lambda b, p: (b, 0, C_RV // 256 + p)),
                  pl.BlockSpec((None, n, 256), lambda b, p: (b, 0, C_RG // 256 + p)),
                  pl.BlockSpec((None, nc, LANES), lambda b, p: (b, 0, C_RK // LANES + p)),
                  pl.BlockSpec((None, nc, 256), lambda b, p: (b, 0, C_RV // 256 + p))],
        out_specs=pl.BlockSpec((None, n, 256), lambda b, p: (b, 0, p)),
        scratch_shapes=[pltpu.VMEM((2, n_lat + n_ctx, CHUNK, LANES), F32),
                        pltpu.VMEM((2, n_lat, CHUNK, LANES), BF16),
                        pltpu.VMEM((n_lat, LANES, CHUNK), BF16),
                        pltpu.VMEM((2, CHUNK, CHUNK), F32)],
        compiler_params=_cparams(("arbitrary", "arbitrary")),
        name="retention",
    )(pb, pb, pb, pb, pbc, pbc)


_LEVELS = (64, 32, 16, 8, 4, 2, 1)


def _expand_rows(r, rep):
    n = r.shape[0]
    if n == 1:
        return jnp.broadcast_to(r, (rep, r.shape[1]))
    return jnp.concatenate(
        [jnp.broadcast_to(r[i:i + 1, :], (rep, r.shape[1])) for i in range(n)], axis=0)


def _hgrn_kernel(q_ref, kf_ref, kb_ref, v_ref, g_ref, lff_ref, lfb_ref,
                 kfc_ref, kbc_ref, vc_ref, lffc_ref, lfbc_ref, nw_ref, o_ref,
                 ut_scr, a_scr, qs_scr, oi_scr, sin_scr, bfb_scr, *, n_lat, n_ctx):
    L = CHUNK
    row = lax.broadcasted_iota(I32, (L, L), 0)
    col = lax.broadcasted_iota(I32, (L, L), 1)
    xr_bits = lax.bitcast_convert_type((row ^ col).astype(F32), I32)
    lv = lax.shift_right_logical(xr_bits, 23) - 127
    row2 = lax.broadcasted_iota(I32, (L, 2 * L), 0)
    col2 = lax.broadcasted_iota(I32, (L, 2 * L), 1) & (L - 1)
    tril2 = jnp.where(col2 <= row2, 1.0, 0.0).astype(BF16)
    triu2 = jnp.where(col2 >= row2, 1.0, 0.0).astype(BF16)

    def cums(lff, lfb):
        hf, lof = _split_bf16(lff)
        hb, lob = _split_bf16(lfb)
        bf = _dot(tril2, jnp.concatenate([hf, lof], axis=0))
        bb = _dot(triu2, jnp.concatenate([hb, lob], axis=0))
        return bf, bb

    def state_part(ci, kf, kb, v_blk, bf, bb):
        endf = bf[L - 1:L, :]
        endb = bb[0:1, :]
        ksf = kf * jnp.exp2(endf - bf)
        ksb = kb * jnp.exp2(endb - bb)
        vt = v_blk.astype(F32).T.astype(BF16)
        ut_scr[ci] = _dot(vt, jnp.concatenate([ksf, ksb], axis=1).astype(BF16))
        a_scr[ci] = jnp.broadcast_to(
            jnp.concatenate([jnp.exp2(endf), jnp.exp2(endb)], axis=1), (8, 2 * LANES))

    for c in range(n_ctx):
        sl = slice(c * L, (c + 1) * L)
        bf, bb = cums(lffc_ref[sl, :], lfbc_ref[sl, :])
        state_part(c, kfc_ref[sl, :].astype(F32), kbc_ref[sl, :].astype(F32),
                   vc_ref[sl, :], bf, bb)

    def lat_chunk(c, u):
        rows = pl.ds(pl.multiple_of(c * L, L), L)
        bf_scr = bfb_scr.at[u, 0]
        bb_scr = bfb_scr.at[u, 1]
        q = q_ref[rows, :].astype(F32)
        kf = kf_ref[rows, :].astype(F32)
        kb = kb_ref[rows, :].astype(F32)
        v_blk = v_ref[rows, :]
        lff = lff_ref[rows, :]
        lfb = lfb_ref[rows, :]
        bf, bb = cums(lff, lfb)
        state_part(n_ctx + c, kf, kb, v_blk, bf, bb)
        qs_scr[rows, :] = jnp.concatenate([q * jnp.exp2(bf), q * jnp.exp2(bb)],
                                          axis=1).astype(BF16)
        bf_scr[...] = bf
        bb_scr[...] = bb

        acc = jnp.zeros((L, L), F32)
        for lvl, h in enumerate(_LEVELS):
            bit = (row & h) != 0
            ksel = jnp.where(bit, kb, kf)
            if h >= 4:
                n = (L // 2) // h
                if n == 1:
                    rf = bf_scr[h - 1:h, :]
                    rb = bb_scr[h:h + 1, :]
                else:
                    rf = bf_scr[pl.ds(h - 1, n, stride=2 * h), :]
                    rb = bb_scr[pl.ds(h, n, stride=2 * h), :]
                df = bf - _expand_rows(rf, 2 * h)
                db = bb - _expand_rows(rb, 2 * h)
                eq = jnp.where(bit, df, db)
                ek = -jnp.where(bit, db, df)
            elif h == 2:
                m = row & 3
                lff_n = pltpu.roll(lff, L - 1, axis=0)
                lfb_n = pltpu.roll(lfb, L - 1, axis=0)
                eq = jnp.where(m == 2, lff,
                               jnp.where(m == 3, lff + pltpu.roll(lff, 1, axis=0),
                                         jnp.where(m == 0, lfb + lfb_n, lfb)))
                ek = jnp.where(m == 3, pltpu.roll(lfb, 1, axis=0),
                               jnp.where(m == 0, lff_n, 0.0))
            else:
                eq = jnp.where(bit, lff, lfb)
                ek = None
            lhs = (q * jnp.exp2(eq)).astype(BF16)
            rhs = (ksel if ek is None else ksel * jnp.exp2(ek)).astype(BF16)
            acc = jnp.where(lv == 6 - lvl, _dot_nt(lhs, rhs), acc)

        dsum = jnp.sum(q * (kf + kb), axis=-1, keepdims=True)
        oi_scr[rows, :] = _dot(acc.astype(BF16), v_blk) + dsum * v_blk.astype(F32)

    def lat_pair(i, carry):
        for u in range(4):
            lat_chunk(4 * i + u, u)
        return carry
    lax.fori_loop(0, n_lat // 4, lat_pair, 0)

    st = jnp.zeros((LANES, LANES), F32)
    for c in range(n_ctx):
        st = st * a_scr[c, 0:1, 0:LANES] + ut_scr[c, :, 0:LANES]
    stb = jnp.zeros((LANES, LANES), F32)
    for c in reversed(range(n_ctx)):
        stb = stb * a_scr[c, 0:1, LANES:2 * LANES] + ut_scr[c, :, LANES:2 * LANES]

    def fwd(c, st):
        sin_scr[c, :, 0:LANES] = st.astype(BF16)
        ci = n_ctx + c
        return st * a_scr[ci, 0:1, 0:LANES] + ut_scr[ci, :, 0:LANES]
    lax.fori_loop(0, n_lat, fwd, st)

    def bwd(i, stb):
        c = n_lat - 1 - i
        sin_scr[c, :, LANES:2 * LANES] = stb.astype(BF16)
        ci = n_ctx + c
        return stb * a_scr[ci, 0:1, LANES:2 * LANES] + ut_scr[ci, :, LANES:2 * LANES]
    lax.fori_loop(0, n_lat, bwd, stb)

    def out_chunk(c, carry):
        rows = pl.ds(pl.multiple_of(c * L, L), L)
        o = oi_scr[rows, :] + _dot_nt(qs_scr[rows, :], sin_scr[c])
        ms = jnp.mean(o * o, axis=-1, keepdims=True)
        y = o * lax.rsqrt(ms + EPS) * nw_ref[...] * g_ref[rows, :].astype(F32)
        o_ref[rows, :] = y.astype(BF16)
        return carry
    lax.fori_loop(0, n_lat, out_chunk, 0, unroll=4)


def _hgrn(pb, lf, pbc, lfc, nw):
    B, n, _ = pb.shape
    nc = pbc.shape[1]
    n_lat, n_ctx = n // CHUNK, nc // CHUNK

    def colblk(rows, col0):
        return pl.BlockSpec((None, rows, LANES), lambda b, h: (b, 0, col0 // LANES + h))

    return pl.pallas_call(
        functools.partial(_hgrn_kernel, n_lat=n_lat, n_ctx=n_ctx),
        out_shape=jax.ShapeDtypeStruct((B, n, 512), BF16),
        grid=(B, HG_HEADS),
        in_specs=[colblk(n, C_HQ), colblk(n, C_FF), colblk(n, C_FB), colblk(n, C_HV),
                  colblk(n, C_HG), colblk(n, 0), colblk(n, 512),
                  colblk(nc, C_FF), colblk(nc, C_FB), colblk(nc, C_HV),
                  colblk(nc, 0), colblk(nc, 512),
                  pl.BlockSpec((1, LANES), lambda b, h: (0, 0))],
        out_specs=pl.BlockSpec((None, n, LANES), lambda b, h: (b, 0, h)),
        scratch_shapes=[pltpu.VMEM((n_lat + n_ctx, LANES, 2 * LANES), F32),
                        pltpu.VMEM((n_lat + n_ctx, 8, 2 * LANES), F32),
                        pltpu.VMEM((n, 2 * LANES), BF16),
                        pltpu.VMEM((n, LANES), F32),
                        pltpu.VMEM((n_lat, LANES, 2 * LANES), BF16),
                        pltpu.VMEM((4, 2, CHUNK, LANES), F32)],
        compiler_params=_cparams(("arbitrary", "arbitrary")),
        name="hgrn2",
    )(pb, pb, pb, pb, pb, lf, lf, pbc, pbc, pbc, lfc, lfc, nw)


def _outproj_kernel(ar_ref, ah_ref, w_ref, x_ref, g1_ref, sc_ref, sh_ref, nw_ref,
                    rw_ref, rb_ref, h1_ref, v_ref, route_ref, route_t_ref, base_ref, cnt_ref, cnt_scr):
    first_step = jnp.logical_and(pl.program_id(0) == 0, pl.program_id(1) == 0)

    @pl.when(first_step)
    def _():
        cnt_scr[...] = jnp.zeros_like(cnt_scr)

    y = _dot(ar_ref[...], w_ref[0:512, :]) + _dot(ah_ref[...], w_ref[512:1024, :])
    h1 = x_ref[...] + g1_ref[...] * y
    _store_tile_rows(h1_ref, h1)
    ms = jnp.mean(h1 * h1, axis=-1, keepdims=True)
    v = h1 * lax.rsqrt(ms + EPS) * nw_ref[...] * (1.0 + sc_ref[...]) + sh_ref[...]
    _store_tile_rows(v_ref, v)

    tm = v.shape[0]
    logits = _dot3(v, rw_ref[...]) + rb_ref[...]
    lane = lax.broadcasted_iota(I32, (tm, LANES), 1)
    lane_f = lane.astype(F32)
    l = logits
    sels, tops, idxs = [], [], []
    for _ in range(TOP_K):
        m = jnp.max(l, axis=1, keepdims=True)
        i = jnp.min(jnp.where(l == m, lane_f, float(LANES)), axis=1, keepdims=True)
        sel = lane_f == i
        l = jnp.where(sel, -jnp.inf, l)
        sels.append(sel)
        tops.append(m)
        idxs.append(i)
    es = [jnp.exp(t - tops[0]) for t in tops]
    den = es[0] + es[1] + es[2] + es[3]
    gates = [e / den for e in es]

    oh = jnp.zeros((tm, LANES), F32)
    for sel in sels:
        oh = jnp.where(sel, 1.0, oh)
    r = lax.broadcasted_iota(I32, (tm, tm), 0)
    c = lax.broadcasted_iota(I32, (tm, tm), 1)
    tri = jnp.where(c < r, 1.0, 0.0).astype(BF16)
    before = _dot(tri, oh.astype(BF16)) + cnt_scr[0:1, :]
    ranks = [jnp.sum(jnp.where(sel, before, 0.0), axis=1, keepdims=True) for sel in sels]
    base = cnt_scr[0:1, :]
    pieces = []
    for p in range(tm // COMBINE_TILE):
        pieces.append(base)
        base = base + jnp.sum(oh[p * COMBINE_TILE:(p + 1) * COMBINE_TILE, :], axis=0, keepdims=True)
    pieces.append(jnp.zeros((8 - len(pieces), LANES), F32))
    base_ref[...] = jnp.concatenate(pieces, axis=0)
    cnt_scr[...] = jnp.broadcast_to(base, cnt_scr.shape)
    cnt_ref[...] = cnt_scr[...]

    out = jnp.zeros((tm, LANES), F32)
    for k in range(TOP_K):
        out = jnp.where(lane == k, idxs[k], out)
        out = jnp.where(lane == TOP_K + k, gates[k], out)
        out = jnp.where(lane == 2 * TOP_K + k, ranks[k], out)
    route_ref[...] = out
    route_t_ref[...] = out.T[0:16, :]


def _outproj(a_ret, a_hg, w_bf, x, g1, sc2, sh2, nw, rw, rb, *, tm):
    B, n, _ = x.shape
    nt = n // tm
    T = B * n
    mrow = lambda b, j: (b, 0, 0)
    tok = lambda b, j: (b * nt + j, 0)
    return pl.pallas_call(
        _outproj_kernel,
        out_shape=(jax.ShapeDtypeStruct((T * ROW_TILES, LANES), F32),
                   jax.ShapeDtypeStruct((T * ROW_TILES, LANES), F32),
                   jax.ShapeDtypeStruct((T, LANES), F32),
                   jax.ShapeDtypeStruct((16, T), F32),
                   jax.ShapeDtypeStruct((B * nt * 8, LANES), F32),
                   jax.ShapeDtypeStruct((8, LANES), F32)),
        grid=(B, nt),
        in_specs=[pl.BlockSpec((None, tm, 512), lambda b, j: (b, j, 0)),
                  pl.BlockSpec((None, tm, 512), lambda b, j: (b, j, 0)),
                  pl.BlockSpec((D_MODEL, D_MODEL), lambda b, j: (0, 0)),
                  pl.BlockSpec((None, tm, D_MODEL), lambda b, j: (b, j, 0)),
                  pl.BlockSpec((None, 1, D_MODEL), mrow),
                  pl.BlockSpec((None, 1, D_MODEL), mrow),
                  pl.BlockSpec((None, 1, D_MODEL), mrow),
                  pl.BlockSpec((1, D_MODEL), lambda b, j: (0, 0)),
                  pl.BlockSpec((D_MODEL, LANES), lambda b, j: (0, 0)),
                  pl.BlockSpec((1, LANES), lambda b, j: (0, 0))],
        out_specs=(pl.BlockSpec((tm * ROW_TILES, LANES), tok),
                   pl.BlockSpec((tm * ROW_TILES, LANES), tok),
                   pl.BlockSpec((tm, LANES), tok),
                   pl.BlockSpec((16, tm), lambda b, j: (0, b * nt + j)),
                   pl.BlockSpec((8, LANES), tok),
                   pl.BlockSpec((8, LANES), lambda b, j: (0, 0))),
        scratch_shapes=[pltpu.VMEM((8, LANES), F32)],
        compiler_params=_cparams(("arbitrary", "arbitrary")),
        name="outproj_router",
    )(a_ret, a_hg, w_bf, x, g1, sc2, sh2, nw, rw, rb)


def _dispatch_kernel(dest_hbm, v_ref, xs_hbm, idx_smem, zbuf, sem_idx, sem_z, sem_rows,
                     *, tile, n_tiles, n_rows):
    i = pl.program_id(0)
    slot = lax.rem(i, 2)

    def idx_copy(j, s):
        n_asg = tile * TOP_K
        return pltpu.make_async_copy(dest_hbm.at[j], idx_smem.at[pl.ds(s * n_asg, n_asg)],
                                     sem_idx.at[s])

    def zero_copy(j):
        r0 = (n_rows + j * SUB_ROWS) * ROW_TILES
        return pltpu.make_async_copy(zbuf, xs_hbm.at[pl.ds(r0, SUB_ROWS * ROW_TILES), :], sem_z)

    @pl.when(i == 0)
    def _():
        idx_copy(0, 0).start()
        zbuf[...] = jnp.zeros_like(zbuf)
        for j in range(ROW_BLOCK // SUB_ROWS):
            zero_copy(j).start()
        for j in range(ROW_BLOCK // SUB_ROWS):
            zero_copy(j).wait()

    idx_copy(i, slot).wait()

    @pl.when(i + 1 < n_tiles)
    def _():
        idx_copy(i + 1, 1 - slot).start()

    def issue(t, carry):
        for k in range(TOP_K):
            d = pl.multiple_of(idx_smem[slot * (tile * TOP_K) + k * tile + t], ROW_TILES)
            src = v_ref.at[pl.ds(pl.multiple_of(t * ROW_TILES, ROW_TILES), ROW_TILES), :]
            pltpu.make_async_copy(src, xs_hbm.at[pl.ds(d, ROW_TILES), :],
                                  sem_rows).start(priority=k % 2)
        return carry
    lax.fori_loop(0, tile, issue, 0, unroll=4)
    for _ in range(TOP_K):
        pltpu.make_async_copy(v_ref, xs_hbm.at[pl.ds(0, tile * ROW_TILES), :], sem_rows).wait()


def _dispatch(dest_tiles, v, *, tile):
    T = v.shape[0] // ROW_TILES
    n_rows = T * TOP_K
    return pl.pallas_call(
        functools.partial(_dispatch_kernel, tile=tile, n_tiles=T // tile, n_rows=n_rows),
        out_shape=jax.ShapeDtypeStruct(((n_rows + ROW_BLOCK) * ROW_TILES, LANES), F32),
        grid=(T // tile,),
        in_specs=[pl.BlockSpec(memory_space=pl.ANY),
                  pl.BlockSpec((tile * ROW_TILES, LANES), lambda i: (i, 0))],
        out_specs=pl.BlockSpec(memory_space=pl.ANY),
        scratch_shapes=[pltpu.SMEM((2 * tile * TOP_K,), I32),
                        pltpu.VMEM((SUB_ROWS * ROW_TILES, LANES), F32),
                        pltpu.SemaphoreType.DMA((2,)),
                        pltpu.SemaphoreType.DMA,
                        pltpu.SemaphoreType.DMA],
        compiler_params=_cparams(("arbitrary",)),
        name="dispatch",
    )(dest_tiles, v)


def _moe_kernel(be_ref, ns_ref, nx_ref, xr_ref, x_hbm, w1_hbm, b1_ref, w2_hbm, b2_ref, y_ref,
                w1_stage, w2_stage, w1_scr, w2_scr, xbuf, x_scr, act_scr, sem_w, sem_x,
                *, n_blocks):
    i = pl.program_id(0)
    slot = lax.rem(i, 2)
    e = be_ref[i]
    nsub = ns_ref[i]
    changed = jnp.logical_or(i == 0, e != be_ref[jnp.maximum(i - 1, 0)])

    def x_copy(j, s):
        r0 = pl.multiple_of(xr_ref[j], ROW_TILES)
        return pltpu.make_async_copy(x_hbm.at[pl.ds(r0, ROW_BLOCK * ROW_TILES), :], xbuf.at[s],
                                     sem_x.at[s])

    @pl.when(i == 0)
    def _():
        x_copy(0, 0).start()

    @pl.when(nsub > 0)
    def _():
        x_copy(i, slot).wait()

    nxt_blk = jnp.minimum(i + 1, n_blocks - 1)

    @pl.when(jnp.logical_and(i + 1 < n_blocks, ns_ref[nxt_blk] > 0))
    def _():
        x_copy(nxt_blk, 1 - slot).start()

    def weight_copies(ex):
        return (pltpu.make_async_copy(w1_hbm.at[ex], w1_stage, sem_w.at[0]),
                pltpu.make_async_copy(w2_hbm.at[ex], w2_stage, sem_w.at[1]))

    @pl.when(i == 0)
    def _():
        for cp in weight_copies(e):
            cp.start()

    @pl.when(jnp.logical_and(changed, nsub > 0))
    def _():
        for cp in weight_copies(e):
            cp.wait()
        w1_scr[...] = w1_stage[...].astype(BF16)
        w2_scr[...] = w2_stage[...].astype(BF16)
        nxt = nx_ref[e]

        @pl.when(nxt >= 0)
        def _():
            for cp in weight_copies(nxt):
                cp.start()

    def compute(rows):
        x_scr[0:rows, :] = _load_tile_rows(xbuf.at[slot], rows).astype(BF16)
        cw = 256
        for c in range(D_FF // cw):
            glu = (_dot(x_scr[0:rows, :], w1_scr[:, c * cw:(c + 1) * cw])
                   + b1_ref[:, c * cw:(c + 1) * cw])
            lin = (_dot(x_scr[0:rows, :], w1_scr[:, D_FF + c * cw:D_FF + (c + 1) * cw])
                   + b1_ref[:, D_FF + c * cw:D_FF + (c + 1) * cw])
            glu = jnp.minimum(glu, SWIGLU_LIMIT)
            lin = jnp.clip(lin, -SWIGLU_LIMIT, SWIGLU_LIMIT)
            act = glu * jax.nn.sigmoid(SWIGLU_ALPHA * glu) * (lin + 1.0)
            act_scr[0:rows, c * cw:(c + 1) * cw] = act.astype(BF16)
        for c in range(D_MODEL // cw):
            y = (_dot(act_scr[0:rows, :], w2_scr[:, c * cw:(c + 1) * cw])
                 + b2_ref[:, c * cw:(c + 1) * cw])
            for s in range(cw // LANES):
                t = c * (cw // LANES) + s
                y_ref[pl.ds(t, rows, stride=ROW_TILES), :] = y[:, s * LANES:(s + 1) * LANES]
        if rows < ROW_BLOCK:
            y_ref[rows * ROW_TILES:, :] = jnp.zeros(((ROW_BLOCK - rows) * ROW_TILES, LANES), F32)

    for m in range(1, ROW_BLOCK // SUB_ROWS + 1):
        @pl.when(nsub == m)
        def _(m=m):
            compute(m * SUB_ROWS)

    @pl.when(nsub == 0)
    def _():
        y_ref[...] = jnp.zeros_like(y_ref)


def _moe(block_e, nsub, next_e, xrow, xs, w1, b1, w2, b2, *, n_blocks):
    return pl.pallas_call(
        functools.partial(_moe_kernel, n_blocks=n_blocks),
        out_shape=jax.ShapeDtypeStruct((n_blocks * ROW_BLOCK * ROW_TILES, LANES), F32),
        grid_spec=pltpu.PrefetchScalarGridSpec(
            num_scalar_prefetch=4,
            grid=(n_blocks,),
            in_specs=[pl.BlockSpec(memory_space=pl.ANY),
                      pl.BlockSpec(memory_space=pl.ANY),
                      pl.BlockSpec((None, 1, 2 * D_FF), lambda i, be, ns, nx, xr: (be[i], 0, 0)),
                      pl.BlockSpec(memory_space=pl.ANY),
                      pl.BlockSpec((None, 1, D_MODEL), lambda i, be, ns, nx, xr: (be[i], 0, 0))],
            out_specs=pl.BlockSpec((ROW_BLOCK * ROW_TILES, LANES),
                                   lambda i, be, ns, nx, xr: (i, 0)),
            scratch_shapes=[pltpu.VMEM((D_MODEL, 2 * D_FF), F32),
                            pltpu.VMEM((D_FF, D_MODEL), F32),
                            pltpu.VMEM((D_MODEL, 2 * D_FF), BF16),
                            pltpu.VMEM((D_FF, D_MODEL), BF16),
                            pltpu.VMEM((2, ROW_BLOCK * ROW_TILES, LANES), F32),
                            pltpu.VMEM((ROW_BLOCK, D_MODEL), BF16),
                            pltpu.VMEM((ROW_BLOCK, D_FF), BF16),
                            pltpu.SemaphoreType.DMA((2,)),
                            pltpu.SemaphoreType.DMA((2,))]),
        compiler_params=_cparams(("arbitrary",)),
        name="moe_ffn",
    )(block_e, nsub, next_e, xrow, xs, w1, b1, w2, b2)


def _combine_kernel(pos_hbm, gate_hbm, src_hbm, yb_hbm, h1_ref, g2_ref, nw_ref, o_ref,
                    pos_smem, gate_smem, src_smem, win, hbuf, sem_tab, sem_win, *, n_tiles):
    i = pl.program_id(0)
    slot = lax.rem(i, 2)
    tile = COMBINE_TILE
    n_asg = tile * TOP_K
    chunk = WIN_ROWS * ROW_TILES
    slot_rows = WIN_CHUNKS * chunk

    def table_copies(j, s):
        return (pltpu.make_async_copy(pos_hbm.at[j], pos_smem.at[pl.ds(s * n_asg, n_asg)],
                                      sem_tab.at[s]),
                pltpu.make_async_copy(gate_hbm.at[j], gate_smem.at[pl.ds(s * n_asg, n_asg)],
                                      sem_tab.at[s]),
                pltpu.make_async_copy(src_hbm.at[j], src_smem.at[pl.ds(s * n_asg, n_asg)],
                                      sem_tab.at[s]))

    def fetch_window(s):
        def one(c, carry):
            src = pl.multiple_of(src_smem[s * n_asg + c], ROW_TILES)
            dst = pl.multiple_of(s * slot_rows + c * chunk, chunk)
            pltpu.make_async_copy(yb_hbm.at[pl.ds(src, chunk), :], win.at[pl.ds(dst, chunk), :],
                                  sem_win.at[s]).start()
            return carry
        lax.fori_loop(0, WIN_CHUNKS, one, 0, unroll=2)

    @pl.when(i == 0)
    def _():
        for cp in table_copies(0, 0):
            cp.start()
        for cp in table_copies(0, 0):
            cp.wait()
        fetch_window(0)
        if n_tiles > 1:
            for cp in table_copies(1, 1):
                cp.start()

    @pl.when(i + 1 < n_tiles)
    def _():
        for cp in table_copies(i + 1, 1 - slot):
            cp.wait()
        fetch_window(1 - slot)

    pltpu.make_async_copy(yb_hbm.at[pl.ds(0, slot_rows), :],
                          win.at[pl.ds(pl.multiple_of(slot * slot_rows, slot_rows), slot_rows), :],
                          sem_win.at[slot]).wait()

    g2 = g2_ref[...]
    base = slot * n_asg

    def token(t, carry):
        e0 = base + t * TOP_K
        acc = None
        for k in range(TOP_K):
            p = pl.multiple_of(pos_smem[e0 + k], ROW_TILES)
            term = gate_smem[e0 + k] * win[pl.ds(p, ROW_TILES), :]
            acc = term if acc is None else acc + term
        r0 = pl.multiple_of(t * ROW_TILES, ROW_TILES)
        hbuf[pl.ds(r0, ROW_TILES), :] = h1_ref[pl.ds(r0, ROW_TILES), :] + g2 * acc
        return carry
    lax.fori_loop(0, tile, token, 0, unroll=4)

    @pl.when(i + 2 < n_tiles)
    def _():
        for cp in table_copies(i + 2, slot):
            cp.start()

    h = _load_tile_rows(hbuf, tile)
    ms = jnp.mean(h * h, axis=-1, keepdims=True)
    o_ref[...] = h * lax.rsqrt(ms + EPS) * nw_ref[...]


def _combine(pos_tiles, gate_tiles, src_tiles, yb, h1, g2_tiles, nw, *, tiles_per_batch):
    tile = COMBINE_TILE
    T = h1.shape[0] // ROW_TILES
    n_asg = tile * TOP_K
    return pl.pallas_call(
        functools.partial(_combine_kernel, n_tiles=T // tile),
        out_shape=jax.ShapeDtypeStruct((T, D_MODEL), F32),
        grid=(T // tile,),
        in_specs=[pl.BlockSpec(memory_space=pl.ANY),
                  pl.BlockSpec(memory_space=pl.ANY),
                  pl.BlockSpec(memory_space=pl.ANY),
                  pl.BlockSpec(memory_space=pl.ANY),
                  pl.BlockSpec((tile * ROW_TILES, LANES), lambda i: (i, 0)),
                  pl.BlockSpec((None, ROW_TILES, LANES), lambda i: (i // tiles_per_batch, 0, 0)),
                  pl.BlockSpec((1, D_MODEL), lambda i: (0, 0))],
        out_specs=pl.BlockSpec((tile, D_MODEL), lambda i: (i, 0)),
        scratch_shapes=[pltpu.SMEM((2 * n_asg,), I32),
                        pltpu.SMEM((2 * n_asg,), F32),
                        pltpu.SMEM((2 * n_asg,), I32),
                        pltpu.VMEM((2 * WIN_CHUNKS * WIN_ROWS * ROW_TILES, LANES), F32),
                        pltpu.VMEM((tile * ROW_TILES, LANES), F32),
                        pltpu.SemaphoreType.DMA((2,)),
                        pltpu.SemaphoreType.DMA((2,))],
        compiler_params=_cparams(("arbitrary",)),
        name="combine_norm",
    )(pos_tiles, gate_tiles, src_tiles, yb, h1, g2_tiles, nw)


def _rope_tables(n):
    rows = n // GRID_W
    row = jnp.repeat(jnp.arange(rows, dtype=F32), GRID_W)
    col = jnp.tile(jnp.arange(GRID_W, dtype=F32), rows)
    n_freq = RET_DK // 4
    inv = ROPE_BASE ** (-jnp.arange(n_freq, dtype=F32) / n_freq)
    ang = jnp.concatenate([row[:, None] * inv, col[:, None] * inv], axis=-1)
    cos, sin = jnp.cos(ang), jnp.sin(ang)
    cos_h = jnp.concatenate([cos, cos], axis=-1)
    sin_h = jnp.concatenate([-sin, sin], axis=-1)
    return jnp.tile(cos_h, (1, RET_HEADS)), jnp.tile(sin_h, (1, RET_HEADS))


def kernel(x, c, ctx, c_ctx, w_ada, b_ada, norm_mix_w, norm_ffn_w, w_in, w_out, hg_lb,
           hg_norm_w, router_w, router_b, w1, b1, w2, b2, norm_final_w):
    B, N, D = x.shape
    C = ctx.shape[1]
    T = B * N
    assert D == D_MODEL and w_ada.shape[0] == 1

    cc = jnp.concatenate([c.astype(F32), c_ctx.astype(F32)[None, :],
                          jnp.zeros((16 - B - 1, D), F32)], axis=0)
    mod = _ada(cc, w_ada[0], b_ada[0][None, :])
    mod = mod.reshape(16, 6, 1, D).transpose(1, 0, 2, 3)
    sh1, sc1, g1, sh2, sc2, g2 = (mod[i] for i in range(6))

    w_in_bf = w_in[0].astype(BF16)
    w_out_bf = w_out[0].astype(BF16)
    nw_mix = norm_mix_w[0][None, :]
    cos_t, sin_t = _rope_tables(N)

    pb, lf = _inproj(x, sc1, sh1, None, nw_mix, w_in_bf, hg_lb[:2], cos_t, sin_t,
                     rope=True, tm=1024)
    pbc, lfc = _inproj(ctx, sc1, sh1, B, nw_mix, w_in_bf, hg_lb[:2], cos_t[:C], sin_t[:C],
                       rope=False, tm=C)

    a_ret = _retention(pb, pbc)
    a_hg = _hgrn(pb, lf, pbc, lfc, hg_norm_w[0][None, :])

    rw = jnp.zeros((D, LANES), F32).at[:, :N_EXPERTS].set(router_w[0])
    rb = jnp.full((1, LANES), -1e30, F32).at[0, :N_EXPERTS].set(router_b[0])
    h1, v, route, route_t, base_cnt, cnt = _outproj(a_ret, a_hg, w_out_bf, x, g1, sc2, sh2,
                                          norm_ffn_w[0][None, :], rw, rb, tm=512)

    del route
    idx = route_t[0:TOP_K].astype(I32)
    gates = route_t[TOP_K:2 * TOP_K]
    rank = route_t[2 * TOP_K:3 * TOP_K].astype(I32)
    counts = cnt[0, :N_EXPERTS].astype(I32)
    padded = (counts + ROW_BLOCK - 1) // ROW_BLOCK * ROW_BLOCK
    pend = jnp.cumsum(padded)
    pstart = pend - padded
    cstart = jnp.cumsum(counts) - counts
    onehot = idx[None] == jnp.arange(N_EXPERTS, dtype=I32)[:, None, None]

    def dest_tiles(first_row, tile):
        dest = (jnp.sum(jnp.where(onehot, first_row[:, None, None], 0), axis=0) + rank) * ROW_TILES
        return dest.reshape(TOP_K, T // tile, tile).transpose(1, 0, 2).reshape(T // tile,
                                                                               TOP_K * tile)
    n_blocks = (T * TOP_K) // ROW_BLOCK + N_EXPERTS
    starts = jnp.arange(n_blocks, dtype=I32) * ROW_BLOCK
    block_e = jnp.minimum(jnp.sum((pend[None, :] <= starts[:, None]).astype(I32), axis=1),
                          N_EXPERTS - 1)
    valid = jnp.clip(counts[block_e] - (starts - pstart[block_e]), 0, ROW_BLOCK)
    nsub = (valid + SUB_ROWS - 1) // SUB_ROWS
    first_from = lax.cummin(jnp.where(counts > 0, jnp.arange(N_EXPERTS, dtype=I32), N_EXPERTS),
                            reverse=True)
    next_e = jnp.concatenate([first_from[1:], jnp.full((1,), N_EXPERTS, I32)])
    next_e = jnp.where(next_e == N_EXPERTS, -1, next_e)
    xrow = jnp.where(nsub > 0, cstart[block_e] + starts - pstart[block_e], 0) * ROW_TILES

    tile_d = 512
    xs = _dispatch(dest_tiles(cstart, tile_d), v, tile=tile_d)
    yb = _moe(block_e, nsub, next_e, xrow.astype(I32), xs, w1[0], b1[0][:, None, :], w2[0],
              b2[0][:, None, :], n_blocks=n_blocks)

    tc = COMBINE_TILE
    n_ct = T // tc
    pieces = 512 // tc
    base = base_cnt.reshape(T // 512, 8, LANES)[:, :pieces, :N_EXPERTS].reshape(n_ct, N_EXPERTS)
    base = base.astype(I32)
    n_run = jnp.concatenate([base[1:], counts[None, :]], axis=0) - base
    n_chunk = (n_run + WIN_ROWS - 1) // WIN_ROWS
    chunk_end = jnp.cumsum(n_chunk, axis=1)
    chunk_start = chunk_end - n_chunk
    win_off = chunk_start * WIN_ROWS
    c_id = jnp.arange(WIN_CHUNKS, dtype=I32)
    c_exp = jnp.sum((chunk_end[:, None, :] <= c_id[None, :, None]).astype(I32), axis=2)
    c_real = c_exp < N_EXPERTS
    c_exp = jnp.minimum(c_exp, N_EXPERTS - 1)
    take = lambda tab: jnp.take_along_axis(tab, c_exp, axis=1)
    c_src = (pstart[c_exp] + take(base) + (c_id[None, :] - take(chunk_start)) * WIN_ROWS)
    src_tiles = jnp.concatenate(
        [jnp.where(c_real, c_src, 0) * ROW_TILES,
         jnp.zeros((n_ct, tc * TOP_K - WIN_CHUNKS), I32)], axis=1)
    slot_off = (jnp.arange(n_ct, dtype=I32) % 2) * (WIN_CHUNKS * WIN_ROWS)
    shift = jnp.repeat(win_off - base + slot_off[:, None], tc, axis=0).T
    pos = (jnp.sum(jnp.where(onehot, shift[:, None, :], 0), axis=0) + rank) * ROW_TILES

    def per_tile(a):
        return a.reshape(TOP_K, n_ct, tc).transpose(1, 2, 0).reshape(n_ct, tc * TOP_K)

    out = _combine(per_tile(pos), per_tile(gates), src_tiles, yb, h1,
                   g2.reshape(16, ROW_TILES, LANES), norm_final_w[None, :],
                   tiles_per_batch=N // tc)
    return out.reshape(B, N, D)
```

```python
import functools
import math

import jax
import jax.numpy as jnp
import numpy as np
from jax import lax
from jax.experimental import pallas as pl
from jax.experimental.pallas import tpu as pltpu

F32 = jnp.float32
BF16 = jnp.bfloat16
I32 = jnp.int32

D_MODEL = 1024
GRID_W = 64
RET_HEADS = 4
RET_DK = 64
HG_HEADS = 4
PROJ_W = 4096
ROPE_BASE = 10000.0
EPS = 1e-6
N_EXPERTS = 32
TOP_K = 4
D_FF = 1024
SWIGLU_LIMIT = 7.0
SWIGLU_ALPHA = 1.702
LOG2_E = 1.0 / math.log(2.0)

LANES = 128
CHUNK = 128
ROW_BLOCK = 1024
SUB_ROWS = 256
COMBINE_TILE = 256
WIN_ROWS = 16
WIN_CHUNKS = (COMBINE_TILE * TOP_K + N_EXPERTS * (WIN_ROWS - 1) + WIN_ROWS - 1) // WIN_ROWS
VMEM_LIMIT = 56 * 1024 * 1024
ROW_TILES = D_MODEL // LANES

C_RQ, C_RK, C_RV, C_RG, C_HQ, C_FF, C_FB, C_HV, C_HG = (
    0, 256, 512, 1024, 1536, 2048, 2560, 3072, 3584)


def _cparams(sem):
    return pltpu.CompilerParams(dimension_semantics=sem, vmem_limit_bytes=VMEM_LIMIT)


def _split_bf16(x):
    hi = x.astype(BF16)
    lo = (x - hi.astype(F32)).astype(BF16)
    return hi, lo


def _dot(a, b):
    return jnp.dot(a, b, preferred_element_type=F32)


def _dot_nt(a, b):
    return lax.dot_general(a, b, (((1,), (1,)), ((), ())), preferred_element_type=F32)


def _dot3(a, b):
    ah, al = _split_bf16(a)
    bh, bl = _split_bf16(b)
    return _dot(ah, bh) + (_dot(ah, bl) + _dot(al, bh))


def _silu(x):
    return x * jax.nn.sigmoid(x)


def _load_tile_rows(ref, n):
    return jnp.concatenate([ref[pl.ds(s, n, stride=ROW_TILES), :] for s in range(ROW_TILES)],
                           axis=1)


def _store_tile_rows(ref, x):
    n = x.shape[0]
    for s in range(ROW_TILES):
        ref[pl.ds(s, n, stride=ROW_TILES), :] = x[:, s * LANES:(s + 1) * LANES]


def _ada_kernel(c_ref, w_ref, b_ref, o_ref):
    s = _silu(c_ref[...])
    o_ref[...] = _dot3(s, w_ref[...]) + b_ref[...]


def _ada(cc, w, b):
    nblk = w.shape[1] // D_MODEL
    return pl.pallas_call(
        _ada_kernel,
        out_shape=jax.ShapeDtypeStruct((cc.shape[0], w.shape[1]), F32),
        grid=(nblk,),
        in_specs=[pl.BlockSpec(cc.shape, lambda j: (0, 0)),
                  pl.BlockSpec((D_MODEL, D_MODEL), lambda j: (0, j)),
                  pl.BlockSpec((1, D_MODEL), lambda j: (0, j))],
        out_specs=pl.BlockSpec((cc.shape[0], D_MODEL), lambda j: (0, j)),
        compiler_params=_cparams(("arbitrary",)),
        name="ada",
    )(cc, w, b)


def _inproj_kernel(x_ref, sc_ref, sh_ref, nw_ref, w_ref, lb_ref, cos_ref, sin_ref,
                   pb_ref, lf_ref, u_scr, *, rope):
    x = x_ref[...]
    ms = jnp.mean(x * x, axis=-1, keepdims=True)
    u = x * lax.rsqrt(ms + EPS) * nw_ref[...] * (1.0 + sc_ref[...]) + sh_ref[...]
    u_scr[...] = u.astype(BF16)

    def proj(lo, width):
        return _dot(u_scr[...], w_ref[:, lo:lo + width])

    tm = x.shape[0]
    if rope:
        lane = lax.broadcasted_iota(I32, (tm, LANES), 1)
        first = (lane & 32) == 0

    def put_rot(col, scale):
        for j in range(2):
            lo = col + j * LANES
            t = proj(lo, LANES)
            if scale != 1.0:
                t = t * scale
            if rope:
                tb = j * LANES
                rot = jnp.where(first, pltpu.roll(t, 96, axis=1), pltpu.roll(t, 32, axis=1))
                t = t * cos_ref[:, tb:tb + LANES] + rot * sin_ref[:, tb:tb + LANES]
            pb_ref[:, lo:lo + LANES] = t.astype(BF16)

    put_rot(C_RQ, 1.0)
    put_rot(C_RK, RET_DK ** -0.5)
    pb_ref[:, C_RV:C_RV + 512] = proj(C_RV, 512).astype(BF16)
    pb_ref[:, C_RG:C_RG + 512] = _silu(proj(C_RG, 512)).astype(BF16)
    pb_ref[:, C_HQ:C_HQ + 512] = _silu(proj(C_HQ, 512)).astype(BF16)
    pb_ref[:, C_HV:C_HV + 512] = proj(C_HV, 512).astype(BF16)
    pb_ref[:, C_HG:C_HG + 512] = _silu(proj(C_HG, 512)).astype(BF16)

    la = lb_ref[0]
    lbb = lb_ref[1]
    mx = jnp.maximum(la, lbb)
    ea = jnp.exp(la - mx)
    eb = jnp.exp(lbb - mx)
    lb = ea / (ea + eb)
    for d, col in enumerate((C_FF, C_FB)):
        lbd = lb[d:d + 1, :]
        f = lbd + (1.0 - lbd) * jax.nn.sigmoid(proj(col, 512))
        pb_ref[:, col:col + 512] = (1.0 - f).astype(BF16)
        lf_ref[:, d * 512:(d + 1) * 512] = jnp.log(f) * LOG2_E


def _inproj(x, sc, sh, mod_row, nw, w_bf, hg_lb, cos_t, sin_t, *, rope, tm):
    B, n, _ = x.shape
    nt = n // tm
    if mod_row is None:
        mrow = lambda b, j: (b, 0, 0)
    else:
        mrow = lambda b, j: (mod_row, 0, 0)
    return pl.pallas_call(
        functools.partial(_inproj_kernel, rope=rope),
        out_shape=(jax.ShapeDtypeStruct((B, n, PROJ_W), BF16),
                   jax.ShapeDtypeStruct((B, n, 1024), F32)),
        grid=(B, nt),
        in_specs=[pl.BlockSpec((None, tm, D_MODEL), lambda b, j: (b, j, 0)),
                  pl.BlockSpec((None, 1, D_MODEL), mrow),
                  pl.BlockSpec((None, 1, D_MODEL), mrow),
                  pl.BlockSpec((1, D_MODEL), lambda b, j: (0, 0)),
                  pl.BlockSpec((D_MODEL, PROJ_W), lambda b, j: (0, 0),
                               pipeline_mode=pl.Buffered(1)),
                  pl.BlockSpec((2, 2, 512), lambda b, j: (0, 0, 0)),
                  pl.BlockSpec((tm, 256), lambda b, j: (j, 0)),
                  pl.BlockSpec((tm, 256), lambda b, j: (j, 0))],
        out_specs=(pl.BlockSpec((None, tm, PROJ_W), lambda b, j: (b, j, 0)),
                   pl.BlockSpec((None, tm, 1024), lambda b, j: (b, j, 0))),
        scratch_shapes=[pltpu.VMEM((tm, D_MODEL), BF16)],
        compiler_params=_cparams(("arbitrary", "arbitrary")),
        name="inproj_rope" if rope else "inproj_ctx",
    )(x, sc, sh, nw, w_bf, hg_lb, cos_t, sin_t)


_RET_LGF = [math.log1p(-(2.0 ** (-5.0 - 2.0 * h))) for h in range(RET_HEADS)]
_RET_LGB = [math.log1p(-(2.0 ** (-6.0 - 2.0 * h))) for h in range(RET_HEADS)]


def _ret_kernel(q_ref, k_ref, v_ref, g_ref, kc_ref, vc_ref, o_ref,
                u_scr, sin_scr, kt_scr, dtot_scr, *, n_lat, n_ctx):
    L = CHUNK
    pair = pl.program_id(1)
    row = lax.broadcasted_iota(I32, (L, L), 0).astype(F32)
    col = lax.broadcasted_iota(I32, (L, L), 1).astype(F32)
    lane = lax.broadcasted_iota(I32, (L, LANES), 1)
    trow = lax.broadcasted_iota(I32, (L, 1), 0).astype(F32)
    tcol = lax.broadcasted_iota(I32, (1, L), 1).astype(F32)
    low_half = lane < RET_DK

    def u_chunk(k_blk, v_blk, ci, store_kt):
        kt = k_blk.astype(F32).T
        if store_kt is not None:
            kt_scr[store_kt] = kt.astype(BF16)
        for hh in range(2):
            lgf = jnp.where(pair == 0, _RET_LGF[hh], _RET_LGF[2 + hh])
            lgb = jnp.where(pair == 0, _RET_LGB[hh], _RET_LGB[2 + hh])
            kth = kt[hh * RET_DK:(hh + 1) * RET_DK, :]
            wkf = jnp.exp(lgf * (L - 1.0 - tcol))
            wkb = jnp.exp(lgb * tcol)
            lhs = jnp.concatenate([kth * wkf, kth * wkb], axis=0).astype(BF16)
            u_scr[hh, ci] = _dot(lhs, v_blk[:, hh * LANES:(hh + 1) * LANES])

    for c in range(n_ctx):
        u_chunk(kc_ref[c * L:(c + 1) * L, :], vc_ref[c * L:(c + 1) * L, :], c, None)

    def lat_u(c, carry):
        r0 = pl.multiple_of(c * L, L)
        u_chunk(k_ref[pl.ds(r0, L), :], v_ref[pl.ds(r0, L), :], n_ctx + c, c)
        return carry
    lax.fori_loop(0, n_lat, lat_u, 0, unroll=2)

    ones = jnp.ones((RET_DK, LANES), F32)
    for hh in range(2):
        lgf = jnp.where(pair == 0, _RET_LGF[hh], _RET_LGF[2 + hh])
        lgb = jnp.where(pair == 0, _RET_LGB[hh], _RET_LGB[2 + hh])
        d = row - col
        dtot_scr[hh] = jnp.where(d > 0, jnp.exp(lgf * jnp.maximum(d, 0.0)),
                                 jnp.where(d < 0, jnp.exp(lgb * jnp.maximum(-d, 0.0)), 2.0))
        af = jnp.exp(ones * (lgf * L))
        ab = jnp.exp(ones * (lgb * L))

        s = jnp.zeros((RET_DK, LANES), F32)
        for c in range(n_ctx):
            s = af * s + u_scr[hh, c, 0:RET_DK, :]
        sb = jnp.zeros((RET_DK, LANES), F32)
        for c in reversed(range(n_ctx)):
            sb = ab * sb + u_scr[hh, c, RET_DK:2 * RET_DK, :]

        def fwd(c, s, hh=hh, af=af):
            sin_scr[hh, c, 0:RET_DK, :] = s.astype(BF16)
            return af * s + u_scr[hh, n_ctx + c, 0:RET_DK, :]
        lax.fori_loop(0, n_lat, fwd, s)

        def bwd(i, sb, hh=hh, ab=ab):
            c = n_lat - 1 - i
            sin_scr[hh, c, RET_DK:2 * RET_DK, :] = sb.astype(BF16)
            return ab * sb + u_scr[hh, n_ctx + c, RET_DK:2 * RET_DK, :]
        lax.fori_loop(0, n_lat, bwd, sb)

    def out_chunk(c, carry):
        r0 = pl.multiple_of(c * L, L)
        q = q_ref[pl.ds(r0, L), :].astype(F32)
        qr = pltpu.roll(q, RET_DK, axis=1)
        kt = kt_scr[c]
        for hh in range(2):
            lgf = jnp.where(pair == 0, _RET_LGF[hh], _RET_LGF[2 + hh])
            lgb = jnp.where(pair == 0, _RET_LGB[hh], _RET_LGB[2 + hh])
            mine = low_half if hh == 0 else jnp.logical_not(low_half)
            qm = jnp.where(mine, q, 0.0).astype(BF16)
            p = (_dot(qm, kt) * dtot_scr[hh]).astype(BF16)
            vh = v_ref[pl.ds(r0, L), hh * LANES:(hh + 1) * LANES]
            wqf = jnp.exp(lgf * (trow + 1.0))
            wqb = jnp.exp(lgb * (L - trow))
            qa, qb = (q, qr) if hh == 0 else (qr, q)
            qs = jnp.where(low_half, qa * wqf, qb * wqb).astype(BF16)
            o = _dot(p, vh) + _dot(qs, sin_scr[hh, c])
            ms = jnp.mean(o * o, axis=-1, keepdims=True)
            gh = g_ref[pl.ds(r0, L), hh * LANES:(hh + 1) * LANES].astype(F32)
            o_ref[pl.ds(r0, L), hh * LANES:(hh + 1) * LANES] = (
                o * lax.rsqrt(ms + EPS) * gh).astype(BF16)
        return carry
    lax.fori_loop(0, n_lat, out_chunk, 0, unroll=2)


def _retention(pb, pbc):
    B, n, _ = pb.shape
    nc = pbc.shape[1]
    n_lat, n_ctx = n // CHUNK, nc // CHUNK
    return pl.pallas_call(
        functools.partial(_ret_kernel, n_lat=n_lat, n_ctx=n_ctx),
        out_shape=jax.ShapeDtypeStruct((B, n, 512), BF16),
        grid=(B, 2),
        in_specs=[pl.BlockSpec((None, n, LANES), lambda b, p: (b, 0, C_RQ // LANES + p)),
                  pl.BlockSpec((None, n, LANES), lambda b, p: (b, 0, C_RK // LANES + p)),
                  pl.BlockSpec((None, n, 256), lambda b, p: (b, 0, C_RV // 256 + p)),
                  pl.BlockSpec((None, n, 256), lambda b, p: (b, 0, C_RG // 256 + p)),
                  pl.BlockSpec((None, nc, LANES), lambda b, p: (b, 0, C_RK // LANES + p)),
                  pl.BlockSpec((None, nc, 256), lambda b, p: (b, 0, C_RV // 256 + p))],
        out_specs=pl.BlockSpec((None, n, 256), lambda b, p: (b, 0, p)),
        scratch_shapes=[pltpu.VMEM((2, n_lat + n_ctx, CHUNK, LANES), F32),
                        pltpu.VMEM((2, n_lat, CHUNK, LANES), BF16),
                        pltpu.VMEM((n_lat, LANES, CHUNK), BF16),
                        pltpu.VMEM((2, CHUNK, CHUNK), F32)],
        compiler_params=_cparams(("arbitrary", "arbitrary")),
        name="retention",
    )(pb, pb, pb, pb, pbc, pbc)


_LEVELS = (64, 32, 16, 8, 4, 2, 1)


def _expand_rows(r, rep):
    n = r.shape[0]
    if n == 1:
        return jnp.broadcast_to(r, (rep, r.shape[1]))
    return jnp.concatenate(
        [jnp.broadcast_to(r[i:i + 1, :], (rep, r.shape[1])) for i in range(n)], axis=0)


def _hgrn_kernel(q_ref, kf_ref, kb_ref, v_ref, g_ref, lff_ref, lfb_ref,
                 kfc_ref, kbc_ref, vc_ref, lffc_ref, lfbc_ref, nw_ref, o_ref,
                 ut_scr, a_scr, qs_scr, oi_scr, sin_scr, bfb_scr, *, n_lat, n_ctx):
    L = CHUNK
    row = lax.broadcasted_iota(I32, (L, L), 0)
    col = lax.broadcasted_iota(I32, (L, L), 1)
    xr_bits = lax.bitcast_convert_type((row ^ col).astype(F32), I32)
    lv = lax.shift_right_logical(xr_bits, 23) - 127
    row2 = lax.broadcasted_iota(I32, (L, 2 * L), 0)
    col2 = lax.broadcasted_iota(I32, (L, 2 * L), 1) & (L - 1)
    tril2 = jnp.where(col2 <= row2, 1.0, 0.0).astype(BF16)
    triu2 = jnp.where(col2 >= row2, 1.0, 0.0).astype(BF16)

    def cums(lff, lfb):
        hf, lof = _split_bf16(lff)
        hb, lob = _split_bf16(lfb)
        bf = _dot(tril2, jnp.concatenate([hf, lof], axis=0))
        bb = _dot(triu2, jnp.concatenate([hb, lob], axis=0))
        return bf, bb

    def state_part(ci, kf, kb, v_blk, bf, bb):
        endf = bf[L - 1:L, :]
        endb = bb[0:1, :]
        ksf = kf * jnp.exp2(endf - bf)
        ksb = kb * jnp.exp2(endb - bb)
        vt = v_blk.astype(F32).T.astype(BF16)
        ut_scr[ci] = _dot(vt, jnp.concatenate([ksf, ksb], axis=1).astype(BF16))
        a_scr[ci] = jnp.broadcast_to(
            jnp.concatenate([jnp.exp2(endf), jnp.exp2(endb)], axis=1), (8, 2 * LANES))

    for c in range(n_ctx):
        sl = slice(c * L, (c + 1) * L)
        bf, bb = cums(lffc_ref[sl, :], lfbc_ref[sl, :])
        state_part(c, kfc_ref[sl, :].astype(F32), kbc_ref[sl, :].astype(F32),
                   vc_ref[sl, :], bf, bb)

    def lat_chunk(c, u):
        rows = pl.ds(pl.multiple_of(c * L, L), L)
        bf_scr = bfb_scr.at[u, 0]
        bb_scr = bfb_scr.at[u, 1]
        q = q_ref[rows, :].astype(F32)
        kf = kf_ref[rows, :].astype(F32)
        kb = kb_ref[rows, :].astype(F32)
        v_blk = v_ref[rows, :]
        lff = lff_ref[rows, :]
        lfb = lfb_ref[rows, :]
        bf, bb = cums(lff, lfb)
        state_part(n_ctx + c, kf, kb, v_blk, bf, bb)
        qs_scr[rows, :] = jnp.concatenate([q * jnp.exp2(bf), q * jnp.exp2(bb)],
                                          axis=1).astype(BF16)
        bf_scr[...] = bf
        bb_scr[...] = bb

        acc = jnp.zeros((L, L), F32)
        for lvl, h in enumerate(_LEVELS):
            bit = (row & h) != 0
            ksel = jnp.where(bit, kb, kf)
            if h >= 4:
                n = (L // 2) // h
                if n == 1:
                    rf = bf_scr[h - 1:h, :]
                    rb = bb_scr[h:h + 1, :]
                else:
                    rf = bf_scr[pl.ds(h - 1, n, stride=2 * h), :]
                    rb = bb_scr[pl.ds(h, n, stride=2 * h), :]
                df = bf - _expand_rows(rf, 2 * h)
                db = bb - _expand_rows(rb, 2 * h)
                eq = jnp.where(bit, df, db)
                ek = -jnp.where(bit, db, df)
            elif h == 2:
                m = row & 3
                lff_n = pltpu.roll(lff, L - 1, axis=0)
                lfb_n = pltpu.roll(lfb, L - 1, axis=0)
                eq = jnp.where(m == 2, lff,
                               jnp.where(m == 3, lff + pltpu.roll(lff, 1, axis=0),
                                         jnp.where(m == 0, lfb + lfb_n, lfb)))
                ek = jnp.where(m == 3, pltpu.roll(lfb, 1, axis=0),
                               jnp.where(m == 0, lff_n, 0.0))
            else:
                eq = jnp.where(bit, lff, lfb)
                ek = None
            lhs = (q * jnp.exp2(eq)).astype(BF16)
            rhs = (ksel if ek is None else ksel * jnp.exp2(ek)).astype(BF16)
            acc = jnp.where(lv == 6 - lvl, _dot_nt(lhs, rhs), acc)

        dsum = jnp.sum(q * (kf + kb), axis=-1, keepdims=True)
        oi_scr[rows, :] = _dot(acc.astype(BF16), v_blk) + dsum * v_blk.astype(F32)

    def lat_pair(i, carry):
        for u in range(4):
            lat_chunk(4 * i + u, u)
        return carry
    lax.fori_loop(0, n_lat // 4, lat_pair, 0)

    st = jnp.zeros((LANES, LANES), F32)
    for c in range(n_ctx):
        st = st * a_scr[c, 0:1, 0:LANES] + ut_scr[c, :, 0:LANES]
    stb = jnp.zeros((LANES, LANES), F32)
    for c in reversed(range(n_ctx)):
        stb = stb * a_scr[c, 0:1, LANES:2 * LANES] + ut_scr[c, :, LANES:2 * LANES]

    def fwd(c, st):
        sin_scr[c, :, 0:LANES] = st.astype(BF16)
        ci = n_ctx + c
        return st * a_scr[ci, 0:1, 0:LANES] + ut_scr[ci, :, 0:LANES]
    lax.fori_loop(0, n_lat, fwd, st)

    def bwd(i, stb):
        c = n_lat - 1 - i
        sin_scr[c, :, LANES:2 * LANES] = stb.astype(BF16)
        ci = n_ctx + c
        return stb * a_scr[ci, 0:1, LANES:2 * LANES] + ut_scr[ci, :, LANES:2 * LANES]
    lax.fori_loop(0, n_lat, bwd, stb)

    def out_chunk(c, carry):
        rows = pl.ds(pl.multiple_of(c * L, L), L)
        o = oi_scr[rows, :] + _dot_nt(qs_scr[rows, :], sin_scr[c])
        ms = jnp.mean(o * o, axis=-1, keepdims=True)
        y = o * lax.rsqrt(ms + EPS) * nw_ref[...] * g_ref[rows, :].astype(F32)
        o_ref[rows, :] = y.astype(BF16)
        return carry
    lax.fori_loop(0, n_lat, out_chunk, 0, unroll=4)


def _hgrn(pb, lf, pbc, lfc, nw):
    B, n, _ = pb.shape
    nc = pbc.shape[1]
    n_lat, n_ctx = n // CHUNK, nc // CHUNK

    def colblk(rows, col0):
        return pl.BlockSpec((None, rows, LANES), lambda b, h: (b, 0, col0 // LANES + h))

    return pl.pallas_call(
        functools.partial(_hgrn_kernel, n_lat=n_lat, n_ctx=n_ctx),
        out_shape=jax.ShapeDtypeStruct((B, n, 512), BF16),
        grid=(B, HG_HEADS),
        in_specs=[colblk(n, C_HQ), colblk(n, C_FF), colblk(n, C_FB), colblk(n, C_HV),
                  colblk(n, C_HG), colblk(n, 0), colblk(n, 512),
                  colblk(nc, C_FF), colblk(nc, C_FB), colblk(nc, C_HV),
                  colblk(nc, 0), colblk(nc, 512),
                  pl.BlockSpec((1, LANES), lambda b, h: (0, 0))],
        out_specs=pl.BlockSpec((None, n, LANES), lambda b, h: (b, 0, h)),
        scratch_shapes=[pltpu.VMEM((n_lat + n_ctx, LANES, 2 * LANES), F32),
                        pltpu.VMEM((n_lat + n_ctx, 8, 2 * LANES), F32),
                        pltpu.VMEM((n, 2 * LANES), BF16),
                        pltpu.VMEM((n, LANES), F32),
                        pltpu.VMEM((n_lat, LANES, 2 * LANES), BF16),
                        pltpu.VMEM((4, 2, CHUNK, LANES), F32)],
        compiler_params=_cparams(("arbitrary", "arbitrary")),
        name="hgrn2",
    )(pb, pb, pb, pb, pb, lf, lf, pbc, pbc, pbc, lfc, lfc, nw)


def _outproj_kernel(ar_ref, ah_ref, w_ref, x_ref, g1_ref, sc_ref, sh_ref, nw_ref,
                    rw_ref, rb_ref, h1_ref, v_ref, route_ref, route_t_ref, gate_ref, base_ref, cnt_ref,
                    cnt_scr):
    first_step = jnp.logical_and(pl.program_id(0) == 0, pl.program_id(1) == 0)

    @pl.when(first_step)
    def _():
        cnt_scr[...] = jnp.zeros_like(cnt_scr)

    y = _dot(ar_ref[...], w_ref[0:512, :]) + _dot(ah_ref[...], w_ref[512:1024, :])
    h1 = x_ref[...] + g1_ref[...] * y
    _store_tile_rows(h1_ref, h1)
    ms = jnp.mean(h1 * h1, axis=-1, keepdims=True)
    v = h1 * lax.rsqrt(ms + EPS) * nw_ref[...] * (1.0 + sc_ref[...]) + sh_ref[...]
    _store_tile_rows(v_ref, v)

    tm = v.shape[0]
    logits = _dot3(v, rw_ref[...]) + rb_ref[...]
    lane = lax.broadcasted_iota(I32, (tm, LANES), 1)
    lane_f = lane.astype(F32)
    l = logits
    sels, tops, idxs = [], [], []
    for _ in range(TOP_K):
        m = jnp.max(l, axis=1, keepdims=True)
        i = jnp.min(jnp.where(l == m, lane_f, float(LANES)), axis=1, keepdims=True)
        sel = lane_f == i
        l = jnp.where(sel, -jnp.inf, l)
        sels.append(sel)
        tops.append(m)
        idxs.append(i)
    es = [jnp.exp(t - tops[0]) for t in tops]
    den = es[0] + es[1] + es[2] + es[3]
    gates = [e / den for e in es]
    for k in range(TOP_K):
        gate_ref[pl.ds(k, tm, stride=TOP_K), :] = jnp.broadcast_to(gates[k], (tm, LANES))

    oh = jnp.zeros((tm, LANES), F32)
    for sel in sels:
        oh = jnp.where(sel, 1.0, oh)
    r = lax.broadcasted_iota(I32, (tm, tm), 0)
    c = lax.broadcasted_iota(I32, (tm, tm), 1)
    tri = jnp.where(c < r, 1.0, 0.0).astype(BF16)
    before = _dot(tri, oh.astype(BF16)) + cnt_scr[0:1, :]
    ranks = [jnp.sum(jnp.where(sel, before, 0.0), axis=1, keepdims=True) for sel in sels]
    base = cnt_scr[0:1, :]
    pieces = []
    for p in range(tm // COMBINE_TILE):
        pieces.append(base)
        base = base + jnp.sum(oh[p * COMBINE_TILE:(p + 1) * COMBINE_TILE, :], axis=0, keepdims=True)
    pieces.append(jnp.zeros((8 - len(pieces), LANES), F32))
    base_ref[...] = jnp.concatenate(pieces, axis=0)
    cnt_scr[...] = jnp.broadcast_to(base, cnt_scr.shape)
    cnt_ref[...] = cnt_scr[...]

    out = jnp.zeros((tm, LANES), F32)
    for k in range(TOP_K):
        out = jnp.where(lane == k, idxs[k], out)
        out = jnp.where(lane == TOP_K + k, gates[k], out)
        out = jnp.where(lane == 2 * TOP_K + k, ranks[k], out)
    route_ref[...] = out
    route_t_ref[...] = out.T[0:16, :]


def _outproj(a_ret, a_hg, w_bf, x, g1, sc2, sh2, nw, rw, rb, *, tm):
    B, n, _ = x.shape
    nt = n // tm
    T = B * n
    mrow = lambda b, j: (b, 0, 0)
    tok = lambda b, j: (b * nt + j, 0)
    return pl.pallas_call(
        _outproj_kernel,
        out_shape=(jax.ShapeDtypeStruct((T * ROW_TILES, LANES), F32),
                   jax.ShapeDtypeStruct((T * ROW_TILES, LANES), F32),
                   jax.ShapeDtypeStruct((T, LANES), F32),
                   jax.ShapeDtypeStruct((16, T), F32),
                   jax.ShapeDtypeStruct((T * TOP_K, LANES), F32),
                   jax.ShapeDtypeStruct((B * nt * 8, LANES), F32),
                   jax.ShapeDtypeStruct((8, LANES), F32)),
        grid=(B, nt),
        in_specs=[pl.BlockSpec((None, tm, 512), lambda b, j: (b, j, 0)),
                  pl.BlockSpec((None, tm, 512), lambda b, j: (b, j, 0)),
                  pl.BlockSpec((D_MODEL, D_MODEL), lambda b, j: (0, 0)),
                  pl.BlockSpec((None, tm, D_MODEL), lambda b, j: (b, j, 0)),
                  pl.BlockSpec((None, 1, D_MODEL), mrow),
                  pl.BlockSpec((None, 1, D_MODEL), mrow),
                  pl.BlockSpec((None, 1, D_MODEL), mrow),
                  pl.BlockSpec((1, D_MODEL), lambda b, j: (0, 0)),
                  pl.BlockSpec((D_MODEL, LANES), lambda b, j: (0, 0)),
                  pl.BlockSpec((1, LANES), lambda b, j: (0, 0))],
        out_specs=(pl.BlockSpec((tm * ROW_TILES, LANES), tok),
                   pl.BlockSpec((tm * ROW_TILES, LANES), tok),
                   pl.BlockSpec((tm, LANES), tok),
                   pl.BlockSpec((16, tm), lambda b, j: (0, b * nt + j)),
                   pl.BlockSpec((tm * TOP_K, LANES), tok),
                   pl.BlockSpec((8, LANES), tok),
                   pl.BlockSpec((8, LANES), lambda b, j: (0, 0))),
        scratch_shapes=[pltpu.VMEM((8, LANES), F32)],
        compiler_params=_cparams(("arbitrary", "arbitrary")),
        name="outproj_router",
    )(a_ret, a_hg, w_bf, x, g1, sc2, sh2, nw, rw, rb)


def _dispatch_kernel(dest_hbm, v_ref, xs_hbm, idx_smem, zbuf, sem_idx, sem_z, sem_rows,
                     *, tile, n_tiles, n_rows):
    i = pl.program_id(0)
    slot = lax.rem(i, 2)

    def idx_copy(j, s):
        n_asg = tile * TOP_K
        return pltpu.make_async_copy(dest_hbm.at[j], idx_smem.at[pl.ds(s * n_asg, n_asg)],
                                     sem_idx.at[s])

    def zero_copy(j):
        r0 = (n_rows + j * SUB_ROWS) * ROW_TILES
        return pltpu.make_async_copy(zbuf, xs_hbm.at[pl.ds(r0, SUB_ROWS * ROW_TILES), :], sem_z)

    @pl.when(i == 0)
    def _():
        idx_copy(0, 0).start()
        zbuf[...] = jnp.zeros_like(zbuf)
        for j in range(ROW_BLOCK // SUB_ROWS):
            zero_copy(j).start()
        for j in range(ROW_BLOCK // SUB_ROWS):
            zero_copy(j).wait()

    idx_copy(i, slot).wait()

    @pl.when(i + 1 < n_tiles)
    def _():
        idx_copy(i + 1, 1 - slot).start()

    def issue(t, carry):
        for k in range(TOP_K):
            d = pl.multiple_of(idx_smem[slot * (tile * TOP_K) + k * tile + t], ROW_TILES)
            src = v_ref.at[pl.ds(pl.multiple_of(t * ROW_TILES, ROW_TILES), ROW_TILES), :]
            pltpu.make_async_copy(src, xs_hbm.at[pl.ds(d, ROW_TILES), :],
                                  sem_rows).start(priority=k % 2)
        return carry
    lax.fori_loop(0, tile, issue, 0, unroll=4)
    for _ in range(TOP_K):
        pltpu.make_async_copy(v_ref, xs_hbm.at[pl.ds(0, tile * ROW_TILES), :], sem_rows).wait()


def _dispatch(dest_tiles, v, *, tile):
    T = v.shape[0] // ROW_TILES
    n_rows = T * TOP_K
    return pl.pallas_call(
        functools.partial(_dispatch_kernel, tile=tile, n_tiles=T // tile, n_rows=n_rows),
        out_shape=jax.ShapeDtypeStruct(((n_rows + ROW_BLOCK) * ROW_TILES, LANES), F32),
        grid=(T // tile,),
        in_specs=[pl.BlockSpec(memory_space=pl.ANY),
                  pl.BlockSpec((tile * ROW_TILES, LANES), lambda i: (i, 0))],
        out_specs=pl.BlockSpec(memory_space=pl.ANY),
        scratch_shapes=[pltpu.SMEM((2 * tile * TOP_K,), I32),
                        pltpu.VMEM((SUB_ROWS * ROW_TILES, LANES), F32),
                        pltpu.SemaphoreType.DMA((2,)),
                        pltpu.SemaphoreType.DMA,
                        pltpu.SemaphoreType.DMA],
        compiler_params=_cparams(("arbitrary",)),
        name="dispatch",
    )(dest_tiles, v)


def _moe_kernel(be_ref, ns_ref, nx_ref, xr_ref, x_hbm, w1_hbm, b1_ref, w2_hbm, b2_ref, y_ref,
                w1_stage, w2_stage, w1_scr, w2_scr, xbuf, x_scr, act_scr, sem_w, sem_x,
                *, n_blocks):
    i = pl.program_id(0)
    slot = lax.rem(i, 2)
    e = be_ref[i]
    nsub = ns_ref[i]
    changed = jnp.logical_or(i == 0, e != be_ref[jnp.maximum(i - 1, 0)])

    def x_copy(j, s):
        r0 = pl.multiple_of(xr_ref[j], ROW_TILES)
        return pltpu.make_async_copy(x_hbm.at[pl.ds(r0, ROW_BLOCK * ROW_TILES), :], xbuf.at[s],
                                     sem_x.at[s])

    @pl.when(i == 0)
    def _():
        x_copy(0, 0).start()

    @pl.when(nsub > 0)
    def _():
        x_copy(i, slot).wait()

    nxt_blk = jnp.minimum(i + 1, n_blocks - 1)

    @pl.when(jnp.logical_and(i + 1 < n_blocks, ns_ref[nxt_blk] > 0))
    def _():
        x_copy(nxt_blk, 1 - slot).start()

    def weight_copies(ex):
        return (pltpu.make_async_copy(w1_hbm.at[ex], w1_stage, sem_w.at[0]),
                pltpu.make_async_copy(w2_hbm.at[ex], w2_stage, sem_w.at[1]))

    @pl.when(i == 0)
    def _():
        for cp in weight_copies(e):
            cp.start()

    @pl.when(jnp.logical_and(changed, nsub > 0))
    def _():
        for cp in weight_copies(e):
            cp.wait()
        w1_scr[...] = w1_stage[...].astype(BF16)
        w2_scr[...] = w2_stage[...].astype(BF16)
        nxt = nx_ref[e]

        @pl.when(nxt >= 0)
        def _():
            for cp in weight_copies(nxt):
                cp.start()

    def compute(rows):
        x_scr[0:rows, :] = _load_tile_rows(xbuf.at[slot], rows).astype(BF16)
        cw = 256
        for c in range(D_FF // cw):
            glu = (_dot(x_scr[0:rows, :], w1_scr[:, c * cw:(c + 1) * cw])
                   + b1_ref[:, c * cw:(c + 1) * cw])
            lin = (_dot(x_scr[0:rows, :], w1_scr[:, D_FF + c * cw:D_FF + (c + 1) * cw])
                   + b1_ref[:, D_FF + c * cw:D_FF + (c + 1) * cw])
            glu = jnp.minimum(glu, SWIGLU_LIMIT)
            lin = jnp.clip(lin, -SWIGLU_LIMIT, SWIGLU_LIMIT)
            act = glu * jax.nn.sigmoid(SWIGLU_ALPHA * glu) * (lin + 1.0)
            act_scr[0:rows, c * cw:(c + 1) * cw] = act.astype(BF16)
        for c in range(D_MODEL // cw):
            y = (_dot(act_scr[0:rows, :], w2_scr[:, c * cw:(c + 1) * cw])
                 + b2_ref[:, c * cw:(c + 1) * cw])
            for s in range(cw // LANES):
                t = c * (cw // LANES) + s
                y_ref[pl.ds(t, rows, stride=ROW_TILES), :] = y[:, s * LANES:(s + 1) * LANES]
        if rows < ROW_BLOCK:
            y_ref[rows * ROW_TILES:, :] = jnp.zeros(((ROW_BLOCK - rows) * ROW_TILES, LANES), F32)

    for m in range(1, ROW_BLOCK // SUB_ROWS + 1):
        @pl.when(nsub == m)
        def _(m=m):
            compute(m * SUB_ROWS)

    @pl.when(nsub == 0)
    def _():
        y_ref[...] = jnp.zeros_like(y_ref)


def _moe(block_e, nsub, next_e, xrow, xs, w1, b1, w2, b2, *, n_blocks):
    return pl.pallas_call(
        functools.partial(_moe_kernel, n_blocks=n_blocks),
        out_shape=jax.ShapeDtypeStruct((n_blocks * ROW_BLOCK * ROW_TILES, LANES), F32),
        grid_spec=pltpu.PrefetchScalarGridSpec(
            num_scalar_prefetch=4,
            grid=(n_blocks,),
            in_specs=[pl.BlockSpec(memory_space=pl.ANY),
                      pl.BlockSpec(memory_space=pl.ANY),
                      pl.BlockSpec((None, 1, 2 * D_FF), lambda i, be, ns, nx, xr: (be[i], 0, 0)),
                      pl.BlockSpec(memory_space=pl.ANY),
                      pl.BlockSpec((None, 1, D_MODEL), lambda i, be, ns, nx, xr: (be[i], 0, 0))],
            out_specs=pl.BlockSpec((ROW_BLOCK * ROW_TILES, LANES),
                                   lambda i, be, ns, nx, xr: (i, 0)),
            scratch_shapes=[pltpu.VMEM((D_MODEL, 2 * D_FF), F32),
                            pltpu.VMEM((D_FF, D_MODEL), F32),
                            pltpu.VMEM((D_MODEL, 2 * D_FF), BF16),
                            pltpu.VMEM((D_FF, D_MODEL), BF16),
                            pltpu.VMEM((2, ROW_BLOCK * ROW_TILES, LANES), F32),
                            pltpu.VMEM((ROW_BLOCK, D_MODEL), BF16),
                            pltpu.VMEM((ROW_BLOCK, D_FF), BF16),
                            pltpu.SemaphoreType.DMA((2,)),
                            pltpu.SemaphoreType.DMA((2,))]),
        compiler_params=_cparams(("arbitrary",)),
        name="moe_ffn",
    )(block_e, nsub, next_e, xrow, xs, w1, b1, w2, b2)


def _combine_kernel(pos_hbm, src_hbm, yb_hbm, gate_ref, h1_ref, g2_ref, nw_ref, o_ref,
                    pos_smem, src_smem, win, hbuf, sem_tab, sem_win, *, n_tiles):
    i = pl.program_id(0)
    slot = lax.rem(i, 2)
    tile = COMBINE_TILE
    n_asg = tile * TOP_K
    chunk = WIN_ROWS * ROW_TILES
    slot_rows = WIN_CHUNKS * chunk

    def table_copies(j, s):
        return (pltpu.make_async_copy(pos_hbm.at[j], pos_smem.at[pl.ds(s * n_asg, n_asg)],
                                      sem_tab.at[s]),
                pltpu.make_async_copy(src_hbm.at[j], src_smem.at[pl.ds(s * n_asg, n_asg)],
                                      sem_tab.at[s]))

    def fetch_window(s):
        def one(c, carry):
            src = pl.multiple_of(src_smem[s * n_asg + c], ROW_TILES)
            dst = pl.multiple_of(s * slot_rows + c * chunk, chunk)
            pltpu.make_async_copy(yb_hbm.at[pl.ds(src, chunk), :], win.at[pl.ds(dst, chunk), :],
                                  sem_win.at[s]).start()
            return carry
        lax.fori_loop(0, WIN_CHUNKS, one, 0, unroll=2)

    @pl.when(i == 0)
    def _():
        for cp in table_copies(0, 0):
            cp.start()
        for cp in table_copies(0, 0):
            cp.wait()
        fetch_window(0)
        if n_tiles > 1:
            for cp in table_copies(1, 1):
                cp.start()

    @pl.when(i + 1 < n_tiles)
    def _():
        for cp in table_copies(i + 1, 1 - slot):
            cp.wait()
        fetch_window(1 - slot)

    pltpu.make_async_copy(yb_hbm.at[pl.ds(0, slot_rows), :],
                          win.at[pl.ds(pl.multiple_of(slot * slot_rows, slot_rows), slot_rows), :],
                          sem_win.at[slot]).wait()

    g2 = g2_ref[...]
    base = slot * n_asg

    def token(t, carry):
        e0 = base + t * TOP_K
        acc = None
        for k in range(TOP_K):
            p = pl.multiple_of(pos_smem[e0 + k], ROW_TILES)
            term = gate_ref[pl.ds(t * TOP_K + k, 1), :] * win[pl.ds(p, ROW_TILES), :]
            acc = term if acc is None else acc + term
        r0 = pl.multiple_of(t * ROW_TILES, ROW_TILES)
        hbuf[pl.ds(r0, ROW_TILES), :] = h1_ref[pl.ds(r0, ROW_TILES), :] + g2 * acc
        return carry
    lax.fori_loop(0, tile, token, 0, unroll=4)

    @pl.when(i + 2 < n_tiles)
    def _():
        for cp in table_copies(i + 2, slot):
            cp.start()

    h = _load_tile_rows(hbuf, tile)
    ms = jnp.mean(h * h, axis=-1, keepdims=True)
    o_ref[...] = h * lax.rsqrt(ms + EPS) * nw_ref[...]


def _combine(pos_tiles, src_tiles, yb, gate_rows, h1, g2_tiles, nw, *, tiles_per_batch):
    tile = COMBINE_TILE
    T = h1.shape[0] // ROW_TILES
    n_asg = tile * TOP_K
    return pl.pallas_call(
        functools.partial(_combine_kernel, n_tiles=T // tile),
        out_shape=jax.ShapeDtypeStruct((T, D_MODEL), F32),
        grid=(T // tile,),
        in_specs=[pl.BlockSpec(memory_space=pl.ANY),
                  pl.BlockSpec(memory_space=pl.ANY),
                  pl.BlockSpec(memory_space=pl.ANY),
                  pl.BlockSpec((tile * TOP_K, LANES), lambda i: (i, 0)),
                  pl.BlockSpec((tile * ROW_TILES, LANES), lambda i: (i, 0)),
                  pl.BlockSpec((None, ROW_TILES, LANES), lambda i: (i // tiles_per_batch, 0, 0)),
                  pl.BlockSpec((1, D_MODEL), lambda i: (0, 0))],
        out_specs=pl.BlockSpec((tile, D_MODEL), lambda i: (i, 0)),
        scratch_shapes=[pltpu.SMEM((2 * n_asg,), I32),
                        pltpu.SMEM((2 * n_asg,), I32),
                        pltpu.VMEM((2 * WIN_CHUNKS * WIN_ROWS * ROW_TILES, LANES), F32),
                        pltpu.VMEM((tile * ROW_TILES, LANES), F32),
                        pltpu.SemaphoreType.DMA((2,)),
                        pltpu.SemaphoreType.DMA((2,))],
        compiler_params=_cparams(("arbitrary",)),
        name="combine_norm",
    )(pos_tiles, src_tiles, yb, gate_rows, h1, g2_tiles, nw)


def _rope_tables(n):
    rows = n // GRID_W
    row = jnp.repeat(jnp.arange(rows, dtype=F32), GRID_W)
    col = jnp.tile(jnp.arange(GRID_W, dtype=F32), rows)
    n_freq = RET_DK // 4
    inv = ROPE_BASE ** (-jnp.arange(n_freq, dtype=F32) / n_freq)
    ang = jnp.concatenate([row[:, None] * inv, col[:, None] * inv], axis=-1)
    cos, sin = jnp.cos(ang), jnp.sin(ang)
    cos_h = jnp.concatenate([cos, cos], axis=-1)
    sin_h = jnp.concatenate([-sin, sin], axis=-1)
    return jnp.tile(cos_h, (1, RET_HEADS)), jnp.tile(sin_h, (1, RET_HEADS))


def kernel(x, c, ctx, c_ctx, w_ada, b_ada, norm_mix_w, norm_ffn_w, w_in, w_out, hg_lb,
           hg_norm_w, router_w, router_b, w1, b1, w2, b2, norm_final_w):
    B, N, D = x.shape
    C = ctx.shape[1]
    T = B * N
    assert D == D_MODEL and w_ada.shape[0] == 1

    cc = jnp.concatenate([c.astype(F32), c_ctx.astype(F32)[None, :],
                          jnp.zeros((16 - B - 1, D), F32)], axis=0)
    mod = _ada(cc, w_ada[0], b_ada[0][None, :])
    mod = mod.reshape(16, 6, 1, D).transpose(1, 0, 2, 3)
    sh1, sc1, g1, sh2, sc2, g2 = (mod[i] for i in range(6))

    w_in_bf = w_in[0].astype(BF16)
    w_out_bf = w_out[0].astype(BF16)
    nw_mix = norm_mix_w[0][None, :]
    cos_t, sin_t = _rope_tables(N)

    pb, lf = _inproj(x, sc1, sh1, None, nw_mix, w_in_bf, hg_lb[:2], cos_t, sin_t,
                     rope=True, tm=1024)
    pbc, lfc = _inproj(ctx, sc1, sh1, B, nw_mix, w_in_bf, hg_lb[:2], cos_t[:C], sin_t[:C],
                       rope=False, tm=C)

    a_ret = _retention(pb, pbc)
    a_hg = _hgrn(pb, lf, pbc, lfc, hg_norm_w[0][None, :])

    rw = jnp.zeros((D, LANES), F32).at[:, :N_EXPERTS].set(router_w[0])
    rb = jnp.full((1, LANES), -1e30, F32).at[0, :N_EXPERTS].set(router_b[0])
    h1, v, route, route_t, gate_rows, base_cnt, cnt = _outproj(a_ret, a_hg, w_out_bf, x, g1, sc2, sh2,
                                          norm_ffn_w[0][None, :], rw, rb, tm=512)

    del route
    idx = route_t[0:TOP_K].astype(I32)
    rank = route_t[2 * TOP_K:3 * TOP_K].astype(I32)
    counts = cnt[0, :N_EXPERTS].astype(I32)
    padded = (counts + ROW_BLOCK - 1) // ROW_BLOCK * ROW_BLOCK
    pend = jnp.cumsum(padded)
    pstart = pend - padded
    cstart = jnp.cumsum(counts) - counts
    onehot = idx[None] == jnp.arange(N_EXPERTS, dtype=I32)[:, None, None]

    def dest_tiles(first_row, tile):
        dest = (jnp.sum(jnp.where(onehot, first_row[:, None, None], 0), axis=0) + rank) * ROW_TILES
        return dest.reshape(TOP_K, T // tile, tile).transpose(1, 0, 2).reshape(T // tile,
                                                                               TOP_K * tile)
    n_blocks = (T * TOP_K) // ROW_BLOCK + N_EXPERTS
    starts = jnp.arange(n_blocks, dtype=I32) * ROW_BLOCK
    block_e = jnp.minimum(jnp.sum((pend[None, :] <= starts[:, None]).astype(I32), axis=1),
                          N_EXPERTS - 1)
    valid = jnp.clip(counts[block_e] - (starts - pstart[block_e]), 0, ROW_BLOCK)
    nsub = (valid + SUB_ROWS - 1) // SUB_ROWS
    first_from = lax.cummin(jnp.where(counts > 0, jnp.arange(N_EXPERTS, dtype=I32), N_EXPERTS),
                            reverse=True)
    next_e = jnp.concatenate([first_from[1:], jnp.full((1,), N_EXPERTS, I32)])
    next_e = jnp.where(next_e == N_EXPERTS, -1, next_e)
    xrow = jnp.where(nsub > 0, cstart[block_e] + starts - pstart[block_e], 0) * ROW_TILES

    tile_d = 512
    xs = _dispatch(dest_tiles(cstart, tile_d), v, tile=tile_d)
    yb = _moe(block_e, nsub, next_e, xrow.astype(I32), xs, w1[0], b1[0][:, None, :], w2[0],
              b2[0][:, None, :], n_blocks=n_blocks)

    tc = COMBINE_TILE
    n_ct = T // tc
    pieces = 512 // tc
    base = base_cnt.reshape(T // 512, 8, LANES)[:, :pieces, :N_EXPERTS].reshape(n_ct, N_EXPERTS)
    base = base.astype(I32)
    n_run = jnp.concatenate([base[1:], counts[None, :]], axis=0) - base
    n_chunk = (n_run + WIN_ROWS - 1) // WIN_ROWS
    chunk_end = jnp.cumsum(n_chunk, axis=1)
    chunk_start = chunk_end - n_chunk
    win_off = chunk_start * WIN_ROWS
    c_id = jnp.arange(WIN_CHUNKS, dtype=I32)
    c_exp = jnp.sum((chunk_end[:, None, :] <= c_id[None, :, None]).astype(I32), axis=2)
    c_real = c_exp < N_EXPERTS
    c_own = c_exp[:, :, None] == jnp.arange(N_EXPERTS, dtype=I32)
    take = lambda tab: jnp.sum(jnp.where(c_own, tab[:, None, :], 0), axis=2)
    c_src = (take(pstart[None, :] + base) + (c_id[None, :] - take(chunk_start)) * WIN_ROWS)
    src_tiles = jnp.concatenate(
        [jnp.where(c_real, c_src, 0) * ROW_TILES,
         jnp.zeros((n_ct, tc * TOP_K - WIN_CHUNKS), I32)], axis=1)
    slot_off = (jnp.arange(n_ct, dtype=I32) % 2) * (WIN_CHUNKS * WIN_ROWS)
    shift = jnp.repeat(win_off - base + slot_off[:, None], tc, axis=0).T
    pos = (jnp.sum(jnp.where(onehot, shift[:, None, :], 0), axis=0) + rank) * ROW_TILES

    def per_tile(a):
        return a.reshape(TOP_K, n_ct, tc).transpose(1, 2, 0).reshape(n_ct, tc * TOP_K)

    out = _combine(per_tile(pos), src_tiles, yb, gate_rows, h1,
                   g2.reshape(16, ROW_TILES, LANES), norm_final_w[None, :],
                   tiles_per_batch=N // tc)
    return out.reshape(B, N, D)
```

```python
import functools
import math

import jax
import jax.numpy as jnp
import numpy as np
from jax import lax
from jax.experimental import pallas as pl
from jax.experimental.pallas import tpu as pltpu

F32 = jnp.float32
BF16 = jnp.bfloat16
I32 = jnp.int32

D_MODEL = 1024
GRID_W = 64
RET_HEADS = 4
RET_DK = 64
HG_HEADS = 4
PROJ_W = 4096
ROPE_BASE = 10000.0
EPS = 1e-6
N_EXPERTS = 32
TOP_K = 4
D_FF = 1024
SWIGLU_LIMIT = 7.0
SWIGLU_ALPHA = 1.702
LOG2_E = 1.0 / math.log(2.0)

LANES = 128
CHUNK = 128
ROW_BLOCK = 1024
SUB_ROWS = 128
COMBINE_TILE = 512
WIN_ROWS = 16
WIN_CHUNKS = (COMBINE_TILE * TOP_K + N_EXPERTS * (WIN_ROWS - 1) + WIN_ROWS - 1) // WIN_ROWS
VMEM_LIMIT = 56 * 1024 * 1024
ROW_TILES = D_MODEL // LANES

C_RQ, C_RK, C_RV, C_RG, C_HQ, C_FF, C_FB, C_HV, C_HG = (
    0, 256, 512, 1024, 1536, 2048, 2560, 3072, 3584)


def _cparams(sem):
    return pltpu.CompilerParams(dimension_semantics=sem, vmem_limit_bytes=VMEM_LIMIT)


def _split_bf16(x):
    hi = x.astype(BF16)
    lo = (x - hi.astype(F32)).astype(BF16)
    return hi, lo


def _dot(a, b):
    return jnp.dot(a, b, preferred_element_type=F32)


def _dot_nt(a, b):
    return lax.dot_general(a, b, (((1,), (1,)), ((), ())), preferred_element_type=F32)


def _dot3(a, b):
    ah, al = _split_bf16(a)
    bh, bl = _split_bf16(b)
    return _dot(ah, bh) + (_dot(ah, bl) + _dot(al, bh))


def _silu(x):
    return x * jax.nn.sigmoid(x)


def _load_tile_rows(ref, n):
    return jnp.concatenate([ref[pl.ds(s, n, stride=ROW_TILES), :] for s in range(ROW_TILES)],
                           axis=1)


def _store_tile_rows(ref, x):
    n = x.shape[0]
    for s in range(ROW_TILES):
        ref[pl.ds(s, n, stride=ROW_TILES), :] = x[:, s * LANES:(s + 1) * LANES]


def _ada_kernel(c_ref, w_ref, b_ref, o_ref):
    s = _silu(c_ref[...])
    o_ref[...] = _dot3(s, w_ref[...]) + b_ref[...]


def _ada(cc, w, b):
    nblk = w.shape[1] // D_MODEL
    return pl.pallas_call(
        _ada_kernel,
        out_shape=jax.ShapeDtypeStruct((cc.shape[0], w.shape[1]), F32),
        grid=(nblk,),
        in_specs=[pl.BlockSpec(cc.shape, lambda j: (0, 0)),
                  pl.BlockSpec((D_MODEL, D_MODEL), lambda j: (0, j)),
                  pl.BlockSpec((1, D_MODEL), lambda j: (0, j))],
        out_specs=pl.BlockSpec((cc.shape[0], D_MODEL), lambda j: (0, j)),
        compiler_params=_cparams(("arbitrary",)),
        name="ada",
    )(cc, w, b)


def _inproj_kernel(x_ref, sc_ref, sh_ref, nw_ref, w_ref, lb_ref, cos_ref, sin_ref,
                   pb_ref, lf_ref, u_scr, *, rope):
    x = x_ref[...]
    ms = jnp.mean(x * x, axis=-1, keepdims=True)
    u = x * lax.rsqrt(ms + EPS) * nw_ref[...] * (1.0 + sc_ref[...]) + sh_ref[...]
    u_scr[...] = u.astype(BF16)

    def proj(lo, width):
        return _dot(u_scr[...], w_ref[:, lo:lo + width])

    tm = x.shape[0]
    if rope:
        lane = lax.broadcasted_iota(I32, (tm, LANES), 1)
        first = (lane & 32) == 0

    def put_rot(col, scale):
        for j in range(2):
            lo = col + j * LANES
            t = proj(lo, LANES)
            if scale != 1.0:
                t = t * scale
            if rope:
                tb = j * LANES
                rot = jnp.where(first, pltpu.roll(t, 96, axis=1), pltpu.roll(t, 32, axis=1))
                t = t * cos_ref[:, tb:tb + LANES] + rot * sin_ref[:, tb:tb + LANES]
            pb_ref[:, lo:lo + LANES] = t.astype(BF16)

    put_rot(C_RQ, 1.0)
    put_rot(C_RK, RET_DK ** -0.5)
    pb_ref[:, C_RV:C_RV + 512] = proj(C_RV, 512).astype(BF16)
    pb_ref[:, C_RG:C_RG + 512] = _silu(proj(C_RG, 512)).astype(BF16)
    pb_ref[:, C_HQ:C_HQ + 512] = _silu(proj(C_HQ, 512)).astype(BF16)
    pb_ref[:, C_HV:C_HV + 512] = proj(C_HV, 512).astype(BF16)
    pb_ref[:, C_HG:C_HG + 512] = _silu(proj(C_HG, 512)).astype(BF16)

    la = lb_ref[0]
    lbb = lb_ref[1]
    mx = jnp.maximum(la, lbb)
    ea = jnp.exp(la - mx)
    eb = jnp.exp(lbb - mx)
    lb = ea / (ea + eb)
    for d, col in enumerate((C_FF, C_FB)):
        lbd = lb[d:d + 1, :]
        f = lbd + (1.0 - lbd) * jax.nn.sigmoid(proj(col, 512))
        pb_ref[:, col:col + 512] = (1.0 - f).astype(BF16)
        lf_ref[:, d * 512:(d + 1) * 512] = jnp.log(f) * LOG2_E


def _inproj(x, sc, sh, mod_row, nw, w_bf, hg_lb, cos_t, sin_t, *, rope, tm):
    B, n, _ = x.shape
    nt = n // tm
    if mod_row is None:
        mrow = lambda b, j: (b, 0, 0)
    else:
        mrow = lambda b, j: (mod_row, 0, 0)
    return pl.pallas_call(
        functools.partial(_inproj_kernel, rope=rope),
        out_shape=(jax.ShapeDtypeStruct((B, n, PROJ_W), BF16),
                   jax.ShapeDtypeStruct((B, n, 1024), F32)),
        grid=(B, nt),
        in_specs=[pl.BlockSpec((None, tm, D_MODEL), lambda b, j: (b, j, 0)),
                  pl.BlockSpec((None, 1, D_MODEL), mrow),
                  pl.BlockSpec((None, 1, D_MODEL), mrow),
                  pl.BlockSpec((1, D_MODEL), lambda b, j: (0, 0)),
                  pl.BlockSpec((D_MODEL, PROJ_W), lambda b, j: (0, 0),
                               pipeline_mode=pl.Buffered(1)),
                  pl.BlockSpec((2, 2, 512), lambda b, j: (0, 0, 0)),
                  pl.BlockSpec((tm, 256), lambda b, j: (j, 0)),
                  pl.BlockSpec((tm, 256), lambda b, j: (j, 0))],
        out_specs=(pl.BlockSpec((None, tm, PROJ_W), lambda b, j: (b, j, 0)),
                   pl.BlockSpec((None, tm, 1024), lambda b, j: (b, j, 0))),
        scratch_shapes=[pltpu.VMEM((tm, D_MODEL), BF16)],
        compiler_params=_cparams(("arbitrary", "arbitrary")),
        name="inproj_rope" if rope else "inproj_ctx",
    )(x, sc, sh, nw, w_bf, hg_lb, cos_t, sin_t)


_RET_LGF = [math.log1p(-(2.0 ** (-5.0 - 2.0 * h))) for h in range(RET_HEADS)]
_RET_LGB = [math.log1p(-(2.0 ** (-6.0 - 2.0 * h))) for h in range(RET_HEADS)]


def _ret_kernel(q_ref, k_ref, v_ref, g_ref, kc_ref, vc_ref, o_ref,
                u_scr, sin_scr, kt_scr, dtot_scr, *, n_lat, n_ctx):
    L = CHUNK
    pair = pl.program_id(1)
    row = lax.broadcasted_iota(I32, (L, L), 0).astype(F32)
    col = lax.broadcasted_iota(I32, (L, L), 1).astype(F32)
    lane = lax.broadcasted_iota(I32, (L, LANES), 1)
    trow = lax.broadcasted_iota(I32, (L, 1), 0).astype(F32)
    tcol = lax.broadcasted_iota(I32, (1, L), 1).astype(F32)
    low_half = lane < RET_DK

    def u_chunk(k_blk, v_blk, ci, store_kt):
        kt = k_blk.astype(F32).T
        if store_kt is not None:
            kt_scr[store_kt] = kt.astype(BF16)
        for hh in range(2):
            lgf = jnp.where(pair == 0, _RET_LGF[hh], _RET_LGF[2 + hh])
            lgb = jnp.where(pair == 0, _RET_LGB[hh], _RET_LGB[2 + hh])
            kth = kt[hh * RET_DK:(hh + 1) * RET_DK, :]
            wkf = jnp.exp(lgf * (L - 1.0 - tcol))
            wkb = jnp.exp(lgb * tcol)
            lhs = jnp.concatenate([kth * wkf, kth * wkb], axis=0).astype(BF16)
            u_scr[hh, ci] = _dot(lhs, v_blk[:, hh * LANES:(hh + 1) * LANES])

    for c in range(n_ctx):
        u_chunk(kc_ref[c * L:(c + 1) * L, :], vc_ref[c * L:(c + 1) * L, :], c, None)

    def lat_u(c, carry):
        r0 = pl.multiple_of(c * L, L)
        u_chunk(k_ref[pl.ds(r0, L), :], v_ref[pl.ds(r0, L), :], n_ctx + c, c)
        return carry
    lax.fori_loop(0, n_lat, lat_u, 0, unroll=2)

    ones = jnp.ones((RET_DK, LANES), F32)
    for hh in range(2):
        lgf = jnp.where(pair == 0, _RET_LGF[hh], _RET_LGF[2 + hh])
        lgb = jnp.where(pair == 0, _RET_LGB[hh], _RET_LGB[2 + hh])
        d = row - col
        dtot_scr[hh] = jnp.where(d > 0, jnp.exp(lgf * jnp.maximum(d, 0.0)),
                                 jnp.where(d < 0, jnp.exp(lgb * jnp.maximum(-d, 0.0)), 2.0))
        af = jnp.exp(ones * (lgf * L))
        ab = jnp.exp(ones * (lgb * L))

        s = jnp.zeros((RET_DK, LANES), F32)
        for c in range(n_ctx):
            s = af * s + u_scr[hh, c, 0:RET_DK, :]
        sb = jnp.zeros((RET_DK, LANES), F32)
        for c in reversed(range(n_ctx)):
            sb = ab * sb + u_scr[hh, c, RET_DK:2 * RET_DK, :]

        def fwd(c, s, hh=hh, af=af):
            sin_scr[hh, c, 0:RET_DK, :] = s.astype(BF16)
            return af * s + u_scr[hh, n_ctx + c, 0:RET_DK, :]
        lax.fori_loop(0, n_lat, fwd, s)

        def bwd(i, sb, hh=hh, ab=ab):
            c = n_lat - 1 - i
            sin_scr[hh, c, RET_DK:2 * RET_DK, :] = sb.astype(BF16)
            return ab * sb + u_scr[hh, n_ctx + c, RET_DK:2 * RET_DK, :]
        lax.fori_loop(0, n_lat, bwd, sb)

    def out_chunk(c, carry):
        r0 = pl.multiple_of(c * L, L)
        q = q_ref[pl.ds(r0, L), :].astype(F32)
        qr = pltpu.roll(q, RET_DK, axis=1)
        kt = kt_scr[c]
        for hh in range(2):
            lgf = jnp.where(pair == 0, _RET_LGF[hh], _RET_LGF[2 + hh])
            lgb = jnp.where(pair == 0, _RET_LGB[hh], _RET_LGB[2 + hh])
            mine = low_half if hh == 0 else jnp.logical_not(low_half)
            qm = jnp.where(mine, q, 0.0).astype(BF16)
            p = (_dot(qm, kt) * dtot_scr[hh]).astype(BF16)
            vh = v_ref[pl.ds(r0, L), hh * LANES:(hh + 1) * LANES]
            wqf = jnp.exp(lgf * (trow + 1.0))
            wqb = jnp.exp(lgb * (L - trow))
            qa, qb = (q, qr) if hh == 0 else (qr, q)
            qs = jnp.where(low_half, qa * wqf, qb * wqb).astype(BF16)
            o = _dot(p, vh) + _dot(qs, sin_scr[hh, c])
            ms = jnp.mean(o * o, axis=-1, keepdims=True)
            gh = g_ref[pl.ds(r0, L), hh * LANES:(hh + 1) * LANES].astype(F32)
            o_ref[pl.ds(r0, L), hh * LANES:(hh + 1) * LANES] = (
                o * lax.rsqrt(ms + EPS) * gh).astype(BF16)
        return carry
    lax.fori_loop(0, n_lat, out_chunk, 0, unroll=2)


def _retention(pb, pbc):
    B, n, _ = pb.shape
    nc = pbc.shape[1]
    n_lat, n_ctx = n // CHUNK, nc // CHUNK
    return pl.pallas_call(
        functools.partial(_ret_kernel, n_lat=n_lat, n_ctx=n_ctx),
        out_shape=jax.ShapeDtypeStruct((B, n, 512), BF16),
        grid=(B, 2),
        in_specs=[pl.BlockSpec((None, n, LANES), lambda b, p: (b, 0, C_RQ // LANES + p)),
                  pl.BlockSpec((None, n, LANES), lambda b, p: (b, 0, C_RK // LANES + p)),
                  pl.BlockSpec((None, n, 256), lambda b, p: (b, 0, C_RV // 256 + p)),
                  pl.BlockSpec((None, n, 256), lambda b, p: (b, 0, C_RG // 256 + p)),
                  pl.BlockSpec((None, nc, LANES), lambda b, p: (b, 0, C_RK // LANES + p)),
                  pl.BlockSpec((None, nc, 256), lambda b, p: (b, 0, C_RV // 256 + p))],
        out_specs=pl.BlockSpec((None, n, 256), lambda b, p: (b, 0, p)),
        scratch_shapes=[pltpu.VMEM((2, n_lat + n_ctx, CHUNK, LANES), F32),
                        pltpu.VMEM((2, n_lat, CHUNK, LANES), BF16),
                        pltpu.VMEM((n_lat, LANES, CHUNK), BF16),
                        pltpu.VMEM((2, CHUNK, CHUNK), F32)],
        compiler_params=_cparams(("arbitrary", "arbitrary")),
        name="retention",
    )(pb, pb, pb, pb, pbc, pbc)


_LEVELS = (64, 32, 16, 8, 4, 2, 1)


def _expand_rows(r, rep):
    n = r.shape[0]
    if n == 1:
        return jnp.broadcast_to(r, (rep, r.shape[1]))
    return jnp.concatenate(
        [jnp.broadcast_to(r[i:i + 1, :], (rep, r.shape[1])) for i in range(n)], axis=0)


def _hgrn_kernel(q_ref, kf_ref, kb_ref, v_ref, g_ref, lff_ref, lfb_ref,
                 kfc_ref, kbc_ref, vc_ref, lffc_ref, lfbc_ref, nw_ref, o_ref,
                 ut_scr, a_scr, qs_scr, oi_scr, sin_scr, bfb_scr, *, n_lat, n_ctx):
    L = CHUNK
    row = lax.broadcasted_iota(I32, (L, L), 0)
    col = lax.broadcasted_iota(I32, (L, L), 1)
    xr_bits = lax.bitcast_convert_type((row ^ col).astype(F32), I32)
    lv = lax.shift_right_logical(xr_bits, 23) - 127
    row2 = lax.broadcasted_iota(I32, (L, 2 * L), 0)
    col2 = lax.broadcasted_iota(I32, (L, 2 * L), 1) & (L - 1)
    tril2 = jnp.where(col2 <= row2, 1.0, 0.0).astype(BF16)
    triu2 = jnp.where(col2 >= row2, 1.0, 0.0).astype(BF16)

    def cums(lff, lfb):
        hf, lof = _split_bf16(lff)
        hb, lob = _split_bf16(lfb)
        bf = _dot(tril2, jnp.concatenate([hf, lof], axis=0))
        bb = _dot(triu2, jnp.concatenate([hb, lob], axis=0))
        return bf, bb

    def state_part(ci, kf, kb, v_blk, bf, bb):
        endf = bf[L - 1:L, :]
        endb = bb[0:1, :]
        ksf = kf * jnp.exp2(endf - bf)
        ksb = kb * jnp.exp2(endb - bb)
        vt = v_blk.astype(F32).T.astype(BF16)
        ut_scr[ci] = _dot(vt, jnp.concatenate([ksf, ksb], axis=1).astype(BF16))
        a_scr[ci] = jnp.broadcast_to(
            jnp.concatenate([jnp.exp2(endf), jnp.exp2(endb)], axis=1), (8, 2 * LANES))

    for c in range(n_ctx):
        sl = slice(c * L, (c + 1) * L)
        bf, bb = cums(lffc_ref[sl, :], lfbc_ref[sl, :])
        state_part(c, kfc_ref[sl, :].astype(F32), kbc_ref[sl, :].astype(F32),
                   vc_ref[sl, :], bf, bb)

    def lat_chunk(c, u):
        rows = pl.ds(pl.multiple_of(c * L, L), L)
        bf_scr = bfb_scr.at[u, 0]
        bb_scr = bfb_scr.at[u, 1]
        q = q_ref[rows, :].astype(F32)
        kf = kf_ref[rows, :].astype(F32)
        kb = kb_ref[rows, :].astype(F32)
        v_blk = v_ref[rows, :]
        lff = lff_ref[rows, :]
        lfb = lfb_ref[rows, :]
        bf, bb = cums(lff, lfb)
        state_part(n_ctx + c, kf, kb, v_blk, bf, bb)
        qs_scr[rows, :] = jnp.concatenate([q * jnp.exp2(bf), q * jnp.exp2(bb)],
                                          axis=1).astype(BF16)
        bf_scr[...] = bf
        bb_scr[...] = bb

        acc = jnp.zeros((L, L), F32)
        for lvl, h in enumerate(_LEVELS):
            bit = (row & h) != 0
            ksel = jnp.where(bit, kb, kf)
            if h >= 4:
                n = (L // 2) // h
                if n == 1:
                    rf = bf_scr[h - 1:h, :]
                    rb = bb_scr[h:h + 1, :]
                else:
                    rf = bf_scr[pl.ds(h - 1, n, stride=2 * h), :]
                    rb = bb_scr[pl.ds(h, n, stride=2 * h), :]
                df = bf - _expand_rows(rf, 2 * h)
                db = bb - _expand_rows(rb, 2 * h)
                eq = jnp.where(bit, df, db)
                ek = -jnp.where(bit, db, df)
            elif h == 2:
                m = row & 3
                lff_n = pltpu.roll(lff, L - 1, axis=0)
                lfb_n = pltpu.roll(lfb, L - 1, axis=0)
                eq = jnp.where(m == 2, lff,
                               jnp.where(m == 3, lff + pltpu.roll(lff, 1, axis=0),
                                         jnp.where(m == 0, lfb + lfb_n, lfb)))
                ek = jnp.where(m == 3, pltpu.roll(lfb, 1, axis=0),
                               jnp.where(m == 0, lff_n, 0.0))
            else:
                eq = jnp.where(bit, lff, lfb)
                ek = None
            lhs = (q * jnp.exp2(eq)).astype(BF16)
            rhs = (ksel if ek is None else ksel * jnp.exp2(ek)).astype(BF16)
            acc = jnp.where(lv == 6 - lvl, _dot_nt(lhs, rhs), acc)

        dsum = jnp.sum(q * (kf + kb), axis=-1, keepdims=True)
        oi_scr[rows, :] = _dot(acc.astype(BF16), v_blk) + dsum * v_blk.astype(F32)

    def lat_pair(i, carry):
        for u in range(4):
            lat_chunk(4 * i + u, u)
        return carry
    lax.fori_loop(0, n_lat // 4, lat_pair, 0)

    st = jnp.zeros((LANES, LANES), F32)
    for c in range(n_ctx):
        st = st * a_scr[c, 0:1, 0:LANES] + ut_scr[c, :, 0:LANES]
    stb = jnp.zeros((LANES, LANES), F32)
    for c in reversed(range(n_ctx)):
        stb = stb * a_scr[c, 0:1, LANES:2 * LANES] + ut_scr[c, :, LANES:2 * LANES]

    def fwd(c, st):
        sin_scr[c, :, 0:LANES] = st.astype(BF16)
        ci = n_ctx + c
        return st * a_scr[ci, 0:1, 0:LANES] + ut_scr[ci, :, 0:LANES]
    lax.fori_loop(0, n_lat, fwd, st)

    def bwd(i, stb):
        c = n_lat - 1 - i
        sin_scr[c, :, LANES:2 * LANES] = stb.astype(BF16)
        ci = n_ctx + c
        return stb * a_scr[ci, 0:1, LANES:2 * LANES] + ut_scr[ci, :, LANES:2 * LANES]
    lax.fori_loop(0, n_lat, bwd, stb)

    def out_chunk(c, carry):
        rows = pl.ds(pl.multiple_of(c * L, L), L)
        o = oi_scr[rows, :] + _dot_nt(qs_scr[rows, :], sin_scr[c])
        ms = jnp.mean(o * o, axis=-1, keepdims=True)
        y = o * lax.rsqrt(ms + EPS) * nw_ref[...] * g_ref[rows, :].astype(F32)
        o_ref[rows, :] = y.astype(BF16)
        return carry
    lax.fori_loop(0, n_lat, out_chunk, 0, unroll=4)


def _hgrn(pb, lf, pbc, lfc, nw):
    B, n, _ = pb.shape
    nc = pbc.shape[1]
    n_lat, n_ctx = n // CHUNK, nc // CHUNK

    def colblk(rows, col0):
        return pl.BlockSpec((None, rows, LANES), lambda b, h: (b, 0, col0 // LANES + h))

    return pl.pallas_call(
        functools.partial(_hgrn_kernel, n_lat=n_lat, n_ctx=n_ctx),
        out_shape=jax.ShapeDtypeStruct((B, n, 512), BF16),
        grid=(B, HG_HEADS),
        in_specs=[colblk(n, C_HQ), colblk(n, C_FF), colblk(n, C_FB), colblk(n, C_HV),
                  colblk(n, C_HG), colblk(n, 0), colblk(n, 512),
                  colblk(nc, C_FF), colblk(nc, C_FB), colblk(nc, C_HV),
                  colblk(nc, 0), colblk(nc, 512),
                  pl.BlockSpec((1, LANES), lambda b, h: (0, 0))],
        out_specs=pl.BlockSpec((None, n, LANES), lambda b, h: (b, 0, h)),
        scratch_shapes=[pltpu.VMEM((n_lat + n_ctx, LANES, 2 * LANES), F32),
                        pltpu.VMEM((n_lat + n_ctx, 8, 2 * LANES), F32),
                        pltpu.VMEM((n, 2 * LANES), BF16),
                        pltpu.VMEM((n, LANES), F32),
                        pltpu.VMEM((n_lat, LANES, 2 * LANES), BF16),
                        pltpu.VMEM((4, 2, CHUNK, LANES), F32)],
        compiler_params=_cparams(("arbitrary", "arbitrary")),
        name="hgrn2",
    )(pb, pb, pb, pb, pb, lf, lf, pbc, pbc, pbc, lfc, lfc, nw)


def _outproj_kernel(ar_ref, ah_ref, w_ref, x_ref, g1_ref, sc_ref, sh_ref, nw_ref,
                    rw_ref, rb_ref, h1_ref, v_ref, route_ref, route_t_ref, gate_ref, base_ref, cnt_ref,
                    cnt_scr):
    first_step = jnp.logical_and(pl.program_id(0) == 0, pl.program_id(1) == 0)

    @pl.when(first_step)
    def _():
        cnt_scr[...] = jnp.zeros_like(cnt_scr)

    y = _dot(ar_ref[...], w_ref[0:512, :]) + _dot(ah_ref[...], w_ref[512:1024, :])
    h1 = x_ref[...] + g1_ref[...] * y
    _store_tile_rows(h1_ref, h1)
    ms = jnp.mean(h1 * h1, axis=-1, keepdims=True)
    v = h1 * lax.rsqrt(ms + EPS) * nw_ref[...] * (1.0 + sc_ref[...]) + sh_ref[...]
    _store_tile_rows(v_ref, v)

    tm = v.shape[0]
    logits = _dot3(v, rw_ref[...]) + rb_ref[...]
    lane = lax.broadcasted_iota(I32, (tm, LANES), 1)
    lane_f = lane.astype(F32)
    l = logits
    sels, tops, idxs = [], [], []
    for _ in range(TOP_K):
        m = jnp.max(l, axis=1, keepdims=True)
        i = jnp.min(jnp.where(l == m, lane_f, float(LANES)), axis=1, keepdims=True)
        sel = lane_f == i
        l = jnp.where(sel, -jnp.inf, l)
        sels.append(sel)
        tops.append(m)
        idxs.append(i)
    es = [jnp.exp(t - tops[0]) for t in tops]
    den = es[0] + es[1] + es[2] + es[3]
    gates = [e / den for e in es]
    for k in range(TOP_K):
        gate_ref[pl.ds(k, tm, stride=TOP_K), :] = jnp.broadcast_to(gates[k], (tm, LANES))

    oh = jnp.zeros((tm, LANES), F32)
    for sel in sels:
        oh = jnp.where(sel, 1.0, oh)
    r = lax.broadcasted_iota(I32, (tm, tm), 0)
    c = lax.broadcasted_iota(I32, (tm, tm), 1)
    tri = jnp.where(c < r, 1.0, 0.0).astype(BF16)
    before = _dot(tri, oh.astype(BF16)) + cnt_scr[0:1, :]
    ranks = [jnp.sum(jnp.where(sel, before, 0.0), axis=1, keepdims=True) for sel in sels]
    base = cnt_scr[0:1, :]
    pieces = []
    for p in range(tm // COMBINE_TILE):
        pieces.append(base)
        base = base + jnp.sum(oh[p * COMBINE_TILE:(p + 1) * COMBINE_TILE, :], axis=0, keepdims=True)
    pieces.append(jnp.zeros((8 - len(pieces), LANES), F32))
    base_ref[...] = jnp.concatenate(pieces, axis=0)
    cnt_scr[...] = jnp.broadcast_to(base, cnt_scr.shape)
    cnt_ref[...] = cnt_scr[...]

    out = jnp.zeros((tm, LANES), F32)
    for k in range(TOP_K):
        out = jnp.where(lane == k, idxs[k], out)
        out = jnp.where(lane == TOP_K + k, gates[k], out)
        out = jnp.where(lane == 2 * TOP_K + k, ranks[k], out)
    route_ref[...] = out
    route_t_ref[...] = out.T[0:16, :]


def _outproj(a_ret, a_hg, w_bf, x, g1, sc2, sh2, nw, rw, rb, *, tm):
    B, n, _ = x.shape
    nt = n // tm
    T = B * n
    mrow = lambda b, j: (b, 0, 0)
    tok = lambda b, j: (b * nt + j, 0)
    return pl.pallas_call(
        _outproj_kernel,
        out_shape=(jax.ShapeDtypeStruct((T * ROW_TILES, LANES), F32),
                   jax.ShapeDtypeStruct((T * ROW_TILES, LANES), F32),
                   jax.ShapeDtypeStruct((T, LANES), F32),
                   jax.ShapeDtypeStruct((16, T), F32),
                   jax.ShapeDtypeStruct((T * TOP_K, LANES), F32),
                   jax.ShapeDtypeStruct((B * nt * 8, LANES), F32),
                   jax.ShapeDtypeStruct((8, LANES), F32)),
        grid=(B, nt),
        in_specs=[pl.BlockSpec((None, tm, 512), lambda b, j: (b, j, 0)),
                  pl.BlockSpec((None, tm, 512), lambda b, j: (b, j, 0)),
                  pl.BlockSpec((D_MODEL, D_MODEL), lambda b, j: (0, 0)),
                  pl.BlockSpec((None, tm, D_MODEL), lambda b, j: (b, j, 0)),
                  pl.BlockSpec((None, 1, D_MODEL), mrow),
                  pl.BlockSpec((None, 1, D_MODEL), mrow),
                  pl.BlockSpec((None, 1, D_MODEL), mrow),
                  pl.BlockSpec((1, D_MODEL), lambda b, j: (0, 0)),
                  pl.BlockSpec((D_MODEL, LANES), lambda b, j: (0, 0)),
                  pl.BlockSpec((1, LANES), lambda b, j: (0, 0))],
        out_specs=(pl.BlockSpec((tm * ROW_TILES, LANES), tok),
                   pl.BlockSpec((tm * ROW_TILES, LANES), tok),
                   pl.BlockSpec((tm, LANES), tok),
                   pl.BlockSpec((16, tm), lambda b, j: (0, b * nt + j)),
                   pl.BlockSpec((tm * TOP_K, LANES), tok),
                   pl.BlockSpec((8, LANES), tok),
                   pl.BlockSpec((8, LANES), lambda b, j: (0, 0))),
        scratch_shapes=[pltpu.VMEM((8, LANES), F32)],
        compiler_params=_cparams(("arbitrary", "arbitrary")),
        name="outproj_router",
    )(a_ret, a_hg, w_bf, x, g1, sc2, sh2, nw, rw, rb)


def _dispatch_kernel(dest_hbm, v_ref, xs_hbm, idx_smem, zbuf, sem_idx, sem_z, sem_rows,
                     *, tile, n_tiles, n_rows):
    i = pl.program_id(0)
    slot = lax.rem(i, 2)

    def idx_copy(j, s):
        n_asg = tile * TOP_K
        return pltpu.make_async_copy(dest_hbm.at[j], idx_smem.at[pl.ds(s * n_asg, n_asg)],
                                     sem_idx.at[s])

    def zero_copy(j):
        r0 = (n_rows + j * SUB_ROWS) * ROW_TILES
        return pltpu.make_async_copy(zbuf, xs_hbm.at[pl.ds(r0, SUB_ROWS * ROW_TILES), :], sem_z)

    @pl.when(i == 0)
    def _():
        idx_copy(0, 0).start()
        zbuf[...] = jnp.zeros_like(zbuf)
        for j in range(ROW_BLOCK // SUB_ROWS):
            zero_copy(j).start()
        for j in range(ROW_BLOCK // SUB_ROWS):
            zero_copy(j).wait()

    idx_copy(i, slot).wait()

    @pl.when(i + 1 < n_tiles)
    def _():
        idx_copy(i + 1, 1 - slot).start()

    def issue(t, carry):
        for k in range(TOP_K):
            d = pl.multiple_of(idx_smem[slot * (tile * TOP_K) + k * tile + t], ROW_TILES)
            src = v_ref.at[pl.ds(pl.multiple_of(t * ROW_TILES, ROW_TILES), ROW_TILES), :]
            pltpu.make_async_copy(src, xs_hbm.at[pl.ds(d, ROW_TILES), :],
                                  sem_rows).start(priority=k % 2)
        return carry
    lax.fori_loop(0, tile, issue, 0, unroll=4)
    for _ in range(TOP_K):
        pltpu.make_async_copy(v_ref, xs_hbm.at[pl.ds(0, tile * ROW_TILES), :], sem_rows).wait()


def _dispatch(dest_tiles, v, *, tile):
    T = v.shape[0] // ROW_TILES
    n_rows = T * TOP_K
    return pl.pallas_call(
        functools.partial(_dispatch_kernel, tile=tile, n_tiles=T // tile, n_rows=n_rows),
        out_shape=jax.ShapeDtypeStruct(((n_rows + ROW_BLOCK) * ROW_TILES, LANES), F32),
        grid=(T // tile,),
        in_specs=[pl.BlockSpec(memory_space=pl.ANY),
                  pl.BlockSpec((tile * ROW_TILES, LANES), lambda i: (i, 0))],
        out_specs=pl.BlockSpec(memory_space=pl.ANY),
        scratch_shapes=[pltpu.SMEM((2 * tile * TOP_K,), I32),
                        pltpu.VMEM((SUB_ROWS * ROW_TILES, LANES), F32),
                        pltpu.SemaphoreType.DMA((2,)),
                        pltpu.SemaphoreType.DMA,
                        pltpu.SemaphoreType.DMA],
        compiler_params=_cparams(("arbitrary",)),
        name="dispatch",
    )(dest_tiles, v)


def _moe_kernel(be_ref, ns_ref, nx_ref, xr_ref, x_hbm, w1_hbm, b1_ref, w2_hbm, b2_ref, y_ref,
                w1_stage, w2_stage, w1_scr, w2_scr, xbuf, x_scr, act_scr, sem_w, sem_x,
                *, n_blocks):
    i = pl.program_id(0)
    slot = lax.rem(i, 2)
    e = be_ref[i]
    nsub = ns_ref[i]
    changed = jnp.logical_or(i == 0, e != be_ref[jnp.maximum(i - 1, 0)])

    def x_copy(j, s):
        r0 = pl.multiple_of(xr_ref[j], ROW_TILES)
        return pltpu.make_async_copy(x_hbm.at[pl.ds(r0, ROW_BLOCK * ROW_TILES), :], xbuf.at[s],
                                     sem_x.at[s])

    @pl.when(i == 0)
    def _():
        x_copy(0, 0).start()

    @pl.when(nsub > 0)
    def _():
        x_copy(i, slot).wait()

    nxt_blk = jnp.minimum(i + 1, n_blocks - 1)

    @pl.when(jnp.logical_and(i + 1 < n_blocks, ns_ref[nxt_blk] > 0))
    def _():
        x_copy(nxt_blk, 1 - slot).start()

    def weight_copies(ex):
        return (pltpu.make_async_copy(w1_hbm.at[ex], w1_stage, sem_w.at[0]),
                pltpu.make_async_copy(w2_hbm.at[ex], w2_stage, sem_w.at[1]))

    @pl.when(i == 0)
    def _():
        for cp in weight_copies(e):
            cp.start()

    @pl.when(jnp.logical_and(changed, nsub > 0))
    def _():
        for cp in weight_copies(e):
            cp.wait()
        w1_scr[...] = w1_stage[...].astype(BF16)
        w2_scr[...] = w2_stage[...].astype(BF16)
        nxt = nx_ref[e]

        @pl.when(nxt >= 0)
        def _():
            for cp in weight_copies(nxt):
                cp.start()

    def compute(rows):
        x_scr[0:rows, :] = _load_tile_rows(xbuf.at[slot], rows).astype(BF16)
        cw = 256
        for c in range(D_FF // cw):
            glu = (_dot(x_scr[0:rows, :], w1_scr[:, c * cw:(c + 1) * cw])
                   + b1_ref[:, c * cw:(c + 1) * cw])
            lin = (_dot(x_scr[0:rows, :], w1_scr[:, D_FF + c * cw:D_FF + (c + 1) * cw])
                   + b1_ref[:, D_FF + c * cw:D_FF + (c + 1) * cw])
            glu = jnp.minimum(glu, SWIGLU_LIMIT)
            lin = jnp.clip(lin, -SWIGLU_LIMIT, SWIGLU_LIMIT)
            act = glu * jax.nn.sigmoid(SWIGLU_ALPHA * glu) * (lin + 1.0)
            act_scr[0:rows, c * cw:(c + 1) * cw] = act.astype(BF16)
        for c in range(D_MODEL // cw):
            y = (_dot(act_scr[0:rows, :], w2_scr[:, c * cw:(c + 1) * cw])
                 + b2_ref[:, c * cw:(c + 1) * cw])
            for s in range(cw // LANES):
                t = c * (cw // LANES) + s
                y_ref[pl.ds(t, rows, stride=ROW_TILES), :] = y[:, s * LANES:(s + 1) * LANES]
        if rows < ROW_BLOCK:
            y_ref[rows * ROW_TILES:, :] = jnp.zeros(((ROW_BLOCK - rows) * ROW_TILES, LANES), F32)

    for m in range(1, ROW_BLOCK // SUB_ROWS + 1):
        @pl.when(nsub == m)
        def _(m=m):
            compute(m * SUB_ROWS)

    @pl.when(nsub == 0)
    def _():
        y_ref[...] = jnp.zeros_like(y_ref)


def _moe(block_e, nsub, next_e, xrow, xs, w1, b1, w2, b2, *, n_blocks):
    return pl.pallas_call(
        functools.partial(_moe_kernel, n_blocks=n_blocks),
        out_shape=jax.ShapeDtypeStruct((n_blocks * ROW_BLOCK * ROW_TILES, LANES), F32),
        grid_spec=pltpu.PrefetchScalarGridSpec(
            num_scalar_prefetch=4,
            grid=(n_blocks,),
            in_specs=[pl.BlockSpec(memory_space=pl.ANY),
                      pl.BlockSpec(memory_space=pl.ANY),
                      pl.BlockSpec((None, 1, 2 * D_FF), lambda i, be, ns, nx, xr: (be[i], 0, 0)),
                      pl.BlockSpec(memory_space=pl.ANY),
                      pl.BlockSpec((None, 1, D_MODEL), lambda i, be, ns, nx, xr: (be[i], 0, 0))],
            out_specs=pl.BlockSpec((ROW_BLOCK * ROW_TILES, LANES),
                                   lambda i, be, ns, nx, xr: (i, 0)),
            scratch_shapes=[pltpu.VMEM((D_MODEL, 2 * D_FF), F32),
                            pltpu.VMEM((D_FF, D_MODEL), F32),
                            pltpu.VMEM((D_MODEL, 2 * D_FF), BF16),
                            pltpu.VMEM((D_FF, D_MODEL), BF16),
                            pltpu.VMEM((2, ROW_BLOCK * ROW_TILES, LANES), F32),
                            pltpu.VMEM((ROW_BLOCK, D_MODEL), BF16),
                            pltpu.VMEM((ROW_BLOCK, D_FF), BF16),
                            pltpu.SemaphoreType.DMA((2,)),
                            pltpu.SemaphoreType.DMA((2,))]),
        compiler_params=_cparams(("arbitrary",)),
        name="moe_ffn",
    )(block_e, nsub, next_e, xrow, xs, w1, b1, w2, b2)


def _combine_kernel(pos_hbm, src_hbm, yb_hbm, gate_ref, h1_ref, g2_ref, nw_ref, o_ref,
                    pos_smem, src_smem, win, hbuf, sem_tab, sem_win, *, n_tiles):
    i = pl.program_id(0)
    slot = lax.rem(i, 2)
    tile = COMBINE_TILE
    n_asg = tile * TOP_K
    chunk = WIN_ROWS * ROW_TILES
    slot_rows = WIN_CHUNKS * chunk

    def table_copies(j, s):
        return (pltpu.make_async_copy(pos_hbm.at[j], pos_smem.at[pl.ds(s * n_asg, n_asg)],
                                      sem_tab.at[s]),
                pltpu.make_async_copy(src_hbm.at[j], src_smem.at[pl.ds(s * n_asg, n_asg)],
                                      sem_tab.at[s]))

    def fetch_window(s):
        def one(c, carry):
            src = pl.multiple_of(src_smem[s * n_asg + c], ROW_TILES)
            dst = pl.multiple_of(s * slot_rows + c * chunk, chunk)
            pltpu.make_async_copy(yb_hbm.at[pl.ds(src, chunk), :], win.at[pl.ds(dst, chunk), :],
                                  sem_win.at[s]).start()
            return carry
        lax.fori_loop(0, WIN_CHUNKS, one, 0, unroll=2)

    @pl.when(i == 0)
    def _():
        for cp in table_copies(0, 0):
            cp.start()
        for cp in table_copies(0, 0):
            cp.wait()
        fetch_window(0)
        if n_tiles > 1:
            for cp in table_copies(1, 1):
                cp.start()

    @pl.when(i + 1 < n_tiles)
    def _():
        for cp in table_copies(i + 1, 1 - slot):
            cp.wait()
        fetch_window(1 - slot)

    pltpu.make_async_copy(yb_hbm.at[pl.ds(0, slot_rows), :],
                          win.at[pl.ds(pl.multiple_of(slot * slot_rows, slot_rows), slot_rows), :],
                          sem_win.at[slot]).wait()

    g2 = g2_ref[...]
    base = slot * n_asg

    def token(t, carry):
        e0 = base + t * TOP_K
        acc = None
        for k in range(TOP_K):
            p = pl.multiple_of(pos_smem[e0 + k], ROW_TILES)
            term = gate_ref[pl.ds(t * TOP_K + k, 1), :] * win[pl.ds(p, ROW_TILES), :]
            acc = term if acc is None else acc + term
        r0 = pl.multiple_of(t * ROW_TILES, ROW_TILES)
        hbuf[pl.ds(r0, ROW_TILES), :] = h1_ref[pl.ds(r0, ROW_TILES), :] + g2 * acc
        return carry
    lax.fori_loop(0, tile, token, 0, unroll=4)

    @pl.when(i + 2 < n_tiles)
    def _():
        for cp in table_copies(i + 2, slot):
            cp.start()

    h = _load_tile_rows(hbuf, tile)
    ms = jnp.mean(h * h, axis=-1, keepdims=True)
    o_ref[...] = h * lax.rsqrt(ms + EPS) * nw_ref[...]


def _combine(pos_tiles, src_tiles, yb, gate_rows, h1, g2_tiles, nw, *, tiles_per_batch):
    tile = COMBINE_TILE
    T = h1.shape[0] // ROW_TILES
    n_asg = tile * TOP_K
    return pl.pallas_call(
        functools.partial(_combine_kernel, n_tiles=T // tile),
        out_shape=jax.ShapeDtypeStruct((T, D_MODEL), F32),
        grid=(T // tile,),
        in_specs=[pl.BlockSpec(memory_space=pl.ANY),
                  pl.BlockSpec(memory_space=pl.ANY),
                  pl.BlockSpec(memory_space=pl.ANY),
                  pl.BlockSpec((tile * TOP_K, LANES), lambda i: (i, 0)),
                  pl.BlockSpec((tile * ROW_TILES, LANES), lambda i: (i, 0)),
                  pl.BlockSpec((None, ROW_TILES, LANES), lambda i: (i // tiles_per_batch, 0, 0)),
                  pl.BlockSpec((1, D_MODEL), lambda i: (0, 0))],
        out_specs=pl.BlockSpec((tile, D_MODEL), lambda i: (i, 0)),
        scratch_shapes=[pltpu.SMEM((2 * n_asg,), I32),
                        pltpu.SMEM((2 * n_asg,), I32),
                        pltpu.VMEM((2 * WIN_CHUNKS * WIN_ROWS * ROW_TILES, LANES), F32),
                        pltpu.VMEM((tile * ROW_TILES, LANES), F32),
                        pltpu.SemaphoreType.DMA((2,)),
                        pltpu.SemaphoreType.DMA((2,))],
        compiler_params=_cparams(("arbitrary",)),
        name="combine_norm",
    )(pos_tiles, src_tiles, yb, gate_rows, h1, g2_tiles, nw)


def _rope_tables(n):
    rows = n // GRID_W
    row = jnp.repeat(jnp.arange(rows, dtype=F32), GRID_W)
    col = jnp.tile(jnp.arange(GRID_W, dtype=F32), rows)
    n_freq = RET_DK // 4
    inv = ROPE_BASE ** (-jnp.arange(n_freq, dtype=F32) / n_freq)
    ang = jnp.concatenate([row[:, None] * inv, col[:, None] * inv], axis=-1)
    cos, sin = jnp.cos(ang), jnp.sin(ang)
    cos_h = jnp.concatenate([cos, cos], axis=-1)
    sin_h = jnp.concatenate([-sin, sin], axis=-1)
    return jnp.tile(cos_h, (1, RET_HEADS)), jnp.tile(sin_h, (1, RET_HEADS))


def kernel(x, c, ctx, c_ctx, w_ada, b_ada, norm_mix_w, norm_ffn_w, w_in, w_out, hg_lb,
           hg_norm_w, router_w, router_b, w1, b1, w2, b2, norm_final_w):
    B, N, D = x.shape
    C = ctx.shape[1]
    T = B * N
    assert D == D_MODEL and w_ada.shape[0] == 1

    cc = jnp.concatenate([c.astype(F32), c_ctx.astype(F32)[None, :],
                          jnp.zeros((16 - B - 1, D), F32)], axis=0)
    mod = _ada(cc, w_ada[0], b_ada[0][None, :])
    mod = mod.reshape(16, 6, 1, D).transpose(1, 0, 2, 3)
    sh1, sc1, g1, sh2, sc2, g2 = (mod[i] for i in range(6))

    w_in_bf = w_in[0].astype(BF16)
    w_out_bf = w_out[0].astype(BF16)
    nw_mix = norm_mix_w[0][None, :]
    cos_t, sin_t = _rope_tables(N)

    pb, lf = _inproj(x, sc1, sh1, None, nw_mix, w_in_bf, hg_lb[:2], cos_t, sin_t,
                     rope=True, tm=1024)
    pbc, lfc = _inproj(ctx, sc1, sh1, B, nw_mix, w_in_bf, hg_lb[:2], cos_t[:C], sin_t[:C],
                       rope=False, tm=C)

    a_ret = _retention(pb, pbc)
    a_hg = _hgrn(pb, lf, pbc, lfc, hg_norm_w[0][None, :])

    rw = jnp.zeros((D, LANES), F32).at[:, :N_EXPERTS].set(router_w[0])
    rb = jnp.full((1, LANES), -1e30, F32).at[0, :N_EXPERTS].set(router_b[0])
    h1, v, route, route_t, gate_rows, base_cnt, cnt = _outproj(a_ret, a_hg, w_out_bf, x, g1, sc2, sh2,
                                          norm_ffn_w[0][None, :], rw, rb, tm=512)

    del route
    idx = route_t[0:TOP_K].astype(I32)
    rank = route_t[2 * TOP_K:3 * TOP_K].astype(I32)
    counts = cnt[0, :N_EXPERTS].astype(I32)
    padded = (counts + ROW_BLOCK - 1) // ROW_BLOCK * ROW_BLOCK
    pend = jnp.cumsum(padded)
    pstart = pend - padded
    cstart = jnp.cumsum(counts) - counts
    onehot = idx[None] == jnp.arange(N_EXPERTS, dtype=I32)[:, None, None]

    def dest_tiles(first_row, tile):
        dest = (jnp.sum(jnp.where(onehot, first_row[:, None, None], 0), axis=0) + rank) * ROW_TILES
        return dest.reshape(TOP_K, T // tile, tile).transpose(1, 0, 2).reshape(T // tile,
                                                                               TOP_K * tile)
    n_blocks = (T * TOP_K) // ROW_BLOCK + N_EXPERTS
    starts = jnp.arange(n_blocks, dtype=I32) * ROW_BLOCK
    block_e = jnp.minimum(jnp.sum((pend[None, :] <= starts[:, None]).astype(I32), axis=1),
                          N_EXPERTS - 1)
    valid = jnp.clip(counts[block_e] - (starts - pstart[block_e]), 0, ROW_BLOCK)
    nsub = (valid + SUB_ROWS - 1) // SUB_ROWS
    first_from = lax.cummin(jnp.where(counts > 0, jnp.arange(N_EXPERTS, dtype=I32), N_EXPERTS),
                            reverse=True)
    next_e = jnp.concatenate([first_from[1:], jnp.full((1,), N_EXPERTS, I32)])
    next_e = jnp.where(next_e == N_EXPERTS, -1, next_e)
    xrow = jnp.where(nsub > 0, cstart[block_e] + starts - pstart[block_e], 0) * ROW_TILES

    tile_d = 512
    xs = _dispatch(dest_tiles(cstart, tile_d), v, tile=tile_d)
    yb = _moe(block_e, nsub, next_e, xrow.astype(I32), xs, w1[0], b1[0][:, None, :], w2[0],
              b2[0][:, None, :], n_blocks=n_blocks)

    tc = COMBINE_TILE
    n_ct = T // tc
    pieces = 512 // tc
    base = base_cnt.reshape(T // 512, 8, LANES)[:, :pieces, :N_EXPERTS].reshape(n_ct, N_EXPERTS)
    base = base.astype(I32)
    n_run = jnp.concatenate([base[1:], counts[None, :]], axis=0) - base
    n_chunk = (n_run + WIN_ROWS - 1) // WIN_ROWS
    chunk_end = jnp.cumsum(n_chunk, axis=1)
    chunk_start = chunk_end - n_chunk
    win_off = chunk_start * WIN_ROWS
    c_id = jnp.arange(WIN_CHUNKS, dtype=I32)
    c_exp = jnp.sum((chunk_end[:, None, :] <= c_id[None, :, None]).astype(I32), axis=2)
    c_real = c_exp < N_EXPERTS
    c_own = c_exp[:, :, None] == jnp.arange(N_EXPERTS, dtype=I32)
    take = lambda tab: jnp.sum(jnp.where(c_own, tab[:, None, :], 0), axis=2)
    c_src = (take(pstart[None, :] + base) + (c_id[None, :] - take(chunk_start)) * WIN_ROWS)
    src_tiles = jnp.concatenate(
        [jnp.where(c_real, c_src, 0) * ROW_TILES,
         jnp.zeros((n_ct, tc * TOP_K - WIN_CHUNKS), I32)], axis=1)
    slot_off = (jnp.arange(n_ct, dtype=I32) % 2) * (WIN_CHUNKS * WIN_ROWS)
    shift = jnp.repeat(win_off - base + slot_off[:, None], tc, axis=0).T
    pos = (jnp.sum(jnp.where(onehot, shift[:, None, :], 0), axis=0) + rank) * ROW_TILES

    def per_tile(a):
        return a.reshape(TOP_K, n_ct, tc).transpose(1, 2, 0).reshape(n_ct, tc * TOP_K)

    out = _combine(per_tile(pos), src_tiles, yb, gate_rows, h1,
                   g2.reshape(16, ROW_TILES, LANES), norm_final_w[None, :],
                   tiles_per_batch=N // tc)
    return out.reshape(B, N, D)
```

```python
import functools
import math

import jax
import jax.numpy as jnp
import numpy as np
from jax import lax
from jax.experimental import pallas as pl
from jax.experimental.pallas import tpu as pltpu

F32 = jnp.float32
BF16 = jnp.bfloat16
I32 = jnp.int32

D_MODEL = 1024
GRID_W = 64
RET_HEADS = 4
RET_DK = 64
HG_HEADS = 4
PROJ_W = 4096
ROPE_BASE = 10000.0
EPS = 1e-6
N_EXPERTS = 32
TOP_K = 4
D_FF = 1024
SWIGLU_LIMIT = 7.0
SWIGLU_ALPHA = 1.702
LOG2_E = 1.0 / math.log(2.0)

LANES = 128
CHUNK = 128
ROW_BLOCK = 1024
SUB_ROWS = 256
COMBINE_TILE = 512
WIN_ROWS = 16
WIN_CHUNKS = (COMBINE_TILE * TOP_K + N_EXPERTS * (WIN_ROWS - 1) + WIN_ROWS - 1) // WIN_ROWS
VMEM_LIMIT = 56 * 1024 * 1024
ROW_TILES = D_MODEL // LANES

C_RQ, C_RK, C_RV, C_RG, C_HQ, C_FF, C_FB, C_HV, C_HG = (
    0, 256, 512, 1024, 1536, 2048, 2560, 3072, 3584)


def _cparams(sem):
    return pltpu.CompilerParams(dimension_semantics=sem, vmem_limit_bytes=VMEM_LIMIT)


def _split_bf16(x):
    hi = x.astype(BF16)
    lo = (x - hi.astype(F32)).astype(BF16)
    return hi, lo


def _dot(a, b):
    return jnp.dot(a, b, preferred_element_type=F32)


def _dot_nt(a, b):
    return lax.dot_general(a, b, (((1,), (1,)), ((), ())), preferred_element_type=F32)


def _dot3(a, b):
    ah, al = _split_bf16(a)
    bh, bl = _split_bf16(b)
    return _dot(ah, bh) + (_dot(ah, bl) + _dot(al, bh))


def _silu(x):
    return x * jax.nn.sigmoid(x)


def _load_tile_rows(ref, n):
    return jnp.concatenate([ref[pl.ds(s, n, stride=ROW_TILES), :] for s in range(ROW_TILES)],
                           axis=1)


def _store_tile_rows(ref, x):
    n = x.shape[0]
    for s in range(ROW_TILES):
        ref[pl.ds(s, n, stride=ROW_TILES), :] = x[:, s * LANES:(s + 1) * LANES]


def _ada_kernel(c_ref, w_ref, b_ref, o_ref):
    s = _silu(c_ref[...])
    o_ref[...] = _dot3(s, w_ref[...]) + b_ref[...]


def _ada(cc, w, b):
    nblk = w.shape[1] // D_MODEL
    return pl.pallas_call(
        _ada_kernel,
        out_shape=jax.ShapeDtypeStruct((cc.shape[0], w.shape[1]), F32),
        grid=(nblk,),
        in_specs=[pl.BlockSpec(cc.shape, lambda j: (0, 0)),
                  pl.BlockSpec((D_MODEL, D_MODEL), lambda j: (0, j)),
                  pl.BlockSpec((1, D_MODEL), lambda j: (0, j))],
        out_specs=pl.BlockSpec((cc.shape[0], D_MODEL), lambda j: (0, j)),
        compiler_params=_cparams(("arbitrary",)),
        name="ada",
    )(cc, w, b)


def _inproj_kernel(x_ref, sc_ref, sh_ref, nw_ref, w_ref, lb_ref, cos_ref, sin_ref,
                   pb_ref, lf_ref, u_scr, *, rope):
    x = x_ref[...]
    ms = jnp.mean(x * x, axis=-1, keepdims=True)
    u = x * lax.rsqrt(ms + EPS) * nw_ref[...] * (1.0 + sc_ref[...]) + sh_ref[...]
    u_scr[...] = u.astype(BF16)

    def proj(lo, width):
        return _dot(u_scr[...], w_ref[:, lo:lo + width])

    tm = x.shape[0]
    if rope:
        lane = lax.broadcasted_iota(I32, (tm, LANES), 1)
        first = (lane & 32) == 0

    def put_rot(col, scale):
        for j in range(2):
            lo = col + j * LANES
            t = proj(lo, LANES)
            if scale != 1.0:
                t = t * scale
            if rope:
                tb = j * LANES
                rot = jnp.where(first, pltpu.roll(t, 96, axis=1), pltpu.roll(t, 32, axis=1))
                t = t * cos_ref[:, tb:tb + LANES] + rot * sin_ref[:, tb:tb + LANES]
            pb_ref[:, lo:lo + LANES] = t.astype(BF16)

    put_rot(C_RQ, 1.0)
    put_rot(C_RK, RET_DK ** -0.5)
    pb_ref[:, C_RV:C_RV + 512] = proj(C_RV, 512).astype(BF16)
    pb_ref[:, C_RG:C_RG + 512] = _silu(proj(C_RG, 512)).astype(BF16)
    pb_ref[:, C_HQ:C_HQ + 512] = _silu(proj(C_HQ, 512)).astype(BF16)
    pb_ref[:, C_HV:C_HV + 512] = proj(C_HV, 512).astype(BF16)
    pb_ref[:, C_HG:C_HG + 512] = _silu(proj(C_HG, 512)).astype(BF16)

    la = lb_ref[0]
    lbb = lb_ref[1]
    mx = jnp.maximum(la, lbb)
    ea = jnp.exp(la - mx)
    eb = jnp.exp(lbb - mx)
    lb = ea / (ea + eb)
    for d, col in enumerate((C_FF, C_FB)):
        lbd = lb[d:d + 1, :]
        f = lbd + (1.0 - lbd) * jax.nn.sigmoid(proj(col, 512))
        pb_ref[:, col:col + 512] = (1.0 - f).astype(BF16)
        lf_ref[:, d * 512:(d + 1) * 512] = jnp.log(f) * LOG2_E


def _inproj(x, sc, sh, mod_row, nw, w_bf, hg_lb, cos_t, sin_t, *, rope, tm):
    B, n, _ = x.shape
    nt = n // tm
    if mod_row is None:
        mrow = lambda b, j: (b, 0, 0)
    else:
        mrow = lambda b, j: (mod_row, 0, 0)
    return pl.pallas_call(
        functools.partial(_inproj_kernel, rope=rope),
        out_shape=(jax.ShapeDtypeStruct((B, n, PROJ_W), BF16),
                   jax.ShapeDtypeStruct((B, n, 1024), F32)),
        grid=(B, nt),
        in_specs=[pl.BlockSpec((None, tm, D_MODEL), lambda b, j: (b, j, 0)),
                  pl.BlockSpec((None, 1, D_MODEL), mrow),
                  pl.BlockSpec((None, 1, D_MODEL), mrow),
                  pl.BlockSpec((1, D_MODEL), lambda b, j: (0, 0)),
                  pl.BlockSpec((D_MODEL, PROJ_W), lambda b, j: (0, 0),
                               pipeline_mode=pl.Buffered(1)),
                  pl.BlockSpec((2, 2, 512), lambda b, j: (0, 0, 0)),
                  pl.BlockSpec((tm, 256), lambda b, j: (j, 0)),
                  pl.BlockSpec((tm, 256), lambda b, j: (j, 0))],
        out_specs=(pl.BlockSpec((None, tm, PROJ_W), lambda b, j: (b, j, 0)),
                   pl.BlockSpec((None, tm, 1024), lambda b, j: (b, j, 0))),
        scratch_shapes=[pltpu.VMEM((tm, D_MODEL), BF16)],
        compiler_params=_cparams(("arbitrary", "arbitrary")),
        name="inproj_rope" if rope else "inproj_ctx",
    )(x, sc, sh, nw, w_bf, hg_lb, cos_t, sin_t)


_RET_LGF = [math.log1p(-(2.0 ** (-5.0 - 2.0 * h))) for h in range(RET_HEADS)]
_RET_LGB = [math.log1p(-(2.0 ** (-6.0 - 2.0 * h))) for h in range(RET_HEADS)]


def _ret_kernel(q_ref, k_ref, v_ref, g_ref, kc_ref, vc_ref, o_ref,
                u_scr, sin_scr, kt_scr, dtot_scr, *, n_lat, n_ctx):
    L = CHUNK
    pair = pl.program_id(1)
    row = lax.broadcasted_iota(I32, (L, L), 0).astype(F32)
    col = lax.broadcasted_iota(I32, (L, L), 1).astype(F32)
    lane = lax.broadcasted_iota(I32, (L, LANES), 1)
    trow = lax.broadcasted_iota(I32, (L, 1), 0).astype(F32)
    tcol = lax.broadcasted_iota(I32, (1, L), 1).astype(F32)
    low_half = lane < RET_DK

    def u_chunk(k_blk, v_blk, ci, store_kt):
        kt = k_blk.astype(F32).T
        if store_kt is not None:
            kt_scr[store_kt] = kt.astype(BF16)
        for hh in range(2):
            lgf = jnp.where(pair == 0, _RET_LGF[hh], _RET_LGF[2 + hh])
            lgb = jnp.where(pair == 0, _RET_LGB[hh], _RET_LGB[2 + hh])
            kth = kt[hh * RET_DK:(hh + 1) * RET_DK, :]
            wkf = jnp.exp(lgf * (L - 1.0 - tcol))
            wkb = jnp.exp(lgb * tcol)
            lhs = jnp.concatenate([kth * wkf, kth * wkb], axis=0).astype(BF16)
            u_scr[hh, ci] = _dot(lhs, v_blk[:, hh * LANES:(hh + 1) * LANES])

    for c in range(n_ctx):
        u_chunk(kc_ref[c * L:(c + 1) * L, :], vc_ref[c * L:(c + 1) * L, :], c, None)

    def lat_u(c, carry):
        r0 = pl.multiple_of(c * L, L)
        u_chunk(k_ref[pl.ds(r0, L), :], v_ref[pl.ds(r0, L), :], n_ctx + c, c)
        return carry
    lax.fori_loop(0, n_lat, lat_u, 0, unroll=2)

    ones = jnp.ones((RET_DK, LANES), F32)
    for hh in range(2):
        lgf = jnp.where(pair == 0, _RET_LGF[hh], _RET_LGF[2 + hh])
        lgb = jnp.where(pair == 0, _RET_LGB[hh], _RET_LGB[2 + hh])
        d = row - col
        dtot_scr[hh] = jnp.where(d > 0, jnp.exp(lgf * jnp.maximum(d, 0.0)),
                                 jnp.where(d < 0, jnp.exp(lgb * jnp.maximum(-d, 0.0)), 2.0))
        af = jnp.exp(ones * (lgf * L))
        ab = jnp.exp(ones * (lgb * L))

        s = jnp.zeros((RET_DK, LANES), F32)
        for c in range(n_ctx):
            s = af * s + u_scr[hh, c, 0:RET_DK, :]
        sb = jnp.zeros((RET_DK, LANES), F32)
        for c in reversed(range(n_ctx)):
            sb = ab * sb + u_scr[hh, c, RET_DK:2 * RET_DK, :]

        def fwd(c, s, hh=hh, af=af):
            sin_scr[hh, c, 0:RET_DK, :] = s.astype(BF16)
            return af * s + u_scr[hh, n_ctx + c, 0:RET_DK, :]
        lax.fori_loop(0, n_lat, fwd, s)

        def bwd(i, sb, hh=hh, ab=ab):
            c = n_lat - 1 - i
            sin_scr[hh, c, RET_DK:2 * RET_DK, :] = sb.astype(BF16)
            return ab * sb + u_scr[hh, n_ctx + c, RET_DK:2 * RET_DK, :]
        lax.fori_loop(0, n_lat, bwd, sb)

    def out_chunk(c, carry):
        r0 = pl.multiple_of(c * L, L)
        q = q_ref[pl.ds(r0, L), :].astype(F32)
        qr = pltpu.roll(q, RET_DK, axis=1)
        kt = kt_scr[c]
        for hh in range(2):
            lgf = jnp.where(pair == 0, _RET_LGF[hh], _RET_LGF[2 + hh])
            lgb = jnp.where(pair == 0, _RET_LGB[hh], _RET_LGB[2 + hh])
            mine = low_half if hh == 0 else jnp.logical_not(low_half)
            qm = jnp.where(mine, q, 0.0).astype(BF16)
            p = (_dot(qm, kt) * dtot_scr[hh]).astype(BF16)
            vh = v_ref[pl.ds(r0, L), hh * LANES:(hh + 1) * LANES]
            wqf = jnp.exp(lgf * (trow + 1.0))
            wqb = jnp.exp(lgb * (L - trow))
            qa, qb = (q, qr) if hh == 0 else (qr, q)
            qs = jnp.where(low_half, qa * wqf, qb * wqb).astype(BF16)
            o = _dot(p, vh) + _dot(qs, sin_scr[hh, c])
            ms = jnp.mean(o * o, axis=-1, keepdims=True)
            gh = g_ref[pl.ds(r0, L), hh * LANES:(hh + 1) * LANES].astype(F32)
            o_ref[pl.ds(r0, L), hh * LANES:(hh + 1) * LANES] = (
                o * lax.rsqrt(ms + EPS) * gh).astype(BF16)
        return carry
    lax.fori_loop(0, n_lat, out_chunk, 0, unroll=2)


def _retention(pb, pbc):
    B, n, _ = pb.shape
    nc = pbc.shape[1]
    n_lat, n_ctx = n // CHUNK, nc // CHUNK
    return pl.pallas_call(
        functools.partial(_ret_kernel, n_lat=n_lat, n_ctx=n_ctx),
        out_shape=jax.ShapeDtypeStruct((B, n, 512), BF16),
        grid=(B, 2),
        in_specs=[pl.BlockSpec((None, n, LANES), lambda b, p: (b, 0, C_RQ // LANES + p)),
                  pl.BlockSpec((None, n, LANES), lambda b, p: (b, 0, C_RK // LANES + p)),
                  pl.BlockSpec((None, n, 256), lambda b, p: (b, 0, C_RV // 256 + p)),
                  pl.BlockSpec((None, n, 256), lambda b, p: (b, 0, C_RG // 256 + p)),
                  pl.BlockSpec((None, nc, LANES), lambda b, p: (b, 0, C_RK // LANES + p)),
                  pl.BlockSpec((None, nc, 256), lambda b, p: (b, 0, C_RV // 256 + p))],
        out_specs=pl.BlockSpec((None, n, 256), lambda b, p: (b, 0, p)),
        scratch_shapes=[pltpu.VMEM((2, n_lat + n_ctx, CHUNK, LANES), F32),
                        pltpu.VMEM((2, n_lat, CHUNK, LANES), BF16),
                        pltpu.VMEM((n_lat, LANES, CHUNK), BF16),
                        pltpu.VMEM((2, CHUNK, CHUNK), F32)],
        compiler_params=_cparams(("arbitrary", "arbitrary")),
        name="retention",
    )(pb, pb, pb, pb, pbc, pbc)


_LEVELS = (64, 32, 16, 8, 4, 2, 1)


def _expand_rows(r, rep):
    n = r.shape[0]
    if n == 1:
        return jnp.broadcast_to(r, (rep, r.shape[1]))
    return jnp.concatenate(
        [jnp.broadcast_to(r[i:i + 1, :], (rep, r.shape[1])) for i in range(n)], axis=0)


def _hgrn_kernel(q_ref, kf_ref, kb_ref, v_ref, g_ref, lff_ref, lfb_ref,
                 kfc_ref, kbc_ref, vc_ref, lffc_ref, lfbc_ref, nw_ref, o_ref,
                 ut_scr, a_scr, qs_scr, oi_scr, sin_scr, bfb_scr, *, n_lat, n_ctx):
    L = CHUNK
    row = lax.broadcasted_iota(I32, (L, L), 0)
    col = lax.broadcasted_iota(I32, (L, L), 1)
    xr_bits = lax.bitcast_convert_type((row ^ col).astype(F32), I32)
    lv = lax.shift_right_logical(xr_bits, 23) - 127
    row2 = lax.broadcasted_iota(I32, (L, 2 * L), 0)
    col2 = lax.broadcasted_iota(I32, (L, 2 * L), 1) & (L - 1)
    tril2 = jnp.where(col2 <= row2, 1.0, 0.0).astype(BF16)
    triu2 = jnp.where(col2 >= row2, 1.0, 0.0).astype(BF16)

    def cums(lff, lfb):
        hf, lof = _split_bf16(lff)
        hb, lob = _split_bf16(lfb)
        bf = _dot(tril2, jnp.concatenate([hf, lof], axis=0))
        bb = _dot(triu2, jnp.concatenate([hb, lob], axis=0))
        return bf, bb

    def state_part(ci, kf, kb, v_blk, bf, bb):
        endf = bf[L - 1:L, :]
        endb = bb[0:1, :]
        ksf = kf * jnp.exp2(endf - bf)
        ksb = kb * jnp.exp2(endb - bb)
        vt = v_blk.astype(F32).T.astype(BF16)
        ut_scr[ci] = _dot(vt, jnp.concatenate([ksf, ksb], axis=1).astype(BF16))
        a_scr[ci] = jnp.broadcast_to(
            jnp.concatenate([jnp.exp2(endf), jnp.exp2(endb)], axis=1), (8, 2 * LANES))

    for c in range(n_ctx):
        sl = slice(c * L, (c + 1) * L)
        bf, bb = cums(lffc_ref[sl, :], lfbc_ref[sl, :])
        state_part(c, kfc_ref[sl, :].astype(F32), kbc_ref[sl, :].astype(F32),
                   vc_ref[sl, :], bf, bb)

    def lat_chunk(c, u):
        rows = pl.ds(pl.multiple_of(c * L, L), L)
        bf_scr = bfb_scr.at[u, 0]
        bb_scr = bfb_scr.at[u, 1]
        q = q_ref[rows, :].astype(F32)
        kf = kf_ref[rows, :].astype(F32)
        kb = kb_ref[rows, :].astype(F32)
        v_blk = v_ref[rows, :]
        lff = lff_ref[rows, :]
        lfb = lfb_ref[rows, :]
        bf, bb = cums(lff, lfb)
        state_part(n_ctx + c, kf, kb, v_blk, bf, bb)
        qs_scr[rows, :] = jnp.concatenate([q * jnp.exp2(bf), q * jnp.exp2(bb)],
                                          axis=1).astype(BF16)
        bf_scr[...] = bf
        bb_scr[...] = bb

        acc = jnp.zeros((L, L), F32)
        for lvl, h in enumerate(_LEVELS):
            bit = (row & h) != 0
            ksel = jnp.where(bit, kb, kf)
            if h >= 4:
                n = (L // 2) // h
                if n == 1:
                    rf = bf_scr[h - 1:h, :]
                    rb = bb_scr[h:h + 1, :]
                else:
                    rf = bf_scr[pl.ds(h - 1, n, stride=2 * h), :]
                    rb = bb_scr[pl.ds(h, n, stride=2 * h), :]
                df = bf - _expand_rows(rf, 2 * h)
                db = bb - _expand_rows(rb, 2 * h)
                eq = jnp.where(bit, df, db)
                ek = -jnp.where(bit, db, df)
            elif h == 2:
                m = row & 3
                lff_n = pltpu.roll(lff, L - 1, axis=0)
                lfb_n = pltpu.roll(lfb, L - 1, axis=0)
                eq = jnp.where(m == 2, lff,
                               jnp.where(m == 3, lff + pltpu.roll(lff, 1, axis=0),
                                         jnp.where(m == 0, lfb + lfb_n, lfb)))
                ek = jnp.where(m == 3, pltpu.roll(lfb, 1, axis=0),
                               jnp.where(m == 0, lff_n, 0.0))
            else:
                eq = jnp.where(bit, lff, lfb)
                ek = None
            lhs = (q * jnp.exp2(eq)).astype(BF16)
            rhs = (ksel if ek is None else ksel * jnp.exp2(ek)).astype(BF16)
            acc = jnp.where(lv == 6 - lvl, _dot_nt(lhs, rhs), acc)

        dsum = jnp.sum(q * (kf + kb), axis=-1, keepdims=True)
        oi_scr[rows, :] = _dot(acc.astype(BF16), v_blk) + dsum * v_blk.astype(F32)

    def lat_pair(i, carry):
        for u in range(4):
            lat_chunk(4 * i + u, u)
        return carry
    lax.fori_loop(0, n_lat // 4, lat_pair, 0)

    st = jnp.zeros((LANES, LANES), F32)
    for c in range(n_ctx):
        st = st * a_scr[c, 0:1, 0:LANES] + ut_scr[c, :, 0:LANES]
    stb = jnp.zeros((LANES, LANES), F32)
    for c in reversed(range(n_ctx)):
        stb = stb * a_scr[c, 0:1, LANES:2 * LANES] + ut_scr[c, :, LANES:2 * LANES]

    def fwd(c, st):
        sin_scr[c, :, 0:LANES] = st.astype(BF16)
        ci = n_ctx + c
        return st * a_scr[ci, 0:1, 0:LANES] + ut_scr[ci, :, 0:LANES]
    lax.fori_loop(0, n_lat, fwd, st)

    def bwd(i, stb):
        c = n_lat - 1 - i
        sin_scr[c, :, LANES:2 * LANES] = stb.astype(BF16)
        ci = n_ctx + c
        return stb * a_scr[ci, 0:1, LANES:2 * LANES] + ut_scr[ci, :, LANES:2 * LANES]
    lax.fori_loop(0, n_lat, bwd, stb)

    def out_chunk(c, carry):
        rows = pl.ds(pl.multiple_of(c * L, L), L)
        o = oi_scr[rows, :] + _dot_nt(qs_scr[rows, :], sin_scr[c])
        ms = jnp.mean(o * o, axis=-1, keepdims=True)
        y = o * lax.rsqrt(ms + EPS) * nw_ref[...] * g_ref[rows, :].astype(F32)
        o_ref[rows, :] = y.astype(BF16)
        return carry
    lax.fori_loop(0, n_lat, out_chunk, 0, unroll=4)


def _hgrn(pb, lf, pbc, lfc, nw):
    B, n, _ = pb.shape
    nc = pbc.shape[1]
    n_lat, n_ctx = n // CHUNK, nc // CHUNK

    def colblk(rows, col0):
        return pl.BlockSpec((None, rows, LANES), lambda b, h: (b, 0, col0 // LANES + h))

    return pl.pallas_call(
        functools.partial(_hgrn_kernel, n_lat=n_lat, n_ctx=n_ctx),
        out_shape=jax.ShapeDtypeStruct((B, n, 512), BF16),
        grid=(B, HG_HEADS),
        in_specs=[colblk(n, C_HQ), colblk(n, C_FF), colblk(n, C_FB), colblk(n, C_HV),
                  colblk(n, C_HG), colblk(n, 0), colblk(n, 512),
                  colblk(nc, C_FF), colblk(nc, C_FB), colblk(nc, C_HV),
                  colblk(nc, 0), colblk(nc, 512),
                  pl.BlockSpec((1, LANES), lambda b, h: (0, 0))],
        out_specs=pl.BlockSpec((None, n, LANES), lambda b, h: (b, 0, h)),
        scratch_shapes=[pltpu.VMEM((n_lat + n_ctx, LANES, 2 * LANES), F32),
                        pltpu.VMEM((n_lat + n_ctx, 8, 2 * LANES), F32),
                        pltpu.VMEM((n, 2 * LANES), BF16),
                        pltpu.VMEM((n, LANES), F32),
                        pltpu.VMEM((n_lat, LANES, 2 * LANES), BF16),
                        pltpu.VMEM((4, 2, CHUNK, LANES), F32)],
        compiler_params=_cparams(("arbitrary", "arbitrary")),
        name="hgrn2",
    )(pb, pb, pb, pb, pb, lf, lf, pbc, pbc, pbc, lfc, lfc, nw)


def _outproj_kernel(ar_ref, ah_ref, w_ref, x_ref, g1_ref, sc_ref, sh_ref, nw_ref,
                    rw_ref, rb_ref, tri_ref, h1_ref, v_ref, route_ref, route_t_ref, gate_ref, base_ref, cnt_ref,
                    cnt_scr):
    first_step = jnp.logical_and(pl.program_id(0) == 0, pl.program_id(1) == 0)

    @pl.when(first_step)
    def _():
        cnt_scr[...] = jnp.zeros_like(cnt_scr)

    y = _dot(ar_ref[...], w_ref[0:512, :]) + _dot(ah_ref[...], w_ref[512:1024, :])
    h1 = x_ref[...] + g1_ref[...] * y
    _store_tile_rows(h1_ref, h1)
    ms = jnp.mean(h1 * h1, axis=-1, keepdims=True)
    v = h1 * lax.rsqrt(ms + EPS) * nw_ref[...] * (1.0 + sc_ref[...]) + sh_ref[...]
    _store_tile_rows(v_ref, v)

    tm = v.shape[0]
    logits = _dot3(v, rw_ref[...]) + rb_ref[...]
    lane = lax.broadcasted_iota(I32, (tm, LANES), 1)
    lane_f = lane.astype(F32)
    l = logits
    sels, tops, idxs = [], [], []
    for _ in range(TOP_K):
        m = jnp.max(l, axis=1, keepdims=True)
        i = jnp.min(jnp.where(l == m, lane_f, float(LANES)), axis=1, keepdims=True)
        sel = lane_f == i
        l = jnp.where(sel, -jnp.inf, l)
        sels.append(sel)
        tops.append(m)
        idxs.append(i)
    es = [jnp.exp(t - tops[0]) for t in tops]
    den = es[0] + es[1] + es[2] + es[3]
    gates = [e / den for e in es]
    for k in range(TOP_K):
        gate_ref[pl.ds(k, tm, stride=TOP_K), :] = jnp.broadcast_to(gates[k], (tm, LANES))

    oh = jnp.zeros((tm, LANES), F32)
    for sel in sels:
        oh = jnp.where(sel, 1.0, oh)
    before = _dot(tri_ref[...], oh.astype(BF16)) + cnt_scr[0:1, :]
    ranks = [jnp.sum(jnp.where(sel, before, 0.0), axis=1, keepdims=True) for sel in sels]
    base = cnt_scr[0:1, :]
    pieces = []
    for p in range(tm // COMBINE_TILE):
        pieces.append(base)
        base = base + jnp.sum(oh[p * COMBINE_TILE:(p + 1) * COMBINE_TILE, :], axis=0, keepdims=True)
    pieces.append(jnp.zeros((8 - len(pieces), LANES), F32))
    base_ref[...] = jnp.concatenate(pieces, axis=0)
    cnt_scr[...] = jnp.broadcast_to(base, cnt_scr.shape)
    cnt_ref[...] = cnt_scr[...]

    out = jnp.zeros((tm, LANES), F32)
    for k in range(TOP_K):
        out = jnp.where(lane == k, idxs[k], out)
        out = jnp.where(lane == TOP_K + k, gates[k], out)
        out = jnp.where(lane == 2 * TOP_K + k, ranks[k], out)
    route_ref[...] = out
    route_t_ref[...] = out.T[0:16, :]


def _outproj(a_ret, a_hg, w_bf, x, g1, sc2, sh2, nw, rw, rb, *, tm):
    B, n, _ = x.shape
    nt = n // tm
    T = B * n
    mrow = lambda b, j: (b, 0, 0)
    tok = lambda b, j: (b * nt + j, 0)
    tri = jnp.tril(jnp.ones((tm, tm), BF16), -1)
    return pl.pallas_call(
        _outproj_kernel,
        out_shape=(jax.ShapeDtypeStruct((T * ROW_TILES, LANES), F32),
                   jax.ShapeDtypeStruct((T * ROW_TILES, LANES), F32),
                   jax.ShapeDtypeStruct((T, LANES), F32),
                   jax.ShapeDtypeStruct((16, T), F32),
                   jax.ShapeDtypeStruct((T * TOP_K, LANES), F32),
                   jax.ShapeDtypeStruct((B * nt * 8, LANES), F32),
                   jax.ShapeDtypeStruct((8, LANES), F32)),
        grid=(B, nt),
        in_specs=[pl.BlockSpec((None, tm, 512), lambda b, j: (b, j, 0)),
                  pl.BlockSpec((None, tm, 512), lambda b, j: (b, j, 0)),
                  pl.BlockSpec((D_MODEL, D_MODEL), lambda b, j: (0, 0)),
                  pl.BlockSpec((None, tm, D_MODEL), lambda b, j: (b, j, 0)),
                  pl.BlockSpec((None, 1, D_MODEL), mrow),
                  pl.BlockSpec((None, 1, D_MODEL), mrow),
                  pl.BlockSpec((None, 1, D_MODEL), mrow),
                  pl.BlockSpec((1, D_MODEL), lambda b, j: (0, 0)),
                  pl.BlockSpec((D_MODEL, LANES), lambda b, j: (0, 0)),
                  pl.BlockSpec((1, LANES), lambda b, j: (0, 0)),
                  pl.BlockSpec((tm, tm), lambda b, j: (0, 0))],
        out_specs=(pl.BlockSpec((tm * ROW_TILES, LANES), tok),
                   pl.BlockSpec((tm * ROW_TILES, LANES), tok),
                   pl.BlockSpec((tm, LANES), tok),
                   pl.BlockSpec((16, tm), lambda b, j: (0, b * nt + j)),
                   pl.BlockSpec((tm * TOP_K, LANES), tok),
                   pl.BlockSpec((8, LANES), tok),
                   pl.BlockSpec((8, LANES), lambda b, j: (0, 0))),
        scratch_shapes=[pltpu.VMEM((8, LANES), F32)],
        compiler_params=_cparams(("arbitrary", "arbitrary")),
        name="outproj_router",
    )(a_ret, a_hg, w_bf, x, g1, sc2, sh2, nw, rw, rb, tri)


def _dispatch_kernel(dest_hbm, v_ref, xs_hbm, idx_smem, zbuf, sem_idx, sem_z, sem_rows,
                     *, tile, n_tiles, n_rows):
    i = pl.program_id(0)
    slot = lax.rem(i, 2)

    def idx_copy(j, s):
        n_asg = tile * TOP_K
        return pltpu.make_async_copy(dest_hbm.at[j], idx_smem.at[pl.ds(s * n_asg, n_asg)],
                                     sem_idx.at[s])

    def zero_copy(j):
        r0 = (n_rows + j * SUB_ROWS) * ROW_TILES
        return pltpu.make_async_copy(zbuf, xs_hbm.at[pl.ds(r0, SUB_ROWS * ROW_TILES), :], sem_z)

    @pl.when(i == 0)
    def _():
        idx_copy(0, 0).start()
        zbuf[...] = jnp.zeros_like(zbuf)
        for j in range(ROW_BLOCK // SUB_ROWS):
            zero_copy(j).start()
        for j in range(ROW_BLOCK // SUB_ROWS):
            zero_copy(j).wait()

    idx_copy(i, slot).wait()

    @pl.when(i + 1 < n_tiles)
    def _():
        idx_copy(i + 1, 1 - slot).start()

    def issue(t, carry):
        for k in range(TOP_K):
            d = pl.multiple_of(idx_smem[slot * (tile * TOP_K) + k * tile + t], ROW_TILES)
            src = v_ref.at[pl.ds(pl.multiple_of(t * ROW_TILES, ROW_TILES), ROW_TILES), :]
            pltpu.make_async_copy(src, xs_hbm.at[pl.ds(d, ROW_TILES), :],
                                  sem_rows).start(priority=k % 2)
        return carry
    lax.fori_loop(0, tile, issue, 0, unroll=4)
    for _ in range(TOP_K):
        pltpu.make_async_copy(v_ref, xs_hbm.at[pl.ds(0, tile * ROW_TILES), :], sem_rows).wait()


def _dispatch(dest_tiles, v, *, tile):
    T = v.shape[0] // ROW_TILES
    n_rows = T * TOP_K
    return pl.pallas_call(
        functools.partial(_dispatch_kernel, tile=tile, n_tiles=T // tile, n_rows=n_rows),
        out_shape=jax.ShapeDtypeStruct(((n_rows + ROW_BLOCK) * ROW_TILES, LANES), F32),
        grid=(T // tile,),
        in_specs=[pl.BlockSpec(memory_space=pl.ANY),
                  pl.BlockSpec((tile * ROW_TILES, LANES), lambda i: (i, 0))],
        out_specs=pl.BlockSpec(memory_space=pl.ANY),
        scratch_shapes=[pltpu.SMEM((2 * tile * TOP_K,), I32),
                        pltpu.VMEM((SUB_ROWS * ROW_TILES, LANES), F32),
                        pltpu.SemaphoreType.DMA((2,)),
                        pltpu.SemaphoreType.DMA,
                        pltpu.SemaphoreType.DMA],
        compiler_params=_cparams(("arbitrary",)),
        name="dispatch",
    )(dest_tiles, v)


def _moe_kernel(be_ref, ns_ref, nx_ref, xr_ref, x_hbm, w1_hbm, b1_ref, w2_hbm, b2_ref, y_ref,
                w1_stage, w2_stage, w1_scr, w2_scr, xbuf, x_scr, act_scr, sem_w, sem_x,
                *, n_blocks):
    i = pl.program_id(0)
    slot = lax.rem(i, 2)
    e = be_ref[i]
    nsub = ns_ref[i]
    changed = jnp.logical_or(i == 0, e != be_ref[jnp.maximum(i - 1, 0)])

    def x_copy(j, s):
        r0 = pl.multiple_of(xr_ref[j], ROW_TILES)
        return pltpu.make_async_copy(x_hbm.at[pl.ds(r0, ROW_BLOCK * ROW_TILES), :], xbuf.at[s],
                                     sem_x.at[s])

    @pl.when(i == 0)
    def _():
        x_copy(0, 0).start()

    @pl.when(nsub > 0)
    def _():
        x_copy(i, slot).wait()

    nxt_blk = jnp.minimum(i + 1, n_blocks - 1)

    @pl.when(jnp.logical_and(i + 1 < n_blocks, ns_ref[nxt_blk] > 0))
    def _():
        x_copy(nxt_blk, 1 - slot).start()

    def weight_copies(ex):
        return (pltpu.make_async_copy(w1_hbm.at[ex], w1_stage, sem_w.at[0]),
                pltpu.make_async_copy(w2_hbm.at[ex], w2_stage, sem_w.at[1]))

    @pl.when(i == 0)
    def _():
        for cp in weight_copies(e):
            cp.start()

    @pl.when(jnp.logical_and(changed, nsub > 0))
    def _():
        for cp in weight_copies(e):
            cp.wait()
        w1_scr[...] = w1_stage[...].astype(BF16)
        w2_scr[...] = w2_stage[...].astype(BF16)
        nxt = nx_ref[e]

        @pl.when(nxt >= 0)
        def _():
            for cp in weight_copies(nxt):
                cp.start()

    def compute(rows):
        x_scr[0:rows, :] = _load_tile_rows(xbuf.at[slot], rows).astype(BF16)
        cw = 256
        for c in range(D_FF // cw):
            glu = (_dot(x_scr[0:rows, :], w1_scr[:, c * cw:(c + 1) * cw])
                   + b1_ref[:, c * cw:(c + 1) * cw])
            lin = (_dot(x_scr[0:rows, :], w1_scr[:, D_FF + c * cw:D_FF + (c + 1) * cw])
                   + b1_ref[:, D_FF + c * cw:D_FF + (c + 1) * cw])
            glu = jnp.minimum(glu, SWIGLU_LIMIT)
            lin = jnp.clip(lin, -SWIGLU_LIMIT, SWIGLU_LIMIT)
            act = glu * jax.nn.sigmoid(SWIGLU_ALPHA * glu) * (lin + 1.0)
            act_scr[0:rows, c * cw:(c + 1) * cw] = act.astype(BF16)
        for c in range(D_MODEL // cw):
            y = (_dot(act_scr[0:rows, :], w2_scr[:, c * cw:(c + 1) * cw])
                 + b2_ref[:, c * cw:(c + 1) * cw])
            for s in range(cw // LANES):
                t = c * (cw // LANES) + s
                y_ref[pl.ds(t, rows, stride=ROW_TILES), :] = y[:, s * LANES:(s + 1) * LANES]
        if rows < ROW_BLOCK:
            y_ref[rows * ROW_TILES:, :] = jnp.zeros(((ROW_BLOCK - rows) * ROW_TILES, LANES), F32)

    for m in range(1, ROW_BLOCK // SUB_ROWS + 1):
        @pl.when(nsub == m)
        def _(m=m):
            compute(m * SUB_ROWS)

    @pl.when(nsub == 0)
    def _():
        y_ref[...] = jnp.zeros_like(y_ref)


def _moe(block_e, nsub, next_e, xrow, xs, w1, b1, w2, b2, *, n_blocks):
    return pl.pallas_call(
        functools.partial(_moe_kernel, n_blocks=n_blocks),
        out_shape=jax.ShapeDtypeStruct((n_blocks * ROW_BLOCK * ROW_TILES, LANES), F32),
        grid_spec=pltpu.PrefetchScalarGridSpec(
            num_scalar_prefetch=4,
            grid=(n_blocks,),
            in_specs=[pl.BlockSpec(memory_space=pl.ANY),
                      pl.BlockSpec(memory_space=pl.ANY),
                      pl.BlockSpec((None, 1, 2 * D_FF), lambda i, be, ns, nx, xr: (be[i], 0, 0)),
                      pl.BlockSpec(memory_space=pl.ANY),
                      pl.BlockSpec((None, 1, D_MODEL), lambda i, be, ns, nx, xr: (be[i], 0, 0))],
            out_specs=pl.BlockSpec((ROW_BLOCK * ROW_TILES, LANES),
                                   lambda i, be, ns, nx, xr: (i, 0)),
            scratch_shapes=[pltpu.VMEM((D_MODEL, 2 * D_FF), F32),
                            pltpu.VMEM((D_FF, D_MODEL), F32),
                            pltpu.VMEM((D_MODEL, 2 * D_FF), BF16),
                            pltpu.VMEM((D_FF, D_MODEL), BF16),
                            pltpu.VMEM((2, ROW_BLOCK * ROW_TILES, LANES), F32),
                            pltpu.VMEM((ROW_BLOCK, D_MODEL), BF16),
                            pltpu.VMEM((ROW_BLOCK, D_FF), BF16),
                            pltpu.SemaphoreType.DMA((2,)),
                            pltpu.SemaphoreType.DMA((2,))]),
        compiler_params=_cparams(("arbitrary",)),
        name="moe_ffn",
    )(block_e, nsub, next_e, xrow, xs, w1, b1, w2, b2)


def _combine_kernel(pos_hbm, src_hbm, yb_hbm, gate_ref, h1_ref, g2_ref, nw_ref, o_ref,
                    pos_smem, src_smem, win, hbuf, sem_tab, sem_win, *, n_tiles):
    i = pl.program_id(0)
    slot = lax.rem(i, 2)
    tile = COMBINE_TILE
    n_asg = tile * TOP_K
    chunk = WIN_ROWS * ROW_TILES
    slot_rows = WIN_CHUNKS * chunk

    def table_copies(j, s):
        return (pltpu.make_async_copy(pos_hbm.at[j], pos_smem.at[pl.ds(s * n_asg, n_asg)],
                                      sem_tab.at[s]),
                pltpu.make_async_copy(src_hbm.at[j], src_smem.at[pl.ds(s * n_asg, n_asg)],
                                      sem_tab.at[s]))

    def fetch_window(s):
        def one(c, carry):
            src = pl.multiple_of(src_smem[s * n_asg + c], ROW_TILES)
            dst = pl.multiple_of(s * slot_rows + c * chunk, chunk)
            pltpu.make_async_copy(yb_hbm.at[pl.ds(src, chunk), :], win.at[pl.ds(dst, chunk), :],
                                  sem_win.at[s]).start()
            return carry
        lax.fori_loop(0, WIN_CHUNKS, one, 0, unroll=2)

    @pl.when(i == 0)
    def _():
        for cp in table_copies(0, 0):
            cp.start()
        for cp in table_copies(0, 0):
            cp.wait()
        fetch_window(0)
        if n_tiles > 1:
            for cp in table_copies(1, 1):
                cp.start()

    @pl.when(i + 1 < n_tiles)
    def _():
        for cp in table_copies(i + 1, 1 - slot):
            cp.wait()
        fetch_window(1 - slot)

    pltpu.make_async_copy(yb_hbm.at[pl.ds(0, slot_rows), :],
                          win.at[pl.ds(pl.multiple_of(slot * slot_rows, slot_rows), slot_rows), :],
                          sem_win.at[slot]).wait()

    g2 = g2_ref[...]
    base = slot * n_asg

    def token(t, carry):
        e0 = base + t * TOP_K
        acc = None
        for k in range(TOP_K):
            p = pl.multiple_of(pos_smem[e0 + k], ROW_TILES)
            term = gate_ref[pl.ds(t * TOP_K + k, 1), :] * win[pl.ds(p, ROW_TILES), :]
            acc = term if acc is None else acc + term
        r0 = pl.multiple_of(t * ROW_TILES, ROW_TILES)
        hbuf[pl.ds(r0, ROW_TILES), :] = h1_ref[pl.ds(r0, ROW_TILES), :] + g2 * acc
        return carry
    lax.fori_loop(0, tile, token, 0, unroll=4)

    @pl.when(i + 2 < n_tiles)
    def _():
        for cp in table_copies(i + 2, slot):
            cp.start()

    h = _load_tile_rows(hbuf, tile)
    ms = jnp.mean(h * h, axis=-1, keepdims=True)
    o_ref[...] = h * lax.rsqrt(ms + EPS) * nw_ref[...]


def _combine(pos_tiles, src_tiles, yb, gate_rows, h1, g2_tiles, nw, *, tiles_per_batch):
    tile = COMBINE_TILE
    T = h1.shape[0] // ROW_TILES
    n_asg = tile * TOP_K
    return pl.pallas_call(
        functools.partial(_combine_kernel, n_tiles=T // tile),
        out_shape=jax.ShapeDtypeStruct((T, D_MODEL), F32),
        grid=(T // tile,),
        in_specs=[pl.BlockSpec(memory_space=pl.ANY),
                  pl.BlockSpec(memory_space=pl.ANY),
                  pl.BlockSpec(memory_space=pl.ANY),
                  pl.BlockSpec((tile * TOP_K, LANES), lambda i: (i, 0)),
                  pl.BlockSpec((tile * ROW_TILES, LANES), lambda i: (i, 0)),
                  pl.BlockSpec((None, ROW_TILES, LANES), lambda i: (i // tiles_per_batch, 0, 0)),
                  pl.BlockSpec((1, D_MODEL), lambda i: (0, 0))],
        out_specs=pl.BlockSpec((tile, D_MODEL), lambda i: (i, 0)),
        scratch_shapes=[pltpu.SMEM((2 * n_asg,), I32),
                        pltpu.SMEM((2 * n_asg,), I32),
                        pltpu.VMEM((2 * WIN_CHUNKS * WIN_ROWS * ROW_TILES, LANES), F32),
                        pltpu.VMEM((tile * ROW_TILES, LANES), F32),
                        pltpu.SemaphoreType.DMA((2,)),
                        pltpu.SemaphoreType.DMA((2,))],
        compiler_params=_cparams(("arbitrary",)),
        name="combine_norm",
    )(pos_tiles, src_tiles, yb, gate_rows, h1, g2_tiles, nw)


def _rope_tables(n):
    rows = n // GRID_W
    row = jnp.repeat(jnp.arange(rows, dtype=F32), GRID_W)
    col = jnp.tile(jnp.arange(GRID_W, dtype=F32), rows)
    n_freq = RET_DK // 4
    inv = ROPE_BASE ** (-jnp.arange(n_freq, dtype=F32) / n_freq)
    ang = jnp.concatenate([row[:, None] * inv, col[:, None] * inv], axis=-1)
    cos, sin = jnp.cos(ang), jnp.sin(ang)
    cos_h = jnp.concatenate([cos, cos], axis=-1)
    sin_h = jnp.concatenate([-sin, sin], axis=-1)
    return jnp.tile(cos_h, (1, RET_HEADS)), jnp.tile(sin_h, (1, RET_HEADS))


def kernel(x, c, ctx, c_ctx, w_ada, b_ada, norm_mix_w, norm_ffn_w, w_in, w_out, hg_lb,
           hg_norm_w, router_w, router_b, w1, b1, w2, b2, norm_final_w):
    B, N, D = x.shape
    C = ctx.shape[1]
    T = B * N
    assert D == D_MODEL and w_ada.shape[0] == 1

    cc = jnp.concatenate([c.astype(F32), c_ctx.astype(F32)[None, :],
                          jnp.zeros((16 - B - 1, D), F32)], axis=0)
    mod = _ada(cc, w_ada[0], b_ada[0][None, :])
    mod = mod.reshape(16, 6, 1, D).transpose(1, 0, 2, 3)
    sh1, sc1, g1, sh2, sc2, g2 = (mod[i] for i in range(6))

    w_in_bf = w_in[0].astype(BF16)
    w_out_bf = w_out[0].astype(BF16)
    nw_mix = norm_mix_w[0][None, :]
    cos_t, sin_t = _rope_tables(N)

    pb, lf = _inproj(x, sc1, sh1, None, nw_mix, w_in_bf, hg_lb[:2], cos_t, sin_t,
                     rope=True, tm=1024)
    pbc, lfc = _inproj(ctx, sc1, sh1, B, nw_mix, w_in_bf, hg_lb[:2], cos_t[:C], sin_t[:C],
                       rope=False, tm=C)

    a_ret = _retention(pb, pbc)
    a_hg = _hgrn(pb, lf, pbc, lfc, hg_norm_w[0][None, :])

    rw = jnp.zeros((D, LANES), F32).at[:, :N_EXPERTS].set(router_w[0])
    rb = jnp.full((1, LANES), -1e30, F32).at[0, :N_EXPERTS].set(router_b[0])
    h1, v, route, route_t, gate_rows, base_cnt, cnt = _outproj(a_ret, a_hg, w_out_bf, x, g1, sc2, sh2,
                                          norm_ffn_w[0][None, :], rw, rb, tm=512)

    del route
    idx = route_t[0:TOP_K].astype(I32)
    rank = route_t[2 * TOP_K:3 * TOP_K].astype(I32)
    counts = cnt[0, :N_EXPERTS].astype(I32)
    padded = (counts + ROW_BLOCK - 1) // ROW_BLOCK * ROW_BLOCK
    pend = jnp.cumsum(padded)
    pstart = pend - padded
    cstart = jnp.cumsum(counts) - counts
    onehot = idx[None] == jnp.arange(N_EXPERTS, dtype=I32)[:, None, None]

    def dest_tiles(first_row, tile):
        dest = (jnp.sum(jnp.where(onehot, first_row[:, None, None], 0), axis=0) + rank) * ROW_TILES
        return dest.reshape(TOP_K, T // tile, tile).transpose(1, 0, 2).reshape(T // tile,
                                                                               TOP_K * tile)
    n_blocks = (T * TOP_K) // ROW_BLOCK + N_EXPERTS
    starts = jnp.arange(n_blocks, dtype=I32) * ROW_BLOCK
    block_e = jnp.minimum(jnp.sum((pend[None, :] <= starts[:, None]).astype(I32), axis=1),
                          N_EXPERTS - 1)
    valid = jnp.clip(counts[block_e] - (starts - pstart[block_e]), 0, ROW_BLOCK)
    nsub = (valid + SUB_ROWS - 1) // SUB_ROWS
    first_from = lax.cummin(jnp.where(counts > 0, jnp.arange(N_EXPERTS, dtype=I32), N_EXPERTS),
                            reverse=True)
    next_e = jnp.concatenate([first_from[1:], jnp.full((1,), N_EXPERTS, I32)])
    next_e = jnp.where(next_e == N_EXPERTS, -1, next_e)
    xrow = jnp.where(nsub > 0, cstart[block_e] + starts - pstart[block_e], 0) * ROW_TILES

    tile_d = 512
    xs = _dispatch(dest_tiles(cstart, tile_d), v, tile=tile_d)
    yb = _moe(block_e, nsub, next_e, xrow.astype(I32), xs, w1[0], b1[0][:, None, :], w2[0],
              b2[0][:, None, :], n_blocks=n_blocks)

    tc = COMBINE_TILE
    n_ct = T // tc
    pieces = 512 // tc
    base = base_cnt.reshape(T // 512, 8, LANES)[:, :pieces, :N_EXPERTS].reshape(n_ct, N_EXPERTS)
    base = base.astype(I32)
    n_run = jnp.concatenate([base[1:], counts[None, :]], axis=0) - base
    n_chunk = (n_run + WIN_ROWS - 1) // WIN_ROWS
    chunk_end = jnp.cumsum(n_chunk, axis=1)
    chunk_start = chunk_end - n_chunk
    win_off = chunk_start * WIN_ROWS
    c_id = jnp.arange(WIN_CHUNKS, dtype=I32)
    c_exp = jnp.sum((chunk_end[:, None, :] <= c_id[None, :, None]).astype(I32), axis=2)
    c_real = c_exp < N_EXPERTS
    c_own = c_exp[:, :, None] == jnp.arange(N_EXPERTS, dtype=I32)
    take = lambda tab: jnp.sum(jnp.where(c_own, tab[:, None, :], 0), axis=2)
    c_src = (take(pstart[None, :] + base) + (c_id[None, :] - take(chunk_start)) * WIN_ROWS)
    src_tiles = jnp.concatenate(
        [jnp.where(c_real, c_src, 0) * ROW_TILES,
         jnp.zeros((n_ct, tc * TOP_K - WIN_CHUNKS), I32)], axis=1)
    slot_off = (jnp.arange(n_ct, dtype=I32) % 2) * (WIN_CHUNKS * WIN_ROWS)
    shift = jnp.repeat(win_off - base + slot_off[:, None], tc, axis=0).T
    pos = (jnp.sum(jnp.where(onehot, shift[:, None, :], 0), axis=0) + rank) * ROW_TILES

    def per_tile(a):
        return a.reshape(TOP_K, n_ct, tc).transpose(1, 2, 0).reshape(n_ct, tc * TOP_K)

    out = _combine(per_tile(pos), src_tiles, yb, gate_rows, h1,
                   g2.reshape(16, ROW_TILES, LANES), norm_final_w[None, :],
                   tiles_per_batch=N // tc)
    return out.reshape(B, N, D)
```

```python
import functools
import math

import jax
import jax.numpy as jnp
import numpy as np
from jax import lax
from jax.experimental import pallas as pl
from jax.experimental.pallas import tpu as pltpu

F32 = jnp.float32
BF16 = jnp.bfloat16
I32 = jnp.int32

D_MODEL = 1024
GRID_W = 64
RET_HEADS = 4
RET_DK = 64
HG_HEADS = 4
PROJ_W = 4096
ROPE_BASE = 10000.0
EPS = 1e-6
N_EXPERTS = 32
TOP_K = 4
D_FF = 1024
SWIGLU_LIMIT = 7.0
SWIGLU_ALPHA = 1.702
LOG2_E = 1.0 / math.log(2.0)

LANES = 128
CHUNK = 128
ROW_BLOCK = 1024
SUB_ROWS = 256
COMBINE_TILE = 512
WIN_ROWS = 16
WIN_CHUNKS = (COMBINE_TILE * TOP_K + N_EXPERTS * (WIN_ROWS - 1) + WIN_ROWS - 1) // WIN_ROWS
VMEM_LIMIT = 56 * 1024 * 1024
ROW_TILES = D_MODEL // LANES

C_RQ, C_RK, C_RV, C_RG, C_HQ, C_FF, C_FB, C_HV, C_HG = (
    0, 256, 512, 1024, 1536, 2048, 2560, 3072, 3584)


def _cparams(sem):
    return pltpu.CompilerParams(dimension_semantics=sem, vmem_limit_bytes=VMEM_LIMIT)


def _split_bf16(x):
    hi = x.astype(BF16)
    lo = (x - hi.astype(F32)).astype(BF16)
    return hi, lo


def _dot(a, b):
    return jnp.dot(a, b, preferred_element_type=F32)


def _dot_nt(a, b):
    return lax.dot_general(a, b, (((1,), (1,)), ((), ())), preferred_element_type=F32)


def _dot3(a, b):
    ah, al = _split_bf16(a)
    bh, bl = _split_bf16(b)
    return _dot(ah, bh) + (_dot(ah, bl) + _dot(al, bh))


def _silu(x):
    return x * jax.nn.sigmoid(x)


def _load_tile_rows(ref, n):
    return jnp.concatenate([ref[pl.ds(s, n, stride=ROW_TILES), :] for s in range(ROW_TILES)],
                           axis=1)


def _store_tile_rows(ref, x):
    n = x.shape[0]
    for s in range(ROW_TILES):
        ref[pl.ds(s, n, stride=ROW_TILES), :] = x[:, s * LANES:(s + 1) * LANES]


def _ada_kernel(c_ref, w_ref, b_ref, o_ref):
    s = _silu(c_ref[...])
    o_ref[...] = _dot3(s, w_ref[...]) + b_ref[...]


def _ada(cc, w, b):
    nblk = w.shape[1] // D_MODEL
    return pl.pallas_call(
        _ada_kernel,
        out_shape=jax.ShapeDtypeStruct((cc.shape[0], w.shape[1]), F32),
        grid=(nblk,),
        in_specs=[pl.BlockSpec(cc.shape, lambda j: (0, 0)),
                  pl.BlockSpec((D_MODEL, D_MODEL), lambda j: (0, j)),
                  pl.BlockSpec((1, D_MODEL), lambda j: (0, j))],
        out_specs=pl.BlockSpec((cc.shape[0], D_MODEL), lambda j: (0, j)),
        compiler_params=_cparams(("arbitrary",)),
        name="ada",
    )(cc, w, b)


def _inproj_kernel(x_ref, sc_ref, sh_ref, nw_ref, w_ref, lb_ref, cos_ref, sin_ref,
                   pb_ref, lf_ref, u_scr, *, rope):
    x = x_ref[...]
    ms = jnp.mean(x * x, axis=-1, keepdims=True)
    u = x * lax.rsqrt(ms + EPS) * nw_ref[...] * (1.0 + sc_ref[...]) + sh_ref[...]
    u_scr[...] = u.astype(BF16)

    def proj(lo, width):
        return _dot(u_scr[...], w_ref[:, lo:lo + width])

    tm = x.shape[0]
    if rope:
        lane = lax.broadcasted_iota(I32, (tm, LANES), 1)
        first = (lane & 32) == 0

    def put_rot(col, scale):
        for j in range(2):
            lo = col + j * LANES
            t = proj(lo, LANES)
            if scale != 1.0:
                t = t * scale
            if rope:
                tb = j * LANES
                rot = jnp.where(first, pltpu.roll(t, 96, axis=1), pltpu.roll(t, 32, axis=1))
                t = t * cos_ref[:, tb:tb + LANES] + rot * sin_ref[:, tb:tb + LANES]
            pb_ref[:, lo:lo + LANES] = t.astype(BF16)

    put_rot(C_RQ, 1.0)
    put_rot(C_RK, RET_DK ** -0.5)
    pb_ref[:, C_RV:C_RV + 512] = proj(C_RV, 512).astype(BF16)
    pb_ref[:, C_RG:C_RG + 512] = _silu(proj(C_RG, 512)).astype(BF16)
    pb_ref[:, C_HQ:C_HQ + 512] = _silu(proj(C_HQ, 512)).astype(BF16)
    pb_ref[:, C_HV:C_HV + 512] = proj(C_HV, 512).astype(BF16)
    pb_ref[:, C_HG:C_HG + 512] = _silu(proj(C_HG, 512)).astype(BF16)

    la = lb_ref[0]
    lbb = lb_ref[1]
    mx = jnp.maximum(la, lbb)
    ea = jnp.exp(la - mx)
    eb = jnp.exp(lbb - mx)
    lb = ea / (ea + eb)
    for d, col in enumerate((C_FF, C_FB)):
        lbd = lb[d:d + 1, :]
        f = lbd + (1.0 - lbd) * jax.nn.sigmoid(proj(col, 512))
        pb_ref[:, col:col + 512] = (1.0 - f).astype(BF16)
        lf_ref[:, d * 512:(d + 1) * 512] = jnp.log(f) * LOG2_E


def _inproj(x, sc, sh, mod_row, nw, w_bf, hg_lb, cos_t, sin_t, *, rope, tm):
    B, n, _ = x.shape
    nt = n // tm
    if mod_row is None:
        mrow = lambda b, j: (b, 0, 0)
    else:
        mrow = lambda b, j: (mod_row, 0, 0)
    return pl.pallas_call(
        functools.partial(_inproj_kernel, rope=rope),
        out_shape=(jax.ShapeDtypeStruct((B, n, PROJ_W), BF16),
                   jax.ShapeDtypeStruct((B, n, 1024), F32)),
        grid=(B, nt),
        in_specs=[pl.BlockSpec((None, tm, D_MODEL), lambda b, j: (b, j, 0)),
                  pl.BlockSpec((None, 1, D_MODEL), mrow),
                  pl.BlockSpec((None, 1, D_MODEL), mrow),
                  pl.BlockSpec((1, D_MODEL), lambda b, j: (0, 0)),
                  pl.BlockSpec((D_MODEL, PROJ_W), lambda b, j: (0, 0),
                               pipeline_mode=pl.Buffered(1)),
                  pl.BlockSpec((2, 2, 512), lambda b, j: (0, 0, 0)),
                  pl.BlockSpec((tm, 256), lambda b, j: (j, 0)),
                  pl.BlockSpec((tm, 256), lambda b, j: (j, 0))],
        out_specs=(pl.BlockSpec((None, tm, PROJ_W), lambda b, j: (b, j, 0)),
                   pl.BlockSpec((None, tm, 1024), lambda b, j: (b, j, 0))),
        scratch_shapes=[pltpu.VMEM((tm, D_MODEL), BF16)],
        compiler_params=_cparams(("arbitrary", "arbitrary")),
        name="inproj_rope" if rope else "inproj_ctx",
    )(x, sc, sh, nw, w_bf, hg_lb, cos_t, sin_t)


_RET_LGF = [math.log1p(-(2.0 ** (-5.0 - 2.0 * h))) for h in range(RET_HEADS)]
_RET_LGB = [math.log1p(-(2.0 ** (-6.0 - 2.0 * h))) for h in range(RET_HEADS)]


def _ret_kernel(q_ref, k_ref, v_ref, g_ref, kc_ref, vc_ref, o_ref,
                u_scr, sin_scr, kt_scr, dtot_scr, *, n_lat, n_ctx):
    L = CHUNK
    pair = pl.program_id(1)
    row = lax.broadcasted_iota(I32, (L, L), 0).astype(F32)
    col = lax.broadcasted_iota(I32, (L, L), 1).astype(F32)
    lane = lax.broadcasted_iota(I32, (L, LANES), 1)
    trow = lax.broadcasted_iota(I32, (L, 1), 0).astype(F32)
    tcol = lax.broadcasted_iota(I32, (1, L), 1).astype(F32)
    low_half = lane < RET_DK

    def u_chunk(k_blk, v_blk, ci, store_kt):
        kt = k_blk.astype(F32).T
        if store_kt is not None:
            kt_scr[store_kt] = kt.astype(BF16)
        for hh in range(2):
            lgf = jnp.where(pair == 0, _RET_LGF[hh], _RET_LGF[2 + hh])
            lgb = jnp.where(pair == 0, _RET_LGB[hh], _RET_LGB[2 + hh])
            kth = kt[hh * RET_DK:(hh + 1) * RET_DK, :]
            wkf = jnp.exp(lgf * (L - 1.0 - tcol))
            wkb = jnp.exp(lgb * tcol)
            lhs = jnp.concatenate([kth * wkf, kth * wkb], axis=0).astype(BF16)
            u_scr[hh, ci] = _dot(lhs, v_blk[:, hh * LANES:(hh + 1) * LANES])

    for c in range(n_ctx):
        u_chunk(kc_ref[c * L:(c + 1) * L, :], vc_ref[c * L:(c + 1) * L, :], c, None)

    def lat_u(c, carry):
        r0 = pl.multiple_of(c * L, L)
        u_chunk(k_ref[pl.ds(r0, L), :], v_ref[pl.ds(r0, L), :], n_ctx + c, c)
        return carry
    lax.fori_loop(0, n_lat, lat_u, 0, unroll=2)

    ones = jnp.ones((RET_DK, LANES), F32)
    for hh in range(2):
        lgf = jnp.where(pair == 0, _RET_LGF[hh], _RET_LGF[2 + hh])
        lgb = jnp.where(pair == 0, _RET_LGB[hh], _RET_LGB[2 + hh])
        d = row - col
        dtot_scr[hh] = jnp.where(d > 0, jnp.exp(lgf * jnp.maximum(d, 0.0)),
                                 jnp.where(d < 0, jnp.exp(lgb * jnp.maximum(-d, 0.0)), 2.0))
        af = jnp.exp(ones * (lgf * L))
        ab = jnp.exp(ones * (lgb * L))

        s = jnp.zeros((RET_DK, LANES), F32)
        for c in range(n_ctx):
            s = af * s + u_scr[hh, c, 0:RET_DK, :]
        sb = jnp.zeros((RET_DK, LANES), F32)
        for c in reversed(range(n_ctx)):
            sb = ab * sb + u_scr[hh, c, RET_DK:2 * RET_DK, :]

        def fwd(c, s, hh=hh, af=af):
            sin_scr[hh, c, 0:RET_DK, :] = s.astype(BF16)
            return af * s + u_scr[hh, n_ctx + c, 0:RET_DK, :]
        lax.fori_loop(0, n_lat, fwd, s)

        def bwd(i, sb, hh=hh, ab=ab):
            c = n_lat - 1 - i
            sin_scr[hh, c, RET_DK:2 * RET_DK, :] = sb.astype(BF16)
            return ab * sb + u_scr[hh, n_ctx + c, RET_DK:2 * RET_DK, :]
        lax.fori_loop(0, n_lat, bwd, sb)

    def out_chunk(c, carry):
        r0 = pl.multiple_of(c * L, L)
        q = q_ref[pl.ds(r0, L), :].astype(F32)
        qr = pltpu.roll(q, RET_DK, axis=1)
        kt = kt_scr[c]
        for hh in range(2):
            lgf = jnp.where(pair == 0, _RET_LGF[hh], _RET_LGF[2 + hh])
            lgb = jnp.where(pair == 0, _RET_LGB[hh], _RET_LGB[2 + hh])
            mine = low_half if hh == 0 else jnp.logical_not(low_half)
            qm = jnp.where(mine, q, 0.0).astype(BF16)
            p = (_dot(qm, kt) * dtot_scr[hh]).astype(BF16)
            vh = v_ref[pl.ds(r0, L), hh * LANES:(hh + 1) * LANES]
            wqf = jnp.exp(lgf * (trow + 1.0))
            wqb = jnp.exp(lgb * (L - trow))
            qa, qb = (q, qr) if hh == 0 else (qr, q)
            qs = jnp.where(low_half, qa * wqf, qb * wqb).astype(BF16)
            o = _dot(p, vh) + _dot(qs, sin_scr[hh, c])
            ms = jnp.mean(o * o, axis=-1, keepdims=True)
            gh = g_ref[pl.ds(r0, L), hh * LANES:(hh + 1) * LANES].astype(F32)
            o_ref[pl.ds(r0, L), hh * LANES:(hh + 1) * LANES] = (
                o * lax.rsqrt(ms + EPS) * gh).astype(BF16)
        return carry
    lax.fori_loop(0, n_lat, out_chunk, 0, unroll=2)


def _retention(pb, pbc):
    B, n, _ = pb.shape
    nc = pbc.shape[1]
    n_lat, n_ctx = n // CHUNK, nc // CHUNK
    return pl.pallas_call(
        functools.partial(_ret_kernel, n_lat=n_lat, n_ctx=n_ctx),
        out_shape=jax.ShapeDtypeStruct((B, n, 512), BF16),
        grid=(B, 2),
        in_specs=[pl.BlockSpec((None, n, LANES), lambda b, p: (b, 0, C_RQ // LANES + p)),
                  pl.BlockSpec((None, n, LANES), lambda b, p: (b, 0, C_RK // LANES + p)),
                  pl.BlockSpec((None, n, 256), lambda b, p: (b, 0, C_RV // 256 + p)),
                  pl.BlockSpec((None, n, 256), lambda b, p: (b, 0, C_RG // 256 + p)),
                  pl.BlockSpec((None, nc, LANES), lambda b, p: (b, 0, C_RK // LANES + p)),
                  pl.BlockSpec((None, nc, 256), lambda b, p: (b, 0, C_RV // 256 + p))],
        out_specs=pl.BlockSpec((None, n, 256), lambda b, p: (b, 0, p)),
        scratch_shapes=[pltpu.VMEM((2, n_lat + n_ctx, CHUNK, LANES), F32),
                        pltpu.VMEM((2, n_lat, CHUNK, LANES), BF16),
                        pltpu.VMEM((n_lat, LANES, CHUNK), BF16),
                        pltpu.VMEM((2, CHUNK, CHUNK), F32)],
        compiler_params=_cparams(("arbitrary", "arbitrary")),
        name="retention",
    )(pb, pb, pb, pb, pbc, pbc)


_LEVELS = (64, 32, 16, 8, 4, 2, 1)


def _expand_rows(r, rep):
    n = r.shape[0]
    if n == 1:
        return jnp.broadcast_to(r, (rep, r.shape[1]))
    return jnp.concatenate(
        [jnp.broadcast_to(r[i:i + 1, :], (rep, r.shape[1])) for i in range(n)], axis=0)


def _hgrn_kernel(q_ref, kf_ref, kb_ref, v_ref, g_ref, lff_ref, lfb_ref,
                 kfc_ref, kbc_ref, vc_ref, lffc_ref, lfbc_ref, nw_ref, o_ref,
                 ut_scr, a_scr, qs_scr, oi_scr, sin_scr, bfb_scr, *, n_lat, n_ctx):
    L = CHUNK
    row = lax.broadcasted_iota(I32, (L, L), 0)
    col = lax.broadcasted_iota(I32, (L, L), 1)
    xr_bits = lax.bitcast_convert_type((row ^ col).astype(F32), I32)
    lv = lax.shift_right_logical(xr_bits, 23) - 127
    row2 = lax.broadcasted_iota(I32, (L, 2 * L), 0)
    col2 = lax.broadcasted_iota(I32, (L, 2 * L), 1) & (L - 1)
    tril2 = jnp.where(col2 <= row2, 1.0, 0.0).astype(BF16)
    triu2 = jnp.where(col2 >= row2, 1.0, 0.0).astype(BF16)

    def cums(lff, lfb):
        hf, lof = _split_bf16(lff)
        hb, lob = _split_bf16(lfb)
        bf = _dot(tril2, jnp.concatenate([hf, lof], axis=0))
        bb = _dot(triu2, jnp.concatenate([hb, lob], axis=0))
        return bf, bb

    def state_part(ci, kf, kb, v_blk, bf, bb):
        endf = bf[L - 1:L, :]
        endb = bb[0:1, :]
        ksf = kf * jnp.exp2(endf - bf)
        ksb = kb * jnp.exp2(endb - bb)
        vt = v_blk.astype(F32).T.astype(BF16)
        ut_scr[ci] = _dot(vt, jnp.concatenate([ksf, ksb], axis=1).astype(BF16))
        a_scr[ci] = jnp.broadcast_to(
            jnp.concatenate([jnp.exp2(endf), jnp.exp2(endb)], axis=1), (8, 2 * LANES))

    for c in range(n_ctx):
        sl = slice(c * L, (c + 1) * L)
        bf, bb = cums(lffc_ref[sl, :], lfbc_ref[sl, :])
        state_part(c, kfc_ref[sl, :].astype(F32), kbc_ref[sl, :].astype(F32),
                   vc_ref[sl, :], bf, bb)

    def lat_chunk(c, u):
        rows = pl.ds(pl.multiple_of(c * L, L), L)
        bf_scr = bfb_scr.at[u, 0]
        bb_scr = bfb_scr.at[u, 1]
        q = q_ref[rows, :].astype(F32)
        kf = kf_ref[rows, :].astype(F32)
        kb = kb_ref[rows, :].astype(F32)
        v_blk = v_ref[rows, :]
        lff = lff_ref[rows, :]
        lfb = lfb_ref[rows, :]
        bf, bb = cums(lff, lfb)
        state_part(n_ctx + c, kf, kb, v_blk, bf, bb)
        qs_scr[rows, :] = jnp.concatenate([q * jnp.exp2(bf), q * jnp.exp2(bb)],
                                          axis=1).astype(BF16)
        bf_scr[...] = bf
        bb_scr[...] = bb

        acc = jnp.zeros((L, L), F32)
        for lvl, h in enumerate(_LEVELS):
            bit = (row & h) != 0
            ksel = jnp.where(bit, kb, kf)
            if h >= 4:
                n = (L // 2) // h
                if n == 1:
                    rf = bf_scr[h - 1:h, :]
                    rb = bb_scr[h:h + 1, :]
                else:
                    rf = bf_scr[pl.ds(h - 1, n, stride=2 * h), :]
                    rb = bb_scr[pl.ds(h, n, stride=2 * h), :]
                df = bf - _expand_rows(rf, 2 * h)
                db = bb - _expand_rows(rb, 2 * h)
                eq = jnp.where(bit, df, db)
                ek = -jnp.where(bit, db, df)
            elif h == 2:
                m = row & 3
                lff_n = pltpu.roll(lff, L - 1, axis=0)
                lfb_n = pltpu.roll(lfb, L - 1, axis=0)
                eq = jnp.where(m == 2, lff,
                               jnp.where(m == 3, lff + pltpu.roll(lff, 1, axis=0),
                                         jnp.where(m == 0, lfb + lfb_n, lfb)))
                ek = jnp.where(m == 3, pltpu.roll(lfb, 1, axis=0),
                               jnp.where(m == 0, lff_n, 0.0))
            else:
                eq = jnp.where(bit, lff, lfb)
                ek = None
            lhs = (q * jnp.exp2(eq)).astype(BF16)
            rhs = (ksel if ek is None else ksel * jnp.exp2(ek)).astype(BF16)
            acc = jnp.where(lv == 6 - lvl, _dot_nt(lhs, rhs), acc)

        dsum = jnp.sum(q * (kf + kb), axis=-1, keepdims=True)
        oi_scr[rows, :] = _dot(acc.astype(BF16), v_blk) + dsum * v_blk.astype(F32)

    def lat_pair(i, carry):
        for u in range(4):
            lat_chunk(4 * i + u, u)
        return carry
    lax.fori_loop(0, n_lat // 4, lat_pair, 0)

    st = jnp.zeros((LANES, LANES), F32)
    for c in range(n_ctx):
        st = st * a_scr[c, 0:1, 0:LANES] + ut_scr[c, :, 0:LANES]
    stb = jnp.zeros((LANES, LANES), F32)
    for c in reversed(range(n_ctx)):
        stb = stb * a_scr[c, 0:1, LANES:2 * LANES] + ut_scr[c, :, LANES:2 * LANES]

    def fwd(c, st):
        sin_scr[c, :, 0:LANES] = st.astype(BF16)
        ci = n_ctx + c
        return st * a_scr[ci, 0:1, 0:LANES] + ut_scr[ci, :, 0:LANES]
    lax.fori_loop(0, n_lat, fwd, st)

    def bwd(i, stb):
        c = n_lat - 1 - i
        sin_scr[c, :, LANES:2 * LANES] = stb.astype(BF16)
        ci = n_ctx + c
        return stb * a_scr[ci, 0:1, LANES:2 * LANES] + ut_scr[ci, :, LANES:2 * LANES]
    lax.fori_loop(0, n_lat, bwd, stb)

    def out_chunk(c, carry):
        rows = pl.ds(pl.multiple_of(c * L, L), L)
        o = oi_scr[rows, :] + _dot_nt(qs_scr[rows, :], sin_scr[c])
        ms = jnp.mean(o * o, axis=-1, keepdims=True)
        y = o * lax.rsqrt(ms + EPS) * nw_ref[...] * g_ref[rows, :].astype(F32)
        o_ref[rows, :] = y.astype(BF16)
        return carry
    lax.fori_loop(0, n_lat, out_chunk, 0, unroll=4)


def _hgrn(pb, lf, pbc, lfc, nw):
    B, n, _ = pb.shape
    nc = pbc.shape[1]
    n_lat, n_ctx = n // CHUNK, nc // CHUNK

    def colblk(rows, col0):
        return pl.BlockSpec((None, rows, LANES), lambda b, h: (b, 0, col0 // LANES + h))

    return pl.pallas_call(
        functools.partial(_hgrn_kernel, n_lat=n_lat, n_ctx=n_ctx),
        out_shape=jax.ShapeDtypeStruct((B, n, 512), BF16),
        grid=(B, HG_HEADS),
        in_specs=[colblk(n, C_HQ), colblk(n, C_FF), colblk(n, C_FB), colblk(n, C_HV),
                  colblk(n, C_HG), colblk(n, 0), colblk(n, 512),
                  colblk(nc, C_FF), colblk(nc, C_FB), colblk(nc, C_HV),
                  colblk(nc, 0), colblk(nc, 512),
                  pl.BlockSpec((1, LANES), lambda b, h: (0, 0))],
        out_specs=pl.BlockSpec((None, n, LANES), lambda b, h: (b, 0, h)),
        scratch_shapes=[pltpu.VMEM((n_lat + n_ctx, LANES, 2 * LANES), F32),
                        pltpu.VMEM((n_lat + n_ctx, 8, 2 * LANES), F32),
                        pltpu.VMEM((n, 2 * LANES), BF16),
                        pltpu.VMEM((n, LANES), F32),
                        pltpu.VMEM((n_lat, LANES, 2 * LANES), BF16),
                        pltpu.VMEM((4, 2, CHUNK, LANES), F32)],
        compiler_params=_cparams(("arbitrary", "arbitrary")),
        name="hgrn2",
    )(pb, pb, pb, pb, pb, lf, lf, pbc, pbc, pbc, lfc, lfc, nw)


def _outproj_kernel(ar_ref, ah_ref, w_ref, x_ref, g1_ref, sc_ref, sh_ref, nw_ref,
                    rwh_ref, rwl_ref, rb_ref, tri_ref, h1_ref, v_ref, route_t_ref, gate_ref, base_ref,
                    cnt_ref, cnt_scr):
    first_step = jnp.logical_and(pl.program_id(0) == 0, pl.program_id(1) == 0)

    @pl.when(first_step)
    def _():
        cnt_scr[...] = jnp.zeros_like(cnt_scr)

    y = _dot(ar_ref[...], w_ref[0:512, :]) + _dot(ah_ref[...], w_ref[512:1024, :])
    h1 = x_ref[...] + g1_ref[...] * y
    _store_tile_rows(h1_ref, h1)
    ms = jnp.mean(h1 * h1, axis=-1, keepdims=True)
    v = h1 * lax.rsqrt(ms + EPS) * nw_ref[...] * (1.0 + sc_ref[...]) + sh_ref[...]
    _store_tile_rows(v_ref, v)

    tm = v.shape[0]
    vh, vl = _split_bf16(v)
    rwh, rwl = rwh_ref[...], rwl_ref[...]
    l = (_dot_nt(rwh, vh) + (_dot_nt(rwh, vl) + _dot_nt(rwl, vh))) + rb_ref[:, 0:1]
    row_f = lax.broadcasted_iota(I32, (N_EXPERTS, tm), 0).astype(F32)
    sels, tops, idxs = [], [], []
    for _ in range(TOP_K):
        m = jnp.max(l, axis=0, keepdims=True)
        i = jnp.min(jnp.where(l == m, row_f, float(N_EXPERTS)), axis=0, keepdims=True)
        sel = row_f == i
        l = jnp.where(sel, -jnp.inf, l)
        sels.append(sel)
        tops.append(m)
        idxs.append(i)
    es = [jnp.exp(t - tops[0]) for t in tops]
    den = es[0] + es[1] + es[2] + es[3]
    gates = [e / den for e in es]

    oh = jnp.zeros((N_EXPERTS, tm), F32)
    for sel in sels:
        oh = jnp.where(sel, 1.0, oh)
    cnt = cnt_scr[:, 0:1]
    before = _dot(oh.astype(BF16), tri_ref[...]) + cnt
    ranks = [jnp.sum(jnp.where(sel, before, 0.0), axis=0, keepdims=True) for sel in sels]
    base_ref[...] = cnt_scr[...]
    cnt_scr[...] = cnt_scr[...] + jnp.sum(oh, axis=1, keepdims=True)
    cnt_ref[...] = cnt_scr[...]

    zero4 = jnp.zeros((TOP_K, tm), F32)
    route_t_ref[...] = jnp.concatenate(idxs + [zero4] + ranks + [zero4], axis=0)
    g_cols = jnp.concatenate(gates + [zero4], axis=0).T
    for k in range(TOP_K):
        gate_ref[pl.ds(k, tm, stride=TOP_K), :] = jnp.broadcast_to(g_cols[:, k:k + 1], (tm, LANES))


def _outproj(a_ret, a_hg, w_bf, x, g1, sc2, sh2, nw, rwh, rwl, rb, *, tm):
    B, n, _ = x.shape
    nt = n // tm
    T = B * n
    assert tm == COMBINE_TILE
    mrow = lambda b, j: (b, 0, 0)
    tok = lambda b, j: (b * nt + j, 0)
    const = lambda b, j: (0, 0)
    tri = jnp.triu(jnp.ones((tm, tm), BF16), 1)
    return pl.pallas_call(
        _outproj_kernel,
        out_shape=(jax.ShapeDtypeStruct((T * ROW_TILES, LANES), F32),
                   jax.ShapeDtypeStruct((T * ROW_TILES, LANES), F32),
                   jax.ShapeDtypeStruct((16, T), F32),
                   jax.ShapeDtypeStruct((T * TOP_K, LANES), F32),
                   jax.ShapeDtypeStruct((B * nt * N_EXPERTS, LANES), F32),
                   jax.ShapeDtypeStruct((N_EXPERTS, LANES), F32)),
        grid=(B, nt),
        in_specs=[pl.BlockSpec((None, tm, 512), lambda b, j: (b, j, 0)),
                  pl.BlockSpec((None, tm, 512), lambda b, j: (b, j, 0)),
                  pl.BlockSpec((D_MODEL, D_MODEL), const),
                  pl.BlockSpec((None, tm, D_MODEL), lambda b, j: (b, j, 0)),
                  pl.BlockSpec((None, 1, D_MODEL), mrow),
                  pl.BlockSpec((None, 1, D_MODEL), mrow),
                  pl.BlockSpec((None, 1, D_MODEL), mrow),
                  pl.BlockSpec((1, D_MODEL), const),
                  pl.BlockSpec((N_EXPERTS, D_MODEL), const),
                  pl.BlockSpec((N_EXPERTS, D_MODEL), const),
                  pl.BlockSpec((N_EXPERTS, LANES), const),
                  pl.BlockSpec((tm, tm), const)],
        out_specs=(pl.BlockSpec((tm * ROW_TILES, LANES), tok),
                   pl.BlockSpec((tm * ROW_TILES, LANES), tok),
                   pl.BlockSpec((16, tm), lambda b, j: (0, b * nt + j)),
                   pl.BlockSpec((tm * TOP_K, LANES), tok),
                   pl.BlockSpec((N_EXPERTS, LANES), tok),
                   pl.BlockSpec((N_EXPERTS, LANES), const)),
        scratch_shapes=[pltpu.VMEM((N_EXPERTS, LANES), F32)],
        compiler_params=_cparams(("arbitrary", "arbitrary")),
        name="outproj_router",
    )(a_ret, a_hg, w_bf, x, g1, sc2, sh2, nw, rwh, rwl, rb, tri)


def _dispatch_kernel(dest_hbm, v_ref, xs_hbm, idx_smem, zbuf, sem_idx, sem_z, sem_rows,
                     *, tile, n_tiles, n_rows):
    i = pl.program_id(0)
    slot = lax.rem(i, 2)

    def idx_copy(j, s):
        n_asg = tile * TOP_K
        return pltpu.make_async_copy(dest_hbm.at[j], idx_smem.at[pl.ds(s * n_asg, n_asg)],
                                     sem_idx.at[s])

    def zero_copy(j):
        r0 = (n_rows + j * SUB_ROWS) * ROW_TILES
        return pltpu.make_async_copy(zbuf, xs_hbm.at[pl.ds(r0, SUB_ROWS * ROW_TILES), :], sem_z)

    @pl.when(i == 0)
    def _():
        idx_copy(0, 0).start()
        zbuf[...] = jnp.zeros_like(zbuf)
        for j in range(ROW_BLOCK // SUB_ROWS):
            zero_copy(j).start()
        for j in range(ROW_BLOCK // SUB_ROWS):
            zero_copy(j).wait()

    idx_copy(i, slot).wait()

    @pl.when(i + 1 < n_tiles)
    def _():
        idx_copy(i + 1, 1 - slot).start()

    def issue(t, carry):
        e0 = slot * (tile * TOP_K) + t * TOP_K
        for k in range(TOP_K):
            d = pl.multiple_of(idx_smem[e0 + k], ROW_TILES)
            src = v_ref.at[pl.ds(pl.multiple_of(t * ROW_TILES, ROW_TILES), ROW_TILES), :]
            pltpu.make_async_copy(src, xs_hbm.at[pl.ds(d, ROW_TILES), :],
                                  sem_rows).start(priority=k % 2)
        return carry
    lax.fori_loop(0, tile, issue, 0, unroll=4)
    for _ in range(TOP_K):
        pltpu.make_async_copy(v_ref, xs_hbm.at[pl.ds(0, tile * ROW_TILES), :], sem_rows).wait()


def _dispatch(dest_tiles, v, *, tile):
    T = v.shape[0] // ROW_TILES
    n_rows = T * TOP_K
    return pl.pallas_call(
        functools.partial(_dispatch_kernel, tile=tile, n_tiles=T // tile, n_rows=n_rows),
        out_shape=jax.ShapeDtypeStruct(((n_rows + ROW_BLOCK) * ROW_TILES, LANES), F32),
        grid=(T // tile,),
        in_specs=[pl.BlockSpec(memory_space=pl.ANY),
                  pl.BlockSpec((tile * ROW_TILES, LANES), lambda i: (i, 0))],
        out_specs=pl.BlockSpec(memory_space=pl.ANY),
        scratch_shapes=[pltpu.SMEM((2 * tile * TOP_K,), I32),
                        pltpu.VMEM((SUB_ROWS * ROW_TILES, LANES), F32),
                        pltpu.SemaphoreType.DMA((2,)),
                        pltpu.SemaphoreType.DMA,
                        pltpu.SemaphoreType.DMA],
        compiler_params=_cparams(("arbitrary",)),
        name="dispatch",
    )(dest_tiles, v)


def _moe_kernel(be_ref, ns_ref, nx_ref, xr_ref, x_hbm, w1_hbm, b1_ref, w2_hbm, b2_ref, y_ref,
                w1_stage, w2_stage, w1_scr, w2_scr, xbuf, x_scr, act_scr, sem_w, sem_x,
                *, n_blocks):
    i = pl.program_id(0)
    slot = lax.rem(i, 2)
    e = be_ref[i]
    nsub = ns_ref[i]
    changed = jnp.logical_or(i == 0, e != be_ref[jnp.maximum(i - 1, 0)])

    def x_copy(j, s):
        r0 = pl.multiple_of(xr_ref[j], ROW_TILES)
        return pltpu.make_async_copy(x_hbm.at[pl.ds(r0, ROW_BLOCK * ROW_TILES), :], xbuf.at[s],
                                     sem_x.at[s])

    @pl.when(i == 0)
    def _():
        x_copy(0, 0).start()

    @pl.when(nsub > 0)
    def _():
        x_copy(i, slot).wait()

    nxt_blk = jnp.minimum(i + 1, n_blocks - 1)

    @pl.when(jnp.logical_and(i + 1 < n_blocks, ns_ref[nxt_blk] > 0))
    def _():
        x_copy(nxt_blk, 1 - slot).start()

    def weight_copies(ex):
        return (pltpu.make_async_copy(w1_hbm.at[ex], w1_stage, sem_w.at[0]),
                pltpu.make_async_copy(w2_hbm.at[ex], w2_stage, sem_w.at[1]))

    @pl.when(i == 0)
    def _():
        for cp in weight_copies(e):
            cp.start()

    @pl.when(jnp.logical_and(changed, nsub > 0))
    def _():
        for cp in weight_copies(e):
            cp.wait()
        w1_scr[...] = w1_stage[...].astype(BF16)
        w2_scr[...] = w2_stage[...].astype(BF16)
        nxt = nx_ref[e]

        @pl.when(nxt >= 0)
        def _():
            for cp in weight_copies(nxt):
                cp.start()

    def compute(rows):
        x_scr[0:rows, :] = _load_tile_rows(xbuf.at[slot], rows).astype(BF16)
        cw = 256
        for c in range(D_FF // cw):
            glu = (_dot(x_scr[0:rows, :], w1_scr[:, c * cw:(c + 1) * cw])
                   + b1_ref[:, c * cw:(c + 1) * cw])
            lin = (_dot(x_scr[0:rows, :], w1_scr[:, D_FF + c * cw:D_FF + (c + 1) * cw])
                   + b1_ref[:, D_FF + c * cw:D_FF + (c + 1) * cw])
            glu = jnp.minimum(glu, SWIGLU_LIMIT)
            lin = jnp.clip(lin, -SWIGLU_LIMIT, SWIGLU_LIMIT)
            act = glu * jax.nn.sigmoid(SWIGLU_ALPHA * glu) * (lin + 1.0)
            act_scr[0:rows, c * cw:(c + 1) * cw] = act.astype(BF16)
        for c in range(D_MODEL // cw):
            y = (_dot(act_scr[0:rows, :], w2_scr[:, c * cw:(c + 1) * cw])
                 + b2_ref[:, c * cw:(c + 1) * cw])
            for s in range(cw // LANES):
                t = c * (cw // LANES) + s
                y_ref[pl.ds(t, rows, stride=ROW_TILES), :] = y[:, s * LANES:(s + 1) * LANES]
        if rows < ROW_BLOCK:
            y_ref[rows * ROW_TILES:, :] = jnp.zeros(((ROW_BLOCK - rows) * ROW_TILES, LANES), F32)

    for m in range(1, ROW_BLOCK // SUB_ROWS + 1):
        @pl.when(nsub == m)
        def _(m=m):
            compute(m * SUB_ROWS)

    @pl.when(nsub == 0)
    def _():
        y_ref[...] = jnp.zeros_like(y_ref)


def _moe(block_e, nsub, next_e, xrow, xs, w1, b1, w2, b2, *, n_blocks):
    return pl.pallas_call(
        functools.partial(_moe_kernel, n_blocks=n_blocks),
        out_shape=jax.ShapeDtypeStruct((n_blocks * ROW_BLOCK * ROW_TILES, LANES), F32),
        grid_spec=pltpu.PrefetchScalarGridSpec(
            num_scalar_prefetch=4,
            grid=(n_blocks,),
            in_specs=[pl.BlockSpec(memory_space=pl.ANY),
                      pl.BlockSpec(memory_space=pl.ANY),
                      pl.BlockSpec((None, 1, 2 * D_FF), lambda i, be, ns, nx, xr: (be[i], 0, 0)),
                      pl.BlockSpec(memory_space=pl.ANY),
                      pl.BlockSpec((None, 1, D_MODEL), lambda i, be, ns, nx, xr: (be[i], 0, 0))],
            out_specs=pl.BlockSpec((ROW_BLOCK * ROW_TILES, LANES),
                                   lambda i, be, ns, nx, xr: (i, 0)),
            scratch_shapes=[pltpu.VMEM((D_MODEL, 2 * D_FF), F32),
                            pltpu.VMEM((D_FF, D_MODEL), F32),
                            pltpu.VMEM((D_MODEL, 2 * D_FF), BF16),
                            pltpu.VMEM((D_FF, D_MODEL), BF16),
                            pltpu.VMEM((2, ROW_BLOCK * ROW_TILES, LANES), F32),
                            pltpu.VMEM((ROW_BLOCK, D_MODEL), BF16),
                            pltpu.VMEM((ROW_BLOCK, D_FF), BF16),
                            pltpu.SemaphoreType.DMA((2,)),
                            pltpu.SemaphoreType.DMA((2,))]),
        compiler_params=_cparams(("arbitrary",)),
        name="moe_ffn",
    )(block_e, nsub, next_e, xrow, xs, w1, b1, w2, b2)


def _combine_kernel(pos_hbm, src_hbm, yb_hbm, gate_ref, h1_ref, g2_ref, nw_ref, o_ref,
                    pos_smem, src_smem, win, hbuf, sem_tab, sem_win, *, n_tiles):
    i = pl.program_id(0)
    slot = lax.rem(i, 2)
    tile = COMBINE_TILE
    n_asg = tile * TOP_K
    chunk = WIN_ROWS * ROW_TILES
    slot_rows = WIN_CHUNKS * chunk

    def table_copies(j, s):
        return (pltpu.make_async_copy(pos_hbm.at[j], pos_smem.at[pl.ds(s * n_asg, n_asg)],
                                      sem_tab.at[s]),
                pltpu.make_async_copy(src_hbm.at[j], src_smem.at[pl.ds(s * n_asg, n_asg)],
                                      sem_tab.at[s]))

    def fetch_window(s):
        def one(c, carry):
            src = pl.multiple_of(src_smem[s * n_asg + c], ROW_TILES)
            dst = pl.multiple_of(s * slot_rows + c * chunk, chunk)
            pltpu.make_async_copy(yb_hbm.at[pl.ds(src, chunk), :], win.at[pl.ds(dst, chunk), :],
                                  sem_win.at[s]).start()
            return carry
        lax.fori_loop(0, WIN_CHUNKS, one, 0, unroll=2)

    @pl.when(i == 0)
    def _():
        for cp in table_copies(0, 0):
            cp.start()
        for cp in table_copies(0, 0):
            cp.wait()
        fetch_window(0)
        if n_tiles > 1:
            for cp in table_copies(1, 1):
                cp.start()

    @pl.when(i + 1 < n_tiles)
    def _():
        for cp in table_copies(i + 1, 1 - slot):
            cp.wait()
        fetch_window(1 - slot)

    pltpu.make_async_copy(yb_hbm.at[pl.ds(0, slot_rows), :],
                          win.at[pl.ds(pl.multiple_of(slot * slot_rows, slot_rows), slot_rows), :],
                          sem_win.at[slot]).wait()

    g2 = g2_ref[...]
    base = slot * n_asg

    def token(t, carry):
        e0 = base + t * TOP_K
        acc = None
        for k in range(TOP_K):
            p = pl.multiple_of(pos_smem[e0 + k], ROW_TILES)
            term = gate_ref[pl.ds(t * TOP_K + k, 1), :] * win[pl.ds(p, ROW_TILES), :]
            acc = term if acc is None else acc + term
        r0 = pl.multiple_of(t * ROW_TILES, ROW_TILES)
        hbuf[pl.ds(r0, ROW_TILES), :] = h1_ref[pl.ds(r0, ROW_TILES), :] + g2 * acc
        return carry
    lax.fori_loop(0, tile, token, 0, unroll=4)

    @pl.when(i + 2 < n_tiles)
    def _():
        for cp in table_copies(i + 2, slot):
            cp.start()

    h = _load_tile_rows(hbuf, tile)
    ms = jnp.mean(h * h, axis=-1, keepdims=True)
    o_ref[...] = h * lax.rsqrt(ms + EPS) * nw_ref[...]


def _combine(pos_tiles, src_tiles, yb, gate_rows, h1, g2_tiles, nw, *, tiles_per_batch):
    tile = COMBINE_TILE
    T = h1.shape[0] // ROW_TILES
    n_asg = tile * TOP_K
    return pl.pallas_call(
        functools.partial(_combine_kernel, n_tiles=T // tile),
        out_shape=jax.ShapeDtypeStruct((T, D_MODEL), F32),
        grid=(T // tile,),
        in_specs=[pl.BlockSpec(memory_space=pl.ANY),
                  pl.BlockSpec(memory_space=pl.ANY),
                  pl.BlockSpec(memory_space=pl.ANY),
                  pl.BlockSpec((tile * TOP_K, LANES), lambda i: (i, 0)),
                  pl.BlockSpec((tile * ROW_TILES, LANES), lambda i: (i, 0)),
                  pl.BlockSpec((None, ROW_TILES, LANES), lambda i: (i // tiles_per_batch, 0, 0)),
                  pl.BlockSpec((1, D_MODEL), lambda i: (0, 0))],
        out_specs=pl.BlockSpec((tile, D_MODEL), lambda i: (i, 0)),
        scratch_shapes=[pltpu.SMEM((2 * n_asg,), I32),
                        pltpu.SMEM((2 * n_asg,), I32),
                        pltpu.VMEM((2 * WIN_CHUNKS * WIN_ROWS * ROW_TILES, LANES), F32),
                        pltpu.VMEM((tile * ROW_TILES, LANES), F32),
                        pltpu.SemaphoreType.DMA((2,)),
                        pltpu.SemaphoreType.DMA((2,))],
        compiler_params=_cparams(("arbitrary",)),
        name="combine_norm",
    )(pos_tiles, src_tiles, yb, gate_rows, h1, g2_tiles, nw)


def _rope_tables(n):
    rows = n // GRID_W
    row = jnp.repeat(jnp.arange(rows, dtype=F32), GRID_W)
    col = jnp.tile(jnp.arange(GRID_W, dtype=F32), rows)
    n_freq = RET_DK // 4
    inv = ROPE_BASE ** (-jnp.arange(n_freq, dtype=F32) / n_freq)
    ang = jnp.concatenate([row[:, None] * inv, col[:, None] * inv], axis=-1)
    cos, sin = jnp.cos(ang), jnp.sin(ang)
    cos_h = jnp.concatenate([cos, cos], axis=-1)
    sin_h = jnp.concatenate([-sin, sin], axis=-1)
    return jnp.tile(cos_h, (1, RET_HEADS)), jnp.tile(sin_h, (1, RET_HEADS))


def kernel(x, c, ctx, c_ctx, w_ada, b_ada, norm_mix_w, norm_ffn_w, w_in, w_out, hg_lb,
           hg_norm_w, router_w, router_b, w1, b1, w2, b2, norm_final_w):
    B, N, D = x.shape
    C = ctx.shape[1]
    T = B * N
    assert D == D_MODEL and w_ada.shape[0] == 1

    cc = jnp.concatenate([c.astype(F32), c_ctx.astype(F32)[None, :],
                          jnp.zeros((16 - B - 1, D), F32)], axis=0)
    mod = _ada(cc, w_ada[0], b_ada[0][None, :])
    mod = mod.reshape(16, 6, 1, D).transpose(1, 0, 2, 3)
    sh1, sc1, g1, sh2, sc2, g2 = (mod[i] for i in range(6))

    w_in_bf = w_in[0].astype(BF16)
    w_out_bf = w_out[0].astype(BF16)
    nw_mix = norm_mix_w[0][None, :]
    cos_t, sin_t = _rope_tables(N)

    pb, lf = _inproj(x, sc1, sh1, None, nw_mix, w_in_bf, hg_lb[:2], cos_t, sin_t,
                     rope=True, tm=1024)
    pbc, lfc = _inproj(ctx, sc1, sh1, B, nw_mix, w_in_bf, hg_lb[:2], cos_t[:C], sin_t[:C],
                       rope=False, tm=C)

    a_ret = _retention(pb, pbc)
    a_hg = _hgrn(pb, lf, pbc, lfc, hg_norm_w[0][None, :])

    rw_t = router_w[0].T
    rwh = rw_t.astype(BF16)
    rwl = (rw_t - rwh.astype(F32)).astype(BF16)
    rb = jnp.broadcast_to(router_b[0][:, None], (N_EXPERTS, LANES))
    h1, v, route_t, gate_rows, base_cnt, cnt = _outproj(a_ret, a_hg, w_out_bf, x, g1, sc2, sh2,
                                                        norm_ffn_w[0][None, :], rwh, rwl, rb, tm=512)

    idx = route_t[0:TOP_K].astype(I32)
    rank = route_t[2 * TOP_K:3 * TOP_K].astype(I32)
    counts = cnt[:, 0].astype(I32)
    padded = (counts + ROW_BLOCK - 1) // ROW_BLOCK * ROW_BLOCK
    pend = jnp.cumsum(padded)
    pstart = pend - padded
    cstart = jnp.cumsum(counts) - counts
    onehot = idx[None] == jnp.arange(N_EXPERTS, dtype=I32)[:, None, None]

    def dest_tiles(first_row, tile):
        dest = (jnp.sum(jnp.where(onehot, first_row[:, None, None], 0), axis=0) + rank) * ROW_TILES
        return dest.T.reshape(T // tile, tile * TOP_K)
    n_blocks = (T * TOP_K) // ROW_BLOCK + N_EXPERTS
    starts = jnp.arange(n_blocks, dtype=I32) * ROW_BLOCK
    block_e = jnp.minimum(jnp.sum((pend[None, :] <= starts[:, None]).astype(I32), axis=1),
                          N_EXPERTS - 1)
    valid = jnp.clip(counts[block_e] - (starts - pstart[block_e]), 0, ROW_BLOCK)
    nsub = (valid + SUB_ROWS - 1) // SUB_ROWS
    first_from = lax.cummin(jnp.where(counts > 0, jnp.arange(N_EXPERTS, dtype=I32), N_EXPERTS),
                            reverse=True)
    next_e = jnp.concatenate([first_from[1:], jnp.full((1,), N_EXPERTS, I32)])
    next_e = jnp.where(next_e == N_EXPERTS, -1, next_e)
    xrow = jnp.where(nsub > 0, cstart[block_e] + starts - pstart[block_e], 0) * ROW_TILES

    tile_d = 512
    xs = _dispatch(dest_tiles(cstart, tile_d), v, tile=tile_d)
    yb = _moe(block_e, nsub, next_e, xrow.astype(I32), xs, w1[0], b1[0][:, None, :], w2[0],
              b2[0][:, None, :], n_blocks=n_blocks)

    tc = COMBINE_TILE
    n_ct = T // tc
    base = base_cnt.reshape(n_ct, N_EXPERTS, LANES)[:, :, 0].astype(I32)
    n_run = jnp.concatenate([base[1:], counts[None, :]], axis=0) - base
    n_chunk = (n_run + WIN_ROWS - 1) // WIN_ROWS
    chunk_end = jnp.cumsum(n_chunk, axis=1)
    chunk_start = chunk_end - n_chunk
    win_off = chunk_start * WIN_ROWS
    c_id = jnp.arange(WIN_CHUNKS, dtype=I32)
    c_exp = jnp.sum((chunk_end[:, None, :] <= c_id[None, :, None]).astype(I32), axis=2)
    c_real = c_exp < N_EXPERTS
    c_own = c_exp[:, :, None] == jnp.arange(N_EXPERTS, dtype=I32)
    take = lambda tab: jnp.sum(jnp.where(c_own, tab[:, None, :], 0), axis=2)
    c_src = (take(pstart[None, :] + base) + (c_id[None, :] - take(chunk_start)) * WIN_ROWS)
    src_tiles = jnp.concatenate(
        [jnp.where(c_real, c_src, 0) * ROW_TILES,
         jnp.zeros((n_ct, tc * TOP_K - WIN_CHUNKS), I32)], axis=1)
    slot_off = (jnp.arange(n_ct, dtype=I32) % 2) * (WIN_CHUNKS * WIN_ROWS)
    shift = jnp.repeat(win_off - base + slot_off[:, None], tc, axis=0).T
    pos = (jnp.sum(jnp.where(onehot, shift[:, None, :], 0), axis=0) + rank) * ROW_TILES

    def per_tile(a):
        return a.reshape(TOP_K, n_ct, tc).transpose(1, 2, 0).reshape(n_ct, tc * TOP_K)

    out = _combine(per_tile(pos), src_tiles, yb, gate_rows, h1,
                   g2.reshape(16, ROW_TILES, LANES), norm_final_w[None, :],
                   tiles_per_batch=N // tc)
    return out.reshape(B, N, D)
```

```python
import functools
import math

import jax
import jax.numpy as jnp
import numpy as np
from jax import lax
from jax.experimental import pallas as pl
from jax.experimental.pallas import tpu as pltpu

F32 = jnp.float32
BF16 = jnp.bfloat16
I32 = jnp.int32

D_MODEL = 1024
GRID_W = 64
RET_HEADS = 4
RET_DK = 64
HG_HEADS = 4
PROJ_W = 4096
ROPE_BASE = 10000.0
EPS = 1e-6
N_EXPERTS = 32
TOP_K = 4
D_FF = 1024
SWIGLU_LIMIT = 7.0
SWIGLU_ALPHA = 1.702
LOG2_E = 1.0 / math.log(2.0)

LANES = 128
CHUNK = 128
ROW_BLOCK = 1024
SUB_ROWS = 256
COMBINE_TILE = 512
WIN_ROWS = 16
WIN_CHUNKS = (COMBINE_TILE * TOP_K + N_EXPERTS * (WIN_ROWS - 1) + WIN_ROWS - 1) // WIN_ROWS
VMEM_LIMIT = 56 * 1024 * 1024
ROW_TILES = D_MODEL // LANES

C_RQ, C_RK, C_RV, C_RG, C_HQ, C_FF, C_FB, C_HV, C_HG = (
    0, 256, 512, 1024, 1536, 2048, 2560, 3072, 3584)


def _cparams(sem):
    return pltpu.CompilerParams(dimension_semantics=sem, vmem_limit_bytes=VMEM_LIMIT)


def _split_bf16(x):
    hi = x.astype(BF16)
    lo = (x - hi.astype(F32)).astype(BF16)
    return hi, lo


def _dot(a, b):
    return jnp.dot(a, b, preferred_element_type=F32)


def _dot_nt(a, b):
    return lax.dot_general(a, b, (((1,), (1,)), ((), ())), preferred_element_type=F32)


def _dot3(a, b):
    ah, al = _split_bf16(a)
    bh, bl = _split_bf16(b)
    return _dot(ah, bh) + (_dot(ah, bl) + _dot(al, bh))


def _silu(x):
    return x * jax.nn.sigmoid(x)


def _load_tile_rows(ref, n):
    return jnp.concatenate([ref[pl.ds(s, n, stride=ROW_TILES), :] for s in range(ROW_TILES)],
                           axis=1)


def _store_tile_rows(ref, x):
    n = x.shape[0]
    for s in range(ROW_TILES):
        ref[pl.ds(s, n, stride=ROW_TILES), :] = x[:, s * LANES:(s + 1) * LANES]


def _ada_kernel(c_ref, w_ref, b_ref, o_ref):
    s = _silu(c_ref[...])
    o_ref[...] = _dot3(s, w_ref[...]) + b_ref[...]


def _ada(cc, w, b):
    nblk = w.shape[1] // D_MODEL
    return pl.pallas_call(
        _ada_kernel,
        out_shape=jax.ShapeDtypeStruct((cc.shape[0], w.shape[1]), F32),
        grid=(nblk,),
        in_specs=[pl.BlockSpec(cc.shape, lambda j: (0, 0)),
                  pl.BlockSpec((D_MODEL, D_MODEL), lambda j: (0, j)),
                  pl.BlockSpec((1, D_MODEL), lambda j: (0, j))],
        out_specs=pl.BlockSpec((cc.shape[0], D_MODEL), lambda j: (0, j)),
        compiler_params=_cparams(("arbitrary",)),
        name="ada",
    )(cc, w, b)


def _inproj_kernel(x_ref, sc_ref, sh_ref, nw_ref, w_ref, lb_ref, cos_ref, sin_ref,
                   pb_ref, lf_ref, u_scr, *, rope):
    x = x_ref[...]
    ms = jnp.mean(x * x, axis=-1, keepdims=True)
    u = x * lax.rsqrt(ms + EPS) * nw_ref[...] * (1.0 + sc_ref[...]) + sh_ref[...]
    u_scr[...] = u.astype(BF16)

    def proj(lo, width):
        return _dot(u_scr[...], w_ref[:, lo:lo + width])

    tm = x.shape[0]
    if rope:
        lane = lax.broadcasted_iota(I32, (tm, LANES), 1)
        first = (lane & 32) == 0

    def put_rot(col, scale):
        for j in range(2):
            lo = col + j * LANES
            t = proj(lo, LANES)
            if scale != 1.0:
                t = t * scale
            if rope:
                tb = j * LANES
                rot = jnp.where(first, pltpu.roll(t, 96, axis=1), pltpu.roll(t, 32, axis=1))
                t = t * cos_ref[:, tb:tb + LANES] + rot * sin_ref[:, tb:tb + LANES]
            pb_ref[:, lo:lo + LANES] = t.astype(BF16)

    put_rot(C_RQ, 1.0)
    put_rot(C_RK, RET_DK ** -0.5)
    pb_ref[:, C_RV:C_RV + 512] = proj(C_RV, 512).astype(BF16)
    pb_ref[:, C_RG:C_RG + 512] = _silu(proj(C_RG, 512)).astype(BF16)
    pb_ref[:, C_HQ:C_HQ + 512] = _silu(proj(C_HQ, 512)).astype(BF16)
    pb_ref[:, C_HV:C_HV + 512] = proj(C_HV, 512).astype(BF16)
    pb_ref[:, C_HG:C_HG + 512] = _silu(proj(C_HG, 512)).astype(BF16)

    la = lb_ref[0]
    lbb = lb_ref[1]
    mx = jnp.maximum(la, lbb)
    ea = jnp.exp(la - mx)
    eb = jnp.exp(lbb - mx)
    lb = ea / (ea + eb)
    for d, col in enumerate((C_FF, C_FB)):
        lbd = lb[d:d + 1, :]
        f = lbd + (1.0 - lbd) * jax.nn.sigmoid(proj(col, 512))
        pb_ref[:, col:col + 512] = (1.0 - f).astype(BF16)
        lf_ref[:, d * 512:(d + 1) * 512] = jnp.log(f) * LOG2_E


def _inproj(x, sc, sh, mod_row, nw, w_bf, hg_lb, cos_t, sin_t, *, rope, tm):
    B, n, _ = x.shape
    nt = n // tm
    if mod_row is None:
        mrow = lambda b, j: (b, 0, 0)
    else:
        mrow = lambda b, j: (mod_row, 0, 0)
    return pl.pallas_call(
        functools.partial(_inproj_kernel, rope=rope),
        out_shape=(jax.ShapeDtypeStruct((B, n, PROJ_W), BF16),
                   jax.ShapeDtypeStruct((B, n, 1024), F32)),
        grid=(B, nt),
        in_specs=[pl.BlockSpec((None, tm, D_MODEL), lambda b, j: (b, j, 0)),
                  pl.BlockSpec((None, 1, D_MODEL), mrow),
                  pl.BlockSpec((None, 1, D_MODEL), mrow),
                  pl.BlockSpec((1, D_MODEL), lambda b, j: (0, 0)),
                  pl.BlockSpec((D_MODEL, PROJ_W), lambda b, j: (0, 0),
                               pipeline_mode=pl.Buffered(1)),
                  pl.BlockSpec((2, 2, 512), lambda b, j: (0, 0, 0)),
                  pl.BlockSpec((tm, 256), lambda b, j: (j, 0)),
                  pl.BlockSpec((tm, 256), lambda b, j: (j, 0))],
        out_specs=(pl.BlockSpec((None, tm, PROJ_W), lambda b, j: (b, j, 0)),
                   pl.BlockSpec((None, tm, 1024), lambda b, j: (b, j, 0))),
        scratch_shapes=[pltpu.VMEM((tm, D_MODEL), BF16)],
        compiler_params=_cparams(("arbitrary", "arbitrary")),
        name="inproj_rope" if rope else "inproj_ctx",
    )(x, sc, sh, nw, w_bf, hg_lb, cos_t, sin_t)


_RET_LGF = [math.log1p(-(2.0 ** (-5.0 - 2.0 * h))) for h in range(RET_HEADS)]
_RET_LGB = [math.log1p(-(2.0 ** (-6.0 - 2.0 * h))) for h in range(RET_HEADS)]


def _ret_kernel(q_ref, k_ref, v_ref, g_ref, kc_ref, vc_ref, o_ref,
                u_scr, sin_scr, kt_scr, dtot_scr, *, n_lat, n_ctx):
    L = CHUNK
    pair = pl.program_id(1)
    row = lax.broadcasted_iota(I32, (L, L), 0).astype(F32)
    col = lax.broadcasted_iota(I32, (L, L), 1).astype(F32)
    lane = lax.broadcasted_iota(I32, (L, LANES), 1)
    trow = lax.broadcasted_iota(I32, (L, 1), 0).astype(F32)
    tcol = lax.broadcasted_iota(I32, (1, L), 1).astype(F32)
    low_half = lane < RET_DK

    def u_chunk(k_blk, v_blk, ci, store_kt):
        kt = k_blk.astype(F32).T
        if store_kt is not None:
            kt_scr[store_kt] = kt.astype(BF16)
        for hh in range(2):
            lgf = jnp.where(pair == 0, _RET_LGF[hh], _RET_LGF[2 + hh])
            lgb = jnp.where(pair == 0, _RET_LGB[hh], _RET_LGB[2 + hh])
            kth = kt[hh * RET_DK:(hh + 1) * RET_DK, :]
            wkf = jnp.exp(lgf * (L - 1.0 - tcol))
            wkb = jnp.exp(lgb * tcol)
            lhs = jnp.concatenate([kth * wkf, kth * wkb], axis=0).astype(BF16)
            u_scr[hh, ci] = _dot(lhs, v_blk[:, hh * LANES:(hh + 1) * LANES])

    for c in range(n_ctx):
        u_chunk(kc_ref[c * L:(c + 1) * L, :], vc_ref[c * L:(c + 1) * L, :], c, None)

    def lat_u(c, carry):
        r0 = pl.multiple_of(c * L, L)
        u_chunk(k_ref[pl.ds(r0, L), :], v_ref[pl.ds(r0, L), :], n_ctx + c, c)
        return carry
    lax.fori_loop(0, n_lat, lat_u, 0, unroll=4)

    ones = jnp.ones((RET_DK, LANES), F32)
    for hh in range(2):
        lgf = jnp.where(pair == 0, _RET_LGF[hh], _RET_LGF[2 + hh])
        lgb = jnp.where(pair == 0, _RET_LGB[hh], _RET_LGB[2 + hh])
        d = row - col
        dtot_scr[hh] = jnp.where(d > 0, jnp.exp(lgf * jnp.maximum(d, 0.0)),
                                 jnp.where(d < 0, jnp.exp(lgb * jnp.maximum(-d, 0.0)), 2.0))
        af = jnp.exp(ones * (lgf * L))
        ab = jnp.exp(ones * (lgb * L))

        s = jnp.zeros((RET_DK, LANES), F32)
        for c in range(n_ctx):
            s = af * s + u_scr[hh, c, 0:RET_DK, :]
        sb = jnp.zeros((RET_DK, LANES), F32)
        for c in reversed(range(n_ctx)):
            sb = ab * sb + u_scr[hh, c, RET_DK:2 * RET_DK, :]

        def fwd(c, s, hh=hh, af=af):
            sin_scr[hh, c, 0:RET_DK, :] = s.astype(BF16)
            return af * s + u_scr[hh, n_ctx + c, 0:RET_DK, :]
        lax.fori_loop(0, n_lat, fwd, s)

        def bwd(i, sb, hh=hh, ab=ab):
            c = n_lat - 1 - i
            sin_scr[hh, c, RET_DK:2 * RET_DK, :] = sb.astype(BF16)
            return ab * sb + u_scr[hh, n_ctx + c, RET_DK:2 * RET_DK, :]
        lax.fori_loop(0, n_lat, bwd, sb)

    def out_chunk(c, carry):
        r0 = pl.multiple_of(c * L, L)
        q = q_ref[pl.ds(r0, L), :].astype(F32)
        qr = pltpu.roll(q, RET_DK, axis=1)
        kt = kt_scr[c]
        for hh in range(2):
            lgf = jnp.where(pair == 0, _RET_LGF[hh], _RET_LGF[2 + hh])
            lgb = jnp.where(pair == 0, _RET_LGB[hh], _RET_LGB[2 + hh])
            mine = low_half if hh == 0 else jnp.logical_not(low_half)
            qm = jnp.where(mine, q, 0.0).astype(BF16)
            p = (_dot(qm, kt) * dtot_scr[hh]).astype(BF16)
            vh = v_ref[pl.ds(r0, L), hh * LANES:(hh + 1) * LANES]
            wqf = jnp.exp(lgf * (trow + 1.0))
            wqb = jnp.exp(lgb * (L - trow))
            qa, qb = (q, qr) if hh == 0 else (qr, q)
            qs = jnp.where(low_half, qa * wqf, qb * wqb).astype(BF16)
            o = _dot(p, vh) + _dot(qs, sin_scr[hh, c])
            ms = jnp.mean(o * o, axis=-1, keepdims=True)
            gh = g_ref[pl.ds(r0, L), hh * LANES:(hh + 1) * LANES].astype(F32)
            o_ref[pl.ds(r0, L), hh * LANES:(hh + 1) * LANES] = (
                o * lax.rsqrt(ms + EPS) * gh).astype(BF16)
        return carry
    lax.fori_loop(0, n_lat, out_chunk, 0, unroll=4)


def _retention(pb, pbc):
    B, n, _ = pb.shape
    nc = pbc.shape[1]
    n_lat, n_ctx = n // CHUNK, nc // CHUNK
    return pl.pallas_call(
        functools.partial(_ret_kernel, n_lat=n_lat, n_ctx=n_ctx),
        out_shape=jax.ShapeDtypeStruct((B, n, 512), BF16),
        grid=(B, 2),
        in_specs=[pl.BlockSpec((None, n, LANES), lambda b, p: (b, 0, C_RQ // LANES + p)),
                  pl.BlockSpec((None, n, LANES), lambda b, p: (b, 0, C_RK // LANES + p)),
                  pl.BlockSpec((None, n, 256), lambda b, p: (b, 0, C_RV // 256 + p)),
                  pl.BlockSpec((None, n, 256), lambda b, p: (b, 0, C_RG // 256 + p)),
                  pl.BlockSpec((None, nc, LANES), lambda b, p: (b, 0, C_RK // LANES + p)),
                  pl.BlockSpec((None, nc, 256), lambda b, p: (b, 0, C_RV // 256 + p))],
        out_specs=pl.BlockSpec((None, n, 256), lambda b, p: (b, 0, p)),
        scratch_shapes=[pltpu.VMEM((2, n_lat + n_ctx, CHUNK, LANES), F32),
                        pltpu.VMEM((2, n_lat, CHUNK, LANES), BF16),
                        pltpu.VMEM((n_lat, LANES, CHUNK), BF16),
                        pltpu.VMEM((2, CHUNK, CHUNK), F32)],
        compiler_params=_cparams(("arbitrary", "arbitrary")),
        name="retention",
    )(pb, pb, pb, pb, pbc, pbc)


_LEVELS = (64, 32, 16, 8, 4, 2, 1)


def _expand_rows(r, rep):
    n = r.shape[0]
    if n == 1:
        return jnp.broadcast_to(r, (rep, r.shape[1]))
    return jnp.concatenate(
        [jnp.broadcast_to(r[i:i + 1, :], (rep, r.shape[1])) for i in range(n)], axis=0)


def _hgrn_kernel(q_ref, kf_ref, kb_ref, v_ref, g_ref, lff_ref, lfb_ref,
                 kfc_ref, kbc_ref, vc_ref, lffc_ref, lfbc_ref, nw_ref, o_ref,
                 ut_scr, a_scr, qs_scr, oi_scr, sin_scr, bfb_scr, *, n_lat, n_ctx):
    L = CHUNK
    row = lax.broadcasted_iota(I32, (L, L), 0)
    col = lax.broadcasted_iota(I32, (L, L), 1)
    xr_bits = lax.bitcast_convert_type((row ^ col).astype(F32), I32)
    lv = lax.shift_right_logical(xr_bits, 23) - 127
    row2 = lax.broadcasted_iota(I32, (L, 2 * L), 0)
    col2 = lax.broadcasted_iota(I32, (L, 2 * L), 1) & (L - 1)
    tril2 = jnp.where(col2 <= row2, 1.0, 0.0).astype(BF16)
    triu2 = jnp.where(col2 >= row2, 1.0, 0.0).astype(BF16)

    def cums(lff, lfb):
        hf, lof = _split_bf16(lff)
        hb, lob = _split_bf16(lfb)
        bf = _dot(tril2, jnp.concatenate([hf, lof], axis=0))
        bb = _dot(triu2, jnp.concatenate([hb, lob], axis=0))
        return bf, bb

    def state_part(ci, kf, kb, v_blk, bf, bb):
        endf = bf[L - 1:L, :]
        endb = bb[0:1, :]
        ksf = kf * jnp.exp2(endf - bf)
        ksb = kb * jnp.exp2(endb - bb)
        vt = v_blk.astype(F32).T.astype(BF16)
        ut_scr[ci] = _dot(vt, jnp.concatenate([ksf, ksb], axis=1).astype(BF16))
        a_scr[ci] = jnp.broadcast_to(
            jnp.concatenate([jnp.exp2(endf), jnp.exp2(endb)], axis=1), (8, 2 * LANES))

    for c in range(n_ctx):
        sl = slice(c * L, (c + 1) * L)
        bf, bb = cums(lffc_ref[sl, :], lfbc_ref[sl, :])
        state_part(c, kfc_ref[sl, :].astype(F32), kbc_ref[sl, :].astype(F32),
                   vc_ref[sl, :], bf, bb)

    def lat_chunk(c, u):
        rows = pl.ds(pl.multiple_of(c * L, L), L)
        bf_scr = bfb_scr.at[u, 0]
        bb_scr = bfb_scr.at[u, 1]
        q = q_ref[rows, :].astype(F32)
        kf = kf_ref[rows, :].astype(F32)
        kb = kb_ref[rows, :].astype(F32)
        v_blk = v_ref[rows, :]
        lff = lff_ref[rows, :]
        lfb = lfb_ref[rows, :]
        bf, bb = cums(lff, lfb)
        state_part(n_ctx + c, kf, kb, v_blk, bf, bb)
        qs_scr[rows, :] = jnp.concatenate([q * jnp.exp2(bf), q * jnp.exp2(bb)],
                                          axis=1).astype(BF16)
        bf_scr[...] = bf
        bb_scr[...] = bb

        acc = jnp.zeros((L, L), F32)
        for lvl, h in enumerate(_LEVELS):
            bit = (row & h) != 0
            ksel = jnp.where(bit, kb, kf)
            if h >= 4:
                n = (L // 2) // h
                if n == 1:
                    rf = bf_scr[h - 1:h, :]
                    rb = bb_scr[h:h + 1, :]
                else:
                    rf = bf_scr[pl.ds(h - 1, n, stride=2 * h), :]
                    rb = bb_scr[pl.ds(h, n, stride=2 * h), :]
                df = bf - _expand_rows(rf, 2 * h)
                db = bb - _expand_rows(rb, 2 * h)
                eq = jnp.where(bit, df, db)
                ek = -jnp.where(bit, db, df)
            elif h == 2:
                m = row & 3
                lff_n = pltpu.roll(lff, L - 1, axis=0)
                lfb_n = pltpu.roll(lfb, L - 1, axis=0)
                eq = jnp.where(m == 2, lff,
                               jnp.where(m == 3, lff + pltpu.roll(lff, 1, axis=0),
                                         jnp.where(m == 0, lfb + lfb_n, lfb)))
                ek = jnp.where(m == 3, pltpu.roll(lfb, 1, axis=0),
                               jnp.where(m == 0, lff_n, 0.0))
            else:
                eq = jnp.where(bit, lff, lfb)
                ek = None
            lhs = (q * jnp.exp2(eq)).astype(BF16)
            rhs = (ksel if ek is None else ksel * jnp.exp2(ek)).astype(BF16)
            acc = jnp.where(lv == 6 - lvl, _dot_nt(lhs, rhs), acc)

        dsum = jnp.sum(q * (kf + kb), axis=-1, keepdims=True)
        oi_scr[rows, :] = _dot(acc.astype(BF16), v_blk) + dsum * v_blk.astype(F32)

    def lat_pair(i, carry):
        for u in range(4):
            lat_chunk(4 * i + u, u)
        return carry
    lax.fori_loop(0, n_lat // 4, lat_pair, 0)

    st = jnp.zeros((LANES, LANES), F32)
    for c in range(n_ctx):
        st = st * a_scr[c, 0:1, 0:LANES] + ut_scr[c, :, 0:LANES]
    stb = jnp.zeros((LANES, LANES), F32)
    for c in reversed(range(n_ctx)):
        stb = stb * a_scr[c, 0:1, LANES:2 * LANES] + ut_scr[c, :, LANES:2 * LANES]

    def fwd(c, st):
        sin_scr[c, :, 0:LANES] = st.astype(BF16)
        ci = n_ctx + c
        return st * a_scr[ci, 0:1, 0:LANES] + ut_scr[ci, :, 0:LANES]
    lax.fori_loop(0, n_lat, fwd, st)

    def bwd(i, stb):
        c = n_lat - 1 - i
        sin_scr[c, :, LANES:2 * LANES] = stb.astype(BF16)
        ci = n_ctx + c
        return stb * a_scr[ci, 0:1, LANES:2 * LANES] + ut_scr[ci, :, LANES:2 * LANES]
    lax.fori_loop(0, n_lat, bwd, stb)

    def out_chunk(c, carry):
        rows = pl.ds(pl.multiple_of(c * L, L), L)
        o = oi_scr[rows, :] + _dot_nt(qs_scr[rows, :], sin_scr[c])
        ms = jnp.mean(o * o, axis=-1, keepdims=True)
        y = o * lax.rsqrt(ms + EPS) * nw_ref[...] * g_ref[rows, :].astype(F32)
        o_ref[rows, :] = y.astype(BF16)
        return carry
    lax.fori_loop(0, n_lat, out_chunk, 0, unroll=8)


def _hgrn(pb, lf, pbc, lfc, nw):
    B, n, _ = pb.shape
    nc = pbc.shape[1]
    n_lat, n_ctx = n // CHUNK, nc // CHUNK

    def colblk(rows, col0):
        return pl.BlockSpec((None, rows, LANES), lambda b, h: (b, 0, col0 // LANES + h))

    return pl.pallas_call(
        functools.partial(_hgrn_kernel, n_lat=n_lat, n_ctx=n_ctx),
        out_shape=jax.ShapeDtypeStruct((B, n, 512), BF16),
        grid=(B, HG_HEADS),
        in_specs=[colblk(n, C_HQ), colblk(n, C_FF), colblk(n, C_FB), colblk(n, C_HV),
                  colblk(n, C_HG), colblk(n, 0), colblk(n, 512),
                  colblk(nc, C_FF), colblk(nc, C_FB), colblk(nc, C_HV),
                  colblk(nc, 0), colblk(nc, 512),
                  pl.BlockSpec((1, LANES), lambda b, h: (0, 0))],
        out_specs=pl.BlockSpec((None, n, LANES), lambda b, h: (b, 0, h)),
        scratch_shapes=[pltpu.VMEM((n_lat + n_ctx, LANES, 2 * LANES), F32),
                        pltpu.VMEM((n_lat + n_ctx, 8, 2 * LANES), F32),
                        pltpu.VMEM((n, 2 * LANES), BF16),
                        pltpu.VMEM((n, LANES), F32),
                        pltpu.VMEM((n_lat, LANES, 2 * LANES), BF16),
                        pltpu.VMEM((4, 2, CHUNK, LANES), F32)],
        compiler_params=_cparams(("arbitrary", "arbitrary")),
        name="hgrn2",
    )(pb, pb, pb, pb, pb, lf, lf, pbc, pbc, pbc, lfc, lfc, nw)


def _outproj_kernel(ar_ref, ah_ref, w_ref, x_ref, g1_ref, sc_ref, sh_ref, nw_ref,
                    rwh_ref, rwl_ref, rb_ref, tri_ref, h1_ref, v_ref, route_t_ref, gate_ref, base_ref,
                    cnt_ref, cnt_scr):
    first_step = jnp.logical_and(pl.program_id(0) == 0, pl.program_id(1) == 0)

    @pl.when(first_step)
    def _():
        cnt_scr[...] = jnp.zeros_like(cnt_scr)

    y = _dot(ar_ref[...], w_ref[0:512, :]) + _dot(ah_ref[...], w_ref[512:1024, :])
    h1 = x_ref[...] + g1_ref[...] * y
    _store_tile_rows(h1_ref, h1)
    ms = jnp.mean(h1 * h1, axis=-1, keepdims=True)
    v = h1 * lax.rsqrt(ms + EPS) * nw_ref[...] * (1.0 + sc_ref[...]) + sh_ref[...]
    _store_tile_rows(v_ref, v)

    tm = v.shape[0]
    vh, vl = _split_bf16(v)
    rwh, rwl = rwh_ref[...], rwl_ref[...]
    l = (_dot_nt(rwh, vh) + (_dot_nt(rwh, vl) + _dot_nt(rwl, vh))) + rb_ref[:, 0:1]
    row_f = lax.broadcasted_iota(I32, (N_EXPERTS, tm), 0).astype(F32)
    sels, tops, idxs = [], [], []
    for _ in range(TOP_K):
        m = jnp.max(l, axis=0, keepdims=True)
        i = jnp.min(jnp.where(l == m, row_f, float(N_EXPERTS)), axis=0, keepdims=True)
        sel = row_f == i
        l = jnp.where(sel, -jnp.inf, l)
        sels.append(sel)
        tops.append(m)
        idxs.append(i)
    es = [jnp.exp(t - tops[0]) for t in tops]
    den = es[0] + es[1] + es[2] + es[3]
    gates = [e / den for e in es]

    oh = jnp.zeros((N_EXPERTS, tm), F32)
    for sel in sels:
        oh = jnp.where(sel, 1.0, oh)
    cnt = cnt_scr[:, 0:1]
    before = _dot(oh.astype(BF16), tri_ref[...]) + cnt
    ranks = [jnp.sum(jnp.where(sel, before, 0.0), axis=0, keepdims=True) for sel in sels]
    base_ref[...] = cnt_scr[...]
    cnt_scr[...] = cnt_scr[...] + jnp.sum(oh, axis=1, keepdims=True)
    cnt_ref[...] = cnt_scr[...]

    zero4 = jnp.zeros((TOP_K, tm), F32)
    route_t_ref[...] = jnp.concatenate(idxs + [zero4] + ranks + [zero4], axis=0)
    g_cols = jnp.concatenate(gates + [zero4], axis=0).T
    for k in range(TOP_K):
        gate_ref[pl.ds(k, tm, stride=TOP_K), :] = jnp.broadcast_to(g_cols[:, k:k + 1], (tm, LANES))


def _outproj(a_ret, a_hg, w_bf, x, g1, sc2, sh2, nw, rwh, rwl, rb, *, tm):
    B, n, _ = x.shape
    nt = n // tm
    T = B * n
    assert tm == COMBINE_TILE
    mrow = lambda b, j: (b, 0, 0)
    tok = lambda b, j: (b * nt + j, 0)
    const = lambda b, j: (0, 0)
    tri = jnp.triu(jnp.ones((tm, tm), BF16), 1)
    return pl.pallas_call(
        _outproj_kernel,
        out_shape=(jax.ShapeDtypeStruct((T * ROW_TILES, LANES), F32),
                   jax.ShapeDtypeStruct((T * ROW_TILES, LANES), F32),
                   jax.ShapeDtypeStruct((16, T), F32),
                   jax.ShapeDtypeStruct((T * TOP_K, LANES), F32),
                   jax.ShapeDtypeStruct((B * nt * N_EXPERTS, LANES), F32),
                   jax.ShapeDtypeStruct((N_EXPERTS, LANES), F32)),
        grid=(B, nt),
        in_specs=[pl.BlockSpec((None, tm, 512), lambda b, j: (b, j, 0)),
                  pl.BlockSpec((None, tm, 512), lambda b, j: (b, j, 0)),
                  pl.BlockSpec((D_MODEL, D_MODEL), const),
                  pl.BlockSpec((None, tm, D_MODEL), lambda b, j: (b, j, 0)),
                  pl.BlockSpec((None, 1, D_MODEL), mrow),
                  pl.BlockSpec((None, 1, D_MODEL), mrow),
                  pl.BlockSpec((None, 1, D_MODEL), mrow),
                  pl.BlockSpec((1, D_MODEL), const),
                  pl.BlockSpec((N_EXPERTS, D_MODEL), const),
                  pl.BlockSpec((N_EXPERTS, D_MODEL), const),
                  pl.BlockSpec((N_EXPERTS, LANES), const),
                  pl.BlockSpec((tm, tm), const)],
        out_specs=(pl.BlockSpec((tm * ROW_TILES, LANES), tok),
                   pl.BlockSpec((tm * ROW_TILES, LANES), tok),
                   pl.BlockSpec((16, tm), lambda b, j: (0, b * nt + j)),
                   pl.BlockSpec((tm * TOP_K, LANES), tok),
                   pl.BlockSpec((N_EXPERTS, LANES), tok),
                   pl.BlockSpec((N_EXPERTS, LANES), const)),
        scratch_shapes=[pltpu.VMEM((N_EXPERTS, LANES), F32)],
        compiler_params=_cparams(("arbitrary", "arbitrary")),
        name="outproj_router",
    )(a_ret, a_hg, w_bf, x, g1, sc2, sh2, nw, rwh, rwl, rb, tri)


def _dispatch_kernel(dest_hbm, v_ref, xs_hbm, idx_smem, zbuf, sem_idx, sem_z, sem_rows,
                     *, tile, n_tiles, n_rows):
    i = pl.program_id(0)
    slot = lax.rem(i, 2)

    def idx_copy(j, s):
        n_asg = tile * TOP_K
        return pltpu.make_async_copy(dest_hbm.at[j], idx_smem.at[pl.ds(s * n_asg, n_asg)],
                                     sem_idx.at[s])

    def zero_copy(j):
        r0 = (n_rows + j * SUB_ROWS) * ROW_TILES
        return pltpu.make_async_copy(zbuf, xs_hbm.at[pl.ds(r0, SUB_ROWS * ROW_TILES), :], sem_z)

    @pl.when(i == 0)
    def _():
        idx_copy(0, 0).start()
        zbuf[...] = jnp.zeros_like(zbuf)
        for j in range(ROW_BLOCK // SUB_ROWS):
            zero_copy(j).start()
        for j in range(ROW_BLOCK // SUB_ROWS):
            zero_copy(j).wait()

    idx_copy(i, slot).wait()

    @pl.when(i + 1 < n_tiles)
    def _():
        idx_copy(i + 1, 1 - slot).start()

    def issue(t, carry):
        e0 = slot * (tile * TOP_K) + t
        for k in range(TOP_K):
            d = pl.multiple_of(idx_smem[e0 + k * tile], ROW_TILES)
            src = v_ref.at[pl.ds(pl.multiple_of(t * ROW_TILES, ROW_TILES), ROW_TILES), :]
            pltpu.make_async_copy(src, xs_hbm.at[pl.ds(d, ROW_TILES), :],
                                  sem_rows).start(priority=k % 2)
        return carry
    lax.fori_loop(0, tile, issue, 0, unroll=4)
    for _ in range(TOP_K):
        pltpu.make_async_copy(v_ref, xs_hbm.at[pl.ds(0, tile * ROW_TILES), :], sem_rows).wait()


def _dispatch(dest_tiles, v, *, tile):
    T = v.shape[0] // ROW_TILES
    n_rows = T * TOP_K
    return pl.pallas_call(
        functools.partial(_dispatch_kernel, tile=tile, n_tiles=T // tile, n_rows=n_rows),
        out_shape=jax.ShapeDtypeStruct(((n_rows + ROW_BLOCK) * ROW_TILES, LANES), F32),
        grid=(T // tile,),
        in_specs=[pl.BlockSpec(memory_space=pl.ANY),
                  pl.BlockSpec((tile * ROW_TILES, LANES), lambda i: (i, 0))],
        out_specs=pl.BlockSpec(memory_space=pl.ANY),
        scratch_shapes=[pltpu.SMEM((2 * tile * TOP_K,), I32),
                        pltpu.VMEM((SUB_ROWS * ROW_TILES, LANES), F32),
                        pltpu.SemaphoreType.DMA((2,)),
                        pltpu.SemaphoreType.DMA,
                        pltpu.SemaphoreType.DMA],
        compiler_params=_cparams(("arbitrary",)),
        name="dispatch",
    )(dest_tiles, v)


def _moe_kernel(be_ref, ns_ref, nx_ref, xr_ref, x_hbm, w1_hbm, b1_ref, w2_hbm, b2_ref, y_ref,
                w1_stage, w2_stage, w1_scr, w2_scr, xbuf, x_scr, act_scr, sem_w, sem_x,
                *, n_blocks):
    i = pl.program_id(0)
    slot = lax.rem(i, 2)
    e = be_ref[i]
    nsub = ns_ref[i]
    changed = jnp.logical_or(i == 0, e != be_ref[jnp.maximum(i - 1, 0)])

    def x_copy(j, s):
        r0 = pl.multiple_of(xr_ref[j], ROW_TILES)
        return pltpu.make_async_copy(x_hbm.at[pl.ds(r0, ROW_BLOCK * ROW_TILES), :], xbuf.at[s],
                                     sem_x.at[s])

    @pl.when(i == 0)
    def _():
        x_copy(0, 0).start()

    @pl.when(nsub > 0)
    def _():
        x_copy(i, slot).wait()

    nxt_blk = jnp.minimum(i + 1, n_blocks - 1)

    @pl.when(jnp.logical_and(i + 1 < n_blocks, ns_ref[nxt_blk] > 0))
    def _():
        x_copy(nxt_blk, 1 - slot).start()

    def weight_copies(ex):
        return (pltpu.make_async_copy(w1_hbm.at[ex], w1_stage, sem_w.at[0]),
                pltpu.make_async_copy(w2_hbm.at[ex], w2_stage, sem_w.at[1]))

    @pl.when(i == 0)
    def _():
        for cp in weight_copies(e):
            cp.start()

    @pl.when(jnp.logical_and(changed, nsub > 0))
    def _():
        for cp in weight_copies(e):
            cp.wait()
        w1_scr[...] = w1_stage[...].astype(BF16)
        w2_scr[...] = w2_stage[...].astype(BF16)
        nxt = nx_ref[e]

        @pl.when(nxt >= 0)
        def _():
            for cp in weight_copies(nxt):
                cp.start()

    def compute(rows):
        x_scr[0:rows, :] = _load_tile_rows(xbuf.at[slot], rows).astype(BF16)
        cw = 256
        for c in range(D_FF // cw):
            glu = (_dot(x_scr[0:rows, :], w1_scr[:, c * cw:(c + 1) * cw])
                   + b1_ref[:, c * cw:(c + 1) * cw])
            lin = (_dot(x_scr[0:rows, :], w1_scr[:, D_FF + c * cw:D_FF + (c + 1) * cw])
                   + b1_ref[:, D_FF + c * cw:D_FF + (c + 1) * cw])
            glu = jnp.minimum(glu, SWIGLU_LIMIT)
            lin = jnp.clip(lin, -SWIGLU_LIMIT, SWIGLU_LIMIT)
            act = glu * jax.nn.sigmoid(SWIGLU_ALPHA * glu) * (lin + 1.0)
            act_scr[0:rows, c * cw:(c + 1) * cw] = act.astype(BF16)
        for c in range(D_MODEL // cw):
            y = (_dot(act_scr[0:rows, :], w2_scr[:, c * cw:(c + 1) * cw])
                 + b2_ref[:, c * cw:(c + 1) * cw])
            for s in range(cw // LANES):
                t = c * (cw // LANES) + s
                y_ref[pl.ds(t, rows, stride=ROW_TILES), :] = y[:, s * LANES:(s + 1) * LANES]
        if rows < ROW_BLOCK:
            y_ref[rows * ROW_TILES:, :] = jnp.zeros(((ROW_BLOCK - rows) * ROW_TILES, LANES), F32)

    for m in range(1, ROW_BLOCK // SUB_ROWS + 1):
        @pl.when(nsub == m)
        def _(m=m):
            compute(m * SUB_ROWS)

    @pl.when(nsub == 0)
    def _():
        y_ref[...] = jnp.zeros_like(y_ref)


def _moe(block_e, nsub, next_e, xrow, xs, w1, b1, w2, b2, *, n_blocks):
    return pl.pallas_call(
        functools.partial(_moe_kernel, n_blocks=n_blocks),
        out_shape=jax.ShapeDtypeStruct((n_blocks * ROW_BLOCK * ROW_TILES, LANES), F32),
        grid_spec=pltpu.PrefetchScalarGridSpec(
            num_scalar_prefetch=4,
            grid=(n_blocks,),
            in_specs=[pl.BlockSpec(memory_space=pl.ANY),
                      pl.BlockSpec(memory_space=pl.ANY),
                      pl.BlockSpec((None, 1, 2 * D_FF), lambda i, be, ns, nx, xr: (be[i], 0, 0)),
                      pl.BlockSpec(memory_space=pl.ANY),
                      pl.BlockSpec((None, 1, D_MODEL), lambda i, be, ns, nx, xr: (be[i], 0, 0))],
            out_specs=pl.BlockSpec((ROW_BLOCK * ROW_TILES, LANES),
                                   lambda i, be, ns, nx, xr: (i, 0)),
            scratch_shapes=[pltpu.VMEM((D_MODEL, 2 * D_FF), F32),
                            pltpu.VMEM((D_FF, D_MODEL), F32),
                            pltpu.VMEM((D_MODEL, 2 * D_FF), BF16),
                            pltpu.VMEM((D_FF, D_MODEL), BF16),
                            pltpu.VMEM((2, ROW_BLOCK * ROW_TILES, LANES), F32),
                            pltpu.VMEM((ROW_BLOCK, D_MODEL), BF16),
                            pltpu.VMEM((ROW_BLOCK, D_FF), BF16),
                            pltpu.SemaphoreType.DMA((2,)),
                            pltpu.SemaphoreType.DMA((2,))]),
        compiler_params=_cparams(("arbitrary",)),
        name="moe_ffn",
    )(block_e, nsub, next_e, xrow, xs, w1, b1, w2, b2)


def _combine_kernel(pos_hbm, src_hbm, yb_hbm, gate_ref, h1_ref, g2_ref, nw_ref, o_ref,
                    pos_smem, src_smem, win, hbuf, sem_tab, sem_win, *, n_tiles):
    i = pl.program_id(0)
    slot = lax.rem(i, 2)
    tile = COMBINE_TILE
    n_asg = tile * TOP_K
    chunk = WIN_ROWS * ROW_TILES
    slot_rows = WIN_CHUNKS * chunk

    def table_copies(j, s):
        return (pltpu.make_async_copy(pos_hbm.at[j], pos_smem.at[pl.ds(s * n_asg, n_asg)],
                                      sem_tab.at[s]),
                pltpu.make_async_copy(src_hbm.at[j], src_smem.at[pl.ds(s * n_asg, n_asg)],
                                      sem_tab.at[s]))

    def fetch_window(s):
        def one(c, carry):
            src = pl.multiple_of(src_smem[s * n_asg + c], ROW_TILES)
            dst = pl.multiple_of(s * slot_rows + c * chunk, chunk)
            pltpu.make_async_copy(yb_hbm.at[pl.ds(src, chunk), :], win.at[pl.ds(dst, chunk), :],
                                  sem_win.at[s]).start()
            return carry
        lax.fori_loop(0, WIN_CHUNKS, one, 0, unroll=2)

    @pl.when(i == 0)
    def _():
        for cp in table_copies(0, 0):
            cp.start()
        for cp in table_copies(0, 0):
            cp.wait()
        fetch_window(0)
        if n_tiles > 1:
            for cp in table_copies(1, 1):
                cp.start()

    @pl.when(i + 1 < n_tiles)
    def _():
        for cp in table_copies(i + 1, 1 - slot):
            cp.wait()
        fetch_window(1 - slot)

    pltpu.make_async_copy(yb_hbm.at[pl.ds(0, slot_rows), :],
                          win.at[pl.ds(pl.multiple_of(slot * slot_rows, slot_rows), slot_rows), :],
                          sem_win.at[slot]).wait()

    g2 = g2_ref[...]
    base = slot * n_asg

    def token(t, carry):
        e0 = base + t * TOP_K
        acc = None
        for k in range(TOP_K):
            p = pl.multiple_of(pos_smem[e0 + k], ROW_TILES)
            term = gate_ref[pl.ds(t * TOP_K + k, 1), :] * win[pl.ds(p, ROW_TILES), :]
            acc = term if acc is None else acc + term
        r0 = pl.multiple_of(t * ROW_TILES, ROW_TILES)
        hbuf[pl.ds(r0, ROW_TILES), :] = h1_ref[pl.ds(r0, ROW_TILES), :] + g2 * acc
        return carry
    lax.fori_loop(0, tile, token, 0, unroll=4)

    @pl.when(i + 2 < n_tiles)
    def _():
        for cp in table_copies(i + 2, slot):
            cp.start()

    h = _load_tile_rows(hbuf, tile)
    ms = jnp.mean(h * h, axis=-1, keepdims=True)
    o_ref[...] = h * lax.rsqrt(ms + EPS) * nw_ref[...]


def _combine(pos_tiles, src_tiles, yb, gate_rows, h1, g2_tiles, nw, *, tiles_per_batch):
    tile = COMBINE_TILE
    T = h1.shape[0] // ROW_TILES
    n_asg = tile * TOP_K
    return pl.pallas_call(
        functools.partial(_combine_kernel, n_tiles=T // tile),
        out_shape=jax.ShapeDtypeStruct((T, D_MODEL), F32),
        grid=(T // tile,),
        in_specs=[pl.BlockSpec(memory_space=pl.ANY),
                  pl.BlockSpec(memory_space=pl.ANY),
                  pl.BlockSpec(memory_space=pl.ANY),
                  pl.BlockSpec((tile * TOP_K, LANES), lambda i: (i, 0)),
                  pl.BlockSpec((tile * ROW_TILES, LANES), lambda i: (i, 0)),
                  pl.BlockSpec((None, ROW_TILES, LANES), lambda i: (i // tiles_per_batch, 0, 0)),
                  pl.BlockSpec((1, D_MODEL), lambda i: (0, 0))],
        out_specs=pl.BlockSpec((tile, D_MODEL), lambda i: (i, 0)),
        scratch_shapes=[pltpu.SMEM((2 * n_asg,), I32),
                        pltpu.SMEM((2 * n_asg,), I32),
                        pltpu.VMEM((2 * WIN_CHUNKS * WIN_ROWS * ROW_TILES, LANES), F32),
                        pltpu.VMEM((tile * ROW_TILES, LANES), F32),
                        pltpu.SemaphoreType.DMA((2,)),
                        pltpu.SemaphoreType.DMA((2,))],
        compiler_params=_cparams(("arbitrary",)),
        name="combine_norm",
    )(pos_tiles, src_tiles, yb, gate_rows, h1, g2_tiles, nw)


def _rope_tables(n):
    rows = n // GRID_W
    row = jnp.repeat(jnp.arange(rows, dtype=F32), GRID_W)
    col = jnp.tile(jnp.arange(GRID_W, dtype=F32), rows)
    n_freq = RET_DK // 4
    inv = ROPE_BASE ** (-jnp.arange(n_freq, dtype=F32) / n_freq)
    ang = jnp.concatenate([row[:, None] * inv, col[:, None] * inv], axis=-1)
    cos, sin = jnp.cos(ang), jnp.sin(ang)
    cos_h = jnp.concatenate([cos, cos], axis=-1)
    sin_h = jnp.concatenate([-sin, sin], axis=-1)
    return jnp.tile(cos_h, (1, RET_HEADS)), jnp.tile(sin_h, (1, RET_HEADS))


def kernel(x, c, ctx, c_ctx, w_ada, b_ada, norm_mix_w, norm_ffn_w, w_in, w_out, hg_lb,
           hg_norm_w, router_w, router_b, w1, b1, w2, b2, norm_final_w):
    B, N, D = x.shape
    C = ctx.shape[1]
    T = B * N
    assert D == D_MODEL and w_ada.shape[0] == 1

    cc = jnp.concatenate([c.astype(F32), c_ctx.astype(F32)[None, :],
                          jnp.zeros((16 - B - 1, D), F32)], axis=0)
    mod = _ada(cc, w_ada[0], b_ada[0][None, :])
    mod = mod.reshape(16, 6, 1, D).transpose(1, 0, 2, 3)
    sh1, sc1, g1, sh2, sc2, g2 = (mod[i] for i in range(6))

    w_in_bf = w_in[0].astype(BF16)
    w_out_bf = w_out[0].astype(BF16)
    nw_mix = norm_mix_w[0][None, :]
    cos_t, sin_t = _rope_tables(N)

    pb, lf = _inproj(x, sc1, sh1, None, nw_mix, w_in_bf, hg_lb[:2], cos_t, sin_t,
                     rope=True, tm=1024)
    pbc, lfc = _inproj(ctx.reshape(1, B * C, D), sc1, sh1, B, nw_mix, w_in_bf, hg_lb[:2],
                       cos_t, sin_t, rope=False, tm=1024)
    pbc = pbc.reshape(B, C, PROJ_W)
    lfc = lfc.reshape(B, C, 1024)

    a_ret = _retention(pb, pbc)
    a_hg = _hgrn(pb, lf, pbc, lfc, hg_norm_w[0][None, :])

    rw_t = router_w[0].T
    rwh = rw_t.astype(BF16)
    rwl = (rw_t - rwh.astype(F32)).astype(BF16)
    rb = jnp.broadcast_to(router_b[0][:, None], (N_EXPERTS, LANES))
    h1, v, route_t, gate_rows, base_cnt, cnt = _outproj(a_ret, a_hg, w_out_bf, x, g1, sc2, sh2,
                                                        norm_ffn_w[0][None, :], rwh, rwl, rb, tm=512)

    idx = route_t[0:TOP_K].astype(I32)
    rank = route_t[2 * TOP_K:3 * TOP_K].astype(I32)
    counts = cnt[:, 0].astype(I32)
    padded = (counts + ROW_BLOCK - 1) // ROW_BLOCK * ROW_BLOCK
    pend = jnp.cumsum(padded)
    pstart = pend - padded
    cstart = jnp.cumsum(counts) - counts
    onehot = idx[None] == jnp.arange(N_EXPERTS, dtype=I32)[:, None, None]

    def dest_tiles(first_row, tile):
        dest = (jnp.sum(jnp.where(onehot, first_row[:, None, None], 0), axis=0) + rank) * ROW_TILES
        return dest.reshape(TOP_K, T // tile, tile).transpose(1, 0, 2).reshape(T // tile,
                                                                               TOP_K * tile)
    n_blocks = (T * TOP_K) // ROW_BLOCK + N_EXPERTS
    starts = jnp.arange(n_blocks, dtype=I32) * ROW_BLOCK
    block_e = jnp.minimum(jnp.sum((pend[None, :] <= starts[:, None]).astype(I32), axis=1),
                          N_EXPERTS - 1)
    valid = jnp.clip(counts[block_e] - (starts - pstart[block_e]), 0, ROW_BLOCK)
    nsub = (valid + SUB_ROWS - 1) // SUB_ROWS
    first_from = lax.cummin(jnp.where(counts > 0, jnp.arange(N_EXPERTS, dtype=I32), N_EXPERTS),
                            reverse=True)
    next_e = jnp.concatenate([first_from[1:], jnp.full((1,), N_EXPERTS, I32)])
    next_e = jnp.where(next_e == N_EXPERTS, -1, next_e)
    xrow = jnp.where(nsub > 0, cstart[block_e] + starts - pstart[block_e], 0) * ROW_TILES

    tile_d = 512
    xs = _dispatch(dest_tiles(cstart, tile_d), v, tile=tile_d)
    yb = _moe(block_e, nsub, next_e, xrow.astype(I32), xs, w1[0], b1[0][:, None, :], w2[0],
              b2[0][:, None, :], n_blocks=n_blocks)

    tc = COMBINE_TILE
    n_ct = T // tc
    base = base_cnt.reshape(n_ct, N_EXPERTS, LANES)[:, :, 0].astype(I32)
    n_run = jnp.concatenate([base[1:], counts[None, :]], axis=0) - base
    n_chunk = (n_run + WIN_ROWS - 1) // WIN_ROWS
    chunk_end = jnp.cumsum(n_chunk, axis=1)
    chunk_start = chunk_end - n_chunk
    win_off = chunk_start * WIN_ROWS
    c_id = jnp.arange(WIN_CHUNKS, dtype=I32)
    c_exp = jnp.sum((chunk_end[:, None, :] <= c_id[None, :, None]).astype(I32), axis=2)
    c_real = c_exp < N_EXPERTS
    c_own = c_exp[:, :, None] == jnp.arange(N_EXPERTS, dtype=I32)
    take = lambda tab: jnp.sum(jnp.where(c_own, tab[:, None, :], 0), axis=2)
    c_src = (take(pstart[None, :] + base) + (c_id[None, :] - take(chunk_start)) * WIN_ROWS)
    src_tiles = jnp.concatenate(
        [jnp.where(c_real, c_src, 0) * ROW_TILES,
         jnp.zeros((n_ct, tc * TOP_K - WIN_CHUNKS), I32)], axis=1)
    slot_off = (jnp.arange(n_ct, dtype=I32) % 2) * (WIN_CHUNKS * WIN_ROWS)
    shift = jnp.repeat(win_off - base + slot_off[:, None], tc, axis=0).T
    pos = (jnp.sum(jnp.where(onehot, shift[:, None, :], 0), axis=0) + rank) * ROW_TILES

    def per_tile(a):
        return a.reshape(TOP_K, n_ct, tc).transpose(1, 2, 0).reshape(n_ct, tc * TOP_K)

    out = _combine(per_tile(pos), src_tiles, yb, gate_rows, h1,
                   g2.reshape(16, ROW_TILES, LANES), norm_final_w[None, :],
                   tiles_per_batch=N // tc)
    return out.reshape(B, N, D)
```

```python
import functools
import math

import jax
import jax.numpy as jnp
import numpy as np
from jax import lax
from jax.experimental import pallas as pl
from jax.experimental.pallas import tpu as pltpu

F32 = jnp.float32
BF16 = jnp.bfloat16
I32 = jnp.int32

D_MODEL = 1024
GRID_W = 64
RET_HEADS = 4
RET_DK = 64
HG_HEADS = 4
PROJ_W = 4096
ROPE_BASE = 10000.0
EPS = 1e-6
N_EXPERTS = 32
TOP_K = 4
D_FF = 1024
SWIGLU_LIMIT = 7.0
SWIGLU_ALPHA = 1.702
LOG2_E = 1.0 / math.log(2.0)

LANES = 128
CHUNK = 128
ROW_BLOCK = 1024
SUB_ROWS = 256
COMBINE_TILE = 512
WIN_ROWS = 16
WIN_CHUNKS = (COMBINE_TILE * TOP_K + N_EXPERTS * (WIN_ROWS - 1) + WIN_ROWS - 1) // WIN_ROWS
VMEM_LIMIT = 56 * 1024 * 1024
ROW_TILES = D_MODEL // LANES

C_RQ, C_RK, C_RV, C_RG, C_HQ, C_FF, C_FB, C_HV, C_HG = (
    0, 256, 512, 1024, 1536, 2048, 2560, 3072, 3584)


def _cparams(sem):
    return pltpu.CompilerParams(dimension_semantics=sem, vmem_limit_bytes=VMEM_LIMIT)


def _split_bf16(x):
    hi = x.astype(BF16)
    lo = (x - hi.astype(F32)).astype(BF16)
    return hi, lo


def _dot(a, b):
    return jnp.dot(a, b, preferred_element_type=F32)


def _dot_nt(a, b):
    return lax.dot_general(a, b, (((1,), (1,)), ((), ())), preferred_element_type=F32)


def _dot3(a, b):
    ah, al = _split_bf16(a)
    bh, bl = _split_bf16(b)
    return _dot(ah, bh) + (_dot(ah, bl) + _dot(al, bh))


def _silu(x):
    return x * jax.nn.sigmoid(x)


def _load_tile_rows(ref, n):
    return jnp.concatenate([ref[pl.ds(s, n, stride=ROW_TILES), :] for s in range(ROW_TILES)],
                           axis=1)


def _store_tile_rows(ref, x):
    n = x.shape[0]
    for s in range(ROW_TILES):
        ref[pl.ds(s, n, stride=ROW_TILES), :] = x[:, s * LANES:(s + 1) * LANES]


def _ada_kernel(c_ref, w_ref, b_ref, o_ref):
    s = _silu(c_ref[...])
    o_ref[...] = _dot3(s, w_ref[...]) + b_ref[...]


def _ada(cc, w, b):
    nblk = w.shape[1] // D_MODEL
    return pl.pallas_call(
        _ada_kernel,
        out_shape=jax.ShapeDtypeStruct((cc.shape[0], w.shape[1]), F32),
        grid=(nblk,),
        in_specs=[pl.BlockSpec(cc.shape, lambda j: (0, 0)),
                  pl.BlockSpec((D_MODEL, D_MODEL), lambda j: (0, j)),
                  pl.BlockSpec((1, D_MODEL), lambda j: (0, j))],
        out_specs=pl.BlockSpec((cc.shape[0], D_MODEL), lambda j: (0, j)),
        compiler_params=_cparams(("arbitrary",)),
        name="ada",
    )(cc, w, b)


def _inproj_kernel(x_ref, sc_ref, sh_ref, nw_ref, w_ref, lb_ref, cos_ref, sin_ref,
                   pb_ref, lf_ref, u_scr, *, rope):
    x = x_ref[...]
    ms = jnp.mean(x * x, axis=-1, keepdims=True)
    u = x * lax.rsqrt(ms + EPS) * nw_ref[...] * (1.0 + sc_ref[...]) + sh_ref[...]
    u_scr[...] = u.astype(BF16)

    def proj(lo, width):
        return _dot(u_scr[...], w_ref[:, lo:lo + width])

    tm = x.shape[0]
    if rope:
        lane = lax.broadcasted_iota(I32, (tm, LANES), 1)
        first = (lane & 32) == 0

    def put_rot(col, scale):
        for j in range(2):
            lo = col + j * LANES
            t = proj(lo, LANES)
            if scale != 1.0:
                t = t * scale
            if rope:
                tb = j * LANES
                rot = jnp.where(first, pltpu.roll(t, 96, axis=1), pltpu.roll(t, 32, axis=1))
                t = t * cos_ref[:, tb:tb + LANES] + rot * sin_ref[:, tb:tb + LANES]
            pb_ref[:, lo:lo + LANES] = t.astype(BF16)

    put_rot(C_RQ, 1.0)
    put_rot(C_RK, RET_DK ** -0.5)
    pb_ref[:, C_RV:C_RV + 512] = proj(C_RV, 512).astype(BF16)
    pb_ref[:, C_RG:C_RG + 512] = _silu(proj(C_RG, 512)).astype(BF16)
    pb_ref[:, C_HQ:C_HQ + 512] = _silu(proj(C_HQ, 512)).astype(BF16)
    pb_ref[:, C_HV:C_HV + 512] = proj(C_HV, 512).astype(BF16)
    pb_ref[:, C_HG:C_HG + 512] = _silu(proj(C_HG, 512)).astype(BF16)

    la = lb_ref[0]
    lbb = lb_ref[1]
    mx = jnp.maximum(la, lbb)
    ea = jnp.exp(la - mx)
    eb = jnp.exp(lbb - mx)
    lb = ea / (ea + eb)
    for d, col in enumerate((C_FF, C_FB)):
        lbd = lb[d:d + 1, :]
        f = lbd + (1.0 - lbd) * jax.nn.sigmoid(proj(col, 512))
        pb_ref[:, col:col + 512] = (1.0 - f).astype(BF16)
        lf_ref[:, d * 512:(d + 1) * 512] = jnp.log(f) * LOG2_E


def _inproj(x, sc, sh, mod_row, nw, w_bf, hg_lb, cos_t, sin_t, *, rope, tm):
    B, n, _ = x.shape
    nt = n // tm
    if mod_row is None:
        mrow = lambda b, j: (b, 0, 0)
    else:
        mrow = lambda b, j: (mod_row, 0, 0)
    return pl.pallas_call(
        functools.partial(_inproj_kernel, rope=rope),
        out_shape=(jax.ShapeDtypeStruct((B, n, PROJ_W), BF16),
                   jax.ShapeDtypeStruct((B, n, 1024), F32)),
        grid=(B, nt),
        in_specs=[pl.BlockSpec((None, tm, D_MODEL), lambda b, j: (b, j, 0)),
                  pl.BlockSpec((None, 1, D_MODEL), mrow),
                  pl.BlockSpec((None, 1, D_MODEL), mrow),
                  pl.BlockSpec((1, D_MODEL), lambda b, j: (0, 0)),
                  pl.BlockSpec((D_MODEL, PROJ_W), lambda b, j: (0, 0),
                               pipeline_mode=pl.Buffered(1)),
                  pl.BlockSpec((2, 2, 512), lambda b, j: (0, 0, 0)),
                  pl.BlockSpec((tm, 256), lambda b, j: (j, 0)),
                  pl.BlockSpec((tm, 256), lambda b, j: (j, 0))],
        out_specs=(pl.BlockSpec((None, tm, PROJ_W), lambda b, j: (b, j, 0)),
                   pl.BlockSpec((None, tm, 1024), lambda b, j: (b, j, 0))),
        scratch_shapes=[pltpu.VMEM((tm, D_MODEL), BF16)],
        compiler_params=_cparams(("arbitrary", "arbitrary")),
        name="inproj_rope" if rope else "inproj_ctx",
    )(x, sc, sh, nw, w_bf, hg_lb, cos_t, sin_t)


_RET_LGF = [math.log1p(-(2.0 ** (-5.0 - 2.0 * h))) for h in range(RET_HEADS)]
_RET_LGB = [math.log1p(-(2.0 ** (-6.0 - 2.0 * h))) for h in range(RET_HEADS)]


def _ret_kernel(q_ref, k_ref, v_ref, g_ref, kc_ref, vc_ref, o_ref,
                u_scr, sin_scr, kt_scr, dtot_scr, *, n_lat, n_ctx):
    L = CHUNK
    pair = pl.program_id(1)
    row = lax.broadcasted_iota(I32, (L, L), 0).astype(F32)
    col = lax.broadcasted_iota(I32, (L, L), 1).astype(F32)
    lane = lax.broadcasted_iota(I32, (L, LANES), 1)
    trow = lax.broadcasted_iota(I32, (L, 1), 0).astype(F32)
    tcol = lax.broadcasted_iota(I32, (1, L), 1).astype(F32)
    low_half = lane < RET_DK

    def u_chunk(k_blk, v_blk, ci, store_kt):
        kt = k_blk.astype(F32).T
        if store_kt is not None:
            kt_scr[store_kt] = kt.astype(BF16)
        for hh in range(2):
            lgf = jnp.where(pair == 0, _RET_LGF[hh], _RET_LGF[2 + hh])
            lgb = jnp.where(pair == 0, _RET_LGB[hh], _RET_LGB[2 + hh])
            kth = kt[hh * RET_DK:(hh + 1) * RET_DK, :]
            wkf = jnp.exp(lgf * (L - 1.0 - tcol))
            wkb = jnp.exp(lgb * tcol)
            lhs = jnp.concatenate([kth * wkf, kth * wkb], axis=0).astype(BF16)
            u_scr[hh, ci] = _dot(lhs, v_blk[:, hh * LANES:(hh + 1) * LANES])

    for c in range(n_ctx):
        u_chunk(kc_ref[c * L:(c + 1) * L, :], vc_ref[c * L:(c + 1) * L, :], c, None)

    def lat_u(c, carry):
        r0 = pl.multiple_of(c * L, L)
        u_chunk(k_ref[pl.ds(r0, L), :], v_ref[pl.ds(r0, L), :], n_ctx + c, c)
        return carry
    lax.fori_loop(0, n_lat, lat_u, 0, unroll=8)

    ones = jnp.ones((RET_DK, LANES), F32)
    for hh in range(2):
        lgf = jnp.where(pair == 0, _RET_LGF[hh], _RET_LGF[2 + hh])
        lgb = jnp.where(pair == 0, _RET_LGB[hh], _RET_LGB[2 + hh])
        d = row - col
        dtot_scr[hh] = jnp.where(d > 0, jnp.exp(lgf * jnp.maximum(d, 0.0)),
                                 jnp.where(d < 0, jnp.exp(lgb * jnp.maximum(-d, 0.0)), 2.0))
        af = jnp.exp(ones * (lgf * L))
        ab = jnp.exp(ones * (lgb * L))

        s = jnp.zeros((RET_DK, LANES), F32)
        for c in range(n_ctx):
            s = af * s + u_scr[hh, c, 0:RET_DK, :]
        sb = jnp.zeros((RET_DK, LANES), F32)
        for c in reversed(range(n_ctx)):
            sb = ab * sb + u_scr[hh, c, RET_DK:2 * RET_DK, :]

        def fwd(c, s, hh=hh, af=af):
            sin_scr[hh, c, 0:RET_DK, :] = s.astype(BF16)
            return af * s + u_scr[hh, n_ctx + c, 0:RET_DK, :]
        lax.fori_loop(0, n_lat, fwd, s)

        def bwd(i, sb, hh=hh, ab=ab):
            c = n_lat - 1 - i
            sin_scr[hh, c, RET_DK:2 * RET_DK, :] = sb.astype(BF16)
            return ab * sb + u_scr[hh, n_ctx + c, RET_DK:2 * RET_DK, :]
        lax.fori_loop(0, n_lat, bwd, sb)

    def out_chunk(c, carry):
        r0 = pl.multiple_of(c * L, L)
        q = q_ref[pl.ds(r0, L), :].astype(F32)
        qr = pltpu.roll(q, RET_DK, axis=1)
        kt = kt_scr[c]
        for hh in range(2):
            lgf = jnp.where(pair == 0, _RET_LGF[hh], _RET_LGF[2 + hh])
            lgb = jnp.where(pair == 0, _RET_LGB[hh], _RET_LGB[2 + hh])
            mine = low_half if hh == 0 else jnp.logical_not(low_half)
            qm = jnp.where(mine, q, 0.0).astype(BF16)
            p = (_dot(qm, kt) * dtot_scr[hh]).astype(BF16)
            vh = v_ref[pl.ds(r0, L), hh * LANES:(hh + 1) * LANES]
            wqf = jnp.exp(lgf * (trow + 1.0))
            wqb = jnp.exp(lgb * (L - trow))
            qa, qb = (q, qr) if hh == 0 else (qr, q)
            qs = jnp.where(low_half, qa * wqf, qb * wqb).astype(BF16)
            o = _dot(p, vh) + _dot(qs, sin_scr[hh, c])
            ms = jnp.mean(o * o, axis=-1, keepdims=True)
            gh = g_ref[pl.ds(r0, L), hh * LANES:(hh + 1) * LANES].astype(F32)
            o_ref[pl.ds(r0, L), hh * LANES:(hh + 1) * LANES] = (
                o * lax.rsqrt(ms + EPS) * gh).astype(BF16)
        return carry
    lax.fori_loop(0, n_lat, out_chunk, 0, unroll=4)


def _retention(pb, pbc):
    B, n, _ = pb.shape
    nc = pbc.shape[1]
    n_lat, n_ctx = n // CHUNK, nc // CHUNK
    return pl.pallas_call(
        functools.partial(_ret_kernel, n_lat=n_lat, n_ctx=n_ctx),
        out_shape=jax.ShapeDtypeStruct((B, n, 512), BF16),
        grid=(B, 2),
        in_specs=[pl.BlockSpec((None, n, LANES), lambda b, p: (b, 0, C_RQ // LANES + p)),
                  pl.BlockSpec((None, n, LANES), lambda b, p: (b, 0, C_RK // LANES + p)),
                  pl.BlockSpec((None, n, 256), lambda b, p: (b, 0, C_RV // 256 + p)),
                  pl.BlockSpec((None, n, 256), lambda b, p: (b, 0, C_RG // 256 + p)),
                  pl.BlockSpec((None, nc, LANES), lambda b, p: (b, 0, C_RK // LANES + p)),
                  pl.BlockSpec((None, nc, 256), lambda b, p: (b, 0, C_RV // 256 + p))],
        out_specs=pl.BlockSpec((None, n, 256), lambda b, p: (b, 0, p)),
        scratch_shapes=[pltpu.VMEM((2, n_lat + n_ctx, CHUNK, LANES), F32),
                        pltpu.VMEM((2, n_lat, CHUNK, LANES), BF16),
                        pltpu.VMEM((n_lat, LANES, CHUNK), BF16),
                        pltpu.VMEM((2, CHUNK, CHUNK), F32)],
        compiler_params=_cparams(("arbitrary", "arbitrary")),
        name="retention",
    )(pb, pb, pb, pb, pbc, pbc)


_LEVELS = (64, 32, 16, 8, 4, 2, 1)


def _expand_rows(r, rep):
    n = r.shape[0]
    if n == 1:
        return jnp.broadcast_to(r, (rep, r.shape[1]))
    return jnp.concatenate(
        [jnp.broadcast_to(r[i:i + 1, :], (rep, r.shape[1])) for i in range(n)], axis=0)


def _hgrn_kernel(q_ref, kf_ref, kb_ref, v_ref, g_ref, lff_ref, lfb_ref,
                 kfc_ref, kbc_ref, vc_ref, lffc_ref, lfbc_ref, nw_ref, o_ref,
                 ut_scr, a_scr, qs_scr, oi_scr, sin_scr, bfb_scr, *, n_lat, n_ctx):
    L = CHUNK
    row = lax.broadcasted_iota(I32, (L, L), 0)
    col = lax.broadcasted_iota(I32, (L, L), 1)
    xr_bits = lax.bitcast_convert_type((row ^ col).astype(F32), I32)
    lv = lax.shift_right_logical(xr_bits, 23) - 127
    row2 = lax.broadcasted_iota(I32, (L, 2 * L), 0)
    col2 = lax.broadcasted_iota(I32, (L, 2 * L), 1) & (L - 1)
    tril2 = jnp.where(col2 <= row2, 1.0, 0.0).astype(BF16)
    triu2 = jnp.where(col2 >= row2, 1.0, 0.0).astype(BF16)

    def cums(lff, lfb):
        hf, lof = _split_bf16(lff)
        hb, lob = _split_bf16(lfb)
        bf = _dot(tril2, jnp.concatenate([hf, lof], axis=0))
        bb = _dot(triu2, jnp.concatenate([hb, lob], axis=0))
        return bf, bb

    def state_part(ci, kf, kb, v_blk, bf, bb):
        endf = bf[L - 1:L, :]
        endb = bb[0:1, :]
        ksf = kf * jnp.exp2(endf - bf)
        ksb = kb * jnp.exp2(endb - bb)
        vt = v_blk.astype(F32).T.astype(BF16)
        ut_scr[ci] = _dot(vt, jnp.concatenate([ksf, ksb], axis=1).astype(BF16))
        a_scr[ci] = jnp.broadcast_to(
            jnp.concatenate([jnp.exp2(endf), jnp.exp2(endb)], axis=1), (8, 2 * LANES))

    for c in range(n_ctx):
        sl = slice(c * L, (c + 1) * L)
        bf, bb = cums(lffc_ref[sl, :], lfbc_ref[sl, :])
        state_part(c, kfc_ref[sl, :].astype(F32), kbc_ref[sl, :].astype(F32),
                   vc_ref[sl, :], bf, bb)

    def lat_chunk(c, u):
        rows = pl.ds(pl.multiple_of(c * L, L), L)
        bf_scr = bfb_scr.at[u, 0]
        bb_scr = bfb_scr.at[u, 1]
        q = q_ref[rows, :].astype(F32)
        kf = kf_ref[rows, :].astype(F32)
        kb = kb_ref[rows, :].astype(F32)
        v_blk = v_ref[rows, :]
        lff = lff_ref[rows, :]
        lfb = lfb_ref[rows, :]
        bf, bb = cums(lff, lfb)
        state_part(n_ctx + c, kf, kb, v_blk, bf, bb)
        qs_scr[rows, :] = jnp.concatenate([q * jnp.exp2(bf), q * jnp.exp2(bb)],
                                          axis=1).astype(BF16)
        bf_scr[...] = bf
        bb_scr[...] = bb

        acc = jnp.zeros((L, L), F32)
        for lvl, h in enumerate(_LEVELS):
            bit = (row & h) != 0
            ksel = jnp.where(bit, kb, kf)
            if h >= 4:
                n = (L // 2) // h
                if n == 1:
                    rf = bf_scr[h - 1:h, :]
                    rb = bb_scr[h:h + 1, :]
                else:
                    rf = bf_scr[pl.ds(h - 1, n, stride=2 * h), :]
                    rb = bb_scr[pl.ds(h, n, stride=2 * h), :]
                df = bf - _expand_rows(rf, 2 * h)
                db = bb - _expand_rows(rb, 2 * h)
                eq = jnp.where(bit, df, db)
                ek = -jnp.where(bit, db, df)
            elif h == 2:
                m = row & 3
                lff_n = pltpu.roll(lff, L - 1, axis=0)
                lfb_n = pltpu.roll(lfb, L - 1, axis=0)
                eq = jnp.where(m == 2, lff,
                               jnp.where(m == 3, lff + pltpu.roll(lff, 1, axis=0),
                                         jnp.where(m == 0, lfb + lfb_n, lfb)))
                ek = jnp.where(m == 3, pltpu.roll(lfb, 1, axis=0),
                               jnp.where(m == 0, lff_n, 0.0))
            else:
                eq = jnp.where(bit, lff, lfb)
                ek = None
            lhs = (q * jnp.exp2(eq)).astype(BF16)
            rhs = (ksel if ek is None else ksel * jnp.exp2(ek)).astype(BF16)
            acc = jnp.where(lv == 6 - lvl, _dot_nt(lhs, rhs), acc)

        dsum = jnp.sum(q * (kf + kb), axis=-1, keepdims=True)
        oi_scr[rows, :] = _dot(acc.astype(BF16), v_blk) + dsum * v_blk.astype(F32)

    def lat_pair(i, carry):
        for u in range(4):
            lat_chunk(4 * i + u, u)
        return carry
    lax.fori_loop(0, n_lat // 4, lat_pair, 0)

    st = jnp.zeros((LANES, LANES), F32)
    for c in range(n_ctx):
        st = st * a_scr[c, 0:1, 0:LANES] + ut_scr[c, :, 0:LANES]
    stb = jnp.zeros((LANES, LANES), F32)
    for c in reversed(range(n_ctx)):
        stb = stb * a_scr[c, 0:1, LANES:2 * LANES] + ut_scr[c, :, LANES:2 * LANES]

    def fwd(c, st):
        sin_scr[c, :, 0:LANES] = st.astype(BF16)
        ci = n_ctx + c
        return st * a_scr[ci, 0:1, 0:LANES] + ut_scr[ci, :, 0:LANES]
    lax.fori_loop(0, n_lat, fwd, st)

    def bwd(i, stb):
        c = n_lat - 1 - i
        sin_scr[c, :, LANES:2 * LANES] = stb.astype(BF16)
        ci = n_ctx + c
        return stb * a_scr[ci, 0:1, LANES:2 * LANES] + ut_scr[ci, :, LANES:2 * LANES]
    lax.fori_loop(0, n_lat, bwd, stb)

    def out_chunk(c, carry):
        rows = pl.ds(pl.multiple_of(c * L, L), L)
        o = oi_scr[rows, :] + _dot_nt(qs_scr[rows, :], sin_scr[c])
        ms = jnp.mean(o * o, axis=-1, keepdims=True)
        y = o * lax.rsqrt(ms + EPS) * nw_ref[...] * g_ref[rows, :].astype(F32)
        o_ref[rows, :] = y.astype(BF16)
        return carry
    lax.fori_loop(0, n_lat, out_chunk, 0, unroll=8)


def _hgrn(pb, lf, pbc, lfc, nw):
    B, n, _ = pb.shape
    nc = pbc.shape[1]
    n_lat, n_ctx = n // CHUNK, nc // CHUNK

    def colblk(rows, col0):
        return pl.BlockSpec((None, rows, LANES), lambda b, h: (b, 0, col0 // LANES + h))

    return pl.pallas_call(
        functools.partial(_hgrn_kernel, n_lat=n_lat, n_ctx=n_ctx),
        out_shape=jax.ShapeDtypeStruct((B, n, 512), BF16),
        grid=(B, HG_HEADS),
        in_specs=[colblk(n, C_HQ), colblk(n, C_FF), colblk(n, C_FB), colblk(n, C_HV),
                  colblk(n, C_HG), colblk(n, 0), colblk(n, 512),
                  colblk(nc, C_FF), colblk(nc, C_FB), colblk(nc, C_HV),
                  colblk(nc, 0), colblk(nc, 512),
                  pl.BlockSpec((1, LANES), lambda b, h: (0, 0))],
        out_specs=pl.BlockSpec((None, n, LANES), lambda b, h: (b, 0, h)),
        scratch_shapes=[pltpu.VMEM((n_lat + n_ctx, LANES, 2 * LANES), F32),
                        pltpu.VMEM((n_lat + n_ctx, 8, 2 * LANES), F32),
                        pltpu.VMEM((n, 2 * LANES), BF16),
                        pltpu.VMEM((n, LANES), F32),
                        pltpu.VMEM((n_lat, LANES, 2 * LANES), BF16),
                        pltpu.VMEM((4, 2, CHUNK, LANES), F32)],
        compiler_params=_cparams(("arbitrary", "arbitrary")),
        name="hgrn2",
    )(pb, pb, pb, pb, pb, lf, lf, pbc, pbc, pbc, lfc, lfc, nw)


def _outproj_kernel(ar_ref, ah_ref, w_ref, x_ref, g1_ref, sc_ref, sh_ref, nw_ref,
                    rwh_ref, rwl_ref, rb_ref, tri_ref, h1_ref, v_ref, route_t_ref, gate_ref, base_ref,
                    cnt_ref, cnt_scr):
    first_step = jnp.logical_and(pl.program_id(0) == 0, pl.program_id(1) == 0)

    @pl.when(first_step)
    def _():
        cnt_scr[...] = jnp.zeros_like(cnt_scr)

    y = _dot(ar_ref[...], w_ref[0:512, :]) + _dot(ah_ref[...], w_ref[512:1024, :])
    h1 = x_ref[...] + g1_ref[...] * y
    _store_tile_rows(h1_ref, h1)
    ms = jnp.mean(h1 * h1, axis=-1, keepdims=True)
    v = h1 * lax.rsqrt(ms + EPS) * nw_ref[...] * (1.0 + sc_ref[...]) + sh_ref[...]
    _store_tile_rows(v_ref, v)

    tm = v.shape[0]
    vh, vl = _split_bf16(v)
    rwh, rwl = rwh_ref[...], rwl_ref[...]
    l = (_dot_nt(rwh, vh) + (_dot_nt(rwh, vl) + _dot_nt(rwl, vh))) + rb_ref[:, 0:1]
    row_f = lax.broadcasted_iota(I32, (N_EXPERTS, tm), 0).astype(F32)
    sels, tops, idxs = [], [], []
    for _ in range(TOP_K):
        m = jnp.max(l, axis=0, keepdims=True)
        i = jnp.min(jnp.where(l == m, row_f, float(N_EXPERTS)), axis=0, keepdims=True)
        sel = row_f == i
        l = jnp.where(sel, -jnp.inf, l)
        sels.append(sel)
        tops.append(m)
        idxs.append(i)
    es = [jnp.exp(t - tops[0]) for t in tops]
    den = es[0] + es[1] + es[2] + es[3]
    gates = [e / den for e in es]

    oh = jnp.zeros((N_EXPERTS, tm), F32)
    for sel in sels:
        oh = jnp.where(sel, 1.0, oh)
    cnt = cnt_scr[:, 0:1]
    before = _dot(oh.astype(BF16), tri_ref[...]) + cnt
    ranks = [jnp.sum(jnp.where(sel, before, 0.0), axis=0, keepdims=True) for sel in sels]
    base_ref[...] = cnt_scr[...]
    cnt_scr[...] = cnt_scr[...] + jnp.sum(oh, axis=1, keepdims=True)
    cnt_ref[...] = cnt_scr[...]

    zero4 = jnp.zeros((TOP_K, tm), F32)
    route_t_ref[...] = jnp.concatenate(idxs + [zero4] + ranks + [zero4], axis=0)
    g_cols = jnp.concatenate(gates + [zero4], axis=0).T
    for k in range(TOP_K):
        gate_ref[pl.ds(k, tm, stride=TOP_K), :] = jnp.broadcast_to(g_cols[:, k:k + 1], (tm, LANES))


def _outproj(a_ret, a_hg, w_bf, x, g1, sc2, sh2, nw, rwh, rwl, rb, *, tm):
    B, n, _ = x.shape
    nt = n // tm
    T = B * n
    assert tm == COMBINE_TILE
    mrow = lambda b, j: (b, 0, 0)
    tok = lambda b, j: (b * nt + j, 0)
    const = lambda b, j: (0, 0)
    tri = jnp.triu(jnp.ones((tm, tm), BF16), 1)
    return pl.pallas_call(
        _outproj_kernel,
        out_shape=(jax.ShapeDtypeStruct((T * ROW_TILES, LANES), F32),
                   jax.ShapeDtypeStruct((T * ROW_TILES, LANES), F32),
                   jax.ShapeDtypeStruct((16, T), F32),
                   jax.ShapeDtypeStruct((T * TOP_K, LANES), F32),
                   jax.ShapeDtypeStruct((B * nt * N_EXPERTS, LANES), F32),
                   jax.ShapeDtypeStruct((N_EXPERTS, LANES), F32)),
        grid=(B, nt),
        in_specs=[pl.BlockSpec((None, tm, 512), lambda b, j: (b, j, 0)),
                  pl.BlockSpec((None, tm, 512), lambda b, j: (b, j, 0)),
                  pl.BlockSpec((D_MODEL, D_MODEL), const),
                  pl.BlockSpec((None, tm, D_MODEL), lambda b, j: (b, j, 0)),
                  pl.BlockSpec((None, 1, D_MODEL), mrow),
                  pl.BlockSpec((None, 1, D_MODEL), mrow),
                  pl.BlockSpec((None, 1, D_MODEL), mrow),
                  pl.BlockSpec((1, D_MODEL), const),
                  pl.BlockSpec((N_EXPERTS, D_MODEL), const),
                  pl.BlockSpec((N_EXPERTS, D_MODEL), const),
                  pl.BlockSpec((N_EXPERTS, LANES), const),
                  pl.BlockSpec((tm, tm), const)],
        out_specs=(pl.BlockSpec((tm * ROW_TILES, LANES), tok),
                   pl.BlockSpec((tm * ROW_TILES, LANES), tok),
                   pl.BlockSpec((16, tm), lambda b, j: (0, b * nt + j)),
                   pl.BlockSpec((tm * TOP_K, LANES), tok),
                   pl.BlockSpec((N_EXPERTS, LANES), tok),
                   pl.BlockSpec((N_EXPERTS, LANES), const)),
        scratch_shapes=[pltpu.VMEM((N_EXPERTS, LANES), F32)],
        compiler_params=_cparams(("arbitrary", "arbitrary")),
        name="outproj_router",
    )(a_ret, a_hg, w_bf, x, g1, sc2, sh2, nw, rwh, rwl, rb, tri)


def _dispatch_kernel(dest_hbm, v_ref, xs_hbm, idx_smem, zbuf, sem_idx, sem_z, sem_rows,
                     *, tile, n_tiles, n_rows):
    i = pl.program_id(0)
    slot = lax.rem(i, 2)

    def idx_copy(j, s):
        n_asg = tile * TOP_K
        return pltpu.make_async_copy(dest_hbm.at[j], idx_smem.at[pl.ds(s * n_asg, n_asg)],
                                     sem_idx.at[s])

    def zero_copy(j):
        r0 = (n_rows + j * SUB_ROWS) * ROW_TILES
        return pltpu.make_async_copy(zbuf, xs_hbm.at[pl.ds(r0, SUB_ROWS * ROW_TILES), :], sem_z)

    @pl.when(i == 0)
    def _():
        idx_copy(0, 0).start()
        zbuf[...] = jnp.zeros_like(zbuf)
        for j in range(ROW_BLOCK // SUB_ROWS):
            zero_copy(j).start()
        for j in range(ROW_BLOCK // SUB_ROWS):
            zero_copy(j).wait()

    idx_copy(i, slot).wait()

    @pl.when(i + 1 < n_tiles)
    def _():
        idx_copy(i + 1, 1 - slot).start()

    def issue(t, carry):
        e0 = slot * (tile * TOP_K) + t
        for k in range(TOP_K):
            d = pl.multiple_of(idx_smem[e0 + k * tile], ROW_TILES)
            src = v_ref.at[pl.ds(pl.multiple_of(t * ROW_TILES, ROW_TILES), ROW_TILES), :]
            pltpu.make_async_copy(src, xs_hbm.at[pl.ds(d, ROW_TILES), :],
                                  sem_rows).start(priority=k % 2)
        return carry
    lax.fori_loop(0, tile, issue, 0, unroll=4)
    for _ in range(TOP_K):
        pltpu.make_async_copy(v_ref, xs_hbm.at[pl.ds(0, tile * ROW_TILES), :], sem_rows).wait()


def _dispatch(dest_tiles, v, *, tile):
    T = v.shape[0] // ROW_TILES
    n_rows = T * TOP_K
    return pl.pallas_call(
        functools.partial(_dispatch_kernel, tile=tile, n_tiles=T // tile, n_rows=n_rows),
        out_shape=jax.ShapeDtypeStruct(((n_rows + ROW_BLOCK) * ROW_TILES, LANES), F32),
        grid=(T // tile,),
        in_specs=[pl.BlockSpec(memory_space=pl.ANY),
                  pl.BlockSpec((tile * ROW_TILES, LANES), lambda i: (i, 0))],
        out_specs=pl.BlockSpec(memory_space=pl.ANY),
        scratch_shapes=[pltpu.SMEM((2 * tile * TOP_K,), I32),
                        pltpu.VMEM((SUB_ROWS * ROW_TILES, LANES), F32),
                        pltpu.SemaphoreType.DMA((2,)),
                        pltpu.SemaphoreType.DMA,
                        pltpu.SemaphoreType.DMA],
        compiler_params=_cparams(("arbitrary",)),
        name="dispatch",
    )(dest_tiles, v)


def _moe_kernel(be_ref, ns_ref, nx_ref, xr_ref, x_hbm, w1_hbm, b1_ref, w2_hbm, b2_ref, y_ref,
                w1_stage, w2_stage, w1_scr, w2_scr, xbuf, x_scr, act_scr, sem_w, sem_x,
                *, n_blocks):
    i = pl.program_id(0)
    slot = lax.rem(i, 2)
    e = be_ref[i]
    nsub = ns_ref[i]
    changed = jnp.logical_or(i == 0, e != be_ref[jnp.maximum(i - 1, 0)])

    def x_copy(j, s):
        r0 = pl.multiple_of(xr_ref[j], ROW_TILES)
        return pltpu.make_async_copy(x_hbm.at[pl.ds(r0, ROW_BLOCK * ROW_TILES), :], xbuf.at[s],
                                     sem_x.at[s])

    @pl.when(i == 0)
    def _():
        x_copy(0, 0).start()

    @pl.when(nsub > 0)
    def _():
        x_copy(i, slot).wait()

    nxt_blk = jnp.minimum(i + 1, n_blocks - 1)

    @pl.when(jnp.logical_and(i + 1 < n_blocks, ns_ref[nxt_blk] > 0))
    def _():
        x_copy(nxt_blk, 1 - slot).start()

    def weight_copies(ex):
        return (pltpu.make_async_copy(w1_hbm.at[ex], w1_stage, sem_w.at[0]),
                pltpu.make_async_copy(w2_hbm.at[ex], w2_stage, sem_w.at[1]))

    @pl.when(i == 0)
    def _():
        for cp in weight_copies(e):
            cp.start()

    @pl.when(jnp.logical_and(changed, nsub > 0))
    def _():
        for cp in weight_copies(e):
            cp.wait()
        w1_scr[...] = w1_stage[...].astype(BF16)
        w2_scr[...] = w2_stage[...].astype(BF16)
        nxt = nx_ref[e]

        @pl.when(nxt >= 0)
        def _():
            for cp in weight_copies(nxt):
                cp.start()

    def compute(rows):
        x_scr[0:rows, :] = _load_tile_rows(xbuf.at[slot], rows).astype(BF16)
        cw = 256
        for c in range(D_FF // cw):
            glu = (_dot(x_scr[0:rows, :], w1_scr[:, c * cw:(c + 1) * cw])
                   + b1_ref[:, c * cw:(c + 1) * cw])
            lin = (_dot(x_scr[0:rows, :], w1_scr[:, D_FF + c * cw:D_FF + (c + 1) * cw])
                   + b1_ref[:, D_FF + c * cw:D_FF + (c + 1) * cw])
            glu = jnp.minimum(glu, SWIGLU_LIMIT)
            lin = jnp.clip(lin, -SWIGLU_LIMIT, SWIGLU_LIMIT)
            act = glu * jax.nn.sigmoid(SWIGLU_ALPHA * glu) * (lin + 1.0)
            act_scr[0:rows, c * cw:(c + 1) * cw] = act.astype(BF16)
        for c in range(D_MODEL // cw):
            y = (_dot(act_scr[0:rows, :], w2_scr[:, c * cw:(c + 1) * cw])
                 + b2_ref[:, c * cw:(c + 1) * cw])
            for s in range(cw // LANES):
                t = c * (cw // LANES) + s
                y_ref[pl.ds(t, rows, stride=ROW_TILES), :] = y[:, s * LANES:(s + 1) * LANES]
        if rows < ROW_BLOCK:
            y_ref[rows * ROW_TILES:, :] = jnp.zeros(((ROW_BLOCK - rows) * ROW_TILES, LANES), F32)

    for m in range(1, ROW_BLOCK // SUB_ROWS + 1):
        @pl.when(nsub == m)
        def _(m=m):
            compute(m * SUB_ROWS)

    @pl.when(nsub == 0)
    def _():
        y_ref[...] = jnp.zeros_like(y_ref)


def _moe(block_e, nsub, next_e, xrow, xs, w1, b1, w2, b2, *, n_blocks):
    return pl.pallas_call(
        functools.partial(_moe_kernel, n_blocks=n_blocks),
        out_shape=jax.ShapeDtypeStruct((n_blocks * ROW_BLOCK * ROW_TILES, LANES), F32),
        grid_spec=pltpu.PrefetchScalarGridSpec(
            num_scalar_prefetch=4,
            grid=(n_blocks,),
            in_specs=[pl.BlockSpec(memory_space=pl.ANY),
                      pl.BlockSpec(memory_space=pl.ANY),
                      pl.BlockSpec((None, 1, 2 * D_FF), lambda i, be, ns, nx, xr: (be[i], 0, 0)),
                      pl.BlockSpec(memory_space=pl.ANY),
                      pl.BlockSpec((None, 1, D_MODEL), lambda i, be, ns, nx, xr: (be[i], 0, 0))],
            out_specs=pl.BlockSpec((ROW_BLOCK * ROW_TILES, LANES),
                                   lambda i, be, ns, nx, xr: (i, 0)),
            scratch_shapes=[pltpu.VMEM((D_MODEL, 2 * D_FF), F32),
                            pltpu.VMEM((D_FF, D_MODEL), F32),
                            pltpu.VMEM((D_MODEL, 2 * D_FF), BF16),
                            pltpu.VMEM((D_FF, D_MODEL), BF16),
                            pltpu.VMEM((2, ROW_BLOCK * ROW_TILES, LANES), F32),
                            pltpu.VMEM((ROW_BLOCK, D_MODEL), BF16),
                            pltpu.VMEM((ROW_BLOCK, D_FF), BF16),
                            pltpu.SemaphoreType.DMA((2,)),
                            pltpu.SemaphoreType.DMA((2,))]),
        compiler_params=_cparams(("arbitrary",)),
        name="moe_ffn",
    )(block_e, nsub, next_e, xrow, xs, w1, b1, w2, b2)


def _combine_kernel(pos_hbm, src_hbm, yb_hbm, gate_ref, h1_ref, g2_ref, nw_ref, o_ref,
                    pos_smem, src_smem, win, hbuf, sem_tab, sem_win, *, n_tiles):
    i = pl.program_id(0)
    slot = lax.rem(i, 2)
    tile = COMBINE_TILE
    n_asg = tile * TOP_K
    chunk = WIN_ROWS * ROW_TILES
    slot_rows = WIN_CHUNKS * chunk

    def table_copies(j, s):
        return (pltpu.make_async_copy(pos_hbm.at[j], pos_smem.at[pl.ds(s * n_asg, n_asg)],
                                      sem_tab.at[s]),
                pltpu.make_async_copy(src_hbm.at[j], src_smem.at[pl.ds(s * n_asg, n_asg)],
                                      sem_tab.at[s]))

    def fetch_window(s):
        def one(c, carry):
            src = pl.multiple_of(src_smem[s * n_asg + c], ROW_TILES)
            dst = pl.multiple_of(s * slot_rows + c * chunk, chunk)
            pltpu.make_async_copy(yb_hbm.at[pl.ds(src, chunk), :], win.at[pl.ds(dst, chunk), :],
                                  sem_win.at[s]).start()
            return carry
        lax.fori_loop(0, WIN_CHUNKS, one, 0, unroll=2)

    @pl.when(i == 0)
    def _():
        for cp in table_copies(0, 0):
            cp.start()
        for cp in table_copies(0, 0):
            cp.wait()
        fetch_window(0)
        if n_tiles > 1:
            for cp in table_copies(1, 1):
                cp.start()

    @pl.when(i + 1 < n_tiles)
    def _():
        for cp in table_copies(i + 1, 1 - slot):
            cp.wait()
        fetch_window(1 - slot)

    pltpu.make_async_copy(yb_hbm.at[pl.ds(0, slot_rows), :],
                          win.at[pl.ds(pl.multiple_of(slot * slot_rows, slot_rows), slot_rows), :],
                          sem_win.at[slot]).wait()

    g2 = g2_ref[...]
    base = slot * n_asg

    def token(t, carry):
        e0 = base + t * TOP_K
        acc = None
        for k in range(TOP_K):
            p = pl.multiple_of(pos_smem[e0 + k], ROW_TILES)
            term = gate_ref[pl.ds(t * TOP_K + k, 1), :] * win[pl.ds(p, ROW_TILES), :]
            acc = term if acc is None else acc + term
        r0 = pl.multiple_of(t * ROW_TILES, ROW_TILES)
        hbuf[pl.ds(r0, ROW_TILES), :] = h1_ref[pl.ds(r0, ROW_TILES), :] + g2 * acc
        return carry
    lax.fori_loop(0, tile, token, 0, unroll=4)

    @pl.when(i + 2 < n_tiles)
    def _():
        for cp in table_copies(i + 2, slot):
            cp.start()

    h = _load_tile_rows(hbuf, tile)
    ms = jnp.mean(h * h, axis=-1, keepdims=True)
    o_ref[...] = h * lax.rsqrt(ms + EPS) * nw_ref[...]


def _combine(pos_tiles, src_tiles, yb, gate_rows, h1, g2_tiles, nw, *, tiles_per_batch):
    tile = COMBINE_TILE
    T = h1.shape[0] // ROW_TILES
    n_asg = tile * TOP_K
    return pl.pallas_call(
        functools.partial(_combine_kernel, n_tiles=T // tile),
        out_shape=jax.ShapeDtypeStruct((T, D_MODEL), F32),
        grid=(T // tile,),
        in_specs=[pl.BlockSpec(memory_space=pl.ANY),
                  pl.BlockSpec(memory_space=pl.ANY),
                  pl.BlockSpec(memory_space=pl.ANY),
                  pl.BlockSpec((tile * TOP_K, LANES), lambda i: (i, 0)),
                  pl.BlockSpec((tile * ROW_TILES, LANES), lambda i: (i, 0)),
                  pl.BlockSpec((None, ROW_TILES, LANES), lambda i: (i // tiles_per_batch, 0, 0)),
                  pl.BlockSpec((1, D_MODEL), lambda i: (0, 0))],
        out_specs=pl.BlockSpec((tile, D_MODEL), lambda i: (i, 0)),
        scratch_shapes=[pltpu.SMEM((2 * n_asg,), I32),
                        pltpu.SMEM((2 * n_asg,), I32),
                        pltpu.VMEM((2 * WIN_CHUNKS * WIN_ROWS * ROW_TILES, LANES), F32),
                        pltpu.VMEM((tile * ROW_TILES, LANES), F32),
                        pltpu.SemaphoreType.DMA((2,)),
                        pltpu.SemaphoreType.DMA((2,))],
        compiler_params=_cparams(("arbitrary",)),
        name="combine_norm",
    )(pos_tiles, src_tiles, yb, gate_rows, h1, g2_tiles, nw)


def _rope_tables(n):
    rows = n // GRID_W
    row = jnp.repeat(jnp.arange(rows, dtype=F32), GRID_W)
    col = jnp.tile(jnp.arange(GRID_W, dtype=F32), rows)
    n_freq = RET_DK // 4
    inv = ROPE_BASE ** (-jnp.arange(n_freq, dtype=F32) / n_freq)
    ang = jnp.concatenate([row[:, None] * inv, col[:, None] * inv], axis=-1)
    cos, sin = jnp.cos(ang), jnp.sin(ang)
    cos_h = jnp.concatenate([cos, cos], axis=-1)
    sin_h = jnp.concatenate([-sin, sin], axis=-1)
    return jnp.tile(cos_h, (1, RET_HEADS)), jnp.tile(sin_h, (1, RET_HEADS))


def kernel(x, c, ctx, c_ctx, w_ada, b_ada, norm_mix_w, norm_ffn_w, w_in, w_out, hg_lb,
           hg_norm_w, router_w, router_b, w1, b1, w2, b2, norm_final_w):
    B, N, D = x.shape
    C = ctx.shape[1]
    T = B * N
    assert D == D_MODEL and w_ada.shape[0] == 1

    cc = jnp.concatenate([c.astype(F32), c_ctx.astype(F32)[None, :],
                          jnp.zeros((16 - B - 1, D), F32)], axis=0)
    mod = _ada(cc, w_ada[0], b_ada[0][None, :])
    mod = mod.reshape(16, 6, 1, D).transpose(1, 0, 2, 3)
    sh1, sc1, g1, sh2, sc2, g2 = (mod[i] for i in range(6))

    w_in_bf = w_in[0].astype(BF16)
    w_out_bf = w_out[0].astype(BF16)
    nw_mix = norm_mix_w[0][None, :]
    cos_t, sin_t = _rope_tables(N)

    pb, lf = _inproj(x, sc1, sh1, None, nw_mix, w_in_bf, hg_lb[:2], cos_t, sin_t,
                     rope=True, tm=1024)
    pbc, lfc = _inproj(ctx.reshape(1, B * C, D), sc1, sh1, B, nw_mix, w_in_bf, hg_lb[:2],
                       cos_t, sin_t, rope=False, tm=512)
    pbc = pbc.reshape(B, C, PROJ_W)
    lfc = lfc.reshape(B, C, 1024)

    a_ret = _retention(pb, pbc)
    a_hg = _hgrn(pb, lf, pbc, lfc, hg_norm_w[0][None, :])

    rw_t = router_w[0].T
    rwh = rw_t.astype(BF16)
    rwl = (rw_t - rwh.astype(F32)).astype(BF16)
    rb = jnp.broadcast_to(router_b[0][:, None], (N_EXPERTS, LANES))
    h1, v, route_t, gate_rows, base_cnt, cnt = _outproj(a_ret, a_hg, w_out_bf, x, g1, sc2, sh2,
                                                        norm_ffn_w[0][None, :], rwh, rwl, rb, tm=512)

    idx = route_t[0:TOP_K].astype(I32)
    rank = route_t[2 * TOP_K:3 * TOP_K].astype(I32)
    counts = cnt[:, 0].astype(I32)
    padded = (counts + ROW_BLOCK - 1) // ROW_BLOCK * ROW_BLOCK
    pend = jnp.cumsum(padded)
    pstart = pend - padded
    cstart = jnp.cumsum(counts) - counts
    onehot = idx[None] == jnp.arange(N_EXPERTS, dtype=I32)[:, None, None]

    def dest_tiles(first_row, tile):
        dest = (jnp.sum(jnp.where(onehot, first_row[:, None, None], 0), axis=0) + rank) * ROW_TILES
        return dest.reshape(TOP_K, T // tile, tile).transpose(1, 0, 2).reshape(T // tile,
                                                                               TOP_K * tile)
    n_blocks = (T * TOP_K) // ROW_BLOCK + N_EXPERTS
    starts = jnp.arange(n_blocks, dtype=I32) * ROW_BLOCK
    block_e = jnp.minimum(jnp.sum((pend[None, :] <= starts[:, None]).astype(I32), axis=1),
                          N_EXPERTS - 1)
    valid = jnp.clip(counts[block_e] - (starts - pstart[block_e]), 0, ROW_BLOCK)
    nsub = (valid + SUB_ROWS - 1) // SUB_ROWS
    first_from = lax.cummin(jnp.where(counts > 0, jnp.arange(N_EXPERTS, dtype=I32), N_EXPERTS),
                            reverse=True)
    next_e = jnp.concatenate([first_from[1:], jnp.full((1,), N_EXPERTS, I32)])
    next_e = jnp.where(next_e == N_EXPERTS, -1, next_e)
    xrow = jnp.where(nsub > 0, cstart[block_e] + starts - pstart[block_e], 0) * ROW_TILES

    tile_d = 512
    xs = _dispatch(dest_tiles(cstart, tile_d), v, tile=tile_d)
    yb = _moe(block_e, nsub, next_e, xrow.astype(I32), xs, w1[0], b1[0][:, None, :], w2[0],
              b2[0][:, None, :], n_blocks=n_blocks)

    tc = COMBINE_TILE
    n_ct = T // tc
    base = base_cnt.reshape(n_ct, N_EXPERTS, LANES)[:, :, 0].astype(I32)
    n_run = jnp.concatenate([base[1:], counts[None, :]], axis=0) - base
    n_chunk = (n_run + WIN_ROWS - 1) // WIN_ROWS
    chunk_end = jnp.cumsum(n_chunk, axis=1)
    chunk_start = chunk_end - n_chunk
    win_off = chunk_start * WIN_ROWS
    c_id = jnp.arange(WIN_CHUNKS, dtype=I32)
    c_exp = jnp.sum((chunk_end[:, None, :] <= c_id[None, :, None]).astype(I32), axis=2)
    c_real = c_exp < N_EXPERTS
    c_own = c_exp[:, :, None] == jnp.arange(N_EXPERTS, dtype=I32)
    take = lambda tab: jnp.sum(jnp.where(c_own, tab[:, None, :], 0), axis=2)
    c_src = (take(pstart[None, :] + base) + (c_id[None, :] - take(chunk_start)) * WIN_ROWS)
    src_tiles = jnp.concatenate(
        [jnp.where(c_real, c_src, 0) * ROW_TILES,
         jnp.zeros((n_ct, tc * TOP_K - WIN_CHUNKS), I32)], axis=1)
    slot_off = (jnp.arange(n_ct, dtype=I32) % 2) * (WIN_CHUNKS * WIN_ROWS)
    shift = jnp.repeat(win_off - base + slot_off[:, None], tc, axis=0).T
    pos = (jnp.sum(jnp.where(onehot, shift[:, None, :], 0), axis=0) + rank) * ROW_TILES

    def per_tile(a):
        return a.reshape(TOP_K, n_ct, tc).transpose(1, 2, 0).reshape(n_ct, tc * TOP_K)

    out = _combine(per_tile(pos), src_tiles, yb, gate_rows, h1,
                   g2.reshape(16, ROW_TILES, LANES), norm_final_w[None, :],
                   tiles_per_batch=N // tc)
    return out.reshape(B, N, D)
```

```python
import functools
import math

import jax
import jax.numpy as jnp
from jax import lax
from jax.experimental import pallas as pl
from jax.experimental.pallas import tpu as pltpu

F32 = jnp.float32
BF16 = jnp.bfloat16
I32 = jnp.int32

D_MODEL = 1024
GRID_W = 64
RET_HEADS = 4
RET_DK = 64
HG_HEADS = 4
PROJ_W = 4096
ROPE_BASE = 10000.0
EPS = 1e-6
N_EXPERTS = 32
TOP_K = 4
D_FF = 1024
SWIGLU_LIMIT = 7.0
SWIGLU_ALPHA = 1.702
LOG2_E = 1.0 / math.log(2.0)

LANES = 128
CHUNK = 128
INPROJ_TILE = 1024
DISPATCH_TILE = 512
ROW_BLOCK = 1024
SUB_ROWS = 256
COMBINE_TILE = 512
WIN_ROWS = 16
WIN_CHUNKS = (COMBINE_TILE * TOP_K + N_EXPERTS * (WIN_ROWS - 1) + WIN_ROWS - 1) // WIN_ROWS
VMEM_LIMIT = 56 * 1024 * 1024
ROW_TILES = D_MODEL // LANES

C_RQ, C_RK, C_RV, C_RG, C_HQ, C_FF, C_FB, C_HV, C_HG = (
    0, 256, 512, 1024, 1536, 2048, 2560, 3072, 3584)


def _cparams(sem):
    return pltpu.CompilerParams(dimension_semantics=sem, vmem_limit_bytes=VMEM_LIMIT)


def _split_bf16(x):
    hi = x.astype(BF16)
    lo = (x - hi.astype(F32)).astype(BF16)
    return hi, lo


def _dot(a, b):
    return jnp.dot(a, b, preferred_element_type=F32)


def _dot_nt(a, b):
    return lax.dot_general(a, b, (((1,), (1,)), ((), ())), preferred_element_type=F32)


def _dot3(a, b):
    ah, al = _split_bf16(a)
    bh, bl = _split_bf16(b)
    return _dot(ah, bh) + (_dot(ah, bl) + _dot(al, bh))


def _silu(x):
    return x * jax.nn.sigmoid(x)


def _load_tile_rows(ref, n):
    return jnp.concatenate([ref[pl.ds(s, n, stride=ROW_TILES), :] for s in range(ROW_TILES)],
                           axis=1)


def _store_tile_rows(ref, x):
    n = x.shape[0]
    for s in range(ROW_TILES):
        ref[pl.ds(s, n, stride=ROW_TILES), :] = x[:, s * LANES:(s + 1) * LANES]


def _ada_kernel(c_ref, w_ref, b_ref, o_ref):
    s = _silu(c_ref[...])
    o_ref[...] = _dot3(s, w_ref[...]) + b_ref[...]


def _ada(cc, w, b):
    nblk = w.shape[1] // D_MODEL
    return pl.pallas_call(
        _ada_kernel,
        out_shape=jax.ShapeDtypeStruct((cc.shape[0], w.shape[1]), F32),
        grid=(nblk,),
        in_specs=[pl.BlockSpec(cc.shape, lambda j: (0, 0)),
                  pl.BlockSpec((D_MODEL, D_MODEL), lambda j: (0, j)),
                  pl.BlockSpec((1, D_MODEL), lambda j: (0, j))],
        out_specs=pl.BlockSpec((cc.shape[0], D_MODEL), lambda j: (0, j)),
        compiler_params=_cparams(("arbitrary",)),
        name="ada",
    )(cc, w, b)


def _inproj_kernel(x_ref, sc_ref, sh_ref, nw_ref, w_ref, lb_ref, cos_ref, sin_ref,
                   pb_ref, lf_ref, u_scr, *, rope):
    x = x_ref[...]
    ms = jnp.mean(x * x, axis=-1, keepdims=True)
    u = x * lax.rsqrt(ms + EPS) * nw_ref[...] * (1.0 + sc_ref[...]) + sh_ref[...]
    u_scr[...] = u.astype(BF16)

    def proj(lo, width):
        return _dot(u_scr[...], w_ref[:, lo:lo + width])

    tm = x.shape[0]
    if rope:
        lane = lax.broadcasted_iota(I32, (tm, LANES), 1)
        first = (lane & 32) == 0

    def put_rot(col, scale):
        for j in range(2):
            lo = col + j * LANES
            t = proj(lo, LANES)
            if scale != 1.0:
                t = t * scale
            if rope:
                tb = j * LANES
                rot = jnp.where(first, pltpu.roll(t, 96, axis=1), pltpu.roll(t, 32, axis=1))
                t = t * cos_ref[:, tb:tb + LANES] + rot * sin_ref[:, tb:tb + LANES]
            pb_ref[:, lo:lo + LANES] = t.astype(BF16)

    put_rot(C_RQ, 1.0)
    put_rot(C_RK, RET_DK ** -0.5)
    pb_ref[:, C_RV:C_RV + 512] = proj(C_RV, 512).astype(BF16)
    pb_ref[:, C_RG:C_RG + 512] = _silu(proj(C_RG, 512)).astype(BF16)
    pb_ref[:, C_HQ:C_HQ + 512] = _silu(proj(C_HQ, 512)).astype(BF16)
    pb_ref[:, C_HV:C_HV + 512] = proj(C_HV, 512).astype(BF16)
    pb_ref[:, C_HG:C_HG + 512] = _silu(proj(C_HG, 512)).astype(BF16)

    la = lb_ref[0]
    lbb = lb_ref[1]
    mx = jnp.maximum(la, lbb)
    ea = jnp.exp(la - mx)
    eb = jnp.exp(lbb - mx)
    lb = ea / (ea + eb)
    for d, col in enumerate((C_FF, C_FB)):
        lbd = lb[d:d + 1, :]
        f = lbd + (1.0 - lbd) * jax.nn.sigmoid(proj(col, 512))
        pb_ref[:, col:col + 512] = (1.0 - f).astype(BF16)
        lf_ref[:, d * 512:(d + 1) * 512] = jnp.log(f) * LOG2_E


def _inproj(x, sc, sh, mod_row, nw, w_bf, hg_lb, cos_t, sin_t, *, rope, tm):
    B, n, _ = x.shape
    nt = n // tm
    if mod_row is None:
        mrow = lambda b, j: (b, 0, 0)
    else:
        mrow = lambda b, j: (mod_row, 0, 0)
    return pl.pallas_call(
        functools.partial(_inproj_kernel, rope=rope),
        out_shape=(jax.ShapeDtypeStruct((B, n, PROJ_W), BF16),
                   jax.ShapeDtypeStruct((B, n, 1024), F32)),
        grid=(B, nt),
        in_specs=[pl.BlockSpec((None, tm, D_MODEL), lambda b, j: (b, j, 0)),
                  pl.BlockSpec((None, 1, D_MODEL), mrow),
                  pl.BlockSpec((None, 1, D_MODEL), mrow),
                  pl.BlockSpec((1, D_MODEL), lambda b, j: (0, 0)),
                  pl.BlockSpec((D_MODEL, PROJ_W), lambda b, j: (0, 0),
                               pipeline_mode=pl.Buffered(1)),
                  pl.BlockSpec((2, 2, 512), lambda b, j: (0, 0, 0)),
                  pl.BlockSpec((tm, 256), lambda b, j: (j, 0)),
                  pl.BlockSpec((tm, 256), lambda b, j: (j, 0))],
        out_specs=(pl.BlockSpec((None, tm, PROJ_W), lambda b, j: (b, j, 0)),
                   pl.BlockSpec((None, tm, 1024), lambda b, j: (b, j, 0))),
        scratch_shapes=[pltpu.VMEM((tm, D_MODEL), BF16)],
        compiler_params=_cparams(("arbitrary", "arbitrary")),
        name="inproj_rope" if rope else "inproj_ctx",
    )(x, sc, sh, nw, w_bf, hg_lb, cos_t, sin_t)


_RET_LGF = [math.log1p(-(2.0 ** (-5.0 - 2.0 * h))) for h in range(RET_HEADS)]
_RET_LGB = [math.log1p(-(2.0 ** (-6.0 - 2.0 * h))) for h in range(RET_HEADS)]


def _ret_kernel(q_ref, k_ref, v_ref, g_ref, kc_ref, vc_ref, o_ref,
                u_scr, sin_scr, kt_scr, dtot_scr, *, n_lat, n_ctx):
    L = CHUNK
    pair = pl.program_id(1)
    row = lax.broadcasted_iota(I32, (L, L), 0).astype(F32)
    col = lax.broadcasted_iota(I32, (L, L), 1).astype(F32)
    lane = lax.broadcasted_iota(I32, (L, LANES), 1)
    trow = lax.broadcasted_iota(I32, (L, 1), 0).astype(F32)
    tcol = lax.broadcasted_iota(I32, (1, L), 1).astype(F32)
    low_half = lane < RET_DK

    def u_chunk(k_blk, v_blk, ci, store_kt):
        kt = k_blk.astype(F32).T
        if store_kt is not None:
            kt_scr[store_kt] = kt.astype(BF16)
        for hh in range(2):
            lgf = jnp.where(pair == 0, _RET_LGF[hh], _RET_LGF[2 + hh])
            lgb = jnp.where(pair == 0, _RET_LGB[hh], _RET_LGB[2 + hh])
            kth = kt[hh * RET_DK:(hh + 1) * RET_DK, :]
            wkf = jnp.exp(lgf * (L - 1.0 - tcol))
            wkb = jnp.exp(lgb * tcol)
            lhs = jnp.concatenate([kth * wkf, kth * wkb], axis=0).astype(BF16)
            u_scr[hh, ci] = _dot(lhs, v_blk[:, hh * LANES:(hh + 1) * LANES])

    for c in range(n_ctx):
        u_chunk(kc_ref[c * L:(c + 1) * L, :], vc_ref[c * L:(c + 1) * L, :], c, None)

    def lat_u(c, carry):
        r0 = pl.multiple_of(c * L, L)
        u_chunk(k_ref[pl.ds(r0, L), :], v_ref[pl.ds(r0, L), :], n_ctx + c, c)
        return carry
    lax.fori_loop(0, n_lat, lat_u, 0, unroll=8)

    ones = jnp.ones((RET_DK, LANES), F32)
    for hh in range(2):
        lgf = jnp.where(pair == 0, _RET_LGF[hh], _RET_LGF[2 + hh])
        lgb = jnp.where(pair == 0, _RET_LGB[hh], _RET_LGB[2 + hh])
        d = row - col
        dtot_scr[hh] = jnp.where(d > 0, jnp.exp(lgf * jnp.maximum(d, 0.0)),
                                 jnp.where(d < 0, jnp.exp(lgb * jnp.maximum(-d, 0.0)), 2.0))
        af = jnp.exp(ones * (lgf * L))
        ab = jnp.exp(ones * (lgb * L))

        s = jnp.zeros((RET_DK, LANES), F32)
        for c in range(n_ctx):
            s = af * s + u_scr[hh, c, 0:RET_DK, :]
        sb = jnp.zeros((RET_DK, LANES), F32)
        for c in reversed(range(n_ctx)):
            sb = ab * sb + u_scr[hh, c, RET_DK:2 * RET_DK, :]

        def fwd(c, s, hh=hh, af=af):
            sin_scr[hh, c, 0:RET_DK, :] = s.astype(BF16)
            return af * s + u_scr[hh, n_ctx + c, 0:RET_DK, :]
        lax.fori_loop(0, n_lat, fwd, s)

        def bwd(i, sb, hh=hh, ab=ab):
            c = n_lat - 1 - i
            sin_scr[hh, c, RET_DK:2 * RET_DK, :] = sb.astype(BF16)
            return ab * sb + u_scr[hh, n_ctx + c, RET_DK:2 * RET_DK, :]
        lax.fori_loop(0, n_lat, bwd, sb)

    def out_chunk(c, carry):
        r0 = pl.multiple_of(c * L, L)
        q = q_ref[pl.ds(r0, L), :].astype(F32)
        qr = pltpu.roll(q, RET_DK, axis=1)
        kt = kt_scr[c]
        for hh in range(2):
            lgf = jnp.where(pair == 0, _RET_LGF[hh], _RET_LGF[2 + hh])
            lgb = jnp.where(pair == 0, _RET_LGB[hh], _RET_LGB[2 + hh])
            mine = low_half if hh == 0 else jnp.logical_not(low_half)
            qm = jnp.where(mine, q, 0.0).astype(BF16)
            p = (_dot(qm, kt) * dtot_scr[hh]).astype(BF16)
            vh = v_ref[pl.ds(r0, L), hh * LANES:(hh + 1) * LANES]
            wqf = jnp.exp(lgf * (trow + 1.0))
            wqb = jnp.exp(lgb * (L - trow))
            qa, qb = (q, qr) if hh == 0 else (qr, q)
            qs = jnp.where(low_half, qa * wqf, qb * wqb).astype(BF16)
            o = _dot(p, vh) + _dot(qs, sin_scr[hh, c])
            ms = jnp.mean(o * o, axis=-1, keepdims=True)
            gh = g_ref[pl.ds(r0, L), hh * LANES:(hh + 1) * LANES].astype(F32)
            o_ref[pl.ds(r0, L), hh * LANES:(hh + 1) * LANES] = (
                o * lax.rsqrt(ms + EPS) * gh).astype(BF16)
        return carry
    lax.fori_loop(0, n_lat, out_chunk, 0, unroll=8)


def _retention(pb, pbc):
    B, n, _ = pb.shape
    nc = pbc.shape[1]
    n_lat, n_ctx = n // CHUNK, nc // CHUNK
    return pl.pallas_call(
        functools.partial(_ret_kernel, n_lat=n_lat, n_ctx=n_ctx),
        out_shape=jax.ShapeDtypeStruct((B, n, 512), BF16),
        grid=(B, 2),
        in_specs=[pl.BlockSpec((None, n, LANES), lambda b, p: (b, 0, C_RQ // LANES + p)),
                  pl.BlockSpec((None, n, LANES), lambda b, p: (b, 0, C_RK // LANES + p)),
                  pl.BlockSpec((None, n, 256), lambda b, p: (b, 0, C_RV // 256 + p)),
                  pl.BlockSpec((None, n, 256), lambda b, p: (b, 0, C_RG // 256 + p)),
                  pl.BlockSpec((None, nc, LANES), lambda b, p: (b, 0, C_RK // LANES + p)),
                  pl.BlockSpec((None, nc, 256), lambda b, p: (b, 0, C_RV // 256 + p))],
        out_specs=pl.BlockSpec((None, n, 256), lambda b, p: (b, 0, p)),
        scratch_shapes=[pltpu.VMEM((2, n_lat + n_ctx, CHUNK, LANES), F32),
                        pltpu.VMEM((2, n_lat, CHUNK, LANES), BF16),
                        pltpu.VMEM((n_lat, LANES, CHUNK), BF16),
                        pltpu.VMEM((2, CHUNK, CHUNK), F32)],
        compiler_params=_cparams(("arbitrary", "arbitrary")),
        name="retention",
    )(pb, pb, pb, pb, pbc, pbc)


_LEVELS = (64, 32, 16, 8, 4, 2, 1)


def _expand_rows(r, rep):
    n = r.shape[0]
    if n == 1:
        return jnp.broadcast_to(r, (rep, r.shape[1]))
    return jnp.concatenate(
        [jnp.broadcast_to(r[i:i + 1, :], (rep, r.shape[1])) for i in range(n)], axis=0)


def _hgrn_kernel(q_ref, kf_ref, kb_ref, v_ref, g_ref, lff_ref, lfb_ref,
                 kfc_ref, kbc_ref, vc_ref, lffc_ref, lfbc_ref, nw_ref, o_ref,
                 ut_scr, a_scr, qs_scr, oi_scr, sin_scr, bfb_scr, *, n_lat, n_ctx):
    L = CHUNK
    row = lax.broadcasted_iota(I32, (L, L), 0)
    col = lax.broadcasted_iota(I32, (L, L), 1)
    xr_bits = lax.bitcast_convert_type((row ^ col).astype(F32), I32)
    lv = lax.shift_right_logical(xr_bits, 23) - 127
    row2 = lax.broadcasted_iota(I32, (L, 2 * L), 0)
    col2 = lax.broadcasted_iota(I32, (L, 2 * L), 1) & (L - 1)
    tril2 = jnp.where(col2 <= row2, 1.0, 0.0).astype(BF16)
    triu2 = jnp.where(col2 >= row2, 1.0, 0.0).astype(BF16)

    def cums(lff, lfb):
        hf, lof = _split_bf16(lff)
        hb, lob = _split_bf16(lfb)
        bf = _dot(tril2, jnp.concatenate([hf, lof], axis=0))
        bb = _dot(triu2, jnp.concatenate([hb, lob], axis=0))
        return bf, bb

    def state_part(ci, kf, kb, v_blk, bf, bb):
        endf = bf[L - 1:L, :]
        endb = bb[0:1, :]
        ksf = kf * jnp.exp2(endf - bf)
        ksb = kb * jnp.exp2(endb - bb)
        vt = v_blk.astype(F32).T.astype(BF16)
        ut_scr[ci] = _dot(vt, jnp.concatenate([ksf, ksb], axis=1).astype(BF16))
        a_scr[ci] = jnp.broadcast_to(
            jnp.concatenate([jnp.exp2(endf), jnp.exp2(endb)], axis=1), (8, 2 * LANES))

    for c in range(n_ctx):
        sl = slice(c * L, (c + 1) * L)
        bf, bb = cums(lffc_ref[sl, :], lfbc_ref[sl, :])
        state_part(c, kfc_ref[sl, :].astype(F32), kbc_ref[sl, :].astype(F32),
                   vc_ref[sl, :], bf, bb)

    def lat_chunk(c, u):
        rows = pl.ds(pl.multiple_of(c * L, L), L)
        bf_scr = bfb_scr.at[u, 0]
        bb_scr = bfb_scr.at[u, 1]
        q = q_ref[rows, :].astype(F32)
        kf = kf_ref[rows, :].astype(F32)
        kb = kb_ref[rows, :].astype(F32)
        v_blk = v_ref[rows, :]
        lff = lff_ref[rows, :]
        lfb = lfb_ref[rows, :]
        bf, bb = cums(lff, lfb)
        state_part(n_ctx + c, kf, kb, v_blk, bf, bb)
        qs_scr[rows, :] = jnp.concatenate([q * jnp.exp2(bf), q * jnp.exp2(bb)],
                                          axis=1).astype(BF16)
        bf_scr[...] = bf
        bb_scr[...] = bb

        acc = jnp.zeros((L, L), F32)
        for lvl, h in enumerate(_LEVELS):
            bit = (row & h) != 0
            ksel = jnp.where(bit, kb, kf)
            if h >= 4:
                n = (L // 2) // h
                if n == 1:
                    rf = bf_scr[h - 1:h, :]
                    rb = bb_scr[h:h + 1, :]
                else:
                    rf = bf_scr[pl.ds(h - 1, n, stride=2 * h), :]
                    rb = bb_scr[pl.ds(h, n, stride=2 * h), :]
                df = bf - _expand_rows(rf, 2 * h)
                db = bb - _expand_rows(rb, 2 * h)
                eq = jnp.where(bit, df, db)
                ek = -jnp.where(bit, db, df)
            elif h == 2:
                m = row & 3
                lff_n = pltpu.roll(lff, L - 1, axis=0)
                lfb_n = pltpu.roll(lfb, L - 1, axis=0)
                eq = jnp.where(m == 2, lff,
                               jnp.where(m == 3, lff + pltpu.roll(lff, 1, axis=0),
                                         jnp.where(m == 0, lfb + lfb_n, lfb)))
                ek = jnp.where(m == 3, pltpu.roll(lfb, 1, axis=0),
                               jnp.where(m == 0, lff_n, 0.0))
            else:
                eq = jnp.where(bit, lff, lfb)
                ek = None
            lhs = (q * jnp.exp2(eq)).astype(BF16)
            rhs = (ksel if ek is None else ksel * jnp.exp2(ek)).astype(BF16)
            acc = jnp.where(lv == 6 - lvl, _dot_nt(lhs, rhs), acc)

        dsum = jnp.sum(q * (kf + kb), axis=-1, keepdims=True)
        oi_scr[rows, :] = _dot(acc.astype(BF16), v_blk) + dsum * v_blk.astype(F32)

    def lat_pair(i, carry):
        for u in range(4):
            lat_chunk(4 * i + u, u)
        return carry
    lax.fori_loop(0, n_lat // 4, lat_pair, 0)

    st = jnp.zeros((LANES, LANES), F32)
    for c in range(n_ctx):
        st = st * a_scr[c, 0:1, 0:LANES] + ut_scr[c, :, 0:LANES]
    stb = jnp.zeros((LANES, LANES), F32)
    for c in reversed(range(n_ctx)):
        stb = stb * a_scr[c, 0:1, LANES:2 * LANES] + ut_scr[c, :, LANES:2 * LANES]

    def fwd(c, st):
        sin_scr[c, :, 0:LANES] = st.astype(BF16)
        ci = n_ctx + c
        return st * a_scr[ci, 0:1, 0:LANES] + ut_scr[ci, :, 0:LANES]
    lax.fori_loop(0, n_lat, fwd, st)

    def bwd(i, stb):
        c = n_lat - 1 - i
        sin_scr[c, :, LANES:2 * LANES] = stb.astype(BF16)
        ci = n_ctx + c
        return stb * a_scr[ci, 0:1, LANES:2 * LANES] + ut_scr[ci, :, LANES:2 * LANES]
    lax.fori_loop(0, n_lat, bwd, stb)

    def out_chunk(c, carry):
        rows = pl.ds(pl.multiple_of(c * L, L), L)
        o = oi_scr[rows, :] + _dot_nt(qs_scr[rows, :], sin_scr[c])
        ms = jnp.mean(o * o, axis=-1, keepdims=True)
        y = o * lax.rsqrt(ms + EPS) * nw_ref[...] * g_ref[rows, :].astype(F32)
        o_ref[rows, :] = y.astype(BF16)
        return carry
    lax.fori_loop(0, n_lat, out_chunk, 0, unroll=8)


def _hgrn(pb, lf, pbc, lfc, nw):
    B, n, _ = pb.shape
    nc = pbc.shape[1]
    n_lat, n_ctx = n // CHUNK, nc // CHUNK

    def colblk(rows, col0):
        return pl.BlockSpec((None, rows, LANES), lambda b, h: (b, 0, col0 // LANES + h))

    return pl.pallas_call(
        functools.partial(_hgrn_kernel, n_lat=n_lat, n_ctx=n_ctx),
        out_shape=jax.ShapeDtypeStruct((B, n, 512), BF16),
        grid=(B, HG_HEADS),
        in_specs=[colblk(n, C_HQ), colblk(n, C_FF), colblk(n, C_FB), colblk(n, C_HV),
                  colblk(n, C_HG), colblk(n, 0), colblk(n, 512),
                  colblk(nc, C_FF), colblk(nc, C_FB), colblk(nc, C_HV),
                  colblk(nc, 0), colblk(nc, 512),
                  pl.BlockSpec((1, LANES), lambda b, h: (0, 0))],
        out_specs=pl.BlockSpec((None, n, LANES), lambda b, h: (b, 0, h)),
        scratch_shapes=[pltpu.VMEM((n_lat + n_ctx, LANES, 2 * LANES), F32),
                        pltpu.VMEM((n_lat + n_ctx, 8, 2 * LANES), F32),
                        pltpu.VMEM((n, 2 * LANES), BF16),
                        pltpu.VMEM((n, LANES), F32),
                        pltpu.VMEM((n_lat, LANES, 2 * LANES), BF16),
                        pltpu.VMEM((4, 2, CHUNK, LANES), F32)],
        compiler_params=_cparams(("arbitrary", "arbitrary")),
        name="hgrn2",
    )(pb, pb, pb, pb, pb, lf, lf, pbc, pbc, pbc, lfc, lfc, nw)


def _outproj_kernel(ar_ref, ah_ref, w_ref, x_ref, g1_ref, sc_ref, sh_ref, nw_ref,
                    rwh_ref, rwl_ref, rb_ref, tri_ref, h1_ref, v_ref, route_t_ref, gate_ref, base_ref,
                    cnt_ref, cnt_scr):
    first_step = jnp.logical_and(pl.program_id(0) == 0, pl.program_id(1) == 0)

    @pl.when(first_step)
    def _():
        cnt_scr[...] = jnp.zeros_like(cnt_scr)

    y = _dot(ar_ref[...], w_ref[0:512, :]) + _dot(ah_ref[...], w_ref[512:1024, :])
    h1 = x_ref[...] + g1_ref[...] * y
    _store_tile_rows(h1_ref, h1)
    ms = jnp.mean(h1 * h1, axis=-1, keepdims=True)
    v = h1 * lax.rsqrt(ms + EPS) * nw_ref[...] * (1.0 + sc_ref[...]) + sh_ref[...]
    _store_tile_rows(v_ref, v)

    tm = v.shape[0]
    vh, vl = _split_bf16(v)
    rwh, rwl = rwh_ref[...], rwl_ref[...]
    l = (_dot_nt(rwh, vh) + (_dot_nt(rwh, vl) + _dot_nt(rwl, vh))) + rb_ref[:, 0:1]
    row_f = lax.broadcasted_iota(I32, (N_EXPERTS, tm), 0).astype(F32)
    sels, tops, idxs = [], [], []
    for _ in range(TOP_K):
        m = jnp.max(l, axis=0, keepdims=True)
        i = jnp.min(jnp.where(l == m, row_f, float(N_EXPERTS)), axis=0, keepdims=True)
        sel = row_f == i
        l = jnp.where(sel, -jnp.inf, l)
        sels.append(sel)
        tops.append(m)
        idxs.append(i)
    es = [jnp.exp(t - tops[0]) for t in tops]
    den = es[0] + es[1] + es[2] + es[3]
    gates = [e / den for e in es]

    oh = jnp.zeros((N_EXPERTS, tm), F32)
    for sel in sels:
        oh = jnp.where(sel, 1.0, oh)
    cnt = cnt_scr[:, 0:1]
    before = _dot(oh.astype(BF16), tri_ref[...]) + cnt
    ranks = [jnp.sum(jnp.where(sel, before, 0.0), axis=0, keepdims=True) for sel in sels]
    base_ref[...] = cnt_scr[...]
    cnt_scr[...] = cnt_scr[...] + jnp.sum(oh, axis=1, keepdims=True)
    cnt_ref[...] = cnt_scr[...]

    zero4 = jnp.zeros((TOP_K, tm), F32)
    route_t_ref[...] = jnp.concatenate(idxs + [zero4] + ranks + [zero4], axis=0)
    g_cols = jnp.concatenate(gates + [zero4], axis=0).T
    for k in range(TOP_K):
        gate_ref[pl.ds(k, tm, stride=TOP_K), :] = jnp.broadcast_to(g_cols[:, k:k + 1], (tm, LANES))


def _outproj(a_ret, a_hg, w_bf, x, g1, sc2, sh2, nw, rwh, rwl, rb, *, tm):
    B, n, _ = x.shape
    nt = n // tm
    T = B * n
    assert tm == COMBINE_TILE
    mrow = lambda b, j: (b, 0, 0)
    tok = lambda b, j: (b * nt + j, 0)
    const = lambda b, j: (0, 0)
    tri = jnp.triu(jnp.ones((tm, tm), BF16), 1)
    return pl.pallas_call(
        _outproj_kernel,
        out_shape=(jax.ShapeDtypeStruct((T * ROW_TILES, LANES), F32),
                   jax.ShapeDtypeStruct((T * ROW_TILES, LANES), F32),
                   jax.ShapeDtypeStruct((16, T), F32),
                   jax.ShapeDtypeStruct((T * TOP_K, LANES), F32),
                   jax.ShapeDtypeStruct((B * nt * N_EXPERTS, LANES), F32),
                   jax.ShapeDtypeStruct((N_EXPERTS, LANES), F32)),
        grid=(B, nt),
        in_specs=[pl.BlockSpec((None, tm, 512), lambda b, j: (b, j, 0)),
                  pl.BlockSpec((None, tm, 512), lambda b, j: (b, j, 0)),
                  pl.BlockSpec((D_MODEL, D_MODEL), const),
                  pl.BlockSpec((None, tm, D_MODEL), lambda b, j: (b, j, 0)),
                  pl.BlockSpec((None, 1, D_MODEL), mrow),
                  pl.BlockSpec((None, 1, D_MODEL), mrow),
                  pl.BlockSpec((None, 1, D_MODEL), mrow),
                  pl.BlockSpec((1, D_MODEL), const),
                  pl.BlockSpec((N_EXPERTS, D_MODEL), const),
                  pl.BlockSpec((N_EXPERTS, D_MODEL), const),
                  pl.BlockSpec((N_EXPERTS, LANES), const),
                  pl.BlockSpec((tm, tm), const)],
        out_specs=(pl.BlockSpec((tm * ROW_TILES, LANES), tok),
                   pl.BlockSpec((tm * ROW_TILES, LANES), tok),
                   pl.BlockSpec((16, tm), lambda b, j: (0, b * nt + j)),
                   pl.BlockSpec((tm * TOP_K, LANES), tok),
                   pl.BlockSpec((N_EXPERTS, LANES), tok),
                   pl.BlockSpec((N_EXPERTS, LANES), const)),
        scratch_shapes=[pltpu.VMEM((N_EXPERTS, LANES), F32)],
        compiler_params=_cparams(("arbitrary", "arbitrary")),
        name="outproj_router",
    )(a_ret, a_hg, w_bf, x, g1, sc2, sh2, nw, rwh, rwl, rb, tri)


def _dispatch_kernel(dest_hbm, v_ref, xs_hbm, idx_smem, zbuf, sem_idx, sem_z, sem_rows,
                     *, tile, n_tiles, n_rows):
    i = pl.program_id(0)
    slot = lax.rem(i, 2)

    def idx_copy(j, s):
        n_asg = tile * TOP_K
        return pltpu.make_async_copy(dest_hbm.at[j], idx_smem.at[pl.ds(s * n_asg, n_asg)],
                                     sem_idx.at[s])

    def zero_copy(j):
        r0 = (n_rows + j * SUB_ROWS) * ROW_TILES
        return pltpu.make_async_copy(zbuf, xs_hbm.at[pl.ds(r0, SUB_ROWS * ROW_TILES), :], sem_z)

    @pl.when(i == 0)
    def _():
        idx_copy(0, 0).start()
        zbuf[...] = jnp.zeros_like(zbuf)
        for j in range(ROW_BLOCK // SUB_ROWS):
            zero_copy(j).start()
        for j in range(ROW_BLOCK // SUB_ROWS):
            zero_copy(j).wait()

    idx_copy(i, slot).wait()

    @pl.when(i + 1 < n_tiles)
    def _():
        idx_copy(i + 1, 1 - slot).start()

    def issue(t, carry):
        e0 = slot * (tile * TOP_K) + t
        for k in range(TOP_K):
            d = pl.multiple_of(idx_smem[e0 + k * tile], ROW_TILES)
            src = v_ref.at[pl.ds(pl.multiple_of(t * ROW_TILES, ROW_TILES), ROW_TILES), :]
            pltpu.make_async_copy(src, xs_hbm.at[pl.ds(d, ROW_TILES), :],
                                  sem_rows).start(priority=k % 2)
        return carry
    lax.fori_loop(0, tile, issue, 0, unroll=4)
    for _ in range(TOP_K):
        pltpu.make_async_copy(v_ref, xs_hbm.at[pl.ds(0, tile * ROW_TILES), :], sem_rows).wait()


def _dispatch(dest_tiles, v, *, tile):
    T = v.shape[0] // ROW_TILES
    n_rows = T * TOP_K
    return pl.pallas_call(
        functools.partial(_dispatch_kernel, tile=tile, n_tiles=T // tile, n_rows=n_rows),
        out_shape=jax.ShapeDtypeStruct(((n_rows + ROW_BLOCK) * ROW_TILES, LANES), F32),
        grid=(T // tile,),
        in_specs=[pl.BlockSpec(memory_space=pl.ANY),
                  pl.BlockSpec((tile * ROW_TILES, LANES), lambda i: (i, 0))],
        out_specs=pl.BlockSpec(memory_space=pl.ANY),
        scratch_shapes=[pltpu.SMEM((2 * tile * TOP_K,), I32),
                        pltpu.VMEM((SUB_ROWS * ROW_TILES, LANES), F32),
                        pltpu.SemaphoreType.DMA((2,)),
                        pltpu.SemaphoreType.DMA,
                        pltpu.SemaphoreType.DMA],
        compiler_params=_cparams(("arbitrary",)),
        name="dispatch",
    )(dest_tiles, v)


def _moe_kernel(be_ref, ns_ref, nx_ref, xr_ref, x_hbm, w1_hbm, b1_ref, w2_hbm, b2_ref, y_ref,
                w1_stage, w2_stage, w1_scr, w2_scr, xbuf, x_scr, act_scr, sem_w, sem_x,
                *, n_blocks):
    i = pl.program_id(0)
    slot = lax.rem(i, 2)
    e = be_ref[i]
    nsub = ns_ref[i]
    changed = jnp.logical_or(i == 0, e != be_ref[jnp.maximum(i - 1, 0)])

    def x_copy(j, s):
        r0 = pl.multiple_of(xr_ref[j], ROW_TILES)
        return pltpu.make_async_copy(x_hbm.at[pl.ds(r0, ROW_BLOCK * ROW_TILES), :], xbuf.at[s],
                                     sem_x.at[s])

    @pl.when(i == 0)
    def _():
        x_copy(0, 0).start()

    @pl.when(nsub > 0)
    def _():
        x_copy(i, slot).wait()

    nxt_blk = jnp.minimum(i + 1, n_blocks - 1)

    @pl.when(jnp.logical_and(i + 1 < n_blocks, ns_ref[nxt_blk] > 0))
    def _():
        x_copy(nxt_blk, 1 - slot).start()

    def weight_copies(ex):
        return (pltpu.make_async_copy(w1_hbm.at[ex], w1_stage, sem_w.at[0]),
                pltpu.make_async_copy(w2_hbm.at[ex], w2_stage, sem_w.at[1]))

    @pl.when(i == 0)
    def _():
        for cp in weight_copies(e):
            cp.start()

    @pl.when(jnp.logical_and(changed, nsub > 0))
    def _():
        for cp in weight_copies(e):
            cp.wait()
        w1_scr[...] = w1_stage[...].astype(BF16)
        w2_scr[...] = w2_stage[...].astype(BF16)
        nxt = nx_ref[e]

        @pl.when(nxt >= 0)
        def _():
            for cp in weight_copies(nxt):
                cp.start()

    def compute(rows):
        x_scr[0:rows, :] = _load_tile_rows(xbuf.at[slot], rows).astype(BF16)
        cw = 256
        for c in range(D_FF // cw):
            glu = (_dot(x_scr[0:rows, :], w1_scr[:, c * cw:(c + 1) * cw])
                   + b1_ref[:, c * cw:(c + 1) * cw])
            lin = (_dot(x_scr[0:rows, :], w1_scr[:, D_FF + c * cw:D_FF + (c + 1) * cw])
                   + b1_ref[:, D_FF + c * cw:D_FF + (c + 1) * cw])
            glu = jnp.minimum(glu, SWIGLU_LIMIT)
            lin = jnp.clip(lin, -SWIGLU_LIMIT, SWIGLU_LIMIT)
            act = glu * jax.nn.sigmoid(SWIGLU_ALPHA * glu) * (lin + 1.0)
            act_scr[0:rows, c * cw:(c + 1) * cw] = act.astype(BF16)
        for c in range(D_MODEL // cw):
            y = (_dot(act_scr[0:rows, :], w2_scr[:, c * cw:(c + 1) * cw])
                 + b2_ref[:, c * cw:(c + 1) * cw])
            for s in range(cw // LANES):
                t = c * (cw // LANES) + s
                y_ref[pl.ds(t, rows, stride=ROW_TILES), :] = y[:, s * LANES:(s + 1) * LANES]
        if rows < ROW_BLOCK:
            y_ref[rows * ROW_TILES:, :] = jnp.zeros(((ROW_BLOCK - rows) * ROW_TILES, LANES), F32)

    for m in range(1, ROW_BLOCK // SUB_ROWS + 1):
        @pl.when(nsub == m)
        def _(m=m):
            compute(m * SUB_ROWS)

    @pl.when(nsub == 0)
    def _():
        y_ref[...] = jnp.zeros_like(y_ref)


def _moe(block_e, nsub, next_e, xrow, xs, w1, b1, w2, b2, *, n_blocks):
    return pl.pallas_call(
        functools.partial(_moe_kernel, n_blocks=n_blocks),
        out_shape=jax.ShapeDtypeStruct((n_blocks * ROW_BLOCK * ROW_TILES, LANES), F32),
        grid_spec=pltpu.PrefetchScalarGridSpec(
            num_scalar_prefetch=4,
            grid=(n_blocks,),
            in_specs=[pl.BlockSpec(memory_space=pl.ANY),
                      pl.BlockSpec(memory_space=pl.ANY),
                      pl.BlockSpec((None, 1, 2 * D_FF), lambda i, be, ns, nx, xr: (be[i], 0, 0)),
                      pl.BlockSpec(memory_space=pl.ANY),
                      pl.BlockSpec((None, 1, D_MODEL), lambda i, be, ns, nx, xr: (be[i], 0, 0))],
            out_specs=pl.BlockSpec((ROW_BLOCK * ROW_TILES, LANES),
                                   lambda i, be, ns, nx, xr: (i, 0)),
            scratch_shapes=[pltpu.VMEM((D_MODEL, 2 * D_FF), F32),
                            pltpu.VMEM((D_FF, D_MODEL), F32),
                            pltpu.VMEM((D_MODEL, 2 * D_FF), BF16),
                            pltpu.VMEM((D_FF, D_MODEL), BF16),
                            pltpu.VMEM((2, ROW_BLOCK * ROW_TILES, LANES), F32),
                            pltpu.VMEM((ROW_BLOCK, D_MODEL), BF16),
                            pltpu.VMEM((ROW_BLOCK, D_FF), BF16),
                            pltpu.SemaphoreType.DMA((2,)),
                            pltpu.SemaphoreType.DMA((2,))]),
        compiler_params=_cparams(("arbitrary",)),
        name="moe_ffn",
    )(block_e, nsub, next_e, xrow, xs, w1, b1, w2, b2)


def _combine_kernel(pos_hbm, src_hbm, yb_hbm, gate_ref, h1_ref, g2_ref, nw_ref, o_ref,
                    pos_smem, src_smem, win, hbuf, sem_tab, sem_win, *, n_tiles):
    i = pl.program_id(0)
    slot = lax.rem(i, 2)
    tile = COMBINE_TILE
    n_asg = tile * TOP_K
    chunk = WIN_ROWS * ROW_TILES
    slot_rows = WIN_CHUNKS * chunk

    def table_copies(j, s):
        return (pltpu.make_async_copy(pos_hbm.at[j], pos_smem.at[pl.ds(s * n_asg, n_asg)],
                                      sem_tab.at[s]),
                pltpu.make_async_copy(src_hbm.at[j], src_smem.at[pl.ds(s * n_asg, n_asg)],
                                      sem_tab.at[s]))

    def fetch_window(s):
        def one(c, carry):
            src = pl.multiple_of(src_smem[s * n_asg + c], ROW_TILES)
            dst = pl.multiple_of(s * slot_rows + c * chunk, chunk)
            pltpu.make_async_copy(yb_hbm.at[pl.ds(src, chunk), :], win.at[pl.ds(dst, chunk), :],
                                  sem_win.at[s]).start()
            return carry
        lax.fori_loop(0, WIN_CHUNKS, one, 0, unroll=2)

    @pl.when(i == 0)
    def _():
        for cp in table_copies(0, 0):
            cp.start()
        for cp in table_copies(0, 0):
            cp.wait()
        fetch_window(0)
        if n_tiles > 1:
            for cp in table_copies(1, 1):
                cp.start()

    @pl.when(i + 1 < n_tiles)
    def _():
        for cp in table_copies(i + 1, 1 - slot):
            cp.wait()
        fetch_window(1 - slot)

    pltpu.make_async_copy(yb_hbm.at[pl.ds(0, slot_rows), :],
                          win.at[pl.ds(pl.multiple_of(slot * slot_rows, slot_rows), slot_rows), :],
                          sem_win.at[slot]).wait()

    g2 = g2_ref[...]
    base = slot * n_asg

    def token(t, carry):
        e0 = base + t * TOP_K
        acc = None
        for k in range(TOP_K):
            p = pl.multiple_of(pos_smem[e0 + k], ROW_TILES)
            term = gate_ref[pl.ds(t * TOP_K + k, 1), :] * win[pl.ds(p, ROW_TILES), :]
            acc = term if acc is None else acc + term
        r0 = pl.multiple_of(t * ROW_TILES, ROW_TILES)
        hbuf[pl.ds(r0, ROW_TILES), :] = h1_ref[pl.ds(r0, ROW_TILES), :] + g2 * acc
        return carry
    lax.fori_loop(0, tile, token, 0, unroll=4)

    @pl.when(i + 2 < n_tiles)
    def _():
        for cp in table_copies(i + 2, slot):
            cp.start()

    h = _load_tile_rows(hbuf, tile)
    ms = jnp.mean(h * h, axis=-1, keepdims=True)
    o_ref[...] = h * lax.rsqrt(ms + EPS) * nw_ref[...]


def _combine(pos_tiles, src_tiles, yb, gate_rows, h1, g2_tiles, nw, *, tiles_per_batch):
    tile = COMBINE_TILE
    T = h1.shape[0] // ROW_TILES
    n_asg = tile * TOP_K
    return pl.pallas_call(
        functools.partial(_combine_kernel, n_tiles=T // tile),
        out_shape=jax.ShapeDtypeStruct((T, D_MODEL), F32),
        grid=(T // tile,),
        in_specs=[pl.BlockSpec(memory_space=pl.ANY),
                  pl.BlockSpec(memory_space=pl.ANY),
                  pl.BlockSpec(memory_space=pl.ANY),
                  pl.BlockSpec((tile * TOP_K, LANES), lambda i: (i, 0)),
                  pl.BlockSpec((tile * ROW_TILES, LANES), lambda i: (i, 0)),
                  pl.BlockSpec((None, ROW_TILES, LANES), lambda i: (i // tiles_per_batch, 0, 0)),
                  pl.BlockSpec((1, D_MODEL), lambda i: (0, 0))],
        out_specs=pl.BlockSpec((tile, D_MODEL), lambda i: (i, 0)),
        scratch_shapes=[pltpu.SMEM((2 * n_asg,), I32),
                        pltpu.SMEM((2 * n_asg,), I32),
                        pltpu.VMEM((2 * WIN_CHUNKS * WIN_ROWS * ROW_TILES, LANES), F32),
                        pltpu.VMEM((tile * ROW_TILES, LANES), F32),
                        pltpu.SemaphoreType.DMA((2,)),
                        pltpu.SemaphoreType.DMA((2,))],
        compiler_params=_cparams(("arbitrary",)),
        name="combine_norm",
    )(pos_tiles, src_tiles, yb, gate_rows, h1, g2_tiles, nw)


def _rope_tables(n):
    rows = n // GRID_W
    row = jnp.repeat(jnp.arange(rows, dtype=F32), GRID_W)
    col = jnp.tile(jnp.arange(GRID_W, dtype=F32), rows)
    n_freq = RET_DK // 4
    inv = ROPE_BASE ** (-jnp.arange(n_freq, dtype=F32) / n_freq)
    ang = jnp.concatenate([row[:, None] * inv, col[:, None] * inv], axis=-1)
    cos, sin = jnp.cos(ang), jnp.sin(ang)
    cos_h = jnp.concatenate([cos, cos], axis=-1)
    sin_h = jnp.concatenate([-sin, sin], axis=-1)
    return jnp.tile(cos_h, (1, RET_HEADS)), jnp.tile(sin_h, (1, RET_HEADS))


def kernel(x, c, ctx, c_ctx, w_ada, b_ada, norm_mix_w, norm_ffn_w, w_in, w_out, hg_lb,
           hg_norm_w, router_w, router_b, w1, b1, w2, b2, norm_final_w):
    B, N, D = x.shape
    C = ctx.shape[1]
    T = B * N
    assert D == D_MODEL and w_ada.shape[0] == 1

    cc = jnp.concatenate([c.astype(F32), c_ctx.astype(F32)[None, :],
                          jnp.zeros((16 - B - 1, D), F32)], axis=0)
    mod = _ada(cc, w_ada[0], b_ada[0][None, :])
    mod = mod.reshape(16, 6, 1, D).transpose(1, 0, 2, 3)
    sh1, sc1, g1, sh2, sc2, g2 = (mod[i] for i in range(6))

    w_in_bf = w_in[0].astype(BF16)
    w_out_bf = w_out[0].astype(BF16)
    nw_mix = norm_mix_w[0][None, :]
    cos_t, sin_t = _rope_tables(N)

    pb, lf = _inproj(x, sc1, sh1, None, nw_mix, w_in_bf, hg_lb[:2], cos_t, sin_t,
                     rope=True, tm=INPROJ_TILE)
    pbc, lfc = _inproj(ctx, sc1, sh1, B, nw_mix, w_in_bf, hg_lb[:2], cos_t[:C], sin_t[:C],
                       rope=False, tm=C)

    a_ret = _retention(pb, pbc)
    a_hg = _hgrn(pb, lf, pbc, lfc, hg_norm_w[0][None, :])

    rw_t = router_w[0].T
    rwh = rw_t.astype(BF16)
    rwl = (rw_t - rwh.astype(F32)).astype(BF16)
    rb = jnp.broadcast_to(router_b[0][:, None], (N_EXPERTS, LANES))
    h1, v, route_t, gate_rows, base_cnt, cnt = _outproj(a_ret, a_hg, w_out_bf, x, g1, sc2, sh2,
                                                        norm_ffn_w[0][None, :], rwh, rwl, rb,
                                                        tm=COMBINE_TILE)

    idx = route_t[0:TOP_K].astype(I32)
    rank = route_t[2 * TOP_K:3 * TOP_K].astype(I32)
    counts = cnt[:, 0].astype(I32)
    padded = (counts + ROW_BLOCK - 1) // ROW_BLOCK * ROW_BLOCK
    pend = jnp.cumsum(padded)
    pstart = pend - padded
    cstart = jnp.cumsum(counts) - counts
    onehot = idx[None] == jnp.arange(N_EXPERTS, dtype=I32)[:, None, None]

    def dest_tiles(first_row, tile):
        dest = (jnp.sum(jnp.where(onehot, first_row[:, None, None], 0), axis=0) + rank) * ROW_TILES
        return dest.reshape(TOP_K, T // tile, tile).transpose(1, 0, 2).reshape(T // tile,
                                                                               TOP_K * tile)
    n_blocks = (T * TOP_K) // ROW_BLOCK + N_EXPERTS
    starts = jnp.arange(n_blocks, dtype=I32) * ROW_BLOCK
    block_e = jnp.minimum(jnp.sum((pend[None, :] <= starts[:, None]).astype(I32), axis=1),
                          N_EXPERTS - 1)
    valid = jnp.clip(counts[block_e] - (starts - pstart[block_e]), 0, ROW_BLOCK)
    nsub = (valid + SUB_ROWS - 1) // SUB_ROWS
    first_from = lax.cummin(jnp.where(counts > 0, jnp.arange(N_EXPERTS, dtype=I32), N_EXPERTS),
                            reverse=True)
    next_e = jnp.concatenate([first_from[1:], jnp.full((1,), N_EXPERTS, I32)])
    next_e = jnp.where(next_e == N_EXPERTS, -1, next_e)
    xrow = jnp.where(nsub > 0, cstart[block_e] + starts - pstart[block_e], 0) * ROW_TILES

    xs = _dispatch(dest_tiles(cstart, DISPATCH_TILE), v, tile=DISPATCH_TILE)
    yb = _moe(block_e, nsub, next_e, xrow.astype(I32), xs, w1[0], b1[0][:, None, :], w2[0],
              b2[0][:, None, :], n_blocks=n_blocks)

    tc = COMBINE_TILE
    n_ct = T // tc
    base = base_cnt.reshape(n_ct, N_EXPERTS, LANES)[:, :, 0].astype(I32)
    n_run = jnp.concatenate([base[1:], counts[None, :]], axis=0) - base
    n_chunk = (n_run + WIN_ROWS - 1) // WIN_ROWS
    chunk_end = jnp.cumsum(n_chunk, axis=1)
    chunk_start = chunk_end - n_chunk
    win_off = chunk_start * WIN_ROWS
    c_id = jnp.arange(WIN_CHUNKS, dtype=I32)
    c_exp = jnp.sum((chunk_end[:, None, :] <= c_id[None, :, None]).astype(I32), axis=2)
    c_real = c_exp < N_EXPERTS
    c_own = c_exp[:, :, None] == jnp.arange(N_EXPERTS, dtype=I32)
    take = lambda tab: jnp.sum(jnp.where(c_own, tab[:, None, :], 0), axis=2)
    c_src = (take(pstart[None, :] + base) + (c_id[None, :] - take(chunk_start)) * WIN_ROWS)
    src_tiles = jnp.concatenate(
        [jnp.where(c_real, c_src, 0) * ROW_TILES,
         jnp.zeros((n_ct, tc * TOP_K - WIN_CHUNKS), I32)], axis=1)
    slot_off = (jnp.arange(n_ct, dtype=I32) % 2) * (WIN_CHUNKS * WIN_ROWS)
    shift = jnp.repeat(win_off - base + slot_off[:, None], tc, axis=0).T
    pos = (jnp.sum(jnp.where(onehot, shift[:, None, :], 0), axis=0) + rank) * ROW_TILES

    def per_tile(a):
        return a.reshape(TOP_K, n_ct, tc).transpose(1, 2, 0).reshape(n_ct, tc * TOP_K)

    out = _combine(per_tile(pos), src_tiles, yb, gate_rows, h1,
                   g2.reshape(16, ROW_TILES, LANES), norm_final_w[None, :],
                   tiles_per_batch=N // tc)
    return out.reshape(B, N, D)
```

```python
import functools
import math

import jax
import jax.numpy as jnp
from jax import lax
from jax.experimental import pallas as pl
from jax.experimental.pallas import tpu as pltpu

F32 = jnp.float32
BF16 = jnp.bfloat16
I32 = jnp.int32

D_MODEL = 1024
GRID_W = 64
RET_HEADS = 4
RET_DK = 64
HG_HEADS = 4
PROJ_W = 4096
ROPE_BASE = 10000.0
EPS = 1e-6
N_EXPERTS = 32
TOP_K = 4
D_FF = 1024
SWIGLU_LIMIT = 7.0
SWIGLU_ALPHA = 1.702
LOG2_E = 1.0 / math.log(2.0)

LANES = 128
CHUNK = 128
INPROJ_TILE = 1024
DISPATCH_TILE = 512
ROW_BLOCK = 1024
SUB_ROWS = 256
COMBINE_TILE = 512
WIN_ROWS = 16
WIN_CHUNKS = (COMBINE_TILE * TOP_K + N_EXPERTS * (WIN_ROWS - 1) + WIN_ROWS - 1) // WIN_ROWS
VMEM_LIMIT = 56 * 1024 * 1024
ROW_TILES = D_MODEL // LANES

C_RQ, C_RK, C_RV, C_RG, C_HQ, C_FF, C_FB, C_HV, C_HG = (
    0, 256, 512, 1024, 1536, 2048, 2560, 3072, 3584)


def _cparams(sem):
    return pltpu.CompilerParams(dimension_semantics=sem, vmem_limit_bytes=VMEM_LIMIT)


def _split_bf16(x):
    hi = x.astype(BF16)
    lo = (x - hi.astype(F32)).astype(BF16)
    return hi, lo


def _dot(a, b):
    return jnp.dot(a, b, preferred_element_type=F32)


def _dot_nt(a, b):
    return lax.dot_general(a, b, (((1,), (1,)), ((), ())), preferred_element_type=F32)


def _dot3(a, b):
    ah, al = _split_bf16(a)
    bh, bl = _split_bf16(b)
    return _dot(ah, bh) + (_dot(ah, bl) + _dot(al, bh))


def _silu(x):
    return x * jax.nn.sigmoid(x)


def _load_tile_rows(ref, n):
    return jnp.concatenate([ref[pl.ds(s, n, stride=ROW_TILES), :] for s in range(ROW_TILES)],
                           axis=1)


def _store_tile_rows(ref, x):
    n = x.shape[0]
    for s in range(ROW_TILES):
        ref[pl.ds(s, n, stride=ROW_TILES), :] = x[:, s * LANES:(s + 1) * LANES]


def _ada_kernel(c_ref, w_ref, b_ref, o_ref):
    s = _silu(c_ref[...])
    o_ref[...] = _dot3(s, w_ref[...]) + b_ref[...]


def _ada(cc, w, b):
    nblk = w.shape[1] // D_MODEL
    return pl.pallas_call(
        _ada_kernel,
        out_shape=jax.ShapeDtypeStruct((cc.shape[0], w.shape[1]), F32),
        grid=(nblk,),
        in_specs=[pl.BlockSpec(cc.shape, lambda j: (0, 0)),
                  pl.BlockSpec((D_MODEL, D_MODEL), lambda j: (0, j)),
                  pl.BlockSpec((1, D_MODEL), lambda j: (0, j))],
        out_specs=pl.BlockSpec((cc.shape[0], D_MODEL), lambda j: (0, j)),
        compiler_params=_cparams(("arbitrary",)),
        name="ada",
    )(cc, w, b)


def _inproj_kernel(x_ref, sc_ref, sh_ref, nw_ref, w_ref, lb_ref, cos_ref, sin_ref,
                   pb_ref, lf_ref, u_scr, *, rope):
    x = x_ref[...]
    ms = jnp.mean(x * x, axis=-1, keepdims=True)
    u = x * lax.rsqrt(ms + EPS) * nw_ref[...] * (1.0 + sc_ref[...]) + sh_ref[...]
    u_scr[...] = u.astype(BF16)

    def proj(lo, width):
        return _dot(u_scr[...], w_ref[:, lo:lo + width])

    tm = x.shape[0]
    if rope:
        lane = lax.broadcasted_iota(I32, (tm, LANES), 1)
        first = (lane & 32) == 0

    def put_rot(col, scale):
        for j in range(2):
            lo = col + j * LANES
            t = proj(lo, LANES)
            if scale != 1.0:
                t = t * scale
            if rope:
                tb = j * LANES
                rot = jnp.where(first, pltpu.roll(t, 96, axis=1), pltpu.roll(t, 32, axis=1))
                t = t * cos_ref[:, tb:tb + LANES] + rot * sin_ref[:, tb:tb + LANES]
            pb_ref[:, lo:lo + LANES] = t.astype(BF16)

    put_rot(C_RQ, 1.0)
    put_rot(C_RK, RET_DK ** -0.5)
    pb_ref[:, C_RV:C_RV + 512] = proj(C_RV, 512).astype(BF16)
    pb_ref[:, C_RG:C_RG + 512] = _silu(proj(C_RG, 512)).astype(BF16)
    pb_ref[:, C_HQ:C_HQ + 512] = _silu(proj(C_HQ, 512)).astype(BF16)
    pb_ref[:, C_HV:C_HV + 512] = proj(C_HV, 512).astype(BF16)
    pb_ref[:, C_HG:C_HG + 512] = _silu(proj(C_HG, 512)).astype(BF16)

    la = lb_ref[0]
    lbb = lb_ref[1]
    mx = jnp.maximum(la, lbb)
    ea = jnp.exp(la - mx)
    eb = jnp.exp(lbb - mx)
    lb = ea / (ea + eb)
    for d, col in enumerate((C_FF, C_FB)):
        lbd = lb[d:d + 1, :]
        f = lbd + (1.0 - lbd) * jax.nn.sigmoid(proj(col, 512))
        pb_ref[:, col:col + 512] = (1.0 - f).astype(BF16)
        lf_ref[:, d * 512:(d + 1) * 512] = jnp.log(f) * LOG2_E


def _inproj(x, sc, sh, mod_row, nw, w_bf, hg_lb, cos_t, sin_t, *, rope, tm):
    B, n, _ = x.shape
    nt = n // tm
    if mod_row is None:
        mrow = lambda b, j: (b, 0, 0)
    else:
        mrow = lambda b, j: (mod_row, 0, 0)
    return pl.pallas_call(
        functools.partial(_inproj_kernel, rope=rope),
        out_shape=(jax.ShapeDtypeStruct((B, n, PROJ_W), BF16),
                   jax.ShapeDtypeStruct((B, n, 1024), F32)),
        grid=(B, nt),
        in_specs=[pl.BlockSpec((None, tm, D_MODEL), lambda b, j: (b, j, 0)),
                  pl.BlockSpec((None, 1, D_MODEL), mrow),
                  pl.BlockSpec((None, 1, D_MODEL), mrow),
                  pl.BlockSpec((1, D_MODEL), lambda b, j: (0, 0)),
                  pl.BlockSpec((D_MODEL, PROJ_W), lambda b, j: (0, 0),
                               pipeline_mode=pl.Buffered(1)),
                  pl.BlockSpec((2, 2, 512), lambda b, j: (0, 0, 0)),
                  pl.BlockSpec((tm, 256), lambda b, j: (j, 0)),
                  pl.BlockSpec((tm, 256), lambda b, j: (j, 0))],
        out_specs=(pl.BlockSpec((None, tm, PROJ_W), lambda b, j: (b, j, 0)),
                   pl.BlockSpec((None, tm, 1024), lambda b, j: (b, j, 0))),
        scratch_shapes=[pltpu.VMEM((tm, D_MODEL), BF16)],
        compiler_params=_cparams(("arbitrary", "arbitrary")),
        name="inproj_rope" if rope else "inproj_ctx",
    )(x, sc, sh, nw, w_bf, hg_lb, cos_t, sin_t)


_RET_LGF = [math.log1p(-(2.0 ** (-5.0 - 2.0 * h))) for h in range(RET_HEADS)]
_RET_LGB = [math.log1p(-(2.0 ** (-6.0 - 2.0 * h))) for h in range(RET_HEADS)]


def _ret_kernel(q_ref, k_ref, v_ref, g_ref, kc_ref, vc_ref, o_ref,
                u_scr, sin_scr, kt_scr, dtot_scr, *, n_lat, n_ctx):
    L = CHUNK
    pair = pl.program_id(1)
    row = lax.broadcasted_iota(I32, (L, L), 0).astype(F32)
    col = lax.broadcasted_iota(I32, (L, L), 1).astype(F32)
    lane = lax.broadcasted_iota(I32, (L, LANES), 1)
    trow = lax.broadcasted_iota(I32, (L, 1), 0).astype(F32)
    tcol = lax.broadcasted_iota(I32, (1, L), 1).astype(F32)
    low_half = lane < RET_DK

    def u_chunk(k_blk, v_blk, ci, store_kt):
        kt = k_blk.astype(F32).T
        if store_kt is not None:
            kt_scr[store_kt] = kt.astype(BF16)
        for hh in range(2):
            lgf = jnp.where(pair == 0, _RET_LGF[hh], _RET_LGF[2 + hh])
            lgb = jnp.where(pair == 0, _RET_LGB[hh], _RET_LGB[2 + hh])
            kth = kt[hh * RET_DK:(hh + 1) * RET_DK, :]
            wkf = jnp.exp(lgf * (L - 1.0 - tcol))
            wkb = jnp.exp(lgb * tcol)
            lhs = jnp.concatenate([kth * wkf, kth * wkb], axis=0).astype(BF16)
            u_scr[hh, ci] = _dot(lhs, v_blk[:, hh * LANES:(hh + 1) * LANES])

    for c in range(n_ctx):
        u_chunk(kc_ref[c * L:(c + 1) * L, :], vc_ref[c * L:(c + 1) * L, :], c, None)

    def lat_u(c, carry):
        r0 = pl.multiple_of(c * L, L)
        u_chunk(k_ref[pl.ds(r0, L), :], v_ref[pl.ds(r0, L), :], n_ctx + c, c)
        return carry
    lax.fori_loop(0, n_lat, lat_u, 0, unroll=8)

    ones = jnp.ones((RET_DK, LANES), F32)
    for hh in range(2):
        lgf = jnp.where(pair == 0, _RET_LGF[hh], _RET_LGF[2 + hh])
        lgb = jnp.where(pair == 0, _RET_LGB[hh], _RET_LGB[2 + hh])
        d = row - col
        dtot_scr[hh] = jnp.where(d > 0, jnp.exp(lgf * jnp.maximum(d, 0.0)),
                                 jnp.where(d < 0, jnp.exp(lgb * jnp.maximum(-d, 0.0)), 2.0))
        af = jnp.exp(ones * (lgf * L))
        ab = jnp.exp(ones * (lgb * L))

        s = jnp.zeros((RET_DK, LANES), F32)
        for c in range(n_ctx):
            s = af * s + u_scr[hh, c, 0:RET_DK, :]
        sb = jnp.zeros((RET_DK, LANES), F32)
        for c in reversed(range(n_ctx)):
            sb = ab * sb + u_scr[hh, c, RET_DK:2 * RET_DK, :]

        def fwd(c, s, hh=hh, af=af):
            sin_scr[hh, c, 0:RET_DK, :] = s.astype(BF16)
            return af * s + u_scr[hh, n_ctx + c, 0:RET_DK, :]
        lax.fori_loop(0, n_lat, fwd, s)

        def bwd(i, sb, hh=hh, ab=ab):
            c = n_lat - 1 - i
            sin_scr[hh, c, RET_DK:2 * RET_DK, :] = sb.astype(BF16)
            return ab * sb + u_scr[hh, n_ctx + c, RET_DK:2 * RET_DK, :]
        lax.fori_loop(0, n_lat, bwd, sb)

    def out_chunk(c, carry):
        r0 = pl.multiple_of(c * L, L)
        q = q_ref[pl.ds(r0, L), :].astype(F32)
        qr = pltpu.roll(q, RET_DK, axis=1)
        kt = kt_scr[c]
        for hh in range(2):
            lgf = jnp.where(pair == 0, _RET_LGF[hh], _RET_LGF[2 + hh])
            lgb = jnp.where(pair == 0, _RET_LGB[hh], _RET_LGB[2 + hh])
            mine = low_half if hh == 0 else jnp.logical_not(low_half)
            qm = jnp.where(mine, q, 0.0).astype(BF16)
            p = (_dot(qm, kt) * dtot_scr[hh]).astype(BF16)
            vh = v_ref[pl.ds(r0, L), hh * LANES:(hh + 1) * LANES]
            wqf = jnp.exp(lgf * (trow + 1.0))
            wqb = jnp.exp(lgb * (L - trow))
            qa, qb = (q, qr) if hh == 0 else (qr, q)
            qs = jnp.where(low_half, qa * wqf, qb * wqb).astype(BF16)
            o = _dot(p, vh) + _dot(qs, sin_scr[hh, c])
            ms = jnp.mean(o * o, axis=-1, keepdims=True)
            gh = g_ref[pl.ds(r0, L), hh * LANES:(hh + 1) * LANES].astype(F32)
            o_ref[pl.ds(r0, L), hh * LANES:(hh + 1) * LANES] = (
                o * lax.rsqrt(ms + EPS) * gh).astype(BF16)
        return carry
    lax.fori_loop(0, n_lat, out_chunk, 0, unroll=8)


def _retention(pb, pbc):
    B, n, _ = pb.shape
    nc = pbc.shape[1]
    n_lat, n_ctx = n // CHUNK, nc // CHUNK
    return pl.pallas_call(
        functools.partial(_ret_kernel, n_lat=n_lat, n_ctx=n_ctx),
        out_shape=jax.ShapeDtypeStruct((B, n, 512), BF16),
        grid=(B, 2),
        in_specs=[pl.BlockSpec((None, n, LANES), lambda b, p: (b, 0, C_RQ // LANES + p)),
                  pl.BlockSpec((None, n, LANES), lambda b, p: (b, 0, C_RK // LANES + p)),
                  pl.BlockSpec((None, n, 256), lambda b, p: (b, 0, C_RV // 256 + p)),
                  pl.BlockSpec((None, n, 256), lambda b, p: (b, 0, C_RG // 256 + p)),
                  pl.BlockSpec((None, nc, LANES), lambda b, p: (b, 0, C_RK // LANES + p)),
                  pl.BlockSpec((None, nc, 256), lambda b, p: (b, 0, C_RV // 256 + p))],
        out_specs=pl.BlockSpec((None, n, 256), lambda b, p: (b, 0, p)),
        scratch_shapes=[pltpu.VMEM((2, n_lat + n_ctx, CHUNK, LANES), F32),
                        pltpu.VMEM((2, n_lat, CHUNK, LANES), BF16),
                        pltpu.VMEM((n_lat, LANES, CHUNK), BF16),
                        pltpu.VMEM((2, CHUNK, CHUNK), F32)],
        compiler_params=_cparams(("arbitrary", "arbitrary")),
        name="retention",
    )(pb, pb, pb, pb, pbc, pbc)


_LEVELS = (64, 32, 16, 8, 4, 2, 1)


def _expand_rows(r, rep):
    n = r.shape[0]
    if n == 1:
        return jnp.broadcast_to(r, (rep, r.shape[1]))
    return jnp.concatenate(
        [jnp.broadcast_to(r[i:i + 1, :], (rep, r.shape[1])) for i in range(n)], axis=0)


def _hgrn_kernel(q_ref, kf_ref, kb_ref, v_ref, g_ref, lff_ref, lfb_ref,
                 kfc_ref, kbc_ref, vc_ref, lffc_ref, lfbc_ref, nw_ref, o_ref,
                 ut_scr, a_scr, qs_scr, oi_scr, sin_scr, bfb_scr, *, n_lat, n_ctx):
    L = CHUNK
    row = lax.broadcasted_iota(I32, (L, L), 0)
    col = lax.broadcasted_iota(I32, (L, L), 1)
    xr_bits = lax.bitcast_convert_type((row ^ col).astype(F32), I32)
    lv = lax.shift_right_logical(xr_bits, 23) - 127
    row2 = lax.broadcasted_iota(I32, (L, 2 * L), 0)
    col2 = lax.broadcasted_iota(I32, (L, 2 * L), 1) & (L - 1)
    tril2 = jnp.where(col2 <= row2, 1.0, 0.0).astype(BF16)
    triu2 = jnp.where(col2 >= row2, 1.0, 0.0).astype(BF16)

    def cums(lff, lfb):
        hf, lof = _split_bf16(lff)
        hb, lob = _split_bf16(lfb)
        bf = _dot(tril2, jnp.concatenate([hf, lof], axis=0))
        bb = _dot(triu2, jnp.concatenate([hb, lob], axis=0))
        return bf, bb

    def state_part(ci, kf, kb, v_blk, bf, bb):
        endf = bf[L - 1:L, :]
        endb = bb[0:1, :]
        ksf = kf * jnp.exp2(endf - bf)
        ksb = kb * jnp.exp2(endb - bb)
        vt = v_blk.astype(F32).T.astype(BF16)
        ut_scr[ci] = _dot(vt, jnp.concatenate([ksf, ksb], axis=1).astype(BF16))
        a_scr[ci] = jnp.broadcast_to(
            jnp.concatenate([jnp.exp2(endf), jnp.exp2(endb)], axis=1), (8, 2 * LANES))

    for c in range(n_ctx):
        sl = slice(c * L, (c + 1) * L)
        bf, bb = cums(lffc_ref[sl, :], lfbc_ref[sl, :])
        state_part(c, kfc_ref[sl, :].astype(F32), kbc_ref[sl, :].astype(F32),
                   vc_ref[sl, :], bf, bb)

    def lat_chunk(c, u):
        rows = pl.ds(pl.multiple_of(c * L, L), L)
        bf_scr = bfb_scr.at[u, 0]
        bb_scr = bfb_scr.at[u, 1]
        q = q_ref[rows, :].astype(F32)
        kf = kf_ref[rows, :].astype(F32)
        kb = kb_ref[rows, :].astype(F32)
        v_blk = v_ref[rows, :]
        lff = lff_ref[rows, :]
        lfb = lfb_ref[rows, :]
        bf, bb = cums(lff, lfb)
        state_part(n_ctx + c, kf, kb, v_blk, bf, bb)
        qs_scr[rows, :] = jnp.concatenate([q * jnp.exp2(bf), q * jnp.exp2(bb)],
                                          axis=1).astype(BF16)
        bf_scr[...] = bf
        bb_scr[...] = bb

        acc = jnp.zeros((L, L), F32)
        for lvl, h in enumerate(_LEVELS):
            bit = (row & h) != 0
            ksel = jnp.where(bit, kb, kf)
            if h >= 4:
                n = (L // 2) // h
                if n == 1:
                    rf = bf_scr[h - 1:h, :]
                    rb = bb_scr[h:h + 1, :]
                else:
                    rf = bf_scr[pl.ds(h - 1, n, stride=2 * h), :]
                    rb = bb_scr[pl.ds(h, n, stride=2 * h), :]
                df = bf - _expand_rows(rf, 2 * h)
                db = bb - _expand_rows(rb, 2 * h)
                eq = jnp.where(bit, df, db)
                ek = -jnp.where(bit, db, df)
            elif h == 2:
                m = row & 3
                lff_n = pltpu.roll(lff, L - 1, axis=0)
                lfb_n = pltpu.roll(lfb, L - 1, axis=0)
                eq = jnp.where(m == 2, lff,
                               jnp.where(m == 3, lff + pltpu.roll(lff, 1, axis=0),
                                         jnp.where(m == 0, lfb + lfb_n, lfb)))
                ek = jnp.where(m == 3, pltpu.roll(lfb, 1, axis=0),
                               jnp.where(m == 0, lff_n, 0.0))
            else:
                eq = jnp.where(bit, lff, lfb)
                ek = None
            lhs = (q * jnp.exp2(eq)).astype(BF16)
            rhs = (ksel if ek is None else ksel * jnp.exp2(ek)).astype(BF16)
            acc = jnp.where(lv == 6 - lvl, _dot_nt(lhs, rhs), acc)

        dsum = jnp.sum(q * (kf + kb), axis=-1, keepdims=True)
        oi_scr[rows, :] = _dot(acc.astype(BF16), v_blk) + dsum * v_blk.astype(F32)

    def lat_pair(i, carry):
        for u in range(8):
            lat_chunk(8 * i + u, u)
        return carry
    lax.fori_loop(0, n_lat // 8, lat_pair, 0)

    st = jnp.zeros((LANES, LANES), F32)
    for c in range(n_ctx):
        st = st * a_scr[c, 0:1, 0:LANES] + ut_scr[c, :, 0:LANES]
    stb = jnp.zeros((LANES, LANES), F32)
    for c in reversed(range(n_ctx)):
        stb = stb * a_scr[c, 0:1, LANES:2 * LANES] + ut_scr[c, :, LANES:2 * LANES]

    def fwd(c, st):
        sin_scr[c, :, 0:LANES] = st.astype(BF16)
        ci = n_ctx + c
        return st * a_scr[ci, 0:1, 0:LANES] + ut_scr[ci, :, 0:LANES]
    lax.fori_loop(0, n_lat, fwd, st)

    def bwd(i, stb):
        c = n_lat - 1 - i
        sin_scr[c, :, LANES:2 * LANES] = stb.astype(BF16)
        ci = n_ctx + c
        return stb * a_scr[ci, 0:1, LANES:2 * LANES] + ut_scr[ci, :, LANES:2 * LANES]
    lax.fori_loop(0, n_lat, bwd, stb)

    def out_chunk(c, carry):
        rows = pl.ds(pl.multiple_of(c * L, L), L)
        o = oi_scr[rows, :] + _dot_nt(qs_scr[rows, :], sin_scr[c])
        ms = jnp.mean(o * o, axis=-1, keepdims=True)
        y = o * lax.rsqrt(ms + EPS) * nw_ref[...] * g_ref[rows, :].astype(F32)
        o_ref[rows, :] = y.astype(BF16)
        return carry
    lax.fori_loop(0, n_lat, out_chunk, 0, unroll=8)


def _hgrn(pb, lf, pbc, lfc, nw):
    B, n, _ = pb.shape
    nc = pbc.shape[1]
    n_lat, n_ctx = n // CHUNK, nc // CHUNK

    def colblk(rows, col0):
        return pl.BlockSpec((None, rows, LANES), lambda b, h: (b, 0, col0 // LANES + h))

    return pl.pallas_call(
        functools.partial(_hgrn_kernel, n_lat=n_lat, n_ctx=n_ctx),
        out_shape=jax.ShapeDtypeStruct((B, n, 512), BF16),
        grid=(B, HG_HEADS),
        in_specs=[colblk(n, C_HQ), colblk(n, C_FF), colblk(n, C_FB), colblk(n, C_HV),
                  colblk(n, C_HG), colblk(n, 0), colblk(n, 512),
                  colblk(nc, C_FF), colblk(nc, C_FB), colblk(nc, C_HV),
                  colblk(nc, 0), colblk(nc, 512),
                  pl.BlockSpec((1, LANES), lambda b, h: (0, 0))],
        out_specs=pl.BlockSpec((None, n, LANES), lambda b, h: (b, 0, h)),
        scratch_shapes=[pltpu.VMEM((n_lat + n_ctx, LANES, 2 * LANES), F32),
                        pltpu.VMEM((n_lat + n_ctx, 8, 2 * LANES), F32),
                        pltpu.VMEM((n, 2 * LANES), BF16),
                        pltpu.VMEM((n, LANES), F32),
                        pltpu.VMEM((n_lat, LANES, 2 * LANES), BF16),
                        pltpu.VMEM((8, 2, CHUNK, LANES), F32)],
        compiler_params=_cparams(("arbitrary", "arbitrary")),
        name="hgrn2",
    )(pb, pb, pb, pb, pb, lf, lf, pbc, pbc, pbc, lfc, lfc, nw)


def _outproj_kernel(ar_ref, ah_ref, w_ref, x_ref, g1_ref, sc_ref, sh_ref, nw_ref,
                    rwh_ref, rwl_ref, rb_ref, tri_ref, h1_ref, v_ref, route_t_ref, gate_ref, base_ref,
                    cnt_ref, cnt_scr):
    first_step = jnp.logical_and(pl.program_id(0) == 0, pl.program_id(1) == 0)

    @pl.when(first_step)
    def _():
        cnt_scr[...] = jnp.zeros_like(cnt_scr)

    y = _dot(ar_ref[...], w_ref[0:512, :]) + _dot(ah_ref[...], w_ref[512:1024, :])
    h1 = x_ref[...] + g1_ref[...] * y
    _store_tile_rows(h1_ref, h1)
    ms = jnp.mean(h1 * h1, axis=-1, keepdims=True)
    v = h1 * lax.rsqrt(ms + EPS) * nw_ref[...] * (1.0 + sc_ref[...]) + sh_ref[...]
    _store_tile_rows(v_ref, v)

    tm = v.shape[0]
    vh, vl = _split_bf16(v)
    rwh, rwl = rwh_ref[...], rwl_ref[...]
    l = (_dot_nt(rwh, vh) + (_dot_nt(rwh, vl) + _dot_nt(rwl, vh))) + rb_ref[:, 0:1]
    row_f = lax.broadcasted_iota(I32, (N_EXPERTS, tm), 0).astype(F32)
    sels, tops, idxs = [], [], []
    for _ in range(TOP_K):
        m = jnp.max(l, axis=0, keepdims=True)
        i = jnp.min(jnp.where(l == m, row_f, float(N_EXPERTS)), axis=0, keepdims=True)
        sel = row_f == i
        l = jnp.where(sel, -jnp.inf, l)
        sels.append(sel)
        tops.append(m)
        idxs.append(i)
    es = [jnp.exp(t - tops[0]) for t in tops]
    den = es[0] + es[1] + es[2] + es[3]
    gates = [e / den for e in es]

    oh = jnp.zeros((N_EXPERTS, tm), F32)
    for sel in sels:
        oh = jnp.where(sel, 1.0, oh)
    cnt = cnt_scr[:, 0:1]
    before = _dot(oh.astype(BF16), tri_ref[...]) + cnt
    ranks = [jnp.sum(jnp.where(sel, before, 0.0), axis=0, keepdims=True) for sel in sels]
    base_ref[...] = cnt_scr[...]
    cnt_scr[...] = cnt_scr[...] + jnp.sum(oh, axis=1, keepdims=True)
    cnt_ref[...] = cnt_scr[...]

    zero4 = jnp.zeros((TOP_K, tm), F32)
    route_t_ref[...] = jnp.concatenate(idxs + [zero4] + ranks + [zero4], axis=0)
    g_cols = jnp.concatenate(gates + [zero4], axis=0).T
    for k in range(TOP_K):
        gate_ref[pl.ds(k, tm, stride=TOP_K), :] = jnp.broadcast_to(g_cols[:, k:k + 1], (tm, LANES))


def _outproj(a_ret, a_hg, w_bf, x, g1, sc2, sh2, nw, rwh, rwl, rb, *, tm):
    B, n, _ = x.shape
    nt = n // tm
    T = B * n
    assert tm == COMBINE_TILE
    mrow = lambda b, j: (b, 0, 0)
    tok = lambda b, j: (b * nt + j, 0)
    const = lambda b, j: (0, 0)
    tri = jnp.triu(jnp.ones((tm, tm), BF16), 1)
    return pl.pallas_call(
        _outproj_kernel,
        out_shape=(jax.ShapeDtypeStruct((T * ROW_TILES, LANES), F32),
                   jax.ShapeDtypeStruct((T * ROW_TILES, LANES), F32),
                   jax.ShapeDtypeStruct((16, T), F32),
                   jax.ShapeDtypeStruct((T * TOP_K, LANES), F32),
                   jax.ShapeDtypeStruct((B * nt * N_EXPERTS, LANES), F32),
                   jax.ShapeDtypeStruct((N_EXPERTS, LANES), F32)),
        grid=(B, nt),
        in_specs=[pl.BlockSpec((None, tm, 512), lambda b, j: (b, j, 0)),
                  pl.BlockSpec((None, tm, 512), lambda b, j: (b, j, 0)),
                  pl.BlockSpec((D_MODEL, D_MODEL), const),
                  pl.BlockSpec((None, tm, D_MODEL), lambda b, j: (b, j, 0)),
                  pl.BlockSpec((None, 1, D_MODEL), mrow),
                  pl.BlockSpec((None, 1, D_MODEL), mrow),
                  pl.BlockSpec((None, 1, D_MODEL), mrow),
                  pl.BlockSpec((1, D_MODEL), const),
                  pl.BlockSpec((N_EXPERTS, D_MODEL), const),
                  pl.BlockSpec((N_EXPERTS, D_MODEL), const),
                  pl.BlockSpec((N_EXPERTS, LANES), const),
                  pl.BlockSpec((tm, tm), const)],
        out_specs=(pl.BlockSpec((tm * ROW_TILES, LANES), tok),
                   pl.BlockSpec((tm * ROW_TILES, LANES), tok),
                   pl.BlockSpec((16, tm), lambda b, j: (0, b * nt + j)),
                   pl.BlockSpec((tm * TOP_K, LANES), tok),
                   pl.BlockSpec((N_EXPERTS, LANES), tok),
                   pl.BlockSpec((N_EXPERTS, LANES), const)),
        scratch_shapes=[pltpu.VMEM((N_EXPERTS, LANES), F32)],
        compiler_params=_cparams(("arbitrary", "arbitrary")),
        name="outproj_router",
    )(a_ret, a_hg, w_bf, x, g1, sc2, sh2, nw, rwh, rwl, rb, tri)


def _dispatch_kernel(dest_hbm, v_ref, xs_hbm, idx_smem, zbuf, sem_idx, sem_z, sem_rows,
                     *, tile, n_tiles, n_rows):
    i = pl.program_id(0)
    slot = lax.rem(i, 2)

    def idx_copy(j, s):
        n_asg = tile * TOP_K
        return pltpu.make_async_copy(dest_hbm.at[j], idx_smem.at[pl.ds(s * n_asg, n_asg)],
                                     sem_idx.at[s])

    def zero_copy(j):
        r0 = (n_rows + j * SUB_ROWS) * ROW_TILES
        return pltpu.make_async_copy(zbuf, xs_hbm.at[pl.ds(r0, SUB_ROWS * ROW_TILES), :], sem_z)

    @pl.when(i == 0)
    def _():
        idx_copy(0, 0).start()
        zbuf[...] = jnp.zeros_like(zbuf)
        for j in range(ROW_BLOCK // SUB_ROWS):
            zero_copy(j).start()
        for j in range(ROW_BLOCK // SUB_ROWS):
            zero_copy(j).wait()

    idx_copy(i, slot).wait()

    @pl.when(i + 1 < n_tiles)
    def _():
        idx_copy(i + 1, 1 - slot).start()

    def issue(t, carry):
        e0 = slot * (tile * TOP_K) + t
        for k in range(TOP_K):
            d = pl.multiple_of(idx_smem[e0 + k * tile], ROW_TILES)
            src = v_ref.at[pl.ds(pl.multiple_of(t * ROW_TILES, ROW_TILES), ROW_TILES), :]
            pltpu.make_async_copy(src, xs_hbm.at[pl.ds(d, ROW_TILES), :],
                                  sem_rows).start(priority=k % 2)
        return carry
    lax.fori_loop(0, tile, issue, 0, unroll=4)
    for _ in range(TOP_K):
        pltpu.make_async_copy(v_ref, xs_hbm.at[pl.ds(0, tile * ROW_TILES), :], sem_rows).wait()


def _dispatch(dest_tiles, v, *, tile):
    T = v.shape[0] // ROW_TILES
    n_rows = T * TOP_K
    return pl.pallas_call(
        functools.partial(_dispatch_kernel, tile=tile, n_tiles=T // tile, n_rows=n_rows),
        out_shape=jax.ShapeDtypeStruct(((n_rows + ROW_BLOCK) * ROW_TILES, LANES), F32),
        grid=(T // tile,),
        in_specs=[pl.BlockSpec(memory_space=pl.ANY),
                  pl.BlockSpec((tile * ROW_TILES, LANES), lambda i: (i, 0))],
        out_specs=pl.BlockSpec(memory_space=pl.ANY),
        scratch_shapes=[pltpu.SMEM((2 * tile * TOP_K,), I32),
                        pltpu.VMEM((SUB_ROWS * ROW_TILES, LANES), F32),
                        pltpu.SemaphoreType.DMA((2,)),
                        pltpu.SemaphoreType.DMA,
                        pltpu.SemaphoreType.DMA],
        compiler_params=_cparams(("arbitrary",)),
        name="dispatch",
    )(dest_tiles, v)


def _moe_kernel(be_ref, ns_ref, nx_ref, xr_ref, x_hbm, w1_hbm, b1_ref, w2_hbm, b2_ref, y_ref,
                w1_stage, w2_stage, w1_scr, w2_scr, xbuf, x_scr, act_scr, sem_w, sem_x,
                *, n_blocks):
    i = pl.program_id(0)
    slot = lax.rem(i, 2)
    e = be_ref[i]
    nsub = ns_ref[i]
    changed = jnp.logical_or(i == 0, e != be_ref[jnp.maximum(i - 1, 0)])

    def x_copy(j, s):
        r0 = pl.multiple_of(xr_ref[j], ROW_TILES)
        return pltpu.make_async_copy(x_hbm.at[pl.ds(r0, ROW_BLOCK * ROW_TILES), :], xbuf.at[s],
                                     sem_x.at[s])

    @pl.when(i == 0)
    def _():
        x_copy(0, 0).start()

    @pl.when(nsub > 0)
    def _():
        x_copy(i, slot).wait()

    nxt_blk = jnp.minimum(i + 1, n_blocks - 1)

    @pl.when(jnp.logical_and(i + 1 < n_blocks, ns_ref[nxt_blk] > 0))
    def _():
        x_copy(nxt_blk, 1 - slot).start()

    def weight_copies(ex):
        return (pltpu.make_async_copy(w1_hbm.at[ex], w1_stage, sem_w.at[0]),
                pltpu.make_async_copy(w2_hbm.at[ex], w2_stage, sem_w.at[1]))

    @pl.when(i == 0)
    def _():
        for cp in weight_copies(e):
            cp.start()

    @pl.when(jnp.logical_and(changed, nsub > 0))
    def _():
        for cp in weight_copies(e):
            cp.wait()
        w1_scr[...] = w1_stage[...].astype(BF16)
        w2_scr[...] = w2_stage[...].astype(BF16)
        nxt = nx_ref[e]

        @pl.when(nxt >= 0)
        def _():
            for cp in weight_copies(nxt):
                cp.start()

    def compute(rows):
        x_scr[0:rows, :] = _load_tile_rows(xbuf.at[slot], rows).astype(BF16)
        cw = 256
        for c in range(D_FF // cw):
            glu = (_dot(x_scr[0:rows, :], w1_scr[:, c * cw:(c + 1) * cw])
                   + b1_ref[:, c * cw:(c + 1) * cw])
            lin = (_dot(x_scr[0:rows, :], w1_scr[:, D_FF + c * cw:D_FF + (c + 1) * cw])
                   + b1_ref[:, D_FF + c * cw:D_FF + (c + 1) * cw])
            glu = jnp.minimum(glu, SWIGLU_LIMIT)
            lin = jnp.clip(lin, -SWIGLU_LIMIT, SWIGLU_LIMIT)
            act = glu * jax.nn.sigmoid(SWIGLU_ALPHA * glu) * (lin + 1.0)
            act_scr[0:rows, c * cw:(c + 1) * cw] = act.astype(BF16)
        for c in range(D_MODEL // cw):
            y = (_dot(act_scr[0:rows, :], w2_scr[:, c * cw:(c + 1) * cw])
                 + b2_ref[:, c * cw:(c + 1) * cw])
            for s in range(cw // LANES):
                t = c * (cw // LANES) + s
                y_ref[pl.ds(t, rows, stride=ROW_TILES), :] = y[:, s * LANES:(s + 1) * LANES]
        if rows < ROW_BLOCK:
            y_ref[rows * ROW_TILES:, :] = jnp.zeros(((ROW_BLOCK - rows) * ROW_TILES, LANES), F32)

    for m in range(1, ROW_BLOCK // SUB_ROWS + 1):
        @pl.when(nsub == m)
        def _(m=m):
            compute(m * SUB_ROWS)

    @pl.when(nsub == 0)
    def _():
        y_ref[...] = jnp.zeros_like(y_ref)


def _moe(block_e, nsub, next_e, xrow, xs, w1, b1, w2, b2, *, n_blocks):
    return pl.pallas_call(
        functools.partial(_moe_kernel, n_blocks=n_blocks),
        out_shape=jax.ShapeDtypeStruct((n_blocks * ROW_BLOCK * ROW_TILES, LANES), F32),
        grid_spec=pltpu.PrefetchScalarGridSpec(
            num_scalar_prefetch=4,
            grid=(n_blocks,),
            in_specs=[pl.BlockSpec(memory_space=pl.ANY),
                      pl.BlockSpec(memory_space=pl.ANY),
                      pl.BlockSpec((None, 1, 2 * D_FF), lambda i, be, ns, nx, xr: (be[i], 0, 0)),
                      pl.BlockSpec(memory_space=pl.ANY),
                      pl.BlockSpec((None, 1, D_MODEL), lambda i, be, ns, nx, xr: (be[i], 0, 0))],
            out_specs=pl.BlockSpec((ROW_BLOCK * ROW_TILES, LANES),
                                   lambda i, be, ns, nx, xr: (i, 0)),
            scratch_shapes=[pltpu.VMEM((D_MODEL, 2 * D_FF), F32),
                            pltpu.VMEM((D_FF, D_MODEL), F32),
                            pltpu.VMEM((D_MODEL, 2 * D_FF), BF16),
                            pltpu.VMEM((D_FF, D_MODEL), BF16),
                            pltpu.VMEM((2, ROW_BLOCK * ROW_TILES, LANES), F32),
                            pltpu.VMEM((ROW_BLOCK, D_MODEL), BF16),
                            pltpu.VMEM((ROW_BLOCK, D_FF), BF16),
                            pltpu.SemaphoreType.DMA((2,)),
                            pltpu.SemaphoreType.DMA((2,))]),
        compiler_params=_cparams(("arbitrary",)),
        name="moe_ffn",
    )(block_e, nsub, next_e, xrow, xs, w1, b1, w2, b2)


def _combine_kernel(pos_hbm, src_hbm, yb_hbm, gate_ref, h1_ref, g2_ref, nw_ref, o_ref,
                    pos_smem, src_smem, win, hbuf, sem_tab, sem_win, *, n_tiles):
    i = pl.program_id(0)
    slot = lax.rem(i, 2)
    tile = COMBINE_TILE
    n_asg = tile * TOP_K
    chunk = WIN_ROWS * ROW_TILES
    slot_rows = WIN_CHUNKS * chunk

    def table_copies(j, s):
        return (pltpu.make_async_copy(pos_hbm.at[j], pos_smem.at[pl.ds(s * n_asg, n_asg)],
                                      sem_tab.at[s]),
                pltpu.make_async_copy(src_hbm.at[j], src_smem.at[pl.ds(s * n_asg, n_asg)],
                                      sem_tab.at[s]))

    def fetch_window(s):
        def one(c, carry):
            src = pl.multiple_of(src_smem[s * n_asg + c], ROW_TILES)
            dst = pl.multiple_of(s * slot_rows + c * chunk, chunk)
            pltpu.make_async_copy(yb_hbm.at[pl.ds(src, chunk), :], win.at[pl.ds(dst, chunk), :],
                                  sem_win.at[s]).start()
            return carry
        lax.fori_loop(0, WIN_CHUNKS, one, 0, unroll=2)

    @pl.when(i == 0)
    def _():
        for cp in table_copies(0, 0):
            cp.start()
        for cp in table_copies(0, 0):
            cp.wait()
        fetch_window(0)
        if n_tiles > 1:
            for cp in table_copies(1, 1):
                cp.start()

    @pl.when(i + 1 < n_tiles)
    def _():
        for cp in table_copies(i + 1, 1 - slot):
            cp.wait()
        fetch_window(1 - slot)

    pltpu.make_async_copy(yb_hbm.at[pl.ds(0, slot_rows), :],
                          win.at[pl.ds(pl.multiple_of(slot * slot_rows, slot_rows), slot_rows), :],
                          sem_win.at[slot]).wait()

    g2 = g2_ref[...]
    base = slot * n_asg

    def token(t, carry):
        e0 = base + t * TOP_K
        acc = None
        for k in range(TOP_K):
            p = pl.multiple_of(pos_smem[e0 + k], ROW_TILES)
            term = gate_ref[pl.ds(t * TOP_K + k, 1), :] * win[pl.ds(p, ROW_TILES), :]
            acc = term if acc is None else acc + term
        r0 = pl.multiple_of(t * ROW_TILES, ROW_TILES)
        hbuf[pl.ds(r0, ROW_TILES), :] = h1_ref[pl.ds(r0, ROW_TILES), :] + g2 * acc
        return carry
    lax.fori_loop(0, tile, token, 0, unroll=4)

    @pl.when(i + 2 < n_tiles)
    def _():
        for cp in table_copies(i + 2, slot):
            cp.start()

    h = _load_tile_rows(hbuf, tile)
    ms = jnp.mean(h * h, axis=-1, keepdims=True)
    o_ref[...] = h * lax.rsqrt(ms + EPS) * nw_ref[...]


def _combine(pos_tiles, src_tiles, yb, gate_rows, h1, g2_tiles, nw, *, tiles_per_batch):
    tile = COMBINE_TILE
    T = h1.shape[0] // ROW_TILES
    n_asg = tile * TOP_K
    return pl.pallas_call(
        functools.partial(_combine_kernel, n_tiles=T // tile),
        out_shape=jax.ShapeDtypeStruct((T, D_MODEL), F32),
        grid=(T // tile,),
        in_specs=[pl.BlockSpec(memory_space=pl.ANY),
                  pl.BlockSpec(memory_space=pl.ANY),
                  pl.BlockSpec(memory_space=pl.ANY),
                  pl.BlockSpec((tile * TOP_K, LANES), lambda i: (i, 0)),
                  pl.BlockSpec((tile * ROW_TILES, LANES), lambda i: (i, 0)),
                  pl.BlockSpec((None, ROW_TILES, LANES), lambda i: (i // tiles_per_batch, 0, 0)),
                  pl.BlockSpec((1, D_MODEL), lambda i: (0, 0))],
        out_specs=pl.BlockSpec((tile, D_MODEL), lambda i: (i, 0)),
        scratch_shapes=[pltpu.SMEM((2 * n_asg,), I32),
                        pltpu.SMEM((2 * n_asg,), I32),
                        pltpu.VMEM((2 * WIN_CHUNKS * WIN_ROWS * ROW_TILES, LANES), F32),
                        pltpu.VMEM((tile * ROW_TILES, LANES), F32),
                        pltpu.SemaphoreType.DMA((2,)),
                        pltpu.SemaphoreType.DMA((2,))],
        compiler_params=_cparams(("arbitrary",)),
        name="combine_norm",
    )(pos_tiles, src_tiles, yb, gate_rows, h1, g2_tiles, nw)


def _rope_tables(n):
    rows = n // GRID_W
    row = jnp.repeat(jnp.arange(rows, dtype=F32), GRID_W)
    col = jnp.tile(jnp.arange(GRID_W, dtype=F32), rows)
    n_freq = RET_DK // 4
    inv = ROPE_BASE ** (-jnp.arange(n_freq, dtype=F32) / n_freq)
    ang = jnp.concatenate([row[:, None] * inv, col[:, None] * inv], axis=-1)
    cos, sin = jnp.cos(ang), jnp.sin(ang)
    cos_h = jnp.concatenate([cos, cos], axis=-1)
    sin_h = jnp.concatenate([-sin, sin], axis=-1)
    return jnp.tile(cos_h, (1, RET_HEADS)), jnp.tile(sin_h, (1, RET_HEADS))


def kernel(x, c, ctx, c_ctx, w_ada, b_ada, norm_mix_w, norm_ffn_w, w_in, w_out, hg_lb,
           hg_norm_w, router_w, router_b, w1, b1, w2, b2, norm_final_w):
    B, N, D = x.shape
    C = ctx.shape[1]
    T = B * N
    assert D == D_MODEL and w_ada.shape[0] == 1

    cc = jnp.concatenate([c.astype(F32), c_ctx.astype(F32)[None, :],
                          jnp.zeros((16 - B - 1, D), F32)], axis=0)
    mod = _ada(cc, w_ada[0], b_ada[0][None, :])
    mod = mod.reshape(16, 6, 1, D).transpose(1, 0, 2, 3)
    sh1, sc1, g1, sh2, sc2, g2 = (mod[i] for i in range(6))

    w_in_bf = w_in[0].astype(BF16)
    w_out_bf = w_out[0].astype(BF16)
    nw_mix = norm_mix_w[0][None, :]
    cos_t, sin_t = _rope_tables(N)

    pb, lf = _inproj(x, sc1, sh1, None, nw_mix, w_in_bf, hg_lb[:2], cos_t, sin_t,
                     rope=True, tm=INPROJ_TILE)
    pbc, lfc = _inproj(ctx, sc1, sh1, B, nw_mix, w_in_bf, hg_lb[:2], cos_t[:C], sin_t[:C],
                       rope=False, tm=C)

    a_ret = _retention(pb, pbc)
    a_hg = _hgrn(pb, lf, pbc, lfc, hg_norm_w[0][None, :])

    rw_t = router_w[0].T
    rwh = rw_t.astype(BF16)
    rwl = (rw_t - rwh.astype(F32)).astype(BF16)
    rb = jnp.broadcast_to(router_b[0][:, None], (N_EXPERTS, LANES))
    h1, v, route_t, gate_rows, base_cnt, cnt = _outproj(a_ret, a_hg, w_out_bf, x, g1, sc2, sh2,
                                                        norm_ffn_w[0][None, :], rwh, rwl, rb,
                                                        tm=COMBINE_TILE)

    idx = route_t[0:TOP_K].astype(I32)
    rank = route_t[2 * TOP_K:3 * TOP_K].astype(I32)
    counts = cnt[:, 0].astype(I32)
    padded = (counts + ROW_BLOCK - 1) // ROW_BLOCK * ROW_BLOCK
    pend = jnp.cumsum(padded)
    pstart = pend - padded
    cstart = jnp.cumsum(counts) - counts
    onehot = idx[None] == jnp.arange(N_EXPERTS, dtype=I32)[:, None, None]

    def dest_tiles(first_row, tile):
        dest = (jnp.sum(jnp.where(onehot, first_row[:, None, None], 0), axis=0) + rank) * ROW_TILES
        return dest.reshape(TOP_K, T // tile, tile).transpose(1, 0, 2).reshape(T // tile,
                                                                               TOP_K * tile)
    n_blocks = (T * TOP_K) // ROW_BLOCK + N_EXPERTS
    starts = jnp.arange(n_blocks, dtype=I32) * ROW_BLOCK
    block_e = jnp.minimum(jnp.sum((pend[None, :] <= starts[:, None]).astype(I32), axis=1),
                          N_EXPERTS - 1)
    valid = jnp.clip(counts[block_e] - (starts - pstart[block_e]), 0, ROW_BLOCK)
    nsub = (valid + SUB_ROWS - 1) // SUB_ROWS
    first_from = lax.cummin(jnp.where(counts > 0, jnp.arange(N_EXPERTS, dtype=I32), N_EXPERTS),
                            reverse=True)
    next_e = jnp.concatenate([first_from[1:], jnp.full((1,), N_EXPERTS, I32)])
    next_e = jnp.where(next_e == N_EXPERTS, -1, next_e)
    xrow = jnp.where(nsub > 0, cstart[block_e] + starts - pstart[block_e], 0) * ROW_TILES

    xs = _dispatch(dest_tiles(cstart, DISPATCH_TILE), v, tile=DISPATCH_TILE)
    yb = _moe(block_e, nsub, next_e, xrow.astype(I32), xs, w1[0], b1[0][:, None, :], w2[0],
              b2[0][:, None, :], n_blocks=n_blocks)

    tc = COMBINE_TILE
    n_ct = T // tc
    base = base_cnt.reshape(n_ct, N_EXPERTS, LANES)[:, :, 0].astype(I32)
    n_run = jnp.concatenate([base[1:], counts[None, :]], axis=0) - base
    n_chunk = (n_run + WIN_ROWS - 1) // WIN_ROWS
    chunk_end = jnp.cumsum(n_chunk, axis=1)
    chunk_start = chunk_end - n_chunk
    win_off = chunk_start * WIN_ROWS
    c_id = jnp.arange(WIN_CHUNKS, dtype=I32)
    c_exp = jnp.sum((chunk_end[:, None, :] <= c_id[None, :, None]).astype(I32), axis=2)
    c_real = c_exp < N_EXPERTS
    c_own = c_exp[:, :, None] == jnp.arange(N_EXPERTS, dtype=I32)
    take = lambda tab: jnp.sum(jnp.where(c_own, tab[:, None, :], 0), axis=2)
    c_src = (take(pstart[None, :] + base) + (c_id[None, :] - take(chunk_start)) * WIN_ROWS)
    src_tiles = jnp.concatenate(
        [jnp.where(c_real, c_src, 0) * ROW_TILES,
         jnp.zeros((n_ct, tc * TOP_K - WIN_CHUNKS), I32)], axis=1)
    slot_off = (jnp.arange(n_ct, dtype=I32) % 2) * (WIN_CHUNKS * WIN_ROWS)
    shift = jnp.repeat(win_off - base + slot_off[:, None], tc, axis=0).T
    pos = (jnp.sum(jnp.where(onehot, shift[:, None, :], 0), axis=0) + rank) * ROW_TILES

    def per_tile(a):
        return a.reshape(TOP_K, n_ct, tc).transpose(1, 2, 0).reshape(n_ct, tc * TOP_K)

    out = _combine(per_tile(pos), src_tiles, yb, gate_rows, h1,
                   g2.reshape(16, ROW_TILES, LANES), norm_final_w[None, :],
                   tiles_per_batch=N // tc)
    return out.reshape(B, N, D)
```

```python
import functools
import math

import jax
import jax.numpy as jnp
from jax import lax
from jax.experimental import pallas as pl
from jax.experimental.pallas import tpu as pltpu

F32 = jnp.float32
BF16 = jnp.bfloat16
I32 = jnp.int32

D_MODEL = 1024
GRID_W = 64
RET_HEADS = 4
RET_DK = 64
HG_HEADS = 4
PROJ_W = 4096
ROPE_BASE = 10000.0
EPS = 1e-6
N_EXPERTS = 32
TOP_K = 4
D_FF = 1024
SWIGLU_LIMIT = 7.0
SWIGLU_ALPHA = 1.702
LOG2_E = 1.0 / math.log(2.0)

LANES = 128
CHUNK = 128
INPROJ_TILE = 1024
ROW_BLOCK = 1024
SUB_ROWS = 256
COMBINE_TILE = 512
WIN_ROWS = 16
WIN_CHUNKS = (COMBINE_TILE * TOP_K + N_EXPERTS * (WIN_ROWS - 1) + WIN_ROWS - 1) // WIN_ROWS
VMEM_LIMIT = 56 * 1024 * 1024
ROW_TILES = D_MODEL // LANES

C_RQ, C_RK, C_RV, C_RG, C_HQ, C_FF, C_FB, C_HV, C_HG = (
    0, 256, 512, 1024, 1536, 2048, 2560, 3072, 3584)


def _cparams(sem):
    return pltpu.CompilerParams(dimension_semantics=sem, vmem_limit_bytes=VMEM_LIMIT)


def _split_bf16(x):
    hi = x.astype(BF16)
    lo = (x - hi.astype(F32)).astype(BF16)
    return hi, lo


def _dot(a, b):
    return jnp.dot(a, b, preferred_element_type=F32)


def _dot_nt(a, b):
    return lax.dot_general(a, b, (((1,), (1,)), ((), ())), preferred_element_type=F32)


def _dot3(a, b):
    ah, al = _split_bf16(a)
    bh, bl = _split_bf16(b)
    return _dot(ah, bh) + (_dot(ah, bl) + _dot(al, bh))


def _silu(x):
    return x * jax.nn.sigmoid(x)


def _load_tile_rows(ref, n):
    return jnp.concatenate([ref[pl.ds(s, n, stride=ROW_TILES), :] for s in range(ROW_TILES)],
                           axis=1)


def _store_tile_rows(ref, x):
    n = x.shape[0]
    for s in range(ROW_TILES):
        ref[pl.ds(s, n, stride=ROW_TILES), :] = x[:, s * LANES:(s + 1) * LANES]


def _ada_kernel(c_ref, w_ref, b_ref, o_ref):
    s = _silu(c_ref[...])
    o_ref[...] = _dot3(s, w_ref[...]) + b_ref[...]


def _ada(cc, w, b):
    nblk = w.shape[1] // D_MODEL
    return pl.pallas_call(
        _ada_kernel,
        out_shape=jax.ShapeDtypeStruct((cc.shape[0], w.shape[1]), F32),
        grid=(nblk,),
        in_specs=[pl.BlockSpec(cc.shape, lambda j: (0, 0)),
                  pl.BlockSpec((D_MODEL, D_MODEL), lambda j: (0, j)),
                  pl.BlockSpec((1, D_MODEL), lambda j: (0, j))],
        out_specs=pl.BlockSpec((cc.shape[0], D_MODEL), lambda j: (0, j)),
        compiler_params=_cparams(("arbitrary",)),
        name="ada",
    )(cc, w, b)


def _inproj_kernel(x_ref, sc_ref, sh_ref, nw_ref, w_ref, lb_ref, cos_ref, sin_ref,
                   pb_ref, lf_ref, u_scr, *, rope):
    x = x_ref[...]
    ms = jnp.mean(x * x, axis=-1, keepdims=True)
    u = x * lax.rsqrt(ms + EPS) * nw_ref[...] * (1.0 + sc_ref[...]) + sh_ref[...]
    u_scr[...] = u.astype(BF16)

    def proj(lo, width):
        return _dot(u_scr[...], w_ref[:, lo:lo + width])

    tm = x.shape[0]
    if rope:
        lane = lax.broadcasted_iota(I32, (tm, LANES), 1)
        first = (lane & 32) == 0

    def put_rot(col, scale):
        for j in range(2):
            lo = col + j * LANES
            t = proj(lo, LANES)
            if scale != 1.0:
                t = t * scale
            if rope:
                tb = j * LANES
                rot = jnp.where(first, pltpu.roll(t, 96, axis=1), pltpu.roll(t, 32, axis=1))
                t = t * cos_ref[:, tb:tb + LANES] + rot * sin_ref[:, tb:tb + LANES]
            pb_ref[:, lo:lo + LANES] = t.astype(BF16)

    put_rot(C_RQ, 1.0)
    put_rot(C_RK, RET_DK ** -0.5)
    pb_ref[:, C_RV:C_RV + 512] = proj(C_RV, 512).astype(BF16)
    pb_ref[:, C_RG:C_RG + 512] = _silu(proj(C_RG, 512)).astype(BF16)
    pb_ref[:, C_HQ:C_HQ + 512] = _silu(proj(C_HQ, 512)).astype(BF16)
    pb_ref[:, C_HV:C_HV + 512] = proj(C_HV, 512).astype(BF16)
    pb_ref[:, C_HG:C_HG + 512] = _silu(proj(C_HG, 512)).astype(BF16)

    la = lb_ref[0]
    lbb = lb_ref[1]
    mx = jnp.maximum(la, lbb)
    ea = jnp.exp(la - mx)
    eb = jnp.exp(lbb - mx)
    lb = ea / (ea + eb)
    for d, col in enumerate((C_FF, C_FB)):
        lbd = lb[d:d + 1, :]
        f = lbd + (1.0 - lbd) * jax.nn.sigmoid(proj(col, 512))
        pb_ref[:, col:col + 512] = (1.0 - f).astype(BF16)
        lf_ref[:, d * 512:(d + 1) * 512] = jnp.log(f) * LOG2_E


def _inproj(x, sc, sh, mod_row, nw, w_bf, hg_lb, cos_t, sin_t, *, rope, tm):
    B, n, _ = x.shape
    nt = n // tm
    if mod_row is None:
        mrow = lambda b, j: (b, 0, 0)
    else:
        mrow = lambda b, j: (mod_row, 0, 0)
    return pl.pallas_call(
        functools.partial(_inproj_kernel, rope=rope),
        out_shape=(jax.ShapeDtypeStruct((B, n, PROJ_W), BF16),
                   jax.ShapeDtypeStruct((B, n, 1024), F32)),
        grid=(B, nt),
        in_specs=[pl.BlockSpec((None, tm, D_MODEL), lambda b, j: (b, j, 0)),
                  pl.BlockSpec((None, 1, D_MODEL), mrow),
                  pl.BlockSpec((None, 1, D_MODEL), mrow),
                  pl.BlockSpec((1, D_MODEL), lambda b, j: (0, 0)),
                  pl.BlockSpec((D_MODEL, PROJ_W), lambda b, j: (0, 0),
                               pipeline_mode=pl.Buffered(1)),
                  pl.BlockSpec((2, 2, 512), lambda b, j: (0, 0, 0)),
                  pl.BlockSpec((tm, 256), lambda b, j: (j, 0)),
                  pl.BlockSpec((tm, 256), lambda b, j: (j, 0))],
        out_specs=(pl.BlockSpec((None, tm, PROJ_W), lambda b, j: (b, j, 0)),
                   pl.BlockSpec((None, tm, 1024), lambda b, j: (b, j, 0))),
        scratch_shapes=[pltpu.VMEM((tm, D_MODEL), BF16)],
        compiler_params=_cparams(("arbitrary", "arbitrary")),
        name="inproj_rope" if rope else "inproj_ctx",
    )(x, sc, sh, nw, w_bf, hg_lb, cos_t, sin_t)


_RET_LGF = [math.log1p(-(2.0 ** (-5.0 - 2.0 * h))) for h in range(RET_HEADS)]
_RET_LGB = [math.log1p(-(2.0 ** (-6.0 - 2.0 * h))) for h in range(RET_HEADS)]


def _ret_kernel(q_ref, k_ref, v_ref, g_ref, kc_ref, vc_ref, o_ref,
                u_scr, sin_scr, kt_scr, dtot_scr, *, n_lat, n_ctx):
    L = CHUNK
    pair = pl.program_id(1)
    row = lax.broadcasted_iota(I32, (L, L), 0).astype(F32)
    col = lax.broadcasted_iota(I32, (L, L), 1).astype(F32)
    lane = lax.broadcasted_iota(I32, (L, LANES), 1)
    trow = lax.broadcasted_iota(I32, (L, 1), 0).astype(F32)
    tcol = lax.broadcasted_iota(I32, (1, L), 1).astype(F32)
    low_half = lane < RET_DK

    def u_chunk(k_blk, v_blk, ci, store_kt):
        kt = k_blk.astype(F32).T
        if store_kt is not None:
            kt_scr[store_kt] = kt.astype(BF16)
        for hh in range(2):
            lgf = jnp.where(pair == 0, _RET_LGF[hh], _RET_LGF[2 + hh])
            lgb = jnp.where(pair == 0, _RET_LGB[hh], _RET_LGB[2 + hh])
            kth = kt[hh * RET_DK:(hh + 1) * RET_DK, :]
            wkf = jnp.exp(lgf * (L - 1.0 - tcol))
            wkb = jnp.exp(lgb * tcol)
            lhs = jnp.concatenate([kth * wkf, kth * wkb], axis=0).astype(BF16)
            u_scr[hh, ci] = _dot(lhs, v_blk[:, hh * LANES:(hh + 1) * LANES])

    for c in range(n_ctx):
        u_chunk(kc_ref[c * L:(c + 1) * L, :], vc_ref[c * L:(c + 1) * L, :], c, None)

    def lat_u(c, carry):
        r0 = pl.multiple_of(c * L, L)
        u_chunk(k_ref[pl.ds(r0, L), :], v_ref[pl.ds(r0, L), :], n_ctx + c, c)
        return carry
    lax.fori_loop(0, n_lat, lat_u, 0, unroll=8)

    ones = jnp.ones((RET_DK, LANES), F32)
    for hh in range(2):
        lgf = jnp.where(pair == 0, _RET_LGF[hh], _RET_LGF[2 + hh])
        lgb = jnp.where(pair == 0, _RET_LGB[hh], _RET_LGB[2 + hh])
        d = row - col
        dtot_scr[hh] = jnp.where(d > 0, jnp.exp(lgf * jnp.maximum(d, 0.0)),
                                 jnp.where(d < 0, jnp.exp(lgb * jnp.maximum(-d, 0.0)), 2.0))
        af = jnp.exp(ones * (lgf * L))
        ab = jnp.exp(ones * (lgb * L))

        s = jnp.zeros((RET_DK, LANES), F32)
        for c in range(n_ctx):
            s = af * s + u_scr[hh, c, 0:RET_DK, :]
        sb = jnp.zeros((RET_DK, LANES), F32)
        for c in reversed(range(n_ctx)):
            sb = ab * sb + u_scr[hh, c, RET_DK:2 * RET_DK, :]

        def fwd(c, s, hh=hh, af=af):
            sin_scr[hh, c, 0:RET_DK, :] = s.astype(BF16)
            return af * s + u_scr[hh, n_ctx + c, 0:RET_DK, :]
        lax.fori_loop(0, n_lat, fwd, s)

        def bwd(i, sb, hh=hh, ab=ab):
            c = n_lat - 1 - i
            sin_scr[hh, c, RET_DK:2 * RET_DK, :] = sb.astype(BF16)
            return ab * sb + u_scr[hh, n_ctx + c, RET_DK:2 * RET_DK, :]
        lax.fori_loop(0, n_lat, bwd, sb)

    def out_chunk(c, carry):
        r0 = pl.multiple_of(c * L, L)
        q = q_ref[pl.ds(r0, L), :].astype(F32)
        qr = pltpu.roll(q, RET_DK, axis=1)
        kt = kt_scr[c]
        for hh in range(2):
            lgf = jnp.where(pair == 0, _RET_LGF[hh], _RET_LGF[2 + hh])
            lgb = jnp.where(pair == 0, _RET_LGB[hh], _RET_LGB[2 + hh])
            mine = low_half if hh == 0 else jnp.logical_not(low_half)
            qm = jnp.where(mine, q, 0.0).astype(BF16)
            p = (_dot(qm, kt) * dtot_scr[hh]).astype(BF16)
            vh = v_ref[pl.ds(r0, L), hh * LANES:(hh + 1) * LANES]
            wqf = jnp.exp(lgf * (trow + 1.0))
            wqb = jnp.exp(lgb * (L - trow))
            qa, qb = (q, qr) if hh == 0 else (qr, q)
            qs = jnp.where(low_half, qa * wqf, qb * wqb).astype(BF16)
            o = _dot(p, vh) + _dot(qs, sin_scr[hh, c])
            ms = jnp.mean(o * o, axis=-1, keepdims=True)
            gh = g_ref[pl.ds(r0, L), hh * LANES:(hh + 1) * LANES].astype(F32)
            o_ref[pl.ds(r0, L), hh * LANES:(hh + 1) * LANES] = (
                o * lax.rsqrt(ms + EPS) * gh).astype(BF16)
        return carry
    lax.fori_loop(0, n_lat, out_chunk, 0, unroll=8)


def _retention(pb, pbc):
    B, n, _ = pb.shape
    nc = pbc.shape[1]
    n_lat, n_ctx = n // CHUNK, nc // CHUNK
    return pl.pallas_call(
        functools.partial(_ret_kernel, n_lat=n_lat, n_ctx=n_ctx),
        out_shape=jax.ShapeDtypeStruct((B, n, 512), BF16),
        grid=(B, 2),
        in_specs=[pl.BlockSpec((None, n, LANES), lambda b, p: (b, 0, C_RQ // LANES + p)),
                  pl.BlockSpec((None, n, LANES), lambda b, p: (b, 0, C_RK // LANES + p)),
                  pl.BlockSpec((None, n, 256), lambda b, p: (b, 0, C_RV // 256 + p)),
                  pl.BlockSpec((None, n, 256), lambda b, p: (b, 0, C_RG // 256 + p)),
                  pl.BlockSpec((None, nc, LANES), lambda b, p: (b, 0, C_RK // LANES + p)),
                  pl.BlockSpec((None, nc, 256), lambda b, p: (b, 0, C_RV // 256 + p))],
        out_specs=pl.BlockSpec((None, n, 256), lambda b, p: (b, 0, p)),
        scratch_shapes=[pltpu.VMEM((2, n_lat + n_ctx, CHUNK, LANES), F32),
                        pltpu.VMEM((2, n_lat, CHUNK, LANES), BF16),
                        pltpu.VMEM((n_lat, LANES, CHUNK), BF16),
                        pltpu.VMEM((2, CHUNK, CHUNK), F32)],
        compiler_params=_cparams(("arbitrary", "arbitrary")),
        name="retention",
    )(pb, pb, pb, pb, pbc, pbc)


_LEVELS = (64, 32, 16, 8, 4, 2, 1)


def _expand_rows(r, rep):
    n = r.shape[0]
    if n == 1:
        return jnp.broadcast_to(r, (rep, r.shape[1]))
    return jnp.concatenate(
        [jnp.broadcast_to(r[i:i + 1, :], (rep, r.shape[1])) for i in range(n)], axis=0)


def _hgrn_kernel(q_ref, kf_ref, kb_ref, v_ref, g_ref, lff_ref, lfb_ref,
                 kfc_ref, kbc_ref, vc_ref, lffc_ref, lfbc_ref, nw_ref, o_ref,
                 ut_scr, a_scr, qs_scr, oi_scr, sin_scr, bfb_scr, *, n_lat, n_ctx):
    L = CHUNK
    row = lax.broadcasted_iota(I32, (L, L), 0)
    col = lax.broadcasted_iota(I32, (L, L), 1)
    xr_bits = lax.bitcast_convert_type((row ^ col).astype(F32), I32)
    lv = lax.shift_right_logical(xr_bits, 23) - 127
    row2 = lax.broadcasted_iota(I32, (L, 2 * L), 0)
    col2 = lax.broadcasted_iota(I32, (L, 2 * L), 1) & (L - 1)
    tril2 = jnp.where(col2 <= row2, 1.0, 0.0).astype(BF16)
    triu2 = jnp.where(col2 >= row2, 1.0, 0.0).astype(BF16)

    def cums(lff, lfb):
        hf, lof = _split_bf16(lff)
        hb, lob = _split_bf16(lfb)
        bf = _dot(tril2, jnp.concatenate([hf, lof], axis=0))
        bb = _dot(triu2, jnp.concatenate([hb, lob], axis=0))
        return bf, bb

    def state_part(ci, kf, kb, v_blk, bf, bb):
        endf = bf[L - 1:L, :]
        endb = bb[0:1, :]
        ksf = kf * jnp.exp2(endf - bf)
        ksb = kb * jnp.exp2(endb - bb)
        vt = v_blk.astype(F32).T.astype(BF16)
        ut_scr[ci] = _dot(vt, jnp.concatenate([ksf, ksb], axis=1).astype(BF16))
        a_scr[ci] = jnp.broadcast_to(
            jnp.concatenate([jnp.exp2(endf), jnp.exp2(endb)], axis=1), (8, 2 * LANES))

    for c in range(n_ctx):
        sl = slice(c * L, (c + 1) * L)
        bf, bb = cums(lffc_ref[sl, :], lfbc_ref[sl, :])
        state_part(c, kfc_ref[sl, :].astype(F32), kbc_ref[sl, :].astype(F32),
                   vc_ref[sl, :], bf, bb)

    def lat_chunk(c, u):
        rows = pl.ds(pl.multiple_of(c * L, L), L)
        bf_scr = bfb_scr.at[u, 0]
        bb_scr = bfb_scr.at[u, 1]
        q = q_ref[rows, :].astype(F32)
        kf = kf_ref[rows, :].astype(F32)
        kb = kb_ref[rows, :].astype(F32)
        v_blk = v_ref[rows, :]
        lff = lff_ref[rows, :]
        lfb = lfb_ref[rows, :]
        bf, bb = cums(lff, lfb)
        state_part(n_ctx + c, kf, kb, v_blk, bf, bb)
        qs_scr[rows, :] = jnp.concatenate([q * jnp.exp2(bf), q * jnp.exp2(bb)],
                                          axis=1).astype(BF16)
        bf_scr[...] = bf
        bb_scr[...] = bb

        acc = jnp.zeros((L, L), F32)
        for lvl, h in enumerate(_LEVELS):
            bit = (row & h) != 0
            ksel = jnp.where(bit, kb, kf)
            if h >= 4:
                n = (L // 2) // h
                if n == 1:
                    rf = bf_scr[h - 1:h, :]
                    rb = bb_scr[h:h + 1, :]
                else:
                    rf = bf_scr[pl.ds(h - 1, n, stride=2 * h), :]
                    rb = bb_scr[pl.ds(h, n, stride=2 * h), :]
                df = bf - _expand_rows(rf, 2 * h)
                db = bb - _expand_rows(rb, 2 * h)
                eq = jnp.where(bit, df, db)
                ek = -jnp.where(bit, db, df)
            elif h == 2:
                m = row & 3
                lff_n = pltpu.roll(lff, L - 1, axis=0)
                lfb_n = pltpu.roll(lfb, L - 1, axis=0)
                eq = jnp.where(m == 2, lff,
                               jnp.where(m == 3, lff + pltpu.roll(lff, 1, axis=0),
                                         jnp.where(m == 0, lfb + lfb_n, lfb)))
                ek = jnp.where(m == 3, pltpu.roll(lfb, 1, axis=0),
                               jnp.where(m == 0, lff_n, 0.0))
            else:
                eq = jnp.where(bit, lff, lfb)
                ek = None
            lhs = (q * jnp.exp2(eq)).astype(BF16)
            rhs = (ksel if ek is None else ksel * jnp.exp2(ek)).astype(BF16)
            acc = jnp.where(lv == 6 - lvl, _dot_nt(lhs, rhs), acc)

        dsum = jnp.sum(q * (kf + kb), axis=-1, keepdims=True)
        oi_scr[rows, :] = _dot(acc.astype(BF16), v_blk) + dsum * v_blk.astype(F32)

    def lat_pair(i, carry):
        for u in range(8):
            lat_chunk(8 * i + u, u)
        return carry
    lax.fori_loop(0, n_lat // 8, lat_pair, 0)

    st = jnp.zeros((LANES, LANES), F32)
    for c in range(n_ctx):
        st = st * a_scr[c, 0:1, 0:LANES] + ut_scr[c, :, 0:LANES]
    stb = jnp.zeros((LANES, LANES), F32)
    for c in reversed(range(n_ctx)):
        stb = stb * a_scr[c, 0:1, LANES:2 * LANES] + ut_scr[c, :, LANES:2 * LANES]

    def fwd(c, st):
        sin_scr[c, :, 0:LANES] = st.astype(BF16)
        ci = n_ctx + c
        return st * a_scr[ci, 0:1, 0:LANES] + ut_scr[ci, :, 0:LANES]
    lax.fori_loop(0, n_lat, fwd, st)

    def bwd(i, stb):
        c = n_lat - 1 - i
        sin_scr[c, :, LANES:2 * LANES] = stb.astype(BF16)
        ci = n_ctx + c
        return stb * a_scr[ci, 0:1, LANES:2 * LANES] + ut_scr[ci, :, LANES:2 * LANES]
    lax.fori_loop(0, n_lat, bwd, stb)

    def out_chunk(c, carry):
        rows = pl.ds(pl.multiple_of(c * L, L), L)
        o = oi_scr[rows, :] + _dot_nt(qs_scr[rows, :], sin_scr[c])
        ms = jnp.mean(o * o, axis=-1, keepdims=True)
        y = o * lax.rsqrt(ms + EPS) * nw_ref[...] * g_ref[rows, :].astype(F32)
        o_ref[rows, :] = y.astype(BF16)
        return carry
    lax.fori_loop(0, n_lat, out_chunk, 0, unroll=8)


def _hgrn(pb, lf, pbc, lfc, nw):
    B, n, _ = pb.shape
    nc = pbc.shape[1]
    n_lat, n_ctx = n // CHUNK, nc // CHUNK

    def colblk(rows, col0):
        return pl.BlockSpec((None, rows, LANES), lambda b, h: (b, 0, col0 // LANES + h))

    return pl.pallas_call(
        functools.partial(_hgrn_kernel, n_lat=n_lat, n_ctx=n_ctx),
        out_shape=jax.ShapeDtypeStruct((B, n, 512), BF16),
        grid=(B, HG_HEADS),
        in_specs=[colblk(n, C_HQ), colblk(n, C_FF), colblk(n, C_FB), colblk(n, C_HV),
                  colblk(n, C_HG), colblk(n, 0), colblk(n, 512),
                  colblk(nc, C_FF), colblk(nc, C_FB), colblk(nc, C_HV),
                  colblk(nc, 0), colblk(nc, 512),
                  pl.BlockSpec((1, LANES), lambda b, h: (0, 0))],
        out_specs=pl.BlockSpec((None, n, LANES), lambda b, h: (b, 0, h)),
        scratch_shapes=[pltpu.VMEM((n_lat + n_ctx, LANES, 2 * LANES), F32),
                        pltpu.VMEM((n_lat + n_ctx, 8, 2 * LANES), F32),
                        pltpu.VMEM((n, 2 * LANES), BF16),
                        pltpu.VMEM((n, LANES), F32),
                        pltpu.VMEM((n_lat, LANES, 2 * LANES), BF16),
                        pltpu.VMEM((8, 2, CHUNK, LANES), F32)],
        compiler_params=_cparams(("arbitrary", "arbitrary")),
        name="hgrn2",
    )(pb, pb, pb, pb, pb, lf, lf, pbc, pbc, pbc, lfc, lfc, nw)


def _outproj_kernel(ar_ref, ah_ref, w_ref, x_ref, g1_ref, sc_ref, sh_ref, nw_ref,
                    rwh_ref, rwl_ref, rb_ref, tri_ref, h1_ref, v_ref, route_t_ref, gate_ref, base_ref,
                    cnt_ref, cnt_scr):
    first_step = jnp.logical_and(pl.program_id(0) == 0, pl.program_id(1) == 0)

    @pl.when(first_step)
    def _():
        cnt_scr[...] = jnp.zeros_like(cnt_scr)

    y = _dot(ar_ref[...], w_ref[0:512, :]) + _dot(ah_ref[...], w_ref[512:1024, :])
    h1 = x_ref[...] + g1_ref[...] * y
    _store_tile_rows(h1_ref, h1)
    ms = jnp.mean(h1 * h1, axis=-1, keepdims=True)
    v = h1 * lax.rsqrt(ms + EPS) * nw_ref[...] * (1.0 + sc_ref[...]) + sh_ref[...]
    _store_tile_rows(v_ref, v)

    tm = v.shape[0]
    vh, vl = _split_bf16(v)
    rwh, rwl = rwh_ref[...], rwl_ref[...]
    l = (_dot_nt(rwh, vh) + (_dot_nt(rwh, vl) + _dot_nt(rwl, vh))) + rb_ref[:, 0:1]
    row_f = lax.broadcasted_iota(I32, (N_EXPERTS, tm), 0).astype(F32)
    sels, tops, idxs = [], [], []
    for _ in range(TOP_K):
        m = jnp.max(l, axis=0, keepdims=True)
        i = jnp.min(jnp.where(l == m, row_f, float(N_EXPERTS)), axis=0, keepdims=True)
        sel = row_f == i
        l = jnp.where(sel, -jnp.inf, l)
        sels.append(sel)
        tops.append(m)
        idxs.append(i)
    es = [jnp.exp(t - tops[0]) for t in tops]
    den = es[0] + es[1] + es[2] + es[3]
    gates = [e / den for e in es]

    oh = jnp.zeros((N_EXPERTS, tm), F32)
    for sel in sels:
        oh = jnp.where(sel, 1.0, oh)
    cnt = cnt_scr[:, 0:1]
    before = _dot(oh.astype(BF16), tri_ref[...]) + cnt
    ranks = [jnp.sum(jnp.where(sel, before, 0.0), axis=0, keepdims=True) for sel in sels]
    base_ref[...] = cnt_scr[...]
    cnt_scr[...] = cnt_scr[...] + jnp.sum(oh, axis=1, keepdims=True)
    cnt_ref[...] = cnt_scr[...]

    zero4 = jnp.zeros((TOP_K, tm), F32)
    route_t_ref[...] = jnp.concatenate(idxs + [zero4] + ranks + [zero4], axis=0)
    g_cols = jnp.concatenate(gates + [zero4], axis=0).T
    for k in range(TOP_K):
        gate_ref[pl.ds(k, tm, stride=TOP_K), :] = jnp.broadcast_to(g_cols[:, k:k + 1], (tm, LANES))


def _outproj(a_ret, a_hg, w_bf, x, g1, sc2, sh2, nw, rwh, rwl, rb, *, tm):
    B, n, _ = x.shape
    nt = n // tm
    T = B * n
    assert tm == COMBINE_TILE
    mrow = lambda b, j: (b, 0, 0)
    tok = lambda b, j: (b * nt + j, 0)
    const = lambda b, j: (0, 0)
    tri = jnp.triu(jnp.ones((tm, tm), BF16), 1)
    return pl.pallas_call(
        _outproj_kernel,
        out_shape=(jax.ShapeDtypeStruct((T * ROW_TILES, LANES), F32),
                   jax.ShapeDtypeStruct((T * ROW_TILES, LANES), F32),
                   jax.ShapeDtypeStruct((16, T), F32),
                   jax.ShapeDtypeStruct((T * TOP_K, LANES), F32),
                   jax.ShapeDtypeStruct((B * nt * N_EXPERTS, LANES), F32),
                   jax.ShapeDtypeStruct((N_EXPERTS, LANES), F32)),
        grid=(B, nt),
        in_specs=[pl.BlockSpec((None, tm, 512), lambda b, j: (b, j, 0)),
                  pl.BlockSpec((None, tm, 512), lambda b, j: (b, j, 0)),
                  pl.BlockSpec((D_MODEL, D_MODEL), const),
                  pl.BlockSpec((None, tm, D_MODEL), lambda b, j: (b, j, 0)),
                  pl.BlockSpec((None, 1, D_MODEL), mrow),
                  pl.BlockSpec((None, 1, D_MODEL), mrow),
                  pl.BlockSpec((None, 1, D_MODEL), mrow),
                  pl.BlockSpec((1, D_MODEL), const),
                  pl.BlockSpec((N_EXPERTS, D_MODEL), const),
                  pl.BlockSpec((N_EXPERTS, D_MODEL), const),
                  pl.BlockSpec((N_EXPERTS, LANES), const),
                  pl.BlockSpec((tm, tm), const)],
        out_specs=(pl.BlockSpec((tm * ROW_TILES, LANES), tok),
                   pl.BlockSpec((tm * ROW_TILES, LANES), tok),
                   pl.BlockSpec((16, tm), lambda b, j: (0, b * nt + j)),
                   pl.BlockSpec((tm * TOP_K, LANES), tok),
                   pl.BlockSpec((N_EXPERTS, LANES), tok),
                   pl.BlockSpec((N_EXPERTS, LANES), const)),
        scratch_shapes=[pltpu.VMEM((N_EXPERTS, LANES), F32)],
        compiler_params=_cparams(("arbitrary", "arbitrary")),
        name="outproj_router",
    )(a_ret, a_hg, w_bf, x, g1, sc2, sh2, nw, rwh, rwl, rb, tri)


def _dispatch_kernel(gap_ref, pos_hbm, dst_hbm, v_ref, xs_hbm, pos_smem, dst_smem, pbuf, zbuf,
                     sem_tab, sem_z, sem_rows, *, n_tiles, slack_row):
    i = pl.program_id(0)
    slot = lax.rem(i, 2)
    tile = COMBINE_TILE
    n_asg = tile * TOP_K
    chunk = WIN_ROWS * ROW_TILES

    def table_copies(j, s):
        return (pltpu.make_async_copy(pos_hbm.at[j], pos_smem.at[pl.ds(s * n_asg, n_asg)],
                                      sem_tab.at[s]),
                pltpu.make_async_copy(dst_hbm.at[j], dst_smem.at[pl.ds(s * n_asg, n_asg)],
                                      sem_tab.at[s]))

    def zero_copy(row0, rows):
        return pltpu.make_async_copy(zbuf.at[pl.ds(0, rows), :], xs_hbm.at[pl.ds(row0, rows), :], sem_z)

    @pl.when(i == 0)
    def _():
        for cp in table_copies(0, 0):
            cp.start()
        pbuf[...] = jnp.zeros_like(pbuf)
        zbuf[...] = jnp.zeros_like(zbuf)
        for e in range(N_EXPERTS):
            zero_copy(pl.multiple_of(gap_ref[e], ROW_TILES), chunk).start()
        for j in range(ROW_BLOCK // SUB_ROWS):
            zero_copy(slack_row + j * SUB_ROWS * ROW_TILES, SUB_ROWS * ROW_TILES).start()
        for e in range(N_EXPERTS):
            zero_copy(0, chunk).wait()
        for j in range(ROW_BLOCK // SUB_ROWS):
            zero_copy(0, SUB_ROWS * ROW_TILES).wait()

    for cp in table_copies(i, slot):
        cp.wait()

    @pl.when(i + 1 < n_tiles)
    def _():
        for cp in table_copies(i + 1, 1 - slot):
            cp.start()

    base = slot * n_asg

    def place(t, carry):
        row = v_ref[pl.ds(pl.multiple_of(t * ROW_TILES, ROW_TILES), ROW_TILES), :]
        for k in range(TOP_K):
            p = pl.multiple_of(pos_smem[base + t * TOP_K + k], ROW_TILES)
            pbuf[pl.ds(p, ROW_TILES), :] = row
        return carry
    lax.fori_loop(0, tile, place, 0, unroll=4)

    n_real = dst_smem[base + WIN_CHUNKS]

    def chunk_copy(c):
        dst = pl.multiple_of(dst_smem[base + c], ROW_TILES)
        return pltpu.make_async_copy(pbuf.at[pl.ds(pl.multiple_of(c * chunk, chunk), chunk), :],
                                     xs_hbm.at[pl.ds(dst, chunk), :], sem_rows)

    def send(c, carry):
        chunk_copy(c).start()
        return carry
    lax.fori_loop(0, n_real, send, 0)

    def drain(c, carry):
        chunk_copy(c).wait()
        return carry
    lax.fori_loop(0, n_real, drain, 0)


def _dispatch(gap_rows, pos_tiles, dst_tiles, v):
    tile = COMBINE_TILE
    T = v.shape[0] // ROW_TILES
    n_asg = tile * TOP_K
    n_rows = T * TOP_K + N_EXPERTS * WIN_ROWS
    return pl.pallas_call(
        functools.partial(_dispatch_kernel, n_tiles=T // tile, slack_row=n_rows * ROW_TILES),
        out_shape=jax.ShapeDtypeStruct(((n_rows + ROW_BLOCK) * ROW_TILES, LANES), F32),
        grid_spec=pltpu.PrefetchScalarGridSpec(
            num_scalar_prefetch=1,
            grid=(T // tile,),
            in_specs=[pl.BlockSpec(memory_space=pl.ANY),
                      pl.BlockSpec(memory_space=pl.ANY),
                      pl.BlockSpec((tile * ROW_TILES, LANES), lambda i, gap: (i, 0))],
            out_specs=pl.BlockSpec(memory_space=pl.ANY),
            scratch_shapes=[pltpu.SMEM((2 * n_asg,), I32),
                            pltpu.SMEM((2 * n_asg,), I32),
                            pltpu.VMEM((WIN_CHUNKS * WIN_ROWS * ROW_TILES, LANES), F32),
                            pltpu.VMEM((SUB_ROWS * ROW_TILES, LANES), F32),
                            pltpu.SemaphoreType.DMA((2,)),
                            pltpu.SemaphoreType.DMA,
                            pltpu.SemaphoreType.DMA]),
        compiler_params=_cparams(("arbitrary",)),
        name="dispatch",
    )(gap_rows, pos_tiles, dst_tiles, v)


def _moe_kernel(be_ref, ns_ref, nx_ref, xr_ref, x_hbm, w1_hbm, b1_ref, w2_hbm, b2_ref, y_ref,
                w1_stage, w2_stage, w1_scr, w2_scr, xbuf, x_scr, act_scr, sem_w, sem_x,
                *, n_blocks):
    i = pl.program_id(0)
    slot = lax.rem(i, 2)
    e = be_ref[i]
    nsub = ns_ref[i]
    changed = jnp.logical_or(i == 0, e != be_ref[jnp.maximum(i - 1, 0)])

    def x_copy(j, s):
        r0 = pl.multiple_of(xr_ref[j], ROW_TILES)
        return pltpu.make_async_copy(x_hbm.at[pl.ds(r0, ROW_BLOCK * ROW_TILES), :], xbuf.at[s],
                                     sem_x.at[s])

    @pl.when(i == 0)
    def _():
        x_copy(0, 0).start()

    @pl.when(nsub > 0)
    def _():
        x_copy(i, slot).wait()

    nxt_blk = jnp.minimum(i + 1, n_blocks - 1)

    @pl.when(jnp.logical_and(i + 1 < n_blocks, ns_ref[nxt_blk] > 0))
    def _():
        x_copy(nxt_blk, 1 - slot).start()

    def weight_copies(ex):
        return (pltpu.make_async_copy(w1_hbm.at[ex], w1_stage, sem_w.at[0]),
                pltpu.make_async_copy(w2_hbm.at[ex], w2_stage, sem_w.at[1]))

    @pl.when(i == 0)
    def _():
        for cp in weight_copies(e):
            cp.start()

    @pl.when(jnp.logical_and(changed, nsub > 0))
    def _():
        for cp in weight_copies(e):
            cp.wait()
        w1_scr[...] = w1_stage[...].astype(BF16)
        w2_scr[...] = w2_stage[...].astype(BF16)
        nxt = nx_ref[e]

        @pl.when(nxt >= 0)
        def _():
            for cp in weight_copies(nxt):
                cp.start()

    def compute(rows):
        x_scr[0:rows, :] = _load_tile_rows(xbuf.at[slot], rows).astype(BF16)
        cw = 256
        for c in range(D_FF // cw):
            glu = (_dot(x_scr[0:rows, :], w1_scr[:, c * cw:(c + 1) * cw])
                   + b1_ref[:, c * cw:(c + 1) * cw])
            lin = (_dot(x_scr[0:rows, :], w1_scr[:, D_FF + c * cw:D_FF + (c + 1) * cw])
                   + b1_ref[:, D_FF + c * cw:D_FF + (c + 1) * cw])
            glu = jnp.minimum(glu, SWIGLU_LIMIT)
            lin = jnp.clip(lin, -SWIGLU_LIMIT, SWIGLU_LIMIT)
            act = glu * jax.nn.sigmoid(SWIGLU_ALPHA * glu) * (lin + 1.0)
            act_scr[0:rows, c * cw:(c + 1) * cw] = act.astype(BF16)
        for c in range(D_MODEL // cw):
            y = (_dot(act_scr[0:rows, :], w2_scr[:, c * cw:(c + 1) * cw])
                 + b2_ref[:, c * cw:(c + 1) * cw])
            for s in range(cw // LANES):
                t = c * (cw // LANES) + s
                y_ref[pl.ds(t, rows, stride=ROW_TILES), :] = y[:, s * LANES:(s + 1) * LANES]
        if rows < ROW_BLOCK:
            y_ref[rows * ROW_TILES:, :] = jnp.zeros(((ROW_BLOCK - rows) * ROW_TILES, LANES), F32)

    for m in range(1, ROW_BLOCK // SUB_ROWS + 1):
        @pl.when(nsub == m)
        def _(m=m):
            compute(m * SUB_ROWS)

    @pl.when(nsub == 0)
    def _():
        y_ref[...] = jnp.zeros_like(y_ref)


def _moe(block_e, nsub, next_e, xrow, xs, w1, b1, w2, b2, *, n_blocks):
    return pl.pallas_call(
        functools.partial(_moe_kernel, n_blocks=n_blocks),
        out_shape=jax.ShapeDtypeStruct((n_blocks * ROW_BLOCK * ROW_TILES, LANES), F32),
        grid_spec=pltpu.PrefetchScalarGridSpec(
            num_scalar_prefetch=4,
            grid=(n_blocks,),
            in_specs=[pl.BlockSpec(memory_space=pl.ANY),
                      pl.BlockSpec(memory_space=pl.ANY),
                      pl.BlockSpec((None, 1, 2 * D_FF), lambda i, be, ns, nx, xr: (be[i], 0, 0)),
                      pl.BlockSpec(memory_space=pl.ANY),
                      pl.BlockSpec((None, 1, D_MODEL), lambda i, be, ns, nx, xr: (be[i], 0, 0))],
            out_specs=pl.BlockSpec((ROW_BLOCK * ROW_TILES, LANES),
                                   lambda i, be, ns, nx, xr: (i, 0)),
            scratch_shapes=[pltpu.VMEM((D_MODEL, 2 * D_FF), F32),
                            pltpu.VMEM((D_FF, D_MODEL), F32),
                            pltpu.VMEM((D_MODEL, 2 * D_FF), BF16),
                            pltpu.VMEM((D_FF, D_MODEL), BF16),
                            pltpu.VMEM((2, ROW_BLOCK * ROW_TILES, LANES), F32),
                            pltpu.VMEM((ROW_BLOCK, D_MODEL), BF16),
                            pltpu.VMEM((ROW_BLOCK, D_FF), BF16),
                            pltpu.SemaphoreType.DMA((2,)),
                            pltpu.SemaphoreType.DMA((2,))]),
        compiler_params=_cparams(("arbitrary",)),
        name="moe_ffn",
    )(block_e, nsub, next_e, xrow, xs, w1, b1, w2, b2)


def _combine_kernel(pos_hbm, src_hbm, yb_hbm, gate_ref, h1_ref, g2_ref, nw_ref, o_ref,
                    pos_smem, src_smem, win, hbuf, sem_tab, sem_win, *, n_tiles):
    i = pl.program_id(0)
    slot = lax.rem(i, 2)
    tile = COMBINE_TILE
    n_asg = tile * TOP_K
    chunk = WIN_ROWS * ROW_TILES
    slot_rows = WIN_CHUNKS * chunk

    def table_copies(j, s):
        return (pltpu.make_async_copy(pos_hbm.at[j], pos_smem.at[pl.ds(s * n_asg, n_asg)],
                                      sem_tab.at[s]),
                pltpu.make_async_copy(src_hbm.at[j], src_smem.at[pl.ds(s * n_asg, n_asg)],
                                      sem_tab.at[s]))

    def fetch_window(s):
        def one(c, carry):
            src = pl.multiple_of(src_smem[s * n_asg + c], ROW_TILES)
            dst = pl.multiple_of(s * slot_rows + c * chunk, chunk)
            pltpu.make_async_copy(yb_hbm.at[pl.ds(src, chunk), :], win.at[pl.ds(dst, chunk), :],
                                  sem_win.at[s]).start()
            return carry
        lax.fori_loop(0, WIN_CHUNKS, one, 0, unroll=2)

    @pl.when(i == 0)
    def _():
        for cp in table_copies(0, 0):
            cp.start()
        for cp in table_copies(0, 0):
            cp.wait()
        fetch_window(0)
        if n_tiles > 1:
            for cp in table_copies(1, 1):
                cp.start()

    @pl.when(i + 1 < n_tiles)
    def _():
        for cp in table_copies(i + 1, 1 - slot):
            cp.wait()
        fetch_window(1 - slot)

    pltpu.make_async_copy(yb_hbm.at[pl.ds(0, slot_rows), :],
                          win.at[pl.ds(pl.multiple_of(slot * slot_rows, slot_rows), slot_rows), :],
                          sem_win.at[slot]).wait()

    g2 = g2_ref[...]
    base = slot * n_asg

    def token(t, carry):
        e0 = base + t * TOP_K
        acc = None
        for k in range(TOP_K):
            p = pl.multiple_of(pos_smem[e0 + k], ROW_TILES)
            term = gate_ref[pl.ds(t * TOP_K + k, 1), :] * win[pl.ds(p, ROW_TILES), :]
            acc = term if acc is None else acc + term
        r0 = pl.multiple_of(t * ROW_TILES, ROW_TILES)
        hbuf[pl.ds(r0, ROW_TILES), :] = h1_ref[pl.ds(r0, ROW_TILES), :] + g2 * acc
        return carry
    lax.fori_loop(0, tile, token, 0, unroll=4)

    @pl.when(i + 2 < n_tiles)
    def _():
        for cp in table_copies(i + 2, slot):
            cp.start()

    h = _load_tile_rows(hbuf, tile)
    ms = jnp.mean(h * h, axis=-1, keepdims=True)
    o_ref[...] = h * lax.rsqrt(ms + EPS) * nw_ref[...]


def _combine(pos_tiles, src_tiles, yb, gate_rows, h1, g2_tiles, nw, *, tiles_per_batch):
    tile = COMBINE_TILE
    T = h1.shape[0] // ROW_TILES
    n_asg = tile * TOP_K
    return pl.pallas_call(
        functools.partial(_combine_kernel, n_tiles=T // tile),
        out_shape=jax.ShapeDtypeStruct((T, D_MODEL), F32),
        grid=(T // tile,),
        in_specs=[pl.BlockSpec(memory_space=pl.ANY),
                  pl.BlockSpec(memory_space=pl.ANY),
                  pl.BlockSpec(memory_space=pl.ANY),
                  pl.BlockSpec((tile * TOP_K, LANES), lambda i: (i, 0)),
                  pl.BlockSpec((tile * ROW_TILES, LANES), lambda i: (i, 0)),
                  pl.BlockSpec((None, ROW_TILES, LANES), lambda i: (i // tiles_per_batch, 0, 0)),
                  pl.BlockSpec((1, D_MODEL), lambda i: (0, 0))],
        out_specs=pl.BlockSpec((tile, D_MODEL), lambda i: (i, 0)),
        scratch_shapes=[pltpu.SMEM((2 * n_asg,), I32),
                        pltpu.SMEM((2 * n_asg,), I32),
                        pltpu.VMEM((2 * WIN_CHUNKS * WIN_ROWS * ROW_TILES, LANES), F32),
                        pltpu.VMEM((tile * ROW_TILES, LANES), F32),
                        pltpu.SemaphoreType.DMA((2,)),
                        pltpu.SemaphoreType.DMA((2,))],
        compiler_params=_cparams(("arbitrary",)),
        name="combine_norm",
    )(pos_tiles, src_tiles, yb, gate_rows, h1, g2_tiles, nw)


def _rope_tables(n):
    rows = n // GRID_W
    row = jnp.repeat(jnp.arange(rows, dtype=F32), GRID_W)
    col = jnp.tile(jnp.arange(GRID_W, dtype=F32), rows)
    n_freq = RET_DK // 4
    inv = ROPE_BASE ** (-jnp.arange(n_freq, dtype=F32) / n_freq)
    ang = jnp.concatenate([row[:, None] * inv, col[:, None] * inv], axis=-1)
    cos, sin = jnp.cos(ang), jnp.sin(ang)
    cos_h = jnp.concatenate([cos, cos], axis=-1)
    sin_h = jnp.concatenate([-sin, sin], axis=-1)
    return jnp.tile(cos_h, (1, RET_HEADS)), jnp.tile(sin_h, (1, RET_HEADS))


def kernel(x, c, ctx, c_ctx, w_ada, b_ada, norm_mix_w, norm_ffn_w, w_in, w_out, hg_lb,
           hg_norm_w, router_w, router_b, w1, b1, w2, b2, norm_final_w):
    B, N, D = x.shape
    C = ctx.shape[1]
    T = B * N
    assert D == D_MODEL and w_ada.shape[0] == 1

    cc = jnp.concatenate([c.astype(F32), c_ctx.astype(F32)[None, :],
                          jnp.zeros((16 - B - 1, D), F32)], axis=0)
    mod = _ada(cc, w_ada[0], b_ada[0][None, :])
    mod = mod.reshape(16, 6, 1, D).transpose(1, 0, 2, 3)
    sh1, sc1, g1, sh2, sc2, g2 = (mod[i] for i in range(6))

    w_in_bf = w_in[0].astype(BF16)
    w_out_bf = w_out[0].astype(BF16)
    nw_mix = norm_mix_w[0][None, :]
    cos_t, sin_t = _rope_tables(N)

    pb, lf = _inproj(x, sc1, sh1, None, nw_mix, w_in_bf, hg_lb[:2], cos_t, sin_t,
                     rope=True, tm=INPROJ_TILE)
    pbc, lfc = _inproj(ctx, sc1, sh1, B, nw_mix, w_in_bf, hg_lb[:2], cos_t[:C], sin_t[:C],
                       rope=False, tm=C)

    a_ret = _retention(pb, pbc)
    a_hg = _hgrn(pb, lf, pbc, lfc, hg_norm_w[0][None, :])

    rw_t = router_w[0].T
    rwh = rw_t.astype(BF16)
    rwl = (rw_t - rwh.astype(F32)).astype(BF16)
    rb = jnp.broadcast_to(router_b[0][:, None], (N_EXPERTS, LANES))
    h1, v, route_t, gate_rows, base_cnt, cnt = _outproj(a_ret, a_hg, w_out_bf, x, g1, sc2, sh2,
                                                        norm_ffn_w[0][None, :], rwh, rwl, rb,
                                                        tm=COMBINE_TILE)

    idx = route_t[0:TOP_K].astype(I32)
    rank = route_t[2 * TOP_K:3 * TOP_K].astype(I32)
    counts = cnt[:, 0].astype(I32)
    padded = (counts + ROW_BLOCK - 1) // ROW_BLOCK * ROW_BLOCK
    pend = jnp.cumsum(padded)
    pstart = pend - padded
    onehot = idx[None] == jnp.arange(N_EXPERTS, dtype=I32)[:, None, None]
    xstart = jnp.cumsum(counts) - counts + jnp.arange(N_EXPERTS, dtype=I32) * WIN_ROWS

    n_blocks = (T * TOP_K) // ROW_BLOCK + N_EXPERTS
    starts = jnp.arange(n_blocks, dtype=I32) * ROW_BLOCK
    block_e = jnp.minimum(jnp.sum((pend[None, :] <= starts[:, None]).astype(I32), axis=1),
                          N_EXPERTS - 1)
    valid = jnp.clip(counts[block_e] - (starts - pstart[block_e]), 0, ROW_BLOCK)
    nsub = (valid + SUB_ROWS - 1) // SUB_ROWS
    first_from = lax.cummin(jnp.where(counts > 0, jnp.arange(N_EXPERTS, dtype=I32), N_EXPERTS),
                            reverse=True)
    next_e = jnp.concatenate([first_from[1:], jnp.full((1,), N_EXPERTS, I32)])
    next_e = jnp.where(next_e == N_EXPERTS, -1, next_e)
    xrow = jnp.where(nsub > 0, xstart[block_e] + starts - pstart[block_e], 0) * ROW_TILES

    tc = COMBINE_TILE
    n_ct = T // tc
    base = base_cnt.reshape(n_ct, N_EXPERTS, LANES)[:, :, 0].astype(I32)
    n_run = jnp.concatenate([base[1:], counts[None, :]], axis=0) - base
    n_chunk = (n_run + WIN_ROWS - 1) // WIN_ROWS
    chunk_end = jnp.cumsum(n_chunk, axis=1)
    chunk_start = chunk_end - n_chunk
    win_off = chunk_start * WIN_ROWS
    c_id = jnp.arange(WIN_CHUNKS, dtype=I32)
    c_exp = jnp.sum((chunk_end[:, None, :] <= c_id[None, :, None]).astype(I32), axis=2)
    c_real = c_exp < N_EXPERTS
    c_own = c_exp[:, :, None] == jnp.arange(N_EXPERTS, dtype=I32)
    take = lambda tab: jnp.sum(jnp.where(c_own, tab[:, None, :], 0), axis=2)
    c_local = (c_id[None, :] - take(chunk_start)) * WIN_ROWS

    def chunk_rows(first_row):
        rows = jnp.where(c_real, take(first_row[None, :] + base) + c_local, 0) * ROW_TILES
        return jnp.concatenate([rows, chunk_end[:, -1:],
                                jnp.zeros((n_ct, tc * TOP_K - WIN_CHUNKS - 1), I32)], axis=1)

    def window_pos(slot_off):
        shift = jnp.repeat(win_off - base + slot_off[:, None], tc, axis=0).T
        pos = (jnp.sum(jnp.where(onehot, shift[:, None, :], 0), axis=0) + rank) * ROW_TILES
        return pos.reshape(TOP_K, n_ct, tc).transpose(1, 2, 0).reshape(n_ct, tc * TOP_K)

    no_off = jnp.zeros((n_ct,), I32)
    xs = _dispatch((xstart + counts) * ROW_TILES, window_pos(no_off), chunk_rows(xstart), v)
    yb = _moe(block_e, nsub, next_e, xrow.astype(I32), xs, w1[0], b1[0][:, None, :], w2[0],
              b2[0][:, None, :], n_blocks=n_blocks)
    slot_off = (jnp.arange(n_ct, dtype=I32) % 2) * (WIN_CHUNKS * WIN_ROWS)
    out = _combine(window_pos(slot_off), chunk_rows(pstart), yb, gate_rows, h1,
                   g2.reshape(16, ROW_TILES, LANES), norm_final_w[None, :],
                   tiles_per_batch=N // tc)
    return out.reshape(B, N, D)
```

```python
import functools
import math

import jax
import jax.numpy as jnp
from jax import lax
from jax.experimental import pallas as pl
from jax.experimental.pallas import tpu as pltpu

F32 = jnp.float32
BF16 = jnp.bfloat16
I32 = jnp.int32

D_MODEL = 1024
GRID_W = 64
RET_HEADS = 4
RET_DK = 64
HG_HEADS = 4
PROJ_W = 4096
ROPE_BASE = 10000.0
EPS = 1e-6
N_EXPERTS = 32
TOP_K = 4
D_FF = 1024
SWIGLU_LIMIT = 7.0
SWIGLU_ALPHA = 1.702
LOG2_E = 1.0 / math.log(2.0)

LANES = 128
CHUNK = 128
INPROJ_TILE = 1024
DISPATCH_TILE = 512
ROW_BLOCK = 1024
SUB_ROWS = 256
COMBINE_TILE = 512
WIN_ROWS = 32
WIN_CHUNKS = (COMBINE_TILE * TOP_K + N_EXPERTS * (WIN_ROWS - 1) + WIN_ROWS - 1) // WIN_ROWS
VMEM_LIMIT = 56 * 1024 * 1024
ROW_TILES = D_MODEL // LANES

C_RQ, C_RK, C_RV, C_RG, C_HQ, C_FF, C_FB, C_HV, C_HG = (
    0, 256, 512, 1024, 1536, 2048, 2560, 3072, 3584)


def _cparams(sem):
    return pltpu.CompilerParams(dimension_semantics=sem, vmem_limit_bytes=VMEM_LIMIT)


def _split_bf16(x):
    hi = x.astype(BF16)
    lo = (x - hi.astype(F32)).astype(BF16)
    return hi, lo


def _dot(a, b):
    return jnp.dot(a, b, preferred_element_type=F32)


def _dot_nt(a, b):
    return lax.dot_general(a, b, (((1,), (1,)), ((), ())), preferred_element_type=F32)


def _dot3(a, b):
    ah, al = _split_bf16(a)
    bh, bl = _split_bf16(b)
    return _dot(ah, bh) + (_dot(ah, bl) + _dot(al, bh))


def _silu(x):
    return x * jax.nn.sigmoid(x)


def _load_tile_rows(ref, n):
    return jnp.concatenate([ref[pl.ds(s, n, stride=ROW_TILES), :] for s in range(ROW_TILES)],
                           axis=1)


def _store_tile_rows(ref, x):
    n = x.shape[0]
    for s in range(ROW_TILES):
        ref[pl.ds(s, n, stride=ROW_TILES), :] = x[:, s * LANES:(s + 1) * LANES]


def _ada_kernel(c_ref, w_ref, b_ref, o_ref):
    s = _silu(c_ref[...])
    o_ref[...] = _dot3(s, w_ref[...]) + b_ref[...]


def _ada(cc, w, b):
    nblk = w.shape[1] // D_MODEL
    return pl.pallas_call(
        _ada_kernel,
        out_shape=jax.ShapeDtypeStruct((cc.shape[0], w.shape[1]), F32),
        grid=(nblk,),
        in_specs=[pl.BlockSpec(cc.shape, lambda j: (0, 0)),
                  pl.BlockSpec((D_MODEL, D_MODEL), lambda j: (0, j)),
                  pl.BlockSpec((1, D_MODEL), lambda j: (0, j))],
        out_specs=pl.BlockSpec((cc.shape[0], D_MODEL), lambda j: (0, j)),
        compiler_params=_cparams(("arbitrary",)),
        name="ada",
    )(cc, w, b)


def _inproj_kernel(x_ref, sc_ref, sh_ref, nw_ref, w_ref, lb_ref, cos_ref, sin_ref,
                   pb_ref, lf_ref, u_scr, *, rope):
    x = x_ref[...]
    ms = jnp.mean(x * x, axis=-1, keepdims=True)
    u = x * lax.rsqrt(ms + EPS) * nw_ref[...] * (1.0 + sc_ref[...]) + sh_ref[...]
    u_scr[...] = u.astype(BF16)

    def proj(lo, width):
        return _dot(u_scr[...], w_ref[:, lo:lo + width])

    tm = x.shape[0]
    if rope:
        lane = lax.broadcasted_iota(I32, (tm, LANES), 1)
        first = (lane & 32) == 0

    def put_rot(col, scale):
        for j in range(2):
            lo = col + j * LANES
            t = proj(lo, LANES)
            if scale != 1.0:
                t = t * scale
            if rope:
                tb = j * LANES
                rot = jnp.where(first, pltpu.roll(t, 96, axis=1), pltpu.roll(t, 32, axis=1))
                t = t * cos_ref[:, tb:tb + LANES] + rot * sin_ref[:, tb:tb + LANES]
            pb_ref[:, lo:lo + LANES] = t.astype(BF16)

    put_rot(C_RQ, 1.0)
    put_rot(C_RK, RET_DK ** -0.5)
    pb_ref[:, C_RV:C_RV + 512] = proj(C_RV, 512).astype(BF16)
    pb_ref[:, C_RG:C_RG + 512] = _silu(proj(C_RG, 512)).astype(BF16)
    pb_ref[:, C_HQ:C_HQ + 512] = _silu(proj(C_HQ, 512)).astype(BF16)
    pb_ref[:, C_HV:C_HV + 512] = proj(C_HV, 512).astype(BF16)
    pb_ref[:, C_HG:C_HG + 512] = _silu(proj(C_HG, 512)).astype(BF16)

    la = lb_ref[0]
    lbb = lb_ref[1]
    mx = jnp.maximum(la, lbb)
    ea = jnp.exp(la - mx)
    eb = jnp.exp(lbb - mx)
    lb = ea / (ea + eb)
    for d, col in enumerate((C_FF, C_FB)):
        lbd = lb[d:d + 1, :]
        f = lbd + (1.0 - lbd) * jax.nn.sigmoid(proj(col, 512))
        pb_ref[:, col:col + 512] = (1.0 - f).astype(BF16)
        lf_ref[:, d * 512:(d + 1) * 512] = jnp.log(f) * LOG2_E


def _inproj(x, sc, sh, mod_row, nw, w_bf, hg_lb, cos_t, sin_t, *, rope, tm):
    B, n, _ = x.shape
    nt = n // tm
    if mod_row is None:
        mrow = lambda b, j: (b, 0, 0)
    else:
        mrow = lambda b, j: (mod_row, 0, 0)
    return pl.pallas_call(
        functools.partial(_inproj_kernel, rope=rope),
        out_shape=(jax.ShapeDtypeStruct((B, n, PROJ_W), BF16),
                   jax.ShapeDtypeStruct((B, n, 1024), F32)),
        grid=(B, nt),
        in_specs=[pl.BlockSpec((None, tm, D_MODEL), lambda b, j: (b, j, 0)),
                  pl.BlockSpec((None, 1, D_MODEL), mrow),
                  pl.BlockSpec((None, 1, D_MODEL), mrow),
                  pl.BlockSpec((1, D_MODEL), lambda b, j: (0, 0)),
                  pl.BlockSpec((D_MODEL, PROJ_W), lambda b, j: (0, 0),
                               pipeline_mode=pl.Buffered(1)),
                  pl.BlockSpec((2, 2, 512), lambda b, j: (0, 0, 0)),
                  pl.BlockSpec((tm, 256), lambda b, j: (j, 0)),
                  pl.BlockSpec((tm, 256), lambda b, j: (j, 0))],
        out_specs=(pl.BlockSpec((None, tm, PROJ_W), lambda b, j: (b, j, 0)),
                   pl.BlockSpec((None, tm, 1024), lambda b, j: (b, j, 0))),
        scratch_shapes=[pltpu.VMEM((tm, D_MODEL), BF16)],
        compiler_params=_cparams(("arbitrary", "arbitrary")),
        name="inproj_rope" if rope else "inproj_ctx",
    )(x, sc, sh, nw, w_bf, hg_lb, cos_t, sin_t)


_RET_LGF = [math.log1p(-(2.0 ** (-5.0 - 2.0 * h))) for h in range(RET_HEADS)]
_RET_LGB = [math.log1p(-(2.0 ** (-6.0 - 2.0 * h))) for h in range(RET_HEADS)]


def _ret_kernel(q_ref, k_ref, v_ref, g_ref, kc_ref, vc_ref, o_ref,
                u_scr, sin_scr, kt_scr, dtot_scr, *, n_lat, n_ctx):
    L = CHUNK
    pair = pl.program_id(1)
    row = lax.broadcasted_iota(I32, (L, L), 0).astype(F32)
    col = lax.broadcasted_iota(I32, (L, L), 1).astype(F32)
    lane = lax.broadcasted_iota(I32, (L, LANES), 1)
    trow = lax.broadcasted_iota(I32, (L, 1), 0).astype(F32)
    tcol = lax.broadcasted_iota(I32, (1, L), 1).astype(F32)
    low_half = lane < RET_DK

    def u_chunk(k_blk, v_blk, ci, store_kt):
        kt = k_blk.astype(F32).T
        if store_kt is not None:
            kt_scr[store_kt] = kt.astype(BF16)
        for hh in range(2):
            lgf = jnp.where(pair == 0, _RET_LGF[hh], _RET_LGF[2 + hh])
            lgb = jnp.where(pair == 0, _RET_LGB[hh], _RET_LGB[2 + hh])
            kth = kt[hh * RET_DK:(hh + 1) * RET_DK, :]
            wkf = jnp.exp(lgf * (L - 1.0 - tcol))
            wkb = jnp.exp(lgb * tcol)
            lhs = jnp.concatenate([kth * wkf, kth * wkb], axis=0).astype(BF16)
            u_scr[hh, ci] = _dot(lhs, v_blk[:, hh * LANES:(hh + 1) * LANES])

    for c in range(n_ctx):
        u_chunk(kc_ref[c * L:(c + 1) * L, :], vc_ref[c * L:(c + 1) * L, :], c, None)

    def lat_u(c, carry):
        r0 = pl.multiple_of(c * L, L)
        u_chunk(k_ref[pl.ds(r0, L), :], v_ref[pl.ds(r0, L), :], n_ctx + c, c)
        return carry
    lax.fori_loop(0, n_lat, lat_u, 0, unroll=8)

    ones = jnp.ones((RET_DK, LANES), F32)
    for hh in range(2):
        lgf = jnp.where(pair == 0, _RET_LGF[hh], _RET_LGF[2 + hh])
        lgb = jnp.where(pair == 0, _RET_LGB[hh], _RET_LGB[2 + hh])
        d = row - col
        dtot_scr[hh] = jnp.where(d > 0, jnp.exp(lgf * jnp.maximum(d, 0.0)),
                                 jnp.where(d < 0, jnp.exp(lgb * jnp.maximum(-d, 0.0)), 2.0))
        af = jnp.exp(ones * (lgf * L))
        ab = jnp.exp(ones * (lgb * L))

        s = jnp.zeros((RET_DK, LANES), F32)
        for c in range(n_ctx):
            s = af * s + u_scr[hh, c, 0:RET_DK, :]
        sb = jnp.zeros((RET_DK, LANES), F32)
        for c in reversed(range(n_ctx)):
            sb = ab * sb + u_scr[hh, c, RET_DK:2 * RET_DK, :]

        def fwd(c, s, hh=hh, af=af):
            sin_scr[hh, c, 0:RET_DK, :] = s.astype(BF16)
            return af * s + u_scr[hh, n_ctx + c, 0:RET_DK, :]
        lax.fori_loop(0, n_lat, fwd, s)

        def bwd(i, sb, hh=hh, ab=ab):
            c = n_lat - 1 - i
            sin_scr[hh, c, RET_DK:2 * RET_DK, :] = sb.astype(BF16)
            return ab * sb + u_scr[hh, n_ctx + c, RET_DK:2 * RET_DK, :]
        lax.fori_loop(0, n_lat, bwd, sb)

    def out_chunk(c, carry):
        r0 = pl.multiple_of(c * L, L)
        q = q_ref[pl.ds(r0, L), :].astype(F32)
        qr = pltpu.roll(q, RET_DK, axis=1)
        kt = kt_scr[c]
        for hh in range(2):
            lgf = jnp.where(pair == 0, _RET_LGF[hh], _RET_LGF[2 + hh])
            lgb = jnp.where(pair == 0, _RET_LGB[hh], _RET_LGB[2 + hh])
            mine = low_half if hh == 0 else jnp.logical_not(low_half)
            qm = jnp.where(mine, q, 0.0).astype(BF16)
            p = (_dot(qm, kt) * dtot_scr[hh]).astype(BF16)
            vh = v_ref[pl.ds(r0, L), hh * LANES:(hh + 1) * LANES]
            wqf = jnp.exp(lgf * (trow + 1.0))
            wqb = jnp.exp(lgb * (L - trow))
            qa, qb = (q, qr) if hh == 0 else (qr, q)
            qs = jnp.where(low_half, qa * wqf, qb * wqb).astype(BF16)
            o = _dot(p, vh) + _dot(qs, sin_scr[hh, c])
            ms = jnp.mean(o * o, axis=-1, keepdims=True)
            gh = g_ref[pl.ds(r0, L), hh * LANES:(hh + 1) * LANES].astype(F32)
            o_ref[pl.ds(r0, L), hh * LANES:(hh + 1) * LANES] = (
                o * lax.rsqrt(ms + EPS) * gh).astype(BF16)
        return carry
    lax.fori_loop(0, n_lat, out_chunk, 0, unroll=8)


def _retention(pb, pbc):
    B, n, _ = pb.shape
    nc = pbc.shape[1]
    n_lat, n_ctx = n // CHUNK, nc // CHUNK
    return pl.pallas_call(
        functools.partial(_ret_kernel, n_lat=n_lat, n_ctx=n_ctx),
        out_shape=jax.ShapeDtypeStruct((B, n, 512), BF16),
        grid=(B, 2),
        in_specs=[pl.BlockSpec((None, n, LANES), lambda b, p: (b, 0, C_RQ // LANES + p)),
                  pl.BlockSpec((None, n, LANES), lambda b, p: (b, 0, C_RK // LANES + p)),
                  pl.BlockSpec((None, n, 256), lambda b, p: (b, 0, C_RV // 256 + p)),
                  pl.BlockSpec((None, n, 256), lambda b, p: (b, 0, C_RG // 256 + p)),
                  pl.BlockSpec((None, nc, LANES), lambda b, p: (b, 0, C_RK // LANES + p)),
                  pl.BlockSpec((None, nc, 256), lambda b, p: (b, 0, C_RV // 256 + p))],
        out_specs=pl.BlockSpec((None, n, 256), lambda b, p: (b, 0, p)),
        scratch_shapes=[pltpu.VMEM((2, n_lat + n_ctx, CHUNK, LANES), F32),
                        pltpu.VMEM((2, n_lat, CHUNK, LANES), BF16),
                        pltpu.VMEM((n_lat, LANES, CHUNK), BF16),
                        pltpu.VMEM((2, CHUNK, CHUNK), F32)],
        compiler_params=_cparams(("arbitrary", "arbitrary")),
        name="retention",
    )(pb, pb, pb, pb, pbc, pbc)


_LEVELS = (64, 32, 16, 8, 4, 2, 1)


def _expand_rows(r, rep):
    n = r.shape[0]
    if n == 1:
        return jnp.broadcast_to(r, (rep, r.shape[1]))
    return jnp.concatenate(
        [jnp.broadcast_to(r[i:i + 1, :], (rep, r.shape[1])) for i in range(n)], axis=0)


def _hgrn_kernel(q_ref, kf_ref, kb_ref, v_ref, g_ref, lff_ref, lfb_ref,
                 kfc_ref, kbc_ref, vc_ref, lffc_ref, lfbc_ref, nw_ref, o_ref,
                 ut_scr, a_scr, qs_scr, oi_scr, sin_scr, bfb_scr, *, n_lat, n_ctx):
    L = CHUNK
    row = lax.broadcasted_iota(I32, (L, L), 0)
    col = lax.broadcasted_iota(I32, (L, L), 1)
    xr_bits = lax.bitcast_convert_type((row ^ col).astype(F32), I32)
    lv = lax.shift_right_logical(xr_bits, 23) - 127
    row2 = lax.broadcasted_iota(I32, (L, 2 * L), 0)
    col2 = lax.broadcasted_iota(I32, (L, 2 * L), 1) & (L - 1)
    tril2 = jnp.where(col2 <= row2, 1.0, 0.0).astype(BF16)
    triu2 = jnp.where(col2 >= row2, 1.0, 0.0).astype(BF16)

    def cums(lff, lfb):
        hf, lof = _split_bf16(lff)
        hb, lob = _split_bf16(lfb)
        bf = _dot(tril2, jnp.concatenate([hf, lof], axis=0))
        bb = _dot(triu2, jnp.concatenate([hb, lob], axis=0))
        return bf, bb

    def state_part(ci, kf, kb, v_blk, bf, bb):
        endf = bf[L - 1:L, :]
        endb = bb[0:1, :]
        ksf = kf * jnp.exp2(endf - bf)
        ksb = kb * jnp.exp2(endb - bb)
        vt = v_blk.astype(F32).T.astype(BF16)
        ut_scr[ci] = _dot(vt, jnp.concatenate([ksf, ksb], axis=1).astype(BF16))
        a_scr[ci] = jnp.broadcast_to(
            jnp.concatenate([jnp.exp2(endf), jnp.exp2(endb)], axis=1), (8, 2 * LANES))

    for c in range(n_ctx):
        sl = slice(c * L, (c + 1) * L)
        bf, bb = cums(lffc_ref[sl, :], lfbc_ref[sl, :])
        state_part(c, kfc_ref[sl, :].astype(F32), kbc_ref[sl, :].astype(F32),
                   vc_ref[sl, :], bf, bb)

    def lat_chunk(c, u):
        rows = pl.ds(pl.multiple_of(c * L, L), L)
        bf_scr = bfb_scr.at[u, 0]
        bb_scr = bfb_scr.at[u, 1]
        q = q_ref[rows, :].astype(F32)
        kf = kf_ref[rows, :].astype(F32)
        kb = kb_ref[rows, :].astype(F32)
        v_blk = v_ref[rows, :]
        lff = lff_ref[rows, :]
        lfb = lfb_ref[rows, :]
        bf, bb = cums(lff, lfb)
        state_part(n_ctx + c, kf, kb, v_blk, bf, bb)
        qs_scr[rows, :] = jnp.concatenate([q * jnp.exp2(bf), q * jnp.exp2(bb)],
                                          axis=1).astype(BF16)
        bf_scr[...] = bf
        bb_scr[...] = bb

        acc = jnp.zeros((L, L), F32)
        for lvl, h in enumerate(_LEVELS):
            bit = (row & h) != 0
            ksel = jnp.where(bit, kb, kf)
            if h >= 4:
                n = (L // 2) // h
                if n == 1:
                    rf = bf_scr[h - 1:h, :]
                    rb = bb_scr[h:h + 1, :]
                else:
                    rf = bf_scr[pl.ds(h - 1, n, stride=2 * h), :]
                    rb = bb_scr[pl.ds(h, n, stride=2 * h), :]
                df = bf - _expand_rows(rf, 2 * h)
                db = bb - _expand_rows(rb, 2 * h)
                eq = jnp.where(bit, df, db)
                ek = -jnp.where(bit, db, df)
            elif h == 2:
                m = row & 3
                lff_n = pltpu.roll(lff, L - 1, axis=0)
                lfb_n = pltpu.roll(lfb, L - 1, axis=0)
                eq = jnp.where(m == 2, lff,
                               jnp.where(m == 3, lff + pltpu.roll(lff, 1, axis=0),
                                         jnp.where(m == 0, lfb + lfb_n, lfb)))
                ek = jnp.where(m == 3, pltpu.roll(lfb, 1, axis=0),
                               jnp.where(m == 0, lff_n, 0.0))
            else:
                eq = jnp.where(bit, lff, lfb)
                ek = None
            lhs = (q * jnp.exp2(eq)).astype(BF16)
            rhs = (ksel if ek is None else ksel * jnp.exp2(ek)).astype(BF16)
            acc = jnp.where(lv == 6 - lvl, _dot_nt(lhs, rhs), acc)

        dsum = jnp.sum(q * (kf + kb), axis=-1, keepdims=True)
        oi_scr[rows, :] = _dot(acc.astype(BF16), v_blk) + dsum * v_blk.astype(F32)

    def lat_pair(i, carry):
        for u in range(8):
            lat_chunk(8 * i + u, u)
        return carry
    lax.fori_loop(0, n_lat // 8, lat_pair, 0)

    st = jnp.zeros((LANES, LANES), F32)
    for c in range(n_ctx):
        st = st * a_scr[c, 0:1, 0:LANES] + ut_scr[c, :, 0:LANES]
    stb = jnp.zeros((LANES, LANES), F32)
    for c in reversed(range(n_ctx)):
        stb = stb * a_scr[c, 0:1, LANES:2 * LANES] + ut_scr[c, :, LANES:2 * LANES]

    def fwd(c, st):
        sin_scr[c, :, 0:LANES] = st.astype(BF16)
        ci = n_ctx + c
        return st * a_scr[ci, 0:1, 0:LANES] + ut_scr[ci, :, 0:LANES]
    lax.fori_loop(0, n_lat, fwd, st)

    def bwd(i, stb):
        c = n_lat - 1 - i
        sin_scr[c, :, LANES:2 * LANES] = stb.astype(BF16)
        ci = n_ctx + c
        return stb * a_scr[ci, 0:1, LANES:2 * LANES] + ut_scr[ci, :, LANES:2 * LANES]
    lax.fori_loop(0, n_lat, bwd, stb)

    def out_chunk(c, carry):
        rows = pl.ds(pl.multiple_of(c * L, L), L)
        o = oi_scr[rows, :] + _dot_nt(qs_scr[rows, :], sin_scr[c])
        ms = jnp.mean(o * o, axis=-1, keepdims=True)
        y = o * lax.rsqrt(ms + EPS) * nw_ref[...] * g_ref[rows, :].astype(F32)
        o_ref[rows, :] = y.astype(BF16)
        return carry
    lax.fori_loop(0, n_lat, out_chunk, 0, unroll=8)


def _hgrn(pb, lf, pbc, lfc, nw):
    B, n, _ = pb.shape
    nc = pbc.shape[1]
    n_lat, n_ctx = n // CHUNK, nc // CHUNK

    def colblk(rows, col0):
        return pl.BlockSpec((None, rows, LANES), lambda b, h: (b, 0, col0 // LANES + h))

    return pl.pallas_call(
        functools.partial(_hgrn_kernel, n_lat=n_lat, n_ctx=n_ctx),
        out_shape=jax.ShapeDtypeStruct((B, n, 512), BF16),
        grid=(B, HG_HEADS),
        in_specs=[colblk(n, C_HQ), colblk(n, C_FF), colblk(n, C_FB), colblk(n, C_HV),
                  colblk(n, C_HG), colblk(n, 0), colblk(n, 512),
                  colblk(nc, C_FF), colblk(nc, C_FB), colblk(nc, C_HV),
                  colblk(nc, 0), colblk(nc, 512),
                  pl.BlockSpec((1, LANES), lambda b, h: (0, 0))],
        out_specs=pl.BlockSpec((None, n, LANES), lambda b, h: (b, 0, h)),
        scratch_shapes=[pltpu.VMEM((n_lat + n_ctx, LANES, 2 * LANES), F32),
                        pltpu.VMEM((n_lat + n_ctx, 8, 2 * LANES), F32),
                        pltpu.VMEM((n, 2 * LANES), BF16),
                        pltpu.VMEM((n, LANES), F32),
                        pltpu.VMEM((n_lat, LANES, 2 * LANES), BF16),
                        pltpu.VMEM((8, 2, CHUNK, LANES), F32)],
        compiler_params=_cparams(("arbitrary", "arbitrary")),
        name="hgrn2",
    )(pb, pb, pb, pb, pb, lf, lf, pbc, pbc, pbc, lfc, lfc, nw)


def _outproj_kernel(ar_ref, ah_ref, w_ref, x_ref, g1_ref, sc_ref, sh_ref, nw_ref,
                    rwh_ref, rwl_ref, rb_ref, tri_ref, h1_ref, v_ref, route_t_ref, gate_ref, base_ref,
                    cnt_ref, cnt_scr):
    first_step = jnp.logical_and(pl.program_id(0) == 0, pl.program_id(1) == 0)

    @pl.when(first_step)
    def _():
        cnt_scr[...] = jnp.zeros_like(cnt_scr)

    y = _dot(ar_ref[...], w_ref[0:512, :]) + _dot(ah_ref[...], w_ref[512:1024, :])
    h1 = x_ref[...] + g1_ref[...] * y
    _store_tile_rows(h1_ref, h1)
    ms = jnp.mean(h1 * h1, axis=-1, keepdims=True)
    v = h1 * lax.rsqrt(ms + EPS) * nw_ref[...] * (1.0 + sc_ref[...]) + sh_ref[...]
    _store_tile_rows(v_ref, v)

    tm = v.shape[0]
    vh, vl = _split_bf16(v)
    rwh, rwl = rwh_ref[...], rwl_ref[...]
    l = (_dot_nt(rwh, vh) + (_dot_nt(rwh, vl) + _dot_nt(rwl, vh))) + rb_ref[:, 0:1]
    row_f = lax.broadcasted_iota(I32, (N_EXPERTS, tm), 0).astype(F32)
    sels, tops, idxs = [], [], []
    for _ in range(TOP_K):
        m = jnp.max(l, axis=0, keepdims=True)
        i = jnp.min(jnp.where(l == m, row_f, float(N_EXPERTS)), axis=0, keepdims=True)
        sel = row_f == i
        l = jnp.where(sel, -jnp.inf, l)
        sels.append(sel)
        tops.append(m)
        idxs.append(i)
    es = [jnp.exp(t - tops[0]) for t in tops]
    den = es[0] + es[1] + es[2] + es[3]
    gates = [e / den for e in es]

    oh = jnp.zeros((N_EXPERTS, tm), F32)
    for sel in sels:
        oh = jnp.where(sel, 1.0, oh)
    cnt = cnt_scr[:, 0:1]
    before = _dot(oh.astype(BF16), tri_ref[...]) + cnt
    ranks = [jnp.sum(jnp.where(sel, before, 0.0), axis=0, keepdims=True) for sel in sels]
    base_ref[...] = cnt_scr[...]
    cnt_scr[...] = cnt_scr[...] + jnp.sum(oh, axis=1, keepdims=True)
    cnt_ref[...] = cnt_scr[...]

    zero4 = jnp.zeros((TOP_K, tm), F32)
    route_t_ref[...] = jnp.concatenate(idxs + [zero4] + ranks + [zero4], axis=0)
    g_cols = jnp.concatenate(gates + [zero4], axis=0).T
    for k in range(TOP_K):
        gate_ref[pl.ds(k, tm, stride=TOP_K), :] = jnp.broadcast_to(g_cols[:, k:k + 1], (tm, LANES))


def _outproj(a_ret, a_hg, w_bf, x, g1, sc2, sh2, nw, rwh, rwl, rb, *, tm):
    B, n, _ = x.shape
    nt = n // tm
    T = B * n
    assert tm == COMBINE_TILE
    mrow = lambda b, j: (b, 0, 0)
    tok = lambda b, j: (b * nt + j, 0)
    const = lambda b, j: (0, 0)
    tri = jnp.triu(jnp.ones((tm, tm), BF16), 1)
    return pl.pallas_call(
        _outproj_kernel,
        out_shape=(jax.ShapeDtypeStruct((T * ROW_TILES, LANES), F32),
                   jax.ShapeDtypeStruct((T * ROW_TILES, LANES), F32),
                   jax.ShapeDtypeStruct((16, T), F32),
                   jax.ShapeDtypeStruct((T * TOP_K, LANES), F32),
                   jax.ShapeDtypeStruct((B * nt * N_EXPERTS, LANES), F32),
                   jax.ShapeDtypeStruct((N_EXPERTS, LANES), F32)),
        grid=(B, nt),
        in_specs=[pl.BlockSpec((None, tm, 512), lambda b, j: (b, j, 0)),
                  pl.BlockSpec((None, tm, 512), lambda b, j: (b, j, 0)),
                  pl.BlockSpec((D_MODEL, D_MODEL), const),
                  pl.BlockSpec((None, tm, D_MODEL), lambda b, j: (b, j, 0)),
                  pl.BlockSpec((None, 1, D_MODEL), mrow),
                  pl.BlockSpec((None, 1, D_MODEL), mrow),
                  pl.BlockSpec((None, 1, D_MODEL), mrow),
                  pl.BlockSpec((1, D_MODEL), const),
                  pl.BlockSpec((N_EXPERTS, D_MODEL), const),
                  pl.BlockSpec((N_EXPERTS, D_MODEL), const),
                  pl.BlockSpec((N_EXPERTS, LANES), const),
                  pl.BlockSpec((tm, tm), const)],
        out_specs=(pl.BlockSpec((tm * ROW_TILES, LANES), tok),
                   pl.BlockSpec((tm * ROW_TILES, LANES), tok),
                   pl.BlockSpec((16, tm), lambda b, j: (0, b * nt + j)),
                   pl.BlockSpec((tm * TOP_K, LANES), tok),
                   pl.BlockSpec((N_EXPERTS, LANES), tok),
                   pl.BlockSpec((N_EXPERTS, LANES), const)),
        scratch_shapes=[pltpu.VMEM((N_EXPERTS, LANES), F32)],
        compiler_params=_cparams(("arbitrary", "arbitrary")),
        name="outproj_router",
    )(a_ret, a_hg, w_bf, x, g1, sc2, sh2, nw, rwh, rwl, rb, tri)


def _dispatch_kernel(dest_hbm, v_ref, xs_hbm, idx_smem, zbuf, sem_idx, sem_z, sem_rows,
                     *, tile, n_tiles, n_rows):
    i = pl.program_id(0)
    slot = lax.rem(i, 2)

    def idx_copy(j, s):
        n_asg = tile * TOP_K
        return pltpu.make_async_copy(dest_hbm.at[j], idx_smem.at[pl.ds(s * n_asg, n_asg)],
                                     sem_idx.at[s])

    def zero_copy(j):
        r0 = (n_rows + j * SUB_ROWS) * ROW_TILES
        return pltpu.make_async_copy(zbuf, xs_hbm.at[pl.ds(r0, SUB_ROWS * ROW_TILES), :], sem_z)

    @pl.when(i == 0)
    def _():
        idx_copy(0, 0).start()
        zbuf[...] = jnp.zeros_like(zbuf)
        for j in range(ROW_BLOCK // SUB_ROWS):
            zero_copy(j).start()
        for j in range(ROW_BLOCK // SUB_ROWS):
            zero_copy(j).wait()

    idx_copy(i, slot).wait()

    @pl.when(i + 1 < n_tiles)
    def _():
        idx_copy(i + 1, 1 - slot).start()

    def issue(t, carry):
        e0 = slot * (tile * TOP_K) + t
        for k in range(TOP_K):
            d = pl.multiple_of(idx_smem[e0 + k * tile], ROW_TILES)
            src = v_ref.at[pl.ds(pl.multiple_of(t * ROW_TILES, ROW_TILES), ROW_TILES), :]
            pltpu.make_async_copy(src, xs_hbm.at[pl.ds(d, ROW_TILES), :],
                                  sem_rows).start(priority=k % 2)
        return carry
    lax.fori_loop(0, tile, issue, 0, unroll=8)
    for _ in range(TOP_K):
        pltpu.make_async_copy(v_ref, xs_hbm.at[pl.ds(0, tile * ROW_TILES), :], sem_rows).wait()


def _dispatch(dest_tiles, v, *, tile):
    T = v.shape[0] // ROW_TILES
    n_rows = T * TOP_K
    return pl.pallas_call(
        functools.partial(_dispatch_kernel, tile=tile, n_tiles=T // tile, n_rows=n_rows),
        out_shape=jax.ShapeDtypeStruct(((n_rows + ROW_BLOCK) * ROW_TILES, LANES), F32),
        grid=(T // tile,),
        in_specs=[pl.BlockSpec(memory_space=pl.ANY),
                  pl.BlockSpec((tile * ROW_TILES, LANES), lambda i: (i, 0))],
        out_specs=pl.BlockSpec(memory_space=pl.ANY),
        scratch_shapes=[pltpu.SMEM((2 * tile * TOP_K,), I32),
                        pltpu.VMEM((SUB_ROWS * ROW_TILES, LANES), F32),
                        pltpu.SemaphoreType.DMA((2,)),
                        pltpu.SemaphoreType.DMA,
                        pltpu.SemaphoreType.DMA],
        compiler_params=_cparams(("arbitrary",)),
        name="dispatch",
    )(dest_tiles, v)


def _moe_kernel(be_ref, ns_ref, nx_ref, xr_ref, x_hbm, w1_hbm, b1_ref, w2_hbm, b2_ref, y_ref,
                w1_stage, w2_stage, w1_scr, w2_scr, xbuf, x_scr, act_scr, sem_w, sem_x,
                *, n_blocks):
    i = pl.program_id(0)
    slot = lax.rem(i, 2)
    e = be_ref[i]
    nsub = ns_ref[i]
    changed = jnp.logical_or(i == 0, e != be_ref[jnp.maximum(i - 1, 0)])

    def x_copy(j, s):
        r0 = pl.multiple_of(xr_ref[j], ROW_TILES)
        return pltpu.make_async_copy(x_hbm.at[pl.ds(r0, ROW_BLOCK * ROW_TILES), :], xbuf.at[s],
                                     sem_x.at[s])

    @pl.when(i == 0)
    def _():
        x_copy(0, 0).start()

    @pl.when(nsub > 0)
    def _():
        x_copy(i, slot).wait()

    nxt_blk = jnp.minimum(i + 1, n_blocks - 1)

    @pl.when(jnp.logical_and(i + 1 < n_blocks, ns_ref[nxt_blk] > 0))
    def _():
        x_copy(nxt_blk, 1 - slot).start()

    def weight_copies(ex):
        return (pltpu.make_async_copy(w1_hbm.at[ex], w1_stage, sem_w.at[0]),
                pltpu.make_async_copy(w2_hbm.at[ex], w2_stage, sem_w.at[1]))

    @pl.when(i == 0)
    def _():
        for cp in weight_copies(e):
            cp.start()

    @pl.when(jnp.logical_and(changed, nsub > 0))
    def _():
        for cp in weight_copies(e):
            cp.wait()
        w1_scr[...] = w1_stage[...].astype(BF16)
        w2_scr[...] = w2_stage[...].astype(BF16)
        nxt = nx_ref[e]

        @pl.when(nxt >= 0)
        def _():
            for cp in weight_copies(nxt):
                cp.start()

    def compute(rows):
        x_scr[0:rows, :] = _load_tile_rows(xbuf.at[slot], rows).astype(BF16)
        cw = 256
        for c in range(D_FF // cw):
            glu = (_dot(x_scr[0:rows, :], w1_scr[:, c * cw:(c + 1) * cw])
                   + b1_ref[:, c * cw:(c + 1) * cw])
            lin = (_dot(x_scr[0:rows, :], w1_scr[:, D_FF + c * cw:D_FF + (c + 1) * cw])
                   + b1_ref[:, D_FF + c * cw:D_FF + (c + 1) * cw])
            glu = jnp.minimum(glu, SWIGLU_LIMIT)
            lin = jnp.clip(lin, -SWIGLU_LIMIT, SWIGLU_LIMIT)
            act = glu * jax.nn.sigmoid(SWIGLU_ALPHA * glu) * (lin + 1.0)
            act_scr[0:rows, c * cw:(c + 1) * cw] = act.astype(BF16)
        for c in range(D_MODEL // cw):
            y = (_dot(act_scr[0:rows, :], w2_scr[:, c * cw:(c + 1) * cw])
                 + b2_ref[:, c * cw:(c + 1) * cw])
            for s in range(cw // LANES):
                t = c * (cw // LANES) + s
                y_ref[pl.ds(t, rows, stride=ROW_TILES), :] = y[:, s * LANES:(s + 1) * LANES]
        if rows < ROW_BLOCK:
            y_ref[rows * ROW_TILES:, :] = jnp.zeros(((ROW_BLOCK - rows) * ROW_TILES, LANES), F32)

    for m in range(1, ROW_BLOCK // SUB_ROWS + 1):
        @pl.when(nsub == m)
        def _(m=m):
            compute(m * SUB_ROWS)

    @pl.when(nsub == 0)
    def _():
        y_ref[...] = jnp.zeros_like(y_ref)


def _moe(block_e, nsub, next_e, xrow, xs, w1, b1, w2, b2, *, n_blocks):
    return pl.pallas_call(
        functools.partial(_moe_kernel, n_blocks=n_blocks),
        out_shape=jax.ShapeDtypeStruct((n_blocks * ROW_BLOCK * ROW_TILES, LANES), F32),
        grid_spec=pltpu.PrefetchScalarGridSpec(
            num_scalar_prefetch=4,
            grid=(n_blocks,),
            in_specs=[pl.BlockSpec(memory_space=pl.ANY),
                      pl.BlockSpec(memory_space=pl.ANY),
                      pl.BlockSpec((None, 1, 2 * D_FF), lambda i, be, ns, nx, xr: (be[i], 0, 0)),
                      pl.BlockSpec(memory_space=pl.ANY),
                      pl.BlockSpec((None, 1, D_MODEL), lambda i, be, ns, nx, xr: (be[i], 0, 0))],
            out_specs=pl.BlockSpec((ROW_BLOCK * ROW_TILES, LANES),
                                   lambda i, be, ns, nx, xr: (i, 0)),
            scratch_shapes=[pltpu.VMEM((D_MODEL, 2 * D_FF), F32),
                            pltpu.VMEM((D_FF, D_MODEL), F32),
                            pltpu.VMEM((D_MODEL, 2 * D_FF), BF16),
                            pltpu.VMEM((D_FF, D_MODEL), BF16),
                            pltpu.VMEM((2, ROW_BLOCK * ROW_TILES, LANES), F32),
                            pltpu.VMEM((ROW_BLOCK, D_MODEL), BF16),
                            pltpu.VMEM((ROW_BLOCK, D_FF), BF16),
                            pltpu.SemaphoreType.DMA((2,)),
                            pltpu.SemaphoreType.DMA((2,))]),
        compiler_params=_cparams(("arbitrary",)),
        name="moe_ffn",
    )(block_e, nsub, next_e, xrow, xs, w1, b1, w2, b2)


def _combine_kernel(pos_hbm, src_hbm, yb_hbm, gate_ref, h1_ref, g2_ref, nw_ref, o_ref,
                    pos_smem, src_smem, win, hbuf, sem_tab, sem_win, *, n_tiles):
    i = pl.program_id(0)
    slot = lax.rem(i, 2)
    tile = COMBINE_TILE
    n_asg = tile * TOP_K
    chunk = WIN_ROWS * ROW_TILES
    slot_rows = WIN_CHUNKS * chunk

    def table_copies(j, s):
        return (pltpu.make_async_copy(pos_hbm.at[j], pos_smem.at[pl.ds(s * n_asg, n_asg)],
                                      sem_tab.at[s]),
                pltpu.make_async_copy(src_hbm.at[j], src_smem.at[pl.ds(s * n_asg, n_asg)],
                                      sem_tab.at[s]))

    def fetch_window(s):
        def one(c, carry):
            src = pl.multiple_of(src_smem[s * n_asg + c], ROW_TILES)
            dst = pl.multiple_of(s * slot_rows + c * chunk, chunk)
            pltpu.make_async_copy(yb_hbm.at[pl.ds(src, chunk), :], win.at[pl.ds(dst, chunk), :],
                                  sem_win.at[s]).start()
            return carry
        lax.fori_loop(0, WIN_CHUNKS, one, 0, unroll=2)

    @pl.when(i == 0)
    def _():
        for cp in table_copies(0, 0):
            cp.start()
        for cp in table_copies(0, 0):
            cp.wait()
        fetch_window(0)
        if n_tiles > 1:
            for cp in table_copies(1, 1):
                cp.start()

    @pl.when(i + 1 < n_tiles)
    def _():
        for cp in table_copies(i + 1, 1 - slot):
            cp.wait()
        fetch_window(1 - slot)

    pltpu.make_async_copy(yb_hbm.at[pl.ds(0, slot_rows), :],
                          win.at[pl.ds(pl.multiple_of(slot * slot_rows, slot_rows), slot_rows), :],
                          sem_win.at[slot]).wait()

    g2 = g2_ref[...]
    base = slot * n_asg

    def token(t, carry):
        e0 = base + t * TOP_K
        acc = None
        for k in range(TOP_K):
            p = pl.multiple_of(pos_smem[e0 + k], ROW_TILES)
            term = gate_ref[pl.ds(t * TOP_K + k, 1), :] * win[pl.ds(p, ROW_TILES), :]
            acc = term if acc is None else acc + term
        r0 = pl.multiple_of(t * ROW_TILES, ROW_TILES)
        hbuf[pl.ds(r0, ROW_TILES), :] = h1_ref[pl.ds(r0, ROW_TILES), :] + g2 * acc
        return carry
    lax.fori_loop(0, tile, token, 0, unroll=8)

    @pl.when(i + 2 < n_tiles)
    def _():
        for cp in table_copies(i + 2, slot):
            cp.start()

    h = _load_tile_rows(hbuf, tile)
    ms = jnp.mean(h * h, axis=-1, keepdims=True)
    o_ref[...] = h * lax.rsqrt(ms + EPS) * nw_ref[...]


def _combine(pos_tiles, src_tiles, yb, gate_rows, h1, g2_tiles, nw, *, tiles_per_batch):
    tile = COMBINE_TILE
    T = h1.shape[0] // ROW_TILES
    n_asg = tile * TOP_K
    return pl.pallas_call(
        functools.partial(_combine_kernel, n_tiles=T // tile),
        out_shape=jax.ShapeDtypeStruct((T, D_MODEL), F32),
        grid=(T // tile,),
        in_specs=[pl.BlockSpec(memory_space=pl.ANY),
                  pl.BlockSpec(memory_space=pl.ANY),
                  pl.BlockSpec(memory_space=pl.ANY),
                  pl.BlockSpec((tile * TOP_K, LANES), lambda i: (i, 0)),
                  pl.BlockSpec((tile * ROW_TILES, LANES), lambda i: (i, 0)),
                  pl.BlockSpec((None, ROW_TILES, LANES), lambda i: (i // tiles_per_batch, 0, 0)),
                  pl.BlockSpec((1, D_MODEL), lambda i: (0, 0))],
        out_specs=pl.BlockSpec((tile, D_MODEL), lambda i: (i, 0)),
        scratch_shapes=[pltpu.SMEM((2 * n_asg,), I32),
                        pltpu.SMEM((2 * n_asg,), I32),
                        pltpu.VMEM((2 * WIN_CHUNKS * WIN_ROWS * ROW_TILES, LANES), F32),
                        pltpu.VMEM((tile * ROW_TILES, LANES), F32),
                        pltpu.SemaphoreType.DMA((2,)),
                        pltpu.SemaphoreType.DMA((2,))],
        compiler_params=_cparams(("arbitrary",)),
        name="combine_norm",
    )(pos_tiles, src_tiles, yb, gate_rows, h1, g2_tiles, nw)


def _rope_tables(n):
    rows = n // GRID_W
    row = jnp.repeat(jnp.arange(rows, dtype=F32), GRID_W)
    col = jnp.tile(jnp.arange(GRID_W, dtype=F32), rows)
    n_freq = RET_DK // 4
    inv = ROPE_BASE ** (-jnp.arange(n_freq, dtype=F32) / n_freq)
    ang = jnp.concatenate([row[:, None] * inv, col[:, None] * inv], axis=-1)
    cos, sin = jnp.cos(ang), jnp.sin(ang)
    cos_h = jnp.concatenate([cos, cos], axis=-1)
    sin_h = jnp.concatenate([-sin, sin], axis=-1)
    return jnp.tile(cos_h, (1, RET_HEADS)), jnp.tile(sin_h, (1, RET_HEADS))


def kernel(x, c, ctx, c_ctx, w_ada, b_ada, norm_mix_w, norm_ffn_w, w_in, w_out, hg_lb,
           hg_norm_w, router_w, router_b, w1, b1, w2, b2, norm_final_w):
    B, N, D = x.shape
    C = ctx.shape[1]
    T = B * N
    assert D == D_MODEL and w_ada.shape[0] == 1

    cc = jnp.concatenate([c.astype(F32), c_ctx.astype(F32)[None, :],
                          jnp.zeros((16 - B - 1, D), F32)], axis=0)
    mod = _ada(cc, w_ada[0], b_ada[0][None, :])
    mod = mod.reshape(16, 6, 1, D).transpose(1, 0, 2, 3)
    sh1, sc1, g1, sh2, sc2, g2 = (mod[i] for i in range(6))

    w_in_bf = w_in[0].astype(BF16)
    w_out_bf = w_out[0].astype(BF16)
    nw_mix = norm_mix_w[0][None, :]
    cos_t, sin_t = _rope_tables(N)

    pb, lf = _inproj(x, sc1, sh1, None, nw_mix, w_in_bf, hg_lb[:2], cos_t, sin_t,
                     rope=True, tm=INPROJ_TILE)
    pbc, lfc = _inproj(ctx, sc1, sh1, B, nw_mix, w_in_bf, hg_lb[:2], cos_t[:C], sin_t[:C],
                       rope=False, tm=C)

    a_ret = _retention(pb, pbc)
    a_hg = _hgrn(pb, lf, pbc, lfc, hg_norm_w[0][None, :])

    rw_t = router_w[0].T
    rwh = rw_t.astype(BF16)
    rwl = (rw_t - rwh.astype(F32)).astype(BF16)
    rb = jnp.broadcast_to(router_b[0][:, None], (N_EXPERTS, LANES))
    h1, v, route_t, gate_rows, base_cnt, cnt = _outproj(a_ret, a_hg, w_out_bf, x, g1, sc2, sh2,
                                                        norm_ffn_w[0][None, :], rwh, rwl, rb,
                                                        tm=COMBINE_TILE)

    idx = route_t[0:TOP_K].astype(I32)
    rank = route_t[2 * TOP_K:3 * TOP_K].astype(I32)
    counts = cnt[:, 0].astype(I32)
    padded = (counts + ROW_BLOCK - 1) // ROW_BLOCK * ROW_BLOCK
    pend = jnp.cumsum(padded)
    pstart = pend - padded
    cstart = jnp.cumsum(counts) - counts
    onehot = idx[None] == jnp.arange(N_EXPERTS, dtype=I32)[:, None, None]

    def dest_tiles(first_row, tile):
        dest = (jnp.sum(jnp.where(onehot, first_row[:, None, None], 0), axis=0) + rank) * ROW_TILES
        return dest.reshape(TOP_K, T // tile, tile).transpose(1, 0, 2).reshape(T // tile,
                                                                               TOP_K * tile)
    n_blocks = (T * TOP_K) // ROW_BLOCK + N_EXPERTS
    starts = jnp.arange(n_blocks, dtype=I32) * ROW_BLOCK
    block_e = jnp.minimum(jnp.sum((pend[None, :] <= starts[:, None]).astype(I32), axis=1),
                          N_EXPERTS - 1)
    valid = jnp.clip(counts[block_e] - (starts - pstart[block_e]), 0, ROW_BLOCK)
    nsub = (valid + SUB_ROWS - 1) // SUB_ROWS
    first_from = lax.cummin(jnp.where(counts > 0, jnp.arange(N_EXPERTS, dtype=I32), N_EXPERTS),
                            reverse=True)
    next_e = jnp.concatenate([first_from[1:], jnp.full((1,), N_EXPERTS, I32)])
    next_e = jnp.where(next_e == N_EXPERTS, -1, next_e)
    xrow = jnp.where(nsub > 0, cstart[block_e] + starts - pstart[block_e], 0) * ROW_TILES

    xs = _dispatch(dest_tiles(cstart, DISPATCH_TILE), v, tile=DISPATCH_TILE)
    yb = _moe(block_e, nsub, next_e, xrow.astype(I32), xs, w1[0], b1[0][:, None, :], w2[0],
              b2[0][:, None, :], n_blocks=n_blocks)

    tc = COMBINE_TILE
    n_ct = T // tc
    base = base_cnt.reshape(n_ct, N_EXPERTS, LANES)[:, :, 0].astype(I32)
    n_run = jnp.concatenate([base[1:], counts[None, :]], axis=0) - base
    n_chunk = (n_run + WIN_ROWS - 1) // WIN_ROWS
    chunk_end = jnp.cumsum(n_chunk, axis=1)
    chunk_start = chunk_end - n_chunk
    win_off = chunk_start * WIN_ROWS
    c_id = jnp.arange(WIN_CHUNKS, dtype=I32)
    c_exp = jnp.sum((chunk_end[:, None, :] <= c_id[None, :, None]).astype(I32), axis=2)
    c_real = c_exp < N_EXPERTS
    c_own = c_exp[:, :, None] == jnp.arange(N_EXPERTS, dtype=I32)
    take = lambda tab: jnp.sum(jnp.where(c_own, tab[:, None, :], 0), axis=2)
    c_src = (take(pstart[None, :] + base) + (c_id[None, :] - take(chunk_start)) * WIN_ROWS)
    src_tiles = jnp.concatenate(
        [jnp.where(c_real, c_src, 0) * ROW_TILES,
         jnp.zeros((n_ct, tc * TOP_K - WIN_CHUNKS), I32)], axis=1)
    slot_off = (jnp.arange(n_ct, dtype=I32) % 2) * (WIN_CHUNKS * WIN_ROWS)
    shift = jnp.repeat(win_off - base + slot_off[:, None], tc, axis=0).T
    pos = (jnp.sum(jnp.where(onehot, shift[:, None, :], 0), axis=0) + rank) * ROW_TILES

    def per_tile(a):
        return a.reshape(TOP_K, n_ct, tc).transpose(1, 2, 0).reshape(n_ct, tc * TOP_K)

    out = _combine(per_tile(pos), src_tiles, yb, gate_rows, h1,
                   g2.reshape(16, ROW_TILES, LANES), norm_final_w[None, :],
                   tiles_per_batch=N // tc)
    return out.reshape(B, N, D)
```

```python
import functools
import math

import jax
import jax.numpy as jnp
from jax import lax
from jax.experimental import pallas as pl
from jax.experimental.pallas import tpu as pltpu

F32 = jnp.float32
BF16 = jnp.bfloat16
I32 = jnp.int32

D_MODEL = 1024
GRID_W = 64
RET_HEADS = 4
RET_DK = 64
HG_HEADS = 4
PROJ_W = 4096
ROPE_BASE = 10000.0
EPS = 1e-6
N_EXPERTS = 32
TOP_K = 4
D_FF = 1024
SWIGLU_LIMIT = 7.0
SWIGLU_ALPHA = 1.702
LOG2_E = 1.0 / math.log(2.0)

LANES = 128
CHUNK = 128
INPROJ_TILE = 1024
DISPATCH_TILE = 512
ROW_BLOCK = 1024
SUB_ROWS = 256
COMBINE_TILE = 512
WIN_ROWS = 8
WIN_CHUNKS = (COMBINE_TILE * TOP_K + N_EXPERTS * (WIN_ROWS - 1) + WIN_ROWS - 1) // WIN_ROWS
VMEM_LIMIT = 56 * 1024 * 1024
ROW_TILES = D_MODEL // LANES

C_RQ, C_RK, C_RV, C_RG, C_HQ, C_FF, C_FB, C_HV, C_HG = (
    0, 256, 512, 1024, 1536, 2048, 2560, 3072, 3584)


def _cparams(sem):
    return pltpu.CompilerParams(dimension_semantics=sem, vmem_limit_bytes=VMEM_LIMIT)


def _split_bf16(x):
    hi = x.astype(BF16)
    lo = (x - hi.astype(F32)).astype(BF16)
    return hi, lo


def _dot(a, b):
    return jnp.dot(a, b, preferred_element_type=F32)


def _dot_nt(a, b):
    return lax.dot_general(a, b, (((1,), (1,)), ((), ())), preferred_element_type=F32)


def _dot3(a, b):
    ah, al = _split_bf16(a)
    bh, bl = _split_bf16(b)
    return _dot(ah, bh) + (_dot(ah, bl) + _dot(al, bh))


def _silu(x):
    return x * jax.nn.sigmoid(x)


def _load_tile_rows(ref, n):
    return jnp.concatenate([ref[pl.ds(s, n, stride=ROW_TILES), :] for s in range(ROW_TILES)],
                           axis=1)


def _store_tile_rows(ref, x):
    n = x.shape[0]
    for s in range(ROW_TILES):
        ref[pl.ds(s, n, stride=ROW_TILES), :] = x[:, s * LANES:(s + 1) * LANES]


def _ada_kernel(c_ref, w_ref, b_ref, o_ref):
    s = _silu(c_ref[...])
    o_ref[...] = _dot3(s, w_ref[...]) + b_ref[...]


def _ada(cc, w, b):
    nblk = w.shape[1] // D_MODEL
    return pl.pallas_call(
        _ada_kernel,
        out_shape=jax.ShapeDtypeStruct((cc.shape[0], w.shape[1]), F32),
        grid=(nblk,),
        in_specs=[pl.BlockSpec(cc.shape, lambda j: (0, 0)),
                  pl.BlockSpec((D_MODEL, D_MODEL), lambda j: (0, j)),
                  pl.BlockSpec((1, D_MODEL), lambda j: (0, j))],
        out_specs=pl.BlockSpec((cc.shape[0], D_MODEL), lambda j: (0, j)),
        compiler_params=_cparams(("arbitrary",)),
        name="ada",
    )(cc, w, b)


def _inproj_kernel(x_ref, sc_ref, sh_ref, nw_ref, w_ref, lb_ref, cos_ref, sin_ref,
                   pb_ref, lf_ref, u_scr, *, rope):
    x = x_ref[...]
    ms = jnp.mean(x * x, axis=-1, keepdims=True)
    u = x * lax.rsqrt(ms + EPS) * nw_ref[...] * (1.0 + sc_ref[...]) + sh_ref[...]
    u_scr[...] = u.astype(BF16)

    def proj(lo, width):
        return _dot(u_scr[...], w_ref[:, lo:lo + width])

    tm = x.shape[0]
    if rope:
        lane = lax.broadcasted_iota(I32, (tm, LANES), 1)
        first = (lane & 32) == 0

    def put_rot(col, scale):
        for j in range(2):
            lo = col + j * LANES
            t = proj(lo, LANES)
            if scale != 1.0:
                t = t * scale
            if rope:
                tb = j * LANES
                rot = jnp.where(first, pltpu.roll(t, 96, axis=1), pltpu.roll(t, 32, axis=1))
                t = t * cos_ref[:, tb:tb + LANES] + rot * sin_ref[:, tb:tb + LANES]
            pb_ref[:, lo:lo + LANES] = t.astype(BF16)

    put_rot(C_RQ, 1.0)
    put_rot(C_RK, RET_DK ** -0.5)
    pb_ref[:, C_RV:C_RV + 512] = proj(C_RV, 512).astype(BF16)
    pb_ref[:, C_RG:C_RG + 512] = _silu(proj(C_RG, 512)).astype(BF16)
    pb_ref[:, C_HQ:C_HQ + 512] = _silu(proj(C_HQ, 512)).astype(BF16)
    pb_ref[:, C_HV:C_HV + 512] = proj(C_HV, 512).astype(BF16)
    pb_ref[:, C_HG:C_HG + 512] = _silu(proj(C_HG, 512)).astype(BF16)

    la = lb_ref[0]
    lbb = lb_ref[1]
    mx = jnp.maximum(la, lbb)
    ea = jnp.exp(la - mx)
    eb = jnp.exp(lbb - mx)
    lb = ea / (ea + eb)
    for d, col in enumerate((C_FF, C_FB)):
        lbd = lb[d:d + 1, :]
        f = lbd + (1.0 - lbd) * jax.nn.sigmoid(proj(col, 512))
        pb_ref[:, col:col + 512] = (1.0 - f).astype(BF16)
        lf_ref[:, d * 512:(d + 1) * 512] = jnp.log(f) * LOG2_E


def _inproj(x, sc, sh, mod_row, nw, w_bf, hg_lb, cos_t, sin_t, *, rope, tm):
    B, n, _ = x.shape
    nt = n // tm
    if mod_row is None:
        mrow = lambda b, j: (b, 0, 0)
    else:
        mrow = lambda b, j: (mod_row, 0, 0)
    return pl.pallas_call(
        functools.partial(_inproj_kernel, rope=rope),
        out_shape=(jax.ShapeDtypeStruct((B, n, PROJ_W), BF16),
                   jax.ShapeDtypeStruct((B, n, 1024), F32)),
        grid=(B, nt),
        in_specs=[pl.BlockSpec((None, tm, D_MODEL), lambda b, j: (b, j, 0)),
                  pl.BlockSpec((None, 1, D_MODEL), mrow),
                  pl.BlockSpec((None, 1, D_MODEL), mrow),
                  pl.BlockSpec((1, D_MODEL), lambda b, j: (0, 0)),
                  pl.BlockSpec((D_MODEL, PROJ_W), lambda b, j: (0, 0),
                               pipeline_mode=pl.Buffered(1)),
                  pl.BlockSpec((2, 2, 512), lambda b, j: (0, 0, 0)),
                  pl.BlockSpec((tm, 256), lambda b, j: (j, 0)),
                  pl.BlockSpec((tm, 256), lambda b, j: (j, 0))],
        out_specs=(pl.BlockSpec((None, tm, PROJ_W), lambda b, j: (b, j, 0)),
                   pl.BlockSpec((None, tm, 1024), lambda b, j: (b, j, 0))),
        scratch_shapes=[pltpu.VMEM((tm, D_MODEL), BF16)],
        compiler_params=_cparams(("arbitrary", "arbitrary")),
        name="inproj_rope" if rope else "inproj_ctx",
    )(x, sc, sh, nw, w_bf, hg_lb, cos_t, sin_t)


_RET_LGF = [math.log1p(-(2.0 ** (-5.0 - 2.0 * h))) for h in range(RET_HEADS)]
_RET_LGB = [math.log1p(-(2.0 ** (-6.0 - 2.0 * h))) for h in range(RET_HEADS)]


def _ret_kernel(q_ref, k_ref, v_ref, g_ref, kc_ref, vc_ref, o_ref,
                u_scr, sin_scr, kt_scr, dtot_scr, *, n_lat, n_ctx):
    L = CHUNK
    pair = pl.program_id(1)
    row = lax.broadcasted_iota(I32, (L, L), 0).astype(F32)
    col = lax.broadcasted_iota(I32, (L, L), 1).astype(F32)
    lane = lax.broadcasted_iota(I32, (L, LANES), 1)
    trow = lax.broadcasted_iota(I32, (L, 1), 0).astype(F32)
    tcol = lax.broadcasted_iota(I32, (1, L), 1).astype(F32)
    low_half = lane < RET_DK

    def u_chunk(k_blk, v_blk, ci, store_kt):
        kt = k_blk.astype(F32).T
        if store_kt is not None:
            kt_scr[store_kt] = kt.astype(BF16)
        for hh in range(2):
            lgf = jnp.where(pair == 0, _RET_LGF[hh], _RET_LGF[2 + hh])
            lgb = jnp.where(pair == 0, _RET_LGB[hh], _RET_LGB[2 + hh])
            kth = kt[hh * RET_DK:(hh + 1) * RET_DK, :]
            wkf = jnp.exp(lgf * (L - 1.0 - tcol))
            wkb = jnp.exp(lgb * tcol)
            lhs = jnp.concatenate([kth * wkf, kth * wkb], axis=0).astype(BF16)
            u_scr[hh, ci] = _dot(lhs, v_blk[:, hh * LANES:(hh + 1) * LANES])

    for c in range(n_ctx):
        u_chunk(kc_ref[c * L:(c + 1) * L, :], vc_ref[c * L:(c + 1) * L, :], c, None)

    def lat_u(c, carry):
        r0 = pl.multiple_of(c * L, L)
        u_chunk(k_ref[pl.ds(r0, L), :], v_ref[pl.ds(r0, L), :], n_ctx + c, c)
        return carry
    lax.fori_loop(0, n_lat, lat_u, 0, unroll=8)

    ones = jnp.ones((RET_DK, LANES), F32)
    for hh in range(2):
        lgf = jnp.where(pair == 0, _RET_LGF[hh], _RET_LGF[2 + hh])
        lgb = jnp.where(pair == 0, _RET_LGB[hh], _RET_LGB[2 + hh])
        d = row - col
        dtot_scr[hh] = jnp.where(d > 0, jnp.exp(lgf * jnp.maximum(d, 0.0)),
                                 jnp.where(d < 0, jnp.exp(lgb * jnp.maximum(-d, 0.0)), 2.0))
        af = jnp.exp(ones * (lgf * L))
        ab = jnp.exp(ones * (lgb * L))

        s = jnp.zeros((RET_DK, LANES), F32)
        for c in range(n_ctx):
            s = af * s + u_scr[hh, c, 0:RET_DK, :]
        sb = jnp.zeros((RET_DK, LANES), F32)
        for c in reversed(range(n_ctx)):
            sb = ab * sb + u_scr[hh, c, RET_DK:2 * RET_DK, :]

        def fwd(c, s, hh=hh, af=af):
            sin_scr[hh, c, 0:RET_DK, :] = s.astype(BF16)
            return af * s + u_scr[hh, n_ctx + c, 0:RET_DK, :]
        lax.fori_loop(0, n_lat, fwd, s)

        def bwd(i, sb, hh=hh, ab=ab):
            c = n_lat - 1 - i
            sin_scr[hh, c, RET_DK:2 * RET_DK, :] = sb.astype(BF16)
            return ab * sb + u_scr[hh, n_ctx + c, RET_DK:2 * RET_DK, :]
        lax.fori_loop(0, n_lat, bwd, sb)

    def out_chunk(c, carry):
        r0 = pl.multiple_of(c * L, L)
        q = q_ref[pl.ds(r0, L), :].astype(F32)
        qr = pltpu.roll(q, RET_DK, axis=1)
        kt = kt_scr[c]
        for hh in range(2):
            lgf = jnp.where(pair == 0, _RET_LGF[hh], _RET_LGF[2 + hh])
            lgb = jnp.where(pair == 0, _RET_LGB[hh], _RET_LGB[2 + hh])
            mine = low_half if hh == 0 else jnp.logical_not(low_half)
            qm = jnp.where(mine, q, 0.0).astype(BF16)
            p = (_dot(qm, kt) * dtot_scr[hh]).astype(BF16)
            vh = v_ref[pl.ds(r0, L), hh * LANES:(hh + 1) * LANES]
            wqf = jnp.exp(lgf * (trow + 1.0))
            wqb = jnp.exp(lgb * (L - trow))
            qa, qb = (q, qr) if hh == 0 else (qr, q)
            qs = jnp.where(low_half, qa * wqf, qb * wqb).astype(BF16)
            o = _dot(p, vh) + _dot(qs, sin_scr[hh, c])
            ms = jnp.mean(o * o, axis=-1, keepdims=True)
            gh = g_ref[pl.ds(r0, L), hh * LANES:(hh + 1) * LANES].astype(F32)
            o_ref[pl.ds(r0, L), hh * LANES:(hh + 1) * LANES] = (
                o * lax.rsqrt(ms + EPS) * gh).astype(BF16)
        return carry
    lax.fori_loop(0, n_lat, out_chunk, 0, unroll=8)


def _retention(pb, pbc):
    B, n, _ = pb.shape
    nc = pbc.shape[1]
    n_lat, n_ctx = n // CHUNK, nc // CHUNK
    return pl.pallas_call(
        functools.partial(_ret_kernel, n_lat=n_lat, n_ctx=n_ctx),
        out_shape=jax.ShapeDtypeStruct((B, n, 512), BF16),
        grid=(B, 2),
        in_specs=[pl.BlockSpec((None, n, LANES), lambda b, p: (b, 0, C_RQ // LANES + p)),
                  pl.BlockSpec((None, n, LANES), lambda b, p: (b, 0, C_RK // LANES + p)),
                  pl.BlockSpec((None, n, 256), lambda b, p: (b, 0, C_RV // 256 + p)),
                  pl.BlockSpec((None, n, 256), lambda b, p: (b, 0, C_RG // 256 + p)),
                  pl.BlockSpec((None, nc, LANES), lambda b, p: (b, 0, C_RK // LANES + p)),
                  pl.BlockSpec((None, nc, 256), lambda b, p: (b, 0, C_RV // 256 + p))],
        out_specs=pl.BlockSpec((None, n, 256), lambda b, p: (b, 0, p)),
        scratch_shapes=[pltpu.VMEM((2, n_lat + n_ctx, CHUNK, LANES), F32),
                        pltpu.VMEM((2, n_lat, CHUNK, LANES), BF16),
                        pltpu.VMEM((n_lat, LANES, CHUNK), BF16),
                        pltpu.VMEM((2, CHUNK, CHUNK), F32)],
        compiler_params=_cparams(("arbitrary", "arbitrary")),
        name="retention",
    )(pb, pb, pb, pb, pbc, pbc)


_LEVELS = (64, 32, 16, 8, 4, 2, 1)


def _expand_rows(r, rep):
    n = r.shape[0]
    if n == 1:
        return jnp.broadcast_to(r, (rep, r.shape[1]))
    return jnp.concatenate(
        [jnp.broadcast_to(r[i:i + 1, :], (rep, r.shape[1])) for i in range(n)], axis=0)


def _hgrn_kernel(q_ref, kf_ref, kb_ref, v_ref, g_ref, lff_ref, lfb_ref,
                 kfc_ref, kbc_ref, vc_ref, lffc_ref, lfbc_ref, nw_ref, o_ref,
                 ut_scr, a_scr, qs_scr, oi_scr, sin_scr, bfb_scr, *, n_lat, n_ctx):
    L = CHUNK
    row = lax.broadcasted_iota(I32, (L, L), 0)
    col = lax.broadcasted_iota(I32, (L, L), 1)
    xr_bits = lax.bitcast_convert_type((row ^ col).astype(F32), I32)
    lv = lax.shift_right_logical(xr_bits, 23) - 127
    row2 = lax.broadcasted_iota(I32, (L, 2 * L), 0)
    col2 = lax.broadcasted_iota(I32, (L, 2 * L), 1) & (L - 1)
    tril2 = jnp.where(col2 <= row2, 1.0, 0.0).astype(BF16)
    triu2 = jnp.where(col2 >= row2, 1.0, 0.0).astype(BF16)

    def cums(lff, lfb):
        hf, lof = _split_bf16(lff)
        hb, lob = _split_bf16(lfb)
        bf = _dot(tril2, jnp.concatenate([hf, lof], axis=0))
        bb = _dot(triu2, jnp.concatenate([hb, lob], axis=0))
        return bf, bb

    def state_part(ci, kf, kb, v_blk, bf, bb):
        endf = bf[L - 1:L, :]
        endb = bb[0:1, :]
        ksf = kf * jnp.exp2(endf - bf)
        ksb = kb * jnp.exp2(endb - bb)
        vt = v_blk.astype(F32).T.astype(BF16)
        ut_scr[ci] = _dot(vt, jnp.concatenate([ksf, ksb], axis=1).astype(BF16))
        a_scr[ci] = jnp.broadcast_to(
            jnp.concatenate([jnp.exp2(endf), jnp.exp2(endb)], axis=1), (8, 2 * LANES))

    for c in range(n_ctx):
        sl = slice(c * L, (c + 1) * L)
        bf, bb = cums(lffc_ref[sl, :], lfbc_ref[sl, :])
        state_part(c, kfc_ref[sl, :].astype(F32), kbc_ref[sl, :].astype(F32),
                   vc_ref[sl, :], bf, bb)

    def lat_chunk(c, u):
        rows = pl.ds(pl.multiple_of(c * L, L), L)
        bf_scr = bfb_scr.at[u, 0]
        bb_scr = bfb_scr.at[u, 1]
        q = q_ref[rows, :].astype(F32)
        kf = kf_ref[rows, :].astype(F32)
        kb = kb_ref[rows, :].astype(F32)
        v_blk = v_ref[rows, :]
        lff = lff_ref[rows, :]
        lfb = lfb_ref[rows, :]
        bf, bb = cums(lff, lfb)
        state_part(n_ctx + c, kf, kb, v_blk, bf, bb)
        qs_scr[rows, :] = jnp.concatenate([q * jnp.exp2(bf), q * jnp.exp2(bb)],
                                          axis=1).astype(BF16)
        bf_scr[...] = bf
        bb_scr[...] = bb

        acc = jnp.zeros((L, L), F32)
        for lvl, h in enumerate(_LEVELS):
            bit = (row & h) != 0
            ksel = jnp.where(bit, kb, kf)
            if h >= 4:
                n = (L // 2) // h
                if n == 1:
                    rf = bf_scr[h - 1:h, :]
                    rb = bb_scr[h:h + 1, :]
                else:
                    rf = bf_scr[pl.ds(h - 1, n, stride=2 * h), :]
                    rb = bb_scr[pl.ds(h, n, stride=2 * h), :]
                df = bf - _expand_rows(rf, 2 * h)
                db = bb - _expand_rows(rb, 2 * h)
                eq = jnp.where(bit, df, db)
                ek = -jnp.where(bit, db, df)
            elif h == 2:
                m = row & 3
                lff_n = pltpu.roll(lff, L - 1, axis=0)
                lfb_n = pltpu.roll(lfb, L - 1, axis=0)
                eq = jnp.where(m == 2, lff,
                               jnp.where(m == 3, lff + pltpu.roll(lff, 1, axis=0),
                                         jnp.where(m == 0, lfb + lfb_n, lfb)))
                ek = jnp.where(m == 3, pltpu.roll(lfb, 1, axis=0),
                               jnp.where(m == 0, lff_n, 0.0))
            else:
                eq = jnp.where(bit, lff, lfb)
                ek = None
            lhs = (q * jnp.exp2(eq)).astype(BF16)
            rhs = (ksel if ek is None else ksel * jnp.exp2(ek)).astype(BF16)
            acc = jnp.where(lv == 6 - lvl, _dot_nt(lhs, rhs), acc)

        dsum = jnp.sum(q * (kf + kb), axis=-1, keepdims=True)
        oi_scr[rows, :] = _dot(acc.astype(BF16), v_blk) + dsum * v_blk.astype(F32)

    def lat_pair(i, carry):
        for u in range(8):
            lat_chunk(8 * i + u, u)
        return carry
    lax.fori_loop(0, n_lat // 8, lat_pair, 0)

    st = jnp.zeros((LANES, LANES), F32)
    for c in range(n_ctx):
        st = st * a_scr[c, 0:1, 0:LANES] + ut_scr[c, :, 0:LANES]
    stb = jnp.zeros((LANES, LANES), F32)
    for c in reversed(range(n_ctx)):
        stb = stb * a_scr[c, 0:1, LANES:2 * LANES] + ut_scr[c, :, LANES:2 * LANES]

    def fwd(c, st):
        sin_scr[c, :, 0:LANES] = st.astype(BF16)
        ci = n_ctx + c
        return st * a_scr[ci, 0:1, 0:LANES] + ut_scr[ci, :, 0:LANES]
    lax.fori_loop(0, n_lat, fwd, st)

    def bwd(i, stb):
        c = n_lat - 1 - i
        sin_scr[c, :, LANES:2 * LANES] = stb.astype(BF16)
        ci = n_ctx + c
        return stb * a_scr[ci, 0:1, LANES:2 * LANES] + ut_scr[ci, :, LANES:2 * LANES]
    lax.fori_loop(0, n_lat, bwd, stb)

    def out_chunk(c, carry):
        rows = pl.ds(pl.multiple_of(c * L, L), L)
        o = oi_scr[rows, :] + _dot_nt(qs_scr[rows, :], sin_scr[c])
        ms = jnp.mean(o * o, axis=-1, keepdims=True)
        y = o * lax.rsqrt(ms + EPS) * nw_ref[...] * g_ref[rows, :].astype(F32)
        o_ref[rows, :] = y.astype(BF16)
        return carry
    lax.fori_loop(0, n_lat, out_chunk, 0, unroll=8)


def _hgrn(pb, lf, pbc, lfc, nw):
    B, n, _ = pb.shape
    nc = pbc.shape[1]
    n_lat, n_ctx = n // CHUNK, nc // CHUNK

    def colblk(rows, col0):
        return pl.BlockSpec((None, rows, LANES), lambda b, h: (b, 0, col0 // LANES + h))

    return pl.pallas_call(
        functools.partial(_hgrn_kernel, n_lat=n_lat, n_ctx=n_ctx),
        out_shape=jax.ShapeDtypeStruct((B, n, 512), BF16),
        grid=(B, HG_HEADS),
        in_specs=[colblk(n, C_HQ), colblk(n, C_FF), colblk(n, C_FB), colblk(n, C_HV),
                  colblk(n, C_HG), colblk(n, 0), colblk(n, 512),
                  colblk(nc, C_FF), colblk(nc, C_FB), colblk(nc, C_HV),
                  colblk(nc, 0), colblk(nc, 512),
                  pl.BlockSpec((1, LANES), lambda b, h: (0, 0))],
        out_specs=pl.BlockSpec((None, n, LANES), lambda b, h: (b, 0, h)),
        scratch_shapes=[pltpu.VMEM((n_lat + n_ctx, LANES, 2 * LANES), F32),
                        pltpu.VMEM((n_lat + n_ctx, 8, 2 * LANES), F32),
                        pltpu.VMEM((n, 2 * LANES), BF16),
                        pltpu.VMEM((n, LANES), F32),
                        pltpu.VMEM((n_lat, LANES, 2 * LANES), BF16),
                        pltpu.VMEM((8, 2, CHUNK, LANES), F32)],
        compiler_params=_cparams(("arbitrary", "arbitrary")),
        name="hgrn2",
    )(pb, pb, pb, pb, pb, lf, lf, pbc, pbc, pbc, lfc, lfc, nw)


def _outproj_kernel(ar_ref, ah_ref, w_ref, x_ref, g1_ref, sc_ref, sh_ref, nw_ref,
                    rwh_ref, rwl_ref, rb_ref, tri_ref, h1_ref, v_ref, route_t_ref, gate_ref, base_ref,
                    cnt_ref, cnt_scr):
    first_step = jnp.logical_and(pl.program_id(0) == 0, pl.program_id(1) == 0)

    @pl.when(first_step)
    def _():
        cnt_scr[...] = jnp.zeros_like(cnt_scr)

    y = _dot(ar_ref[...], w_ref[0:512, :]) + _dot(ah_ref[...], w_ref[512:1024, :])
    h1 = x_ref[...] + g1_ref[...] * y
    _store_tile_rows(h1_ref, h1)
    ms = jnp.mean(h1 * h1, axis=-1, keepdims=True)
    v = h1 * lax.rsqrt(ms + EPS) * nw_ref[...] * (1.0 + sc_ref[...]) + sh_ref[...]
    _store_tile_rows(v_ref, v)

    tm = v.shape[0]
    vh, vl = _split_bf16(v)
    rwh, rwl = rwh_ref[...], rwl_ref[...]
    l = (_dot_nt(rwh, vh) + (_dot_nt(rwh, vl) + _dot_nt(rwl, vh))) + rb_ref[:, 0:1]
    row_f = lax.broadcasted_iota(I32, (N_EXPERTS, tm), 0).astype(F32)
    sels, tops, idxs = [], [], []
    for _ in range(TOP_K):
        m = jnp.max(l, axis=0, keepdims=True)
        i = jnp.min(jnp.where(l == m, row_f, float(N_EXPERTS)), axis=0, keepdims=True)
        sel = row_f == i
        l = jnp.where(sel, -jnp.inf, l)
        sels.append(sel)
        tops.append(m)
        idxs.append(i)
    es = [jnp.exp(t - tops[0]) for t in tops]
    den = es[0] + es[1] + es[2] + es[3]
    gates = [e / den for e in es]

    oh = jnp.zeros((N_EXPERTS, tm), F32)
    for sel in sels:
        oh = jnp.where(sel, 1.0, oh)
    cnt = cnt_scr[:, 0:1]
    before = _dot(oh.astype(BF16), tri_ref[...]) + cnt
    ranks = [jnp.sum(jnp.where(sel, before, 0.0), axis=0, keepdims=True) for sel in sels]
    base_ref[...] = cnt_scr[...]
    cnt_scr[...] = cnt_scr[...] + jnp.sum(oh, axis=1, keepdims=True)
    cnt_ref[...] = cnt_scr[...]

    zero4 = jnp.zeros((TOP_K, tm), F32)
    route_t_ref[...] = jnp.concatenate(idxs + [zero4] + ranks + [zero4], axis=0)
    g_cols = jnp.concatenate(gates + [zero4], axis=0).T
    for k in range(TOP_K):
        gate_ref[pl.ds(k, tm, stride=TOP_K), :] = jnp.broadcast_to(g_cols[:, k:k + 1], (tm, LANES))


def _outproj(a_ret, a_hg, w_bf, x, g1, sc2, sh2, nw, rwh, rwl, rb, *, tm):
    B, n, _ = x.shape
    nt = n // tm
    T = B * n
    assert tm == COMBINE_TILE
    mrow = lambda b, j: (b, 0, 0)
    tok = lambda b, j: (b * nt + j, 0)
    const = lambda b, j: (0, 0)
    tri = jnp.triu(jnp.ones((tm, tm), BF16), 1)
    return pl.pallas_call(
        _outproj_kernel,
        out_shape=(jax.ShapeDtypeStruct((T * ROW_TILES, LANES), F32),
                   jax.ShapeDtypeStruct((T * ROW_TILES, LANES), F32),
                   jax.ShapeDtypeStruct((16, T), F32),
                   jax.ShapeDtypeStruct((T * TOP_K, LANES), F32),
                   jax.ShapeDtypeStruct((B * nt * N_EXPERTS, LANES), F32),
                   jax.ShapeDtypeStruct((N_EXPERTS, LANES), F32)),
        grid=(B, nt),
        in_specs=[pl.BlockSpec((None, tm, 512), lambda b, j: (b, j, 0)),
                  pl.BlockSpec((None, tm, 512), lambda b, j: (b, j, 0)),
                  pl.BlockSpec((D_MODEL, D_MODEL), const),
                  pl.BlockSpec((None, tm, D_MODEL), lambda b, j: (b, j, 0)),
                  pl.BlockSpec((None, 1, D_MODEL), mrow),
                  pl.BlockSpec((None, 1, D_MODEL), mrow),
                  pl.BlockSpec((None, 1, D_MODEL), mrow),
                  pl.BlockSpec((1, D_MODEL), const),
                  pl.BlockSpec((N_EXPERTS, D_MODEL), const),
                  pl.BlockSpec((N_EXPERTS, D_MODEL), const),
                  pl.BlockSpec((N_EXPERTS, LANES), const),
                  pl.BlockSpec((tm, tm), const)],
        out_specs=(pl.BlockSpec((tm * ROW_TILES, LANES), tok),
                   pl.BlockSpec((tm * ROW_TILES, LANES), tok),
                   pl.BlockSpec((16, tm), lambda b, j: (0, b * nt + j)),
                   pl.BlockSpec((tm * TOP_K, LANES), tok),
                   pl.BlockSpec((N_EXPERTS, LANES), tok),
                   pl.BlockSpec((N_EXPERTS, LANES), const)),
        scratch_shapes=[pltpu.VMEM((N_EXPERTS, LANES), F32)],
        compiler_params=_cparams(("arbitrary", "arbitrary")),
        name="outproj_router",
    )(a_ret, a_hg, w_bf, x, g1, sc2, sh2, nw, rwh, rwl, rb, tri)


def _dispatch_kernel(dest_hbm, v_ref, xs_hbm, idx_smem, zbuf, sem_idx, sem_z, sem_rows,
                     *, tile, n_tiles, n_rows):
    i = pl.program_id(0)
    slot = lax.rem(i, 2)

    def idx_copy(j, s):
        n_asg = tile * TOP_K
        return pltpu.make_async_copy(dest_hbm.at[j], idx_smem.at[pl.ds(s * n_asg, n_asg)],
                                     sem_idx.at[s])

    def zero_copy(j):
        r0 = (n_rows + j * SUB_ROWS) * ROW_TILES
        return pltpu.make_async_copy(zbuf, xs_hbm.at[pl.ds(r0, SUB_ROWS * ROW_TILES), :], sem_z)

    @pl.when(i == 0)
    def _():
        idx_copy(0, 0).start()
        zbuf[...] = jnp.zeros_like(zbuf)
        for j in range(ROW_BLOCK // SUB_ROWS):
            zero_copy(j).start()
        for j in range(ROW_BLOCK // SUB_ROWS):
            zero_copy(j).wait()

    idx_copy(i, slot).wait()

    @pl.when(i + 1 < n_tiles)
    def _():
        idx_copy(i + 1, 1 - slot).start()

    def issue(t, carry):
        e0 = slot * (tile * TOP_K) + t
        for k in range(TOP_K):
            d = pl.multiple_of(idx_smem[e0 + k * tile], ROW_TILES)
            src = v_ref.at[pl.ds(pl.multiple_of(t * ROW_TILES, ROW_TILES), ROW_TILES), :]
            pltpu.make_async_copy(src, xs_hbm.at[pl.ds(d, ROW_TILES), :],
                                  sem_rows).start(priority=k % 2)
        return carry
    lax.fori_loop(0, tile, issue, 0, unroll=8)
    for _ in range(TOP_K):
        pltpu.make_async_copy(v_ref, xs_hbm.at[pl.ds(0, tile * ROW_TILES), :], sem_rows).wait()


def _dispatch(dest_tiles, v, *, tile):
    T = v.shape[0] // ROW_TILES
    n_rows = T * TOP_K
    return pl.pallas_call(
        functools.partial(_dispatch_kernel, tile=tile, n_tiles=T // tile, n_rows=n_rows),
        out_shape=jax.ShapeDtypeStruct(((n_rows + ROW_BLOCK) * ROW_TILES, LANES), F32),
        grid=(T // tile,),
        in_specs=[pl.BlockSpec(memory_space=pl.ANY),
                  pl.BlockSpec((tile * ROW_TILES, LANES), lambda i: (i, 0))],
        out_specs=pl.BlockSpec(memory_space=pl.ANY),
        scratch_shapes=[pltpu.SMEM((2 * tile * TOP_K,), I32),
                        pltpu.VMEM((SUB_ROWS * ROW_TILES, LANES), F32),
                        pltpu.SemaphoreType.DMA((2,)),
                        pltpu.SemaphoreType.DMA,
                        pltpu.SemaphoreType.DMA],
        compiler_params=_cparams(("arbitrary",)),
        name="dispatch",
    )(dest_tiles, v)


def _moe_kernel(be_ref, ns_ref, nx_ref, xr_ref, x_hbm, w1_hbm, b1_ref, w2_hbm, b2_ref, y_ref,
                w1_stage, w2_stage, w1_scr, w2_scr, xbuf, x_scr, act_scr, sem_w, sem_x,
                *, n_blocks):
    i = pl.program_id(0)
    slot = lax.rem(i, 2)
    e = be_ref[i]
    nsub = ns_ref[i]
    changed = jnp.logical_or(i == 0, e != be_ref[jnp.maximum(i - 1, 0)])

    def x_copy(j, s):
        r0 = pl.multiple_of(xr_ref[j], ROW_TILES)
        return pltpu.make_async_copy(x_hbm.at[pl.ds(r0, ROW_BLOCK * ROW_TILES), :], xbuf.at[s],
                                     sem_x.at[s])

    @pl.when(i == 0)
    def _():
        x_copy(0, 0).start()

    @pl.when(nsub > 0)
    def _():
        x_copy(i, slot).wait()

    nxt_blk = jnp.minimum(i + 1, n_blocks - 1)

    @pl.when(jnp.logical_and(i + 1 < n_blocks, ns_ref[nxt_blk] > 0))
    def _():
        x_copy(nxt_blk, 1 - slot).start()

    def weight_copies(ex):
        return (pltpu.make_async_copy(w1_hbm.at[ex], w1_stage, sem_w.at[0]),
                pltpu.make_async_copy(w2_hbm.at[ex], w2_stage, sem_w.at[1]))

    @pl.when(i == 0)
    def _():
        for cp in weight_copies(e):
            cp.start()

    @pl.when(jnp.logical_and(changed, nsub > 0))
    def _():
        for cp in weight_copies(e):
            cp.wait()
        w1_scr[...] = w1_stage[...].astype(BF16)
        w2_scr[...] = w2_stage[...].astype(BF16)
        nxt = nx_ref[e]

        @pl.when(nxt >= 0)
        def _():
            for cp in weight_copies(nxt):
                cp.start()

    def compute(rows):
        x_scr[0:rows, :] = _load_tile_rows(xbuf.at[slot], rows).astype(BF16)
        cw = 256
        for c in range(D_FF // cw):
            glu = (_dot(x_scr[0:rows, :], w1_scr[:, c * cw:(c + 1) * cw])
                   + b1_ref[:, c * cw:(c + 1) * cw])
            lin = (_dot(x_scr[0:rows, :], w1_scr[:, D_FF + c * cw:D_FF + (c + 1) * cw])
                   + b1_ref[:, D_FF + c * cw:D_FF + (c + 1) * cw])
            glu = jnp.minimum(glu, SWIGLU_LIMIT)
            lin = jnp.clip(lin, -SWIGLU_LIMIT, SWIGLU_LIMIT)
            act = glu * jax.nn.sigmoid(SWIGLU_ALPHA * glu) * (lin + 1.0)
            act_scr[0:rows, c * cw:(c + 1) * cw] = act.astype(BF16)
        for c in range(D_MODEL // cw):
            y = (_dot(act_scr[0:rows, :], w2_scr[:, c * cw:(c + 1) * cw])
                 + b2_ref[:, c * cw:(c + 1) * cw])
            for s in range(cw // LANES):
                t = c * (cw // LANES) + s
                y_ref[pl.ds(t, rows, stride=ROW_TILES), :] = y[:, s * LANES:(s + 1) * LANES]
        if rows < ROW_BLOCK:
            y_ref[rows * ROW_TILES:, :] = jnp.zeros(((ROW_BLOCK - rows) * ROW_TILES, LANES), F32)

    for m in range(1, ROW_BLOCK // SUB_ROWS + 1):
        @pl.when(nsub == m)
        def _(m=m):
            compute(m * SUB_ROWS)

    @pl.when(nsub == 0)
    def _():
        y_ref[...] = jnp.zeros_like(y_ref)


def _moe(block_e, nsub, next_e, xrow, xs, w1, b1, w2, b2, *, n_blocks):
    return pl.pallas_call(
        functools.partial(_moe_kernel, n_blocks=n_blocks),
        out_shape=jax.ShapeDtypeStruct((n_blocks * ROW_BLOCK * ROW_TILES, LANES), F32),
        grid_spec=pltpu.PrefetchScalarGridSpec(
            num_scalar_prefetch=4,
            grid=(n_blocks,),
            in_specs=[pl.BlockSpec(memory_space=pl.ANY),
                      pl.BlockSpec(memory_space=pl.ANY),
                      pl.BlockSpec((None, 1, 2 * D_FF), lambda i, be, ns, nx, xr: (be[i], 0, 0)),
                      pl.BlockSpec(memory_space=pl.ANY),
                      pl.BlockSpec((None, 1, D_MODEL), lambda i, be, ns, nx, xr: (be[i], 0, 0))],
            out_specs=pl.BlockSpec((ROW_BLOCK * ROW_TILES, LANES),
                                   lambda i, be, ns, nx, xr: (i, 0)),
            scratch_shapes=[pltpu.VMEM((D_MODEL, 2 * D_FF), F32),
                            pltpu.VMEM((D_FF, D_MODEL), F32),
                            pltpu.VMEM((D_MODEL, 2 * D_FF), BF16),
                            pltpu.VMEM((D_FF, D_MODEL), BF16),
                            pltpu.VMEM((2, ROW_BLOCK * ROW_TILES, LANES), F32),
                            pltpu.VMEM((ROW_BLOCK, D_MODEL), BF16),
                            pltpu.VMEM((ROW_BLOCK, D_FF), BF16),
                            pltpu.SemaphoreType.DMA((2,)),
                            pltpu.SemaphoreType.DMA((2,))]),
        compiler_params=_cparams(("arbitrary",)),
        name="moe_ffn",
    )(block_e, nsub, next_e, xrow, xs, w1, b1, w2, b2)


def _combine_kernel(pos_hbm, src_hbm, yb_hbm, gate_ref, h1_ref, g2_ref, nw_ref, o_ref,
                    pos_smem, src_smem, win, hbuf, sem_tab, sem_win, *, n_tiles):
    i = pl.program_id(0)
    slot = lax.rem(i, 2)
    tile = COMBINE_TILE
    n_asg = tile * TOP_K
    chunk = WIN_ROWS * ROW_TILES
    slot_rows = WIN_CHUNKS * chunk

    def table_copies(j, s):
        return (pltpu.make_async_copy(pos_hbm.at[j], pos_smem.at[pl.ds(s * n_asg, n_asg)],
                                      sem_tab.at[s]),
                pltpu.make_async_copy(src_hbm.at[j], src_smem.at[pl.ds(s * n_asg, n_asg)],
                                      sem_tab.at[s]))

    def fetch_window(s):
        def one(c, carry):
            src = pl.multiple_of(src_smem[s * n_asg + c], ROW_TILES)
            dst = pl.multiple_of(s * slot_rows + c * chunk, chunk)
            pltpu.make_async_copy(yb_hbm.at[pl.ds(src, chunk), :], win.at[pl.ds(dst, chunk), :],
                                  sem_win.at[s]).start()
            return carry
        lax.fori_loop(0, WIN_CHUNKS, one, 0, unroll=2)

    @pl.when(i == 0)
    def _():
        for cp in table_copies(0, 0):
            cp.start()
        for cp in table_copies(0, 0):
            cp.wait()
        fetch_window(0)
        if n_tiles > 1:
            for cp in table_copies(1, 1):
                cp.start()

    @pl.when(i + 1 < n_tiles)
    def _():
        for cp in table_copies(i + 1, 1 - slot):
            cp.wait()
        fetch_window(1 - slot)

    pltpu.make_async_copy(yb_hbm.at[pl.ds(0, slot_rows), :],
                          win.at[pl.ds(pl.multiple_of(slot * slot_rows, slot_rows), slot_rows), :],
                          sem_win.at[slot]).wait()

    g2 = g2_ref[...]
    base = slot * n_asg

    def token(t, carry):
        e0 = base + t * TOP_K
        acc = None
        for k in range(TOP_K):
            p = pl.multiple_of(pos_smem[e0 + k], ROW_TILES)
            term = gate_ref[pl.ds(t * TOP_K + k, 1), :] * win[pl.ds(p, ROW_TILES), :]
            acc = term if acc is None else acc + term
        r0 = pl.multiple_of(t * ROW_TILES, ROW_TILES)
        hbuf[pl.ds(r0, ROW_TILES), :] = h1_ref[pl.ds(r0, ROW_TILES), :] + g2 * acc
        return carry
    lax.fori_loop(0, tile, token, 0, unroll=8)

    @pl.when(i + 2 < n_tiles)
    def _():
        for cp in table_copies(i + 2, slot):
            cp.start()

    h = _load_tile_rows(hbuf, tile)
    ms = jnp.mean(h * h, axis=-1, keepdims=True)
    o_ref[...] = h * lax.rsqrt(ms + EPS) * nw_ref[...]


def _combine(pos_tiles, src_tiles, yb, gate_rows, h1, g2_tiles, nw, *, tiles_per_batch):
    tile = COMBINE_TILE
    T = h1.shape[0] // ROW_TILES
    n_asg = tile * TOP_K
    return pl.pallas_call(
        functools.partial(_combine_kernel, n_tiles=T // tile),
        out_shape=jax.ShapeDtypeStruct((T, D_MODEL), F32),
        grid=(T // tile,),
        in_specs=[pl.BlockSpec(memory_space=pl.ANY),
                  pl.BlockSpec(memory_space=pl.ANY),
                  pl.BlockSpec(memory_space=pl.ANY),
                  pl.BlockSpec((tile * TOP_K, LANES), lambda i: (i, 0)),
                  pl.BlockSpec((tile * ROW_TILES, LANES), lambda i: (i, 0)),
                  pl.BlockSpec((None, ROW_TILES, LANES), lambda i: (i // tiles_per_batch, 0, 0)),
                  pl.BlockSpec((1, D_MODEL), lambda i: (0, 0))],
        out_specs=pl.BlockSpec((tile, D_MODEL), lambda i: (i, 0)),
        scratch_shapes=[pltpu.SMEM((2 * n_asg,), I32),
                        pltpu.SMEM((2 * n_asg,), I32),
                        pltpu.VMEM((2 * WIN_CHUNKS * WIN_ROWS * ROW_TILES, LANES), F32),
                        pltpu.VMEM((tile * ROW_TILES, LANES), F32),
                        pltpu.SemaphoreType.DMA((2,)),
                        pltpu.SemaphoreType.DMA((2,))],
        compiler_params=_cparams(("arbitrary",)),
        name="combine_norm",
    )(pos_tiles, src_tiles, yb, gate_rows, h1, g2_tiles, nw)


def _rope_tables(n):
    rows = n // GRID_W
    row = jnp.repeat(jnp.arange(rows, dtype=F32), GRID_W)
    col = jnp.tile(jnp.arange(GRID_W, dtype=F32), rows)
    n_freq = RET_DK // 4
    inv = ROPE_BASE ** (-jnp.arange(n_freq, dtype=F32) / n_freq)
    ang = jnp.concatenate([row[:, None] * inv, col[:, None] * inv], axis=-1)
    cos, sin = jnp.cos(ang), jnp.sin(ang)
    cos_h = jnp.concatenate([cos, cos], axis=-1)
    sin_h = jnp.concatenate([-sin, sin], axis=-1)
    return jnp.tile(cos_h, (1, RET_HEADS)), jnp.tile(sin_h, (1, RET_HEADS))


def kernel(x, c, ctx, c_ctx, w_ada, b_ada, norm_mix_w, norm_ffn_w, w_in, w_out, hg_lb,
           hg_norm_w, router_w, router_b, w1, b1, w2, b2, norm_final_w):
    B, N, D = x.shape
    C = ctx.shape[1]
    T = B * N
    assert D == D_MODEL and w_ada.shape[0] == 1

    cc = jnp.concatenate([c.astype(F32), c_ctx.astype(F32)[None, :],
                          jnp.zeros((16 - B - 1, D), F32)], axis=0)
    mod = _ada(cc, w_ada[0], b_ada[0][None, :])
    mod = mod.reshape(16, 6, 1, D).transpose(1, 0, 2, 3)
    sh1, sc1, g1, sh2, sc2, g2 = (mod[i] for i in range(6))

    w_in_bf = w_in[0].astype(BF16)
    w_out_bf = w_out[0].astype(BF16)
    nw_mix = norm_mix_w[0][None, :]
    cos_t, sin_t = _rope_tables(N)

    pb, lf = _inproj(x, sc1, sh1, None, nw_mix, w_in_bf, hg_lb[:2], cos_t, sin_t,
                     rope=True, tm=INPROJ_TILE)
    pbc, lfc = _inproj(ctx, sc1, sh1, B, nw_mix, w_in_bf, hg_lb[:2], cos_t[:C], sin_t[:C],
                       rope=False, tm=C)

    a_ret = _retention(pb, pbc)
    a_hg = _hgrn(pb, lf, pbc, lfc, hg_norm_w[0][None, :])

    rw_t = router_w[0].T
    rwh = rw_t.astype(BF16)
    rwl = (rw_t - rwh.astype(F32)).astype(BF16)
    rb = jnp.broadcast_to(router_b[0][:, None], (N_EXPERTS, LANES))
    h1, v, route_t, gate_rows, base_cnt, cnt = _outproj(a_ret, a_hg, w_out_bf, x, g1, sc2, sh2,
                                                        norm_ffn_w[0][None, :], rwh, rwl, rb,
                                                        tm=COMBINE_TILE)

    idx = route_t[0:TOP_K].astype(I32)
    rank = route_t[2 * TOP_K:3 * TOP_K].astype(I32)
    counts = cnt[:, 0].astype(I32)
    padded = (counts + ROW_BLOCK - 1) // ROW_BLOCK * ROW_BLOCK
    pend = jnp.cumsum(padded)
    pstart = pend - padded
    cstart = jnp.cumsum(counts) - counts
    onehot = idx[None] == jnp.arange(N_EXPERTS, dtype=I32)[:, None, None]

    def dest_tiles(first_row, tile):
        dest = (jnp.sum(jnp.where(onehot, first_row[:, None, None], 0), axis=0) + rank) * ROW_TILES
        return dest.reshape(TOP_K, T // tile, tile).transpose(1, 0, 2).reshape(T // tile,
                                                                               TOP_K * tile)
    n_blocks = (T * TOP_K) // ROW_BLOCK + N_EXPERTS
    starts = jnp.arange(n_blocks, dtype=I32) * ROW_BLOCK
    block_e = jnp.minimum(jnp.sum((pend[None, :] <= starts[:, None]).astype(I32), axis=1),
                          N_EXPERTS - 1)
    valid = jnp.clip(counts[block_e] - (starts - pstart[block_e]), 0, ROW_BLOCK)
    nsub = (valid + SUB_ROWS - 1) // SUB_ROWS
    first_from = lax.cummin(jnp.where(counts > 0, jnp.arange(N_EXPERTS, dtype=I32), N_EXPERTS),
                            reverse=True)
    next_e = jnp.concatenate([first_from[1:], jnp.full((1,), N_EXPERTS, I32)])
    next_e = jnp.where(next_e == N_EXPERTS, -1, next_e)
    xrow = jnp.where(nsub > 0, cstart[block_e] + starts - pstart[block_e], 0) * ROW_TILES

    xs = _dispatch(dest_tiles(cstart, DISPATCH_TILE), v, tile=DISPATCH_TILE)
    yb = _moe(block_e, nsub, next_e, xrow.astype(I32), xs, w1[0], b1[0][:, None, :], w2[0],
              b2[0][:, None, :], n_blocks=n_blocks)

    tc = COMBINE_TILE
    n_ct = T // tc
    base = base_cnt.reshape(n_ct, N_EXPERTS, LANES)[:, :, 0].astype(I32)
    n_run = jnp.concatenate([base[1:], counts[None, :]], axis=0) - base
    n_chunk = (n_run + WIN_ROWS - 1) // WIN_ROWS
    chunk_end = jnp.cumsum(n_chunk, axis=1)
    chunk_start = chunk_end - n_chunk
    win_off = chunk_start * WIN_ROWS
    c_id = jnp.arange(WIN_CHUNKS, dtype=I32)
    c_exp = jnp.sum((chunk_end[:, None, :] <= c_id[None, :, None]).astype(I32), axis=2)
    c_real = c_exp < N_EXPERTS
    c_own = c_exp[:, :, None] == jnp.arange(N_EXPERTS, dtype=I32)
    take = lambda tab: jnp.sum(jnp.where(c_own, tab[:, None, :], 0), axis=2)
    c_src = (take(pstart[None, :] + base) + (c_id[None, :] - take(chunk_start)) * WIN_ROWS)
    src_tiles = jnp.concatenate(
        [jnp.where(c_real, c_src, 0) * ROW_TILES,
         jnp.zeros((n_ct, tc * TOP_K - WIN_CHUNKS), I32)], axis=1)
    slot_off = (jnp.arange(n_ct, dtype=I32) % 2) * (WIN_CHUNKS * WIN_ROWS)
    shift = jnp.repeat(win_off - base + slot_off[:, None], tc, axis=0).T
    pos = (jnp.sum(jnp.where(onehot, shift[:, None, :], 0), axis=0) + rank) * ROW_TILES

    def per_tile(a):
        return a.reshape(TOP_K, n_ct, tc).transpose(1, 2, 0).reshape(n_ct, tc * TOP_K)

    out = _combine(per_tile(pos), src_tiles, yb, gate_rows, h1,
                   g2.reshape(16, ROW_TILES, LANES), norm_final_w[None, :],
                   tiles_per_batch=N // tc)
    return out.reshape(B, N, D)
```

```python
import functools
import math

import jax
import jax.numpy as jnp
from jax import lax
from jax.experimental import pallas as pl
from jax.experimental.pallas import tpu as pltpu

F32 = jnp.float32
BF16 = jnp.bfloat16
I32 = jnp.int32

D_MODEL = 1024
GRID_W = 64
RET_HEADS = 4
RET_DK = 64
HG_HEADS = 4
PROJ_W = 4096
ROPE_BASE = 10000.0
EPS = 1e-6
N_EXPERTS = 32
TOP_K = 4
D_FF = 1024
SWIGLU_LIMIT = 7.0
SWIGLU_ALPHA = 1.702
LOG2_E = 1.0 / math.log(2.0)

LANES = 128
CHUNK = 128
INPROJ_TILE = 1024
DISPATCH_TILE = 512
ROW_BLOCK = 1024
SUB_ROWS = 256
COMBINE_TILE = 512
WIN_ROWS = 8
WIN_CHUNKS = (COMBINE_TILE * TOP_K + N_EXPERTS * (WIN_ROWS - 1) + WIN_ROWS - 1) // WIN_ROWS
VMEM_LIMIT = 56 * 1024 * 1024
ROW_TILES = D_MODEL // LANES

C_RQ, C_RK, C_RV, C_RG, C_HQ, C_FF, C_FB, C_HV, C_HG = (
    0, 256, 512, 1024, 1536, 2048, 2560, 3072, 3584)


def _cparams(sem):
    return pltpu.CompilerParams(dimension_semantics=sem, vmem_limit_bytes=VMEM_LIMIT)


def _split_bf16(x):
    hi = x.astype(BF16)
    lo = (x - hi.astype(F32)).astype(BF16)
    return hi, lo


def _dot(a, b):
    return jnp.dot(a, b, preferred_element_type=F32)


def _dot_nt(a, b):
    return lax.dot_general(a, b, (((1,), (1,)), ((), ())), preferred_element_type=F32)


def _dot3(a, b):
    ah, al = _split_bf16(a)
    bh, bl = _split_bf16(b)
    return _dot(ah, bh) + (_dot(ah, bl) + _dot(al, bh))


def _silu(x):
    return x * jax.nn.sigmoid(x)


def _load_tile_rows(ref, n):
    return jnp.concatenate([ref[pl.ds(s, n, stride=ROW_TILES), :] for s in range(ROW_TILES)],
                           axis=1)


def _store_tile_rows(ref, x):
    n = x.shape[0]
    for s in range(ROW_TILES):
        ref[pl.ds(s, n, stride=ROW_TILES), :] = x[:, s * LANES:(s + 1) * LANES]


def _ada_kernel(c_ref, w_ref, b_ref, o_ref):
    s = _silu(c_ref[...])
    o_ref[...] = _dot3(s, w_ref[...]) + b_ref[...]


def _ada(cc, w, b):
    nblk = w.shape[1] // D_MODEL
    return pl.pallas_call(
        _ada_kernel,
        out_shape=jax.ShapeDtypeStruct((cc.shape[0], w.shape[1]), F32),
        grid=(nblk,),
        in_specs=[pl.BlockSpec(cc.shape, lambda j: (0, 0)),
                  pl.BlockSpec((D_MODEL, D_MODEL), lambda j: (0, j)),
                  pl.BlockSpec((1, D_MODEL), lambda j: (0, j))],
        out_specs=pl.BlockSpec((cc.shape[0], D_MODEL), lambda j: (0, j)),
        compiler_params=_cparams(("arbitrary",)),
        name="ada",
    )(cc, w, b)


def _inproj_kernel(x_ref, sc_ref, sh_ref, nw_ref, w_ref, lb_ref, cos_ref, sin_ref,
                   pb_ref, lf_ref, u_scr, *, rope):
    x = x_ref[...]
    ms = jnp.mean(x * x, axis=-1, keepdims=True)
    u = x * lax.rsqrt(ms + EPS) * nw_ref[...] * (1.0 + sc_ref[...]) + sh_ref[...]
    u_scr[...] = u.astype(BF16)

    def proj(lo, width):
        return _dot(u_scr[...], w_ref[:, lo:lo + width])

    tm = x.shape[0]
    if rope:
        lane = lax.broadcasted_iota(I32, (tm, LANES), 1)
        first = (lane & 32) == 0

    def put_rot(col, scale):
        for j in range(2):
            lo = col + j * LANES
            t = proj(lo, LANES)
            if scale != 1.0:
                t = t * scale
            if rope:
                tb = j * LANES
                rot = jnp.where(first, pltpu.roll(t, 96, axis=1), pltpu.roll(t, 32, axis=1))
                t = t * cos_ref[:, tb:tb + LANES] + rot * sin_ref[:, tb:tb + LANES]
            pb_ref[:, lo:lo + LANES] = t.astype(BF16)

    put_rot(C_RQ, 1.0)
    put_rot(C_RK, RET_DK ** -0.5)
    pb_ref[:, C_RV:C_RV + 512] = proj(C_RV, 512).astype(BF16)
    pb_ref[:, C_RG:C_RG + 512] = _silu(proj(C_RG, 512)).astype(BF16)
    pb_ref[:, C_HQ:C_HQ + 512] = _silu(proj(C_HQ, 512)).astype(BF16)
    pb_ref[:, C_HV:C_HV + 512] = proj(C_HV, 512).astype(BF16)
    pb_ref[:, C_HG:C_HG + 512] = _silu(proj(C_HG, 512)).astype(BF16)

    la = lb_ref[0]
    lbb = lb_ref[1]
    mx = jnp.maximum(la, lbb)
    ea = jnp.exp(la - mx)
    eb = jnp.exp(lbb - mx)
    lb = ea / (ea + eb)
    for d, col in enumerate((C_FF, C_FB)):
        lbd = lb[d:d + 1, :]
        f = lbd + (1.0 - lbd) * jax.nn.sigmoid(proj(col, 512))
        pb_ref[:, col:col + 512] = (1.0 - f).astype(BF16)
        lf_ref[:, d * 512:(d + 1) * 512] = jnp.log(f) * LOG2_E


def _inproj(x, sc, sh, mod_row, nw, w_bf, hg_lb, cos_t, sin_t, *, rope, tm):
    B, n, _ = x.shape
    nt = n // tm
    if mod_row is None:
        mrow = lambda b, j: (b, 0, 0)
    else:
        mrow = lambda b, j: (mod_row, 0, 0)
    return pl.pallas_call(
        functools.partial(_inproj_kernel, rope=rope),
        out_shape=(jax.ShapeDtypeStruct((B, n, PROJ_W), BF16),
                   jax.ShapeDtypeStruct((B, n, 1024), F32)),
        grid=(B, nt),
        in_specs=[pl.BlockSpec((None, tm, D_MODEL), lambda b, j: (b, j, 0)),
                  pl.BlockSpec((None, 1, D_MODEL), mrow),
                  pl.BlockSpec((None, 1, D_MODEL), mrow),
                  pl.BlockSpec((1, D_MODEL), lambda b, j: (0, 0)),
                  pl.BlockSpec((D_MODEL, PROJ_W), lambda b, j: (0, 0),
                               pipeline_mode=pl.Buffered(1)),
                  pl.BlockSpec((2, 2, 512), lambda b, j: (0, 0, 0)),
                  pl.BlockSpec((tm, 256), lambda b, j: (j, 0)),
                  pl.BlockSpec((tm, 256), lambda b, j: (j, 0))],
        out_specs=(pl.BlockSpec((None, tm, PROJ_W), lambda b, j: (b, j, 0)),
                   pl.BlockSpec((None, tm, 1024), lambda b, j: (b, j, 0))),
        scratch_shapes=[pltpu.VMEM((tm, D_MODEL), BF16)],
        compiler_params=_cparams(("arbitrary", "arbitrary")),
        name="inproj_rope" if rope else "inproj_ctx",
    )(x, sc, sh, nw, w_bf, hg_lb, cos_t, sin_t)


_RET_LGF = [math.log1p(-(2.0 ** (-5.0 - 2.0 * h))) for h in range(RET_HEADS)]
_RET_LGB = [math.log1p(-(2.0 ** (-6.0 - 2.0 * h))) for h in range(RET_HEADS)]


def _ret_kernel(q_ref, k_ref, v_ref, g_ref, kc_ref, vc_ref, o_ref,
                u_scr, sin_scr, kt_scr, dtot_scr, *, n_lat, n_ctx):
    L = CHUNK
    pair = pl.program_id(1)
    row = lax.broadcasted_iota(I32, (L, L), 0).astype(F32)
    col = lax.broadcasted_iota(I32, (L, L), 1).astype(F32)
    lane = lax.broadcasted_iota(I32, (L, LANES), 1)
    trow = lax.broadcasted_iota(I32, (L, 1), 0).astype(F32)
    tcol = lax.broadcasted_iota(I32, (1, L), 1).astype(F32)
    low_half = lane < RET_DK

    def u_chunk(k_blk, v_blk, ci, store_kt):
        kt = k_blk.astype(F32).T
        if store_kt is not None:
            kt_scr[store_kt] = kt.astype(BF16)
        for hh in range(2):
            lgf = jnp.where(pair == 0, _RET_LGF[hh], _RET_LGF[2 + hh])
            lgb = jnp.where(pair == 0, _RET_LGB[hh], _RET_LGB[2 + hh])
            kth = kt[hh * RET_DK:(hh + 1) * RET_DK, :]
            wkf = jnp.exp(lgf * (L - 1.0 - tcol))
            wkb = jnp.exp(lgb * tcol)
            lhs = jnp.concatenate([kth * wkf, kth * wkb], axis=0).astype(BF16)
            u_scr[hh, ci] = _dot(lhs, v_blk[:, hh * LANES:(hh + 1) * LANES])

    for c in range(n_ctx):
        u_chunk(kc_ref[c * L:(c + 1) * L, :], vc_ref[c * L:(c + 1) * L, :], c, None)

    def lat_u(c, carry):
        r0 = pl.multiple_of(c * L, L)
        u_chunk(k_ref[pl.ds(r0, L), :], v_ref[pl.ds(r0, L), :], n_ctx + c, c)
        return carry
    lax.fori_loop(0, n_lat, lat_u, 0, unroll=8)

    ones = jnp.ones((RET_DK, LANES), F32)
    for hh in range(2):
        lgf = jnp.where(pair == 0, _RET_LGF[hh], _RET_LGF[2 + hh])
        lgb = jnp.where(pair == 0, _RET_LGB[hh], _RET_LGB[2 + hh])
        d = row - col
        dtot_scr[hh] = jnp.where(d > 0, jnp.exp(lgf * jnp.maximum(d, 0.0)),
                                 jnp.where(d < 0, jnp.exp(lgb * jnp.maximum(-d, 0.0)), 2.0))
        af = jnp.exp(ones * (lgf * L))
        ab = jnp.exp(ones * (lgb * L))

        s = jnp.zeros((RET_DK, LANES), F32)
        for c in range(n_ctx):
            s = af * s + u_scr[hh, c, 0:RET_DK, :]
        sb = jnp.zeros((RET_DK, LANES), F32)
        for c in reversed(range(n_ctx)):
            sb = ab * sb + u_scr[hh, c, RET_DK:2 * RET_DK, :]

        def fwd(c, s, hh=hh, af=af):
            sin_scr[hh, c, 0:RET_DK, :] = s.astype(BF16)
            return af * s + u_scr[hh, n_ctx + c, 0:RET_DK, :]
        lax.fori_loop(0, n_lat, fwd, s)

        def bwd(i, sb, hh=hh, ab=ab):
            c = n_lat - 1 - i
            sin_scr[hh, c, RET_DK:2 * RET_DK, :] = sb.astype(BF16)
            return ab * sb + u_scr[hh, n_ctx + c, RET_DK:2 * RET_DK, :]
        lax.fori_loop(0, n_lat, bwd, sb)

    def out_chunk(c, carry):
        r0 = pl.multiple_of(c * L, L)
        q = q_ref[pl.ds(r0, L), :].astype(F32)
        qr = pltpu.roll(q, RET_DK, axis=1)
        kt = kt_scr[c]
        for hh in range(2):
            lgf = jnp.where(pair == 0, _RET_LGF[hh], _RET_LGF[2 + hh])
            lgb = jnp.where(pair == 0, _RET_LGB[hh], _RET_LGB[2 + hh])
            mine = low_half if hh == 0 else jnp.logical_not(low_half)
            qm = jnp.where(mine, q, 0.0).astype(BF16)
            p = (_dot(qm, kt) * dtot_scr[hh]).astype(BF16)
            vh = v_ref[pl.ds(r0, L), hh * LANES:(hh + 1) * LANES]
            wqf = jnp.exp(lgf * (trow + 1.0))
            wqb = jnp.exp(lgb * (L - trow))
            qa, qb = (q, qr) if hh == 0 else (qr, q)
            qs = jnp.where(low_half, qa * wqf, qb * wqb).astype(BF16)
            o = _dot(p, vh) + _dot(qs, sin_scr[hh, c])
            ms = jnp.mean(o * o, axis=-1, keepdims=True)
            gh = g_ref[pl.ds(r0, L), hh * LANES:(hh + 1) * LANES].astype(F32)
            o_ref[pl.ds(r0, L), hh * LANES:(hh + 1) * LANES] = (
                o * lax.rsqrt(ms + EPS) * gh).astype(BF16)
        return carry
    lax.fori_loop(0, n_lat, out_chunk, 0, unroll=8)


def _retention(pb, pbc):
    B, n, _ = pb.shape
    nc = pbc.shape[1]
    n_lat, n_ctx = n // CHUNK, nc // CHUNK
    return pl.pallas_call(
        functools.partial(_ret_kernel, n_lat=n_lat, n_ctx=n_ctx),
        out_shape=jax.ShapeDtypeStruct((B, n, 512), BF16),
        grid=(B, 2),
        in_specs=[pl.BlockSpec((None, n, LANES), lambda b, p: (b, 0, C_RQ // LANES + p)),
                  pl.BlockSpec((None, n, LANES), lambda b, p: (b, 0, C_RK // LANES + p)),
                  pl.BlockSpec((None, n, 256), lambda b, p: (b, 0, C_RV // 256 + p)),
                  pl.BlockSpec((None, n, 256), lambda b, p: (b, 0, C_RG // 256 + p)),
                  pl.BlockSpec((None, nc, LANES), lambda b, p: (b, 0, C_RK // LANES + p)),
                  pl.BlockSpec((None, nc, 256), lambda b, p: (b, 0, C_RV // 256 + p))],
        out_specs=pl.BlockSpec((None, n, 256), lambda b, p: (b, 0, p)),
        scratch_shapes=[pltpu.VMEM((2, n_lat + n_ctx, CHUNK, LANES), F32),
                        pltpu.VMEM((2, n_lat, CHUNK, LANES), BF16),
                        pltpu.VMEM((n_lat, LANES, CHUNK), BF16),
                        pltpu.VMEM((2, CHUNK, CHUNK), F32)],
        compiler_params=_cparams(("arbitrary", "arbitrary")),
        name="retention",
    )(pb, pb, pb, pb, pbc, pbc)


_LEVELS = (64, 32, 16, 8, 4, 2, 1)


def _expand_rows(r, rep):
    n = r.shape[0]
    if n == 1:
        return jnp.broadcast_to(r, (rep, r.shape[1]))
    return jnp.concatenate(
        [jnp.broadcast_to(r[i:i + 1, :], (rep, r.shape[1])) for i in range(n)], axis=0)


def _hgrn_kernel(q_ref, kf_ref, kb_ref, v_ref, g_ref, lff_ref, lfb_ref,
                 kfc_ref, kbc_ref, vc_ref, lffc_ref, lfbc_ref, nw_ref, o_ref,
                 ut_scr, a_scr, qs_scr, oi_scr, sin_scr, bfb_scr, *, n_lat, n_ctx):
    L = CHUNK
    row = lax.broadcasted_iota(I32, (L, L), 0)
    col = lax.broadcasted_iota(I32, (L, L), 1)
    xr_bits = lax.bitcast_convert_type((row ^ col).astype(F32), I32)
    lv = lax.shift_right_logical(xr_bits, 23) - 127
    row2 = lax.broadcasted_iota(I32, (L, 2 * L), 0)
    col2 = lax.broadcasted_iota(I32, (L, 2 * L), 1) & (L - 1)
    tril2 = jnp.where(col2 <= row2, 1.0, 0.0).astype(BF16)
    triu2 = jnp.where(col2 >= row2, 1.0, 0.0).astype(BF16)

    def cums(lff, lfb):
        hf, lof = _split_bf16(lff)
        hb, lob = _split_bf16(lfb)
        bf = _dot(tril2, jnp.concatenate([hf, lof], axis=0))
        bb = _dot(triu2, jnp.concatenate([hb, lob], axis=0))
        return bf, bb

    def state_part(ci, kf, kb, v_blk, bf, bb):
        endf = bf[L - 1:L, :]
        endb = bb[0:1, :]
        ksf = kf * jnp.exp2(endf - bf)
        ksb = kb * jnp.exp2(endb - bb)
        vt = v_blk.astype(F32).T.astype(BF16)
        ut_scr[ci] = _dot(vt, jnp.concatenate([ksf, ksb], axis=1).astype(BF16))
        a_scr[ci] = jnp.broadcast_to(
            jnp.concatenate([jnp.exp2(endf), jnp.exp2(endb)], axis=1), (8, 2 * LANES))

    for c in range(n_ctx):
        sl = slice(c * L, (c + 1) * L)
        bf, bb = cums(lffc_ref[sl, :], lfbc_ref[sl, :])
        state_part(c, kfc_ref[sl, :].astype(F32), kbc_ref[sl, :].astype(F32),
                   vc_ref[sl, :], bf, bb)

    def lat_chunk(c, u):
        rows = pl.ds(pl.multiple_of(c * L, L), L)
        bf_scr = bfb_scr.at[u, 0]
        bb_scr = bfb_scr.at[u, 1]
        q = q_ref[rows, :].astype(F32)
        kf = kf_ref[rows, :].astype(F32)
        kb = kb_ref[rows, :].astype(F32)
        v_blk = v_ref[rows, :]
        lff = lff_ref[rows, :]
        lfb = lfb_ref[rows, :]
        bf, bb = cums(lff, lfb)
        state_part(n_ctx + c, kf, kb, v_blk, bf, bb)
        qs_scr[rows, :] = jnp.concatenate([q * jnp.exp2(bf), q * jnp.exp2(bb)],
                                          axis=1).astype(BF16)
        bf_scr[...] = bf
        bb_scr[...] = bb

        acc = jnp.zeros((L, L), F32)
        for lvl, h in enumerate(_LEVELS):
            bit = (row & h) != 0
            ksel = jnp.where(bit, kb, kf)
            if h >= 4:
                n = (L // 2) // h
                if n == 1:
                    rf = bf_scr[h - 1:h, :]
                    rb = bb_scr[h:h + 1, :]
                else:
                    rf = bf_scr[pl.ds(h - 1, n, stride=2 * h), :]
                    rb = bb_scr[pl.ds(h, n, stride=2 * h), :]
                df = bf - _expand_rows(rf, 2 * h)
                db = bb - _expand_rows(rb, 2 * h)
                eq = jnp.where(bit, df, db)
                ek = -jnp.where(bit, db, df)
            elif h == 2:
                m = row & 3
                lff_n = pltpu.roll(lff, L - 1, axis=0)
                lfb_n = pltpu.roll(lfb, L - 1, axis=0)
                eq = jnp.where(m == 2, lff,
                               jnp.where(m == 3, lff + pltpu.roll(lff, 1, axis=0),
                                         jnp.where(m == 0, lfb + lfb_n, lfb)))
                ek = jnp.where(m == 3, pltpu.roll(lfb, 1, axis=0),
                               jnp.where(m == 0, lff_n, 0.0))
            else:
                eq = jnp.where(bit, lff, lfb)
                ek = None
            lhs = (q * jnp.exp2(eq)).astype(BF16)
            rhs = (ksel if ek is None else ksel * jnp.exp2(ek)).astype(BF16)
            acc = jnp.where(lv == 6 - lvl, _dot_nt(lhs, rhs), acc)

        dsum = jnp.sum(q * (kf + kb), axis=-1, keepdims=True)
        oi_scr[rows, :] = _dot(acc.astype(BF16), v_blk) + dsum * v_blk.astype(F32)

    def lat_pair(i, carry):
        for u in range(8):
            lat_chunk(8 * i + u, u)
        return carry
    lax.fori_loop(0, n_lat // 8, lat_pair, 0)

    st = jnp.zeros((LANES, LANES), F32)
    for c in range(n_ctx):
        st = st * a_scr[c, 0:1, 0:LANES] + ut_scr[c, :, 0:LANES]
    stb = jnp.zeros((LANES, LANES), F32)
    for c in reversed(range(n_ctx)):
        stb = stb * a_scr[c, 0:1, LANES:2 * LANES] + ut_scr[c, :, LANES:2 * LANES]

    def fwd(c, st):
        sin_scr[c, :, 0:LANES] = st.astype(BF16)
        ci = n_ctx + c
        return st * a_scr[ci, 0:1, 0:LANES] + ut_scr[ci, :, 0:LANES]
    lax.fori_loop(0, n_lat, fwd, st)

    def bwd(i, stb):
        c = n_lat - 1 - i
        sin_scr[c, :, LANES:2 * LANES] = stb.astype(BF16)
        ci = n_ctx + c
        return stb * a_scr[ci, 0:1, LANES:2 * LANES] + ut_scr[ci, :, LANES:2 * LANES]
    lax.fori_loop(0, n_lat, bwd, stb)

    def out_chunk(c, carry):
        rows = pl.ds(pl.multiple_of(c * L, L), L)
        o = oi_scr[rows, :] + _dot_nt(qs_scr[rows, :], sin_scr[c])
        ms = jnp.mean(o * o, axis=-1, keepdims=True)
        y = o * lax.rsqrt(ms + EPS) * nw_ref[...] * g_ref[rows, :].astype(F32)
        o_ref[rows, :] = y.astype(BF16)
        return carry
    lax.fori_loop(0, n_lat, out_chunk, 0, unroll=8)


def _hgrn(pb, lf, pbc, lfc, nw):
    B, n, _ = pb.shape
    nc = pbc.shape[1]
    n_lat, n_ctx = n // CHUNK, nc // CHUNK

    def colblk(rows, col0):
        return pl.BlockSpec((None, rows, LANES), lambda b, h: (b, 0, col0 // LANES + h))

    return pl.pallas_call(
        functools.partial(_hgrn_kernel, n_lat=n_lat, n_ctx=n_ctx),
        out_shape=jax.ShapeDtypeStruct((B, n, 512), BF16),
        grid=(B, HG_HEADS),
        in_specs=[colblk(n, C_HQ), colblk(n, C_FF), colblk(n, C_FB), colblk(n, C_HV),
                  colblk(n, C_HG), colblk(n, 0), colblk(n, 512),
                  colblk(nc, C_FF), colblk(nc, C_FB), colblk(nc, C_HV),
                  colblk(nc, 0), colblk(nc, 512),
                  pl.BlockSpec((1, LANES), lambda b, h: (0, 0))],
        out_specs=pl.BlockSpec((None, n, LANES), lambda b, h: (b, 0, h)),
        scratch_shapes=[pltpu.VMEM((n_lat + n_ctx, LANES, 2 * LANES), F32),
                        pltpu.VMEM((n_lat + n_ctx, 8, 2 * LANES), F32),
                        pltpu.VMEM((n, 2 * LANES), BF16),
                        pltpu.VMEM((n, LANES), F32),
                        pltpu.VMEM((n_lat, LANES, 2 * LANES), BF16),
                        pltpu.VMEM((8, 2, CHUNK, LANES), F32)],
        compiler_params=_cparams(("arbitrary", "arbitrary")),
        name="hgrn2",
    )(pb, pb, pb, pb, pb, lf, lf, pbc, pbc, pbc, lfc, lfc, nw)


def _outproj_kernel(ar_ref, ah_ref, w_ref, x_ref, g1_ref, sc_ref, sh_ref, nw_ref,
                    rwh_ref, rwl_ref, rb_ref, tri_ref, h1_ref, v_ref, route_t_ref, gate_ref, base_ref,
                    cnt_ref, cnt_scr):
    first_step = jnp.logical_and(pl.program_id(0) == 0, pl.program_id(1) == 0)

    @pl.when(first_step)
    def _():
        cnt_scr[...] = jnp.zeros_like(cnt_scr)

    y = _dot(ar_ref[...], w_ref[0:512, :]) + _dot(ah_ref[...], w_ref[512:1024, :])
    h1 = x_ref[...] + g1_ref[...] * y
    _store_tile_rows(h1_ref, h1)
    ms = jnp.mean(h1 * h1, axis=-1, keepdims=True)
    v = h1 * lax.rsqrt(ms + EPS) * nw_ref[...] * (1.0 + sc_ref[...]) + sh_ref[...]
    _store_tile_rows(v_ref, v)

    tm = v.shape[0]
    vh, vl = _split_bf16(v)
    rwh, rwl = rwh_ref[...], rwl_ref[...]
    l = (_dot_nt(rwh, vh) + (_dot_nt(rwh, vl) + _dot_nt(rwl, vh))) + rb_ref[:, 0:1]
    row_f = lax.broadcasted_iota(I32, (N_EXPERTS, tm), 0).astype(F32)
    sels, tops, idxs = [], [], []
    for _ in range(TOP_K):
        m = jnp.max(l, axis=0, keepdims=True)
        i = jnp.min(jnp.where(l == m, row_f, float(N_EXPERTS)), axis=0, keepdims=True)
        sel = row_f == i
        l = jnp.where(sel, -jnp.inf, l)
        sels.append(sel)
        tops.append(m)
        idxs.append(i)
    es = [jnp.exp(t - tops[0]) for t in tops]
    den = es[0] + es[1] + es[2] + es[3]
    gates = [e / den for e in es]

    oh = jnp.zeros((N_EXPERTS, tm), F32)
    for sel in sels:
        oh = jnp.where(sel, 1.0, oh)
    cnt = cnt_scr[:, 0:1]
    before = _dot(oh.astype(BF16), tri_ref[...]) + cnt
    ranks = [jnp.sum(jnp.where(sel, before, 0.0), axis=0, keepdims=True) for sel in sels]
    base_ref[...] = cnt_scr[...]
    cnt_scr[...] = cnt_scr[...] + jnp.sum(oh, axis=1, keepdims=True)
    cnt_ref[...] = cnt_scr[...]

    zero4 = jnp.zeros((TOP_K, tm), F32)
    route_t_ref[...] = jnp.concatenate(idxs + [zero4] + ranks + [zero4], axis=0)
    g_cols = jnp.concatenate(gates + [zero4], axis=0).T
    for k in range(TOP_K):
        gate_ref[pl.ds(k, tm, stride=TOP_K), :] = jnp.broadcast_to(g_cols[:, k:k + 1], (tm, LANES))


def _outproj(a_ret, a_hg, w_bf, x, g1, sc2, sh2, nw, rwh, rwl, rb, *, tm):
    B, n, _ = x.shape
    nt = n // tm
    T = B * n
    assert tm == COMBINE_TILE
    mrow = lambda b, j: (b, 0, 0)
    tok = lambda b, j: (b * nt + j, 0)
    const = lambda b, j: (0, 0)
    tri = jnp.triu(jnp.ones((tm, tm), BF16), 1)
    return pl.pallas_call(
        _outproj_kernel,
        out_shape=(jax.ShapeDtypeStruct((T * ROW_TILES, LANES), F32),
                   jax.ShapeDtypeStruct((T * ROW_TILES, LANES), F32),
                   jax.ShapeDtypeStruct((16, T), F32),
                   jax.ShapeDtypeStruct((T * TOP_K, LANES), F32),
                   jax.ShapeDtypeStruct((B * nt * N_EXPERTS, LANES), F32),
                   jax.ShapeDtypeStruct((N_EXPERTS, LANES), F32)),
        grid=(B, nt),
        in_specs=[pl.BlockSpec((None, tm, 512), lambda b, j: (b, j, 0)),
                  pl.BlockSpec((None, tm, 512), lambda b, j: (b, j, 0)),
                  pl.BlockSpec((D_MODEL, D_MODEL), const),
                  pl.BlockSpec((None, tm, D_MODEL), lambda b, j: (b, j, 0)),
                  pl.BlockSpec((None, 1, D_MODEL), mrow),
                  pl.BlockSpec((None, 1, D_MODEL), mrow),
                  pl.BlockSpec((None, 1, D_MODEL), mrow),
                  pl.BlockSpec((1, D_MODEL), const),
                  pl.BlockSpec((N_EXPERTS, D_MODEL), const),
                  pl.BlockSpec((N_EXPERTS, D_MODEL), const),
                  pl.BlockSpec((N_EXPERTS, LANES), const),
                  pl.BlockSpec((tm, tm), const)],
        out_specs=(pl.BlockSpec((tm * ROW_TILES, LANES), tok),
                   pl.BlockSpec((tm * ROW_TILES, LANES), tok),
                   pl.BlockSpec((16, tm), lambda b, j: (0, b * nt + j)),
                   pl.BlockSpec((tm * TOP_K, LANES), tok),
                   pl.BlockSpec((N_EXPERTS, LANES), tok),
                   pl.BlockSpec((N_EXPERTS, LANES), const)),
        scratch_shapes=[pltpu.VMEM((N_EXPERTS, LANES), F32)],
        compiler_params=_cparams(("arbitrary", "arbitrary")),
        name="outproj_router",
    )(a_ret, a_hg, w_bf, x, g1, sc2, sh2, nw, rwh, rwl, rb, tri)


def _dispatch_kernel(dest_hbm, v_ref, xs_hbm, idx_smem, zbuf, sem_idx, sem_z, sem_rows,
                     *, tile, n_tiles, n_rows):
    i = pl.program_id(0)
    slot = lax.rem(i, 2)

    def idx_copy(j, s):
        n_asg = tile * TOP_K
        return pltpu.make_async_copy(dest_hbm.at[j], idx_smem.at[pl.ds(s * n_asg, n_asg)],
                                     sem_idx.at[s])

    def zero_copy(j):
        r0 = (n_rows + j * SUB_ROWS) * ROW_TILES
        return pltpu.make_async_copy(zbuf, xs_hbm.at[pl.ds(r0, SUB_ROWS * ROW_TILES), :], sem_z)

    @pl.when(i == 0)
    def _():
        idx_copy(0, 0).start()
        zbuf[...] = jnp.zeros_like(zbuf)
        for j in range(ROW_BLOCK // SUB_ROWS):
            zero_copy(j).start()
        for j in range(ROW_BLOCK // SUB_ROWS):
            zero_copy(j).wait()

    idx_copy(i, slot).wait()

    @pl.when(i + 1 < n_tiles)
    def _():
        idx_copy(i + 1, 1 - slot).start()

    def issue(t, carry):
        e0 = slot * (tile * TOP_K) + t
        for k in range(TOP_K):
            d = pl.multiple_of(idx_smem[e0 + k * tile], ROW_TILES)
            src = v_ref.at[pl.ds(pl.multiple_of(t * ROW_TILES, ROW_TILES), ROW_TILES), :]
            pltpu.make_async_copy(src, xs_hbm.at[pl.ds(d, ROW_TILES), :],
                                  sem_rows).start(priority=k % 2)
        return carry
    lax.fori_loop(0, tile, issue, 0, unroll=8)
    for _ in range(TOP_K):
        pltpu.make_async_copy(v_ref, xs_hbm.at[pl.ds(0, tile * ROW_TILES), :], sem_rows).wait()


def _dispatch(dest_tiles, v, *, tile):
    T = v.shape[0] // ROW_TILES
    n_rows = T * TOP_K
    return pl.pallas_call(
        functools.partial(_dispatch_kernel, tile=tile, n_tiles=T // tile, n_rows=n_rows),
        out_shape=jax.ShapeDtypeStruct(((n_rows + ROW_BLOCK) * ROW_TILES, LANES), F32),
        grid=(T // tile,),
        in_specs=[pl.BlockSpec(memory_space=pl.ANY),
                  pl.BlockSpec((tile * ROW_TILES, LANES), lambda i: (i, 0))],
        out_specs=pl.BlockSpec(memory_space=pl.ANY),
        scratch_shapes=[pltpu.SMEM((2 * tile * TOP_K,), I32),
                        pltpu.VMEM((SUB_ROWS * ROW_TILES, LANES), F32),
                        pltpu.SemaphoreType.DMA((2,)),
                        pltpu.SemaphoreType.DMA,
                        pltpu.SemaphoreType.DMA],
        compiler_params=_cparams(("arbitrary",)),
        name="dispatch",
    )(dest_tiles, v)


def _moe_kernel(be_ref, ns_ref, nx_ref, xr_ref, x_hbm, w1_hbm, b1_ref, w2_hbm, b2_ref, y_ref,
                w1_stage, w2_stage, w1_scr, w2_scr, xbuf, x_scr, act_scr, sem_w, sem_x,
                *, n_blocks):
    i = pl.program_id(0)
    slot = lax.rem(i, 2)
    e = be_ref[i]
    nsub = ns_ref[i]
    changed = jnp.logical_or(i == 0, e != be_ref[jnp.maximum(i - 1, 0)])

    def x_copy(j, s):
        r0 = pl.multiple_of(xr_ref[j], ROW_TILES)
        return pltpu.make_async_copy(x_hbm.at[pl.ds(r0, ROW_BLOCK * ROW_TILES), :], xbuf.at[s],
                                     sem_x.at[s])

    @pl.when(i == 0)
    def _():
        x_copy(0, 0).start()

    @pl.when(nsub > 0)
    def _():
        x_copy(i, slot).wait()

    nxt_blk = jnp.minimum(i + 1, n_blocks - 1)

    @pl.when(jnp.logical_and(i + 1 < n_blocks, ns_ref[nxt_blk] > 0))
    def _():
        x_copy(nxt_blk, 1 - slot).start()

    def weight_copies(ex):
        return (pltpu.make_async_copy(w1_hbm.at[ex], w1_stage, sem_w.at[0]),
                pltpu.make_async_copy(w2_hbm.at[ex], w2_stage, sem_w.at[1]))

    @pl.when(i == 0)
    def _():
        for cp in weight_copies(e):
            cp.start()

    @pl.when(jnp.logical_and(changed, nsub > 0))
    def _():
        for cp in weight_copies(e):
            cp.wait()
        w1_scr[...] = w1_stage[...].astype(BF16)
        w2_scr[...] = w2_stage[...].astype(BF16)
        nxt = nx_ref[e]

        @pl.when(nxt >= 0)
        def _():
            for cp in weight_copies(nxt):
                cp.start()

    def compute(rows):
        x_scr[0:rows, :] = _load_tile_rows(xbuf.at[slot], rows).astype(BF16)
        cw = 256
        for c in range(D_FF // cw):
            glu = (_dot(x_scr[0:rows, :], w1_scr[:, c * cw:(c + 1) * cw])
                   + b1_ref[:, c * cw:(c + 1) * cw])
            lin = (_dot(x_scr[0:rows, :], w1_scr[:, D_FF + c * cw:D_FF + (c + 1) * cw])
                   + b1_ref[:, D_FF + c * cw:D_FF + (c + 1) * cw])
            glu = jnp.minimum(glu, SWIGLU_LIMIT)
            lin = jnp.clip(lin, -SWIGLU_LIMIT, SWIGLU_LIMIT)
            act = glu * jax.nn.sigmoid(SWIGLU_ALPHA * glu) * (lin + 1.0)
            act_scr[0:rows, c * cw:(c + 1) * cw] = act.astype(BF16)
        for c in range(D_MODEL // cw):
            y = (_dot(act_scr[0:rows, :], w2_scr[:, c * cw:(c + 1) * cw])
                 + b2_ref[:, c * cw:(c + 1) * cw])
            for s in range(cw // LANES):
                t = c * (cw // LANES) + s
                y_ref[pl.ds(t, rows, stride=ROW_TILES), :] = y[:, s * LANES:(s + 1) * LANES]
        if rows < ROW_BLOCK:
            y_ref[rows * ROW_TILES:, :] = jnp.zeros(((ROW_BLOCK - rows) * ROW_TILES, LANES), F32)

    for m in range(1, ROW_BLOCK // SUB_ROWS + 1):
        @pl.when(nsub == m)
        def _(m=m):
            compute(m * SUB_ROWS)

    @pl.when(nsub == 0)
    def _():
        y_ref[...] = jnp.zeros_like(y_ref)


def _moe(block_e, nsub, next_e, xrow, xs, w1, b1, w2, b2, *, n_blocks):
    return pl.pallas_call(
        functools.partial(_moe_kernel, n_blocks=n_blocks),
        out_shape=jax.ShapeDtypeStruct((n_blocks * ROW_BLOCK * ROW_TILES, LANES), F32),
        grid_spec=pltpu.PrefetchScalarGridSpec(
            num_scalar_prefetch=4,
            grid=(n_blocks,),
            in_specs=[pl.BlockSpec(memory_space=pl.ANY),
                      pl.BlockSpec(memory_space=pl.ANY),
                      pl.BlockSpec((None, 1, 2 * D_FF), lambda i, be, ns, nx, xr: (be[i], 0, 0)),
                      pl.BlockSpec(memory_space=pl.ANY),
                      pl.BlockSpec((None, 1, D_MODEL), lambda i, be, ns, nx, xr: (be[i], 0, 0))],
            out_specs=pl.BlockSpec((ROW_BLOCK * ROW_TILES, LANES),
                                   lambda i, be, ns, nx, xr: (i, 0)),
            scratch_shapes=[pltpu.VMEM((D_MODEL, 2 * D_FF), F32),
                            pltpu.VMEM((D_FF, D_MODEL), F32),
                            pltpu.VMEM((D_MODEL, 2 * D_FF), BF16),
                            pltpu.VMEM((D_FF, D_MODEL), BF16),
                            pltpu.VMEM((2, ROW_BLOCK * ROW_TILES, LANES), F32),
                            pltpu.VMEM((ROW_BLOCK, D_MODEL), BF16),
                            pltpu.VMEM((ROW_BLOCK, D_FF), BF16),
                            pltpu.SemaphoreType.DMA((2,)),
                            pltpu.SemaphoreType.DMA((2,))]),
        compiler_params=_cparams(("arbitrary",)),
        name="moe_ffn",
    )(block_e, nsub, next_e, xrow, xs, w1, b1, w2, b2)


def _combine_kernel(pos_hbm, src_hbm, yb_hbm, gate_ref, h1_ref, g2_ref, nw_ref, o_ref,
                    pos_smem, src_smem, win, hbuf, sem_tab, sem_win, *, n_tiles):
    i = pl.program_id(0)
    slot = lax.rem(i, 2)
    tile = COMBINE_TILE
    n_asg = tile * TOP_K
    chunk = WIN_ROWS * ROW_TILES
    slot_rows = WIN_CHUNKS * chunk

    def table_copies(j, s):
        return (pltpu.make_async_copy(pos_hbm.at[j], pos_smem.at[pl.ds(s * n_asg, n_asg)],
                                      sem_tab.at[s]),
                pltpu.make_async_copy(src_hbm.at[j], src_smem.at[pl.ds(s * n_asg, n_asg)],
                                      sem_tab.at[s]))

    def fetch_window(s):
        def one(c, carry):
            src = pl.multiple_of(src_smem[s * n_asg + c], ROW_TILES)
            dst = pl.multiple_of(s * slot_rows + c * chunk, chunk)
            pltpu.make_async_copy(yb_hbm.at[pl.ds(src, chunk), :], win.at[pl.ds(dst, chunk), :],
                                  sem_win.at[s]).start()
            return carry
        lax.fori_loop(0, WIN_CHUNKS, one, 0, unroll=4)

    @pl.when(i == 0)
    def _():
        for cp in table_copies(0, 0):
            cp.start()
        for cp in table_copies(0, 0):
            cp.wait()
        fetch_window(0)
        if n_tiles > 1:
            for cp in table_copies(1, 1):
                cp.start()

    @pl.when(i + 1 < n_tiles)
    def _():
        for cp in table_copies(i + 1, 1 - slot):
            cp.wait()
        fetch_window(1 - slot)

    pltpu.make_async_copy(yb_hbm.at[pl.ds(0, slot_rows), :],
                          win.at[pl.ds(pl.multiple_of(slot * slot_rows, slot_rows), slot_rows), :],
                          sem_win.at[slot]).wait()

    g2 = g2_ref[...]
    base = slot * n_asg

    def token(t, carry):
        e0 = base + t * TOP_K
        acc = None
        for k in range(TOP_K):
            p = pl.multiple_of(pos_smem[e0 + k], ROW_TILES)
            term = gate_ref[pl.ds(t * TOP_K + k, 1), :] * win[pl.ds(p, ROW_TILES), :]
            acc = term if acc is None else acc + term
        r0 = pl.multiple_of(t * ROW_TILES, ROW_TILES)
        hbuf[pl.ds(r0, ROW_TILES), :] = h1_ref[pl.ds(r0, ROW_TILES), :] + g2 * acc
        return carry
    lax.fori_loop(0, tile, token, 0, unroll=8)

    @pl.when(i + 2 < n_tiles)
    def _():
        for cp in table_copies(i + 2, slot):
            cp.start()

    h = _load_tile_rows(hbuf, tile)
    ms = jnp.mean(h * h, axis=-1, keepdims=True)
    o_ref[...] = h * lax.rsqrt(ms + EPS) * nw_ref[...]


def _combine(pos_tiles, src_tiles, yb, gate_rows, h1, g2_tiles, nw, *, tiles_per_batch):
    tile = COMBINE_TILE
    T = h1.shape[0] // ROW_TILES
    n_asg = tile * TOP_K
    return pl.pallas_call(
        functools.partial(_combine_kernel, n_tiles=T // tile),
        out_shape=jax.ShapeDtypeStruct((T, D_MODEL), F32),
        grid=(T // tile,),
        in_specs=[pl.BlockSpec(memory_space=pl.ANY),
                  pl.BlockSpec(memory_space=pl.ANY),
                  pl.BlockSpec(memory_space=pl.ANY),
                  pl.BlockSpec((tile * TOP_K, LANES), lambda i: (i, 0)),
                  pl.BlockSpec((tile * ROW_TILES, LANES), lambda i: (i, 0)),
                  pl.BlockSpec((None, ROW_TILES, LANES), lambda i: (i // tiles_per_batch, 0, 0)),
                  pl.BlockSpec((1, D_MODEL), lambda i: (0, 0))],
        out_specs=pl.BlockSpec((tile, D_MODEL), lambda i: (i, 0)),
        scratch_shapes=[pltpu.SMEM((2 * n_asg,), I32),
                        pltpu.SMEM((2 * n_asg,), I32),
                        pltpu.VMEM((2 * WIN_CHUNKS * WIN_ROWS * ROW_TILES, LANES), F32),
                        pltpu.VMEM((tile * ROW_TILES, LANES), F32),
                        pltpu.SemaphoreType.DMA((2,)),
                        pltpu.SemaphoreType.DMA((2,))],
        compiler_params=_cparams(("arbitrary",)),
        name="combine_norm",
    )(pos_tiles, src_tiles, yb, gate_rows, h1, g2_tiles, nw)


def _rope_tables(n):
    rows = n // GRID_W
    row = jnp.repeat(jnp.arange(rows, dtype=F32), GRID_W)
    col = jnp.tile(jnp.arange(GRID_W, dtype=F32), rows)
    n_freq = RET_DK // 4
    inv = ROPE_BASE ** (-jnp.arange(n_freq, dtype=F32) / n_freq)
    ang = jnp.concatenate([row[:, None] * inv, col[:, None] * inv], axis=-1)
    cos, sin = jnp.cos(ang), jnp.sin(ang)
    cos_h = jnp.concatenate([cos, cos], axis=-1)
    sin_h = jnp.concatenate([-sin, sin], axis=-1)
    return jnp.tile(cos_h, (1, RET_HEADS)), jnp.tile(sin_h, (1, RET_HEADS))


def kernel(x, c, ctx, c_ctx, w_ada, b_ada, norm_mix_w, norm_ffn_w, w_in, w_out, hg_lb,
           hg_norm_w, router_w, router_b, w1, b1, w2, b2, norm_final_w):
    B, N, D = x.shape
    C = ctx.shape[1]
    T = B * N
    assert D == D_MODEL and w_ada.shape[0] == 1

    cc = jnp.concatenate([c.astype(F32), c_ctx.astype(F32)[None, :],
                          jnp.zeros((16 - B - 1, D), F32)], axis=0)
    mod = _ada(cc, w_ada[0], b_ada[0][None, :])
    mod = mod.reshape(16, 6, 1, D).transpose(1, 0, 2, 3)
    sh1, sc1, g1, sh2, sc2, g2 = (mod[i] for i in range(6))

    w_in_bf = w_in[0].astype(BF16)
    w_out_bf = w_out[0].astype(BF16)
    nw_mix = norm_mix_w[0][None, :]
    cos_t, sin_t = _rope_tables(N)

    pb, lf = _inproj(x, sc1, sh1, None, nw_mix, w_in_bf, hg_lb[:2], cos_t, sin_t,
                     rope=True, tm=INPROJ_TILE)
    pbc, lfc = _inproj(ctx, sc1, sh1, B, nw_mix, w_in_bf, hg_lb[:2], cos_t[:C], sin_t[:C],
                       rope=False, tm=C)

    a_ret = _retention(pb, pbc)
    a_hg = _hgrn(pb, lf, pbc, lfc, hg_norm_w[0][None, :])

    rw_t = router_w[0].T
    rwh = rw_t.astype(BF16)
    rwl = (rw_t - rwh.astype(F32)).astype(BF16)
    rb = jnp.broadcast_to(router_b[0][:, None], (N_EXPERTS, LANES))
    h1, v, route_t, gate_rows, base_cnt, cnt = _outproj(a_ret, a_hg, w_out_bf, x, g1, sc2, sh2,
                                                        norm_ffn_w[0][None, :], rwh, rwl, rb,
                                                        tm=COMBINE_TILE)

    idx = route_t[0:TOP_K].astype(I32)
    rank = route_t[2 * TOP_K:3 * TOP_K].astype(I32)
    counts = cnt[:, 0].astype(I32)
    padded = (counts + ROW_BLOCK - 1) // ROW_BLOCK * ROW_BLOCK
    pend = jnp.cumsum(padded)
    pstart = pend - padded
    cstart = jnp.cumsum(counts) - counts
    onehot = idx[None] == jnp.arange(N_EXPERTS, dtype=I32)[:, None, None]

    def dest_tiles(first_row, tile):
        dest = (jnp.sum(jnp.where(onehot, first_row[:, None, None], 0), axis=0) + rank) * ROW_TILES
        return dest.reshape(TOP_K, T // tile, tile).transpose(1, 0, 2).reshape(T // tile,
                                                                               TOP_K * tile)
    n_blocks = (T * TOP_K) // ROW_BLOCK + N_EXPERTS
    starts = jnp.arange(n_blocks, dtype=I32) * ROW_BLOCK
    block_e = jnp.minimum(jnp.sum((pend[None, :] <= starts[:, None]).astype(I32), axis=1),
                          N_EXPERTS - 1)
    valid = jnp.clip(counts[block_e] - (starts - pstart[block_e]), 0, ROW_BLOCK)
    nsub = (valid + SUB_ROWS - 1) // SUB_ROWS
    first_from = lax.cummin(jnp.where(counts > 0, jnp.arange(N_EXPERTS, dtype=I32), N_EXPERTS),
                            reverse=True)
    next_e = jnp.concatenate([first_from[1:], jnp.full((1,), N_EXPERTS, I32)])
    next_e = jnp.where(next_e == N_EXPERTS, -1, next_e)
    xrow = jnp.where(nsub > 0, cstart[block_e] + starts - pstart[block_e], 0) * ROW_TILES

    xs = _dispatch(dest_tiles(cstart, DISPATCH_TILE), v, tile=DISPATCH_TILE)
    yb = _moe(block_e, nsub, next_e, xrow.astype(I32), xs, w1[0], b1[0][:, None, :], w2[0],
              b2[0][:, None, :], n_blocks=n_blocks)

    tc = COMBINE_TILE
    n_ct = T // tc
    base = base_cnt.reshape(n_ct, N_EXPERTS, LANES)[:, :, 0].astype(I32)
    n_run = jnp.concatenate([base[1:], counts[None, :]], axis=0) - base
    n_chunk = (n_run + WIN_ROWS - 1) // WIN_ROWS
    chunk_end = jnp.cumsum(n_chunk, axis=1)
    chunk_start = chunk_end - n_chunk
    win_off = chunk_start * WIN_ROWS
    c_id = jnp.arange(WIN_CHUNKS, dtype=I32)
    c_exp = jnp.sum((chunk_end[:, None, :] <= c_id[None, :, None]).astype(I32), axis=2)
    c_real = c_exp < N_EXPERTS
    c_own = c_exp[:, :, None] == jnp.arange(N_EXPERTS, dtype=I32)
    take = lambda tab: jnp.sum(jnp.where(c_own, tab[:, None, :], 0), axis=2)
    c_src = (take(pstart[None, :] + base) + (c_id[None, :] - take(chunk_start)) * WIN_ROWS)
    src_tiles = jnp.concatenate(
        [jnp.where(c_real, c_src, 0) * ROW_TILES,
         jnp.zeros((n_ct, tc * TOP_K - WIN_CHUNKS), I32)], axis=1)
    slot_off = (jnp.arange(n_ct, dtype=I32) % 2) * (WIN_CHUNKS * WIN_ROWS)
    shift = jnp.repeat(win_off - base + slot_off[:, None], tc, axis=0).T
    pos = (jnp.sum(jnp.where(onehot, shift[:, None, :], 0), axis=0) + rank) * ROW_TILES

    def per_tile(a):
        return a.reshape(TOP_K, n_ct, tc).transpose(1, 2, 0).reshape(n_ct, tc * TOP_K)

    out = _combine(per_tile(pos), src_tiles, yb, gate_rows, h1,
                   g2.reshape(16, ROW_TILES, LANES), norm_final_w[None, :],
                   tiles_per_batch=N // tc)
    return out.reshape(B, N, D)
```

```python
import functools
import math

import jax
import jax.numpy as jnp
from jax import lax
from jax.experimental import pallas as pl
from jax.experimental.pallas import tpu as pltpu

F32 = jnp.float32
BF16 = jnp.bfloat16
I32 = jnp.int32

D_MODEL = 1024
GRID_W = 64
RET_HEADS = 4
RET_DK = 64
HG_HEADS = 4
PROJ_W = 4096
ROPE_BASE = 10000.0
EPS = 1e-6
N_EXPERTS = 32
TOP_K = 4
D_FF = 1024
SWIGLU_LIMIT = 7.0
SWIGLU_ALPHA = 1.702
LOG2_E = 1.0 / math.log(2.0)

LANES = 128
CHUNK = 128
INPROJ_TILE = 1024
DISPATCH_TILE = 512
ROW_BLOCK = 1024
SUB_ROWS = 256
COMBINE_TILE = 512
WIN_ROWS = 8
WIN_CHUNKS = (COMBINE_TILE * TOP_K + N_EXPERTS * (WIN_ROWS - 1) + WIN_ROWS - 1) // WIN_ROWS
VMEM_LIMIT = 56 * 1024 * 1024
ROW_TILES = D_MODEL // LANES

C_RQ, C_RK, C_RV, C_RG, C_HQ, C_FF, C_FB, C_HV, C_HG = (
    0, 256, 512, 1024, 1536, 2048, 2560, 3072, 3584)


def _cparams(sem):
    return pltpu.CompilerParams(dimension_semantics=sem, vmem_limit_bytes=VMEM_LIMIT)


def _split_bf16(x):
    hi = x.astype(BF16)
    lo = (x - hi.astype(F32)).astype(BF16)
    return hi, lo


def _dot(a, b):
    return jnp.dot(a, b, preferred_element_type=F32)


def _dot_nt(a, b):
    return lax.dot_general(a, b, (((1,), (1,)), ((), ())), preferred_element_type=F32)


def _dot3(a, b):
    ah, al = _split_bf16(a)
    bh, bl = _split_bf16(b)
    return _dot(ah, bh) + (_dot(ah, bl) + _dot(al, bh))


def _silu(x):
    return x * jax.nn.sigmoid(x)


def _load_tile_rows(ref, n):
    return jnp.concatenate([ref[pl.ds(s, n, stride=ROW_TILES), :] for s in range(ROW_TILES)],
                           axis=1)


def _store_tile_rows(ref, x):
    n = x.shape[0]
    for s in range(ROW_TILES):
        ref[pl.ds(s, n, stride=ROW_TILES), :] = x[:, s * LANES:(s + 1) * LANES]


def _ada_kernel(c_ref, w_ref, b_ref, o_ref):
    s = _silu(c_ref[...])
    o_ref[...] = _dot3(s, w_ref[...]) + b_ref[...]


def _ada(cc, w, b):
    nblk = w.shape[1] // D_MODEL
    return pl.pallas_call(
        _ada_kernel,
        out_shape=jax.ShapeDtypeStruct((cc.shape[0], w.shape[1]), F32),
        grid=(nblk,),
        in_specs=[pl.BlockSpec(cc.shape, lambda j: (0, 0)),
                  pl.BlockSpec((D_MODEL, D_MODEL), lambda j: (0, j)),
                  pl.BlockSpec((1, D_MODEL), lambda j: (0, j))],
        out_specs=pl.BlockSpec((cc.shape[0], D_MODEL), lambda j: (0, j)),
        compiler_params=_cparams(("arbitrary",)),
        name="ada",
    )(cc, w, b)


def _inproj_kernel(x_ref, sc_ref, sh_ref, nw_ref, w_ref, lb_ref, cos_ref, sin_ref,
                   pb_ref, lf_ref, u_scr, *, rope):
    x = x_ref[...]
    ms = jnp.mean(x * x, axis=-1, keepdims=True)
    u = x * lax.rsqrt(ms + EPS) * nw_ref[...] * (1.0 + sc_ref[...]) + sh_ref[...]
    u_scr[...] = u.astype(BF16)

    def proj(lo, width):
        return _dot(u_scr[...], w_ref[:, lo:lo + width])

    tm = x.shape[0]
    if rope:
        lane = lax.broadcasted_iota(I32, (tm, LANES), 1)
        first = (lane & 32) == 0

    def put_rot(col, scale):
        for j in range(2):
            lo = col + j * LANES
            t = proj(lo, LANES)
            if scale != 1.0:
                t = t * scale
            if rope:
                tb = j * LANES
                rot = jnp.where(first, pltpu.roll(t, 96, axis=1), pltpu.roll(t, 32, axis=1))
                t = t * cos_ref[:, tb:tb + LANES] + rot * sin_ref[:, tb:tb + LANES]
            pb_ref[:, lo:lo + LANES] = t.astype(BF16)

    put_rot(C_RQ, 1.0)
    put_rot(C_RK, RET_DK ** -0.5)
    pb_ref[:, C_RV:C_RV + 512] = proj(C_RV, 512).astype(BF16)
    pb_ref[:, C_RG:C_RG + 512] = _silu(proj(C_RG, 512)).astype(BF16)
    pb_ref[:, C_HQ:C_HQ + 512] = _silu(proj(C_HQ, 512)).astype(BF16)
    pb_ref[:, C_HV:C_HV + 512] = proj(C_HV, 512).astype(BF16)
    pb_ref[:, C_HG:C_HG + 512] = _silu(proj(C_HG, 512)).astype(BF16)

    la = lb_ref[0]
    lbb = lb_ref[1]
    mx = jnp.maximum(la, lbb)
    ea = jnp.exp(la - mx)
    eb = jnp.exp(lbb - mx)
    lb = ea / (ea + eb)
    for d, col in enumerate((C_FF, C_FB)):
        lbd = lb[d:d + 1, :]
        f = lbd + (1.0 - lbd) * jax.nn.sigmoid(proj(col, 512))
        pb_ref[:, col:col + 512] = (1.0 - f).astype(BF16)
        lf_ref[:, d * 512:(d + 1) * 512] = jnp.log(f) * LOG2_E


def _inproj(x, sc, sh, mod_row, nw, w_bf, hg_lb, cos_t, sin_t, *, rope, tm):
    B, n, _ = x.shape
    nt = n // tm
    if mod_row is None:
        mrow = lambda b, j: (b, 0, 0)
    else:
        mrow = lambda b, j: (mod_row, 0, 0)
    return pl.pallas_call(
        functools.partial(_inproj_kernel, rope=rope),
        out_shape=(jax.ShapeDtypeStruct((B, n, PROJ_W), BF16),
                   jax.ShapeDtypeStruct((B, n, 1024), F32)),
        grid=(B, nt),
        in_specs=[pl.BlockSpec((None, tm, D_MODEL), lambda b, j: (b, j, 0)),
                  pl.BlockSpec((None, 1, D_MODEL), mrow),
                  pl.BlockSpec((None, 1, D_MODEL), mrow),
                  pl.BlockSpec((1, D_MODEL), lambda b, j: (0, 0)),
                  pl.BlockSpec((D_MODEL, PROJ_W), lambda b, j: (0, 0),
                               pipeline_mode=pl.Buffered(1)),
                  pl.BlockSpec((2, 2, 512), lambda b, j: (0, 0, 0)),
                  pl.BlockSpec((tm, 256), lambda b, j: (j, 0)),
                  pl.BlockSpec((tm, 256), lambda b, j: (j, 0))],
        out_specs=(pl.BlockSpec((None, tm, PROJ_W), lambda b, j: (b, j, 0)),
                   pl.BlockSpec((None, tm, 1024), lambda b, j: (b, j, 0))),
        scratch_shapes=[pltpu.VMEM((tm, D_MODEL), BF16)],
        compiler_params=_cparams(("arbitrary", "arbitrary")),
        name="inproj_rope" if rope else "inproj_ctx",
    )(x, sc, sh, nw, w_bf, hg_lb, cos_t, sin_t)


_RET_LGF = [math.log1p(-(2.0 ** (-5.0 - 2.0 * h))) for h in range(RET_HEADS)]
_RET_LGB = [math.log1p(-(2.0 ** (-6.0 - 2.0 * h))) for h in range(RET_HEADS)]


def _ret_kernel(q_ref, k_ref, v_ref, g_ref, kc_ref, vc_ref, o_ref,
                u_scr, sin_scr, kt_scr, dtot_scr, *, n_lat, n_ctx):
    L = CHUNK
    pair = pl.program_id(1)
    row = lax.broadcasted_iota(I32, (L, L), 0).astype(F32)
    col = lax.broadcasted_iota(I32, (L, L), 1).astype(F32)
    lane = lax.broadcasted_iota(I32, (L, LANES), 1)
    trow = lax.broadcasted_iota(I32, (L, 1), 0).astype(F32)
    tcol = lax.broadcasted_iota(I32, (1, L), 1).astype(F32)
    low_half = lane < RET_DK

    def u_chunk(k_blk, v_blk, ci, store_kt):
        kt = k_blk.astype(F32).T
        if store_kt is not None:
            kt_scr[store_kt] = kt.astype(BF16)
        for hh in range(2):
            lgf = jnp.where(pair == 0, _RET_LGF[hh], _RET_LGF[2 + hh])
            lgb = jnp.where(pair == 0, _RET_LGB[hh], _RET_LGB[2 + hh])
            kth = kt[hh * RET_DK:(hh + 1) * RET_DK, :]
            wkf = jnp.exp(lgf * (L - 1.0 - tcol))
            wkb = jnp.exp(lgb * tcol)
            lhs = jnp.concatenate([kth * wkf, kth * wkb], axis=0).astype(BF16)
            u_scr[hh, ci] = _dot(lhs, v_blk[:, hh * LANES:(hh + 1) * LANES])

    for c in range(n_ctx):
        u_chunk(kc_ref[c * L:(c + 1) * L, :], vc_ref[c * L:(c + 1) * L, :], c, None)

    def lat_u(c, carry):
        r0 = pl.multiple_of(c * L, L)
        u_chunk(k_ref[pl.ds(r0, L), :], v_ref[pl.ds(r0, L), :], n_ctx + c, c)
        return carry
    lax.fori_loop(0, n_lat, lat_u, 0, unroll=8)

    ones = jnp.ones((RET_DK, LANES), F32)
    for hh in range(2):
        lgf = jnp.where(pair == 0, _RET_LGF[hh], _RET_LGF[2 + hh])
        lgb = jnp.where(pair == 0, _RET_LGB[hh], _RET_LGB[2 + hh])
        d = row - col
        dtot_scr[hh] = jnp.where(d > 0, jnp.exp(lgf * jnp.maximum(d, 0.0)),
                                 jnp.where(d < 0, jnp.exp(lgb * jnp.maximum(-d, 0.0)), 2.0))
        af = jnp.exp(ones * (lgf * L))
        ab = jnp.exp(ones * (lgb * L))

        s = jnp.zeros((RET_DK, LANES), F32)
        for c in range(n_ctx):
            s = af * s + u_scr[hh, c, 0:RET_DK, :]
        sb = jnp.zeros((RET_DK, LANES), F32)
        for c in reversed(range(n_ctx)):
            sb = ab * sb + u_scr[hh, c, RET_DK:2 * RET_DK, :]

        def fwd(c, s, hh=hh, af=af):
            sin_scr[hh, c, 0:RET_DK, :] = s.astype(BF16)
            return af * s + u_scr[hh, n_ctx + c, 0:RET_DK, :]
        lax.fori_loop(0, n_lat, fwd, s)

        def bwd(i, sb, hh=hh, ab=ab):
            c = n_lat - 1 - i
            sin_scr[hh, c, RET_DK:2 * RET_DK, :] = sb.astype(BF16)
            return ab * sb + u_scr[hh, n_ctx + c, RET_DK:2 * RET_DK, :]
        lax.fori_loop(0, n_lat, bwd, sb)

    def out_chunk(c, carry):
        r0 = pl.multiple_of(c * L, L)
        q = q_ref[pl.ds(r0, L), :].astype(F32)
        qr = pltpu.roll(q, RET_DK, axis=1)
        kt = kt_scr[c]
        for hh in range(2):
            lgf = jnp.where(pair == 0, _RET_LGF[hh], _RET_LGF[2 + hh])
            lgb = jnp.where(pair == 0, _RET_LGB[hh], _RET_LGB[2 + hh])
            mine = low_half if hh == 0 else jnp.logical_not(low_half)
            qm = jnp.where(mine, q, 0.0).astype(BF16)
            p = (_dot(qm, kt) * dtot_scr[hh]).astype(BF16)
            vh = v_ref[pl.ds(r0, L), hh * LANES:(hh + 1) * LANES]
            wqf = jnp.exp(lgf * (trow + 1.0))
            wqb = jnp.exp(lgb * (L - trow))
            qa, qb = (q, qr) if hh == 0 else (qr, q)
            qs = jnp.where(low_half, qa * wqf, qb * wqb).astype(BF16)
            o = _dot(p, vh) + _dot(qs, sin_scr[hh, c])
            ms = jnp.mean(o * o, axis=-1, keepdims=True)
            gh = g_ref[pl.ds(r0, L), hh * LANES:(hh + 1) * LANES].astype(F32)
            o_ref[pl.ds(r0, L), hh * LANES:(hh + 1) * LANES] = (
                o * lax.rsqrt(ms + EPS) * gh).astype(BF16)
        return carry
    lax.fori_loop(0, n_lat, out_chunk, 0, unroll=8)


def _retention(pb, pbc):
    B, n, _ = pb.shape
    nc = pbc.shape[1]
    n_lat, n_ctx = n // CHUNK, nc // CHUNK
    return pl.pallas_call(
        functools.partial(_ret_kernel, n_lat=n_lat, n_ctx=n_ctx),
        out_shape=jax.ShapeDtypeStruct((B, n, 512), BF16),
        grid=(B, 2),
        in_specs=[pl.BlockSpec((None, n, LANES), lambda b, p: (b, 0, C_RQ // LANES + p)),
                  pl.BlockSpec((None, n, LANES), lambda b, p: (b, 0, C_RK // LANES + p)),
                  pl.BlockSpec((None, n, 256), lambda b, p: (b, 0, C_RV // 256 + p)),
                  pl.BlockSpec((None, n, 256), lambda b, p: (b, 0, C_RG // 256 + p)),
                  pl.BlockSpec((None, nc, LANES), lambda b, p: (b, 0, C_RK // LANES + p)),
                  pl.BlockSpec((None, nc, 256), lambda b, p: (b, 0, C_RV // 256 + p))],
        out_specs=pl.BlockSpec((None, n, 256), lambda b, p: (b, 0, p)),
        scratch_shapes=[pltpu.VMEM((2, n_lat + n_ctx, CHUNK, LANES), F32),
                        pltpu.VMEM((2, n_lat, CHUNK, LANES), BF16),
                        pltpu.VMEM((n_lat, LANES, CHUNK), BF16),
                        pltpu.VMEM((2, CHUNK, CHUNK), F32)],
        compiler_params=_cparams(("arbitrary", "arbitrary")),
        name="retention",
    )(pb, pb, pb, pb, pbc, pbc)


_LEVELS = (64, 32, 16, 8, 4, 2, 1)


def _expand_rows(r, rep):
    n = r.shape[0]
    if n == 1:
        return jnp.broadcast_to(r, (rep, r.shape[1]))
    return jnp.concatenate(
        [jnp.broadcast_to(r[i:i + 1, :], (rep, r.shape[1])) for i in range(n)], axis=0)


def _hgrn_kernel(q_ref, kf_ref, kb_ref, v_ref, g_ref, lff_ref, lfb_ref,
                 kfc_ref, kbc_ref, vc_ref, lffc_ref, lfbc_ref, nw_ref, o_ref,
                 ut_scr, a_scr, qs_scr, oi_scr, sin_scr, bfb_scr, *, n_lat, n_ctx):
    L = CHUNK
    row = lax.broadcasted_iota(I32, (L, L), 0)
    col = lax.broadcasted_iota(I32, (L, L), 1)
    xr_bits = lax.bitcast_convert_type((row ^ col).astype(F32), I32)
    lv = lax.shift_right_logical(xr_bits, 23) - 127
    row2 = lax.broadcasted_iota(I32, (L, 2 * L), 0)
    col2 = lax.broadcasted_iota(I32, (L, 2 * L), 1) & (L - 1)
    tril2 = jnp.where(col2 <= row2, 1.0, 0.0).astype(BF16)
    triu2 = jnp.where(col2 >= row2, 1.0, 0.0).astype(BF16)

    def cums(lff, lfb):
        hf, lof = _split_bf16(lff)
        hb, lob = _split_bf16(lfb)
        bf = _dot(tril2, jnp.concatenate([hf, lof], axis=0))
        bb = _dot(triu2, jnp.concatenate([hb, lob], axis=0))
        return bf, bb

    def state_part(ci, kf, kb, v_blk, bf, bb):
        endf = bf[L - 1:L, :]
        endb = bb[0:1, :]
        ksf = kf * jnp.exp2(endf - bf)
        ksb = kb * jnp.exp2(endb - bb)
        vt = v_blk.astype(F32).T.astype(BF16)
        ut_scr[ci] = _dot(vt, jnp.concatenate([ksf, ksb], axis=1).astype(BF16))
        a_scr[ci] = jnp.broadcast_to(
            jnp.concatenate([jnp.exp2(endf), jnp.exp2(endb)], axis=1), (8, 2 * LANES))

    for c in range(n_ctx):
        sl = slice(c * L, (c + 1) * L)
        bf, bb = cums(lffc_ref[sl, :], lfbc_ref[sl, :])
        state_part(c, kfc_ref[sl, :].astype(F32), kbc_ref[sl, :].astype(F32),
                   vc_ref[sl, :], bf, bb)

    def lat_chunk(c, u):
        rows = pl.ds(pl.multiple_of(c * L, L), L)
        bf_scr = bfb_scr.at[u, 0]
        bb_scr = bfb_scr.at[u, 1]
        q = q_ref[rows, :].astype(F32)
        kf = kf_ref[rows, :].astype(F32)
        kb = kb_ref[rows, :].astype(F32)
        v_blk = v_ref[rows, :]
        lff = lff_ref[rows, :]
        lfb = lfb_ref[rows, :]
        bf, bb = cums(lff, lfb)
        state_part(n_ctx + c, kf, kb, v_blk, bf, bb)
        qs_scr[rows, :] = jnp.concatenate([q * jnp.exp2(bf), q * jnp.exp2(bb)],
                                          axis=1).astype(BF16)
        bf_scr[...] = bf
        bb_scr[...] = bb

        acc = jnp.zeros((L, L), F32)
        for lvl, h in enumerate(_LEVELS):
            bit = (row & h) != 0
            ksel = jnp.where(bit, kb, kf)
            if h >= 4:
                n = (L // 2) // h
                if n == 1:
                    rf = bf_scr[h - 1:h, :]
                    rb = bb_scr[h:h + 1, :]
                else:
                    rf = bf_scr[pl.ds(h - 1, n, stride=2 * h), :]
                    rb = bb_scr[pl.ds(h, n, stride=2 * h), :]
                df = bf - _expand_rows(rf, 2 * h)
                db = bb - _expand_rows(rb, 2 * h)
                eq = jnp.where(bit, df, db)
                ek = -jnp.where(bit, db, df)
            elif h == 2:
                m = row & 3
                lff_n = pltpu.roll(lff, L - 1, axis=0)
                lfb_n = pltpu.roll(lfb, L - 1, axis=0)
                eq = jnp.where(m == 2, lff,
                               jnp.where(m == 3, lff + pltpu.roll(lff, 1, axis=0),
                                         jnp.where(m == 0, lfb + lfb_n, lfb)))
                ek = jnp.where(m == 3, pltpu.roll(lfb, 1, axis=0),
                               jnp.where(m == 0, lff_n, 0.0))
            else:
                eq = jnp.where(bit, lff, lfb)
                ek = None
            lhs = (q * jnp.exp2(eq)).astype(BF16)
            rhs = (ksel if ek is None else ksel * jnp.exp2(ek)).astype(BF16)
            acc = jnp.where(lv == 6 - lvl, _dot_nt(lhs, rhs), acc)

        dsum = jnp.sum(q * (kf + kb), axis=-1, keepdims=True)
        oi_scr[rows, :] = _dot(acc.astype(BF16), v_blk) + dsum * v_blk.astype(F32)

    def lat_pair(i, carry):
        for u in range(16):
            lat_chunk(16 * i + u, u)
        return carry
    lax.fori_loop(0, n_lat // 16, lat_pair, 0)

    st = jnp.zeros((LANES, LANES), F32)
    for c in range(n_ctx):
        st = st * a_scr[c, 0:1, 0:LANES] + ut_scr[c, :, 0:LANES]
    stb = jnp.zeros((LANES, LANES), F32)
    for c in reversed(range(n_ctx)):
        stb = stb * a_scr[c, 0:1, LANES:2 * LANES] + ut_scr[c, :, LANES:2 * LANES]

    def fwd(c, st):
        sin_scr[c, :, 0:LANES] = st.astype(BF16)
        ci = n_ctx + c
        return st * a_scr[ci, 0:1, 0:LANES] + ut_scr[ci, :, 0:LANES]
    lax.fori_loop(0, n_lat, fwd, st)

    def bwd(i, stb):
        c = n_lat - 1 - i
        sin_scr[c, :, LANES:2 * LANES] = stb.astype(BF16)
        ci = n_ctx + c
        return stb * a_scr[ci, 0:1, LANES:2 * LANES] + ut_scr[ci, :, LANES:2 * LANES]
    lax.fori_loop(0, n_lat, bwd, stb)

    def out_chunk(c, carry):
        rows = pl.ds(pl.multiple_of(c * L, L), L)
        o = oi_scr[rows, :] + _dot_nt(qs_scr[rows, :], sin_scr[c])
        ms = jnp.mean(o * o, axis=-1, keepdims=True)
        y = o * lax.rsqrt(ms + EPS) * nw_ref[...] * g_ref[rows, :].astype(F32)
        o_ref[rows, :] = y.astype(BF16)
        return carry
    lax.fori_loop(0, n_lat, out_chunk, 0, unroll=8)


def _hgrn(pb, lf, pbc, lfc, nw):
    B, n, _ = pb.shape
    nc = pbc.shape[1]
    n_lat, n_ctx = n // CHUNK, nc // CHUNK

    def colblk(rows, col0):
        return pl.BlockSpec((None, rows, LANES), lambda b, h: (b, 0, col0 // LANES + h))

    return pl.pallas_call(
        functools.partial(_hgrn_kernel, n_lat=n_lat, n_ctx=n_ctx),
        out_shape=jax.ShapeDtypeStruct((B, n, 512), BF16),
        grid=(B, HG_HEADS),
        in_specs=[colblk(n, C_HQ), colblk(n, C_FF), colblk(n, C_FB), colblk(n, C_HV),
                  colblk(n, C_HG), colblk(n, 0), colblk(n, 512),
                  colblk(nc, C_FF), colblk(nc, C_FB), colblk(nc, C_HV),
                  colblk(nc, 0), colblk(nc, 512),
                  pl.BlockSpec((1, LANES), lambda b, h: (0, 0))],
        out_specs=pl.BlockSpec((None, n, LANES), lambda b, h: (b, 0, h)),
        scratch_shapes=[pltpu.VMEM((n_lat + n_ctx, LANES, 2 * LANES), F32),
                        pltpu.VMEM((n_lat + n_ctx, 8, 2 * LANES), F32),
                        pltpu.VMEM((n, 2 * LANES), BF16),
                        pltpu.VMEM((n, LANES), F32),
                        pltpu.VMEM((n_lat, LANES, 2 * LANES), BF16),
                        pltpu.VMEM((16, 2, CHUNK, LANES), F32)],
        compiler_params=_cparams(("arbitrary", "arbitrary")),
        name="hgrn2",
    )(pb, pb, pb, pb, pb, lf, lf, pbc, pbc, pbc, lfc, lfc, nw)


def _outproj_kernel(ar_ref, ah_ref, w_ref, x_ref, g1_ref, sc_ref, sh_ref, nw_ref,
                    rwh_ref, rwl_ref, rb_ref, tri_ref, h1_ref, v_ref, route_t_ref, gate_ref, base_ref,
                    cnt_ref, cnt_scr):
    first_step = jnp.logical_and(pl.program_id(0) == 0, pl.program_id(1) == 0)

    @pl.when(first_step)
    def _():
        cnt_scr[...] = jnp.zeros_like(cnt_scr)

    y = _dot(ar_ref[...], w_ref[0:512, :]) + _dot(ah_ref[...], w_ref[512:1024, :])
    h1 = x_ref[...] + g1_ref[...] * y
    _store_tile_rows(h1_ref, h1)
    ms = jnp.mean(h1 * h1, axis=-1, keepdims=True)
    v = h1 * lax.rsqrt(ms + EPS) * nw_ref[...] * (1.0 + sc_ref[...]) + sh_ref[...]
    _store_tile_rows(v_ref, v)

    tm = v.shape[0]
    vh, vl = _split_bf16(v)
    rwh, rwl = rwh_ref[...], rwl_ref[...]
    l = (_dot_nt(rwh, vh) + (_dot_nt(rwh, vl) + _dot_nt(rwl, vh))) + rb_ref[:, 0:1]
    row_f = lax.broadcasted_iota(I32, (N_EXPERTS, tm), 0).astype(F32)
    sels, tops, idxs = [], [], []
    for _ in range(TOP_K):
        m = jnp.max(l, axis=0, keepdims=True)
        i = jnp.min(jnp.where(l == m, row_f, float(N_EXPERTS)), axis=0, keepdims=True)
        sel = row_f == i
        l = jnp.where(sel, -jnp.inf, l)
        sels.append(sel)
        tops.append(m)
        idxs.append(i)
    es = [jnp.exp(t - tops[0]) for t in tops]
    den = es[0] + es[1] + es[2] + es[3]
    gates = [e / den for e in es]

    oh = jnp.zeros((N_EXPERTS, tm), F32)
    for sel in sels:
        oh = jnp.where(sel, 1.0, oh)
    cnt = cnt_scr[:, 0:1]
    before = _dot(oh.astype(BF16), tri_ref[...]) + cnt
    ranks = [jnp.sum(jnp.where(sel, before, 0.0), axis=0, keepdims=True) for sel in sels]
    base_ref[...] = cnt_scr[...]
    cnt_scr[...] = cnt_scr[...] + jnp.sum(oh, axis=1, keepdims=True)
    cnt_ref[...] = cnt_scr[...]

    zero4 = jnp.zeros((TOP_K, tm), F32)
    route_t_ref[...] = jnp.concatenate(idxs + [zero4] + ranks + [zero4], axis=0)
    g_cols = jnp.concatenate(gates + [zero4], axis=0).T
    for k in range(TOP_K):
        gate_ref[pl.ds(k, tm, stride=TOP_K), :] = jnp.broadcast_to(g_cols[:, k:k + 1], (tm, LANES))


def _outproj(a_ret, a_hg, w_bf, x, g1, sc2, sh2, nw, rwh, rwl, rb, *, tm):
    B, n, _ = x.shape
    nt = n // tm
    T = B * n
    assert tm == COMBINE_TILE
    mrow = lambda b, j: (b, 0, 0)
    tok = lambda b, j: (b * nt + j, 0)
    const = lambda b, j: (0, 0)
    tri = jnp.triu(jnp.ones((tm, tm), BF16), 1)
    return pl.pallas_call(
        _outproj_kernel,
        out_shape=(jax.ShapeDtypeStruct((T * ROW_TILES, LANES), F32),
                   jax.ShapeDtypeStruct((T * ROW_TILES, LANES), F32),
                   jax.ShapeDtypeStruct((16, T), F32),
                   jax.ShapeDtypeStruct((T * TOP_K, LANES), F32),
                   jax.ShapeDtypeStruct((B * nt * N_EXPERTS, LANES), F32),
                   jax.ShapeDtypeStruct((N_EXPERTS, LANES), F32)),
        grid=(B, nt),
        in_specs=[pl.BlockSpec((None, tm, 512), lambda b, j: (b, j, 0)),
                  pl.BlockSpec((None, tm, 512), lambda b, j: (b, j, 0)),
                  pl.BlockSpec((D_MODEL, D_MODEL), const),
                  pl.BlockSpec((None, tm, D_MODEL), lambda b, j: (b, j, 0)),
                  pl.BlockSpec((None, 1, D_MODEL), mrow),
                  pl.BlockSpec((None, 1, D_MODEL), mrow),
                  pl.BlockSpec((None, 1, D_MODEL), mrow),
                  pl.BlockSpec((1, D_MODEL), const),
                  pl.BlockSpec((N_EXPERTS, D_MODEL), const),
                  pl.BlockSpec((N_EXPERTS, D_MODEL), const),
                  pl.BlockSpec((N_EXPERTS, LANES), const),
                  pl.BlockSpec((tm, tm), const)],
        out_specs=(pl.BlockSpec((tm * ROW_TILES, LANES), tok),
                   pl.BlockSpec((tm * ROW_TILES, LANES), tok),
                   pl.BlockSpec((16, tm), lambda b, j: (0, b * nt + j)),
                   pl.BlockSpec((tm * TOP_K, LANES), tok),
                   pl.BlockSpec((N_EXPERTS, LANES), tok),
                   pl.BlockSpec((N_EXPERTS, LANES), const)),
        scratch_shapes=[pltpu.VMEM((N_EXPERTS, LANES), F32)],
        compiler_params=_cparams(("arbitrary", "arbitrary")),
        name="outproj_router",
    )(a_ret, a_hg, w_bf, x, g1, sc2, sh2, nw, rwh, rwl, rb, tri)


def _dispatch_kernel(dest_hbm, v_ref, xs_hbm, idx_smem, zbuf, sem_idx, sem_z, sem_rows,
                     *, tile, n_tiles, n_rows):
    i = pl.program_id(0)
    slot = lax.rem(i, 2)

    def idx_copy(j, s):
        n_asg = tile * TOP_K
        return pltpu.make_async_copy(dest_hbm.at[j], idx_smem.at[pl.ds(s * n_asg, n_asg)],
                                     sem_idx.at[s])

    def zero_copy(j):
        r0 = (n_rows + j * SUB_ROWS) * ROW_TILES
        return pltpu.make_async_copy(zbuf, xs_hbm.at[pl.ds(r0, SUB_ROWS * ROW_TILES), :], sem_z)

    @pl.when(i == 0)
    def _():
        idx_copy(0, 0).start()
        zbuf[...] = jnp.zeros_like(zbuf)
        for j in range(ROW_BLOCK // SUB_ROWS):
            zero_copy(j).start()
        for j in range(ROW_BLOCK // SUB_ROWS):
            zero_copy(j).wait()

    idx_copy(i, slot).wait()

    @pl.when(i + 1 < n_tiles)
    def _():
        idx_copy(i + 1, 1 - slot).start()

    def issue(t, carry):
        e0 = slot * (tile * TOP_K) + t
        for k in range(TOP_K):
            d = pl.multiple_of(idx_smem[e0 + k * tile], ROW_TILES)
            src = v_ref.at[pl.ds(pl.multiple_of(t * ROW_TILES, ROW_TILES), ROW_TILES), :]
            pltpu.make_async_copy(src, xs_hbm.at[pl.ds(d, ROW_TILES), :],
                                  sem_rows).start(priority=k % 2)
        return carry
    lax.fori_loop(0, tile, issue, 0, unroll=8)
    for _ in range(TOP_K):
        pltpu.make_async_copy(v_ref, xs_hbm.at[pl.ds(0, tile * ROW_TILES), :], sem_rows).wait()


def _dispatch(dest_tiles, v, *, tile):
    T = v.shape[0] // ROW_TILES
    n_rows = T * TOP_K
    return pl.pallas_call(
        functools.partial(_dispatch_kernel, tile=tile, n_tiles=T // tile, n_rows=n_rows),
        out_shape=jax.ShapeDtypeStruct(((n_rows + ROW_BLOCK) * ROW_TILES, LANES), F32),
        grid=(T // tile,),
        in_specs=[pl.BlockSpec(memory_space=pl.ANY),
                  pl.BlockSpec((tile * ROW_TILES, LANES), lambda i: (i, 0))],
        out_specs=pl.BlockSpec(memory_space=pl.ANY),
        scratch_shapes=[pltpu.SMEM((2 * tile * TOP_K,), I32),
                        pltpu.VMEM((SUB_ROWS * ROW_TILES, LANES), F32),
                        pltpu.SemaphoreType.DMA((2,)),
                        pltpu.SemaphoreType.DMA,
                        pltpu.SemaphoreType.DMA],
        compiler_params=_cparams(("arbitrary",)),
        name="dispatch",
    )(dest_tiles, v)


def _moe_kernel(be_ref, ns_ref, nx_ref, xr_ref, x_hbm, w1_hbm, b1_ref, w2_hbm, b2_ref, y_ref,
                w1_stage, w2_stage, w1_scr, w2_scr, xbuf, x_scr, act_scr, sem_w, sem_x,
                *, n_blocks):
    i = pl.program_id(0)
    slot = lax.rem(i, 2)
    e = be_ref[i]
    nsub = ns_ref[i]
    changed = jnp.logical_or(i == 0, e != be_ref[jnp.maximum(i - 1, 0)])

    def x_copy(j, s):
        r0 = pl.multiple_of(xr_ref[j], ROW_TILES)
        return pltpu.make_async_copy(x_hbm.at[pl.ds(r0, ROW_BLOCK * ROW_TILES), :], xbuf.at[s],
                                     sem_x.at[s])

    @pl.when(i == 0)
    def _():
        x_copy(0, 0).start()

    @pl.when(nsub > 0)
    def _():
        x_copy(i, slot).wait()

    nxt_blk = jnp.minimum(i + 1, n_blocks - 1)

    @pl.when(jnp.logical_and(i + 1 < n_blocks, ns_ref[nxt_blk] > 0))
    def _():
        x_copy(nxt_blk, 1 - slot).start()

    def weight_copies(ex):
        return (pltpu.make_async_copy(w1_hbm.at[ex], w1_stage, sem_w.at[0]),
                pltpu.make_async_copy(w2_hbm.at[ex], w2_stage, sem_w.at[1]))

    @pl.when(i == 0)
    def _():
        for cp in weight_copies(e):
            cp.start()

    @pl.when(jnp.logical_and(changed, nsub > 0))
    def _():
        for cp in weight_copies(e):
            cp.wait()
        w1_scr[...] = w1_stage[...].astype(BF16)
        w2_scr[...] = w2_stage[...].astype(BF16)
        nxt = nx_ref[e]

        @pl.when(nxt >= 0)
        def _():
            for cp in weight_copies(nxt):
                cp.start()

    def compute(rows):
        x_scr[0:rows, :] = _load_tile_rows(xbuf.at[slot], rows).astype(BF16)
        cw = 256
        for c in range(D_FF // cw):
            glu = (_dot(x_scr[0:rows, :], w1_scr[:, c * cw:(c + 1) * cw])
                   + b1_ref[:, c * cw:(c + 1) * cw])
            lin = (_dot(x_scr[0:rows, :], w1_scr[:, D_FF + c * cw:D_FF + (c + 1) * cw])
                   + b1_ref[:, D_FF + c * cw:D_FF + (c + 1) * cw])
            glu = jnp.minimum(glu, SWIGLU_LIMIT)
            lin = jnp.clip(lin, -SWIGLU_LIMIT, SWIGLU_LIMIT)
            act = glu * jax.nn.sigmoid(SWIGLU_ALPHA * glu) * (lin + 1.0)
            act_scr[0:rows, c * cw:(c + 1) * cw] = act.astype(BF16)
        for c in range(D_MODEL // cw):
            y = (_dot(act_scr[0:rows, :], w2_scr[:, c * cw:(c + 1) * cw])
                 + b2_ref[:, c * cw:(c + 1) * cw])
            for s in range(cw // LANES):
                t = c * (cw // LANES) + s
                y_ref[pl.ds(t, rows, stride=ROW_TILES), :] = y[:, s * LANES:(s + 1) * LANES]
        if rows < ROW_BLOCK:
            y_ref[rows * ROW_TILES:, :] = jnp.zeros(((ROW_BLOCK - rows) * ROW_TILES, LANES), F32)

    for m in range(1, ROW_BLOCK // SUB_ROWS + 1):
        @pl.when(nsub == m)
        def _(m=m):
            compute(m * SUB_ROWS)

    @pl.when(nsub == 0)
    def _():
        y_ref[...] = jnp.zeros_like(y_ref)


def _moe(block_e, nsub, next_e, xrow, xs, w1, b1, w2, b2, *, n_blocks):
    return pl.pallas_call(
        functools.partial(_moe_kernel, n_blocks=n_blocks),
        out_shape=jax.ShapeDtypeStruct((n_blocks * ROW_BLOCK * ROW_TILES, LANES), F32),
        grid_spec=pltpu.PrefetchScalarGridSpec(
            num_scalar_prefetch=4,
            grid=(n_blocks,),
            in_specs=[pl.BlockSpec(memory_space=pl.ANY),
                      pl.BlockSpec(memory_space=pl.ANY),
                      pl.BlockSpec((None, 1, 2 * D_FF), lambda i, be, ns, nx, xr: (be[i], 0, 0)),
                      pl.BlockSpec(memory_space=pl.ANY),
                      pl.BlockSpec((None, 1, D_MODEL), lambda i, be, ns, nx, xr: (be[i], 0, 0))],
            out_specs=pl.BlockSpec((ROW_BLOCK * ROW_TILES, LANES),
                                   lambda i, be, ns, nx, xr: (i, 0)),
            scratch_shapes=[pltpu.VMEM((D_MODEL, 2 * D_FF), F32),
                            pltpu.VMEM((D_FF, D_MODEL), F32),
                            pltpu.VMEM((D_MODEL, 2 * D_FF), BF16),
                            pltpu.VMEM((D_FF, D_MODEL), BF16),
                            pltpu.VMEM((2, ROW_BLOCK * ROW_TILES, LANES), F32),
                            pltpu.VMEM((ROW_BLOCK, D_MODEL), BF16),
                            pltpu.VMEM((ROW_BLOCK, D_FF), BF16),
                            pltpu.SemaphoreType.DMA((2,)),
                            pltpu.SemaphoreType.DMA((2,))]),
        compiler_params=_cparams(("arbitrary",)),
        name="moe_ffn",
    )(block_e, nsub, next_e, xrow, xs, w1, b1, w2, b2)


def _combine_kernel(pos_hbm, src_hbm, yb_hbm, gate_ref, h1_ref, g2_ref, nw_ref, o_ref,
                    pos_smem, src_smem, win, hbuf, sem_tab, sem_win, *, n_tiles):
    i = pl.program_id(0)
    slot = lax.rem(i, 2)
    tile = COMBINE_TILE
    n_asg = tile * TOP_K
    chunk = WIN_ROWS * ROW_TILES
    slot_rows = WIN_CHUNKS * chunk

    def table_copies(j, s):
        return (pltpu.make_async_copy(pos_hbm.at[j], pos_smem.at[pl.ds(s * n_asg, n_asg)],
                                      sem_tab.at[s]),
                pltpu.make_async_copy(src_hbm.at[j], src_smem.at[pl.ds(s * n_asg, n_asg)],
                                      sem_tab.at[s]))

    def fetch_window(s):
        def one(c, carry):
            src = pl.multiple_of(src_smem[s * n_asg + c], ROW_TILES)
            dst = pl.multiple_of(s * slot_rows + c * chunk, chunk)
            pltpu.make_async_copy(yb_hbm.at[pl.ds(src, chunk), :], win.at[pl.ds(dst, chunk), :],
                                  sem_win.at[s]).start()
            return carry
        lax.fori_loop(0, WIN_CHUNKS, one, 0, unroll=4)

    @pl.when(i == 0)
    def _():
        for cp in table_copies(0, 0):
            cp.start()
        for cp in table_copies(0, 0):
            cp.wait()
        fetch_window(0)
        if n_tiles > 1:
            for cp in table_copies(1, 1):
                cp.start()

    @pl.when(i + 1 < n_tiles)
    def _():
        for cp in table_copies(i + 1, 1 - slot):
            cp.wait()
        fetch_window(1 - slot)

    pltpu.make_async_copy(yb_hbm.at[pl.ds(0, slot_rows), :],
                          win.at[pl.ds(pl.multiple_of(slot * slot_rows, slot_rows), slot_rows), :],
                          sem_win.at[slot]).wait()

    g2 = g2_ref[...]
    base = slot * n_asg

    def token(t, carry):
        e0 = base + t * TOP_K
        acc = None
        for k in range(TOP_K):
            p = pl.multiple_of(pos_smem[e0 + k], ROW_TILES)
            term = gate_ref[pl.ds(t * TOP_K + k, 1), :] * win[pl.ds(p, ROW_TILES), :]
            acc = term if acc is None else acc + term
        r0 = pl.multiple_of(t * ROW_TILES, ROW_TILES)
        hbuf[pl.ds(r0, ROW_TILES), :] = h1_ref[pl.ds(r0, ROW_TILES), :] + g2 * acc
        return carry
    lax.fori_loop(0, tile, token, 0, unroll=8)

    @pl.when(i + 2 < n_tiles)
    def _():
        for cp in table_copies(i + 2, slot):
            cp.start()

    h = _load_tile_rows(hbuf, tile)
    ms = jnp.mean(h * h, axis=-1, keepdims=True)
    o_ref[...] = h * lax.rsqrt(ms + EPS) * nw_ref[...]


def _combine(pos_tiles, src_tiles, yb, gate_rows, h1, g2_tiles, nw, *, tiles_per_batch):
    tile = COMBINE_TILE
    T = h1.shape[0] // ROW_TILES
    n_asg = tile * TOP_K
    return pl.pallas_call(
        functools.partial(_combine_kernel, n_tiles=T // tile),
        out_shape=jax.ShapeDtypeStruct((T, D_MODEL), F32),
        grid=(T // tile,),
        in_specs=[pl.BlockSpec(memory_space=pl.ANY),
                  pl.BlockSpec(memory_space=pl.ANY),
                  pl.BlockSpec(memory_space=pl.ANY),
                  pl.BlockSpec((tile * TOP_K, LANES), lambda i: (i, 0)),
                  pl.BlockSpec((tile * ROW_TILES, LANES), lambda i: (i, 0)),
                  pl.BlockSpec((None, ROW_TILES, LANES), lambda i: (i // tiles_per_batch, 0, 0)),
                  pl.BlockSpec((1, D_MODEL), lambda i: (0, 0))],
        out_specs=pl.BlockSpec((tile, D_MODEL), lambda i: (i, 0)),
        scratch_shapes=[pltpu.SMEM((2 * n_asg,), I32),
                        pltpu.SMEM((2 * n_asg,), I32),
                        pltpu.VMEM((2 * WIN_CHUNKS * WIN_ROWS * ROW_TILES, LANES), F32),
                        pltpu.VMEM((tile * ROW_TILES, LANES), F32),
                        pltpu.SemaphoreType.DMA((2,)),
                        pltpu.SemaphoreType.DMA((2,))],
        compiler_params=_cparams(("arbitrary",)),
        name="combine_norm",
    )(pos_tiles, src_tiles, yb, gate_rows, h1, g2_tiles, nw)


def _rope_tables(n):
    rows = n // GRID_W
    row = jnp.repeat(jnp.arange(rows, dtype=F32), GRID_W)
    col = jnp.tile(jnp.arange(GRID_W, dtype=F32), rows)
    n_freq = RET_DK // 4
    inv = ROPE_BASE ** (-jnp.arange(n_freq, dtype=F32) / n_freq)
    ang = jnp.concatenate([row[:, None] * inv, col[:, None] * inv], axis=-1)
    cos, sin = jnp.cos(ang), jnp.sin(ang)
    cos_h = jnp.concatenate([cos, cos], axis=-1)
    sin_h = jnp.concatenate([-sin, sin], axis=-1)
    return jnp.tile(cos_h, (1, RET_HEADS)), jnp.tile(sin_h, (1, RET_HEADS))


def kernel(x, c, ctx, c_ctx, w_ada, b_ada, norm_mix_w, norm_ffn_w, w_in, w_out, hg_lb,
           hg_norm_w, router_w, router_b, w1, b1, w2, b2, norm_final_w):
    B, N, D = x.shape
    C = ctx.shape[1]
    T = B * N
    assert D == D_MODEL and w_ada.shape[0] == 1

    cc = jnp.concatenate([c.astype(F32), c_ctx.astype(F32)[None, :],
                          jnp.zeros((16 - B - 1, D), F32)], axis=0)
    mod = _ada(cc, w_ada[0], b_ada[0][None, :])
    mod = mod.reshape(16, 6, 1, D).transpose(1, 0, 2, 3)
    sh1, sc1, g1, sh2, sc2, g2 = (mod[i] for i in range(6))

    w_in_bf = w_in[0].astype(BF16)
    w_out_bf = w_out[0].astype(BF16)
    nw_mix = norm_mix_w[0][None, :]
    cos_t, sin_t = _rope_tables(N)

    pb, lf = _inproj(x, sc1, sh1, None, nw_mix, w_in_bf, hg_lb[:2], cos_t, sin_t,
                     rope=True, tm=INPROJ_TILE)
    pbc, lfc = _inproj(ctx, sc1, sh1, B, nw_mix, w_in_bf, hg_lb[:2], cos_t[:C], sin_t[:C],
                       rope=False, tm=C)

    a_ret = _retention(pb, pbc)
    a_hg = _hgrn(pb, lf, pbc, lfc, hg_norm_w[0][None, :])

    rw_t = router_w[0].T
    rwh = rw_t.astype(BF16)
    rwl = (rw_t - rwh.astype(F32)).astype(BF16)
    rb = jnp.broadcast_to(router_b[0][:, None], (N_EXPERTS, LANES))
    h1, v, route_t, gate_rows, base_cnt, cnt = _outproj(a_ret, a_hg, w_out_bf, x, g1, sc2, sh2,
                                                        norm_ffn_w[0][None, :], rwh, rwl, rb,
                                                        tm=COMBINE_TILE)

    idx = route_t[0:TOP_K].astype(I32)
    rank = route_t[2 * TOP_K:3 * TOP_K].astype(I32)
    counts = cnt[:, 0].astype(I32)
    padded = (counts + ROW_BLOCK - 1) // ROW_BLOCK * ROW_BLOCK
    pend = jnp.cumsum(padded)
    pstart = pend - padded
    cstart = jnp.cumsum(counts) - counts
    onehot = idx[None] == jnp.arange(N_EXPERTS, dtype=I32)[:, None, None]

    def dest_tiles(first_row, tile):
        dest = (jnp.sum(jnp.where(onehot, first_row[:, None, None], 0), axis=0) + rank) * ROW_TILES
        return dest.reshape(TOP_K, T // tile, tile).transpose(1, 0, 2).reshape(T // tile,
                                                                               TOP_K * tile)
    n_blocks = (T * TOP_K) // ROW_BLOCK + N_EXPERTS
    starts = jnp.arange(n_blocks, dtype=I32) * ROW_BLOCK
    block_e = jnp.minimum(jnp.sum((pend[None, :] <= starts[:, None]).astype(I32), axis=1),
                          N_EXPERTS - 1)
    valid = jnp.clip(counts[block_e] - (starts - pstart[block_e]), 0, ROW_BLOCK)
    nsub = (valid + SUB_ROWS - 1) // SUB_ROWS
    first_from = lax.cummin(jnp.where(counts > 0, jnp.arange(N_EXPERTS, dtype=I32), N_EXPERTS),
                            reverse=True)
    next_e = jnp.concatenate([first_from[1:], jnp.full((1,), N_EXPERTS, I32)])
    next_e = jnp.where(next_e == N_EXPERTS, -1, next_e)
    xrow = jnp.where(nsub > 0, cstart[block_e] + starts - pstart[block_e], 0) * ROW_TILES

    xs = _dispatch(dest_tiles(cstart, DISPATCH_TILE), v, tile=DISPATCH_TILE)
    yb = _moe(block_e, nsub, next_e, xrow.astype(I32), xs, w1[0], b1[0][:, None, :], w2[0],
              b2[0][:, None, :], n_blocks=n_blocks)

    tc = COMBINE_TILE
    n_ct = T // tc
    base = base_cnt.reshape(n_ct, N_EXPERTS, LANES)[:, :, 0].astype(I32)
    n_run = jnp.concatenate([base[1:], counts[None, :]], axis=0) - base
    n_chunk = (n_run + WIN_ROWS - 1) // WIN_ROWS
    chunk_end = jnp.cumsum(n_chunk, axis=1)
    chunk_start = chunk_end - n_chunk
    win_off = chunk_start * WIN_ROWS
    c_id = jnp.arange(WIN_CHUNKS, dtype=I32)
    c_exp = jnp.sum((chunk_end[:, None, :] <= c_id[None, :, None]).astype(I32), axis=2)
    c_real = c_exp < N_EXPERTS
    c_own = c_exp[:, :, None] == jnp.arange(N_EXPERTS, dtype=I32)
    take = lambda tab: jnp.sum(jnp.where(c_own, tab[:, None, :], 0), axis=2)
    c_src = (take(pstart[None, :] + base) + (c_id[None, :] - take(chunk_start)) * WIN_ROWS)
    src_tiles = jnp.concatenate(
        [jnp.where(c_real, c_src, 0) * ROW_TILES,
         jnp.zeros((n_ct, tc * TOP_K - WIN_CHUNKS), I32)], axis=1)
    slot_off = (jnp.arange(n_ct, dtype=I32) % 2) * (WIN_CHUNKS * WIN_ROWS)
    shift = jnp.repeat(win_off - base + slot_off[:, None], tc, axis=0).T
    pos = (jnp.sum(jnp.where(onehot, shift[:, None, :], 0), axis=0) + rank) * ROW_TILES

    def per_tile(a):
        return a.reshape(TOP_K, n_ct, tc).transpose(1, 2, 0).reshape(n_ct, tc * TOP_K)

    out = _combine(per_tile(pos), src_tiles, yb, gate_rows, h1,
                   g2.reshape(16, ROW_TILES, LANES), norm_final_w[None, :],
                   tiles_per_batch=N // tc)
    return out.reshape(B, N, D)
```

```python
import functools
import math

import jax
import jax.numpy as jnp
from jax import lax
from jax.experimental import pallas as pl
from jax.experimental.pallas import tpu as pltpu

F32 = jnp.float32
BF16 = jnp.bfloat16
I32 = jnp.int32

D_MODEL = 1024
GRID_W = 64
RET_HEADS = 4
RET_DK = 64
HG_HEADS = 4
PROJ_W = 4096
ROPE_BASE = 10000.0
EPS = 1e-6
N_EXPERTS = 32
TOP_K = 4
D_FF = 1024
SWIGLU_LIMIT = 7.0
SWIGLU_ALPHA = 1.702
LOG2_E = 1.0 / math.log(2.0)

LANES = 128
CHUNK = 128
INPROJ_TILE = 1024
DISPATCH_TILE = 512
ROW_BLOCK = 1024
SUB_ROWS = 256
COMBINE_TILE = 512
WIN_ROWS = 8
WIN_CHUNKS = (COMBINE_TILE * TOP_K + N_EXPERTS * (WIN_ROWS - 1) + WIN_ROWS - 1) // WIN_ROWS
VMEM_LIMIT = 56 * 1024 * 1024
ROW_TILES = D_MODEL // LANES

C_RQ, C_RK, C_RV, C_RG, C_HQ, C_FF, C_FB, C_HV, C_HG = (
    0, 256, 512, 1024, 1536, 2048, 2560, 3072, 3584)


def _cparams(sem):
    return pltpu.CompilerParams(dimension_semantics=sem, vmem_limit_bytes=VMEM_LIMIT)


def _split_bf16(x):
    hi = x.astype(BF16)
    lo = (x - hi.astype(F32)).astype(BF16)
    return hi, lo


def _dot(a, b):
    return jnp.dot(a, b, preferred_element_type=F32)


def _dot_nt(a, b):
    return lax.dot_general(a, b, (((1,), (1,)), ((), ())), preferred_element_type=F32)


def _dot3(a, b):
    ah, al = _split_bf16(a)
    bh, bl = _split_bf16(b)
    return _dot(ah, bh) + (_dot(ah, bl) + _dot(al, bh))


def _silu(x):
    return x * jax.nn.sigmoid(x)


def _load_tile_rows(ref, n):
    return jnp.concatenate([ref[pl.ds(s, n, stride=ROW_TILES), :] for s in range(ROW_TILES)],
                           axis=1)


def _store_tile_rows(ref, x):
    n = x.shape[0]
    for s in range(ROW_TILES):
        ref[pl.ds(s, n, stride=ROW_TILES), :] = x[:, s * LANES:(s + 1) * LANES]


def _ada_kernel(c_ref, w_ref, b_ref, o_ref):
    s = _silu(c_ref[...])
    o_ref[...] = _dot3(s, w_ref[...]) + b_ref[...]


def _ada(cc, w, b):
    nblk = w.shape[1] // D_MODEL
    return pl.pallas_call(
        _ada_kernel,
        out_shape=jax.ShapeDtypeStruct((cc.shape[0], w.shape[1]), F32),
        grid=(nblk,),
        in_specs=[pl.BlockSpec(cc.shape, lambda j: (0, 0)),
                  pl.BlockSpec((D_MODEL, D_MODEL), lambda j: (0, j)),
                  pl.BlockSpec((1, D_MODEL), lambda j: (0, j))],
        out_specs=pl.BlockSpec((cc.shape[0], D_MODEL), lambda j: (0, j)),
        compiler_params=_cparams(("arbitrary",)),
        name="ada",
    )(cc, w, b)


def _inproj_kernel(x_ref, sc_ref, sh_ref, nw_ref, w_ref, lb_ref, cos_ref, sin_ref,
                   pb_ref, lf_ref, u_scr, *, rope):
    x = x_ref[...]
    ms = jnp.mean(x * x, axis=-1, keepdims=True)
    u = x * lax.rsqrt(ms + EPS) * nw_ref[...] * (1.0 + sc_ref[...]) + sh_ref[...]
    u_scr[...] = u.astype(BF16)

    def proj(lo, width):
        return _dot(u_scr[...], w_ref[:, lo:lo + width])

    tm = x.shape[0]
    if rope:
        lane = lax.broadcasted_iota(I32, (tm, LANES), 1)
        first = (lane & 32) == 0

    def put_rot(col, scale):
        for j in range(2):
            lo = col + j * LANES
            t = proj(lo, LANES)
            if scale != 1.0:
                t = t * scale
            if rope:
                tb = j * LANES
                rot = jnp.where(first, pltpu.roll(t, 96, axis=1), pltpu.roll(t, 32, axis=1))
                t = t * cos_ref[:, tb:tb + LANES] + rot * sin_ref[:, tb:tb + LANES]
            pb_ref[:, lo:lo + LANES] = t.astype(BF16)

    put_rot(C_RQ, 1.0)
    put_rot(C_RK, RET_DK ** -0.5)
    pb_ref[:, C_RV:C_RV + 512] = proj(C_RV, 512).astype(BF16)
    pb_ref[:, C_RG:C_RG + 512] = _silu(proj(C_RG, 512)).astype(BF16)
    pb_ref[:, C_HQ:C_HQ + 512] = _silu(proj(C_HQ, 512)).astype(BF16)
    pb_ref[:, C_HV:C_HV + 512] = proj(C_HV, 512).astype(BF16)
    pb_ref[:, C_HG:C_HG + 512] = _silu(proj(C_HG, 512)).astype(BF16)

    la = lb_ref[0]
    lbb = lb_ref[1]
    mx = jnp.maximum(la, lbb)
    ea = jnp.exp(la - mx)
    eb = jnp.exp(lbb - mx)
    lb = ea / (ea + eb)
    for d, col in enumerate((C_FF, C_FB)):
        lbd = lb[d:d + 1, :]
        f = lbd + (1.0 - lbd) * jax.nn.sigmoid(proj(col, 512))
        pb_ref[:, col:col + 512] = (1.0 - f).astype(BF16)
        lf_ref[:, d * 512:(d + 1) * 512] = jnp.log(f) * LOG2_E


def _inproj(x, sc, sh, mod_row, nw, w_bf, hg_lb, cos_t, sin_t, *, rope, tm):
    B, n, _ = x.shape
    nt = n // tm
    if mod_row is None:
        mrow = lambda b, j: (b, 0, 0)
    else:
        mrow = lambda b, j: (mod_row, 0, 0)
    return pl.pallas_call(
        functools.partial(_inproj_kernel, rope=rope),
        out_shape=(jax.ShapeDtypeStruct((B, n, PROJ_W), BF16),
                   jax.ShapeDtypeStruct((B, n, 1024), F32)),
        grid=(B, nt),
        in_specs=[pl.BlockSpec((None, tm, D_MODEL), lambda b, j: (b, j, 0)),
                  pl.BlockSpec((None, 1, D_MODEL), mrow),
                  pl.BlockSpec((None, 1, D_MODEL), mrow),
                  pl.BlockSpec((1, D_MODEL), lambda b, j: (0, 0)),
                  pl.BlockSpec((D_MODEL, PROJ_W), lambda b, j: (0, 0),
                               pipeline_mode=pl.Buffered(1)),
                  pl.BlockSpec((2, 2, 512), lambda b, j: (0, 0, 0)),
                  pl.BlockSpec((tm, 256), lambda b, j: (j, 0)),
                  pl.BlockSpec((tm, 256), lambda b, j: (j, 0))],
        out_specs=(pl.BlockSpec((None, tm, PROJ_W), lambda b, j: (b, j, 0)),
                   pl.BlockSpec((None, tm, 1024), lambda b, j: (b, j, 0))),
        scratch_shapes=[pltpu.VMEM((tm, D_MODEL), BF16)],
        compiler_params=_cparams(("arbitrary", "arbitrary")),
        name="inproj_rope" if rope else "inproj_ctx",
    )(x, sc, sh, nw, w_bf, hg_lb, cos_t, sin_t)


_RET_LGF = [math.log1p(-(2.0 ** (-5.0 - 2.0 * h))) for h in range(RET_HEADS)]
_RET_LGB = [math.log1p(-(2.0 ** (-6.0 - 2.0 * h))) for h in range(RET_HEADS)]


def _ret_kernel(q_ref, k_ref, v_ref, g_ref, kc_ref, vc_ref, o_ref,
                u_scr, sin_scr, kt_scr, dtot_scr, *, n_lat, n_ctx):
    L = CHUNK
    pair = pl.program_id(1)
    row = lax.broadcasted_iota(I32, (L, L), 0).astype(F32)
    col = lax.broadcasted_iota(I32, (L, L), 1).astype(F32)
    lane = lax.broadcasted_iota(I32, (L, LANES), 1)
    trow = lax.broadcasted_iota(I32, (L, 1), 0).astype(F32)
    tcol = lax.broadcasted_iota(I32, (1, L), 1).astype(F32)
    low_half = lane < RET_DK

    def u_chunk(k_blk, v_blk, ci, store_kt):
        kt = k_blk.astype(F32).T
        if store_kt is not None:
            kt_scr[store_kt] = kt.astype(BF16)
        for hh in range(2):
            lgf = jnp.where(pair == 0, _RET_LGF[hh], _RET_LGF[2 + hh])
            lgb = jnp.where(pair == 0, _RET_LGB[hh], _RET_LGB[2 + hh])
            kth = kt[hh * RET_DK:(hh + 1) * RET_DK, :]
            wkf = jnp.exp(lgf * (L - 1.0 - tcol))
            wkb = jnp.exp(lgb * tcol)
            lhs = jnp.concatenate([kth * wkf, kth * wkb], axis=0).astype(BF16)
            u_scr[hh, ci] = _dot(lhs, v_blk[:, hh * LANES:(hh + 1) * LANES])

    for c in range(n_ctx):
        u_chunk(kc_ref[c * L:(c + 1) * L, :], vc_ref[c * L:(c + 1) * L, :], c, None)

    def lat_u(c, carry):
        r0 = pl.multiple_of(c * L, L)
        u_chunk(k_ref[pl.ds(r0, L), :], v_ref[pl.ds(r0, L), :], n_ctx + c, c)
        return carry
    lax.fori_loop(0, n_lat, lat_u, 0, unroll=16)

    ones = jnp.ones((RET_DK, LANES), F32)
    for hh in range(2):
        lgf = jnp.where(pair == 0, _RET_LGF[hh], _RET_LGF[2 + hh])
        lgb = jnp.where(pair == 0, _RET_LGB[hh], _RET_LGB[2 + hh])
        d = row - col
        dtot_scr[hh] = jnp.where(d > 0, jnp.exp(lgf * jnp.maximum(d, 0.0)),
                                 jnp.where(d < 0, jnp.exp(lgb * jnp.maximum(-d, 0.0)), 2.0))
        af = jnp.exp(ones * (lgf * L))
        ab = jnp.exp(ones * (lgb * L))

        s = jnp.zeros((RET_DK, LANES), F32)
        for c in range(n_ctx):
            s = af * s + u_scr[hh, c, 0:RET_DK, :]
        sb = jnp.zeros((RET_DK, LANES), F32)
        for c in reversed(range(n_ctx)):
            sb = ab * sb + u_scr[hh, c, RET_DK:2 * RET_DK, :]

        def fwd(c, s, hh=hh, af=af):
            sin_scr[hh, c, 0:RET_DK, :] = s.astype(BF16)
            return af * s + u_scr[hh, n_ctx + c, 0:RET_DK, :]
        lax.fori_loop(0, n_lat, fwd, s)

        def bwd(i, sb, hh=hh, ab=ab):
            c = n_lat - 1 - i
            sin_scr[hh, c, RET_DK:2 * RET_DK, :] = sb.astype(BF16)
            return ab * sb + u_scr[hh, n_ctx + c, RET_DK:2 * RET_DK, :]
        lax.fori_loop(0, n_lat, bwd, sb)

    def out_chunk(c, carry):
        r0 = pl.multiple_of(c * L, L)
        q = q_ref[pl.ds(r0, L), :].astype(F32)
        qr = pltpu.roll(q, RET_DK, axis=1)
        kt = kt_scr[c]
        for hh in range(2):
            lgf = jnp.where(pair == 0, _RET_LGF[hh], _RET_LGF[2 + hh])
            lgb = jnp.where(pair == 0, _RET_LGB[hh], _RET_LGB[2 + hh])
            mine = low_half if hh == 0 else jnp.logical_not(low_half)
            qm = jnp.where(mine, q, 0.0).astype(BF16)
            p = (_dot(qm, kt) * dtot_scr[hh]).astype(BF16)
            vh = v_ref[pl.ds(r0, L), hh * LANES:(hh + 1) * LANES]
            wqf = jnp.exp(lgf * (trow + 1.0))
            wqb = jnp.exp(lgb * (L - trow))
            qa, qb = (q, qr) if hh == 0 else (qr, q)
            qs = jnp.where(low_half, qa * wqf, qb * wqb).astype(BF16)
            o = _dot(p, vh) + _dot(qs, sin_scr[hh, c])
            ms = jnp.mean(o * o, axis=-1, keepdims=True)
            gh = g_ref[pl.ds(r0, L), hh * LANES:(hh + 1) * LANES].astype(F32)
            o_ref[pl.ds(r0, L), hh * LANES:(hh + 1) * LANES] = (
                o * lax.rsqrt(ms + EPS) * gh).astype(BF16)
        return carry
    lax.fori_loop(0, n_lat, out_chunk, 0, unroll=16)


def _retention(pb, pbc):
    B, n, _ = pb.shape
    nc = pbc.shape[1]
    n_lat, n_ctx = n // CHUNK, nc // CHUNK
    return pl.pallas_call(
        functools.partial(_ret_kernel, n_lat=n_lat, n_ctx=n_ctx),
        out_shape=jax.ShapeDtypeStruct((B, n, 512), BF16),
        grid=(B, 2),
        in_specs=[pl.BlockSpec((None, n, LANES), lambda b, p: (b, 0, C_RQ // LANES + p)),
                  pl.BlockSpec((None, n, LANES), lambda b, p: (b, 0, C_RK // LANES + p)),
                  pl.BlockSpec((None, n, 256), lambda b, p: (b, 0, C_RV // 256 + p)),
                  pl.BlockSpec((None, n, 256), lambda b, p: (b, 0, C_RG // 256 + p)),
                  pl.BlockSpec((None, nc, LANES), lambda b, p: (b, 0, C_RK // LANES + p)),
                  pl.BlockSpec((None, nc, 256), lambda b, p: (b, 0, C_RV // 256 + p))],
        out_specs=pl.BlockSpec((None, n, 256), lambda b, p: (b, 0, p)),
        scratch_shapes=[pltpu.VMEM((2, n_lat + n_ctx, CHUNK, LANES), F32),
                        pltpu.VMEM((2, n_lat, CHUNK, LANES), BF16),
                        pltpu.VMEM((n_lat, LANES, CHUNK), BF16),
                        pltpu.VMEM((2, CHUNK, CHUNK), F32)],
        compiler_params=_cparams(("arbitrary", "arbitrary")),
        name="retention",
    )(pb, pb, pb, pb, pbc, pbc)


_LEVELS = (64, 32, 16, 8, 4, 2, 1)


def _expand_rows(r, rep):
    n = r.shape[0]
    if n == 1:
        return jnp.broadcast_to(r, (rep, r.shape[1]))
    return jnp.concatenate(
        [jnp.broadcast_to(r[i:i + 1, :], (rep, r.shape[1])) for i in range(n)], axis=0)


def _hgrn_kernel(q_ref, kf_ref, kb_ref, v_ref, g_ref, lff_ref, lfb_ref,
                 kfc_ref, kbc_ref, vc_ref, lffc_ref, lfbc_ref, nw_ref, o_ref,
                 ut_scr, a_scr, qs_scr, oi_scr, sin_scr, bfb_scr, *, n_lat, n_ctx):
    L = CHUNK
    row = lax.broadcasted_iota(I32, (L, L), 0)
    col = lax.broadcasted_iota(I32, (L, L), 1)
    xr_bits = lax.bitcast_convert_type((row ^ col).astype(F32), I32)
    lv = lax.shift_right_logical(xr_bits, 23) - 127
    row2 = lax.broadcasted_iota(I32, (L, 2 * L), 0)
    col2 = lax.broadcasted_iota(I32, (L, 2 * L), 1) & (L - 1)
    tril2 = jnp.where(col2 <= row2, 1.0, 0.0).astype(BF16)
    triu2 = jnp.where(col2 >= row2, 1.0, 0.0).astype(BF16)

    def cums(lff, lfb):
        hf, lof = _split_bf16(lff)
        hb, lob = _split_bf16(lfb)
        bf = _dot(tril2, jnp.concatenate([hf, lof], axis=0))
        bb = _dot(triu2, jnp.concatenate([hb, lob], axis=0))
        return bf, bb

    def state_part(ci, kf, kb, v_blk, bf, bb):
        endf = bf[L - 1:L, :]
        endb = bb[0:1, :]
        ksf = kf * jnp.exp2(endf - bf)
        ksb = kb * jnp.exp2(endb - bb)
        vt = v_blk.astype(F32).T.astype(BF16)
        ut_scr[ci] = _dot(vt, jnp.concatenate([ksf, ksb], axis=1).astype(BF16))
        a_scr[ci] = jnp.broadcast_to(
            jnp.concatenate([jnp.exp2(endf), jnp.exp2(endb)], axis=1), (8, 2 * LANES))

    for c in range(n_ctx):
        sl = slice(c * L, (c + 1) * L)
        bf, bb = cums(lffc_ref[sl, :], lfbc_ref[sl, :])
        state_part(c, kfc_ref[sl, :].astype(F32), kbc_ref[sl, :].astype(F32),
                   vc_ref[sl, :], bf, bb)

    def lat_chunk(c, u):
        rows = pl.ds(pl.multiple_of(c * L, L), L)
        bf_scr = bfb_scr.at[u, 0]
        bb_scr = bfb_scr.at[u, 1]
        q = q_ref[rows, :].astype(F32)
        kf = kf_ref[rows, :].astype(F32)
        kb = kb_ref[rows, :].astype(F32)
        v_blk = v_ref[rows, :]
        lff = lff_ref[rows, :]
        lfb = lfb_ref[rows, :]
        bf, bb = cums(lff, lfb)
        state_part(n_ctx + c, kf, kb, v_blk, bf, bb)
        qs_scr[rows, :] = jnp.concatenate([q * jnp.exp2(bf), q * jnp.exp2(bb)],
                                          axis=1).astype(BF16)
        bf_scr[...] = bf
        bb_scr[...] = bb

        acc = jnp.zeros((L, L), F32)
        for lvl, h in enumerate(_LEVELS):
            bit = (row & h) != 0
            ksel = jnp.where(bit, kb, kf)
            if h >= 4:
                n = (L // 2) // h
                if n == 1:
                    rf = bf_scr[h - 1:h, :]
                    rb = bb_scr[h:h + 1, :]
                else:
                    rf = bf_scr[pl.ds(h - 1, n, stride=2 * h), :]
                    rb = bb_scr[pl.ds(h, n, stride=2 * h), :]
                df = bf - _expand_rows(rf, 2 * h)
                db = bb - _expand_rows(rb, 2 * h)
                eq = jnp.where(bit, df, db)
                ek = -jnp.where(bit, db, df)
            elif h == 2:
                m = row & 3
                lff_n = pltpu.roll(lff, L - 1, axis=0)
                lfb_n = pltpu.roll(lfb, L - 1, axis=0)
                eq = jnp.where(m == 2, lff,
                               jnp.where(m == 3, lff + pltpu.roll(lff, 1, axis=0),
                                         jnp.where(m == 0, lfb + lfb_n, lfb)))
                ek = jnp.where(m == 3, pltpu.roll(lfb, 1, axis=0),
                               jnp.where(m == 0, lff_n, 0.0))
            else:
                eq = jnp.where(bit, lff, lfb)
                ek = None
            lhs = (q * jnp.exp2(eq)).astype(BF16)
            rhs = (ksel if ek is None else ksel * jnp.exp2(ek)).astype(BF16)
            acc = jnp.where(lv == 6 - lvl, _dot_nt(lhs, rhs), acc)

        dsum = jnp.sum(q * (kf + kb), axis=-1, keepdims=True)
        oi_scr[rows, :] = _dot(acc.astype(BF16), v_blk) + dsum * v_blk.astype(F32)

    def lat_pair(i, carry):
        for u in range(16):
            lat_chunk(16 * i + u, u)
        return carry
    lax.fori_loop(0, n_lat // 16, lat_pair, 0)

    st = jnp.zeros((LANES, LANES), F32)
    for c in range(n_ctx):
        st = st * a_scr[c, 0:1, 0:LANES] + ut_scr[c, :, 0:LANES]
    stb = jnp.zeros((LANES, LANES), F32)
    for c in reversed(range(n_ctx)):
        stb = stb * a_scr[c, 0:1, LANES:2 * LANES] + ut_scr[c, :, LANES:2 * LANES]

    def fwd(c, st):
        sin_scr[c, :, 0:LANES] = st.astype(BF16)
        ci = n_ctx + c
        return st * a_scr[ci, 0:1, 0:LANES] + ut_scr[ci, :, 0:LANES]
    lax.fori_loop(0, n_lat, fwd, st)

    def bwd(i, stb):
        c = n_lat - 1 - i
        sin_scr[c, :, LANES:2 * LANES] = stb.astype(BF16)
        ci = n_ctx + c
        return stb * a_scr[ci, 0:1, LANES:2 * LANES] + ut_scr[ci, :, LANES:2 * LANES]
    lax.fori_loop(0, n_lat, bwd, stb)

    def out_chunk(c, carry):
        rows = pl.ds(pl.multiple_of(c * L, L), L)
        o = oi_scr[rows, :] + _dot_nt(qs_scr[rows, :], sin_scr[c])
        ms = jnp.mean(o * o, axis=-1, keepdims=True)
        y = o * lax.rsqrt(ms + EPS) * nw_ref[...] * g_ref[rows, :].astype(F32)
        o_ref[rows, :] = y.astype(BF16)
        return carry
    lax.fori_loop(0, n_lat, out_chunk, 0, unroll=16)


def _hgrn(pb, lf, pbc, lfc, nw):
    B, n, _ = pb.shape
    nc = pbc.shape[1]
    n_lat, n_ctx = n // CHUNK, nc // CHUNK

    def colblk(rows, col0):
        return pl.BlockSpec((None, rows, LANES), lambda b, h: (b, 0, col0 // LANES + h))

    return pl.pallas_call(
        functools.partial(_hgrn_kernel, n_lat=n_lat, n_ctx=n_ctx),
        out_shape=jax.ShapeDtypeStruct((B, n, 512), BF16),
        grid=(B, HG_HEADS),
        in_specs=[colblk(n, C_HQ), colblk(n, C_FF), colblk(n, C_FB), colblk(n, C_HV),
                  colblk(n, C_HG), colblk(n, 0), colblk(n, 512),
                  colblk(nc, C_FF), colblk(nc, C_FB), colblk(nc, C_HV),
                  colblk(nc, 0), colblk(nc, 512),
                  pl.BlockSpec((1, LANES), lambda b, h: (0, 0))],
        out_specs=pl.BlockSpec((None, n, LANES), lambda b, h: (b, 0, h)),
        scratch_shapes=[pltpu.VMEM((n_lat + n_ctx, LANES, 2 * LANES), F32),
                        pltpu.VMEM((n_lat + n_ctx, 8, 2 * LANES), F32),
                        pltpu.VMEM((n, 2 * LANES), BF16),
                        pltpu.VMEM((n, LANES), F32),
                        pltpu.VMEM((n_lat, LANES, 2 * LANES), BF16),
                        pltpu.VMEM((16, 2, CHUNK, LANES), F32)],
        compiler_params=_cparams(("arbitrary", "arbitrary")),
        name="hgrn2",
    )(pb, pb, pb, pb, pb, lf, lf, pbc, pbc, pbc, lfc, lfc, nw)


def _outproj_kernel(ar_ref, ah_ref, w_ref, x_ref, g1_ref, sc_ref, sh_ref, nw_ref,
                    rwh_ref, rwl_ref, rb_ref, tri_ref, h1_ref, v_ref, route_t_ref, gate_ref, base_ref,
                    cnt_ref, cnt_scr):
    first_step = jnp.logical_and(pl.program_id(0) == 0, pl.program_id(1) == 0)

    @pl.when(first_step)
    def _():
        cnt_scr[...] = jnp.zeros_like(cnt_scr)

    y = _dot(ar_ref[...], w_ref[0:512, :]) + _dot(ah_ref[...], w_ref[512:1024, :])
    h1 = x_ref[...] + g1_ref[...] * y
    _store_tile_rows(h1_ref, h1)
    ms = jnp.mean(h1 * h1, axis=-1, keepdims=True)
    v = h1 * lax.rsqrt(ms + EPS) * nw_ref[...] * (1.0 + sc_ref[...]) + sh_ref[...]
    _store_tile_rows(v_ref, v)

    tm = v.shape[0]
    vh, vl = _split_bf16(v)
    rwh, rwl = rwh_ref[...], rwl_ref[...]
    l = (_dot_nt(rwh, vh) + (_dot_nt(rwh, vl) + _dot_nt(rwl, vh))) + rb_ref[:, 0:1]
    row_f = lax.broadcasted_iota(I32, (N_EXPERTS, tm), 0).astype(F32)
    sels, tops, idxs = [], [], []
    for _ in range(TOP_K):
        m = jnp.max(l, axis=0, keepdims=True)
        i = jnp.min(jnp.where(l == m, row_f, float(N_EXPERTS)), axis=0, keepdims=True)
        sel = row_f == i
        l = jnp.where(sel, -jnp.inf, l)
        sels.append(sel)
        tops.append(m)
        idxs.append(i)
    es = [jnp.exp(t - tops[0]) for t in tops]
    den = es[0] + es[1] + es[2] + es[3]
    gates = [e / den for e in es]

    oh = jnp.zeros((N_EXPERTS, tm), F32)
    for sel in sels:
        oh = jnp.where(sel, 1.0, oh)
    cnt = cnt_scr[:, 0:1]
    before = _dot(oh.astype(BF16), tri_ref[...]) + cnt
    ranks = [jnp.sum(jnp.where(sel, before, 0.0), axis=0, keepdims=True) for sel in sels]
    base_ref[...] = cnt_scr[...]
    cnt_scr[...] = cnt_scr[...] + jnp.sum(oh, axis=1, keepdims=True)
    cnt_ref[...] = cnt_scr[...]

    zero4 = jnp.zeros((TOP_K, tm), F32)
    route_t_ref[...] = jnp.concatenate(idxs + [zero4] + ranks + [zero4], axis=0)
    g_cols = jnp.concatenate(gates + [zero4], axis=0).T
    for k in range(TOP_K):
        gate_ref[pl.ds(k, tm, stride=TOP_K), :] = jnp.broadcast_to(g_cols[:, k:k + 1], (tm, LANES))


def _outproj(a_ret, a_hg, w_bf, x, g1, sc2, sh2, nw, rwh, rwl, rb, *, tm):
    B, n, _ = x.shape
    nt = n // tm
    T = B * n
    assert tm == COMBINE_TILE
    mrow = lambda b, j: (b, 0, 0)
    tok = lambda b, j: (b * nt + j, 0)
    const = lambda b, j: (0, 0)
    tri = jnp.triu(jnp.ones((tm, tm), BF16), 1)
    return pl.pallas_call(
        _outproj_kernel,
        out_shape=(jax.ShapeDtypeStruct((T * ROW_TILES, LANES), F32),
                   jax.ShapeDtypeStruct((T * ROW_TILES, LANES), F32),
                   jax.ShapeDtypeStruct((16, T), F32),
                   jax.ShapeDtypeStruct((T * TOP_K, LANES), F32),
                   jax.ShapeDtypeStruct((B * nt * N_EXPERTS, LANES), F32),
                   jax.ShapeDtypeStruct((N_EXPERTS, LANES), F32)),
        grid=(B, nt),
        in_specs=[pl.BlockSpec((None, tm, 512), lambda b, j: (b, j, 0)),
                  pl.BlockSpec((None, tm, 512), lambda b, j: (b, j, 0)),
                  pl.BlockSpec((D_MODEL, D_MODEL), const),
                  pl.BlockSpec((None, tm, D_MODEL), lambda b, j: (b, j, 0)),
                  pl.BlockSpec((None, 1, D_MODEL), mrow),
                  pl.BlockSpec((None, 1, D_MODEL), mrow),
                  pl.BlockSpec((None, 1, D_MODEL), mrow),
                  pl.BlockSpec((1, D_MODEL), const),
                  pl.BlockSpec((N_EXPERTS, D_MODEL), const),
                  pl.BlockSpec((N_EXPERTS, D_MODEL), const),
                  pl.BlockSpec((N_EXPERTS, LANES), const),
                  pl.BlockSpec((tm, tm), const)],
        out_specs=(pl.BlockSpec((tm * ROW_TILES, LANES), tok),
                   pl.BlockSpec((tm * ROW_TILES, LANES), tok),
                   pl.BlockSpec((16, tm), lambda b, j: (0, b * nt + j)),
                   pl.BlockSpec((tm * TOP_K, LANES), tok),
                   pl.BlockSpec((N_EXPERTS, LANES), tok),
                   pl.BlockSpec((N_EXPERTS, LANES), const)),
        scratch_shapes=[pltpu.VMEM((N_EXPERTS, LANES), F32)],
        compiler_params=_cparams(("arbitrary", "arbitrary")),
        name="outproj_router",
    )(a_ret, a_hg, w_bf, x, g1, sc2, sh2, nw, rwh, rwl, rb, tri)


def _dispatch_kernel(dest_hbm, v_ref, xs_hbm, idx_smem, zbuf, sem_idx, sem_z, sem_rows,
                     *, tile, n_tiles, n_rows):
    i = pl.program_id(0)
    slot = lax.rem(i, 2)

    def idx_copy(j, s):
        n_asg = tile * TOP_K
        return pltpu.make_async_copy(dest_hbm.at[j], idx_smem.at[pl.ds(s * n_asg, n_asg)],
                                     sem_idx.at[s])

    def zero_copy(j):
        r0 = (n_rows + j * SUB_ROWS) * ROW_TILES
        return pltpu.make_async_copy(zbuf, xs_hbm.at[pl.ds(r0, SUB_ROWS * ROW_TILES), :], sem_z)

    @pl.when(i == 0)
    def _():
        idx_copy(0, 0).start()
        zbuf[...] = jnp.zeros_like(zbuf)
        for j in range(ROW_BLOCK // SUB_ROWS):
            zero_copy(j).start()
        for j in range(ROW_BLOCK // SUB_ROWS):
            zero_copy(j).wait()

    idx_copy(i, slot).wait()

    @pl.when(i + 1 < n_tiles)
    def _():
        idx_copy(i + 1, 1 - slot).start()

    def issue(t, carry):
        e0 = slot * (tile * TOP_K) + t
        for k in range(TOP_K):
            d = pl.multiple_of(idx_smem[e0 + k * tile], ROW_TILES)
            src = v_ref.at[pl.ds(pl.multiple_of(t * ROW_TILES, ROW_TILES), ROW_TILES), :]
            pltpu.make_async_copy(src, xs_hbm.at[pl.ds(d, ROW_TILES), :],
                                  sem_rows).start(priority=k % 2)
        return carry
    lax.fori_loop(0, tile, issue, 0, unroll=8)
    for _ in range(TOP_K):
        pltpu.make_async_copy(v_ref, xs_hbm.at[pl.ds(0, tile * ROW_TILES), :], sem_rows).wait()


def _dispatch(dest_tiles, v, *, tile):
    T = v.shape[0] // ROW_TILES
    n_rows = T * TOP_K
    return pl.pallas_call(
        functools.partial(_dispatch_kernel, tile=tile, n_tiles=T // tile, n_rows=n_rows),
        out_shape=jax.ShapeDtypeStruct(((n_rows + ROW_BLOCK) * ROW_TILES, LANES), F32),
        grid=(T // tile,),
        in_specs=[pl.BlockSpec(memory_space=pl.ANY),
                  pl.BlockSpec((tile * ROW_TILES, LANES), lambda i: (i, 0))],
        out_specs=pl.BlockSpec(memory_space=pl.ANY),
        scratch_shapes=[pltpu.SMEM((2 * tile * TOP_K,), I32),
                        pltpu.VMEM((SUB_ROWS * ROW_TILES, LANES), F32),
                        pltpu.SemaphoreType.DMA((2,)),
                        pltpu.SemaphoreType.DMA,
                        pltpu.SemaphoreType.DMA],
        compiler_params=_cparams(("arbitrary",)),
        name="dispatch",
    )(dest_tiles, v)


def _moe_kernel(be_ref, ns_ref, nx_ref, xr_ref, x_hbm, w1_hbm, b1_ref, w2_hbm, b2_ref, y_ref,
                w1_stage, w2_stage, w1_scr, w2_scr, xbuf, x_scr, act_scr, sem_w, sem_x,
                *, n_blocks):
    i = pl.program_id(0)
    slot = lax.rem(i, 2)
    e = be_ref[i]
    nsub = ns_ref[i]
    changed = jnp.logical_or(i == 0, e != be_ref[jnp.maximum(i - 1, 0)])

    def x_copy(j, s):
        r0 = pl.multiple_of(xr_ref[j], ROW_TILES)
        return pltpu.make_async_copy(x_hbm.at[pl.ds(r0, ROW_BLOCK * ROW_TILES), :], xbuf.at[s],
                                     sem_x.at[s])

    @pl.when(i == 0)
    def _():
        x_copy(0, 0).start()

    @pl.when(nsub > 0)
    def _():
        x_copy(i, slot).wait()

    nxt_blk = jnp.minimum(i + 1, n_blocks - 1)

    @pl.when(jnp.logical_and(i + 1 < n_blocks, ns_ref[nxt_blk] > 0))
    def _():
        x_copy(nxt_blk, 1 - slot).start()

    def weight_copies(ex):
        return (pltpu.make_async_copy(w1_hbm.at[ex], w1_stage, sem_w.at[0]),
                pltpu.make_async_copy(w2_hbm.at[ex], w2_stage, sem_w.at[1]))

    @pl.when(i == 0)
    def _():
        for cp in weight_copies(e):
            cp.start()

    @pl.when(jnp.logical_and(changed, nsub > 0))
    def _():
        for cp in weight_copies(e):
            cp.wait()
        w1_scr[...] = w1_stage[...].astype(BF16)
        w2_scr[...] = w2_stage[...].astype(BF16)
        nxt = nx_ref[e]

        @pl.when(nxt >= 0)
        def _():
            for cp in weight_copies(nxt):
                cp.start()

    def compute(rows):
        x_scr[0:rows, :] = _load_tile_rows(xbuf.at[slot], rows).astype(BF16)
        cw = 256
        for c in range(D_FF // cw):
            glu = (_dot(x_scr[0:rows, :], w1_scr[:, c * cw:(c + 1) * cw])
                   + b1_ref[:, c * cw:(c + 1) * cw])
            lin = (_dot(x_scr[0:rows, :], w1_scr[:, D_FF + c * cw:D_FF + (c + 1) * cw])
                   + b1_ref[:, D_FF + c * cw:D_FF + (c + 1) * cw])
            glu = jnp.minimum(glu, SWIGLU_LIMIT)
            lin = jnp.clip(lin, -SWIGLU_LIMIT, SWIGLU_LIMIT)
            act = glu * jax.nn.sigmoid(SWIGLU_ALPHA * glu) * (lin + 1.0)
            act_scr[0:rows, c * cw:(c + 1) * cw] = act.astype(BF16)
        for c in range(D_MODEL // cw):
            y = (_dot(act_scr[0:rows, :], w2_scr[:, c * cw:(c + 1) * cw])
                 + b2_ref[:, c * cw:(c + 1) * cw])
            for s in range(cw // LANES):
                t = c * (cw // LANES) + s
                y_ref[pl.ds(t, rows, stride=ROW_TILES), :] = y[:, s * LANES:(s + 1) * LANES]
        if rows < ROW_BLOCK:
            y_ref[rows * ROW_TILES:, :] = jnp.zeros(((ROW_BLOCK - rows) * ROW_TILES, LANES), F32)

    for m in range(1, ROW_BLOCK // SUB_ROWS + 1):
        @pl.when(nsub == m)
        def _(m=m):
            compute(m * SUB_ROWS)

    @pl.when(nsub == 0)
    def _():
        y_ref[...] = jnp.zeros_like(y_ref)


def _moe(block_e, nsub, next_e, xrow, xs, w1, b1, w2, b2, *, n_blocks):
    return pl.pallas_call(
        functools.partial(_moe_kernel, n_blocks=n_blocks),
        out_shape=jax.ShapeDtypeStruct((n_blocks * ROW_BLOCK * ROW_TILES, LANES), F32),
        grid_spec=pltpu.PrefetchScalarGridSpec(
            num_scalar_prefetch=4,
            grid=(n_blocks,),
            in_specs=[pl.BlockSpec(memory_space=pl.ANY),
                      pl.BlockSpec(memory_space=pl.ANY),
                      pl.BlockSpec((None, 1, 2 * D_FF), lambda i, be, ns, nx, xr: (be[i], 0, 0)),
                      pl.BlockSpec(memory_space=pl.ANY),
                      pl.BlockSpec((None, 1, D_MODEL), lambda i, be, ns, nx, xr: (be[i], 0, 0))],
            out_specs=pl.BlockSpec((ROW_BLOCK * ROW_TILES, LANES),
                                   lambda i, be, ns, nx, xr: (i, 0)),
            scratch_shapes=[pltpu.VMEM((D_MODEL, 2 * D_FF), F32),
                            pltpu.VMEM((D_FF, D_MODEL), F32),
                            pltpu.VMEM((D_MODEL, 2 * D_FF), BF16),
                            pltpu.VMEM((D_FF, D_MODEL), BF16),
                            pltpu.VMEM((2, ROW_BLOCK * ROW_TILES, LANES), F32),
                            pltpu.VMEM((ROW_BLOCK, D_MODEL), BF16),
                            pltpu.VMEM((ROW_BLOCK, D_FF), BF16),
                            pltpu.SemaphoreType.DMA((2,)),
                            pltpu.SemaphoreType.DMA((2,))]),
        compiler_params=_cparams(("arbitrary",)),
        name="moe_ffn",
    )(block_e, nsub, next_e, xrow, xs, w1, b1, w2, b2)


def _combine_kernel(pos_hbm, src_hbm, yb_hbm, gate_ref, h1_ref, g2_ref, nw_ref, o_ref,
                    pos_smem, src_smem, win, hbuf, sem_tab, sem_win, *, n_tiles):
    i = pl.program_id(0)
    slot = lax.rem(i, 2)
    tile = COMBINE_TILE
    n_asg = tile * TOP_K
    chunk = WIN_ROWS * ROW_TILES
    slot_rows = WIN_CHUNKS * chunk

    def table_copies(j, s):
        return (pltpu.make_async_copy(pos_hbm.at[j], pos_smem.at[pl.ds(s * n_asg, n_asg)],
                                      sem_tab.at[s]),
                pltpu.make_async_copy(src_hbm.at[j], src_smem.at[pl.ds(s * n_asg, n_asg)],
                                      sem_tab.at[s]))

    def fetch_window(s):
        def one(c, carry):
            src = pl.multiple_of(src_smem[s * n_asg + c], ROW_TILES)
            dst = pl.multiple_of(s * slot_rows + c * chunk, chunk)
            pltpu.make_async_copy(yb_hbm.at[pl.ds(src, chunk), :], win.at[pl.ds(dst, chunk), :],
                                  sem_win.at[s]).start()
            return carry
        lax.fori_loop(0, WIN_CHUNKS, one, 0, unroll=4)

    @pl.when(i == 0)
    def _():
        for cp in table_copies(0, 0):
            cp.start()
        for cp in table_copies(0, 0):
            cp.wait()
        fetch_window(0)
        if n_tiles > 1:
            for cp in table_copies(1, 1):
                cp.start()

    @pl.when(i + 1 < n_tiles)
    def _():
        for cp in table_copies(i + 1, 1 - slot):
            cp.wait()
        fetch_window(1 - slot)

    pltpu.make_async_copy(yb_hbm.at[pl.ds(0, slot_rows), :],
                          win.at[pl.ds(pl.multiple_of(slot * slot_rows, slot_rows), slot_rows), :],
                          sem_win.at[slot]).wait()

    g2 = g2_ref[...]
    base = slot * n_asg

    def token(t, carry):
        e0 = base + t * TOP_K
        acc = None
        for k in range(TOP_K):
            p = pl.multiple_of(pos_smem[e0 + k], ROW_TILES)
            term = gate_ref[pl.ds(t * TOP_K + k, 1), :] * win[pl.ds(p, ROW_TILES), :]
            acc = term if acc is None else acc + term
        r0 = pl.multiple_of(t * ROW_TILES, ROW_TILES)
        hbuf[pl.ds(r0, ROW_TILES), :] = h1_ref[pl.ds(r0, ROW_TILES), :] + g2 * acc
        return carry
    lax.fori_loop(0, tile, token, 0, unroll=8)

    @pl.when(i + 2 < n_tiles)
    def _():
        for cp in table_copies(i + 2, slot):
            cp.start()

    h = _load_tile_rows(hbuf, tile)
    ms = jnp.mean(h * h, axis=-1, keepdims=True)
    o_ref[...] = h * lax.rsqrt(ms + EPS) * nw_ref[...]


def _combine(pos_tiles, src_tiles, yb, gate_rows, h1, g2_tiles, nw, *, tiles_per_batch):
    tile = COMBINE_TILE
    T = h1.shape[0] // ROW_TILES
    n_asg = tile * TOP_K
    return pl.pallas_call(
        functools.partial(_combine_kernel, n_tiles=T // tile),
        out_shape=jax.ShapeDtypeStruct((T, D_MODEL), F32),
        grid=(T // tile,),
        in_specs=[pl.BlockSpec(memory_space=pl.ANY),
                  pl.BlockSpec(memory_space=pl.ANY),
                  pl.BlockSpec(memory_space=pl.ANY),
                  pl.BlockSpec((tile * TOP_K, LANES), lambda i: (i, 0)),
                  pl.BlockSpec((tile * ROW_TILES, LANES), lambda i: (i, 0)),
                  pl.BlockSpec((None, ROW_TILES, LANES), lambda i: (i // tiles_per_batch, 0, 0)),
                  pl.BlockSpec((1, D_MODEL), lambda i: (0, 0))],
        out_specs=pl.BlockSpec((tile, D_MODEL), lambda i: (i, 0)),
        scratch_shapes=[pltpu.SMEM((2 * n_asg,), I32),
                        pltpu.SMEM((2 * n_asg,), I32),
                        pltpu.VMEM((2 * WIN_CHUNKS * WIN_ROWS * ROW_TILES, LANES), F32),
                        pltpu.VMEM((tile * ROW_TILES, LANES), F32),
                        pltpu.SemaphoreType.DMA((2,)),
                        pltpu.SemaphoreType.DMA((2,))],
        compiler_params=_cparams(("arbitrary",)),
        name="combine_norm",
    )(pos_tiles, src_tiles, yb, gate_rows, h1, g2_tiles, nw)


def _rope_tables(n):
    rows = n // GRID_W
    row = jnp.repeat(jnp.arange(rows, dtype=F32), GRID_W)
    col = jnp.tile(jnp.arange(GRID_W, dtype=F32), rows)
    n_freq = RET_DK // 4
    inv = ROPE_BASE ** (-jnp.arange(n_freq, dtype=F32) / n_freq)
    ang = jnp.concatenate([row[:, None] * inv, col[:, None] * inv], axis=-1)
    cos, sin = jnp.cos(ang), jnp.sin(ang)
    cos_h = jnp.concatenate([cos, cos], axis=-1)
    sin_h = jnp.concatenate([-sin, sin], axis=-1)
    return jnp.tile(cos_h, (1, RET_HEADS)), jnp.tile(sin_h, (1, RET_HEADS))


def kernel(x, c, ctx, c_ctx, w_ada, b_ada, norm_mix_w, norm_ffn_w, w_in, w_out, hg_lb,
           hg_norm_w, router_w, router_b, w1, b1, w2, b2, norm_final_w):
    B, N, D = x.shape
    C = ctx.shape[1]
    T = B * N
    assert D == D_MODEL and w_ada.shape[0] == 1

    cc = jnp.concatenate([c.astype(F32), c_ctx.astype(F32)[None, :],
                          jnp.zeros((16 - B - 1, D), F32)], axis=0)
    mod = _ada(cc, w_ada[0], b_ada[0][None, :])
    mod = mod.reshape(16, 6, 1, D).transpose(1, 0, 2, 3)
    sh1, sc1, g1, sh2, sc2, g2 = (mod[i] for i in range(6))

    w_in_bf = w_in[0].astype(BF16)
    w_out_bf = w_out[0].astype(BF16)
    nw_mix = norm_mix_w[0][None, :]
    cos_t, sin_t = _rope_tables(N)

    pb, lf = _inproj(x, sc1, sh1, None, nw_mix, w_in_bf, hg_lb[:2], cos_t, sin_t,
                     rope=True, tm=INPROJ_TILE)
    pbc, lfc = _inproj(ctx, sc1, sh1, B, nw_mix, w_in_bf, hg_lb[:2], cos_t[:C], sin_t[:C],
                       rope=False, tm=C)

    a_ret = _retention(pb, pbc)
    a_hg = _hgrn(pb, lf, pbc, lfc, hg_norm_w[0][None, :])

    rw_t = router_w[0].T
    rwh = rw_t.astype(BF16)
    rwl = (rw_t - rwh.astype(F32)).astype(BF16)
    rb = jnp.broadcast_to(router_b[0][:, None], (N_EXPERTS, LANES))
    h1, v, route_t, gate_rows, base_cnt, cnt = _outproj(a_ret, a_hg, w_out_bf, x, g1, sc2, sh2,
                                                        norm_ffn_w[0][None, :], rwh, rwl, rb,
                                                        tm=COMBINE_TILE)

    idx = route_t[0:TOP_K].astype(I32)
    rank = route_t[2 * TOP_K:3 * TOP_K].astype(I32)
    counts = cnt[:, 0].astype(I32)
    padded = (counts + ROW_BLOCK - 1) // ROW_BLOCK * ROW_BLOCK
    pend = jnp.cumsum(padded)
    pstart = pend - padded
    cstart = jnp.cumsum(counts) - counts
    onehot = idx[None] == jnp.arange(N_EXPERTS, dtype=I32)[:, None, None]

    def dest_tiles(first_row, tile):
        dest = (jnp.sum(jnp.where(onehot, first_row[:, None, None], 0), axis=0) + rank) * ROW_TILES
        return dest.reshape(TOP_K, T // tile, tile).transpose(1, 0, 2).reshape(T // tile,
                                                                               TOP_K * tile)
    n_blocks = (T * TOP_K) // ROW_BLOCK + N_EXPERTS
    starts = jnp.arange(n_blocks, dtype=I32) * ROW_BLOCK
    block_e = jnp.minimum(jnp.sum((pend[None, :] <= starts[:, None]).astype(I32), axis=1),
                          N_EXPERTS - 1)
    valid = jnp.clip(counts[block_e] - (starts - pstart[block_e]), 0, ROW_BLOCK)
    nsub = (valid + SUB_ROWS - 1) // SUB_ROWS
    first_from = lax.cummin(jnp.where(counts > 0, jnp.arange(N_EXPERTS, dtype=I32), N_EXPERTS),
                            reverse=True)
    next_e = jnp.concatenate([first_from[1:], jnp.full((1,), N_EXPERTS, I32)])
    next_e = jnp.where(next_e == N_EXPERTS, -1, next_e)
    xrow = jnp.where(nsub > 0, cstart[block_e] + starts - pstart[block_e], 0) * ROW_TILES

    xs = _dispatch(dest_tiles(cstart, DISPATCH_TILE), v, tile=DISPATCH_TILE)
    yb = _moe(block_e, nsub, next_e, xrow.astype(I32), xs, w1[0], b1[0][:, None, :], w2[0],
              b2[0][:, None, :], n_blocks=n_blocks)

    tc = COMBINE_TILE
    n_ct = T // tc
    base = base_cnt.reshape(n_ct, N_EXPERTS, LANES)[:, :, 0].astype(I32)
    n_run = jnp.concatenate([base[1:], counts[None, :]], axis=0) - base
    n_chunk = (n_run + WIN_ROWS - 1) // WIN_ROWS
    chunk_end = jnp.cumsum(n_chunk, axis=1)
    chunk_start = chunk_end - n_chunk
    win_off = chunk_start * WIN_ROWS
    c_id = jnp.arange(WIN_CHUNKS, dtype=I32)
    c_exp = jnp.sum((chunk_end[:, None, :] <= c_id[None, :, None]).astype(I32), axis=2)
    c_real = c_exp < N_EXPERTS
    c_own = c_exp[:, :, None] == jnp.arange(N_EXPERTS, dtype=I32)
    take = lambda tab: jnp.sum(jnp.where(c_own, tab[:, None, :], 0), axis=2)
    c_src = (take(pstart[None, :] + base) + (c_id[None, :] - take(chunk_start)) * WIN_ROWS)
    src_tiles = jnp.concatenate(
        [jnp.where(c_real, c_src, 0) * ROW_TILES,
         jnp.zeros((n_ct, tc * TOP_K - WIN_CHUNKS), I32)], axis=1)
    slot_off = (jnp.arange(n_ct, dtype=I32) % 2) * (WIN_CHUNKS * WIN_ROWS)
    shift = jnp.repeat(win_off - base + slot_off[:, None], tc, axis=0).T
    pos = (jnp.sum(jnp.where(onehot, shift[:, None, :], 0), axis=0) + rank) * ROW_TILES

    def per_tile(a):
        return a.reshape(TOP_K, n_ct, tc).transpose(1, 2, 0).reshape(n_ct, tc * TOP_K)

    out = _combine(per_tile(pos), src_tiles, yb, gate_rows, h1,
                   g2.reshape(16, ROW_TILES, LANES), norm_final_w[None, :],
                   tiles_per_batch=N // tc)
    return out.reshape(B, N, D)
```
